```python
import jax, jax.numpy as jnp
from jax import lax
import numpy as np

D_MODEL = 1024
BATCH = 8
SEQ = 16384
DEPTH = 1

GRID_W = 64
NA_W = D_MODEL // 2
NA_HEAD_DIM = 64
NA_HEADS = NA_W // NA_HEAD_DIM
NA_KH = 8
NA_KW = 16
ML_W = D_MODEL // 2
ML_HEADS = 4
ML_HEAD_DIM = ML_W // ML_HEADS
MIX_W = NA_W + ML_W
ML_CHUNK = 64
CONV_W = 5
EPS = 1e-6
IN_SPLITS = (NA_W,) * 4 + (ML_W,) * 5 + (ML_HEADS,) * 4
IN_W = 4 * NA_W + 5 * ML_W + 4 * ML_HEADS

kernel_name = 'hybrid_natten2d_bimlstm_block'


def rmsnorm(x, w):
    x32 = x.astype(jnp.float32)
    return x32 * lax.rsqrt(jnp.mean(x32 * x32, axis=-1, keepdims=True) + EPS) * w


def centred_short_conv(u, w, b):
    T = u.shape[1]
    pad = CONV_W // 2
    up = jnp.pad(u, ((0, 0), (pad, pad), (0, 0)))
    out = up[:, 0:T] * w[0]
    for j in range(1, CONV_W):
        out = out + up[:, j:j + T] * w[j]
    return out + b


def neighbourhood_attention(q, k, v, rpb):
    B, T, _ = q.shape
    rows = T // GRID_W
    kh = min(NA_KH, rows)

    def to_grid(t):
        return t.reshape(B, rows, GRID_W, NA_HEADS, NA_HEAD_DIM).transpose(0, 3, 1, 2, 4)

    qg = to_grid(q) * (NA_HEAD_DIM ** -0.5)
    kgrid, vgrid = to_grid(k), to_grid(v)
    r = jnp.arange(rows)
    row_idx = jnp.clip(r - kh // 2, 0, rows - kh)[:, None] + jnp.arange(kh)[None, :]
    kg = kgrid[:, :, row_idx]
    vg = vgrid[:, :, row_idx]
    col = jnp.arange(GRID_W)
    col_start = jnp.clip(col - NA_KW // 2, 0, GRID_W - NA_KW)
    col_mask = (col[None, :] >= col_start[:, None]) & (col[None, :] < col_start[:, None] + NA_KW)
    dy = row_idx - r[:, None] + NA_KH - 1
    dx = jnp.clip(col[None, :] - col[:, None], -(NA_KW - 1), NA_KW - 1) + NA_KW - 1
    bias = rpb[:, dy[:, None, :, None], dx[None, :, None, :]]
    s = jnp.einsum('bhrcd,bhrjkd->bhrcjk', qg, kg).astype(jnp.float32) + bias[None]
    s = jnp.where(col_mask[:, None, :], s, -jnp.inf)
    p = jax.nn.softmax(s.reshape(B, NA_HEADS, rows, GRID_W, kh * GRID_W), axis=-1).reshape(s.shape)
    o = jnp.einsum('bhrcjk,bhrjkd->bhrcd', p, vg)
    return o.transpose(0, 2, 3, 1, 4).reshape(B, T, NA_W)


def mlstm_chunkwise(q, k, v, i_pre, f_pre):
    B, H, T, d = q.shape
    L = ML_CHUNK
    nc = T // L
    q = q.reshape(B, H, nc, L, d)
    k = (k * (d ** -0.5)).reshape(B, H, nc, L, d)
    v = v.reshape(B, H, nc, L, d)
    i_pre = i_pre.reshape(B, H, nc, L)
    b = jnp.cumsum(jax.nn.log_sigmoid(f_pre).reshape(B, H, nc, L), axis=-1)
    b_last = b[..., -1]
    a = b_last[..., None] - b + i_pre

    def step(carry, inp):
        C, n, m = carry
        k_c, v_c, a_c, bl_c = inp
        m_new = jnp.maximum(bl_c + m, jnp.max(a_c, axis=-1))
        decay = jnp.exp(bl_c + m - m_new)
        w = jnp.exp(a_c - m_new[..., None])
        C_new = decay[..., None, None] * C + jnp.einsum('bhs,bhse,bhsd->bhed', w, v_c, k_c)
        n_new = decay[..., None] * n + jnp.einsum('bhs,bhsd->bhd', w, k_c)
        return (C_new, n_new, m_new), (C, n, m)

    init = (jnp.zeros((B, H, d, d), q.dtype), jnp.zeros((B, H, d), q.dtype), jnp.zeros((B, H), q.dtype))
    xs = (jnp.moveaxis(k, 2, 0), jnp.moveaxis(v, 2, 0), jnp.moveaxis(a, 2, 0), jnp.moveaxis(b_last, 2, 0))
    _, (C_prev, n_prev, m_prev) = lax.scan(step, init, xs)
    C_prev = jnp.moveaxis(C_prev, 0, 2)
    n_prev = jnp.moveaxis(n_prev, 0, 2)
    m_prev = jnp.moveaxis(m_prev, 0, 2)

    lower = jnp.tril(jnp.ones((L, L), dtype=bool))
    Dlog = jnp.where(lower, b[..., :, None] - b[..., None, :] + i_pre[..., None, :], -jnp.inf)
    m_inter = b + m_prev[..., None]
    m_t = jnp.maximum(m_inter, jnp.max(Dlog, axis=-1))
    S = jnp.einsum('bhntd,bhnsd->bhnts', q, k) * jnp.exp(Dlog - m_t[..., None])
    inter = jnp.exp(m_inter - m_t)
    num = jnp.einsum('bhnts,bhnse->bhnte', S, v) + inter[..., None] * jnp.einsum('bhned,bhntd->bhnte', C_prev, q)
    den = jnp.sum(S, axis=-1) + inter * jnp.einsum('bhnd,bhntd->bhnt', n_prev, q)
    h = num / jnp.maximum(jnp.abs(den), jnp.exp(-m_t))[..., None]
    return h.reshape(B, H, T, d)


def hybrid_mixer(h, w_in, b_in, conv_w, conv_b, rpb, ml_norm_w, w_out):
    B, T, _ = h.shape
    proj = h @ w_in + b_in
    split_idx = np.cumsum(IN_SPLITS)[:-1].tolist()
    (na_q, na_k, na_v, na_z, ml_q, ml_k, ml_v, ml_o, ml_z,
     i_f, f_f, i_b, f_b) = jnp.split(proj, split_idx, axis=-1)

    na_out = neighbourhood_attention(na_q, na_k, na_v, rpb) * jax.nn.silu(na_z)

    qk = jax.nn.silu(centred_short_conv(jnp.concatenate([ml_q, ml_k], axis=-1), conv_w, conv_b))
    mq, mk = jnp.split(qk, 2, axis=-1)

    def heads(t):
        return t.reshape(B, T, ML_HEADS, ML_HEAD_DIM).transpose(0, 2, 1, 3)

    mq, mk, mv = heads(mq), heads(mk), heads(ml_v)
    i_f, f_f, i_b, f_b = (g.transpose(0, 2, 1) for g in (i_f, f_f, i_b, f_b))
    h_fwd = mlstm_chunkwise(mq, mk, mv, i_f, f_f)
    flip = lambda t: jnp.flip(t, axis=2)
    h_bwd = flip(mlstm_chunkwise(flip(mq), flip(mk), flip(mv), flip(i_b), flip(f_b)))
    hm = (h_fwd + h_bwd).transpose(0, 2, 1, 3) * jax.nn.sigmoid(ml_o).reshape(B, T, ML_HEADS, ML_HEAD_DIM)
    mu = jnp.mean(hm, axis=-1, keepdims=True)
    var = jnp.mean(jnp.square(hm - mu), axis=-1, keepdims=True)
    hm = ((hm - mu) * lax.rsqrt(var + EPS)).reshape(B, T, ML_W) * ml_norm_w
    ml_out = hm * jax.nn.silu(ml_z)

    return jnp.concatenate([na_out, ml_out], axis=-1) @ w_out


def _fwd_setup_inputs(seed: int = 0) -> dict:
    key = jax.random.key(seed)
    ks = jax.random.split(key, 14)
    D = D_MODEL
    x = jax.random.normal(ks[0], (BATCH, SEQ, D), jnp.float32)
    c = jax.random.normal(ks[1], (BATCH, D), jnp.float32)
    w_ada = jax.random.normal(ks[2], (DEPTH, D, 3 * D), jnp.float32) * (0.5 * D ** -0.5)
    b_ada = jax.random.normal(ks[3], (DEPTH, 3 * D), jnp.float32) * 0.02
    norm_w = 1.0 + 0.1 * jax.random.normal(ks[4], (DEPTH, D), jnp.float32)
    w_in = jax.random.normal(ks[5], (DEPTH, D, IN_W), jnp.float32) * (D ** -0.5)
    n_main = 4 * NA_W + 5 * ML_W
    fgate_base = jnp.linspace(3.0, 6.0, ML_HEADS, dtype=jnp.float32)
    gk = jax.random.split(ks[6], 5)
    b_in = jnp.concatenate([
        0.02 * jax.random.normal(gk[0], (DEPTH, n_main), jnp.float32),
        0.1 * jax.random.normal(gk[1], (DEPTH, ML_HEADS), jnp.float32),
        fgate_base + 0.1 * jax.random.normal(gk[2], (DEPTH, ML_HEADS), jnp.float32),
        0.1 * jax.random.normal(gk[3], (DEPTH, ML_HEADS), jnp.float32),
        fgate_base + 0.1 * jax.random.normal(gk[4], (DEPTH, ML_HEADS), jnp.float32),
    ], axis=-1)
    conv_w = jax.random.normal(ks[7], (DEPTH, CONV_W, 2 * ML_W), jnp.float32) * (CONV_W ** -0.5)
    conv_b = 0.02 * jax.random.normal(ks[8], (DEPTH, 2 * ML_W), jnp.float32)
    rpb = 0.1 * jax.random.normal(ks[9], (DEPTH, NA_HEADS, 2 * NA_KH - 1, 2 * NA_KW - 1), jnp.float32)
    ml_norm_w = 1.0 + 0.1 * jax.random.normal(ks[10], (DEPTH, ML_W), jnp.float32)
    w_out = jax.random.normal(ks[11], (DEPTH, MIX_W, D), jnp.float32) * (MIX_W ** -0.5)
    final_norm_w = 1.0 + 0.1 * jax.random.normal(ks[12], (D,), jnp.float32)
    return {'x': x, 'c': c, 'w_ada': w_ada, 'b_ada': b_ada, 'norm_w': norm_w, 'w_in': w_in,
            'b_in': b_in, 'conv_w': conv_w, 'conv_b': conv_b, 'rpb': rpb, 'ml_norm_w': ml_norm_w,
            'w_out': w_out, 'final_norm_w': final_norm_w}


def _fwd_reference(x, c, w_ada, b_ada, norm_w, w_in, b_in, conv_w, conv_b, rpb, ml_norm_w, w_out, final_norm_w):
    h_res = x.astype(jnp.float32)
    c_act = jax.nn.silu(c.astype(jnp.float32))
    for l in range(DEPTH):
        mod = c_act @ w_ada[l] + b_ada[l]
        shift, scale, gate = jnp.split(mod, 3, axis=-1)
        h = rmsnorm(h_res, norm_w[l]) * (1.0 + scale[:, None, :]) + shift[:, None, :]
        y = hybrid_mixer(h, w_in[l], b_in[l], conv_w[l], conv_b[l], rpb[l], ml_norm_w[l], w_out[l])
        h_res = h_res + gate[:, None, :] * y
    return rmsnorm(h_res, final_norm_w).astype(x.dtype)


import jax as _jax
import jax.numpy as _jnp

TWIN_FORMAT = 'train_step'
FWD_PARAMS = ['x', 'c', 'w_ada', 'b_ada', 'norm_w', 'w_in', 'b_in', 'conv_w', 'conv_b', 'rpb', 'ml_norm_w', 'w_out', 'final_norm_w']
TWIN_WEIGHTS = ['w_ada', 'b_ada', 'norm_w', 'w_in', 'b_in', 'conv_w', 'conv_b', 'rpb', 'ml_norm_w', 'w_out', 'final_norm_w']
TWIN_DIFF_INPUT = 'x'
TWIN_INPUTS = ['x', 'c', 'w_ada', 'b_ada', 'norm_w', 'w_in', 'b_in', 'conv_w', 'conv_b', 'rpb', 'ml_norm_w', 'w_out', 'final_norm_w', 'loss_target', 'm_w_ada', 'm_b_ada', 'm_norm_w', 'm_w_in', 'm_b_in', 'm_conv_w', 'm_conv_b', 'm_rpb', 'm_ml_norm_w', 'm_w_out', 'm_final_norm_w', 'v_w_ada', 'v_b_ada', 'v_norm_w', 'v_w_in', 'v_b_in', 'v_conv_w', 'v_conv_b', 'v_rpb', 'v_ml_norm_w', 'v_w_out', 'v_final_norm_w']
TWIN_OUTPUTS = ['loss', 'grad_x', 'grad_w_ada', 'grad_b_ada', 'grad_norm_w', 'grad_w_in', 'grad_b_in', 'grad_conv_w', 'grad_conv_b', 'grad_rpb', 'grad_ml_norm_w', 'grad_w_out', 'grad_final_norm_w', 'delta_w_ada', 'delta_b_ada', 'delta_norm_w', 'delta_w_in', 'delta_b_in', 'delta_conv_w', 'delta_conv_b', 'delta_rpb', 'delta_ml_norm_w', 'delta_w_out', 'delta_final_norm_w', 'new_m_w_ada', 'new_m_b_ada', 'new_m_norm_w', 'new_m_w_in', 'new_m_b_in', 'new_m_conv_w', 'new_m_conv_b', 'new_m_rpb', 'new_m_ml_norm_w', 'new_m_w_out', 'new_m_final_norm_w', 'new_v_w_ada', 'new_v_b_ada', 'new_v_norm_w', 'new_v_w_in', 'new_v_b_in', 'new_v_conv_w', 'new_v_conv_b', 'new_v_rpb', 'new_v_ml_norm_w', 'new_v_w_out', 'new_v_final_norm_w']
TWIN_LEAF_KINDS = {'loss': 'loss', 'grad_x': 'grad_x', 'grad_w_ada': 'grad_w', 'grad_b_ada': 'grad_w', 'grad_norm_w': 'grad_w', 'grad_w_in': 'grad_w', 'grad_b_in': 'grad_w', 'grad_conv_w': 'grad_w', 'grad_conv_b': 'grad_w', 'grad_rpb': 'grad_w', 'grad_ml_norm_w': 'grad_w', 'grad_w_out': 'grad_w', 'grad_final_norm_w': 'grad_w', 'delta_w_ada': 'delta_w', 'delta_b_ada': 'delta_w', 'delta_norm_w': 'delta_w', 'delta_w_in': 'delta_w', 'delta_b_in': 'delta_w', 'delta_conv_w': 'delta_w', 'delta_conv_b': 'delta_w', 'delta_rpb': 'delta_w', 'delta_ml_norm_w': 'delta_w', 'delta_w_out': 'delta_w', 'delta_final_norm_w': 'delta_w', 'new_m_w_ada': 'new_m', 'new_m_b_ada': 'new_m', 'new_m_norm_w': 'new_m', 'new_m_w_in': 'new_m', 'new_m_b_in': 'new_m', 'new_m_conv_w': 'new_m', 'new_m_conv_b': 'new_m', 'new_m_rpb': 'new_m', 'new_m_ml_norm_w': 'new_m', 'new_m_w_out': 'new_m', 'new_m_final_norm_w': 'new_m', 'new_v_w_ada': 'new_v', 'new_v_b_ada': 'new_v', 'new_v_norm_w': 'new_v', 'new_v_w_in': 'new_v', 'new_v_b_in': 'new_v', 'new_v_conv_w': 'new_v', 'new_v_conv_b': 'new_v', 'new_v_rpb': 'new_v', 'new_v_ml_norm_w': 'new_v', 'new_v_w_out': 'new_v', 'new_v_final_norm_w': 'new_v'}


def _forward(args):
    return _fwd_reference(*[args[k] for k in FWD_PARAMS])


def _output_shape():
    def fwd():
        inp = _fwd_setup_inputs(0)
        return _fwd_reference(*[inp[k] for k in FWD_PARAMS])
    out = _jax.eval_shape(fwd)
    return out.shape, out.dtype

N_MICROBATCH = 1
ADAM_LR = 0.001
ADAM_B1 = 0.9
ADAM_B2 = 0.999
ADAM_EPS = 1e-08
ADAM_WD = 0.01
ADAM_STEP = 10
PER_EXAMPLE_BATCH_AXIS = {'x': 0, 'c': 0, 'loss_target': 0}
SHARED_INPUTS = []
_WEIGHT_DTYPES = {'w_ada': _jnp.float32, 'b_ada': _jnp.float32, 'norm_w': _jnp.float32, 'w_in': _jnp.float32, 'b_in': _jnp.float32, 'conv_w': _jnp.float32, 'conv_b': _jnp.float32, 'rpb': _jnp.float32, 'ml_norm_w': _jnp.float32, 'w_out': _jnp.float32, 'final_norm_w': _jnp.float32}
MOMENT_SCALE = {'w_ada': 1.486208e-01, 'b_ada': 3.027360e-01, 'norm_w': 7.831151e-02, 'w_in': 4.373613e-02, 'b_in': 1.017431e-01, 'conv_w': 2.854766e-02, 'conv_b': 2.518579e-02, 'rpb': 5.371041e-03, 'ml_norm_w': 1.105570e-01, 'w_out': 5.768280e-02, 'final_norm_w': 1.292434e+02}


def _to_microbatches(a, axis):
    t = _jnp.moveaxis(a, axis, 0)
    t = t.reshape((N_MICROBATCH, t.shape[0] // N_MICROBATCH) + t.shape[1:])
    return _jnp.moveaxis(t, 1, axis + 1)


def setup_inputs(seed: int = 0) -> dict:
    inp = _fwd_setup_inputs(seed)
    key = _jax.random.fold_in(_jax.random.key(seed), 7919)
    shape, _ = _output_shape()
    out = dict(inp)
    out["loss_target"] = _jax.random.normal(_jax.random.fold_in(key, 0), shape, _jnp.float32)
    for i, name in enumerate(TWIN_WEIGHTS):
        w = inp[name].astype(_jnp.float32)
        if MOMENT_SCALE is None:
            s = _jnp.sqrt(_jnp.mean(_jnp.square(w)) + 1e-30)
        else:
            s = MOMENT_SCALE[name]
        km, kv = _jax.random.split(_jax.random.fold_in(key, i + 1))
        out[name] = w
        out["m_" + name] = s * _jax.random.normal(km, w.shape, _jnp.float32)
        out["v_" + name] = (s * s) * _jax.random.uniform(kv, w.shape, _jnp.float32, 0.5, 1.5)
    if N_MICROBATCH > 1:
        for name, axis in PER_EXAMPLE_BATCH_AXIS.items():
            out[name] = _to_microbatches(out[name], axis)
    return {'x': out['x'], 'c': out['c'], 'w_ada': out['w_ada'], 'b_ada': out['b_ada'], 'norm_w': out['norm_w'], 'w_in': out['w_in'], 'b_in': out['b_in'], 'conv_w': out['conv_w'], 'conv_b': out['conv_b'], 'rpb': out['rpb'], 'ml_norm_w': out['ml_norm_w'], 'w_out': out['w_out'], 'final_norm_w': out['final_norm_w'], 'loss_target': out['loss_target'], 'm_w_ada': out['m_w_ada'], 'm_b_ada': out['m_b_ada'], 'm_norm_w': out['m_norm_w'], 'm_w_in': out['m_w_in'], 'm_b_in': out['m_b_in'], 'm_conv_w': out['m_conv_w'], 'm_conv_b': out['m_conv_b'], 'm_rpb': out['m_rpb'], 'm_ml_norm_w': out['m_ml_norm_w'], 'm_w_out': out['m_w_out'], 'm_final_norm_w': out['m_final_norm_w'], 'v_w_ada': out['v_w_ada'], 'v_b_ada': out['v_b_ada'], 'v_norm_w': out['v_norm_w'], 'v_w_in': out['v_w_in'], 'v_b_in': out['v_b_in'], 'v_conv_w': out['v_conv_w'], 'v_conv_b': out['v_conv_b'], 'v_rpb': out['v_rpb'], 'v_ml_norm_w': out['v_ml_norm_w'], 'v_w_out': out['v_w_out'], 'v_final_norm_w': out['v_final_norm_w']}


def _loss(weights, diff, rest, loss_target):
    with _jax.named_scope("forward"):
        args = {**rest, TWIN_DIFF_INPUT: diff, **{k: w.astype(_WEIGHT_DTYPES[k]) for k, w in weights.items()}}
        y = _forward(args)
    with _jax.named_scope("loss_head"):
        err = _jnp.square(y.astype(_jnp.float32) - loss_target)
        return 0.5 * _jnp.sum(_jnp.mean(err, axis=-1)) if err.ndim else 0.5 * err


def _adamw(w, g, m, v):
    m = ADAM_B1 * m + (1.0 - ADAM_B1) * g
    v = ADAM_B2 * v + (1.0 - ADAM_B2) * _jnp.square(g)
    m_hat = m / (1.0 - ADAM_B1 ** ADAM_STEP)
    v_hat = v / (1.0 - ADAM_B2 ** ADAM_STEP)
    delta = -ADAM_LR * (m_hat / (_jnp.sqrt(v_hat) + ADAM_EPS) + ADAM_WD * w)
    return delta, m, v


def reference(x, c, w_ada, b_ada, norm_w, w_in, b_in, conv_w, conv_b, rpb, ml_norm_w, w_out, final_norm_w, loss_target, m_w_ada, m_b_ada, m_norm_w, m_w_in, m_b_in, m_conv_w, m_conv_b, m_rpb, m_ml_norm_w, m_w_out, m_final_norm_w, v_w_ada, v_b_ada, v_norm_w, v_w_in, v_b_in, v_conv_w, v_conv_b, v_rpb, v_ml_norm_w, v_w_out, v_final_norm_w):
    given = dict(x=x, c=c, w_ada=w_ada, b_ada=b_ada, norm_w=norm_w, w_in=w_in, b_in=b_in, conv_w=conv_w, conv_b=conv_b, rpb=rpb, ml_norm_w=ml_norm_w, w_out=w_out, final_norm_w=final_norm_w, loss_target=loss_target, m_w_ada=m_w_ada, m_b_ada=m_b_ada, m_norm_w=m_norm_w, m_w_in=m_w_in, m_b_in=m_b_in, m_conv_w=m_conv_w, m_conv_b=m_conv_b, m_rpb=m_rpb, m_ml_norm_w=m_ml_norm_w, m_w_out=m_w_out, m_final_norm_w=m_final_norm_w, v_w_ada=v_w_ada, v_b_ada=v_b_ada, v_norm_w=v_norm_w, v_w_in=v_w_in, v_b_in=v_b_in, v_conv_w=v_conv_w, v_conv_b=v_conv_b, v_rpb=v_rpb, v_ml_norm_w=v_ml_norm_w, v_w_out=v_w_out, v_final_norm_w=v_final_norm_w)
    weights = {n: given[n] for n in TWIN_WEIGHTS}
    shared = {n: given[n] for n in SHARED_INPUTS}
    per_example = {n: given[n] for n in ['x', 'c']}
    grad_fn = _jax.value_and_grad(_loss, argnums=(0, 1))

    def one_microbatch(ex, loss_target):
        ex = dict(ex)
        diff = ex.pop(TWIN_DIFF_INPUT)
        return grad_fn(weights, diff, {**shared, **ex}, loss_target)

    if N_MICROBATCH == 1:
        loss, (grad_w, grad_x) = one_microbatch(per_example, given["loss_target"])
    else:
        def body(carry, xs):
            loss_sum, grad_sum = carry
            l_k, (gw_k, gx_k) = one_microbatch(xs[0], xs[1])
            with _jax.named_scope("update"):
                return (loss_sum + l_k, _jax.tree.map(_jnp.add, grad_sum, gw_k)), gx_k

        init = (_jnp.zeros((), _jnp.float32), _jax.tree.map(_jnp.zeros_like, weights))
        (loss, grad_w), grad_x = _jax.lax.scan(body, init, (per_example, given["loss_target"]))
    with _jax.named_scope("update"):
        delta_w, new_m, new_v = {}, {}, {}
        for n in TWIN_WEIGHTS:
            delta_w[n], new_m[n], new_v[n] = _adamw(weights[n], grad_w[n], given["m_" + n], given["v_" + n])
    return (loss, grad_x, *[grad_w[n] for n in TWIN_WEIGHTS], *[delta_w[n] for n in TWIN_WEIGHTS],
            *[new_m[n] for n in TWIN_WEIGHTS], *[new_v[n] for n in TWIN_WEIGHTS])
```

```python
import functools

import numpy as np
import jax
import jax.numpy as jnp
from jax import lax
from jax.experimental import pallas as pl
from jax.experimental.pallas import tpu as pltpu

F32 = jnp.float32
BF16 = jnp.bfloat16
HI = lax.Precision.HIGHEST

D_MODEL = 1024
GRID_W = 64
NA_W = 512
NA_HEAD_DIM = 64
NA_HEADS = 8
NA_KH = 8
NA_KW = 16
ML_W = 512
ML_HEADS = 4
ML_HEAD_DIM = 128
ML_CHUNK = 64
CONV_W = 5
EPS = 1e-6
IN_W = 4 * NA_W + 5 * ML_W + 4 * ML_HEADS
IN_MAIN = 4 * NA_W + 5 * ML_W
IN_PAD = IN_MAIN + 128
NEG = -1e30

ADAM_LR = 0.001
ADAM_B1 = 0.9
ADAM_B2 = 0.999
ADAM_EPS = 1e-08
ADAM_WD = 0.01
ADAM_STEP = 10

NA_QROWS = 8
NA_KROWS = 16
NA_QT = NA_QROWS * GRID_W
NA_KT = NA_KROWS * GRID_W
NA_KCH = 256
ML_NB = 8
ML_TB = ML_NB * ML_CHUNK

VMEM_LIMIT = 56 * 1024 * 1024


def _cparams(sem, vmem=VMEM_LIMIT):
    return pltpu.CompilerParams(dimension_semantics=sem, vmem_limit_bytes=vmem)


def _silu(x):
    return x * jax.nn.sigmoid(x)


def _dsilu(x):
    s = jax.nn.sigmoid(x)
    return s * (1.0 + x * (1.0 - s))


def _dot(a, b, dims):
    return lax.dot_general(a, b, (dims, ((), ())), preferred_element_type=F32)


def _nn(a, b):
    return _dot(a, b, ((1,), (0,)))


def _nt(a, b):
    return _dot(a, b, ((1,), (1,)))


def _tn(a, b):
    return _dot(a, b, ((0,), (0,)))


def _mm(a, b, mode, tm, tn, tk, name, bias=None, out_dtype=F32):
    if mode == 'tn':
        kdim, m = a.shape
    else:
        m, kdim = a.shape
    n = b.shape[0] if mode == 'nt' else b.shape[1]
    assert m % tm == 0 and kdim % tk == 0
    nk = kdim // tk
    grid = (m // tm, pl.cdiv(n, tn), nk)
    a_spec = (pl.BlockSpec((tk, tm), lambda i, j, k: (k, i)) if mode == 'tn'
              else pl.BlockSpec((tm, tk), lambda i, j, k: (i, k)))
    b_spec = (pl.BlockSpec((tn, tk), lambda i, j, k: (j, k)) if mode == 'nt'
              else pl.BlockSpec((tk, tn), lambda i, j, k: (k, j)))
    in_specs = [a_spec, b_spec]
    args = [a, b]
    if bias is not None:
        in_specs.append(pl.BlockSpec((1, tn), lambda i, j, k: (0, j)))
        args.append(bias)
    dot = {'nn': _nn, 'nt': _nt, 'tn': _tn}[mode]

    def body(*refs):
        if bias is not None:
            a_ref, b_ref, bias_ref, o_ref, acc_ref = refs
        else:
            a_ref, b_ref, o_ref, acc_ref = refs
        k = pl.program_id(2)

        @pl.when(k == 0)
        def _():
            acc_ref[...] = jnp.zeros_like(acc_ref)

        acc_ref[...] += dot(a_ref[...].astype(BF16), b_ref[...].astype(BF16))

        @pl.when(k == nk - 1)
        def _():
            r = acc_ref[...]
            if bias is not None:
                r = r + bias_ref[...]
            o_ref[...] = r.astype(out_dtype)

    return pl.pallas_call(
        body, name=name, grid=grid, in_specs=in_specs,
        out_specs=pl.BlockSpec((tm, tn), lambda i, j, k: (i, j)),
        out_shape=jax.ShapeDtypeStruct((m, n), out_dtype),
        scratch_shapes=[pltpu.VMEM((tm, tn), F32)],
        compiler_params=_cparams(("parallel", "parallel", "arbitrary")),
    )(*args)


def _colsum(a, tm, name):
    t, n = a.shape

    def body(a_ref, o_ref):
        @pl.when(pl.program_id(0) == 0)
        def _():
            o_ref[...] = jnp.zeros_like(o_ref)
        o_ref[...] += jnp.sum(a_ref[...], axis=0, keepdims=True)

    return pl.pallas_call(
        body, name=name, grid=(t // tm,),
        in_specs=[pl.BlockSpec((tm, n), lambda i: (i, 0))],
        out_specs=pl.BlockSpec((1, n), lambda i: (0, 0)),
        out_shape=jax.ShapeDtypeStruct((1, n), F32),
        compiler_params=_cparams(("arbitrary",)),
    )(a)


def _row(n):
    return pl.BlockSpec((1, n), lambda i: (0, 0))


def _prologue(x, norm_w, scale, shift, tm):
    t, d = x.shape

    def body(x_ref, nw_ref, sc_ref, sh_ref, h_ref):
        xv = x_ref[...]
        r = lax.rsqrt(jnp.mean(xv * xv, axis=-1, keepdims=True) + EPS)
        h_ref[...] = (xv * r * nw_ref[...] * (1.0 + sc_ref[...]) + sh_ref[...]).astype(BF16)

    return pl.pallas_call(
        body, name="prologue", grid=(t // tm,),
        in_specs=[pl.BlockSpec((tm, d), lambda i: (i, 0)), _row(d), _row(d), _row(d)],
        out_specs=pl.BlockSpec((tm, d), lambda i: (i, 0)),
        out_shape=jax.ShapeDtypeStruct((t, d), BF16),
        compiler_params=_cparams(("parallel",)),
    )(x, norm_w, scale, shift)


def _ml_norm_parts(hs, o, z, nw):
    outs = []
    for hh in range(ML_HEADS):
        sl = slice(hh * ML_HEAD_DIM, (hh + 1) * ML_HEAD_DIM)
        hm = hs[:, sl] * jax.nn.sigmoid(o[:, sl])
        mu = jnp.mean(hm, axis=-1, keepdims=True)
        cen = hm - mu
        var = jnp.mean(cen * cen, axis=-1, keepdims=True)
        rs = lax.rsqrt(var + EPS)
        outs.append((sl, cen * rs, rs))
    return outs


def _mix_fwd(o_na, proj, h_f, h_b, ml_norm_w, tm):
    t = o_na.shape[0]

    def body(ona_ref, naz_ref, hf_ref, hb_ref, o_ref, z_ref, nw_ref, mix_ref):
        mix_ref[:, 0:NA_W] = (ona_ref[...] * _silu(naz_ref[...])).astype(BF16)
        hs = hf_ref[...] + hb_ref[...]
        z = z_ref[...]
        for sl, xn, _ in _ml_norm_parts(hs, o_ref[...], z, nw_ref[...]):
            mix_ref[:, NA_W + sl.start:NA_W + sl.stop] = (xn * nw_ref[:, sl] * _silu(z[:, sl])).astype(BF16)

    blk = lambda c: pl.BlockSpec((tm, 512), lambda i, c=c: (i, c))
    return pl.pallas_call(
        body, name="mix_fwd", grid=(t // tm,),
        in_specs=[blk(0), blk(3), blk(0), blk(0), blk(7), blk(8), _row(ML_W)],
        out_specs=pl.BlockSpec((tm, 1024), lambda i: (i, 0)),
        out_shape=jax.ShapeDtypeStruct((t, 1024), BF16),
        compiler_params=_cparams(("parallel",)),
    )(o_na, proj, h_f, h_b, proj, proj, ml_norm_w)


def _loss_fb(x, y, gate, target, fnw, tm):
    t, d = x.shape

    def body(x_ref, y_ref, g_ref, tg_ref, w_ref, loss_ref, dres_ref, dy_ref, dgate_ref, gw_ref):
        @pl.when(pl.program_id(0) == 0)
        def _():
            loss_ref[...] = jnp.zeros_like(loss_ref)
            dgate_ref[...] = jnp.zeros_like(dgate_ref)
            gw_ref[...] = jnp.zeros_like(gw_ref)
        yv = y_ref[...]
        gate_v = g_ref[...]
        hres = x_ref[...] + gate_v * yv
        r = lax.rsqrt(jnp.mean(hres * hres, axis=-1, keepdims=True) + EPS)
        xn = hres * r
        err = xn * w_ref[...] - tg_ref[...]
        loss_ref[...] += 0.5 * jnp.sum(jnp.sum(err * err, axis=-1, keepdims=True) * (1.0 / d), axis=0, keepdims=True)
        dout = err * (1.0 / d)
        gw_ref[...] += jnp.sum(dout * xn, axis=0, keepdims=True)
        dxn = dout * w_ref[...]
        dres = r * (dxn - xn * jnp.mean(dxn * xn, axis=-1, keepdims=True))
        dres_ref[...] = dres
        dy_ref[...] = (dres * gate_v).astype(BF16)
        dgate_ref[...] += jnp.sum(dres * yv, axis=0, keepdims=True)

    full = pl.BlockSpec((tm, d), lambda i: (i, 0))
    return pl.pallas_call(
        body, name="loss_fb", grid=(t // tm,),
        in_specs=[full, full, _row(d), full, _row(d)],
        out_specs=[pl.BlockSpec((1, 128), lambda i: (0, 0)), full, full, _row(d), _row(d)],
        out_shape=[jax.ShapeDtypeStruct((1, 128), F32), jax.ShapeDtypeStruct((t, d), F32),
                   jax.ShapeDtypeStruct((t, d), BF16), jax.ShapeDtypeStruct((1, d), F32),
                   jax.ShapeDtypeStruct((1, d), F32)],
        compiler_params=_cparams(("arbitrary",)),
    )(x, y, gate, target, fnw)


def _mix_bwd(dmix, o_na, proj, h_f, h_b, ml_norm_w, tm):
    t = o_na.shape[0]

    def body(dna_ref, dml_ref, ona_ref, naz_ref, hf_ref, hb_ref, o_ref, z_ref, nw_ref,
             dona_ref, dnaz_ref, dhs_ref, do_ref, dz_ref, gw_ref):
        @pl.when(pl.program_id(0) == 0)
        def _():
            gw_ref[...] = jnp.zeros_like(gw_ref)
        dna = dna_ref[...]
        naz = naz_ref[...]
        dona_ref[...] = dna * _silu(naz)
        dnaz_ref[...] = dna * ona_ref[...] * _dsilu(naz)
        hs = hf_ref[...] + hb_ref[...]
        z = z_ref[...]
        ov = o_ref[...]
        for sl, xn, rs in _ml_norm_parts(hs, ov, z, nw_ref[...]):
            dyv = dml_ref[:, sl]
            zz = z[:, sl]
            w = nw_ref[:, sl]
            sz = _silu(zz)
            dz_ref[:, sl] = dyv * xn * w * _dsilu(zz)
            gw_ref[:, sl] += jnp.sum(dyv * xn * sz, axis=0, keepdims=True)
            dxn = dyv * w * sz
            dhm = rs * (dxn - jnp.mean(dxn, axis=-1, keepdims=True)
                        - xn * jnp.mean(dxn * xn, axis=-1, keepdims=True))
            so = jax.nn.sigmoid(ov[:, sl])
            dhs_ref[:, sl] = dhm * so
            do_ref[:, sl] = dhm * hs[:, sl] * so * (1.0 - so)

    blk = lambda c: pl.BlockSpec((tm, 512), lambda i, c=c: (i, c))
    o512 = jax.ShapeDtypeStruct((t, 512), F32)
    return pl.pallas_call(
        body, name="mix_bwd", grid=(t // tm,),
        in_specs=[blk(0), blk(1), blk(0), blk(3), blk(0), blk(0), blk(7), blk(8), _row(ML_W)],
        out_specs=[blk(0)] * 5 + [_row(ML_W)],
        out_shape=[o512] * 5 + [jax.ShapeDtypeStruct((1, ML_W), F32)],
        compiler_params=_cparams(("arbitrary",)),
    )(dmix, dmix, o_na, proj, h_f, h_b, proj, proj, ml_norm_w)


def _prologue_bwd(x, dh, dres, norm_w, scale, tm):
    t, d = x.shape

    def body(x_ref, dh_ref, dres_ref, nw_ref, sc_ref, gx_ref, dsc_ref, dsh_ref, gnw_ref):
        @pl.when(pl.program_id(0) == 0)
        def _():
            dsc_ref[...] = jnp.zeros_like(dsc_ref)
            dsh_ref[...] = jnp.zeros_like(dsh_ref)
            gnw_ref[...] = jnp.zeros_like(gnw_ref)
        xv = x_ref[...]
        dhv = dh_ref[...]
        r = lax.rsqrt(jnp.mean(xv * xv, axis=-1, keepdims=True) + EPS)
        xn = xv * r
        nw = nw_ref[...]
        s1 = 1.0 + sc_ref[...]
        dsh_ref[...] += jnp.sum(dhv, axis=0, keepdims=True)
        dsc_ref[...] += jnp.sum(dhv * xn * nw, axis=0, keepdims=True)
        gnw_ref[...] += jnp.sum(dhv * xn * s1, axis=0, keepdims=True)
        dxn = dhv * nw * s1
        gx_ref[...] = dres_ref[...] + r * (dxn - xn * jnp.mean(dxn * xn, axis=-1, keepdims=True))

    full = pl.BlockSpec((tm, d), lambda i: (i, 0))
    return pl.pallas_call(
        body, name="prologue_bwd", grid=(t // tm,),
        in_specs=[full, full, full, _row(d), _row(d)],
        out_specs=[full, _row(d), _row(d), _row(d)],
        out_shape=[jax.ShapeDtypeStruct((t, d), F32)] + [jax.ShapeDtypeStruct((1, d), F32)] * 3,
        compiler_params=_cparams(("arbitrary",)),
    )(x, dh, dres, norm_w, scale)


def _na_static(rows):
    cases = [(0, 0), (NA_QROWS, NA_QROWS - 4), (rows - NA_QROWS, rows - NA_KROWS)]
    dy = np.zeros((3, NA_QROWS, NA_KROWS), np.int32)
    rv = np.zeros((3, NA_QROWS, NA_KROWS), bool)
    for cs, (r0, kr0) in enumerate(cases):
        for i in range(NA_QROWS):
            for j in range(NA_KROWS):
                r, kr = r0 + i, kr0 + j
                rs = min(max(r - NA_KH // 2, 0), rows - NA_KH)
                rv[cs, i, j] = rs <= kr <= rs + NA_KH - 1
                dy[cs, i, j] = min(max(kr - r + NA_KH - 1, 0), 2 * NA_KH - 2)
    cq = np.arange(GRID_W)[:, None]
    ck = np.arange(GRID_W)[None, :]
    cs0 = np.clip(cq - NA_KW // 2, 0, GRID_W - NA_KW)
    cv = (ck >= cs0) & (ck < cs0 + NA_KW)
    dx = np.clip(ck - cq, -(NA_KW - 1), NA_KW - 1) + NA_KW - 1
    return dy, rv, dx.astype(np.int32), cv


def _na_bias_table(rpb, rows):
    dy, rv, dx, cv = _na_static(rows)
    rpx = jnp.where(cv[None, None], rpb[:, :, dx], NEG)
    tbl = rpx[:, dy.reshape(-1)].reshape(NA_HEADS, 3, NA_QROWS, NA_KROWS, GRID_W, GRID_W)
    tbl = jnp.where(rv[None, :, :, :, None, None], tbl, NEG)
    return tbl.transpose(0, 1, 2, 4, 3, 5).reshape(NA_HEADS, 3, NA_QT, NA_KT)


def _na_specs(t):
    nb = t // NA_QT
    nkb = t // NA_KCH
    npieces = NA_KT // NA_KCH

    def kb0(b):
        return jnp.clip(2 * b - 1, 0, nkb - npieces)

    def case(b):
        return jnp.where(b == 0, 0, jnp.where(b == nb - 1, 2, 1))

    q_spec = pl.BlockSpec((NA_QT, 128), lambda p, b: (b, p))
    k_specs = [pl.BlockSpec((NA_KCH, 128), lambda p, b, i=i: (kb0(b) + i, 4 + p)) for i in range(npieces)]
    v_specs = [pl.BlockSpec((NA_KCH, 128), lambda p, b, i=i: (kb0(b) + i, 8 + p)) for i in range(npieces)]
    tbl_spec = pl.BlockSpec((2, 1, NA_QT, NA_KT), lambda p, b: (p, case(b), 0, 0))
    io_spec = pl.BlockSpec((NA_QT, 128), lambda p, b: (b, p))
    return nb, npieces, kb0, case, q_spec, k_specs, v_specs, tbl_spec, io_spec


def _na_probs(qh, ks, tbl_ref, hh, npieces):
    s = [_nt(qh, ks[i]) + tbl_ref[hh, 0, :, i * NA_KCH:(i + 1) * NA_KCH] for i in range(npieces)]
    m = functools.reduce(jnp.maximum, [jnp.max(si, axis=1, keepdims=True) for si in s])
    p = [jnp.exp(si - m) for si in s]
    l = functools.reduce(jnp.add, [jnp.sum(pi, axis=1, keepdims=True) for pi in p])
    return p, l


def _na_fwd(proj, tbl):
    t = proj.shape[0]
    nb, npieces, _, _, q_spec, k_specs, v_specs, tbl_spec, io_spec = _na_specs(t)

    def body(*refs):
        q_ref = refs[0]
        k_refs = refs[1:1 + npieces]
        v_refs = refs[1 + npieces:1 + 2 * npieces]
        tbl_ref, o_ref = refs[1 + 2 * npieces:]
        lane = lax.broadcasted_iota(jnp.int32, (1, 128), 1)
        qv = q_ref[...] * (NA_HEAD_DIM ** -0.5)
        ks = [r[...].astype(BF16) for r in k_refs]
        vs = [r[...].astype(BF16) for r in v_refs]
        out = jnp.zeros((NA_QT, 128), F32)
        for hh in range(2):
            msk = (lane // NA_HEAD_DIM) == hh
            qh = jnp.where(msk, qv, 0.0).astype(BF16)
            p, l = _na_probs(qh, ks, tbl_ref, hh, npieces)
            o = functools.reduce(jnp.add, [_nn(p[i].astype(BF16), vs[i]) for i in range(npieces)])
            out = jnp.where(msk, o / l, out)
        o_ref[...] = out

    return pl.pallas_call(
        body, name="na_fwd", grid=(4, nb),
        in_specs=[q_spec] + k_specs + v_specs + [tbl_spec],
        out_specs=io_spec,
        out_shape=jax.ShapeDtypeStruct((t, NA_W), F32),
        compiler_params=_cparams(("parallel", "arbitrary")),
    )(*([proj] * (1 + 2 * npieces)), tbl)


def _na_bwd(proj, tbl, d_o, o_na):
    t = proj.shape[0]
    nb, npieces, kb0, case, q_spec, k_specs, v_specs, tbl_spec, io_spec = _na_specs(t)

    def body(*refs):
        q_ref = refs[0]
        k_refs = refs[1:1 + npieces]
        v_refs = refs[1 + npieces:1 + 2 * npieces]
        (tbl_ref, do_ref, o_ref, dq_ref, dk_hbm, dv_hbm, rpb_ref,
         dk_acc, dv_acc, ds_scr, sem) = refs[1 + 2 * npieces:]
        p_id = pl.program_id(0)
        b = pl.program_id(1)

        @pl.when(b == 0)
        def _():
            dk_acc[...] = jnp.zeros_like(dk_acc)
            dv_acc[...] = jnp.zeros_like(dv_acc)

        @pl.when((b == 0) | (b == 1) | (b == nb - 1))
        def _():
            rpb_ref[...] = jnp.zeros_like(rpb_ref)

        lane = lax.broadcasted_iota(jnp.int32, (1, 128), 1)
        scale = NA_HEAD_DIM ** -0.5
        qv = q_ref[...] * scale
        ks = [r[...].astype(BF16) for r in k_refs]
        vs = [r[...].astype(BF16) for r in v_refs]
        dov = do_ref[...]
        ov = o_ref[...]
        tok0 = kb0(b) * NA_KCH
        dq = jnp.zeros((NA_QT, 128), F32)
        for hh in range(2):
            msk = (lane // NA_HEAD_DIM) == hh
            qh = jnp.where(msk, qv, 0.0).astype(BF16)
            p, l = _na_probs(qh, ks, tbl_ref, hh, npieces)
            rl = 1.0 / l
            doh = jnp.where(msk, dov, 0.0)
            dd = jnp.sum(doh * ov, axis=1, keepdims=True)
            dohb = doh.astype(BF16)
            dqh = jnp.zeros((NA_QT, 128), F32)
            for i in range(npieces):
                pn = p[i] * rl
                ds = pn * (_nt(dohb, vs[i]) - dd)
                dsb = ds.astype(BF16)
                dqh = dqh + _nn(dsb, ks[i])
                rows = pl.ds(pl.multiple_of(tok0 + i * NA_KCH, NA_KCH), NA_KCH)
                dk_acc[rows, :] += _tn(dsb, qh)
                dv_acc[rows, :] += _tn(pn.astype(BF16), dohb)
                for cb in range(NA_KCH // 128):
                    ds_scr[i * (NA_KCH // 128) + cb] = ds[:, cb * 128:(cb + 1) * 128]
            dq = jnp.where(msk, dqh * scale, dq)
            acc = jnp.zeros((NA_QROWS, NA_KT), F32)
            for cq in range(GRID_W):
                tile = jnp.concatenate([ds_scr[cb, pl.ds(cq, NA_QROWS, stride=GRID_W), :]
                                        for cb in range(NA_KT // 128)], axis=1)
                acc = acc + pltpu.roll(tile, (NA_KT - cq) % NA_KT, 1)
            rpb_ref[0, 0, hh] += acc
        dq_ref[...] = dq

        @pl.when(b == nb - 1)
        def _():
            cols = pl.ds(pl.multiple_of(p_id * 128, 128), 128)
            ck = pltpu.make_async_copy(dk_acc, dk_hbm.at[:, cols], sem.at[0])
            cv = pltpu.make_async_copy(dv_acc, dv_hbm.at[:, cols], sem.at[1])
            ck.start()
            cv.start()
            ck.wait()
            cv.wait()

    o512 = jax.ShapeDtypeStruct((t, NA_W), F32)
    return pl.pallas_call(
        body, name="na_bwd", grid=(4, nb),
        in_specs=[q_spec] + k_specs + v_specs + [tbl_spec, io_spec, io_spec],
        out_specs=[io_spec, pl.BlockSpec(memory_space=pl.ANY), pl.BlockSpec(memory_space=pl.ANY),
                   pl.BlockSpec((1, 1, 2, NA_QROWS, NA_KT), lambda p, b: (p, case(b), 0, 0, 0))],
        out_shape=[o512, o512, o512, jax.ShapeDtypeStruct((4, 3, 2, NA_QROWS, NA_KT), F32)],
        scratch_shapes=[pltpu.VMEM((t, 128), F32), pltpu.VMEM((t, 128), F32),
                        pltpu.VMEM((NA_KT // 128, NA_QT, 128), F32), pltpu.SemaphoreType.DMA((2,))],
        compiler_params=_cparams(("arbitrary", "arbitrary")),
    )(*([proj] * (1 + 2 * npieces)), tbl, d_o, o_na)


def _rpb_reduce(rpbacc, rows):
    dy, rv, _, _ = _na_static(rows)
    a = rpbacc.transpose(0, 2, 1, 3, 4).reshape(NA_HEADS, 3, NA_QROWS, NA_KT)
    if rows // NA_QROWS < 3:
        a = a.at[:, 1].set(0.0)
    jj = np.arange(NA_KROWS)[:, None]
    dxo = np.arange(-(NA_KW - 1), NA_KW)[None, :]
    idx = ((jj * GRID_W + dxo) % NA_KT).reshape(-1)
    g = a[..., idx].reshape(NA_HEADS, 3 * NA_QROWS * NA_KROWS, 2 * NA_KW - 1)
    g = jnp.pad(g, ((0, 0), (0, 0), (0, 128 - (2 * NA_KW - 1))))
    nmat = np.zeros((16, 3 * NA_QROWS * NA_KROWS), np.float32)
    flat_dy = dy.reshape(-1)
    flat_rv = rv.reshape(-1)
    nmat[flat_dy[flat_rv], np.nonzero(flat_rv)[0]] = 1.0

    def body(n_ref, g_ref, o_ref):
        o_ref[0] = jnp.dot(n_ref[...], g_ref[0], precision=HI, preferred_element_type=F32)

    out = pl.pallas_call(
        body, name="rpb_reduce", grid=(NA_HEADS,),
        in_specs=[pl.BlockSpec((16, nmat.shape[1]), lambda h: (0, 0)),
                  pl.BlockSpec((1, nmat.shape[1], 128), lambda h: (h, 0, 0))],
        out_specs=pl.BlockSpec((1, 16, 128), lambda h: (h, 0, 0)),
        out_shape=jax.ShapeDtypeStruct((NA_HEADS, 16, 128), F32),
        compiler_params=_cparams(("parallel",)),
    )(jnp.asarray(nmat), g)
    return out[:, :2 * NA_KH - 1, :2 * NA_KW - 1]


def _halo_specs(tm, t, col):
    nt8 = t // 8
    per = tm // 8
    return [pl.BlockSpec((tm, 1024), lambda i: (i, col)),
            pl.BlockSpec((8, 1024), lambda i: (jnp.maximum(i * per - 1, 0), col)),
            pl.BlockSpec((8, 1024), lambda i: (jnp.minimum((i + 1) * per, nt8 - 1), col))]


def _fill_ext(ext, cur_ref, prev_ref, next_ref, tm, nt):
    i = pl.program_id(0)
    ext[0:8, :] = jnp.where(i == 0, 0.0, prev_ref[...])
    ext[8:8 + tm, :] = cur_ref[...]
    ext[8 + tm:16 + tm, :] = jnp.where(i == nt - 1, 0.0, next_ref[...])


def _conv_fwd(proj, conv_w8, conv_b, tm):
    t = proj.shape[0]
    nt = t // tm

    def body(u_ref, up_ref, un_ref, w_ref, b_ref, pre_ref, act_ref, ext):
        _fill_ext(ext, u_ref, up_ref, un_ref, tm, nt)
        pre = b_ref[...] + w_ref[0:1, :] * ext[pl.ds(6, tm), :]
        for j in range(1, CONV_W):
            pre = pre + w_ref[j:j + 1, :] * ext[pl.ds(6 + j, tm), :]
        pre_ref[...] = pre
        act_ref[...] = _silu(pre)

    full = pl.BlockSpec((tm, 1024), lambda i: (i, 0))
    o = jax.ShapeDtypeStruct((t, 1024), F32)
    return pl.pallas_call(
        body, name="conv_fwd", grid=(nt,),
        in_specs=_halo_specs(tm, t, 2) + [pl.BlockSpec((8, 1024), lambda i: (0, 0)), _row(1024)],
        out_specs=[full, full], out_shape=[o, o],
        scratch_shapes=[pltpu.VMEM((tm + 16, 1024), F32)],
        compiler_params=_cparams(("parallel",)),
    )(proj, proj, proj, conv_w8, conv_b)


def _conv_bwd1(dq_f, dq_b, dk_f, dk_b, dv_f, dv_b, pre, tm):
    t = pre.shape[0]

    def body(dqf, dqb, dkf, dkb, dvf, dvb, pre_ref, dpre_ref, dv_ref, gb_ref):
        @pl.when(pl.program_id(0) == 0)
        def _():
            gb_ref[...] = jnp.zeros_like(gb_ref)
        ds = _dsilu(pre_ref[...])
        dpre_ref[:, 0:ML_W] = (dqf[...] + dqb[...]) * ds[:, 0:ML_W]
        dpre_ref[:, ML_W:] = (dkf[...] + dkb[...]) * ds[:, ML_W:]
        dv_ref[...] = dvf[...] + dvb[...]
        gb_ref[...] += jnp.sum(dpre_ref[...], axis=0, keepdims=True)

    half = pl.BlockSpec((tm, 512), lambda i: (i, 0))
    full = pl.BlockSpec((tm, 1024), lambda i: (i, 0))
    return pl.pallas_call(
        body, name="conv_bwd1", grid=(t // tm,),
        in_specs=[half] * 6 + [full],
        out_specs=[full, half, _row(1024)],
        out_shape=[jax.ShapeDtypeStruct((t, 1024), F32), jax.ShapeDtypeStruct((t, 512), F32),
                   jax.ShapeDtypeStruct((1, 1024), F32)],
        compiler_params=_cparams(("arbitrary",)),
    )(dq_f, dq_b, dk_f, dk_b, dv_f, dv_b, pre)


def _conv_bwd2(dpre, proj, conv_w8, tm):
    t = dpre.shape[0]
    nt = t // tm

    def body(d_ref, dp_ref, dn_ref, u_ref, up_ref, un_ref, w_ref, du_ref, gw_ref, extd, extu):
        @pl.when(pl.program_id(0) == 0)
        def _():
            gw_ref[...] = jnp.zeros_like(gw_ref)
        _fill_ext(extd, d_ref, dp_ref, dn_ref, tm, nt)
        _fill_ext(extu, u_ref, up_ref, un_ref, tm, nt)
        dcur = d_ref[...]
        du = w_ref[0:1, :] * extd[pl.ds(10, tm), :]
        for j in range(1, CONV_W):
            du = du + w_ref[j:j + 1, :] * extd[pl.ds(10 - j, tm), :]
        du_ref[...] = du
        for j in range(CONV_W):
            gw_ref[j:j + 1, :] += jnp.sum(dcur * extu[pl.ds(6 + j, tm), :], axis=0, keepdims=True)

    full = pl.BlockSpec((tm, 1024), lambda i: (i, 0))
    return pl.pallas_call(
        body, name="conv_bwd2", grid=(nt,),
        in_specs=_halo_specs(tm, t, 0) + _halo_specs(tm, t, 2) + [pl.BlockSpec((8, 1024), lambda i: (0, 0))],
        out_specs=[full, pl.BlockSpec((8, 1024), lambda i: (0, 0))],
        out_shape=[jax.ShapeDtypeStruct((t, 1024), F32), jax.ShapeDtypeStruct((8, 1024), F32)],
        scratch_shapes=[pltpu.VMEM((tm + 16, 1024), F32), pltpu.VMEM((tm + 16, 1024), F32)],
        compiler_params=_cparams(("arbitrary",)),
    )(dpre, dpre, dpre, proj, proj, proj, conv_w8)


def _ml_consts(rev):
    iu = lax.broadcasted_iota(jnp.int32, (ML_CHUNK, ML_CHUNK), 0)
    js = lax.broadcasted_iota(jnp.int32, (ML_CHUNK, ML_CHUNK), 1)
    eye = iu == js
    le = iu <= js
    ge = iu >= js
    csum, csum_t, causal = (ge, le, le) if rev else (le, ge, ge)
    return eye, csum.astype(F32), csum_t.astype(F32), causal


def _col(row, eye):
    return jnp.sum(jnp.where(eye, row, 0.0), axis=1, keepdims=True)


def _rowof(col, eye):
    return jnp.sum(jnp.where(eye, col, 0.0), axis=0, keepdims=True)


def _ml_gates(gi, gf, m0, csum, rev):
    lf = jax.nn.log_sigmoid(gf)
    b_rows = jnp.dot(lf, csum, precision=HI, preferred_element_type=F32)
    bl = jnp.sum(lf, axis=1, keepdims=True)
    a_rows = bl - b_rows + gi
    mloc = jnp.max(a_rows, axis=1, keepdims=True)
    order = list(range(ML_NB))[::-1] if rev else list(range(ML_NB))
    mp, mn, decay = {}, {}, {}
    m = m0
    for n in order:
        mp[n] = m
        m = jnp.maximum(bl[n:n + 1] + m, mloc[n:n + 1])
        mn[n] = m
        decay[n] = jnp.exp(bl[n:n + 1] + mp[n] - m)
    return b_rows, a_rows, gi - b_rows, mp, mn, decay, order


def _ml_load(q_ref, k_ref, v_ref, n):
    sl = slice(n * ML_CHUNK, (n + 1) * ML_CHUNK)
    qb = q_ref[sl, :].astype(BF16)
    kb = (k_ref[sl, :] * (ML_HEAD_DIM ** -0.5)).astype(BF16)
    vn = v_ref[sl, :]
    return sl, qb, kb, vn


def _ml_state_scan(q_ref, k_ref, v_ref, a_rows, mn, decay, order, c0, n0, eye):
    u, nu, wcol = {}, {}, {}
    for n in range(ML_NB):
        _, _, kb, vn = _ml_load(q_ref, k_ref, v_ref, n)
        wcol[n] = jnp.exp(_col(a_rows[n:n + 1], eye) - mn[n])
        u[n] = _tn((wcol[n] * vn).astype(BF16), kb)
        nu[n] = jnp.sum(wcol[n] * kb.astype(F32), axis=0, keepdims=True)
    cp, npv = {}, {}
    c, nv = c0, n0
    for n in order:
        cp[n], npv[n] = c, nv
        c = decay[n] * c + u[n]
        nv = decay[n] * nv + nu[n]
    return cp, npv, wcol, c, nv


def _ml_intra(qb, kb, vn, b_row, imb_row, mpn, cpn, npn, causal, eye):
    b_col = _col(b_row, eye)
    dlog = jnp.where(causal, b_col + imb_row, NEG)
    m_inter = b_col + mpn
    m_t = jnp.maximum(m_inter, jnp.max(dlog, axis=1, keepdims=True))
    pm = jnp.exp(dlog - m_t)
    s = _nt(qb, kb) * pm
    inter = jnp.exp(m_inter - m_t)
    cq = _nt(qb, cpn.astype(BF16))
    qn = jnp.sum(qb.astype(F32) * npn.astype(BF16).astype(F32), axis=1, keepdims=True)
    num = _nn(s.astype(BF16), vn.astype(BF16)) + inter * cq
    den = jnp.sum(s, axis=1, keepdims=True) + inter * qn
    floor = jnp.exp(-m_t)
    dn = jnp.maximum(jnp.abs(den), floor)
    return dict(pm=pm, s=s, inter=inter, cq=cq, qn=qn, num=num, den=den, floor=floor, dn=dn)


def _ml_specs(t, rev):
    nblk = t // ML_TB
    blk = (lambda g: nblk - 1 - g) if rev else (lambda g: g)
    tile = lambda c0: pl.BlockSpec((ML_TB, 128), lambda hd, g, c0=c0: (blk(g), c0 + hd))
    gate = pl.BlockSpec((1, ML_NB, ML_CHUNK), lambda hd, g: (hd, blk(g), 0))
    cchk = pl.BlockSpec((1, 1, 128, 128), lambda hd, g: (hd, blk(g), 0, 0))
    nmchk = pl.BlockSpec((1, 1, 8, 128), lambda hd, g: (hd, blk(g), 0, 0))
    return nblk, blk, tile, gate, cchk, nmchk


def _ml_fwd(qk_act, proj, gi, gf, rev, name):
    t = qk_act.shape[0]
    nblk, _, tile, gate, cchk, nmchk = _ml_specs(t, rev)

    def body(q_ref, k_ref, v_ref, gi_ref, gf_ref, h_ref, cchk_ref, nmchk_ref, c_ref, nm_ref):
        @pl.when(pl.program_id(1) == 0)
        def _():
            c_ref[...] = jnp.zeros_like(c_ref)
            nm_ref[...] = jnp.zeros_like(nm_ref)
        cchk_ref[0, 0] = c_ref[...]
        nmchk_ref[0, 0] = nm_ref[...]
        eye, csum, _, causal = _ml_consts(rev)
        b_rows, a_rows, imb_rows, mp, mn, decay, order = _ml_gates(
            gi_ref[0], gf_ref[0], nm_ref[1:2, 0:1], csum, rev)
        cp, npv, _, c, nv = _ml_state_scan(q_ref, k_ref, v_ref, a_rows, mn, decay, order,
                                            c_ref[...], nm_ref[0:1, :], eye)
        c_ref[...] = c
        nm_ref[0:1, :] = nv
        nm_ref[1:2, :] = jnp.broadcast_to(mn[order[-1]], (1, 128))
        for n in range(ML_NB):
            sl, qb, kb, vn = _ml_load(q_ref, k_ref, v_ref, n)
            r = _ml_intra(qb, kb, vn, b_rows[n:n + 1], imb_rows[n:n + 1], mp[n], cp[n], npv[n], causal, eye)
            h_ref[sl, :] = r['num'] / r['dn']

    return pl.pallas_call(
        body, name=name, grid=(ML_HEADS, nblk),
        in_specs=[tile(0), tile(4), tile(24), gate, gate],
        out_specs=[tile(0), cchk, nmchk],
        out_shape=[jax.ShapeDtypeStruct((t, ML_W), F32),
                   jax.ShapeDtypeStruct((ML_HEADS, nblk, 128, 128), F32),
                   jax.ShapeDtypeStruct((ML_HEADS, nblk, 8, 128), F32)],
        scratch_shapes=[pltpu.VMEM((128, 128), F32), pltpu.VMEM((8, 128), F32)],
        compiler_params=_cparams(("parallel", "arbitrary")),
    )(qk_act, qk_act, proj, gi, gf)


def _ml_bwd(qk_act, proj, gi, gf, dh, cchk_a, nmchk_a, rev, name):
    t = qk_act.shape[0]
    nblk, _, tile, gate, cchk, nmchk = _ml_specs(t, not rev)

    def body(q_ref, k_ref, v_ref, gi_ref, gf_ref, dh_ref, cchk_ref, nmchk_ref,
             dq_ref, dk_ref, dv_ref, dgi_ref, dgf_ref, dc_ref, dn_ref, db_scr, dbl_scr, di_scr):
        @pl.when(pl.program_id(1) == 0)
        def _():
            dc_ref[...] = jnp.zeros_like(dc_ref)
            dn_ref[...] = jnp.zeros_like(dn_ref)
        eye, csum, csum_t, causal = _ml_consts(rev)
        gfv = gf_ref[0]
        b_rows, a_rows, imb_rows, mp, mn, decay, order = _ml_gates(
            gi_ref[0], gfv, nmchk_ref[0, 0, 1:2, 0:1], csum, rev)
        cp, npv, wcol, _, _ = _ml_state_scan(q_ref, k_ref, v_ref, a_rows, mn, decay, order,
                                             cchk_ref[0, 0], nmchk_ref[0, 0, 0:1, :], eye)
        xs, xns, dbcol, dimb = {}, {}, {}, {}
        for n in range(ML_NB):
            sl, qb, kb, vn = _ml_load(q_ref, k_ref, v_ref, n)
            vb = vn.astype(BF16)
            r = _ml_intra(qb, kb, vn, b_rows[n:n + 1], imb_rows[n:n + 1], mp[n], cp[n], npv[n], causal, eye)
            dhv = dh_ref[sl, :]
            rdn = 1.0 / r['dn']
            dnum = dhv * rdn
            live = jnp.abs(r['den']) > r['floor']
            dden = jnp.where(live, -jnp.sum(dnum * r['num'], axis=1, keepdims=True) * rdn
                             * jnp.sign(r['den']), 0.0)
            dnb = dnum.astype(BF16)
            dsf = _nt(dnb, vb) + dden
            dv_ref[sl, :] = _tn(r['s'].astype(BF16), dnb)
            gb = (dsf * r['pm']).astype(BF16)
            cpb = cp[n].astype(BF16)
            npf = npv[n].astype(BF16).astype(F32)
            idd = r['inter'] * dden
            dq_ref[sl, :] = _nn(gb, kb) + r['inter'] * _nn(dnb, cpb) + idd * npf
            dk_ref[sl, :] = _tn(gb, qb)
            rr = dsf * r['s']
            dinter = jnp.sum(dnum * r['cq'], axis=1, keepdims=True) + dden * r['qn']
            dbcol[n] = jnp.sum(rr, axis=1, keepdims=True) + dinter * r['inter']
            dimb[n] = jnp.sum(rr, axis=0, keepdims=True)
            xs[n] = _tn((r['inter'] * dnum).astype(BF16), qb)
            xns[n] = jnp.sum(idd * qb.astype(F32), axis=0, keepdims=True)
        dcn, dnn = {}, {}
        dc, dn = dc_ref[...], dn_ref[0:1, :]
        for n in order[::-1]:
            dcn[n], dnn[n] = dc, dn
            dc = decay[n] * dc + xs[n]
            dn = decay[n] * dn + xns[n]
        dc_ref[...] = dc
        dn_ref[0:1, :] = dn
        kscale = ML_HEAD_DIM ** -0.5
        for n in range(ML_NB):
            sl, qb, kb, vn = _ml_load(q_ref, k_ref, v_ref, n)
            kf = kb.astype(F32)
            dcb = dcn[n].astype(BF16)
            ddecay = (jnp.sum(jnp.sum(dcn[n] * cp[n], axis=1, keepdims=True), axis=0, keepdims=True)
                      + jnp.sum(dnn[n] * npv[n], axis=1, keepdims=True))
            z = _nn(vn.astype(BF16), dcb)
            dw = jnp.sum((z + dnn[n]) * kf, axis=1, keepdims=True)
            dv_ref[sl, :] += wcol[n] * _nt(kb, dcb)
            dk_ref[sl, :] = (dk_ref[sl, :] + wcol[n] * (z + dnn[n])) * kscale
            da = dw * wcol[n]
            dbl = jnp.sum(da, axis=0, keepdims=True) + ddecay * decay[n]
            da_row = _rowof(da, eye)
            db_scr[n:n + 1, :] = _rowof(dbcol[n] - da, eye) - dimb[n]
            di_scr[n:n + 1, :] = dimb[n] + da_row
            dbl_scr[n:n + 1, :] = jnp.broadcast_to(dbl, (1, ML_CHUNK))
        dlf = jnp.dot(db_scr[...], csum_t, precision=HI, preferred_element_type=F32) + dbl_scr[...]
        dgf_ref[0] = dlf * jax.nn.sigmoid(-gfv)
        dgi_ref[0] = di_scr[...]

    nc = t // ML_CHUNK
    o512 = jax.ShapeDtypeStruct((t, ML_W), F32)
    og = jax.ShapeDtypeStruct((ML_HEADS, nc, ML_CHUNK), F32)
    return pl.pallas_call(
        body, name=name, grid=(ML_HEADS, nblk),
        in_specs=[tile(0), tile(4), tile(24), gate, gate, tile(0), cchk, nmchk],
        out_specs=[tile(0), tile(0), tile(0), gate, gate],
        out_shape=[o512, o512, o512, og, og],
        scratch_shapes=[pltpu.VMEM((128, 128), F32), pltpu.VMEM((8, 128), F32),
                        pltpu.VMEM((ML_NB, ML_CHUNK), F32), pltpu.VMEM((ML_NB, ML_CHUNK), F32),
                        pltpu.VMEM((ML_NB, ML_CHUNK), F32)],
        compiler_params=_cparams(("parallel", "arbitrary")),
    )(qk_act, qk_act, proj, gi, gf, dh, cchk_a, nmchk_a)


def _gate_rows(gates16, t):
    g = gates16.reshape(t // ML_CHUNK, ML_CHUNK, 4, ML_HEADS).transpose(2, 3, 0, 1)
    return g[0], g[1], g[2], g[3]


def _gate_cols(dgi_f, dgf_f, dgi_b, dgf_b, t):
    g = jnp.stack([dgi_f, dgf_f, dgi_b, dgf_b]).transpose(2, 3, 0, 1).reshape(t, 4 * ML_HEADS)
    return jnp.pad(g, ((0, 0), (0, 128 - 4 * ML_HEADS)))


def _local_step(x, target, shift, scale, gate, norm_w, w_in_p, b_in_p, conv_w8, conv_b, rpb,
                ml_norm_w, w_out_b, final_norm_w):
    t = x.shape[0]
    rows = t // GRID_W
    tm = 512
    h = _prologue(x, norm_w, scale, shift, tm)
    proj = _mm(h, w_in_p, 'nn', 1024, 512, 1024, "in_proj", bias=b_in_p)
    tbl = _na_bias_table(rpb, rows)
    o_na = _na_fwd(proj, tbl)
    pre, qk_act = _conv_fwd(proj, conv_w8, conv_b, tm)
    gi_f, gf_f, gi_b, gf_b = _gate_rows(proj[:, IN_MAIN:IN_MAIN + 4 * ML_HEADS], t)
    h_f, cchk_f, nmchk_f = _ml_fwd(qk_act, proj, gi_f, gf_f, False, "ml_fwd_f")
    h_b, cchk_b, nmchk_b = _ml_fwd(qk_act, proj, gi_b, gf_b, True, "ml_fwd_b")
    mix = _mix_fwd(o_na, proj, h_f, h_b, ml_norm_w, tm)
    y = _mm(mix, w_out_b, 'nn', 1024, 512, 1024, "out_proj")
    loss, dres, dy, dgate, g_fnw = _loss_fb(x, y, gate, target, final_norm_w, tm)
    dmix = _mm(dy, w_out_b, 'nt', 1024, 512, 1024, "d_mix")
    g_w_out = _mm(mix, dy, 'tn', 1024, 512, 1024, "g_w_out")
    d_ona, d_naz, dhs, d_o, d_z, g_mlnw = _mix_bwd(dmix, o_na, proj, h_f, h_b, ml_norm_w, tm)
    dq_na, dk_na, dv_na, rpbacc = _na_bwd(proj, tbl, d_ona, o_na)
    g_rpb = _rpb_reduce(rpbacc, rows)
    dq_f, dk_f, dv_f, dgi_f, dgf_f = _ml_bwd(qk_act, proj, gi_f, gf_f, dhs, cchk_f, nmchk_f, False, "ml_bwd_f")
    dq_b, dk_b, dv_b, dgi_b, dgf_b = _ml_bwd(qk_act, proj, gi_b, gf_b, dhs, cchk_b, nmchk_b, True, "ml_bwd_b")
    dpre, dv_ml, g_conv_b = _conv_bwd1(dq_f, dq_b, dk_f, dk_b, dv_f, dv_b, pre, tm)
    du, g_conv_w = _conv_bwd2(dpre, proj, conv_w8, tm)
    dgates = _gate_cols(dgi_f, dgf_f, dgi_b, dgf_b, t)
    dproj = jnp.concatenate([dq_na, dk_na, dv_na, d_naz, du, dv_ml, d_o, d_z, dgates], axis=1)
    g_b_in = _colsum(dproj, tm, "g_b_in")
    dh = _mm(dproj, w_in_p, 'nt', 256, 1024, IN_PAD, "d_h")
    g_w_in = _mm(h, dproj, 'tn', 1024, 512, 1024, "g_w_in")
    grad_x, dscale, dshift, g_nw = _prologue_bwd(x, dh, dres, norm_w, scale, tm)
    dmod = jnp.concatenate([dshift, dscale, dgate], axis=1)
    return (loss, grad_x, dmod, g_nw, g_w_in, g_b_in, g_conv_w, g_conv_b, g_rpb, g_mlnw, g_w_out, g_fnw)


MESH = pl.DeviceIdType.MESH
N_DEV = 8
ANY = pl.BlockSpec(memory_space=pl.ANY)
WHOLE_VMEM = pl.BlockSpec(memory_space=pltpu.VMEM)


def _allgather8(blocks, name):
    na = len(blocks)

    def body(*refs):
        x_refs = refs[:na]
        out_refs = refs[na:2 * na]
        send_sems, recv_sems, local_sems = refs[2 * na:]
        x, y, c = lax.axis_index("x"), lax.axis_index("y"), lax.axis_index("c")
        me, sibling = (x, y, c), (x, y, 1 - c)
        chips = [(1 - x, y), (x, 1 - y), (1 - x, 1 - y)]

        def rows(a, px, py, pc):
            m_per = x_refs[a].shape[0]
            return out_refs[a].at[pl.ds((4 * px + 2 * py + pc) * m_per, m_per), :]

        def copy(a, k, block, to, src=None):
            return pltpu.make_async_remote_copy(
                src_ref=rows(a, *block) if src is None else src, dst_ref=rows(a, *block),
                send_sem=send_sems.at[a, k], recv_sem=recv_sems.at[a, k],
                device_id=to, device_id_type=MESH)

        mine, first, passed = [], [], []
        for a in range(na):
            cp = pltpu.make_async_copy(x_refs[a], rows(a, *me), local_sems.at[a])
            cp.start()
            mine.append(cp)
            first.append(copy(a, 0, me, sibling, src=x_refs[a]))
            first += [copy(a, 1 + j, me, (*chip, c), src=x_refs[a]) for j, chip in enumerate(chips)]
        for cp in first:
            cp.start()
        for a in range(na):
            for j, chip in enumerate(chips):
                copy(a, 1 + j, (*chip, c), me).wait_recv()
                fwd = copy(a, 4 + j, (*chip, c), sibling)
                fwd.start()
                passed.append(fwd)
        for a in range(na):
            copy(a, 0, sibling, me).wait_recv()
            for j, chip in enumerate(chips):
                copy(a, 4 + j, (*chip, 1 - c), me).wait_recv()
        for cp in first + passed:
            cp.wait_send()
        for cp in mine:
            cp.wait()

    return pl.pallas_call(
        body, name=name,
        out_shape=[jax.ShapeDtypeStruct((N_DEV * b.shape[0], b.shape[1]), b.dtype) for b in blocks],
        in_specs=[WHOLE_VMEM] * na, out_specs=[WHOLE_VMEM] * na,
        scratch_shapes=[pltpu.SemaphoreType.DMA((na, 7)), pltpu.SemaphoreType.DMA((na, 7)),
                        pltpu.SemaphoreType.DMA((na,))],
        compiler_params=pltpu.CompilerParams(vmem_limit_bytes=VMEM_LIMIT),
    )(*blocks)


def _pair_exchange(arrs, name):
    na = len(arrs)

    def body(*refs):
        in_refs = refs[:na]
        out_refs = refs[na:2 * na]
        send_sems, recv_sems = refs[2 * na:]
        sibling = (lax.axis_index("x"), lax.axis_index("y"), 1 - lax.axis_index("c"))
        copies = [pltpu.make_async_remote_copy(
            src_ref=in_refs[a], dst_ref=out_refs[a], send_sem=send_sems.at[a], recv_sem=recv_sems.at[a],
            device_id=sibling, device_id_type=MESH) for a in range(na)]
        for cp in copies:
            cp.start()
        for cp in copies:
            cp.wait()

    return pl.pallas_call(
        body, name=name,
        out_shape=[jax.ShapeDtypeStruct(a.shape, a.dtype) for a in arrs],
        in_specs=[ANY] * na, out_specs=[ANY] * na,
        scratch_shapes=[pltpu.SemaphoreType.DMA((na,)), pltpu.SemaphoreType.DMA((na,))],
    )(*arrs)


def _chip_exchange(arrs, name):
    na = len(arrs)

    def body(*refs):
        in_refs = refs[:na]
        out_refs = refs[na:2 * na]
        send_sems, recv_sems, local_sems = refs[2 * na:]
        x, y, c = lax.axis_index("x"), lax.axis_index("y"), lax.axis_index("c")
        my_chip = 2 * x + y
        chips = [(1 - x, y), (x, 1 - y), (1 - x, 1 - y)]
        local, remote = [], []
        for a in range(na):
            cp = pltpu.make_async_copy(in_refs[a].at[my_chip], out_refs[a].at[my_chip], local_sems.at[a])
            cp.start()
            local.append(cp)
            for j, (px, py) in enumerate(chips):
                cp = pltpu.make_async_remote_copy(
                    src_ref=in_refs[a].at[2 * px + py], dst_ref=out_refs[a].at[my_chip],
                    send_sem=send_sems.at[a, j], recv_sem=recv_sems.at[a, j],
                    device_id=(px, py, c), device_id_type=MESH)
                cp.start()
                remote.append(cp)
        for cp in remote:
            cp.wait()
        for cp in local:
            cp.wait()

    return pl.pallas_call(
        body, name=name,
        out_shape=[jax.ShapeDtypeStruct(a.shape, a.dtype) for a in arrs],
        in_specs=[ANY] * na, out_specs=[ANY] * na,
        scratch_shapes=[pltpu.SemaphoreType.DMA((na, 3)), pltpu.SemaphoreType.DMA((na, 3)),
                        pltpu.SemaphoreType.DMA((na,))],
    )(*arrs)


def _rows_tile(r):
    for cand in (512, 256, 128, 64, 32, 16, 8):
        if r % cand == 0:
            return cand
    return r


def _add2(a, b, name):
    s, r, n = a.shape
    tr = _rows_tile(r)

    def body(a_ref, b_ref, o_ref):
        o_ref[...] = a_ref[...] + b_ref[...]

    spec = pl.BlockSpec((1, tr, n), lambda i, j: (i, j, 0))
    return pl.pallas_call(
        body, name=name, grid=(s, r // tr), in_specs=[spec, spec], out_specs=spec,
        out_shape=jax.ShapeDtypeStruct(a.shape, a.dtype),
        compiler_params=_cparams(("parallel", "parallel")),
    )(a, b)


def _sum_slabs(a, name):
    s, r, n = a.shape
    tr = _rows_tile(r)

    def body(a_ref, o_ref):
        acc = a_ref[0]
        for k in range(1, s):
            acc = acc + a_ref[k]
        o_ref[...] = acc

    return pl.pallas_call(
        body, name=name, grid=(r // tr,),
        in_specs=[pl.BlockSpec((s, tr, n), lambda i: (0, i, 0))],
        out_specs=pl.BlockSpec((tr, n), lambda i: (i, 0)),
        out_shape=jax.ShapeDtypeStruct((r, n), a.dtype),
        compiler_params=_cparams(("parallel",)),
    )(a)


def _adamw(w, g, m, v, name):
    r, n = w.shape
    tr = _rows_tile(r)
    c1 = 1.0 / (1.0 - ADAM_B1 ** ADAM_STEP)
    c2 = 1.0 / (1.0 - ADAM_B2 ** ADAM_STEP)

    def body(w_ref, g_ref, m_ref, v_ref, d_ref, nm_ref, nv_ref):
        gv = g_ref[...]
        nm = ADAM_B1 * m_ref[...] + (1.0 - ADAM_B1) * gv
        nv = ADAM_B2 * v_ref[...] + (1.0 - ADAM_B2) * (gv * gv)
        nm_ref[...] = nm
        nv_ref[...] = nv
        d_ref[...] = -ADAM_LR * ((nm * c1) / (jnp.sqrt(nv * c2) + ADAM_EPS) + ADAM_WD * w_ref[...])

    spec = pl.BlockSpec((tr, n), lambda i: (i, 0))
    o = jax.ShapeDtypeStruct((r, n), F32)
    return pl.pallas_call(
        body, name=name, grid=(r // tr,), in_specs=[spec] * 4, out_specs=[spec] * 3, out_shape=[o, o, o],
        compiler_params=_cparams(("parallel",)),
    )(w, g, m, v)


def _mod_fwd(c_all, w_ada_s, b_ada_s):
    def body(c_ref, w_ref, b_ref, o_ref):
        o_ref[...] = jnp.dot(_silu(c_ref[...]), w_ref[...], precision=HI, preferred_element_type=F32) + b_ref[...]

    return pl.pallas_call(
        body, name="mod_fwd", out_shape=jax.ShapeDtypeStruct((c_all.shape[0], w_ada_s.shape[1]), F32),
        in_specs=[WHOLE_VMEM] * 3, out_specs=WHOLE_VMEM,
        compiler_params=pltpu.CompilerParams(vmem_limit_bytes=VMEM_LIMIT),
    )(c_all, w_ada_s, b_ada_s)


def _wada_grad(c_all, dmod_s):
    def body(c_ref, d_ref, o_ref):
        o_ref[...] = lax.dot_general(_silu(c_ref[...]), d_ref[...], (((0,), (0,)), ((), ())),
                                     precision=HI, preferred_element_type=F32)

    return pl.pallas_call(
        body, name="w_ada_grad", out_shape=jax.ShapeDtypeStruct((c_all.shape[1], dmod_s.shape[1]), F32),
        in_specs=[WHOLE_VMEM] * 2, out_specs=WHOLE_VMEM,
        compiler_params=pltpu.CompilerParams(vmem_limit_bytes=VMEM_LIMIT),
    )(c_all, dmod_s)


SMALL_ROWS = 24


def _pad_rows(v, nrows):
    v = v.reshape(-1)
    return jnp.pad(v, (0, nrows * 1024 - v.shape[0])).reshape(nrows, 1024)


def _pack_small(b_ada, norm_w, b_in, conv_w_full, conv_b, rpb, ml_norm_w, final_norm_w, last):
    parts = [_pad_rows(b_ada, 3), _pad_rows(norm_w, 1), _pad_rows(b_in, 5), _pad_rows(conv_w_full, 5),
             _pad_rows(conv_b, 1), _pad_rows(rpb, 4), _pad_rows(ml_norm_w, 1), _pad_rows(final_norm_w, 1),
             _pad_rows(last, 3)]
    return jnp.concatenate(parts, axis=0)


def _unpack_small(p):
    return dict(b_ada=p[0:3].reshape(1, 3072), norm_w=p[3:4], b_in=p[4:9].reshape(-1)[:IN_W].reshape(1, IN_W),
                conv_w=p[9:14], conv_b=p[14:15],
                rpb=p[15:19].reshape(-1)[:NA_HEADS * 15 * 31].reshape(1, NA_HEADS, 15, 31),
                ml_norm_w=p[19:20, :ML_W], final_norm_w=p[20], last=p[21])


def kernel(x, c, w_ada, b_ada, norm_w, w_in, b_in, conv_w, conv_b, rpb, ml_norm_w, w_out, final_norm_w, loss_target, m_w_ada, m_b_ada, m_norm_w, m_w_in, m_b_in, m_conv_w, m_conv_b, m_rpb, m_ml_norm_w, m_w_out, m_final_norm_w, v_w_ada, v_b_ada, v_norm_w, v_w_in, v_b_in, v_conv_w, v_conv_b, v_rpb, v_ml_norm_w, v_w_out, v_final_norm_w):
    xi, yi, ci = lax.axis_index("x"), lax.axis_index("y"), lax.axis_index("c")
    chip = 2 * xi + yi
    dev = 2 * chip + ci
    t = x.shape[1]
    ada_n = w_ada.shape[2]
    in_n = w_in.shape[2]
    out_r = w_out.shape[1]

    c_blk = jnp.pad(c, ((0, 7), (0, 0)))
    w_in_half = lax.dynamic_slice_in_dim(w_in[0], ci * 512, 512, axis=0).astype(BF16)
    w_out_half = lax.dynamic_slice_in_dim(w_out[0], ci * (out_r // 2), out_r // 2, axis=0).astype(BF16)
    conv_blk = jnp.pad(conv_w[0], ((0, 3), (0, 0)))
    c_g, conv_g, w_in_g, w_out_g = _allgather8([c_blk, conv_blk, w_in_half, w_out_half], "gather_c_weights")
    c_all = c_g.reshape(N_DEV, 8, D_MODEL)[:, 0]
    b_ada_s = lax.dynamic_slice_in_dim(b_ada, chip * ada_n, ada_n, axis=1)
    mod_s = _mod_fwd(c_all, w_ada[0], b_ada_s)
    (mod_g,) = _allgather8([mod_s], "gather_mod")
    mod_mine = lax.dynamic_index_in_dim(mod_g.reshape(N_DEV, 8, ada_n), dev, axis=1, keepdims=False)
    mod = mod_mine[0::2].reshape(1, 3 * D_MODEL)
    shift, scale, gate = mod[:, :D_MODEL], mod[:, D_MODEL:2 * D_MODEL], mod[:, 2 * D_MODEL:]

    w_in_full = w_in_g.reshape(4, D_MODEL, in_n).transpose(1, 0, 2).reshape(D_MODEL, 4 * in_n)
    w_in_p = jnp.pad(w_in_full, ((0, 0), (0, IN_PAD - IN_W)))
    b_in_p = jnp.pad(b_in, ((0, 0), (0, IN_PAD - IN_W)))
    conv_w8 = conv_g.reshape(4, 2, 8, conv_w.shape[2])[:, 0].transpose(1, 0, 2).reshape(8, D_MODEL)

    (loss, grad_x, dmod, g_nw, g_w_in, g_b_in, g_conv_w, g_conv_b, g_rpb, g_mlnw, g_w_out, g_fnw) = _local_step(
        x[0], loss_target[0], shift, scale, gate, norm_w, w_in_p, b_in_p, conv_w8, conv_b, rpb[0],
        ml_norm_w, w_out_g, final_norm_w.reshape(1, D_MODEL))

    gi4 = g_w_in[:, :IN_W].reshape(2, 512, 4, in_n).transpose(0, 2, 1, 3)
    go4 = g_w_out.reshape(4, 2, out_r // 2, D_MODEL).transpose(1, 0, 2, 3)
    pick = lambda a, k: lax.dynamic_index_in_dim(a, k, axis=0, keepdims=False)
    ri, ro = _pair_exchange([pick(gi4, 1 - ci), pick(go4, 1 - ci)], "rs_pair")
    pi = _add2(pick(gi4, ci), ri, "rs_pair_add_in")
    po = _add2(pick(go4, ci), ro, "rs_pair_add_out")
    qi, qo = _chip_exchange([pi, po], "rs_chips")
    si = _sum_slabs(qi, "rs_sum_in")
    so = _sum_slabs(qo, "rs_sum_out")
    ti, to = _pair_exchange([si, so], "rs_share")
    g_w_in_s = jnp.where(ci == 0, jnp.concatenate([si, ti], axis=0), jnp.concatenate([ti, si], axis=0))
    g_w_out_s = jnp.where(ci == 0, jnp.concatenate([so, to], axis=0), jnp.concatenate([to, so], axis=0))

    small = _pack_small(dmod, g_nw, g_b_in[:, :IN_W], g_conv_w[:CONV_W], g_conv_b, g_rpb, g_mlnw, g_fnw,
                        jnp.pad(loss, ((0, 0), (0, 1024 - 128))))
    (small_g,) = _allgather8([small], "gather_small")
    small_g = small_g.reshape(N_DEV, SMALL_ROWS, 1024)
    small_sum = _sum_slabs(small_g, "small_sum")
    gs = _unpack_small(small_sum)
    dmod_all = small_g[:, 0:3].reshape(N_DEV, 3 * D_MODEL)
    g_w_ada_s = _wada_grad(c_all, lax.dynamic_slice_in_dim(dmod_all, chip * ada_n, ada_n, axis=1))
    g_conv_w_s = lax.dynamic_slice_in_dim(gs['conv_w'], chip * conv_w.shape[2], conv_w.shape[2], axis=1)
    loss_total = gs['last'][0]

    zeros3 = jnp.zeros((3, 1024), F32)
    zc = jnp.zeros((CONV_W, D_MODEL), F32)
    pw = _pack_small(b_ada, norm_w, b_in, zc, conv_b, rpb, ml_norm_w, final_norm_w, zeros3)
    pm = _pack_small(m_b_ada, m_norm_w, m_b_in, zc, m_conv_b, m_rpb, m_ml_norm_w, m_final_norm_w, zeros3)
    pv = _pack_small(v_b_ada, v_norm_w, v_b_in, zc, v_conv_b, v_rpb, v_ml_norm_w, v_final_norm_w, zeros3)
    ds_, nms, nvs = [_unpack_small(a) for a in _adamw(pw, small_sum, pm, pv, "adamw_small")]
    d_ada, nm_ada, nv_ada = _adamw(w_ada[0], g_w_ada_s, m_w_ada[0], v_w_ada[0], "adamw_w_ada")
    d_in, nm_in, nv_in = _adamw(w_in[0], g_w_in_s, m_w_in[0], v_w_in[0], "adamw_w_in")
    d_out, nm_out, nv_out = _adamw(w_out[0], g_w_out_s, m_w_out[0], v_w_out[0], "adamw_w_out")
    d_cw, nm_cw, nv_cw = _adamw(conv_w[0], g_conv_w_s, m_conv_w[0], v_conv_w[0], "adamw_conv_w")

    def group(big_ada, big_in, big_out, cw, sm):
        return (big_ada[None], sm['b_ada'], sm['norm_w'], big_in[None], sm['b_in'], cw[None], sm['conv_b'],
                sm['rpb'], sm['ml_norm_w'], big_out[None], sm['final_norm_w'])

    return ((loss_total, grad_x[None])
            + group(g_w_ada_s, g_w_in_s, g_w_out_s, g_conv_w_s, gs)
            + group(d_ada, d_in, d_out, d_cw, ds_)
            + group(nm_ada, nm_in, nm_out, nm_cw, nms)
            + group(nv_ada, nv_in, nv_out, nv_cw, nvs))
```

```python
import functools

import numpy as np
import jax
import jax.numpy as jnp
from jax import lax
from jax.experimental import pallas as pl
from jax.experimental.pallas import tpu as pltpu

F32 = jnp.float32
BF16 = jnp.bfloat16
HI = lax.Precision.HIGHEST

D_MODEL = 1024
GRID_W = 64
NA_W = 512
NA_HEAD_DIM = 64
NA_HEADS = 8
NA_KH = 8
NA_KW = 16
ML_W = 512
ML_HEADS = 4
ML_HEAD_DIM = 128
ML_CHUNK = 128
CONV_W = 5
EPS = 1e-6
IN_W = 4 * NA_W + 5 * ML_W + 4 * ML_HEADS
IN_MAIN = 4 * NA_W + 5 * ML_W
IN_PAD = IN_MAIN + 128
NEG = -1e30

ADAM_LR = 0.001
ADAM_B1 = 0.9
ADAM_B2 = 0.999
ADAM_EPS = 1e-08
ADAM_WD = 0.01
ADAM_STEP = 10

NA_QROWS = 8
NA_KROWS = 16
NA_QT = NA_QROWS * GRID_W
NA_KT = NA_KROWS * GRID_W
NA_KCH = 256
ML_NB = 8
ML_TB = ML_NB * ML_CHUNK

VMEM_LIMIT = 56 * 1024 * 1024


def _cparams(sem, vmem=VMEM_LIMIT):
    return pltpu.CompilerParams(dimension_semantics=sem, vmem_limit_bytes=vmem)


def _silu(x):
    return x * jax.nn.sigmoid(x)


def _dsilu(x):
    s = jax.nn.sigmoid(x)
    return s * (1.0 + x * (1.0 - s))


def _dot(a, b, dims):
    return lax.dot_general(a, b, (dims, ((), ())), preferred_element_type=F32)


def _nn(a, b):
    return _dot(a, b, ((1,), (0,)))


def _nt(a, b):
    return _dot(a, b, ((1,), (1,)))


def _tn(a, b):
    return _dot(a, b, ((0,), (0,)))


def _mm(a, b, mode, tm, tn, tk, name, bias=None, out_dtype=F32):
    if mode == 'tn':
        kdim, m = a.shape
    else:
        m, kdim = a.shape
    n = b.shape[0] if mode == 'nt' else b.shape[1]
    assert m % tm == 0 and kdim % tk == 0
    nk = kdim // tk
    grid = (m // tm, pl.cdiv(n, tn), nk)
    a_spec = (pl.BlockSpec((tk, tm), lambda i, j, k: (k, i)) if mode == 'tn'
              else pl.BlockSpec((tm, tk), lambda i, j, k: (i, k)))
    b_spec = (pl.BlockSpec((tn, tk), lambda i, j, k: (j, k)) if mode == 'nt'
              else pl.BlockSpec((tk, tn), lambda i, j, k: (k, j)))
    in_specs = [a_spec, b_spec]
    args = [a, b]
    if bias is not None:
        in_specs.append(pl.BlockSpec((1, tn), lambda i, j, k: (0, j)))
        args.append(bias)
    dot = {'nn': _nn, 'nt': _nt, 'tn': _tn}[mode]

    def body(*refs):
        if bias is not None:
            a_ref, b_ref, bias_ref, o_ref, acc_ref = refs
        else:
            a_ref, b_ref, o_ref, acc_ref = refs
        k = pl.program_id(2)

        @pl.when(k == 0)
        def _():
            acc_ref[...] = jnp.zeros_like(acc_ref)

        acc_ref[...] += dot(a_ref[...].astype(BF16), b_ref[...].astype(BF16))

        @pl.when(k == nk - 1)
        def _():
            r = acc_ref[...]
            if bias is not None:
                r = r + bias_ref[...]
            o_ref[...] = r.astype(out_dtype)

    return pl.pallas_call(
        body, name=name, grid=grid, in_specs=in_specs,
        out_specs=pl.BlockSpec((tm, tn), lambda i, j, k: (i, j)),
        out_shape=jax.ShapeDtypeStruct((m, n), out_dtype),
        scratch_shapes=[pltpu.VMEM((tm, tn), F32)],
        compiler_params=_cparams(("parallel", "parallel", "arbitrary")),
    )(*args)


def _row(n):
    return pl.BlockSpec((1, n), lambda i: (0, 0))


def _modulated_norm(xv, nw, sc, sh):
    r = lax.rsqrt(jnp.mean(xv * xv, axis=-1, keepdims=True) + EPS)
    xn = xv * r
    return xn * nw * (1.0 + sc) + sh, xn, r


IN_TN = 1536


def _in_proj(x, norm_w, scale, shift, w_in_p, b_in_p):
    t, d = x.shape
    tm = 1024
    gcol = IN_MAIN // 128

    def body(x_ref, nw_ref, sc_ref, sh_ref, w_ref, b_ref, wg_ref, bg_ref, proj_ref, g_ref, h_scr):
        @pl.when(pl.program_id(1) == 0)
        def _():
            h, _, _ = _modulated_norm(x_ref[...], nw_ref[...], sc_ref[...], sh_ref[...])
            h_scr[...] = h.astype(BF16)
            g_ref[...] = _nn(h_scr[...], wg_ref[...]) + bg_ref[...]
        proj_ref[...] = _nn(h_scr[...], w_ref[...]) + b_ref[...]

    row = lambda n: pl.BlockSpec((1, n), lambda i, j: (0, 0))
    return pl.pallas_call(
        body, name="in_proj", grid=(t // tm, IN_MAIN // IN_TN),
        in_specs=[pl.BlockSpec((tm, d), lambda i, j: (i, 0)), row(d), row(d), row(d),
                  pl.BlockSpec((d, IN_TN), lambda i, j: (0, j)), pl.BlockSpec((1, IN_TN), lambda i, j: (0, j)),
                  pl.BlockSpec((d, 128), lambda i, j: (0, gcol)), pl.BlockSpec((1, 128), lambda i, j: (0, gcol))],
        out_specs=[pl.BlockSpec((tm, IN_TN), lambda i, j: (i, j)), pl.BlockSpec((tm, 128), lambda i, j: (i, 0))],
        out_shape=[jax.ShapeDtypeStruct((t, IN_MAIN), F32), jax.ShapeDtypeStruct((t, 128), F32)],
        scratch_shapes=[pltpu.VMEM((tm, d), BF16)],
        compiler_params=_cparams(("parallel", "arbitrary")),
    )(x, norm_w, scale, shift, w_in_p, b_in_p, w_in_p, b_in_p)


def _ml_norm_parts(hs, o, z, nw):
    outs = []
    for hh in range(ML_HEADS):
        sl = slice(hh * ML_HEAD_DIM, (hh + 1) * ML_HEAD_DIM)
        hm = hs[:, sl] * jax.nn.sigmoid(o[:, sl])
        mu = jnp.mean(hm, axis=-1, keepdims=True)
        cen = hm - mu
        var = jnp.mean(cen * cen, axis=-1, keepdims=True)
        rs = lax.rsqrt(var + EPS)
        outs.append((sl, cen * rs, rs))
    return outs


def _mix_fwd(o_na, proj, h_f, h_b, ml_norm_w, tm):
    t = o_na.shape[0]

    def body(ona_ref, naz_ref, hf_ref, hb_ref, o_ref, z_ref, nw_ref, mix_ref):
        mix_ref[:, 0:NA_W] = (ona_ref[...] * _silu(naz_ref[...])).astype(BF16)
        hs = hf_ref[...] + hb_ref[...]
        z = z_ref[...]
        for sl, xn, _ in _ml_norm_parts(hs, o_ref[...], z, nw_ref[...]):
            mix_ref[:, NA_W + sl.start:NA_W + sl.stop] = (xn * nw_ref[:, sl] * _silu(z[:, sl])).astype(BF16)

    blk = lambda c: pl.BlockSpec((tm, 512), lambda i, c=c: (i, c))
    return pl.pallas_call(
        body, name="mix_fwd", grid=(t // tm,),
        in_specs=[blk(0), blk(3), blk(0), blk(0), blk(7), blk(8), _row(ML_W)],
        out_specs=pl.BlockSpec((tm, 1024), lambda i: (i, 0)),
        out_shape=jax.ShapeDtypeStruct((t, 1024), BF16),
        compiler_params=_cparams(("parallel",)),
    )(o_na, proj, h_f, h_b, proj, proj, ml_norm_w)


def _loss_fb(x, y, gate, target, fnw, tm):
    t, d = x.shape

    def body(x_ref, y_ref, g_ref, tg_ref, w_ref, loss_ref, dres_ref, dy_ref, dgate_ref, gw_ref):
        @pl.when(pl.program_id(0) == 0)
        def _():
            loss_ref[...] = jnp.zeros_like(loss_ref)
            dgate_ref[...] = jnp.zeros_like(dgate_ref)
            gw_ref[...] = jnp.zeros_like(gw_ref)
        yv = y_ref[...]
        gate_v = g_ref[...]
        hres = x_ref[...] + gate_v * yv
        r = lax.rsqrt(jnp.mean(hres * hres, axis=-1, keepdims=True) + EPS)
        xn = hres * r
        err = xn * w_ref[...] - tg_ref[...]
        loss_ref[...] += 0.5 * jnp.sum(jnp.sum(err * err, axis=-1, keepdims=True) * (1.0 / d), axis=0, keepdims=True)
        dout = err * (1.0 / d)
        gw_ref[...] += jnp.sum(dout * xn, axis=0, keepdims=True)
        dxn = dout * w_ref[...]
        dres = r * (dxn - xn * jnp.mean(dxn * xn, axis=-1, keepdims=True))
        dres_ref[...] = dres
        dy_ref[...] = (dres * gate_v).astype(BF16)
        dgate_ref[...] += jnp.sum(dres * yv, axis=0, keepdims=True)

    full = pl.BlockSpec((tm, d), lambda i: (i, 0))
    return pl.pallas_call(
        body, name="loss_fb", grid=(t // tm,),
        in_specs=[full, full, _row(d), full, _row(d)],
        out_specs=[pl.BlockSpec((1, 128), lambda i: (0, 0)), full, full, _row(d), _row(d)],
        out_shape=[jax.ShapeDtypeStruct((1, 128), F32), jax.ShapeDtypeStruct((t, d), F32),
                   jax.ShapeDtypeStruct((t, d), BF16), jax.ShapeDtypeStruct((1, d), F32),
                   jax.ShapeDtypeStruct((1, d), F32)],
        compiler_params=_cparams(("arbitrary",)),
    )(x, y, gate, target, fnw)


def _mix_bwd(dmix, o_na, proj, h_f, h_b, ml_norm_w, tm):
    t = o_na.shape[0]

    def body(dna_ref, dml_ref, ona_ref, naz_ref, hf_ref, hb_ref, o_ref, z_ref, nw_ref,
             dona_ref, dnaz_ref, dhs_ref, do_ref, dz_ref, gw_ref):
        @pl.when(pl.program_id(0) == 0)
        def _():
            gw_ref[...] = jnp.zeros_like(gw_ref)
        dna = dna_ref[...]
        naz = naz_ref[...]
        dona_ref[...] = dna * _silu(naz)
        dnaz_ref[...] = dna * ona_ref[...] * _dsilu(naz)
        hs = hf_ref[...] + hb_ref[...]
        z = z_ref[...]
        ov = o_ref[...]
        for sl, xn, rs in _ml_norm_parts(hs, ov, z, nw_ref[...]):
            dyv = dml_ref[:, sl]
            zz = z[:, sl]
            w = nw_ref[:, sl]
            sz = _silu(zz)
            dz_ref[:, sl] = dyv * xn * w * _dsilu(zz)
            gw_ref[:, sl] += jnp.sum(dyv * xn * sz, axis=0, keepdims=True)
            dxn = dyv * w * sz
            dhm = rs * (dxn - jnp.mean(dxn, axis=-1, keepdims=True)
                        - xn * jnp.mean(dxn * xn, axis=-1, keepdims=True))
            so = jax.nn.sigmoid(ov[:, sl])
            dhs_ref[:, sl] = dhm * so
            do_ref[:, sl] = dhm * hs[:, sl] * so * (1.0 - so)

    blk = lambda c: pl.BlockSpec((tm, 512), lambda i, c=c: (i, c))
    o512 = jax.ShapeDtypeStruct((t, 512), F32)
    return pl.pallas_call(
        body, name="mix_bwd", grid=(t // tm,),
        in_specs=[blk(0), blk(1), blk(0), blk(3), blk(0), blk(0), blk(7), blk(8), _row(ML_W)],
        out_specs=[blk(0)] * 5 + [_row(ML_W)],
        out_shape=[o512] * 5 + [jax.ShapeDtypeStruct((1, ML_W), F32)],
        compiler_params=_cparams(("arbitrary",)),
    )(dmix, dmix, o_na, proj, h_f, h_b, proj, proj, ml_norm_w)


def _in_bwd(pieces, x, dres, w_in_p, norm_w, scale, shift):
    t, d = x.shape
    tm = 256
    nt = t // tm
    widths = [p.shape[1] for p in pieces]
    offs = [sum(widths[:k]) for k in range(len(widths))]
    assert sum(widths) == IN_PAD
    npc = len(pieces)

    def body(*refs):
        p_refs = refs[:npc]
        (x_ref, dres_ref, w_hbm, nw_ref, sc_ref, sh_ref,
         gx_ref, gw_hbm, gb_ref, dsc_ref, dsh_ref, gnw_ref, w_vmem, acc, sem) = refs[npc:]
        i = pl.program_id(0)

        @pl.when(i == 0)
        def _():
            cp = pltpu.make_async_copy(w_hbm, w_vmem, sem.at[0])
            cp.start()
            acc[...] = jnp.zeros_like(acc)
            gb_ref[...] = jnp.zeros_like(gb_ref)
            dsc_ref[...] = jnp.zeros_like(dsc_ref)
            dsh_ref[...] = jnp.zeros_like(dsh_ref)
            gnw_ref[...] = jnp.zeros_like(gnw_ref)
            cp.wait()

        nw = nw_ref[...]
        s1 = 1.0 + sc_ref[...]
        h, xn, r = _modulated_norm(x_ref[...], nw, sc_ref[...], sh_ref[...])
        hb = h.astype(BF16)
        dhv = jnp.zeros((tm, d), F32)
        for p_ref, c0, w in zip(p_refs, offs, widths):
            pt = p_ref[...]
            pb = pt.astype(BF16)
            dhv = dhv + _nt(pb, w_vmem[:, c0:c0 + w])
            acc[:, c0:c0 + w] += _tn(hb, pb)
            gb_ref[:, c0:c0 + w] += jnp.sum(pt, axis=0, keepdims=True)
        dsh_ref[...] += jnp.sum(dhv, axis=0, keepdims=True)
        dsc_ref[...] += jnp.sum(dhv * xn * nw, axis=0, keepdims=True)
        gnw_ref[...] += jnp.sum(dhv * xn * s1, axis=0, keepdims=True)
        dxn = dhv * nw * s1
        gx_ref[...] = dres_ref[...] + r * (dxn - xn * jnp.mean(dxn * xn, axis=-1, keepdims=True))

        @pl.when(i == nt - 1)
        def _():
            cp = pltpu.make_async_copy(acc, gw_hbm, sem.at[1])
            cp.start()
            cp.wait()

    full = pl.BlockSpec((tm, d), lambda i: (i, 0))
    return pl.pallas_call(
        body, name="in_bwd", grid=(nt,),
        in_specs=[pl.BlockSpec((tm, w), lambda i: (i, 0)) for w in widths]
        + [full, full, pl.BlockSpec(memory_space=pl.ANY), _row(d), _row(d), _row(d)],
        out_specs=[full, pl.BlockSpec(memory_space=pl.ANY), _row(IN_PAD), _row(d), _row(d), _row(d)],
        out_shape=[jax.ShapeDtypeStruct((t, d), F32), jax.ShapeDtypeStruct((d, IN_PAD), F32),
                   jax.ShapeDtypeStruct((1, IN_PAD), F32)] + [jax.ShapeDtypeStruct((1, d), F32)] * 3,
        scratch_shapes=[pltpu.VMEM((d, IN_PAD), BF16), pltpu.VMEM((d, IN_PAD), F32),
                        pltpu.SemaphoreType.DMA((2,))],
        compiler_params=_cparams(("arbitrary",)),
    )(*pieces, x, dres, w_in_p, norm_w, scale, shift)


def _na_static(rows):
    cases = [(0, 0), (NA_QROWS, NA_QROWS - 4), (rows - NA_QROWS, rows - NA_KROWS)]
    dy = np.zeros((3, NA_QROWS, NA_KROWS), np.int32)
    rv = np.zeros((3, NA_QROWS, NA_KROWS), bool)
    for cs, (r0, kr0) in enumerate(cases):
        for i in range(NA_QROWS):
            for j in range(NA_KROWS):
                r, kr = r0 + i, kr0 + j
                rs = min(max(r - NA_KH // 2, 0), rows - NA_KH)
                rv[cs, i, j] = rs <= kr <= rs + NA_KH - 1
                dy[cs, i, j] = min(max(kr - r + NA_KH - 1, 0), 2 * NA_KH - 2)
    cq = np.arange(GRID_W)[:, None]
    ck = np.arange(GRID_W)[None, :]
    cs0 = np.clip(cq - NA_KW // 2, 0, GRID_W - NA_KW)
    cv = (ck >= cs0) & (ck < cs0 + NA_KW)
    dx = np.clip(ck - cq, -(NA_KW - 1), NA_KW - 1) + NA_KW - 1
    return dy, rv, dx.astype(np.int32), cv


def _na_bias_table(rpb, rows):
    dy, rv, dx, cv = _na_static(rows)
    rpx = jnp.where(cv[None, None], rpb[:, :, dx], NEG)
    tbl = rpx[:, dy.reshape(-1)].reshape(NA_HEADS, 3, NA_QROWS, NA_KROWS, GRID_W, GRID_W)
    tbl = jnp.where(rv[None, :, :, :, None, None], tbl, NEG)
    return tbl.transpose(0, 1, 2, 4, 3, 5).reshape(NA_HEADS, 3, NA_QT, NA_KT)


def _na_specs(t):
    nb = t // NA_QT
    nkb = t // NA_KCH
    npieces = NA_KT // NA_KCH

    def kb0(b):
        return jnp.clip(2 * b - 1, 0, nkb - npieces)

    def case(b):
        return jnp.where(b == 0, 0, jnp.where(b == nb - 1, 2, 1))

    q_spec = pl.BlockSpec((NA_QT, 128), lambda p, b: (b, p))
    k_specs = [pl.BlockSpec((NA_KCH, 128), lambda p, b, i=i: (kb0(b) + i, 4 + p)) for i in range(npieces)]
    v_specs = [pl.BlockSpec((NA_KCH, 128), lambda p, b, i=i: (kb0(b) + i, 8 + p)) for i in range(npieces)]
    tbl_spec = pl.BlockSpec((2, 1, NA_QT, NA_KT), lambda p, b: (p, case(b), 0, 0))
    io_spec = pl.BlockSpec((NA_QT, 128), lambda p, b: (b, p))
    return nb, npieces, kb0, case, q_spec, k_specs, v_specs, tbl_spec, io_spec


def _na_probs(qh, ks, tbl_ref, hh, npieces):
    s = [_nt(qh, ks[i]) + tbl_ref[hh, 0, :, i * NA_KCH:(i + 1) * NA_KCH] for i in range(npieces)]
    m = functools.reduce(jnp.maximum, [jnp.max(si, axis=1, keepdims=True) for si in s])
    p = [jnp.exp(si - m) for si in s]
    l = functools.reduce(jnp.add, [jnp.sum(pi, axis=1, keepdims=True) for pi in p])
    return p, l


def _na_fwd(proj, tbl):
    t = proj.shape[0]
    nb, npieces, _, _, q_spec, k_specs, v_specs, tbl_spec, io_spec = _na_specs(t)

    def body(*refs):
        q_ref = refs[0]
        k_refs = refs[1:1 + npieces]
        v_refs = refs[1 + npieces:1 + 2 * npieces]
        tbl_ref, o_ref = refs[1 + 2 * npieces:]
        lane = lax.broadcasted_iota(jnp.int32, (1, 128), 1)
        qv = q_ref[...] * (NA_HEAD_DIM ** -0.5)
        ks = [r[...].astype(BF16) for r in k_refs]
        vs = [r[...].astype(BF16) for r in v_refs]
        out = jnp.zeros((NA_QT, 128), F32)
        for hh in range(2):
            msk = (lane // NA_HEAD_DIM) == hh
            qh = jnp.where(msk, qv, 0.0).astype(BF16)
            p, l = _na_probs(qh, ks, tbl_ref, hh, npieces)
            o = functools.reduce(jnp.add, [_nn(p[i].astype(BF16), vs[i]) for i in range(npieces)])
            out = jnp.where(msk, o / l, out)
        o_ref[...] = out

    return pl.pallas_call(
        body, name="na_fwd", grid=(4, nb),
        in_specs=[q_spec] + k_specs + v_specs + [tbl_spec],
        out_specs=io_spec,
        out_shape=jax.ShapeDtypeStruct((t, NA_W), F32),
        compiler_params=_cparams(("parallel", "arbitrary")),
    )(*([proj] * (1 + 2 * npieces)), tbl)


def _na_bwd(proj, tbl, d_o, o_na):
    t = proj.shape[0]
    nb, npieces, kb0, case, q_spec, k_specs, v_specs, tbl_spec, io_spec = _na_specs(t)

    def body(*refs):
        q_ref = refs[0]
        k_refs = refs[1:1 + npieces]
        v_refs = refs[1 + npieces:1 + 2 * npieces]
        (tbl_ref, do_ref, o_ref, dq_ref, dk_hbm, dv_hbm, rpb_ref,
         dk_acc, dv_acc, ds_scr, sem) = refs[1 + 2 * npieces:]
        p_id = pl.program_id(0)
        b = pl.program_id(1)

        @pl.when(b == 0)
        def _():
            dk_acc[...] = jnp.zeros_like(dk_acc)
            dv_acc[...] = jnp.zeros_like(dv_acc)

        @pl.when((b == 0) | (b == 1) | (b == nb - 1))
        def _():
            rpb_ref[...] = jnp.zeros_like(rpb_ref)

        lane = lax.broadcasted_iota(jnp.int32, (1, 128), 1)
        scale = NA_HEAD_DIM ** -0.5
        qv = q_ref[...] * scale
        ks = [r[...].astype(BF16) for r in k_refs]
        vs = [r[...].astype(BF16) for r in v_refs]
        dov = do_ref[...]
        ov = o_ref[...]
        tok0 = kb0(b) * NA_KCH
        dq = jnp.zeros((NA_QT, 128), F32)
        for hh in range(2):
            msk = (lane // NA_HEAD_DIM) == hh
            qh = jnp.where(msk, qv, 0.0).astype(BF16)
            p, l = _na_probs(qh, ks, tbl_ref, hh, npieces)
            rl = 1.0 / l
            doh = jnp.where(msk, dov, 0.0)
            dd = jnp.sum(doh * ov, axis=1, keepdims=True)
            dohb = doh.astype(BF16)
            dqh = jnp.zeros((NA_QT, 128), F32)
            for i in range(npieces):
                pn = p[i] * rl
                ds = pn * (_nt(dohb, vs[i]) - dd)
                dsb = ds.astype(BF16)
                dqh = dqh + _nn(dsb, ks[i])
                rows = pl.ds(pl.multiple_of(tok0 + i * NA_KCH, NA_KCH), NA_KCH)
                dk_acc[rows, :] += _tn(dsb, qh)
                dv_acc[rows, :] += _tn(pn.astype(BF16), dohb)
                ds_scr[:, i * NA_KCH:(i + 1) * NA_KCH] = ds
            dq = jnp.where(msk, dqh * scale, dq)
            acc = ds_scr[0:GRID_W, :]
            for i in range(1, NA_QROWS):
                acc = acc + pltpu.roll(ds_scr[i * GRID_W:(i + 1) * GRID_W, :], NA_KT - i * GRID_W, 1)
            rpb_ref[0, 0, hh] += acc
        dq_ref[...] = dq

        @pl.when(b == nb - 1)
        def _():
            cols = pl.ds(pl.multiple_of(p_id * 128, 128), 128)
            ck = pltpu.make_async_copy(dk_acc, dk_hbm.at[:, cols], sem.at[0])
            cv = pltpu.make_async_copy(dv_acc, dv_hbm.at[:, cols], sem.at[1])
            ck.start()
            cv.start()
            ck.wait()
            cv.wait()

    o512 = jax.ShapeDtypeStruct((t, NA_W), F32)
    return pl.pallas_call(
        body, name="na_bwd", grid=(4, nb),
        in_specs=[q_spec] + k_specs + v_specs + [tbl_spec, io_spec, io_spec],
        out_specs=[io_spec, pl.BlockSpec(memory_space=pl.ANY), pl.BlockSpec(memory_space=pl.ANY),
                   pl.BlockSpec((1, 1, 2, GRID_W, NA_KT), lambda p, b: (p, case(b), 0, 0, 0))],
        out_shape=[o512, o512, o512, jax.ShapeDtypeStruct((4, 3, 2, GRID_W, NA_KT), F32)],
        scratch_shapes=[pltpu.VMEM((t, 128), F32), pltpu.VMEM((t, 128), F32),
                        pltpu.VMEM((NA_QT, NA_KT), F32), pltpu.SemaphoreType.DMA((2,))],
        compiler_params=_cparams(("arbitrary", "arbitrary")),
    )(*([proj] * (1 + 2 * npieces)), tbl, d_o, o_na)


def _rpb_reduce(rpbacc, rows):
    nacc = 4 * 3 * 2

    def shift_body(a_ref, o_ref):
        acc = a_ref[0, 0:1, :]
        for cq in range(1, GRID_W):
            acc = acc + pltpu.roll(a_ref[0, cq:cq + 1, :], NA_KT - cq, 1)
        o_ref[0] = jnp.broadcast_to(acc, (8, NA_KT))

    vec = pl.pallas_call(
        shift_body, name="rpb_shift", grid=(nacc,),
        in_specs=[pl.BlockSpec((1, GRID_W, NA_KT), lambda a: (a, 0, 0))],
        out_specs=pl.BlockSpec((1, 8, NA_KT), lambda a: (a, 0, 0)),
        out_shape=jax.ShapeDtypeStruct((nacc, 8, NA_KT), F32),
        compiler_params=_cparams(("parallel",)),
    )(rpbacc.reshape(nacc, GRID_W, NA_KT))
    a = vec[:, 0].reshape(4, 3, 2, NA_KT).transpose(0, 2, 1, 3).reshape(NA_HEADS, 3, NA_KT)
    if rows // NA_QROWS < 3:
        a = a.at[:, 1].set(0.0)
    dd = np.arange(NA_KROWS)[:, None]
    dxo = np.arange(-(NA_KW - 1), NA_KW)[None, :]
    idx = ((dd * GRID_W + dxo) % NA_KT).reshape(-1)
    g = a[..., idx].reshape(NA_HEADS, 3 * NA_KROWS, 2 * NA_KW - 1)
    g = jnp.pad(g, ((0, 0), (0, 0), (0, 128 - (2 * NA_KW - 1))))
    nmat = np.zeros((16, 3 * NA_KROWS), np.float32)
    for cs, delta in enumerate((0, -(NA_KH // 2), -(NA_KROWS - NA_QROWS))):
        for d in range(NA_KROWS):
            dy = (d if d <= 11 else d - NA_KROWS) + delta + NA_KH - 1
            if 0 <= dy <= 2 * NA_KH - 2:
                nmat[dy, cs * NA_KROWS + d] = 1.0

    def body(n_ref, g_ref, o_ref):
        o_ref[0] = jnp.dot(n_ref[...], g_ref[0], precision=HI, preferred_element_type=F32)

    out = pl.pallas_call(
        body, name="rpb_reduce", grid=(NA_HEADS,),
        in_specs=[pl.BlockSpec((16, nmat.shape[1]), lambda h: (0, 0)),
                  pl.BlockSpec((1, nmat.shape[1], 128), lambda h: (h, 0, 0))],
        out_specs=pl.BlockSpec((1, 16, 128), lambda h: (h, 0, 0)),
        out_shape=jax.ShapeDtypeStruct((NA_HEADS, 16, 128), F32),
        compiler_params=_cparams(("parallel",)),
    )(jnp.asarray(nmat), g)
    return out[:, :2 * NA_KH - 1, :2 * NA_KW - 1]


def _halo_specs(tm, t, col):
    nt8 = t // 8
    per = tm // 8
    return [pl.BlockSpec((tm, 1024), lambda i: (i, col)),
            pl.BlockSpec((8, 1024), lambda i: (jnp.maximum(i * per - 1, 0), col)),
            pl.BlockSpec((8, 1024), lambda i: (jnp.minimum((i + 1) * per, nt8 - 1), col))]


def _fill_ext(ext, cur_ref, prev_ref, next_ref, tm, nt):
    i = pl.program_id(0)
    ext[0:8, :] = jnp.where(i == 0, 0.0, prev_ref[...])
    ext[8:8 + tm, :] = cur_ref[...]
    ext[8 + tm:16 + tm, :] = jnp.where(i == nt - 1, 0.0, next_ref[...])


def _conv_fwd(proj, conv_w8, conv_b, tm):
    t = proj.shape[0]
    nt = t // tm

    def body(u_ref, up_ref, un_ref, w_ref, b_ref, pre_ref, act_ref, ext):
        _fill_ext(ext, u_ref, up_ref, un_ref, tm, nt)
        pre = b_ref[...] + w_ref[0:1, :] * ext[pl.ds(6, tm), :]
        for j in range(1, CONV_W):
            pre = pre + w_ref[j:j + 1, :] * ext[pl.ds(6 + j, tm), :]
        pre_ref[...] = pre
        act_ref[...] = _silu(pre)

    full = pl.BlockSpec((tm, 1024), lambda i: (i, 0))
    o = jax.ShapeDtypeStruct((t, 1024), F32)
    return pl.pallas_call(
        body, name="conv_fwd", grid=(nt,),
        in_specs=_halo_specs(tm, t, 2) + [pl.BlockSpec((8, 1024), lambda i: (0, 0)), _row(1024)],
        out_specs=[full, full], out_shape=[o, o],
        scratch_shapes=[pltpu.VMEM((tm + 16, 1024), F32)],
        compiler_params=_cparams(("parallel",)),
    )(proj, proj, proj, conv_w8, conv_b)


def _conv_bwd1(dq_f, dq_b, dk_f, dk_b, dv_f, dv_b, pre, tm):
    t = pre.shape[0]

    def body(dqf, dqb, dkf, dkb, dvf, dvb, pre_ref, dpre_ref, dv_ref, gb_ref):
        @pl.when(pl.program_id(0) == 0)
        def _():
            gb_ref[...] = jnp.zeros_like(gb_ref)
        ds = _dsilu(pre_ref[...])
        dpre_ref[:, 0:ML_W] = (dqf[...] + dqb[...]) * ds[:, 0:ML_W]
        dpre_ref[:, ML_W:] = (dkf[...] + dkb[...]) * ds[:, ML_W:]
        dv_ref[...] = dvf[...] + dvb[...]
        gb_ref[...] += jnp.sum(dpre_ref[...], axis=0, keepdims=True)

    half = pl.BlockSpec((tm, 512), lambda i: (i, 0))
    full = pl.BlockSpec((tm, 1024), lambda i: (i, 0))
    return pl.pallas_call(
        body, name="conv_bwd1", grid=(t // tm,),
        in_specs=[half] * 6 + [full],
        out_specs=[full, half, _row(1024)],
        out_shape=[jax.ShapeDtypeStruct((t, 1024), F32), jax.ShapeDtypeStruct((t, 512), F32),
                   jax.ShapeDtypeStruct((1, 1024), F32)],
        compiler_params=_cparams(("arbitrary",)),
    )(dq_f, dq_b, dk_f, dk_b, dv_f, dv_b, pre)


def _conv_bwd2(dpre, proj, conv_w8, tm):
    t = dpre.shape[0]
    nt = t // tm

    def body(d_ref, dp_ref, dn_ref, u_ref, up_ref, un_ref, w_ref, du_ref, gw_ref, extd, extu):
        @pl.when(pl.program_id(0) == 0)
        def _():
            gw_ref[...] = jnp.zeros_like(gw_ref)
        _fill_ext(extd, d_ref, dp_ref, dn_ref, tm, nt)
        _fill_ext(extu, u_ref, up_ref, un_ref, tm, nt)
        dcur = d_ref[...]
        du = w_ref[0:1, :] * extd[pl.ds(10, tm), :]
        for j in range(1, CONV_W):
            du = du + w_ref[j:j + 1, :] * extd[pl.ds(10 - j, tm), :]
        du_ref[...] = du
        for j in range(CONV_W):
            gw_ref[j:j + 1, :] += jnp.sum(dcur * extu[pl.ds(6 + j, tm), :], axis=0, keepdims=True)

    full = pl.BlockSpec((tm, 1024), lambda i: (i, 0))
    return pl.pallas_call(
        body, name="conv_bwd2", grid=(nt,),
        in_specs=_halo_specs(tm, t, 0) + _halo_specs(tm, t, 2) + [pl.BlockSpec((8, 1024), lambda i: (0, 0))],
        out_specs=[full, pl.BlockSpec((8, 1024), lambda i: (0, 0))],
        out_shape=[jax.ShapeDtypeStruct((t, 1024), F32), jax.ShapeDtypeStruct((8, 1024), F32)],
        scratch_shapes=[pltpu.VMEM((tm + 16, 1024), F32), pltpu.VMEM((tm + 16, 1024), F32)],
        compiler_params=_cparams(("arbitrary",)),
    )(dpre, dpre, dpre, proj, proj, proj, conv_w8)


def _ml_consts(rev):
    iu = lax.broadcasted_iota(jnp.int32, (ML_CHUNK, ML_CHUNK), 0)
    js = lax.broadcasted_iota(jnp.int32, (ML_CHUNK, ML_CHUNK), 1)
    eye = iu == js
    le = iu <= js
    ge = iu >= js
    csum, csum_t, causal = (ge, le, le) if rev else (le, ge, ge)
    return eye, csum.astype(F32), csum_t.astype(F32), causal


def _col(row, eye):
    return jnp.sum(jnp.where(eye, row, 0.0), axis=1, keepdims=True)


def _rowof(col, eye):
    return jnp.sum(jnp.where(eye, col, 0.0), axis=0, keepdims=True)


def _ml_gates(gi, gf, m0, csum, rev):
    lf = jax.nn.log_sigmoid(gf)
    b_rows = jnp.dot(lf, csum, precision=HI, preferred_element_type=F32)
    bl = jnp.sum(lf, axis=1, keepdims=True)
    a_rows = bl - b_rows + gi
    mloc = jnp.max(a_rows, axis=1, keepdims=True)
    order = list(range(ML_NB))[::-1] if rev else list(range(ML_NB))
    mp, mn, decay = {}, {}, {}
    m = m0
    for n in order:
        mp[n] = m
        m = jnp.maximum(bl[n:n + 1] + m, mloc[n:n + 1])
        mn[n] = m
        decay[n] = jnp.exp(bl[n:n + 1] + mp[n] - m)
    return b_rows, a_rows, gi - b_rows, mp, mn, decay, order


def _ml_load(q_ref, k_ref, v_ref, n):
    sl = slice(n * ML_CHUNK, (n + 1) * ML_CHUNK)
    qb = q_ref[sl, :].astype(BF16)
    kb = (k_ref[sl, :] * (ML_HEAD_DIM ** -0.5)).astype(BF16)
    vn = v_ref[sl, :]
    return sl, qb, kb, vn


def _ml_state_scan(q_ref, k_ref, v_ref, a_rows, mn, decay, order, c0, n0, eye):
    u, nu, wcol = {}, {}, {}
    for n in range(ML_NB):
        _, _, kb, vn = _ml_load(q_ref, k_ref, v_ref, n)
        wcol[n] = jnp.exp(_col(a_rows[n:n + 1], eye) - mn[n])
        u[n] = _tn((wcol[n] * vn).astype(BF16), kb)
        nu[n] = jnp.sum(wcol[n] * kb.astype(F32), axis=0, keepdims=True)
    cp, npv = {}, {}
    c, nv = c0, n0
    for n in order:
        cp[n], npv[n] = c, nv
        c = decay[n] * c + u[n]
        nv = decay[n] * nv + nu[n]
    return cp, npv, wcol, c, nv


def _ml_intra(qb, kb, vn, b_row, imb_row, mpn, cpn, npn, causal, eye):
    b_col = _col(b_row, eye)
    dlog = jnp.where(causal, b_col + imb_row, NEG)
    m_inter = b_col + mpn
    m_t = jnp.maximum(m_inter, jnp.max(dlog, axis=1, keepdims=True))
    pm = jnp.exp(dlog - m_t)
    s = _nt(qb, kb) * pm
    inter = jnp.exp(m_inter - m_t)
    cq = _nt(qb, cpn.astype(BF16))
    qn = jnp.sum(qb.astype(F32) * npn.astype(BF16).astype(F32), axis=1, keepdims=True)
    num = _nn(s.astype(BF16), vn.astype(BF16)) + inter * cq
    den = jnp.sum(s, axis=1, keepdims=True) + inter * qn
    floor = jnp.exp(-m_t)
    dn = jnp.maximum(jnp.abs(den), floor)
    return dict(pm=pm, s=s, inter=inter, cq=cq, qn=qn, num=num, den=den, floor=floor, dn=dn)


def _ml_specs(t, rev):
    nblk = t // ML_TB
    blk = (lambda g: nblk - 1 - g) if rev else (lambda g: g)
    tile = lambda c0: pl.BlockSpec((ML_TB, 128), lambda hd, g, c0=c0: (blk(g), c0 + hd))
    gate = pl.BlockSpec((1, ML_NB, ML_CHUNK), lambda hd, g: (hd, blk(g), 0))
    cchk = pl.BlockSpec((1, 1, 128, 128), lambda hd, g: (hd, blk(g), 0, 0))
    nmchk = pl.BlockSpec((1, 1, 8, 128), lambda hd, g: (hd, blk(g), 0, 0))
    return nblk, blk, tile, gate, cchk, nmchk


def _ml_fwd(qk_act, proj, gi, gf, rev, name):
    t = qk_act.shape[0]
    nblk, _, tile, gate, cchk, nmchk = _ml_specs(t, rev)

    def body(q_ref, k_ref, v_ref, gi_ref, gf_ref, h_ref, cchk_ref, nmchk_ref, c_ref, nm_ref):
        @pl.when(pl.program_id(1) == 0)
        def _():
            c_ref[...] = jnp.zeros_like(c_ref)
            nm_ref[...] = jnp.zeros_like(nm_ref)
        cchk_ref[0, 0] = c_ref[...]
        nmchk_ref[0, 0] = nm_ref[...]
        eye, csum, _, causal = _ml_consts(rev)
        b_rows, a_rows, imb_rows, mp, mn, decay, order = _ml_gates(
            gi_ref[0], gf_ref[0], nm_ref[1:2, 0:1], csum, rev)
        cp, npv, _, c, nv = _ml_state_scan(q_ref, k_ref, v_ref, a_rows, mn, decay, order,
                                            c_ref[...], nm_ref[0:1, :], eye)
        c_ref[...] = c
        nm_ref[0:1, :] = nv
        nm_ref[1:2, :] = jnp.broadcast_to(mn[order[-1]], (1, 128))
        for n in range(ML_NB):
            sl, qb, kb, vn = _ml_load(q_ref, k_ref, v_ref, n)
            r = _ml_intra(qb, kb, vn, b_rows[n:n + 1], imb_rows[n:n + 1], mp[n], cp[n], npv[n], causal, eye)
            h_ref[sl, :] = r['num'] / r['dn']

    return pl.pallas_call(
        body, name=name, grid=(ML_HEADS, nblk),
        in_specs=[tile(0), tile(4), tile(24), gate, gate],
        out_specs=[tile(0), cchk, nmchk],
        out_shape=[jax.ShapeDtypeStruct((t, ML_W), F32),
                   jax.ShapeDtypeStruct((ML_HEADS, nblk, 128, 128), F32),
                   jax.ShapeDtypeStruct((ML_HEADS, nblk, 8, 128), F32)],
        scratch_shapes=[pltpu.VMEM((128, 128), F32), pltpu.VMEM((8, 128), F32)],
        compiler_params=_cparams(("parallel", "arbitrary")),
    )(qk_act, qk_act, proj, gi, gf)


def _ml_bwd(qk_act, proj, gi, gf, dh, cchk_a, nmchk_a, rev, name):
    t = qk_act.shape[0]
    nblk, _, tile, gate, cchk, nmchk = _ml_specs(t, not rev)

    def body(q_ref, k_ref, v_ref, gi_ref, gf_ref, dh_ref, cchk_ref, nmchk_ref,
             dq_ref, dk_ref, dv_ref, dgi_ref, dgf_ref, dc_ref, dn_ref, db_scr, dbl_scr, di_scr):
        @pl.when(pl.program_id(1) == 0)
        def _():
            dc_ref[...] = jnp.zeros_like(dc_ref)
            dn_ref[...] = jnp.zeros_like(dn_ref)
        eye, csum, csum_t, causal = _ml_consts(rev)
        gfv = gf_ref[0]
        b_rows, a_rows, imb_rows, mp, mn, decay, order = _ml_gates(
            gi_ref[0], gfv, nmchk_ref[0, 0, 1:2, 0:1], csum, rev)
        cp, npv, wcol, _, _ = _ml_state_scan(q_ref, k_ref, v_ref, a_rows, mn, decay, order,
                                             cchk_ref[0, 0], nmchk_ref[0, 0, 0:1, :], eye)
        xs, xns, dbcol, dimb = {}, {}, {}, {}
        for n in range(ML_NB):
            sl, qb, kb, vn = _ml_load(q_ref, k_ref, v_ref, n)
            vb = vn.astype(BF16)
            r = _ml_intra(qb, kb, vn, b_rows[n:n + 1], imb_rows[n:n + 1], mp[n], cp[n], npv[n], causal, eye)
            dhv = dh_ref[sl, :]
            rdn = 1.0 / r['dn']
            dnum = dhv * rdn
            live = jnp.abs(r['den']) > r['floor']
            dden = jnp.where(live, -jnp.sum(dnum * r['num'], axis=1, keepdims=True) * rdn
                             * jnp.sign(r['den']), 0.0)
            dnb = dnum.astype(BF16)
            dsf = _nt(dnb, vb) + dden
            dv_ref[sl, :] = _tn(r['s'].astype(BF16), dnb)
            gb = (dsf * r['pm']).astype(BF16)
            cpb = cp[n].astype(BF16)
            npf = npv[n].astype(BF16).astype(F32)
            idd = r['inter'] * dden
            dq_ref[sl, :] = _nn(gb, kb) + r['inter'] * _nn(dnb, cpb) + idd * npf
            dk_ref[sl, :] = _tn(gb, qb)
            rr = dsf * r['s']
            dinter = jnp.sum(dnum * r['cq'], axis=1, keepdims=True) + dden * r['qn']
            dbcol[n] = jnp.sum(rr, axis=1, keepdims=True) + dinter * r['inter']
            dimb[n] = jnp.sum(rr, axis=0, keepdims=True)
            xs[n] = _tn((r['inter'] * dnum).astype(BF16), qb)
            xns[n] = jnp.sum(idd * qb.astype(F32), axis=0, keepdims=True)
        dcn, dnn = {}, {}
        dc, dn = dc_ref[...], dn_ref[0:1, :]
        for n in order[::-1]:
            dcn[n], dnn[n] = dc, dn
            dc = decay[n] * dc + xs[n]
            dn = decay[n] * dn + xns[n]
        dc_ref[...] = dc
        dn_ref[0:1, :] = dn
        kscale = ML_HEAD_DIM ** -0.5
        for n in range(ML_NB):
            sl, qb, kb, vn = _ml_load(q_ref, k_ref, v_ref, n)
            kf = kb.astype(F32)
            dcb = dcn[n].astype(BF16)
            ddecay = (jnp.sum(jnp.sum(dcn[n] * cp[n], axis=1, keepdims=True), axis=0, keepdims=True)
                      + jnp.sum(dnn[n] * npv[n], axis=1, keepdims=True))
            z = _nn(vn.astype(BF16), dcb)
            dw = jnp.sum((z + dnn[n]) * kf, axis=1, keepdims=True)
            dv_ref[sl, :] += wcol[n] * _nt(kb, dcb)
            dk_ref[sl, :] = (dk_ref[sl, :] + wcol[n] * (z + dnn[n])) * kscale
            da = dw * wcol[n]
            dbl = jnp.sum(da, axis=0, keepdims=True) + ddecay * decay[n]
            da_row = _rowof(da, eye)
            db_scr[n:n + 1, :] = _rowof(dbcol[n] - da, eye) - dimb[n]
            di_scr[n:n + 1, :] = dimb[n] + da_row
            dbl_scr[n:n + 1, :] = jnp.broadcast_to(dbl, (1, ML_CHUNK))
        dlf = jnp.dot(db_scr[...], csum_t, precision=HI, preferred_element_type=F32) + dbl_scr[...]
        dgf_ref[0] = dlf * jax.nn.sigmoid(-gfv)
        dgi_ref[0] = di_scr[...]

    nc = t // ML_CHUNK
    o512 = jax.ShapeDtypeStruct((t, ML_W), F32)
    og = jax.ShapeDtypeStruct((ML_HEADS, nc, ML_CHUNK), F32)
    return pl.pallas_call(
        body, name=name, grid=(ML_HEADS, nblk),
        in_specs=[tile(0), tile(4), tile(24), gate, gate, tile(0), cchk, nmchk],
        out_specs=[tile(0), tile(0), tile(0), gate, gate],
        out_shape=[o512, o512, o512, og, og],
        scratch_shapes=[pltpu.VMEM((128, 128), F32), pltpu.VMEM((8, 128), F32),
                        pltpu.VMEM((ML_NB, ML_CHUNK), F32), pltpu.VMEM((ML_NB, ML_CHUNK), F32),
                        pltpu.VMEM((ML_NB, ML_CHUNK), F32)],
        compiler_params=_cparams(("parallel", "arbitrary")),
    )(qk_act, qk_act, proj, gi, gf, dh, cchk_a, nmchk_a)


def _gate_rows(gates16, t):
    g = gates16.reshape(t // ML_CHUNK, ML_CHUNK, 4, ML_HEADS).transpose(2, 3, 0, 1)
    return g[0], g[1], g[2], g[3]


def _gate_cols(dgi_f, dgf_f, dgi_b, dgf_b, t):
    g = jnp.stack([dgi_f, dgf_f, dgi_b, dgf_b]).transpose(2, 3, 0, 1).reshape(t, 4 * ML_HEADS)
    return jnp.pad(g, ((0, 0), (0, 128 - 4 * ML_HEADS)))


def _local_step(x, target, shift, scale, gate, norm_w, w_in_p, b_in_p, conv_w8, conv_b, rpb,
                ml_norm_w, w_out_b, final_norm_w):
    t = x.shape[0]
    rows = t // GRID_W
    tm = 512
    proj, gates = _in_proj(x, norm_w, scale, shift, w_in_p, b_in_p)
    tbl = _na_bias_table(rpb, rows)
    o_na = _na_fwd(proj, tbl)
    pre, qk_act = _conv_fwd(proj, conv_w8, conv_b, tm)
    gi_f, gf_f, gi_b, gf_b = _gate_rows(gates[:, :4 * ML_HEADS], t)
    h_f, cchk_f, nmchk_f = _ml_fwd(qk_act, proj, gi_f, gf_f, False, "ml_fwd_f")
    h_b, cchk_b, nmchk_b = _ml_fwd(qk_act, proj, gi_b, gf_b, True, "ml_fwd_b")
    mix = _mix_fwd(o_na, proj, h_f, h_b, ml_norm_w, tm)
    y = _mm(mix, w_out_b, 'nn', 1024, 512, 1024, "out_proj")
    loss, dres, dy, dgate, g_fnw = _loss_fb(x, y, gate, target, final_norm_w, tm)
    dmix = _mm(dy, w_out_b, 'nt', 1024, 512, 1024, "d_mix")
    g_w_out = _mm(mix, dy, 'tn', 1024, 512, 1024, "g_w_out")
    d_ona, d_naz, dhs, d_o, d_z, g_mlnw = _mix_bwd(dmix, o_na, proj, h_f, h_b, ml_norm_w, tm)
    dq_na, dk_na, dv_na, rpbacc = _na_bwd(proj, tbl, d_ona, o_na)
    g_rpb = _rpb_reduce(rpbacc, rows)
    dq_f, dk_f, dv_f, dgi_f, dgf_f = _ml_bwd(qk_act, proj, gi_f, gf_f, dhs, cchk_f, nmchk_f, False, "ml_bwd_f")
    dq_b, dk_b, dv_b, dgi_b, dgf_b = _ml_bwd(qk_act, proj, gi_b, gf_b, dhs, cchk_b, nmchk_b, True, "ml_bwd_b")
    dpre, dv_ml, g_conv_b = _conv_bwd1(dq_f, dq_b, dk_f, dk_b, dv_f, dv_b, pre, tm)
    du, g_conv_w = _conv_bwd2(dpre, proj, conv_w8, tm)
    dgates = _gate_cols(dgi_f, dgf_f, dgi_b, dgf_b, t)
    grad_x, g_w_in, g_b_in, dscale, dshift, g_nw = _in_bwd(
        [dq_na, dk_na, dv_na, d_naz, du, dv_ml, d_o, d_z, dgates], x, dres, w_in_p, norm_w, scale, shift)
    dmod = jnp.concatenate([dshift, dscale, dgate], axis=1)
    return (loss, grad_x, dmod, g_nw, g_w_in, g_b_in, g_conv_w, g_conv_b, g_rpb, g_mlnw, g_w_out, g_fnw)


MESH = pl.DeviceIdType.MESH
N_DEV = 8
ANY = pl.BlockSpec(memory_space=pl.ANY)
WHOLE_VMEM = pl.BlockSpec(memory_space=pltpu.VMEM)


def _allgather8(blocks, name):
    na = len(blocks)

    def body(*refs):
        x_refs = refs[:na]
        out_refs = refs[na:2 * na]
        send_sems, recv_sems, local_sems = refs[2 * na:]
        x, y, c = lax.axis_index("x"), lax.axis_index("y"), lax.axis_index("c")
        me, sibling = (x, y, c), (x, y, 1 - c)
        chips = [(1 - x, y), (x, 1 - y), (1 - x, 1 - y)]

        def rows(a, px, py, pc):
            m_per = x_refs[a].shape[0]
            return out_refs[a].at[pl.ds((4 * px + 2 * py + pc) * m_per, m_per), :]

        def copy(a, k, block, to, src=None):
            return pltpu.make_async_remote_copy(
                src_ref=rows(a, *block) if src is None else src, dst_ref=rows(a, *block),
                send_sem=send_sems.at[a, k], recv_sem=recv_sems.at[a, k],
                device_id=to, device_id_type=MESH)

        mine, first, passed = [], [], []
        for a in range(na):
            cp = pltpu.make_async_copy(x_refs[a], rows(a, *me), local_sems.at[a])
            cp.start()
            mine.append(cp)
            first.append(copy(a, 0, me, sibling, src=x_refs[a]))
            first += [copy(a, 1 + j, me, (*chip, c), src=x_refs[a]) for j, chip in enumerate(chips)]
        for cp in first:
            cp.start()
        for a in range(na):
            for j, chip in enumerate(chips):
                copy(a, 1 + j, (*chip, c), me).wait_recv()
                fwd = copy(a, 4 + j, (*chip, c), sibling)
                fwd.start()
                passed.append(fwd)
        for a in range(na):
            copy(a, 0, sibling, me).wait_recv()
            for j, chip in enumerate(chips):
                copy(a, 4 + j, (*chip, 1 - c), me).wait_recv()
        for cp in first + passed:
            cp.wait_send()
        for cp in mine:
            cp.wait()

    return pl.pallas_call(
        body, name=name,
        out_shape=[jax.ShapeDtypeStruct((N_DEV * b.shape[0], b.shape[1]), b.dtype) for b in blocks],
        in_specs=[WHOLE_VMEM] * na, out_specs=[WHOLE_VMEM] * na,
        scratch_shapes=[pltpu.SemaphoreType.DMA((na, 7)), pltpu.SemaphoreType.DMA((na, 7)),
                        pltpu.SemaphoreType.DMA((na,))],
        compiler_params=pltpu.CompilerParams(vmem_limit_bytes=VMEM_LIMIT),
    )(*blocks)


def _pair_exchange(arrs, name):
    na = len(arrs)

    def body(*refs):
        in_refs = refs[:na]
        out_refs = refs[na:2 * na]
        send_sems, recv_sems = refs[2 * na:]
        sibling = (lax.axis_index("x"), lax.axis_index("y"), 1 - lax.axis_index("c"))
        copies = [pltpu.make_async_remote_copy(
            src_ref=in_refs[a], dst_ref=out_refs[a], send_sem=send_sems.at[a], recv_sem=recv_sems.at[a],
            device_id=sibling, device_id_type=MESH) for a in range(na)]
        for cp in copies:
            cp.start()
        for cp in copies:
            cp.wait()

    return pl.pallas_call(
        body, name=name,
        out_shape=[jax.ShapeDtypeStruct(a.shape, a.dtype) for a in arrs],
        in_specs=[ANY] * na, out_specs=[ANY] * na,
        scratch_shapes=[pltpu.SemaphoreType.DMA((na,)), pltpu.SemaphoreType.DMA((na,))],
    )(*arrs)


def _chip_exchange(arrs, name):
    na = len(arrs)

    def body(*refs):
        in_refs = refs[:na]
        out_refs = refs[na:2 * na]
        send_sems, recv_sems, local_sems = refs[2 * na:]
        x, y, c = lax.axis_index("x"), lax.axis_index("y"), lax.axis_index("c")
        my_chip = 2 * x + y
        chips = [(1 - x, y), (x, 1 - y), (1 - x, 1 - y)]
        local, remote = [], []
        for a in range(na):
            cp = pltpu.make_async_copy(in_refs[a].at[my_chip], out_refs[a].at[my_chip], local_sems.at[a])
            cp.start()
            local.append(cp)
            for j, (px, py) in enumerate(chips):
                cp = pltpu.make_async_remote_copy(
                    src_ref=in_refs[a].at[2 * px + py], dst_ref=out_refs[a].at[my_chip],
                    send_sem=send_sems.at[a, j], recv_sem=recv_sems.at[a, j],
                    device_id=(px, py, c), device_id_type=MESH)
                cp.start()
                remote.append(cp)
        for cp in remote:
            cp.wait()
        for cp in local:
            cp.wait()

    return pl.pallas_call(
        body, name=name,
        out_shape=[jax.ShapeDtypeStruct(a.shape, a.dtype) for a in arrs],
        in_specs=[ANY] * na, out_specs=[ANY] * na,
        scratch_shapes=[pltpu.SemaphoreType.DMA((na, 3)), pltpu.SemaphoreType.DMA((na, 3)),
                        pltpu.SemaphoreType.DMA((na,))],
    )(*arrs)


def _rows_tile(r):
    for cand in (512, 256, 128, 64, 32, 16, 8):
        if r % cand == 0:
            return cand
    return r


def _add2(a, b, name):
    s, r, n = a.shape
    tr = _rows_tile(r)

    def body(a_ref, b_ref, o_ref):
        o_ref[...] = a_ref[...] + b_ref[...]

    spec = pl.BlockSpec((1, tr, n), lambda i, j: (i, j, 0))
    return pl.pallas_call(
        body, name=name, grid=(s, r // tr), in_specs=[spec, spec], out_specs=spec,
        out_shape=jax.ShapeDtypeStruct(a.shape, a.dtype),
        compiler_params=_cparams(("parallel", "parallel")),
    )(a, b)


def _sum_slabs(a, name):
    s, r, n = a.shape
    tr = _rows_tile(r)

    def body(a_ref, o_ref):
        acc = a_ref[0]
        for k in range(1, s):
            acc = acc + a_ref[k]
        o_ref[...] = acc

    return pl.pallas_call(
        body, name=name, grid=(r // tr,),
        in_specs=[pl.BlockSpec((s, tr, n), lambda i: (0, i, 0))],
        out_specs=pl.BlockSpec((tr, n), lambda i: (i, 0)),
        out_shape=jax.ShapeDtypeStruct((r, n), a.dtype),
        compiler_params=_cparams(("parallel",)),
    )(a)


def _adamw(w, g, m, v, name):
    r, n = w.shape
    tr = _rows_tile(r)
    c1 = 1.0 / (1.0 - ADAM_B1 ** ADAM_STEP)
    c2 = 1.0 / (1.0 - ADAM_B2 ** ADAM_STEP)

    def body(w_ref, g_ref, m_ref, v_ref, d_ref, nm_ref, nv_ref):
        gv = g_ref[...]
        nm = ADAM_B1 * m_ref[...] + (1.0 - ADAM_B1) * gv
        nv = ADAM_B2 * v_ref[...] + (1.0 - ADAM_B2) * (gv * gv)
        nm_ref[...] = nm
        nv_ref[...] = nv
        d_ref[...] = -ADAM_LR * ((nm * c1) / (jnp.sqrt(nv * c2) + ADAM_EPS) + ADAM_WD * w_ref[...])

    spec = pl.BlockSpec((tr, n), lambda i: (i, 0))
    o = jax.ShapeDtypeStruct((r, n), F32)
    return pl.pallas_call(
        body, name=name, grid=(r // tr,), in_specs=[spec] * 4, out_specs=[spec] * 3, out_shape=[o, o, o],
        compiler_params=_cparams(("parallel",)),
    )(w, g, m, v)


def _mod_fwd(c_all, w_ada_s, b_ada_s):
    def body(c_ref, w_ref, b_ref, o_ref):
        o_ref[...] = jnp.dot(_silu(c_ref[...]), w_ref[...], precision=HI, preferred_element_type=F32) + b_ref[...]

    return pl.pallas_call(
        body, name="mod_fwd", out_shape=jax.ShapeDtypeStruct((c_all.shape[0], w_ada_s.shape[1]), F32),
        in_specs=[WHOLE_VMEM] * 3, out_specs=WHOLE_VMEM,
        compiler_params=pltpu.CompilerParams(vmem_limit_bytes=VMEM_LIMIT),
    )(c_all, w_ada_s, b_ada_s)


def _wada_grad(c_all, dmod_s):
    def body(c_ref, d_ref, o_ref):
        o_ref[...] = lax.dot_general(_silu(c_ref[...]), d_ref[...], (((0,), (0,)), ((), ())),
                                     precision=HI, preferred_element_type=F32)

    return pl.pallas_call(
        body, name="w_ada_grad", out_shape=jax.ShapeDtypeStruct((c_all.shape[1], dmod_s.shape[1]), F32),
        in_specs=[WHOLE_VMEM] * 2, out_specs=WHOLE_VMEM,
        compiler_params=pltpu.CompilerParams(vmem_limit_bytes=VMEM_LIMIT),
    )(c_all, dmod_s)


SMALL_ROWS = 24


def _pad_rows(v, nrows):
    v = v.reshape(-1)
    return jnp.pad(v, (0, nrows * 1024 - v.shape[0])).reshape(nrows, 1024)


def _pack_small(b_ada, norm_w, b_in, conv_w_full, conv_b, rpb, ml_norm_w, final_norm_w, last):
    parts = [_pad_rows(b_ada, 3), _pad_rows(norm_w, 1), _pad_rows(b_in, 5), _pad_rows(conv_w_full, 5),
             _pad_rows(conv_b, 1), _pad_rows(rpb, 4), _pad_rows(ml_norm_w, 1), _pad_rows(final_norm_w, 1),
             _pad_rows(last, 3)]
    return jnp.concatenate(parts, axis=0)


def _unpack_small(p):
    return dict(b_ada=p[0:3].reshape(1, 3072), norm_w=p[3:4], b_in=p[4:9].reshape(-1)[:IN_W].reshape(1, IN_W),
                conv_w=p[9:14], conv_b=p[14:15],
                rpb=p[15:19].reshape(-1)[:NA_HEADS * 15 * 31].reshape(1, NA_HEADS, 15, 31),
                ml_norm_w=p[19:20, :ML_W], final_norm_w=p[20], last=p[21])


def kernel(x, c, w_ada, b_ada, norm_w, w_in, b_in, conv_w, conv_b, rpb, ml_norm_w, w_out, final_norm_w, loss_target, m_w_ada, m_b_ada, m_norm_w, m_w_in, m_b_in, m_conv_w, m_conv_b, m_rpb, m_ml_norm_w, m_w_out, m_final_norm_w, v_w_ada, v_b_ada, v_norm_w, v_w_in, v_b_in, v_conv_w, v_conv_b, v_rpb, v_ml_norm_w, v_w_out, v_final_norm_w):
    xi, yi, ci = lax.axis_index("x"), lax.axis_index("y"), lax.axis_index("c")
    chip = 2 * xi + yi
    dev = 2 * chip + ci
    t = x.shape[1]
    ada_n = w_ada.shape[2]
    in_n = w_in.shape[2]
    out_r = w_out.shape[1]

    c_blk = jnp.pad(c, ((0, 7), (0, 0)))
    w_in_half = lax.dynamic_slice_in_dim(w_in[0], ci * 512, 512, axis=0).astype(BF16)
    w_out_half = lax.dynamic_slice_in_dim(w_out[0], ci * (out_r // 2), out_r // 2, axis=0).astype(BF16)
    conv_blk = jnp.pad(conv_w[0], ((0, 3), (0, 0)))
    c_g, conv_g, w_in_g, w_out_g = _allgather8([c_blk, conv_blk, w_in_half, w_out_half], "gather_c_weights")
    c_all = c_g.reshape(N_DEV, 8, D_MODEL)[:, 0]
    b_ada_s = lax.dynamic_slice_in_dim(b_ada, chip * ada_n, ada_n, axis=1)
    mod_s = _mod_fwd(c_all, w_ada[0], b_ada_s)
    (mod_g,) = _allgather8([mod_s], "gather_mod")
    mod_mine = lax.dynamic_index_in_dim(mod_g.reshape(N_DEV, 8, ada_n), dev, axis=1, keepdims=False)
    mod = mod_mine[0::2].reshape(1, 3 * D_MODEL)
    shift, scale, gate = mod[:, :D_MODEL], mod[:, D_MODEL:2 * D_MODEL], mod[:, 2 * D_MODEL:]

    w_in_full = w_in_g.reshape(4, D_MODEL, in_n).transpose(1, 0, 2).reshape(D_MODEL, 4 * in_n)
    w_in_p = jnp.pad(w_in_full, ((0, 0), (0, IN_PAD - IN_W)))
    b_in_p = jnp.pad(b_in, ((0, 0), (0, IN_PAD - IN_W)))
    conv_w8 = conv_g.reshape(4, 2, 8, conv_w.shape[2])[:, 0].transpose(1, 0, 2).reshape(8, D_MODEL)

    (loss, grad_x, dmod, g_nw, g_w_in, g_b_in, g_conv_w, g_conv_b, g_rpb, g_mlnw, g_w_out, g_fnw) = _local_step(
        x[0], loss_target[0], shift, scale, gate, norm_w, w_in_p, b_in_p, conv_w8, conv_b, rpb[0],
        ml_norm_w, w_out_g, final_norm_w.reshape(1, D_MODEL))

    gi4 = g_w_in[:, :IN_W].reshape(2, 512, 4, in_n).transpose(0, 2, 1, 3)
    go4 = g_w_out.reshape(4, 2, out_r // 2, D_MODEL).transpose(1, 0, 2, 3)
    pick = lambda a, k: lax.dynamic_index_in_dim(a, k, axis=0, keepdims=False)
    ri, ro = _pair_exchange([pick(gi4, 1 - ci), pick(go4, 1 - ci)], "rs_pair")
    pi = _add2(pick(gi4, ci), ri, "rs_pair_add_in")
    po = _add2(pick(go4, ci), ro, "rs_pair_add_out")
    qi, qo = _chip_exchange([pi, po], "rs_chips")
    si = _sum_slabs(qi, "rs_sum_in")
    so = _sum_slabs(qo, "rs_sum_out")
    ti, to = _pair_exchange([si, so], "rs_share")
    g_w_in_s = jnp.where(ci == 0, jnp.concatenate([si, ti], axis=0), jnp.concatenate([ti, si], axis=0))
    g_w_out_s = jnp.where(ci == 0, jnp.concatenate([so, to], axis=0), jnp.concatenate([to, so], axis=0))

    small = _pack_small(dmod, g_nw, g_b_in[:, :IN_W], g_conv_w[:CONV_W], g_conv_b, g_rpb, g_mlnw, g_fnw,
                        jnp.pad(loss, ((0, 0), (0, 1024 - 128))))
    (small_g,) = _allgather8([small], "gather_small")
    small_g = small_g.reshape(N_DEV, SMALL_ROWS, 1024)
    small_sum = _sum_slabs(small_g, "small_sum")
    gs = _unpack_small(small_sum)
    dmod_all = small_g[:, 0:3].reshape(N_DEV, 3 * D_MODEL)
    g_w_ada_s = _wada_grad(c_all, lax.dynamic_slice_in_dim(dmod_all, chip * ada_n, ada_n, axis=1))
    g_conv_w_s = lax.dynamic_slice_in_dim(gs['conv_w'], chip * conv_w.shape[2], conv_w.shape[2], axis=1)
    loss_total = gs['last'][0]

    zeros3 = jnp.zeros((3, 1024), F32)
    zc = jnp.zeros((CONV_W, D_MODEL), F32)
    pw = _pack_small(b_ada, norm_w, b_in, zc, conv_b, rpb, ml_norm_w, final_norm_w, zeros3)
    pm = _pack_small(m_b_ada, m_norm_w, m_b_in, zc, m_conv_b, m_rpb, m_ml_norm_w, m_final_norm_w, zeros3)
    pv = _pack_small(v_b_ada, v_norm_w, v_b_in, zc, v_conv_b, v_rpb, v_ml_norm_w, v_final_norm_w, zeros3)
    ds_, nms, nvs = [_unpack_small(a) for a in _adamw(pw, small_sum, pm, pv, "adamw_small")]
    d_ada, nm_ada, nv_ada = _adamw(w_ada[0], g_w_ada_s, m_w_ada[0], v_w_ada[0], "adamw_w_ada")
    d_in, nm_in, nv_in = _adamw(w_in[0], g_w_in_s, m_w_in[0], v_w_in[0], "adamw_w_in")
    d_out, nm_out, nv_out = _adamw(w_out[0], g_w_out_s, m_w_out[0], v_w_out[0], "adamw_w_out")
    d_cw, nm_cw, nv_cw = _adamw(conv_w[0], g_conv_w_s, m_conv_w[0], v_conv_w[0], "adamw_conv_w")

    def group(big_ada, big_in, big_out, cw, sm):
        return (big_ada[None], sm['b_ada'], sm['norm_w'], big_in[None], sm['b_in'], cw[None], sm['conv_b'],
                sm['rpb'], sm['ml_norm_w'], big_out[None], sm['final_norm_w'])

    return ((loss_total, grad_x[None])
            + group(g_w_ada_s, g_w_in_s, g_w_out_s, g_conv_w_s, gs)
            + group(d_ada, d_in, d_out, d_cw, ds_)
            + group(nm_ada, nm_in, nm_out, nm_cw, nms)
            + group(nv_ada, nv_in, nv_out, nv_cw, nvs))
```

```python
import functools

import numpy as np
import jax
import jax.numpy as jnp
from jax import lax
from jax.experimental import pallas as pl
from jax.experimental.pallas import tpu as pltpu

F32 = jnp.float32
BF16 = jnp.bfloat16
HI = lax.Precision.HIGHEST

D_MODEL = 1024
GRID_W = 64
NA_W = 512
NA_HEAD_DIM = 64
NA_HEADS = 8
NA_KH = 8
NA_KW = 16
ML_W = 512
ML_HEADS = 4
ML_HEAD_DIM = 128
ML_CHUNK = 128
CONV_W = 5
EPS = 1e-6
IN_W = 4 * NA_W + 5 * ML_W + 4 * ML_HEADS
IN_MAIN = 4 * NA_W + 5 * ML_W
IN_PAD = IN_MAIN + 128
NEG = -1e30

ADAM_LR = 0.001
ADAM_B1 = 0.9
ADAM_B2 = 0.999
ADAM_EPS = 1e-08
ADAM_WD = 0.01
ADAM_STEP = 10

NA_QROWS = 8
NA_KROWS = 16
NA_QT = NA_QROWS * GRID_W
NA_KT = NA_KROWS * GRID_W
NA_KCH = 256
ML_NB = 8
ML_TB = ML_NB * ML_CHUNK
ML_HPS = 1

VMEM_LIMIT = 56 * 1024 * 1024


def _cparams(sem, vmem=VMEM_LIMIT):
    return pltpu.CompilerParams(dimension_semantics=sem, vmem_limit_bytes=vmem)


def _silu(x):
    return x * jax.nn.sigmoid(x)


def _dsilu(x):
    s = jax.nn.sigmoid(x)
    return s * (1.0 + x * (1.0 - s))


def _dot(a, b, dims):
    return lax.dot_general(a, b, (dims, ((), ())), preferred_element_type=F32)


def _nn(a, b):
    return _dot(a, b, ((1,), (0,)))


def _nt(a, b):
    return _dot(a, b, ((1,), (1,)))


def _tn(a, b):
    return _dot(a, b, ((0,), (0,)))


def _mm(a, b, mode, tm, tn, tk, name, bias=None, out_dtype=F32):
    if mode == 'tn':
        kdim, m = a.shape
    else:
        m, kdim = a.shape
    n = b.shape[0] if mode == 'nt' else b.shape[1]
    assert m % tm == 0 and kdim % tk == 0
    nk = kdim // tk
    grid = (m // tm, pl.cdiv(n, tn), nk)
    a_spec = (pl.BlockSpec((tk, tm), lambda i, j, k: (k, i)) if mode == 'tn'
              else pl.BlockSpec((tm, tk), lambda i, j, k: (i, k)))
    b_spec = (pl.BlockSpec((tn, tk), lambda i, j, k: (j, k)) if mode == 'nt'
              else pl.BlockSpec((tk, tn), lambda i, j, k: (k, j)))
    in_specs = [a_spec, b_spec]
    args = [a, b]
    if bias is not None:
        in_specs.append(pl.BlockSpec((1, tn), lambda i, j, k: (0, j)))
        args.append(bias)
    dot = {'nn': _nn, 'nt': _nt, 'tn': _tn}[mode]

    def body(*refs):
        if bias is not None:
            a_ref, b_ref, bias_ref, o_ref, acc_ref = refs
        else:
            a_ref, b_ref, o_ref, acc_ref = refs
        k = pl.program_id(2)

        @pl.when(k == 0)
        def _():
            acc_ref[...] = jnp.zeros_like(acc_ref)

        acc_ref[...] += dot(a_ref[...].astype(BF16), b_ref[...].astype(BF16))

        @pl.when(k == nk - 1)
        def _():
            r = acc_ref[...]
            if bias is not None:
                r = r + bias_ref[...]
            o_ref[...] = r.astype(out_dtype)

    return pl.pallas_call(
        body, name=name, grid=grid, in_specs=in_specs,
        out_specs=pl.BlockSpec((tm, tn), lambda i, j, k: (i, j)),
        out_shape=jax.ShapeDtypeStruct((m, n), out_dtype),
        scratch_shapes=[pltpu.VMEM((tm, tn), F32)],
        compiler_params=_cparams(("parallel", "parallel", "arbitrary")),
    )(*args)


def _row(n):
    return pl.BlockSpec((1, n), lambda i: (0, 0))


def _modulated_norm(xv, nw, sc, sh):
    r = lax.rsqrt(jnp.mean(xv * xv, axis=-1, keepdims=True) + EPS)
    xn = xv * r
    return xn * nw * (1.0 + sc) + sh, xn, r


IN_TN = 1536


def _in_proj(x, norm_w, scale, shift, w_in_p, b_in_p):
    t, d = x.shape
    tm = 1024
    gcol = IN_MAIN // 128

    def body(x_ref, nw_ref, sc_ref, sh_ref, w_ref, b_ref, wg_ref, bg_ref, proj_ref, g_ref, h_scr):
        @pl.when(pl.program_id(1) == 0)
        def _():
            h, _, _ = _modulated_norm(x_ref[...], nw_ref[...], sc_ref[...], sh_ref[...])
            h_scr[...] = h.astype(BF16)
            g_ref[...] = _nn(h_scr[...], wg_ref[...]) + bg_ref[...]
        proj_ref[...] = _nn(h_scr[...], w_ref[...]) + b_ref[...]

    row = lambda n: pl.BlockSpec((1, n), lambda i, j: (0, 0))
    return pl.pallas_call(
        body, name="in_proj", grid=(t // tm, IN_MAIN // IN_TN),
        in_specs=[pl.BlockSpec((tm, d), lambda i, j: (i, 0)), row(d), row(d), row(d),
                  pl.BlockSpec((d, IN_TN), lambda i, j: (0, j)), pl.BlockSpec((1, IN_TN), lambda i, j: (0, j)),
                  pl.BlockSpec((d, 128), lambda i, j: (0, gcol)), pl.BlockSpec((1, 128), lambda i, j: (0, gcol))],
        out_specs=[pl.BlockSpec((tm, IN_TN), lambda i, j: (i, j)), pl.BlockSpec((tm, 128), lambda i, j: (i, 0))],
        out_shape=[jax.ShapeDtypeStruct((t, IN_MAIN), F32), jax.ShapeDtypeStruct((t, 128), F32)],
        scratch_shapes=[pltpu.VMEM((tm, d), BF16)],
        compiler_params=_cparams(("parallel", "arbitrary")),
    )(x, norm_w, scale, shift, w_in_p, b_in_p, w_in_p, b_in_p)


def _ml_norm_parts(hs, o, z, nw):
    outs = []
    for hh in range(ML_HEADS):
        sl = slice(hh * ML_HEAD_DIM, (hh + 1) * ML_HEAD_DIM)
        hm = hs[:, sl] * jax.nn.sigmoid(o[:, sl])
        mu = jnp.mean(hm, axis=-1, keepdims=True)
        cen = hm - mu
        var = jnp.mean(cen * cen, axis=-1, keepdims=True)
        rs = lax.rsqrt(var + EPS)
        outs.append((sl, cen * rs, rs))
    return outs


def _mix_fwd(o_na, proj, h_f, h_b, ml_norm_w, tm):
    t = o_na.shape[0]

    def body(ona_ref, naz_ref, hf_ref, hb_ref, o_ref, z_ref, nw_ref, mix_ref):
        mix_ref[:, 0:NA_W] = (ona_ref[...] * _silu(naz_ref[...])).astype(BF16)
        hs = hf_ref[...] + hb_ref[...]
        z = z_ref[...]
        for sl, xn, _ in _ml_norm_parts(hs, o_ref[...], z, nw_ref[...]):
            mix_ref[:, NA_W + sl.start:NA_W + sl.stop] = (xn * nw_ref[:, sl] * _silu(z[:, sl])).astype(BF16)

    blk = lambda c: pl.BlockSpec((tm, 512), lambda i, c=c: (i, c))
    return pl.pallas_call(
        body, name="mix_fwd", grid=(t // tm,),
        in_specs=[blk(0), blk(3), blk(0), blk(0), blk(7), blk(8), _row(ML_W)],
        out_specs=pl.BlockSpec((tm, 1024), lambda i: (i, 0)),
        out_shape=jax.ShapeDtypeStruct((t, 1024), BF16),
        compiler_params=_cparams(("parallel",)),
    )(o_na, proj, h_f, h_b, proj, proj, ml_norm_w)


def _loss_fb(x, y, gate, target, fnw, tm):
    t, d = x.shape

    def body(x_ref, y_ref, g_ref, tg_ref, w_ref, loss_ref, dres_ref, dy_ref, dgate_ref, gw_ref):
        @pl.when(pl.program_id(0) == 0)
        def _():
            loss_ref[...] = jnp.zeros_like(loss_ref)
            dgate_ref[...] = jnp.zeros_like(dgate_ref)
            gw_ref[...] = jnp.zeros_like(gw_ref)
        yv = y_ref[...]
        gate_v = g_ref[...]
        hres = x_ref[...] + gate_v * yv
        r = lax.rsqrt(jnp.mean(hres * hres, axis=-1, keepdims=True) + EPS)
        xn = hres * r
        err = xn * w_ref[...] - tg_ref[...]
        loss_ref[...] += 0.5 * jnp.sum(jnp.sum(err * err, axis=-1, keepdims=True) * (1.0 / d), axis=0, keepdims=True)
        dout = err * (1.0 / d)
        gw_ref[...] += jnp.sum(dout * xn, axis=0, keepdims=True)
        dxn = dout * w_ref[...]
        dres = r * (dxn - xn * jnp.mean(dxn * xn, axis=-1, keepdims=True))
        dres_ref[...] = dres
        dy_ref[...] = (dres * gate_v).astype(BF16)
        dgate_ref[...] += jnp.sum(dres * yv, axis=0, keepdims=True)

    full = pl.BlockSpec((tm, d), lambda i: (i, 0))
    return pl.pallas_call(
        body, name="loss_fb", grid=(t // tm,),
        in_specs=[full, full, _row(d), full, _row(d)],
        out_specs=[pl.BlockSpec((1, 128), lambda i: (0, 0)), full, full, _row(d), _row(d)],
        out_shape=[jax.ShapeDtypeStruct((1, 128), F32), jax.ShapeDtypeStruct((t, d), F32),
                   jax.ShapeDtypeStruct((t, d), BF16), jax.ShapeDtypeStruct((1, d), F32),
                   jax.ShapeDtypeStruct((1, d), F32)],
        compiler_params=_cparams(("arbitrary",)),
    )(x, y, gate, target, fnw)


def _mix_bwd(dmix, o_na, proj, h_f, h_b, ml_norm_w, tm):
    t = o_na.shape[0]

    def body(dna_ref, dml_ref, ona_ref, naz_ref, hf_ref, hb_ref, o_ref, z_ref, nw_ref,
             dona_ref, dnaz_ref, dhs_ref, do_ref, dz_ref, gw_ref):
        @pl.when(pl.program_id(0) == 0)
        def _():
            gw_ref[...] = jnp.zeros_like(gw_ref)
        dna = dna_ref[...]
        naz = naz_ref[...]
        dona_ref[...] = dna * _silu(naz)
        dnaz_ref[...] = dna * ona_ref[...] * _dsilu(naz)
        hs = hf_ref[...] + hb_ref[...]
        z = z_ref[...]
        ov = o_ref[...]
        for sl, xn, rs in _ml_norm_parts(hs, ov, z, nw_ref[...]):
            dyv = dml_ref[:, sl]
            zz = z[:, sl]
            w = nw_ref[:, sl]
            sz = _silu(zz)
            dz_ref[:, sl] = dyv * xn * w * _dsilu(zz)
            gw_ref[:, sl] += jnp.sum(dyv * xn * sz, axis=0, keepdims=True)
            dxn = dyv * w * sz
            dhm = rs * (dxn - jnp.mean(dxn, axis=-1, keepdims=True)
                        - xn * jnp.mean(dxn * xn, axis=-1, keepdims=True))
            so = jax.nn.sigmoid(ov[:, sl])
            dhs_ref[:, sl] = dhm * so
            do_ref[:, sl] = dhm * hs[:, sl] * so * (1.0 - so)

    blk = lambda c: pl.BlockSpec((tm, 512), lambda i, c=c: (i, c))
    o512 = jax.ShapeDtypeStruct((t, 512), F32)
    return pl.pallas_call(
        body, name="mix_bwd", grid=(t // tm,),
        in_specs=[blk(0), blk(1), blk(0), blk(3), blk(0), blk(0), blk(7), blk(8), _row(ML_W)],
        out_specs=[blk(0)] * 5 + [_row(ML_W)],
        out_shape=[o512] * 5 + [jax.ShapeDtypeStruct((1, ML_W), F32)],
        compiler_params=_cparams(("arbitrary",)),
    )(dmix, dmix, o_na, proj, h_f, h_b, proj, proj, ml_norm_w)


def _in_bwd(pieces, x, dres, w_in_p, norm_w, scale, shift):
    t, d = x.shape
    tm = 256
    nt = t // tm
    widths = [p.shape[1] for p in pieces]
    offs = [sum(widths[:k]) for k in range(len(widths))]
    assert sum(widths) == IN_PAD
    npc = len(pieces)

    def body(*refs):
        p_refs = refs[:npc]
        (x_ref, dres_ref, w_hbm, nw_ref, sc_ref, sh_ref,
         gx_ref, gw_hbm, gb_ref, dsc_ref, dsh_ref, gnw_ref, w_vmem, acc, sem) = refs[npc:]
        i = pl.program_id(0)

        @pl.when(i == 0)
        def _():
            cp = pltpu.make_async_copy(w_hbm, w_vmem, sem.at[0])
            cp.start()
            acc[...] = jnp.zeros_like(acc)
            gb_ref[...] = jnp.zeros_like(gb_ref)
            dsc_ref[...] = jnp.zeros_like(dsc_ref)
            dsh_ref[...] = jnp.zeros_like(dsh_ref)
            gnw_ref[...] = jnp.zeros_like(gnw_ref)
            cp.wait()

        nw = nw_ref[...]
        s1 = 1.0 + sc_ref[...]
        h, xn, r = _modulated_norm(x_ref[...], nw, sc_ref[...], sh_ref[...])
        hb = h.astype(BF16)
        dhv = jnp.zeros((tm, d), F32)
        for p_ref, c0, w in zip(p_refs, offs, widths):
            pt = p_ref[...]
            pb = pt.astype(BF16)
            dhv = dhv + _nt(pb, w_vmem[:, c0:c0 + w])
            acc[:, c0:c0 + w] += _tn(hb, pb)
            gb_ref[:, c0:c0 + w] += jnp.sum(pt, axis=0, keepdims=True)
        dsh_ref[...] += jnp.sum(dhv, axis=0, keepdims=True)
        dsc_ref[...] += jnp.sum(dhv * xn * nw, axis=0, keepdims=True)
        gnw_ref[...] += jnp.sum(dhv * xn * s1, axis=0, keepdims=True)
        dxn = dhv * nw * s1
        gx_ref[...] = dres_ref[...] + r * (dxn - xn * jnp.mean(dxn * xn, axis=-1, keepdims=True))

        @pl.when(i == nt - 1)
        def _():
            cp = pltpu.make_async_copy(acc, gw_hbm, sem.at[1])
            cp.start()
            cp.wait()

    full = pl.BlockSpec((tm, d), lambda i: (i, 0))
    return pl.pallas_call(
        body, name="in_bwd", grid=(nt,),
        in_specs=[pl.BlockSpec((tm, w), lambda i: (i, 0)) for w in widths]
        + [full, full, pl.BlockSpec(memory_space=pl.ANY), _row(d), _row(d), _row(d)],
        out_specs=[full, pl.BlockSpec(memory_space=pl.ANY), _row(IN_PAD), _row(d), _row(d), _row(d)],
        out_shape=[jax.ShapeDtypeStruct((t, d), F32), jax.ShapeDtypeStruct((d, IN_PAD), F32),
                   jax.ShapeDtypeStruct((1, IN_PAD), F32)] + [jax.ShapeDtypeStruct((1, d), F32)] * 3,
        scratch_shapes=[pltpu.VMEM((d, IN_PAD), BF16), pltpu.VMEM((d, IN_PAD), F32),
                        pltpu.SemaphoreType.DMA((2,))],
        compiler_params=_cparams(("arbitrary",)),
    )(*pieces, x, dres, w_in_p, norm_w, scale, shift)


def _na_static(rows):
    cases = [(0, 0), (NA_QROWS, NA_QROWS - 4), (rows - NA_QROWS, rows - NA_KROWS)]
    dy = np.zeros((3, NA_QROWS, NA_KROWS), np.int32)
    rv = np.zeros((3, NA_QROWS, NA_KROWS), bool)
    for cs, (r0, kr0) in enumerate(cases):
        for i in range(NA_QROWS):
            for j in range(NA_KROWS):
                r, kr = r0 + i, kr0 + j
                rs = min(max(r - NA_KH // 2, 0), rows - NA_KH)
                rv[cs, i, j] = rs <= kr <= rs + NA_KH - 1
                dy[cs, i, j] = min(max(kr - r + NA_KH - 1, 0), 2 * NA_KH - 2)
    cq = np.arange(GRID_W)[:, None]
    ck = np.arange(GRID_W)[None, :]
    cs0 = np.clip(cq - NA_KW // 2, 0, GRID_W - NA_KW)
    cv = (ck >= cs0) & (ck < cs0 + NA_KW)
    dx = np.clip(ck - cq, -(NA_KW - 1), NA_KW - 1) + NA_KW - 1
    return dy, rv, dx.astype(np.int32), cv


def _na_bias_table(rpb, rows):
    _, _, dx, cv = _na_static(rows)
    ndy = 2 * NA_KH - 1
    rpx = jnp.where(cv[None, None], rpb[:, :, dx], NEG)
    neg = jnp.full((NA_HEADS, 1, GRID_W, GRID_W), NEG, F32)
    rpx = jnp.concatenate([rpx, neg], axis=1)
    nxt = jnp.concatenate([rpx[:, 1:], neg], axis=1)
    negs = jnp.broadcast_to(neg, rpx.shape)
    pairs = jnp.concatenate([jnp.concatenate([rpx, nxt], axis=3), jnp.concatenate([rpx, negs], axis=3),
                             jnp.concatenate([negs, rpx], axis=3)], axis=1)
    npair = pairs.shape[1]

    def body(m_ref, o_ref):
        cs = pl.program_id(1)
        r0 = jnp.where(cs == 0, 0, jnp.where(cs == 1, NA_QROWS, rows - NA_QROWS))
        kr0 = jnp.where(cs == 0, 0, jnp.where(cs == 1, NA_QROWS - NA_KH // 2, rows - NA_KROWS))
        for i in range(NA_QROWS):
            r = r0 + i
            rs = jnp.clip(r - NA_KH // 2, 0, rows - NA_KH)
            for jp in range(NA_KROWS // 2):
                kl = kr0 + 2 * jp
                vl = (kl >= rs) & (kl <= rs + NA_KH - 1)
                vr = (kl + 1 >= rs) & (kl + 1 <= rs + NA_KH - 1)
                dyl = jnp.clip(kl - r + NA_KH - 1, 0, ndy)
                dyr = jnp.clip(kl + 1 - r + NA_KH - 1, 0, ndy)
                idx = jnp.where(vl & vr, dyl, jnp.where(vl, 16 + dyl, jnp.where(vr, 32 + dyr, 16 + ndy)))
                o_ref[0, 0, i * GRID_W:(i + 1) * GRID_W, jp * 128:(jp + 1) * 128] = m_ref[0, idx]

    return pl.pallas_call(
        body, name="na_bias_table", grid=(NA_HEADS, 3),
        in_specs=[pl.BlockSpec((1, npair, GRID_W, 128), lambda h, cs: (h, 0, 0, 0))],
        out_specs=pl.BlockSpec((1, 1, NA_QT, NA_KT), lambda h, cs: (h, cs, 0, 0)),
        out_shape=jax.ShapeDtypeStruct((NA_HEADS, 3, NA_QT, NA_KT), F32),
        compiler_params=_cparams(("parallel", "parallel")),
    )(pairs)


def _na_specs(t):
    nb = t // NA_QT
    nkb = t // NA_KCH
    npieces = NA_KT // NA_KCH

    def kb0(b):
        return jnp.clip(b * (NA_QT // NA_KCH) - 1, 0, nkb - npieces)

    def case(b):
        return jnp.where(b == 0, 0, jnp.where(b == nb - 1, 2, 1))

    q_spec = pl.BlockSpec((NA_QT, 128), lambda p, b: (b, p))
    k_specs = [pl.BlockSpec((NA_KCH, 128), lambda p, b, i=i: (kb0(b) + i, 4 + p)) for i in range(npieces)]
    v_specs = [pl.BlockSpec((NA_KCH, 128), lambda p, b, i=i: (kb0(b) + i, 8 + p)) for i in range(npieces)]
    tbl_spec = pl.BlockSpec((2, 1, NA_QT, NA_KT), lambda p, b: (p, case(b), 0, 0))
    io_spec = pl.BlockSpec((NA_QT, 128), lambda p, b: (b, p))
    return nb, npieces, kb0, case, q_spec, k_specs, v_specs, tbl_spec, io_spec


def _na_probs(qh, ks, tbl_ref, hh, npieces):
    s = [_nt(qh, ks[i]) + tbl_ref[hh, 0, :, i * NA_KCH:(i + 1) * NA_KCH] for i in range(npieces)]
    m = functools.reduce(jnp.maximum, [jnp.max(si, axis=1, keepdims=True) for si in s])
    p = [jnp.exp(si - m) for si in s]
    l = functools.reduce(jnp.add, [jnp.sum(pi, axis=1, keepdims=True) for pi in p])
    return p, l


def _na_fwd(proj, tbl):
    t = proj.shape[0]
    nb, npieces, _, _, q_spec, k_specs, v_specs, tbl_spec, io_spec = _na_specs(t)

    def body(*refs):
        q_ref = refs[0]
        k_refs = refs[1:1 + npieces]
        v_refs = refs[1 + npieces:1 + 2 * npieces]
        tbl_ref, o_ref = refs[1 + 2 * npieces:]
        lane = lax.broadcasted_iota(jnp.int32, (1, 128), 1)
        qv = q_ref[...] * (NA_HEAD_DIM ** -0.5)
        ks = [r[...].astype(BF16) for r in k_refs]
        vs = [r[...].astype(BF16) for r in v_refs]
        out = jnp.zeros((NA_QT, 128), F32)
        for hh in range(2):
            msk = (lane // NA_HEAD_DIM) == hh
            qh = jnp.where(msk, qv, 0.0).astype(BF16)
            p, l = _na_probs(qh, ks, tbl_ref, hh, npieces)
            o = functools.reduce(jnp.add, [_nn(p[i].astype(BF16), vs[i]) for i in range(npieces)])
            out = jnp.where(msk, o / l, out)
        o_ref[...] = out

    return pl.pallas_call(
        body, name="na_fwd", grid=(4, nb),
        in_specs=[q_spec] + k_specs + v_specs + [tbl_spec],
        out_specs=io_spec,
        out_shape=jax.ShapeDtypeStruct((t, NA_W), F32),
        compiler_params=_cparams(("parallel", "arbitrary")),
    )(*([proj] * (1 + 2 * npieces)), tbl)


def _na_bwd(proj, tbl, d_o, o_na):
    t = proj.shape[0]
    nb, npieces, kb0, case, q_spec, k_specs, v_specs, tbl_spec, io_spec = _na_specs(t)

    def body(*refs):
        q_ref = refs[0]
        k_refs = refs[1:1 + npieces]
        v_refs = refs[1 + npieces:1 + 2 * npieces]
        (tbl_ref, do_ref, o_ref, dq_ref, dk_hbm, dv_hbm, rpb_ref,
         dk_acc, dv_acc, ds_scr, sem) = refs[1 + 2 * npieces:]
        p_id = pl.program_id(0)
        b = pl.program_id(1)

        @pl.when(b == 0)
        def _():
            dk_acc[...] = jnp.zeros_like(dk_acc)
            dv_acc[...] = jnp.zeros_like(dv_acc)

        @pl.when((b == 0) | (b == 1) | (b == nb - 1))
        def _():
            rpb_ref[...] = jnp.zeros_like(rpb_ref)

        lane = lax.broadcasted_iota(jnp.int32, (1, 128), 1)
        scale = NA_HEAD_DIM ** -0.5
        qv = q_ref[...] * scale
        ks = [r[...].astype(BF16) for r in k_refs]
        vs = [r[...].astype(BF16) for r in v_refs]
        dov = do_ref[...]
        ov = o_ref[...]
        tok0 = kb0(b) * NA_KCH
        dq = jnp.zeros((NA_QT, 128), F32)
        for hh in range(2):
            msk = (lane // NA_HEAD_DIM) == hh
            qh = jnp.where(msk, qv, 0.0).astype(BF16)
            p, l = _na_probs(qh, ks, tbl_ref, hh, npieces)
            rl = 1.0 / l
            doh = jnp.where(msk, dov, 0.0)
            dd = jnp.sum(doh * ov, axis=1, keepdims=True)
            dohb = doh.astype(BF16)
            dqh = jnp.zeros((NA_QT, 128), F32)
            for i in range(npieces):
                pn = p[i] * rl
                ds = pn * (_nt(dohb, vs[i]) - dd)
                dsb = ds.astype(BF16)
                dqh = dqh + _nn(dsb, ks[i])
                rows = pl.ds(pl.multiple_of(tok0 + i * NA_KCH, NA_KCH), NA_KCH)
                dk_acc[rows, :] += _tn(dsb, qh)
                dv_acc[rows, :] += _tn(pn.astype(BF16), dohb)
                ds_scr[:, i * NA_KCH:(i + 1) * NA_KCH] = ds
            dq = jnp.where(msk, dqh * scale, dq)
            acc = ds_scr[0:GRID_W, :]
            for i in range(1, NA_QROWS):
                acc = acc + pltpu.roll(ds_scr[i * GRID_W:(i + 1) * GRID_W, :], NA_KT - i * GRID_W, 1)
            rpb_ref[0, 0, hh] += acc
        dq_ref[...] = dq

        @pl.when(b == nb - 1)
        def _():
            cols = pl.ds(pl.multiple_of(p_id * 128, 128), 128)
            ck = pltpu.make_async_copy(dk_acc, dk_hbm.at[:, cols], sem.at[0])
            cv = pltpu.make_async_copy(dv_acc, dv_hbm.at[:, cols], sem.at[1])
            ck.start()
            cv.start()
            ck.wait()
            cv.wait()

    o512 = jax.ShapeDtypeStruct((t, NA_W), F32)
    return pl.pallas_call(
        body, name="na_bwd", grid=(4, nb),
        in_specs=[q_spec] + k_specs + v_specs + [tbl_spec, io_spec, io_spec],
        out_specs=[io_spec, pl.BlockSpec(memory_space=pl.ANY), pl.BlockSpec(memory_space=pl.ANY),
                   pl.BlockSpec((1, 1, 2, GRID_W, NA_KT), lambda p, b: (p, case(b), 0, 0, 0))],
        out_shape=[o512, o512, o512, jax.ShapeDtypeStruct((4, 3, 2, GRID_W, NA_KT), F32)],
        scratch_shapes=[pltpu.VMEM((t, 128), F32), pltpu.VMEM((t, 128), F32),
                        pltpu.VMEM((NA_QT, NA_KT), F32), pltpu.SemaphoreType.DMA((2,))],
        compiler_params=_cparams(("arbitrary", "arbitrary")),
    )(*([proj] * (1 + 2 * npieces)), tbl, d_o, o_na)


def _rpb_reduce(rpbacc, rows):
    nacc = 4 * 3 * 2

    def shift_body(a_ref, o_ref):
        acc = a_ref[0, 0:1, :]
        for cq in range(1, GRID_W):
            acc = acc + pltpu.roll(a_ref[0, cq:cq + 1, :], NA_KT - cq, 1)
        o_ref[0] = jnp.broadcast_to(acc, (8, NA_KT))

    vec = pl.pallas_call(
        shift_body, name="rpb_shift", grid=(nacc,),
        in_specs=[pl.BlockSpec((1, GRID_W, NA_KT), lambda a: (a, 0, 0))],
        out_specs=pl.BlockSpec((1, 8, NA_KT), lambda a: (a, 0, 0)),
        out_shape=jax.ShapeDtypeStruct((nacc, 8, NA_KT), F32),
        compiler_params=_cparams(("parallel",)),
    )(rpbacc.reshape(nacc, GRID_W, NA_KT))
    a = vec[:, 0].reshape(4, 3, 2, NA_KT).transpose(0, 2, 1, 3).reshape(NA_HEADS, 3, NA_KT)
    if rows // NA_QROWS < 3:
        a = a.at[:, 1].set(0.0)
    dd = np.arange(NA_KROWS)[:, None]
    dxo = np.arange(-(NA_KW - 1), NA_KW)[None, :]
    idx = ((dd * GRID_W + dxo) % NA_KT).reshape(-1)
    g = a[..., idx].reshape(NA_HEADS, 3 * NA_KROWS, 2 * NA_KW - 1)
    g = jnp.pad(g, ((0, 0), (0, 0), (0, 128 - (2 * NA_KW - 1))))
    nmat = np.zeros((16, 3 * NA_KROWS), np.float32)
    for cs, delta in enumerate((0, -(NA_KH // 2), -(NA_KROWS - NA_QROWS))):
        for d in range(NA_KROWS):
            jmi = d - NA_KROWS if (cs == 0 and d > NA_KH - 1) else d
            dy = jmi + delta + NA_KH - 1
            if 0 <= dy <= 2 * NA_KH - 2:
                nmat[dy, cs * NA_KROWS + d] = 1.0

    def body(n_ref, g_ref, o_ref):
        o_ref[0] = jnp.dot(n_ref[...], g_ref[0], precision=HI, preferred_element_type=F32)

    out = pl.pallas_call(
        body, name="rpb_reduce", grid=(NA_HEADS,),
        in_specs=[pl.BlockSpec((16, nmat.shape[1]), lambda h: (0, 0)),
                  pl.BlockSpec((1, nmat.shape[1], 128), lambda h: (h, 0, 0))],
        out_specs=pl.BlockSpec((1, 16, 128), lambda h: (h, 0, 0)),
        out_shape=jax.ShapeDtypeStruct((NA_HEADS, 16, 128), F32),
        compiler_params=_cparams(("parallel",)),
    )(jnp.asarray(nmat), g)
    return out[:, :2 * NA_KH - 1, :2 * NA_KW - 1]


def _halo_specs(tm, t, col):
    nt8 = t // 8
    per = tm // 8
    return [pl.BlockSpec((tm, 1024), lambda i: (i, col)),
            pl.BlockSpec((8, 1024), lambda i: (jnp.maximum(i * per - 1, 0), col)),
            pl.BlockSpec((8, 1024), lambda i: (jnp.minimum((i + 1) * per, nt8 - 1), col))]


def _fill_ext(ext, cur_ref, prev_ref, next_ref, tm, nt):
    i = pl.program_id(0)
    ext[0:8, :] = jnp.where(i == 0, 0.0, prev_ref[...])
    ext[8:8 + tm, :] = cur_ref[...]
    ext[8 + tm:16 + tm, :] = jnp.where(i == nt - 1, 0.0, next_ref[...])


CONV_RC = 16
CONV_CB = 512


def _conv_chunks(tm):
    return [(slice(cb, cb + CONV_CB), slice(rb, rb + CONV_RC))
            for cb in range(0, 1024, CONV_CB) for rb in range(0, tm, CONV_RC)]


def _conv_fwd(proj, conv_w8, conv_b, tm):
    t = proj.shape[0]
    nt = t // tm

    def body(u_ref, up_ref, un_ref, w_ref, b_ref, pre_ref, act_ref, ext):
        _fill_ext(ext, u_ref, up_ref, un_ref, tm, nt)
        for cs, rs in _conv_chunks(tm):
            pre = b_ref[:, cs] + w_ref[0:1, cs] * ext[pl.ds(rs.start + 6, CONV_RC), cs]
            for j in range(1, CONV_W):
                pre = pre + w_ref[j:j + 1, cs] * ext[pl.ds(rs.start + 6 + j, CONV_RC), cs]
            pre_ref[rs, cs] = pre
            act_ref[rs, cs] = _silu(pre)

    full = pl.BlockSpec((tm, 1024), lambda i: (i, 0))
    o = jax.ShapeDtypeStruct((t, 1024), F32)
    return pl.pallas_call(
        body, name="conv_fwd", grid=(nt,),
        in_specs=_halo_specs(tm, t, 2) + [pl.BlockSpec((8, 1024), lambda i: (0, 0)), _row(1024)],
        out_specs=[full, full], out_shape=[o, o],
        scratch_shapes=[pltpu.VMEM((tm + 16, 1024), F32)],
        compiler_params=_cparams(("parallel",)),
    )(proj, proj, proj, conv_w8, conv_b)


def _conv_bwd1(dq_f, dq_b, dk_f, dk_b, dv_f, dv_b, pre, tm):
    t = pre.shape[0]

    def body(dqf, dqb, dkf, dkb, dvf, dvb, pre_ref, dpre_ref, dv_ref, gb_ref):
        @pl.when(pl.program_id(0) == 0)
        def _():
            gb_ref[...] = jnp.zeros_like(gb_ref)
        ds = _dsilu(pre_ref[...])
        dpre_ref[:, 0:ML_W] = (dqf[...] + dqb[...]) * ds[:, 0:ML_W]
        dpre_ref[:, ML_W:] = (dkf[...] + dkb[...]) * ds[:, ML_W:]
        dv_ref[...] = dvf[...] + dvb[...]
        gb_ref[...] += jnp.sum(dpre_ref[...], axis=0, keepdims=True)

    half = pl.BlockSpec((tm, 512), lambda i: (i, 0))
    full = pl.BlockSpec((tm, 1024), lambda i: (i, 0))
    return pl.pallas_call(
        body, name="conv_bwd1", grid=(t // tm,),
        in_specs=[half] * 6 + [full],
        out_specs=[full, half, _row(1024)],
        out_shape=[jax.ShapeDtypeStruct((t, 1024), F32), jax.ShapeDtypeStruct((t, 512), F32),
                   jax.ShapeDtypeStruct((1, 1024), F32)],
        compiler_params=_cparams(("arbitrary",)),
    )(dq_f, dq_b, dk_f, dk_b, dv_f, dv_b, pre)


def _conv_bwd2(dpre, proj, conv_w8, tm):
    t = dpre.shape[0]
    nt = t // tm

    def body(d_ref, dp_ref, dn_ref, u_ref, up_ref, un_ref, w_ref, du_ref, gw_ref, extd, extu):
        @pl.when(pl.program_id(0) == 0)
        def _():
            gw_ref[...] = jnp.zeros_like(gw_ref)
        _fill_ext(extd, d_ref, dp_ref, dn_ref, tm, nt)
        _fill_ext(extu, u_ref, up_ref, un_ref, tm, nt)
        gacc = None
        for cs, rs in _conv_chunks(tm):
            if rs.start == 0:
                gacc = [jnp.zeros((8, CONV_CB), F32) for _ in range(CONV_W)]
            du = w_ref[0:1, cs] * extd[pl.ds(rs.start + 10, CONV_RC), cs]
            for j in range(1, CONV_W):
                du = du + w_ref[j:j + 1, cs] * extd[pl.ds(rs.start + 10 - j, CONV_RC), cs]
            du_ref[rs, cs] = du
            dcur = d_ref[rs, cs]
            for j in range(CONV_W):
                prod = dcur * extu[pl.ds(rs.start + 6 + j, CONV_RC), cs]
                gacc[j] = gacc[j] + functools.reduce(
                    jnp.add, [prod[k:k + 8] for k in range(0, CONV_RC, 8)])
            if rs.stop == tm:
                for j in range(CONV_W):
                    gw_ref[j:j + 1, cs] += jnp.sum(gacc[j], axis=0, keepdims=True)

    full = pl.BlockSpec((tm, 1024), lambda i: (i, 0))
    return pl.pallas_call(
        body, name="conv_bwd2", grid=(nt,),
        in_specs=_halo_specs(tm, t, 0) + _halo_specs(tm, t, 2) + [pl.BlockSpec((8, 1024), lambda i: (0, 0))],
        out_specs=[full, pl.BlockSpec((8, 1024), lambda i: (0, 0))],
        out_shape=[jax.ShapeDtypeStruct((t, 1024), F32), jax.ShapeDtypeStruct((8, 1024), F32)],
        scratch_shapes=[pltpu.VMEM((tm + 16, 1024), F32), pltpu.VMEM((tm + 16, 1024), F32)],
        compiler_params=_cparams(("arbitrary",)),
    )(dpre, dpre, dpre, proj, proj, proj, conv_w8)


def _ml_consts(rev):
    iu = lax.broadcasted_iota(jnp.int32, (ML_CHUNK, ML_CHUNK), 0)
    js = lax.broadcasted_iota(jnp.int32, (ML_CHUNK, ML_CHUNK), 1)
    eye = iu == js
    le = iu <= js
    ge = iu >= js
    csum, csum_t, causal = (ge, le, le) if rev else (le, ge, ge)
    return eye, csum.astype(F32), csum_t.astype(F32), causal


def _col(row, eye):
    return jnp.sum(jnp.where(eye, row, 0.0), axis=1, keepdims=True)


def _rowof(col, eye):
    return jnp.sum(jnp.where(eye, col, 0.0), axis=0, keepdims=True)


def _ml_gates(gi, gf, m0, csum, rev):
    lf = jax.nn.log_sigmoid(gf)
    b_rows = jnp.dot(lf, csum, precision=HI, preferred_element_type=F32)
    bl = jnp.sum(lf, axis=1, keepdims=True)
    a_rows = bl - b_rows + gi
    mloc = jnp.max(a_rows, axis=1, keepdims=True)
    order = list(range(ML_NB))[::-1] if rev else list(range(ML_NB))
    mp, mn, decay = {}, {}, {}
    m = m0
    for n in order:
        mp[n] = m
        m = jnp.maximum(bl[n:n + 1] + m, mloc[n:n + 1])
        mn[n] = m
        decay[n] = jnp.exp(bl[n:n + 1] + mp[n] - m)
    return b_rows, a_rows, gi - b_rows, mp, mn, decay, order


def _ml_load(q_ref, k_ref, v_ref, n):
    sl = slice(n * ML_CHUNK, (n + 1) * ML_CHUNK)
    qb = q_ref[sl, :].astype(BF16)
    kb = (k_ref[sl, :] * (ML_HEAD_DIM ** -0.5)).astype(BF16)
    vn = v_ref[sl, :]
    return sl, qb, kb, vn


def _ml_state_scan(q_ref, k_ref, v_ref, a_rows, mn, decay, order, c0, n0, eye):
    ns = range(ML_NB)
    ld = [_ml_load(q_ref, k_ref, v_ref, n) for n in ns]
    acol = [_col(a_rows[n:n + 1], eye) for n in ns]
    wcol = [jnp.exp(acol[n] - mn[n]) for n in ns]
    u = [_tn((wcol[n] * ld[n][3]).astype(BF16), ld[n][2]) for n in ns]
    nu = [jnp.sum(wcol[n] * ld[n][2].astype(F32), axis=0, keepdims=True) for n in ns]
    cp, npv = {}, {}
    c, nv = c0, n0
    for n in order:
        cp[n], npv[n] = c, nv
        c = decay[n] * c + u[n]
        nv = decay[n] * nv + nu[n]
    return cp, npv, wcol, c, nv


def _ml_intra_all(q_ref, k_ref, v_ref, b_rows, imb_rows, mp, cp, npv, causal, eye):
    ns = range(ML_NB)
    ld = [_ml_load(q_ref, k_ref, v_ref, n) for n in ns]
    qk = [_nt(ld[n][1], ld[n][2]) for n in ns]
    cq = [_nt(ld[n][1], cp[n].astype(BF16)) for n in ns]
    b_col = [_col(b_rows[n:n + 1], eye) for n in ns]
    dlog = [jnp.where(causal, b_col[n] + imb_rows[n:n + 1], NEG) for n in ns]
    m_inter = [b_col[n] + mp[n] for n in ns]
    m_t = [jnp.maximum(m_inter[n], jnp.max(dlog[n], axis=1, keepdims=True)) for n in ns]
    pm = [jnp.exp(dlog[n] - m_t[n]) for n in ns]
    inter = [jnp.exp(m_inter[n] - m_t[n]) for n in ns]
    floor = [jnp.exp(-m_t[n]) for n in ns]
    s = [qk[n] * pm[n] for n in ns]
    qn = [jnp.sum(ld[n][1].astype(F32) * npv[n].astype(BF16).astype(F32), axis=1, keepdims=True) for n in ns]
    sv = [_nn(s[n].astype(BF16), ld[n][3].astype(BF16)) for n in ns]
    den = [jnp.sum(s[n], axis=1, keepdims=True) + inter[n] * qn[n] for n in ns]
    num = [sv[n] + inter[n] * cq[n] for n in ns]
    dn = [jnp.maximum(jnp.abs(den[n]), floor[n]) for n in ns]
    return ld, [dict(pm=pm[n], s=s[n], inter=inter[n], cq=cq[n], qn=qn[n], num=num[n], den=den[n],
                     floor=floor[n], dn=dn[n]) for n in ns]


def _ml_specs(t, rev):
    nblk = t // ML_TB
    blk = (lambda g: nblk - 1 - g) if rev else (lambda g: g)
    hps = ML_HPS
    tile = lambda c0: pl.BlockSpec((ML_TB, 128 * hps), lambda hg, g, c0=c0: (blk(g), c0 // hps + hg))
    gate = pl.BlockSpec((hps, ML_NB, ML_CHUNK), lambda hg, g: (hg, blk(g), 0))
    cchk = pl.BlockSpec((hps, 1, 128, 128), lambda hg, g: (hg, blk(g), 0, 0))
    nmchk = pl.BlockSpec((hps, 1, 8, 128), lambda hg, g: (hg, blk(g), 0, 0))
    return nblk, blk, tile, gate, cchk, nmchk


def _ml_head_views(refs, hh):
    cols = slice(hh * ML_HEAD_DIM, (hh + 1) * ML_HEAD_DIM)
    return [r.at[:, cols] if len(r.shape) == 2 else r.at[hh] for r in refs]


def _ml_fwd(qk_act, proj, gi, gf, rev, name):
    t = qk_act.shape[0]
    nblk, _, tile, gate, cchk, nmchk = _ml_specs(t, rev)

    def body(*refs):
        for hh in range(ML_HPS):
            one_head(*_ml_head_views(refs, hh))

    def one_head(q_ref, k_ref, v_ref, gi_ref, gf_ref, h_ref, cchk_ref, nmchk_ref, c_ref, nm_ref):
        @pl.when(pl.program_id(1) == 0)
        def _():
            c_ref[...] = jnp.zeros_like(c_ref)
            nm_ref[...] = jnp.zeros_like(nm_ref)
        cchk_ref[0] = c_ref[...]
        nmchk_ref[0] = nm_ref[...]
        eye, csum, _, causal = _ml_consts(rev)
        b_rows, a_rows, imb_rows, mp, mn, decay, order = _ml_gates(
            gi_ref[...], gf_ref[...], nm_ref[1:2, 0:1], csum, rev)
        cp, npv, _, c, nv = _ml_state_scan(q_ref, k_ref, v_ref, a_rows, mn, decay, order,
                                            c_ref[...], nm_ref[0:1, :], eye)
        c_ref[...] = c
        nm_ref[0:1, :] = nv
        nm_ref[1:2, :] = jnp.broadcast_to(mn[order[-1]], (1, 128))
        _, rs = _ml_intra_all(q_ref, k_ref, v_ref, b_rows, imb_rows, mp, cp, npv, causal, eye)
        for n in range(ML_NB):
            h_ref[n * ML_CHUNK:(n + 1) * ML_CHUNK, :] = rs[n]['num'] / rs[n]['dn']

    return pl.pallas_call(
        body, name=name, grid=(ML_HEADS // ML_HPS, nblk),
        in_specs=[tile(0), tile(4), tile(24), gate, gate],
        out_specs=[tile(0), cchk, nmchk],
        out_shape=[jax.ShapeDtypeStruct((t, ML_W), F32),
                   jax.ShapeDtypeStruct((ML_HEADS, nblk, 128, 128), F32),
                   jax.ShapeDtypeStruct((ML_HEADS, nblk, 8, 128), F32)],
        scratch_shapes=[pltpu.VMEM((ML_HPS, 128, 128), F32), pltpu.VMEM((ML_HPS, 8, 128), F32)],
        compiler_params=_cparams(("parallel", "arbitrary")),
    )(qk_act, qk_act, proj, gi, gf)


def _ml_bwd(qk_act, proj, gi, gf, dh, cchk_a, nmchk_a, rev, name):
    t = qk_act.shape[0]
    nblk, _, tile, gate, cchk, nmchk = _ml_specs(t, not rev)

    def body(*refs):
        for hh in range(ML_HPS):
            one_head(*_ml_head_views(refs, hh))

    def one_head(q_ref, k_ref, v_ref, gi_ref, gf_ref, dh_ref, cchk_ref, nmchk_ref,
                 dq_ref, dk_ref, dv_ref, dgi_ref, dgf_ref, dc_ref, dn_ref, db_scr, dbl_scr, di_scr):
        @pl.when(pl.program_id(1) == 0)
        def _():
            dc_ref[...] = jnp.zeros_like(dc_ref)
            dn_ref[...] = jnp.zeros_like(dn_ref)
        eye, csum, csum_t, causal = _ml_consts(rev)
        gfv = gf_ref[...]
        b_rows, a_rows, imb_rows, mp, mn, decay, order = _ml_gates(
            gi_ref[...], gfv, nmchk_ref[0, 1:2, 0:1], csum, rev)
        cp, npv, wcol, _, _ = _ml_state_scan(q_ref, k_ref, v_ref, a_rows, mn, decay, order,
                                             cchk_ref[0], nmchk_ref[0, 0:1, :], eye)
        ns = range(ML_NB)
        ld, rs = _ml_intra_all(q_ref, k_ref, v_ref, b_rows, imb_rows, mp, cp, npv, causal, eye)
        sls = [ld[n][0] for n in ns]
        qbs = [ld[n][1] for n in ns]
        kbs = [ld[n][2] for n in ns]
        vbs = [ld[n][3].astype(BF16) for n in ns]
        rdn = [1.0 / rs[n]['dn'] for n in ns]
        dnum = [dh_ref[sls[n], :] * rdn[n] for n in ns]
        hsum = [jnp.sum(dnum[n] * rs[n]['num'], axis=1, keepdims=True) for n in ns]
        dden = [jnp.where(jnp.abs(rs[n]['den']) > rs[n]['floor'],
                          -hsum[n] * rdn[n] * jnp.sign(rs[n]['den']), 0.0) for n in ns]
        dnb = [dnum[n].astype(BF16) for n in ns]
        dsf = [_nt(dnb[n], vbs[n]) + dden[n] for n in ns]
        dv0 = [_tn(rs[n]['s'].astype(BF16), dnb[n]) for n in ns]
        gb = [(dsf[n] * rs[n]['pm']).astype(BF16) for n in ns]
        cpb = [cp[n].astype(BF16) for n in ns]
        idd = [rs[n]['inter'] * dden[n] for n in ns]
        dqa = [_nn(gb[n], kbs[n]) for n in ns]
        dqc = [_nn(dnb[n], cpb[n]) for n in ns]
        dk0 = [_tn(gb[n], qbs[n]) for n in ns]
        xs = [_tn((rs[n]['inter'] * dnum[n]).astype(BF16), qbs[n]) for n in ns]
        for n in ns:
            dq_ref[sls[n], :] = (dqa[n] + rs[n]['inter'] * dqc[n]
                                 + idd[n] * npv[n].astype(BF16).astype(F32))
        rr = [dsf[n] * rs[n]['s'] for n in ns]
        dinter = [jnp.sum(dnum[n] * rs[n]['cq'], axis=1, keepdims=True) + dden[n] * rs[n]['qn'] for n in ns]
        dbcol = [jnp.sum(rr[n], axis=1, keepdims=True) + dinter[n] * rs[n]['inter'] for n in ns]
        dimb = [jnp.sum(rr[n], axis=0, keepdims=True) for n in ns]
        xns = [jnp.sum(idd[n] * qbs[n].astype(F32), axis=0, keepdims=True) for n in ns]
        dcn, dnn = {}, {}
        dc, dn = dc_ref[...], dn_ref[0:1, :]
        for n in order[::-1]:
            dcn[n], dnn[n] = dc, dn
            dc = decay[n] * dc + xs[n]
            dn = decay[n] * dn + xns[n]
        dc_ref[...] = dc
        dn_ref[0:1, :] = dn
        kscale = ML_HEAD_DIM ** -0.5
        dcb = [dcn[n].astype(BF16) for n in ns]
        z = [_nn(vbs[n], dcb[n]) for n in ns]
        kd = [_nt(kbs[n], dcb[n]) for n in ns]
        ddecay = [jnp.sum(jnp.sum(dcn[n] * cp[n], axis=1, keepdims=True), axis=0, keepdims=True)
                  + jnp.sum(dnn[n] * npv[n], axis=1, keepdims=True) for n in ns]
        zd = [z[n] + dnn[n] for n in ns]
        dw = [jnp.sum(zd[n] * kbs[n].astype(F32), axis=1, keepdims=True) for n in ns]
        for n in ns:
            dv_ref[sls[n], :] = dv0[n] + wcol[n] * kd[n]
            dk_ref[sls[n], :] = (dk0[n] + wcol[n] * zd[n]) * kscale
        da = [dw[n] * wcol[n] for n in ns]
        dbl = [jnp.sum(da[n], axis=0, keepdims=True) + ddecay[n] * decay[n] for n in ns]
        da_row = [_rowof(da[n], eye) for n in ns]
        db_row = [_rowof(dbcol[n] - da[n], eye) for n in ns]
        for n in ns:
            db_scr[n:n + 1, :] = db_row[n] - dimb[n]
            di_scr[n:n + 1, :] = dimb[n] + da_row[n]
            dbl_scr[n:n + 1, :] = jnp.broadcast_to(dbl[n], (1, ML_CHUNK))
        dlf = jnp.dot(db_scr[...], csum_t, precision=HI, preferred_element_type=F32) + dbl_scr[...]
        dgf_ref[...] = dlf * jax.nn.sigmoid(-gfv)
        dgi_ref[...] = di_scr[...]

    nc = t // ML_CHUNK
    o512 = jax.ShapeDtypeStruct((t, ML_W), F32)
    og = jax.ShapeDtypeStruct((ML_HEADS, nc, ML_CHUNK), F32)
    return pl.pallas_call(
        body, name=name, grid=(ML_HEADS // ML_HPS, nblk),
        in_specs=[tile(0), tile(4), tile(24), gate, gate, tile(0), cchk, nmchk],
        out_specs=[tile(0), tile(0), tile(0), gate, gate],
        out_shape=[o512, o512, o512, og, og],
        scratch_shapes=[pltpu.VMEM((ML_HPS, 128, 128), F32), pltpu.VMEM((ML_HPS, 8, 128), F32)]
        + [pltpu.VMEM((ML_HPS, ML_NB, ML_CHUNK), F32)] * 3,
        compiler_params=_cparams(("parallel", "arbitrary")),
    )(qk_act, qk_act, proj, gi, gf, dh, cchk_a, nmchk_a)


def _gate_rows(gates16, t):
    g = gates16.reshape(t // ML_CHUNK, ML_CHUNK, 4, ML_HEADS).transpose(2, 3, 0, 1)
    return g[0], g[1], g[2], g[3]


def _gate_cols(dgi_f, dgf_f, dgi_b, dgf_b, t):
    g = jnp.stack([dgi_f, dgf_f, dgi_b, dgf_b]).transpose(2, 3, 0, 1).reshape(t, 4 * ML_HEADS)
    return jnp.pad(g, ((0, 0), (0, 128 - 4 * ML_HEADS)))


def _local_step(x, target, shift, scale, gate, norm_w, w_in_p, b_in_p, conv_w8, conv_b, rpb,
                ml_norm_w, w_out_b, final_norm_w):
    t = x.shape[0]
    rows = t // GRID_W
    tm = 512
    proj, gates = _in_proj(x, norm_w, scale, shift, w_in_p, b_in_p)
    tbl = _na_bias_table(rpb, rows)
    o_na = _na_fwd(proj, tbl)
    pre, qk_act = _conv_fwd(proj, conv_w8, conv_b, tm)
    gi_f, gf_f, gi_b, gf_b = _gate_rows(gates[:, :4 * ML_HEADS], t)
    h_f, cchk_f, nmchk_f = _ml_fwd(qk_act, proj, gi_f, gf_f, False, "ml_fwd_f")
    h_b, cchk_b, nmchk_b = _ml_fwd(qk_act, proj, gi_b, gf_b, True, "ml_fwd_b")
    mix = _mix_fwd(o_na, proj, h_f, h_b, ml_norm_w, tm)
    y = _mm(mix, w_out_b, 'nn', 1024, 512, 1024, "out_proj")
    loss, dres, dy, dgate, g_fnw = _loss_fb(x, y, gate, target, final_norm_w, tm)
    dmix = _mm(dy, w_out_b, 'nt', 1024, 512, 1024, "d_mix")
    g_w_out = _mm(mix, dy, 'tn', 1024, 512, 1024, "g_w_out")
    d_ona, d_naz, dhs, d_o, d_z, g_mlnw = _mix_bwd(dmix, o_na, proj, h_f, h_b, ml_norm_w, tm)
    dq_na, dk_na, dv_na, rpbacc = _na_bwd(proj, tbl, d_ona, o_na)
    g_rpb = _rpb_reduce(rpbacc, rows)
    dq_f, dk_f, dv_f, dgi_f, dgf_f = _ml_bwd(qk_act, proj, gi_f, gf_f, dhs, cchk_f, nmchk_f, False, "ml_bwd_f")
    dq_b, dk_b, dv_b, dgi_b, dgf_b = _ml_bwd(qk_act, proj, gi_b, gf_b, dhs, cchk_b, nmchk_b, True, "ml_bwd_b")
    dpre, dv_ml, g_conv_b = _conv_bwd1(dq_f, dq_b, dk_f, dk_b, dv_f, dv_b, pre, tm)
    du, g_conv_w = _conv_bwd2(dpre, proj, conv_w8, tm)
    dgates = _gate_cols(dgi_f, dgf_f, dgi_b, dgf_b, t)
    grad_x, g_w_in, g_b_in, dscale, dshift, g_nw = _in_bwd(
        [dq_na, dk_na, dv_na, d_naz, du, dv_ml, d_o, d_z, dgates], x, dres, w_in_p, norm_w, scale, shift)
    dmod = jnp.concatenate([dshift, dscale, dgate], axis=1)
    return (loss, grad_x, dmod, g_nw, g_w_in, g_b_in, g_conv_w, g_conv_b, g_rpb, g_mlnw, g_w_out, g_fnw)


MESH = pl.DeviceIdType.MESH
N_DEV = 8
ANY = pl.BlockSpec(memory_space=pl.ANY)
WHOLE_VMEM = pl.BlockSpec(memory_space=pltpu.VMEM)


def _allgather8(blocks, name):
    na = len(blocks)

    def body(*refs):
        x_refs = refs[:na]
        out_refs = refs[na:2 * na]
        send_sems, recv_sems, local_sems = refs[2 * na:]
        x, y, c = lax.axis_index("x"), lax.axis_index("y"), lax.axis_index("c")
        me, sibling = (x, y, c), (x, y, 1 - c)
        chips = [(1 - x, y), (x, 1 - y), (1 - x, 1 - y)]

        def rows(a, px, py, pc):
            m_per = x_refs[a].shape[0]
            return out_refs[a].at[pl.ds((4 * px + 2 * py + pc) * m_per, m_per), :]

        def copy(a, k, block, to, src=None):
            return pltpu.make_async_remote_copy(
                src_ref=rows(a, *block) if src is None else src, dst_ref=rows(a, *block),
                send_sem=send_sems.at[a, k], recv_sem=recv_sems.at[a, k],
                device_id=to, device_id_type=MESH)

        mine, first, passed = [], [], []
        for a in range(na):
            cp = pltpu.make_async_copy(x_refs[a], rows(a, *me), local_sems.at[a])
            cp.start()
            mine.append(cp)
            first.append(copy(a, 0, me, sibling, src=x_refs[a]))
            first += [copy(a, 1 + j, me, (*chip, c), src=x_refs[a]) for j, chip in enumerate(chips)]
        for cp in first:
            cp.start()
        for a in range(na):
            for j, chip in enumerate(chips):
                copy(a, 1 + j, (*chip, c), me).wait_recv()
                fwd = copy(a, 4 + j, (*chip, c), sibling)
                fwd.start()
                passed.append(fwd)
        for a in range(na):
            copy(a, 0, sibling, me).wait_recv()
            for j, chip in enumerate(chips):
                copy(a, 4 + j, (*chip, 1 - c), me).wait_recv()
        for cp in first + passed:
            cp.wait_send()
        for cp in mine:
            cp.wait()

    return pl.pallas_call(
        body, name=name,
        out_shape=[jax.ShapeDtypeStruct((N_DEV * b.shape[0], b.shape[1]), b.dtype) for b in blocks],
        in_specs=[WHOLE_VMEM] * na, out_specs=[WHOLE_VMEM] * na,
        scratch_shapes=[pltpu.SemaphoreType.DMA((na, 7)), pltpu.SemaphoreType.DMA((na, 7)),
                        pltpu.SemaphoreType.DMA((na,))],
        compiler_params=pltpu.CompilerParams(vmem_limit_bytes=VMEM_LIMIT),
    )(*blocks)


def _pair_exchange(arrs, name):
    na = len(arrs)

    def body(*refs):
        in_refs = refs[:na]
        out_refs = refs[na:2 * na]
        send_sems, recv_sems = refs[2 * na:]
        sibling = (lax.axis_index("x"), lax.axis_index("y"), 1 - lax.axis_index("c"))
        copies = [pltpu.make_async_remote_copy(
            src_ref=in_refs[a], dst_ref=out_refs[a], send_sem=send_sems.at[a], recv_sem=recv_sems.at[a],
            device_id=sibling, device_id_type=MESH) for a in range(na)]
        for cp in copies:
            cp.start()
        for cp in copies:
            cp.wait()

    return pl.pallas_call(
        body, name=name,
        out_shape=[jax.ShapeDtypeStruct(a.shape, a.dtype) for a in arrs],
        in_specs=[ANY] * na, out_specs=[ANY] * na,
        scratch_shapes=[pltpu.SemaphoreType.DMA((na,)), pltpu.SemaphoreType.DMA((na,))],
    )(*arrs)


def _chip_exchange(arrs, name):
    na = len(arrs)

    def body(*refs):
        in_refs = refs[:na]
        out_refs = refs[na:2 * na]
        send_sems, recv_sems, local_sems = refs[2 * na:]
        x, y, c = lax.axis_index("x"), lax.axis_index("y"), lax.axis_index("c")
        my_chip = 2 * x + y
        chips = [(1 - x, y), (x, 1 - y), (1 - x, 1 - y)]
        local, remote = [], []
        for a in range(na):
            cp = pltpu.make_async_copy(in_refs[a].at[my_chip], out_refs[a].at[my_chip], local_sems.at[a])
            cp.start()
            local.append(cp)
            for j, (px, py) in enumerate(chips):
                cp = pltpu.make_async_remote_copy(
                    src_ref=in_refs[a].at[2 * px + py], dst_ref=out_refs[a].at[my_chip],
                    send_sem=send_sems.at[a, j], recv_sem=recv_sems.at[a, j],
                    device_id=(px, py, c), device_id_type=MESH)
                cp.start()
                remote.append(cp)
        for cp in remote:
            cp.wait()
        for cp in local:
            cp.wait()

    return pl.pallas_call(
        body, name=name,
        out_shape=[jax.ShapeDtypeStruct(a.shape, a.dtype) for a in arrs],
        in_specs=[ANY] * na, out_specs=[ANY] * na,
        scratch_shapes=[pltpu.SemaphoreType.DMA((na, 3)), pltpu.SemaphoreType.DMA((na, 3)),
                        pltpu.SemaphoreType.DMA((na,))],
    )(*arrs)


def _rows_tile(r):
    for cand in (512, 256, 128, 64, 32, 16, 8):
        if r % cand == 0:
            return cand
    return r


def _add2(a, b, name):
    s, r, n = a.shape
    tr = _rows_tile(r)

    def body(a_ref, b_ref, o_ref):
        o_ref[...] = a_ref[...] + b_ref[...]

    spec = pl.BlockSpec((1, tr, n), lambda i, j: (i, j, 0))
    return pl.pallas_call(
        body, name=name, grid=(s, r // tr), in_specs=[spec, spec], out_specs=spec,
        out_shape=jax.ShapeDtypeStruct(a.shape, a.dtype),
        compiler_params=_cparams(("parallel", "parallel")),
    )(a, b)


def _sum_slabs(a, name):
    s, r, n = a.shape
    tr = _rows_tile(r)

    def body(a_ref, o_ref):
        acc = a_ref[0]
        for k in range(1, s):
            acc = acc + a_ref[k]
        o_ref[...] = acc

    return pl.pallas_call(
        body, name=name, grid=(r // tr,),
        in_specs=[pl.BlockSpec((s, tr, n), lambda i: (0, i, 0))],
        out_specs=pl.BlockSpec((tr, n), lambda i: (i, 0)),
        out_shape=jax.ShapeDtypeStruct((r, n), a.dtype),
        compiler_params=_cparams(("parallel",)),
    )(a)


def _adamw(w, g, m, v, name):
    r, n = w.shape
    tr = _rows_tile(r)
    c1 = 1.0 / (1.0 - ADAM_B1 ** ADAM_STEP)
    c2 = 1.0 / (1.0 - ADAM_B2 ** ADAM_STEP)

    def body(w_ref, g_ref, m_ref, v_ref, d_ref, nm_ref, nv_ref):
        gv = g_ref[...]
        nm = ADAM_B1 * m_ref[...] + (1.0 - ADAM_B1) * gv
        nv = ADAM_B2 * v_ref[...] + (1.0 - ADAM_B2) * (gv * gv)
        nm_ref[...] = nm
        nv_ref[...] = nv
        d_ref[...] = -ADAM_LR * ((nm * c1) / (jnp.sqrt(nv * c2) + ADAM_EPS) + ADAM_WD * w_ref[...])

    spec = pl.BlockSpec((tr, n), lambda i: (i, 0))
    o = jax.ShapeDtypeStruct((r, n), F32)
    return pl.pallas_call(
        body, name=name, grid=(r // tr,), in_specs=[spec] * 4, out_specs=[spec] * 3, out_shape=[o, o, o],
        compiler_params=_cparams(("parallel",)),
    )(w, g, m, v)


def _mod_fwd(c_all, w_ada_s, b_ada_s):
    def body(c_ref, w_ref, b_ref, o_ref):
        o_ref[...] = jnp.dot(_silu(c_ref[...]), w_ref[...], precision=HI, preferred_element_type=F32) + b_ref[...]

    return pl.pallas_call(
        body, name="mod_fwd", out_shape=jax.ShapeDtypeStruct((c_all.shape[0], w_ada_s.shape[1]), F32),
        in_specs=[WHOLE_VMEM] * 3, out_specs=WHOLE_VMEM,
        compiler_params=pltpu.CompilerParams(vmem_limit_bytes=VMEM_LIMIT),
    )(c_all, w_ada_s, b_ada_s)


def _wada_grad(c_all, dmod_s):
    def body(c_ref, d_ref, o_ref):
        o_ref[...] = lax.dot_general(_silu(c_ref[...]), d_ref[...], (((0,), (0,)), ((), ())),
                                     precision=HI, preferred_element_type=F32)

    return pl.pallas_call(
        body, name="w_ada_grad", out_shape=jax.ShapeDtypeStruct((c_all.shape[1], dmod_s.shape[1]), F32),
        in_specs=[WHOLE_VMEM] * 2, out_specs=WHOLE_VMEM,
        compiler_params=pltpu.CompilerParams(vmem_limit_bytes=VMEM_LIMIT),
    )(c_all, dmod_s)


SMALL_ROWS = 24


def _pad_rows(v, nrows):
    v = v.reshape(-1)
    return jnp.pad(v, (0, nrows * 1024 - v.shape[0])).reshape(nrows, 1024)


def _pack_small(b_ada, norm_w, b_in, conv_w_full, conv_b, rpb, ml_norm_w, final_norm_w, last):
    parts = [_pad_rows(b_ada, 3), _pad_rows(norm_w, 1), _pad_rows(b_in, 5), _pad_rows(conv_w_full, 5),
             _pad_rows(conv_b, 1), _pad_rows(rpb, 4), _pad_rows(ml_norm_w, 1), _pad_rows(final_norm_w, 1),
             _pad_rows(last, 3)]
    return jnp.concatenate(parts, axis=0)


def _unpack_small(p):
    return dict(b_ada=p[0:3].reshape(1, 3072), norm_w=p[3:4], b_in=p[4:9].reshape(-1)[:IN_W].reshape(1, IN_W),
                conv_w=p[9:14], conv_b=p[14:15],
                rpb=p[15:19].reshape(-1)[:NA_HEADS * 15 * 31].reshape(1, NA_HEADS, 15, 31),
                ml_norm_w=p[19:20, :ML_W], final_norm_w=p[20], last=p[21])


def kernel(x, c, w_ada, b_ada, norm_w, w_in, b_in, conv_w, conv_b, rpb, ml_norm_w, w_out, final_norm_w, loss_target, m_w_ada, m_b_ada, m_norm_w, m_w_in, m_b_in, m_conv_w, m_conv_b, m_rpb, m_ml_norm_w, m_w_out, m_final_norm_w, v_w_ada, v_b_ada, v_norm_w, v_w_in, v_b_in, v_conv_w, v_conv_b, v_rpb, v_ml_norm_w, v_w_out, v_final_norm_w):
    xi, yi, ci = lax.axis_index("x"), lax.axis_index("y"), lax.axis_index("c")
    chip = 2 * xi + yi
    dev = 2 * chip + ci
    t = x.shape[1]
    ada_n = w_ada.shape[2]
    in_n = w_in.shape[2]
    out_r = w_out.shape[1]

    c_blk = jnp.pad(c, ((0, 7), (0, 0)))
    w_in_half = lax.dynamic_slice_in_dim(w_in[0], ci * 512, 512, axis=0).astype(BF16)
    w_out_half = lax.dynamic_slice_in_dim(w_out[0], ci * (out_r // 2), out_r // 2, axis=0).astype(BF16)
    conv_blk = jnp.pad(conv_w[0], ((0, 3), (0, 0)))
    c_g, conv_g, w_in_g, w_out_g = _allgather8([c_blk, conv_blk, w_in_half, w_out_half], "gather_c_weights")
    c_all = c_g.reshape(N_DEV, 8, D_MODEL)[:, 0]
    b_ada_s = lax.dynamic_slice_in_dim(b_ada, chip * ada_n, ada_n, axis=1)
    mod_s = _mod_fwd(c_all, w_ada[0], b_ada_s)
    (mod_g,) = _allgather8([mod_s], "gather_mod")
    mod_mine = lax.dynamic_index_in_dim(mod_g.reshape(N_DEV, 8, ada_n), dev, axis=1, keepdims=False)
    mod = mod_mine[0::2].reshape(1, 3 * D_MODEL)
    shift, scale, gate = mod[:, :D_MODEL], mod[:, D_MODEL:2 * D_MODEL], mod[:, 2 * D_MODEL:]

    w_in_full = w_in_g.reshape(4, D_MODEL, in_n).transpose(1, 0, 2).reshape(D_MODEL, 4 * in_n)
    w_in_p = jnp.pad(w_in_full, ((0, 0), (0, IN_PAD - IN_W)))
    b_in_p = jnp.pad(b_in, ((0, 0), (0, IN_PAD - IN_W)))
    conv_w8 = conv_g.reshape(4, 2, 8, conv_w.shape[2])[:, 0].transpose(1, 0, 2).reshape(8, D_MODEL)

    (loss, grad_x, dmod, g_nw, g_w_in, g_b_in, g_conv_w, g_conv_b, g_rpb, g_mlnw, g_w_out, g_fnw) = _local_step(
        x[0], loss_target[0], shift, scale, gate, norm_w, w_in_p, b_in_p, conv_w8, conv_b, rpb[0],
        ml_norm_w, w_out_g, final_norm_w.reshape(1, D_MODEL))

    gi4 = g_w_in[:, :IN_W].reshape(2, 512, 4, in_n).transpose(0, 2, 1, 3)
    go4 = g_w_out.reshape(4, 2, out_r // 2, D_MODEL).transpose(1, 0, 2, 3)
    pick = lambda a, k: lax.dynamic_index_in_dim(a, k, axis=0, keepdims=False)
    ri, ro = _pair_exchange([pick(gi4, 1 - ci), pick(go4, 1 - ci)], "rs_pair")
    pi = _add2(pick(gi4, ci), ri, "rs_pair_add_in")
    po = _add2(pick(go4, ci), ro, "rs_pair_add_out")
    qi, qo = _chip_exchange([pi, po], "rs_chips")
    si = _sum_slabs(qi, "rs_sum_in")
    so = _sum_slabs(qo, "rs_sum_out")
    ti, to = _pair_exchange([si, so], "rs_share")
    g_w_in_s = jnp.where(ci == 0, jnp.concatenate([si, ti], axis=0), jnp.concatenate([ti, si], axis=0))
    g_w_out_s = jnp.where(ci == 0, jnp.concatenate([so, to], axis=0), jnp.concatenate([to, so], axis=0))

    small = _pack_small(dmod, g_nw, g_b_in[:, :IN_W], g_conv_w[:CONV_W], g_conv_b, g_rpb, g_mlnw, g_fnw,
                        jnp.pad(loss, ((0, 0), (0, 1024 - 128))))
    (small_g,) = _allgather8([small], "gather_small")
    small_g = small_g.reshape(N_DEV, SMALL_ROWS, 1024)
    small_sum = _sum_slabs(small_g, "small_sum")
    gs = _unpack_small(small_sum)
    dmod_all = small_g[:, 0:3].reshape(N_DEV, 3 * D_MODEL)
    g_w_ada_s = _wada_grad(c_all, lax.dynamic_slice_in_dim(dmod_all, chip * ada_n, ada_n, axis=1))
    g_conv_w_s = lax.dynamic_slice_in_dim(gs['conv_w'], chip * conv_w.shape[2], conv_w.shape[2], axis=1)
    loss_total = gs['last'][0]

    zeros3 = jnp.zeros((3, 1024), F32)
    zc = jnp.zeros((CONV_W, D_MODEL), F32)
    pw = _pack_small(b_ada, norm_w, b_in, zc, conv_b, rpb, ml_norm_w, final_norm_w, zeros3)
    pm = _pack_small(m_b_ada, m_norm_w, m_b_in, zc, m_conv_b, m_rpb, m_ml_norm_w, m_final_norm_w, zeros3)
    pv = _pack_small(v_b_ada, v_norm_w, v_b_in, zc, v_conv_b, v_rpb, v_ml_norm_w, v_final_norm_w, zeros3)
    ds_, nms, nvs = [_unpack_small(a) for a in _adamw(pw, small_sum, pm, pv, "adamw_small")]
    d_ada, nm_ada, nv_ada = _adamw(w_ada[0], g_w_ada_s, m_w_ada[0], v_w_ada[0], "adamw_w_ada")
    d_in, nm_in, nv_in = _adamw(w_in[0], g_w_in_s, m_w_in[0], v_w_in[0], "adamw_w_in")
    d_out, nm_out, nv_out = _adamw(w_out[0], g_w_out_s, m_w_out[0], v_w_out[0], "adamw_w_out")
    d_cw, nm_cw, nv_cw = _adamw(conv_w[0], g_conv_w_s, m_conv_w[0], v_conv_w[0], "adamw_conv_w")

    def group(big_ada, big_in, big_out, cw, sm):
        return (big_ada[None], sm['b_ada'], sm['norm_w'], big_in[None], sm['b_in'], cw[None], sm['conv_b'],
                sm['rpb'], sm['ml_norm_w'], big_out[None], sm['final_norm_w'])

    return ((loss_total, grad_x[None])
            + group(g_w_ada_s, g_w_in_s, g_w_out_s, g_conv_w_s, gs)
            + group(d_ada, d_in, d_out, d_cw, ds_)
            + group(nm_ada, nm_in, nm_out, nm_cw, nms)
            + group(nv_ada, nv_in, nv_out, nv_cw, nvs))
```

```python
import functools

import numpy as np
import jax
import jax.numpy as jnp
from jax import lax
from jax.experimental import pallas as pl
from jax.experimental.pallas import tpu as pltpu

F32 = jnp.float32
BF16 = jnp.bfloat16
HI = lax.Precision.HIGHEST

D_MODEL = 1024
GRID_W = 64
NA_W = 512
NA_HEAD_DIM = 64
NA_HEADS = 8
NA_KH = 8
NA_KW = 16
ML_W = 512
ML_HEADS = 4
ML_HEAD_DIM = 128
ML_CHUNK = 128
CONV_W = 5
EPS = 1e-6
IN_W = 4 * NA_W + 5 * ML_W + 4 * ML_HEADS
IN_MAIN = 4 * NA_W + 5 * ML_W
IN_PAD = IN_MAIN + 128
NEG = -1e30

ADAM_LR = 0.001
ADAM_B1 = 0.9
ADAM_B2 = 0.999
ADAM_EPS = 1e-08
ADAM_WD = 0.01
ADAM_STEP = 10

NA_QROWS = 8
NA_KROWS = 16
NA_QT = NA_QROWS * GRID_W
NA_KT = NA_KROWS * GRID_W
NA_KCH = 256
ML_NB = 8
ML_TB = ML_NB * ML_CHUNK
ML_HPS = 1

VMEM_LIMIT = 56 * 1024 * 1024


def _cparams(sem, vmem=VMEM_LIMIT):
    return pltpu.CompilerParams(dimension_semantics=sem, vmem_limit_bytes=vmem)


def _silu(x):
    return x * jax.nn.sigmoid(x)


def _dsilu(x):
    s = jax.nn.sigmoid(x)
    return s * (1.0 + x * (1.0 - s))


def _dot(a, b, dims):
    return lax.dot_general(a, b, (dims, ((), ())), preferred_element_type=F32)


def _nn(a, b):
    return _dot(a, b, ((1,), (0,)))


def _nt(a, b):
    return _dot(a, b, ((1,), (1,)))


def _tn(a, b):
    return _dot(a, b, ((0,), (0,)))


def _row(n):
    return pl.BlockSpec((1, n), lambda i: (0, 0))


def _modulated_norm(xv, nw, sc, sh):
    r = lax.rsqrt(jnp.mean(xv * xv, axis=-1, keepdims=True) + EPS)
    xn = xv * r
    return xn * nw * (1.0 + sc) + sh, xn, r


IN_TN = 1536


def _in_proj(x, norm_w, scale, shift, w_in_t, b_in_p):
    t, d = x.shape
    tm = 1024
    gcol = IN_MAIN // 128

    def body(x_ref, nw_ref, sc_ref, sh_ref, w_ref, b_ref, wg_ref, bg_ref, proj_ref, g_ref, h_scr):
        @pl.when(pl.program_id(1) == 0)
        def _():
            h, _, _ = _modulated_norm(x_ref[...], nw_ref[...], sc_ref[...], sh_ref[...])
            h_scr[...] = h.astype(BF16)
            g_ref[...] = _nt(h_scr[...], wg_ref[...]) + bg_ref[...]
        proj_ref[...] = _nt(h_scr[...], w_ref[...]) + b_ref[...]

    row = lambda n: pl.BlockSpec((1, n), lambda i, j: (0, 0))
    return pl.pallas_call(
        body, name="in_proj", grid=(t // tm, IN_MAIN // IN_TN),
        in_specs=[pl.BlockSpec((tm, d), lambda i, j: (i, 0)), row(d), row(d), row(d),
                  pl.BlockSpec((IN_TN, d), lambda i, j: (j, 0)), pl.BlockSpec((1, IN_TN), lambda i, j: (0, j)),
                  pl.BlockSpec((128, d), lambda i, j: (gcol, 0)), pl.BlockSpec((1, 128), lambda i, j: (0, gcol))],
        out_specs=[pl.BlockSpec((tm, IN_TN), lambda i, j: (i, j)), pl.BlockSpec((tm, 128), lambda i, j: (i, 0))],
        out_shape=[jax.ShapeDtypeStruct((t, IN_MAIN), F32), jax.ShapeDtypeStruct((t, 128), F32)],
        scratch_shapes=[pltpu.VMEM((tm, d), BF16)],
        compiler_params=_cparams(("parallel", "arbitrary")),
    )(x, norm_w, scale, shift, w_in_t, b_in_p, w_in_t, b_in_p)


def _ml_norm_parts(hs, o, z, nw):
    outs = []
    for hh in range(ML_HEADS):
        sl = slice(hh * ML_HEAD_DIM, (hh + 1) * ML_HEAD_DIM)
        hm = hs[:, sl] * jax.nn.sigmoid(o[:, sl])
        mu = jnp.mean(hm, axis=-1, keepdims=True)
        cen = hm - mu
        var = jnp.mean(cen * cen, axis=-1, keepdims=True)
        rs = lax.rsqrt(var + EPS)
        outs.append((sl, cen * rs, rs))
    return outs


def _tail(o_na, proj, h_f, h_b, x, target, gate, ml_norm_w, fnw, w_out_b):
    t, d = x.shape
    tm = 256

    def body(ona_ref, naz_ref, hf_ref, hb_ref, o_ref, z_ref, x_ref, tg_ref, g_ref, nw_ref, fw_ref, w_ref,
             loss_ref, dres_ref, dona_ref, dnaz_ref, dhs_ref, do_ref, dz_ref, dgate_ref, gfw_ref, gnw_ref,
             gwo_ref, mix_scr):
        @pl.when(pl.program_id(0) == 0)
        def _():
            for r in (loss_ref, dgate_ref, gfw_ref, gnw_ref, gwo_ref):
                r[...] = jnp.zeros_like(r)
        naz = naz_ref[...]
        ona = ona_ref[...]
        sna = _silu(naz)
        mix_scr[:, 0:NA_W] = (ona * sna).astype(BF16)
        hs = hf_ref[...] + hb_ref[...]
        z = z_ref[...]
        ov = o_ref[...]
        parts = _ml_norm_parts(hs, ov, z, nw_ref[...])
        szs = []
        for sl, xn, _ in parts:
            sz = _silu(z[:, sl])
            szs.append(sz)
            mix_scr[:, NA_W + sl.start:NA_W + sl.stop] = (xn * nw_ref[:, sl] * sz).astype(BF16)
        mixb = mix_scr[...]
        wv = w_ref[...]
        yv = _nn(mixb, wv)
        gate_v = g_ref[...]
        hres = x_ref[...] + gate_v * yv
        r = lax.rsqrt(jnp.mean(hres * hres, axis=-1, keepdims=True) + EPS)
        xnf = hres * r
        err = xnf * fw_ref[...] - tg_ref[...]
        loss_ref[...] += 0.5 * jnp.sum(jnp.sum(err * err, axis=-1, keepdims=True) * (1.0 / d), axis=0, keepdims=True)
        dout = err * (1.0 / d)
        gfw_ref[...] += jnp.sum(dout * xnf, axis=0, keepdims=True)
        dxn = dout * fw_ref[...]
        dres = r * (dxn - xnf * jnp.mean(dxn * xnf, axis=-1, keepdims=True))
        dres_ref[...] = dres
        dgate_ref[...] += jnp.sum(dres * yv, axis=0, keepdims=True)
        dyb = (dres * gate_v).astype(BF16)
        gwo_ref[...] += _tn(mixb, dyb)
        dmix = _nt(dyb, wv)
        dna = dmix[:, 0:NA_W]
        dona_ref[...] = dna * sna
        dnaz_ref[...] = dna * ona * _dsilu(naz)
        for (sl, xn, rs), sz in zip(parts, szs):
            dyv = dmix[:, NA_W + sl.start:NA_W + sl.stop]
            zz = z[:, sl]
            w = nw_ref[:, sl]
            dz_ref[:, sl] = dyv * xn * w * _dsilu(zz)
            gnw_ref[:, sl] += jnp.sum(dyv * xn * sz, axis=0, keepdims=True)
            dxm = dyv * w * sz
            dhm = rs * (dxm - jnp.mean(dxm, axis=-1, keepdims=True)
                        - xn * jnp.mean(dxm * xn, axis=-1, keepdims=True))
            so = jax.nn.sigmoid(ov[:, sl])
            dhs_ref[:, sl] = dhm * so
            do_ref[:, sl] = dhm * hs[:, sl] * so * (1.0 - so)

    blk = lambda c: pl.BlockSpec((tm, 512), lambda i, c=c: (i, c))
    full = pl.BlockSpec((tm, d), lambda i: (i, 0))
    o512 = jax.ShapeDtypeStruct((t, 512), F32)
    whole = pl.BlockSpec((d, d), lambda i: (0, 0))
    return pl.pallas_call(
        body, name="tail", grid=(t // tm,),
        in_specs=[blk(0), blk(3), blk(0), blk(0), blk(7), blk(8), full, full, _row(d), _row(ML_W), _row(d), whole],
        out_specs=[pl.BlockSpec((1, 128), lambda i: (0, 0)), full] + [blk(0)] * 5
        + [_row(d), _row(d), _row(ML_W), whole],
        out_shape=[jax.ShapeDtypeStruct((1, 128), F32), jax.ShapeDtypeStruct((t, d), F32)] + [o512] * 5
        + [jax.ShapeDtypeStruct((1, d), F32), jax.ShapeDtypeStruct((1, d), F32),
           jax.ShapeDtypeStruct((1, ML_W), F32), jax.ShapeDtypeStruct((d, d), F32)],
        scratch_shapes=[pltpu.VMEM((tm, d), BF16)],
        compiler_params=_cparams(("arbitrary",)),
    )(o_na, proj, h_f, h_b, proj, proj, x, target, gate, ml_norm_w, fnw, w_out_b)


def _in_bwd(pieces, x, dres, w_in_t, norm_w, scale, shift):
    t, d = x.shape
    tm = 256
    nt = t // tm
    widths = [p.shape[1] for p in pieces]
    offs = [sum(widths[:k]) for k in range(len(widths))]
    assert sum(widths) == IN_PAD
    npc = len(pieces)

    def body(*refs):
        p_refs = refs[:npc]
        (x_ref, dres_ref, w_hbm, nw_ref, sc_ref, sh_ref,
         gx_ref, gw_hbm, gb_ref, dsc_ref, dsh_ref, gnw_ref, w_vmem, acc, sem) = refs[npc:]
        i = pl.program_id(0)

        @pl.when(i == 0)
        def _():
            cp = pltpu.make_async_copy(w_hbm, w_vmem, sem.at[0])
            cp.start()
            acc[...] = jnp.zeros_like(acc)
            gb_ref[...] = jnp.zeros_like(gb_ref)
            dsc_ref[...] = jnp.zeros_like(dsc_ref)
            dsh_ref[...] = jnp.zeros_like(dsh_ref)
            gnw_ref[...] = jnp.zeros_like(gnw_ref)
            cp.wait()

        nw = nw_ref[...]
        s1 = 1.0 + sc_ref[...]
        h, xn, r = _modulated_norm(x_ref[...], nw, sc_ref[...], sh_ref[...])
        hb = h.astype(BF16)
        dhv = jnp.zeros((tm, d), F32)
        for p_ref, c0, w in zip(p_refs, offs, widths):
            pt = p_ref[...]
            pb = pt.astype(BF16)
            dhv = dhv + _nn(pb, w_vmem[c0:c0 + w, :])
            acc[:, c0:c0 + w] += _tn(hb, pb)
            gb_ref[:, c0:c0 + w] += jnp.sum(pt, axis=0, keepdims=True)
        dsh_ref[...] += jnp.sum(dhv, axis=0, keepdims=True)
        dsc_ref[...] += jnp.sum(dhv * xn * nw, axis=0, keepdims=True)
        gnw_ref[...] += jnp.sum(dhv * xn * s1, axis=0, keepdims=True)
        dxn = dhv * nw * s1
        gx_ref[...] = dres_ref[...] + r * (dxn - xn * jnp.mean(dxn * xn, axis=-1, keepdims=True))

        @pl.when(i == nt - 1)
        def _():
            cp = pltpu.make_async_copy(acc, gw_hbm, sem.at[1])
            cp.start()
            cp.wait()

    full = pl.BlockSpec((tm, d), lambda i: (i, 0))
    return pl.pallas_call(
        body, name="in_bwd", grid=(nt,),
        in_specs=[pl.BlockSpec((tm, w), lambda i: (i, 0)) for w in widths]
        + [full, full, pl.BlockSpec(memory_space=pl.ANY), _row(d), _row(d), _row(d)],
        out_specs=[full, pl.BlockSpec(memory_space=pl.ANY), _row(IN_PAD), _row(d), _row(d), _row(d)],
        out_shape=[jax.ShapeDtypeStruct((t, d), F32), jax.ShapeDtypeStruct((d, IN_PAD), F32),
                   jax.ShapeDtypeStruct((1, IN_PAD), F32)] + [jax.ShapeDtypeStruct((1, d), F32)] * 3,
        scratch_shapes=[pltpu.VMEM((IN_PAD, d), BF16), pltpu.VMEM((d, IN_PAD), F32),
                        pltpu.SemaphoreType.DMA((2,))],
        compiler_params=_cparams(("arbitrary",)),
    )(*pieces, x, dres, w_in_t, norm_w, scale, shift)


def _na_static(rows):
    cases = [(0, 0), (NA_QROWS, NA_QROWS - 4), (rows - NA_QROWS, rows - NA_KROWS)]
    dy = np.zeros((3, NA_QROWS, NA_KROWS), np.int32)
    rv = np.zeros((3, NA_QROWS, NA_KROWS), bool)
    for cs, (r0, kr0) in enumerate(cases):
        for i in range(NA_QROWS):
            for j in range(NA_KROWS):
                r, kr = r0 + i, kr0 + j
                rs = min(max(r - NA_KH // 2, 0), rows - NA_KH)
                rv[cs, i, j] = rs <= kr <= rs + NA_KH - 1
                dy[cs, i, j] = min(max(kr - r + NA_KH - 1, 0), 2 * NA_KH - 2)
    cq = np.arange(GRID_W)[:, None]
    ck = np.arange(GRID_W)[None, :]
    cs0 = np.clip(cq - NA_KW // 2, 0, GRID_W - NA_KW)
    cv = (ck >= cs0) & (ck < cs0 + NA_KW)
    dx = np.clip(ck - cq, -(NA_KW - 1), NA_KW - 1) + NA_KW - 1
    return dy, rv, dx.astype(np.int32), cv


def _na_bias_table(rpb, rows):
    _, _, dx, cv = _na_static(rows)
    ndy = 2 * NA_KH - 1
    onehot = (dx.reshape(1, -1) == np.arange(2 * NA_KW - 1)[:, None]).astype(np.float32)
    rpx = jnp.dot(rpb.reshape(NA_HEADS * ndy, 2 * NA_KW - 1), jnp.asarray(onehot), precision=HI)
    rpx = jnp.where(cv[None, None], rpx.reshape(NA_HEADS, ndy, GRID_W, GRID_W), NEG)
    neg = jnp.full((NA_HEADS, 1, GRID_W, GRID_W), NEG, F32)
    rpx = jnp.concatenate([rpx, neg], axis=1)
    nxt = jnp.concatenate([rpx[:, 1:], neg], axis=1)
    negs = jnp.broadcast_to(neg, rpx.shape)
    pairs = jnp.concatenate([jnp.concatenate([rpx, nxt], axis=3), jnp.concatenate([rpx, negs], axis=3),
                             jnp.concatenate([negs, rpx], axis=3)], axis=1)
    npair = pairs.shape[1]

    def body(m_ref, o_ref):
        cs = pl.program_id(1)
        r0 = jnp.where(cs == 0, 0, jnp.where(cs == 1, NA_QROWS, rows - NA_QROWS))
        kr0 = jnp.where(cs == 0, 0, jnp.where(cs == 1, NA_QROWS - NA_KH // 2, rows - NA_KROWS))
        for i in range(NA_QROWS):
            r = r0 + i
            rs = jnp.clip(r - NA_KH // 2, 0, rows - NA_KH)
            for jp in range(NA_KROWS // 2):
                kl = kr0 + 2 * jp
                vl = (kl >= rs) & (kl <= rs + NA_KH - 1)
                vr = (kl + 1 >= rs) & (kl + 1 <= rs + NA_KH - 1)
                dyl = jnp.clip(kl - r + NA_KH - 1, 0, ndy)
                dyr = jnp.clip(kl + 1 - r + NA_KH - 1, 0, ndy)
                idx = jnp.where(vl & vr, dyl, jnp.where(vl, 16 + dyl, jnp.where(vr, 32 + dyr, 16 + ndy)))
                o_ref[0, 0, i * GRID_W:(i + 1) * GRID_W, jp * 128:(jp + 1) * 128] = m_ref[0, idx]

    return pl.pallas_call(
        body, name="na_bias_table", grid=(NA_HEADS, 3),
        in_specs=[pl.BlockSpec((1, npair, GRID_W, 128), lambda h, cs: (h, 0, 0, 0))],
        out_specs=pl.BlockSpec((1, 1, NA_QT, NA_KT), lambda h, cs: (h, cs, 0, 0)),
        out_shape=jax.ShapeDtypeStruct((NA_HEADS, 3, NA_QT, NA_KT), F32),
        compiler_params=_cparams(("parallel", "parallel")),
    )(pairs)


def _na_specs(t):
    nb = t // NA_QT
    nkb = t // NA_KCH
    npieces = NA_KT // NA_KCH

    def kb0(b):
        return jnp.clip(b * (NA_QT // NA_KCH) - 1, 0, nkb - npieces)

    def case(b):
        return jnp.where(b == 0, 0, jnp.where(b == nb - 1, 2, 1))

    q_spec = pl.BlockSpec((NA_QT, 128), lambda p, b: (b, p))
    k_specs = [pl.BlockSpec((NA_KCH, 128), lambda p, b, i=i: (kb0(b) + i, 4 + p)) for i in range(npieces)]
    v_specs = [pl.BlockSpec((NA_KCH, 128), lambda p, b, i=i: (kb0(b) + i, 8 + p)) for i in range(npieces)]
    tbl_spec = pl.BlockSpec((2, 1, NA_QT, NA_KT), lambda p, b: (p, case(b), 0, 0))
    io_spec = pl.BlockSpec((NA_QT, 128), lambda p, b: (b, p))
    return nb, npieces, kb0, case, q_spec, k_specs, v_specs, tbl_spec, io_spec


def _na_probs(qh, ks, tbl_ref, hh, npieces):
    s = [_nt(qh, ks[i]) + tbl_ref[hh, 0, :, i * NA_KCH:(i + 1) * NA_KCH] for i in range(npieces)]
    m = functools.reduce(jnp.maximum, [jnp.max(si, axis=1, keepdims=True) for si in s])
    p = [jnp.exp(si - m) for si in s]
    l = functools.reduce(jnp.add, [jnp.sum(pi, axis=1, keepdims=True) for pi in p])
    return p, l


def _na_fwd(proj, tbl):
    t = proj.shape[0]
    nb, npieces, _, _, q_spec, k_specs, v_specs, tbl_spec, io_spec = _na_specs(t)

    def body(*refs):
        q_ref = refs[0]
        k_refs = refs[1:1 + npieces]
        v_refs = refs[1 + npieces:1 + 2 * npieces]
        tbl_ref, o_ref = refs[1 + 2 * npieces:]
        lane = lax.broadcasted_iota(jnp.int32, (1, 128), 1)
        qv = q_ref[...] * (NA_HEAD_DIM ** -0.5)
        ks = [r[...].astype(BF16) for r in k_refs]
        vs = [r[...].astype(BF16) for r in v_refs]
        out = jnp.zeros((NA_QT, 128), F32)
        for hh in range(2):
            msk = (lane // NA_HEAD_DIM) == hh
            qh = jnp.where(msk, qv, 0.0).astype(BF16)
            p, l = _na_probs(qh, ks, tbl_ref, hh, npieces)
            o = functools.reduce(jnp.add, [_nn(p[i].astype(BF16), vs[i]) for i in range(npieces)])
            out = jnp.where(msk, o / l, out)
        o_ref[...] = out

    return pl.pallas_call(
        body, name="na_fwd", grid=(4, nb),
        in_specs=[q_spec] + k_specs + v_specs + [tbl_spec],
        out_specs=io_spec,
        out_shape=jax.ShapeDtypeStruct((t, NA_W), F32),
        compiler_params=_cparams(("parallel", "arbitrary")),
    )(*([proj] * (1 + 2 * npieces)), tbl)


def _na_bwd(proj, tbl, d_o, o_na):
    t = proj.shape[0]
    nb, npieces, kb0, case, q_spec, k_specs, v_specs, tbl_spec, io_spec = _na_specs(t)

    def body(*refs):
        q_ref = refs[0]
        k_refs = refs[1:1 + npieces]
        v_refs = refs[1 + npieces:1 + 2 * npieces]
        (tbl_ref, do_ref, o_ref, dq_ref, dk_hbm, dv_hbm, rpb_ref,
         dk_acc, dv_acc, ds_scr, sem) = refs[1 + 2 * npieces:]
        p_id = pl.program_id(0)
        b = pl.program_id(1)

        @pl.when(b == 0)
        def _():
            dk_acc[...] = jnp.zeros_like(dk_acc)
            dv_acc[...] = jnp.zeros_like(dv_acc)

        @pl.when((b == 0) | (b == 1) | (b == nb - 1))
        def _():
            rpb_ref[...] = jnp.zeros_like(rpb_ref)

        lane = lax.broadcasted_iota(jnp.int32, (1, 128), 1)
        scale = NA_HEAD_DIM ** -0.5
        qv = q_ref[...] * scale
        ks = [r[...].astype(BF16) for r in k_refs]
        vs = [r[...].astype(BF16) for r in v_refs]
        dov = do_ref[...]
        ov = o_ref[...]
        tok0 = kb0(b) * NA_KCH
        dq = jnp.zeros((NA_QT, 128), F32)
        for hh in range(2):
            msk = (lane // NA_HEAD_DIM) == hh
            qh = jnp.where(msk, qv, 0.0).astype(BF16)
            p, l = _na_probs(qh, ks, tbl_ref, hh, npieces)
            rl = 1.0 / l
            doh = jnp.where(msk, dov, 0.0)
            dd = jnp.sum(doh * ov, axis=1, keepdims=True)
            dohb = doh.astype(BF16)
            dqh = jnp.zeros((NA_QT, 128), F32)
            for i in range(npieces):
                pn = p[i] * rl
                ds = pn * (_nt(dohb, vs[i]) - dd)
                dsb = ds.astype(BF16)
                dqh = dqh + _nn(dsb, ks[i])
                rows = pl.ds(pl.multiple_of(tok0 + i * NA_KCH, NA_KCH), NA_KCH)
                dk_acc[rows, :] += _tn(dsb, qh)
                dv_acc[rows, :] += _tn(pn.astype(BF16), dohb)
                ds_scr[:, i * NA_KCH:(i + 1) * NA_KCH] = ds
            dq = jnp.where(msk, dqh * scale, dq)
            acc = ds_scr[0:GRID_W, :]
            for i in range(1, NA_QROWS):
                acc = acc + pltpu.roll(ds_scr[i * GRID_W:(i + 1) * GRID_W, :], NA_KT - i * GRID_W, 1)
            rpb_ref[0, 0, hh] += acc
        dq_ref[...] = dq

        @pl.when(b == nb - 1)
        def _():
            cols = pl.ds(pl.multiple_of(p_id * 128, 128), 128)
            ck = pltpu.make_async_copy(dk_acc, dk_hbm.at[:, cols], sem.at[0])
            cv = pltpu.make_async_copy(dv_acc, dv_hbm.at[:, cols], sem.at[1])
            ck.start()
            cv.start()
            ck.wait()
            cv.wait()

    o512 = jax.ShapeDtypeStruct((t, NA_W), F32)
    return pl.pallas_call(
        body, name="na_bwd", grid=(4, nb),
        in_specs=[q_spec] + k_specs + v_specs + [tbl_spec, io_spec, io_spec],
        out_specs=[io_spec, pl.BlockSpec(memory_space=pl.ANY), pl.BlockSpec(memory_space=pl.ANY),
                   pl.BlockSpec((1, 1, 2, GRID_W, NA_KT), lambda p, b: (p, case(b), 0, 0, 0))],
        out_shape=[o512, o512, o512, jax.ShapeDtypeStruct((4, 3, 2, GRID_W, NA_KT), F32)],
        scratch_shapes=[pltpu.VMEM((t, 128), F32), pltpu.VMEM((t, 128), F32),
                        pltpu.VMEM((NA_QT, NA_KT), F32), pltpu.SemaphoreType.DMA((2,))],
        compiler_params=_cparams(("arbitrary", "arbitrary")),
    )(*([proj] * (1 + 2 * npieces)), tbl, d_o, o_na)


def _rpb_reduce(rpbacc, rows):
    nacc = 4 * 3 * 2

    def shift_body(a_ref, o_ref):
        acc = a_ref[0, 0:1, :]
        for cq in range(1, GRID_W):
            acc = acc + pltpu.roll(a_ref[0, cq:cq + 1, :], NA_KT - cq, 1)
        o_ref[0] = jnp.broadcast_to(acc, (8, NA_KT))

    vec = pl.pallas_call(
        shift_body, name="rpb_shift", grid=(nacc,),
        in_specs=[pl.BlockSpec((1, GRID_W, NA_KT), lambda a: (a, 0, 0))],
        out_specs=pl.BlockSpec((1, 8, NA_KT), lambda a: (a, 0, 0)),
        out_shape=jax.ShapeDtypeStruct((nacc, 8, NA_KT), F32),
        compiler_params=_cparams(("parallel",)),
    )(rpbacc.reshape(nacc, GRID_W, NA_KT))
    a = vec[:, 0].reshape(4, 3, 2, NA_KT).transpose(0, 2, 1, 3).reshape(NA_HEADS, 3, NA_KT)
    if rows // NA_QROWS < 3:
        a = a.at[:, 1].set(0.0)
    dd = np.arange(NA_KROWS)[:, None]
    dxo = np.arange(-(NA_KW - 1), NA_KW)[None, :]
    idx = ((dd * GRID_W + dxo) % NA_KT).reshape(-1)
    g = a[..., idx].reshape(NA_HEADS, 3 * NA_KROWS, 2 * NA_KW - 1)
    g = jnp.pad(g, ((0, 0), (0, 0), (0, 128 - (2 * NA_KW - 1))))
    nmat = np.zeros((16, 3 * NA_KROWS), np.float32)
    for cs, delta in enumerate((0, -(NA_KH // 2), -(NA_KROWS - NA_QROWS))):
        for d in range(NA_KROWS):
            jmi = d - NA_KROWS if (cs == 0 and d > NA_KH - 1) else d
            dy = jmi + delta + NA_KH - 1
            if 0 <= dy <= 2 * NA_KH - 2:
                nmat[dy, cs * NA_KROWS + d] = 1.0

    def body(n_ref, g_ref, o_ref):
        o_ref[0] = jnp.dot(n_ref[...], g_ref[0], precision=HI, preferred_element_type=F32)

    out = pl.pallas_call(
        body, name="rpb_reduce", grid=(NA_HEADS,),
        in_specs=[pl.BlockSpec((16, nmat.shape[1]), lambda h: (0, 0)),
                  pl.BlockSpec((1, nmat.shape[1], 128), lambda h: (h, 0, 0))],
        out_specs=pl.BlockSpec((1, 16, 128), lambda h: (h, 0, 0)),
        out_shape=jax.ShapeDtypeStruct((NA_HEADS, 16, 128), F32),
        compiler_params=_cparams(("parallel",)),
    )(jnp.asarray(nmat), g)
    return out[:, :2 * NA_KH - 1, :2 * NA_KW - 1]


def _halo_specs(tm, t, col):
    nt8 = t // 8
    per = tm // 8
    return [pl.BlockSpec((tm, 1024), lambda i: (i, col)),
            pl.BlockSpec((8, 1024), lambda i: (jnp.maximum(i * per - 1, 0), col)),
            pl.BlockSpec((8, 1024), lambda i: (jnp.minimum((i + 1) * per, nt8 - 1), col))]


def _fill_ext(ext, cur_ref, prev_ref, next_ref, tm, nt):
    i = pl.program_id(0)
    ext[0:8, :] = jnp.where(i == 0, 0.0, prev_ref[...])
    ext[8:8 + tm, :] = cur_ref[...]
    ext[8 + tm:16 + tm, :] = jnp.where(i == nt - 1, 0.0, next_ref[...])


CONV_RC = 16
CONV_CB = 512


def _conv_chunks(tm):
    return [(slice(cb, cb + CONV_CB), slice(rb, rb + CONV_RC))
            for cb in range(0, 1024, CONV_CB) for rb in range(0, tm, CONV_RC)]


def _conv_fwd(proj, conv_w8, conv_b, tm):
    t = proj.shape[0]
    nt = t // tm

    def body(u_ref, up_ref, un_ref, w_ref, b_ref, pre_ref, act_ref, ext):
        _fill_ext(ext, u_ref, up_ref, un_ref, tm, nt)
        for cs, rs in _conv_chunks(tm):
            pre = b_ref[:, cs] + w_ref[0:1, cs] * ext[pl.ds(rs.start + 6, CONV_RC), cs]
            for j in range(1, CONV_W):
                pre = pre + w_ref[j:j + 1, cs] * ext[pl.ds(rs.start + 6 + j, CONV_RC), cs]
            pre_ref[rs, cs] = pre
            act_ref[rs, cs] = _silu(pre)

    full = pl.BlockSpec((tm, 1024), lambda i: (i, 0))
    o = jax.ShapeDtypeStruct((t, 1024), F32)
    return pl.pallas_call(
        body, name="conv_fwd", grid=(nt,),
        in_specs=_halo_specs(tm, t, 2) + [pl.BlockSpec((8, 1024), lambda i: (0, 0)), _row(1024)],
        out_specs=[full, full], out_shape=[o, o],
        scratch_shapes=[pltpu.VMEM((tm + 16, 1024), F32)],
        compiler_params=_cparams(("parallel",)),
    )(proj, proj, proj, conv_w8, conv_b)


def _conv_bwd1(dq_f, dq_b, dk_f, dk_b, dv_f, dv_b, pre, tm):
    t = pre.shape[0]

    def body(dqf, dqb, dkf, dkb, dvf, dvb, pre_ref, dpre_ref, dv_ref, gb_ref):
        @pl.when(pl.program_id(0) == 0)
        def _():
            gb_ref[...] = jnp.zeros_like(gb_ref)
        ds = _dsilu(pre_ref[...])
        dpre_ref[:, 0:ML_W] = (dqf[...] + dqb[...]) * ds[:, 0:ML_W]
        dpre_ref[:, ML_W:] = (dkf[...] + dkb[...]) * ds[:, ML_W:]
        dv_ref[...] = dvf[...] + dvb[...]
        gb_ref[...] += jnp.sum(dpre_ref[...], axis=0, keepdims=True)

    half = pl.BlockSpec((tm, 512), lambda i: (i, 0))
    full = pl.BlockSpec((tm, 1024), lambda i: (i, 0))
    return pl.pallas_call(
        body, name="conv_bwd1", grid=(t // tm,),
        in_specs=[half] * 6 + [full],
        out_specs=[full, half, _row(1024)],
        out_shape=[jax.ShapeDtypeStruct((t, 1024), F32), jax.ShapeDtypeStruct((t, 512), F32),
                   jax.ShapeDtypeStruct((1, 1024), F32)],
        compiler_params=_cparams(("arbitrary",)),
    )(dq_f, dq_b, dk_f, dk_b, dv_f, dv_b, pre)


def _conv_bwd2(dpre, proj, conv_w8, tm):
    t = dpre.shape[0]
    nt = t // tm

    def body(d_ref, dp_ref, dn_ref, u_ref, up_ref, un_ref, w_ref, du_ref, gw_ref, extd, extu):
        @pl.when(pl.program_id(0) == 0)
        def _():
            gw_ref[...] = jnp.zeros_like(gw_ref)
        _fill_ext(extd, d_ref, dp_ref, dn_ref, tm, nt)
        _fill_ext(extu, u_ref, up_ref, un_ref, tm, nt)
        gacc = None
        for cs, rs in _conv_chunks(tm):
            if rs.start == 0:
                gacc = [jnp.zeros((8, CONV_CB), F32) for _ in range(CONV_W)]
            du = w_ref[0:1, cs] * extd[pl.ds(rs.start + 10, CONV_RC), cs]
            for j in range(1, CONV_W):
                du = du + w_ref[j:j + 1, cs] * extd[pl.ds(rs.start + 10 - j, CONV_RC), cs]
            du_ref[rs, cs] = du
            dcur = d_ref[rs, cs]
            for j in range(CONV_W):
                prod = dcur * extu[pl.ds(rs.start + 6 + j, CONV_RC), cs]
                gacc[j] = gacc[j] + functools.reduce(
                    jnp.add, [prod[k:k + 8] for k in range(0, CONV_RC, 8)])
            if rs.stop == tm:
                for j in range(CONV_W):
                    gw_ref[j:j + 1, cs] += jnp.sum(gacc[j], axis=0, keepdims=True)

    full = pl.BlockSpec((tm, 1024), lambda i: (i, 0))
    return pl.pallas_call(
        body, name="conv_bwd2", grid=(nt,),
        in_specs=_halo_specs(tm, t, 0) + _halo_specs(tm, t, 2) + [pl.BlockSpec((8, 1024), lambda i: (0, 0))],
        out_specs=[full, pl.BlockSpec((8, 1024), lambda i: (0, 0))],
        out_shape=[jax.ShapeDtypeStruct((t, 1024), F32), jax.ShapeDtypeStruct((8, 1024), F32)],
        scratch_shapes=[pltpu.VMEM((tm + 16, 1024), F32), pltpu.VMEM((tm + 16, 1024), F32)],
        compiler_params=_cparams(("arbitrary",)),
    )(dpre, dpre, dpre, proj, proj, proj, conv_w8)


def _ml_consts(rev):
    iu = lax.broadcasted_iota(jnp.int32, (ML_CHUNK, ML_CHUNK), 0)
    js = lax.broadcasted_iota(jnp.int32, (ML_CHUNK, ML_CHUNK), 1)
    eye = iu == js
    le = iu <= js
    ge = iu >= js
    csum, csum_t, causal = (ge, le, le) if rev else (le, ge, ge)
    return eye, csum.astype(F32), csum_t.astype(F32), causal


def _col(row, eye):
    return jnp.sum(jnp.where(eye, row, 0.0), axis=1, keepdims=True)


def _rowof(col, eye):
    return jnp.sum(jnp.where(eye, col, 0.0), axis=0, keepdims=True)


def _ml_gates(gi, gf, m0, csum, rev):
    lf = jax.nn.log_sigmoid(gf)
    b_rows = jnp.dot(lf, csum, precision=HI, preferred_element_type=F32)
    bl = jnp.sum(lf, axis=1, keepdims=True)
    a_rows = bl - b_rows + gi
    mloc = jnp.max(a_rows, axis=1, keepdims=True)
    order = list(range(ML_NB))[::-1] if rev else list(range(ML_NB))
    mp, mn, decay = {}, {}, {}
    m = m0
    for n in order:
        mp[n] = m
        m = jnp.maximum(bl[n:n + 1] + m, mloc[n:n + 1])
        mn[n] = m
        decay[n] = jnp.exp(bl[n:n + 1] + mp[n] - m)
    return b_rows, a_rows, gi - b_rows, mp, mn, decay, order


def _ml_load(q_ref, k_ref, v_ref, n):
    sl = slice(n * ML_CHUNK, (n + 1) * ML_CHUNK)
    qb = q_ref[sl, :].astype(BF16)
    kb = (k_ref[sl, :] * (ML_HEAD_DIM ** -0.5)).astype(BF16)
    vn = v_ref[sl, :]
    return sl, qb, kb, vn


def _ml_state_scan(q_ref, k_ref, v_ref, a_rows, mn, decay, order, c0, n0, eye):
    ns = range(ML_NB)
    ld = [_ml_load(q_ref, k_ref, v_ref, n) for n in ns]
    acol = [_col(a_rows[n:n + 1], eye) for n in ns]
    wcol = [jnp.exp(acol[n] - mn[n]) for n in ns]
    u = [_tn((wcol[n] * ld[n][3]).astype(BF16), ld[n][2]) for n in ns]
    nu = [jnp.sum(wcol[n] * ld[n][2].astype(F32), axis=0, keepdims=True) for n in ns]
    cp, npv = {}, {}
    c, nv = c0, n0
    for n in order:
        cp[n], npv[n] = c, nv
        c = decay[n] * c + u[n]
        nv = decay[n] * nv + nu[n]
    return cp, npv, wcol, c, nv


def _ml_intra_all(q_ref, k_ref, v_ref, b_rows, imb_rows, mp, cp, npv, causal, eye):
    ns = range(ML_NB)
    ld = [_ml_load(q_ref, k_ref, v_ref, n) for n in ns]
    qk = [_nt(ld[n][1], ld[n][2]) for n in ns]
    cq = [_nt(ld[n][1], cp[n].astype(BF16)) for n in ns]
    b_col = [_col(b_rows[n:n + 1], eye) for n in ns]
    dlog = [jnp.where(causal, b_col[n] + imb_rows[n:n + 1], NEG) for n in ns]
    m_inter = [b_col[n] + mp[n] for n in ns]
    m_t = [jnp.maximum(m_inter[n], jnp.max(dlog[n], axis=1, keepdims=True)) for n in ns]
    pm = [jnp.exp(dlog[n] - m_t[n]) for n in ns]
    inter = [jnp.exp(m_inter[n] - m_t[n]) for n in ns]
    floor = [jnp.exp(-m_t[n]) for n in ns]
    s = [qk[n] * pm[n] for n in ns]
    qn = [jnp.sum(ld[n][1].astype(F32) * npv[n].astype(BF16).astype(F32), axis=1, keepdims=True) for n in ns]
    sv = [_nn(s[n].astype(BF16), ld[n][3].astype(BF16)) for n in ns]
    den = [jnp.sum(s[n], axis=1, keepdims=True) + inter[n] * qn[n] for n in ns]
    num = [sv[n] + inter[n] * cq[n] for n in ns]
    dn = [jnp.maximum(jnp.abs(den[n]), floor[n]) for n in ns]
    return ld, [dict(pm=pm[n], s=s[n], inter=inter[n], cq=cq[n], qn=qn[n], num=num[n], den=den[n],
                     floor=floor[n], dn=dn[n]) for n in ns]


def _ml_specs(t, rev):
    nblk = t // ML_TB
    blk = (lambda g: nblk - 1 - g) if rev else (lambda g: g)
    hps = ML_HPS
    tile = lambda c0: pl.BlockSpec((ML_TB, 128 * hps), lambda hg, g, c0=c0: (blk(g), c0 // hps + hg))
    gate = pl.BlockSpec((hps, ML_NB, ML_CHUNK), lambda hg, g: (hg, blk(g), 0))
    cchk = pl.BlockSpec((hps, 1, 128, 128), lambda hg, g: (hg, blk(g), 0, 0))
    nmchk = pl.BlockSpec((hps, 1, 8, 128), lambda hg, g: (hg, blk(g), 0, 0))
    return nblk, blk, tile, gate, cchk, nmchk


def _ml_head_views(refs, hh):
    cols = slice(hh * ML_HEAD_DIM, (hh + 1) * ML_HEAD_DIM)
    return [r.at[:, cols] if len(r.shape) == 2 else r.at[hh] for r in refs]


def _ml_fwd(qk_act, proj, gi, gf, rev, name):
    t = qk_act.shape[0]
    nblk, _, tile, gate, cchk, nmchk = _ml_specs(t, rev)

    def body(*refs):
        for hh in range(ML_HPS):
            one_head(*_ml_head_views(refs, hh))

    def one_head(q_ref, k_ref, v_ref, gi_ref, gf_ref, h_ref, cchk_ref, nmchk_ref, c_ref, nm_ref):
        @pl.when(pl.program_id(1) == 0)
        def _():
            c_ref[...] = jnp.zeros_like(c_ref)
            nm_ref[...] = jnp.zeros_like(nm_ref)
        cchk_ref[0] = c_ref[...]
        nmchk_ref[0] = nm_ref[...]
        eye, csum, _, causal = _ml_consts(rev)
        b_rows, a_rows, imb_rows, mp, mn, decay, order = _ml_gates(
            gi_ref[...], gf_ref[...], nm_ref[1:2, 0:1], csum, rev)
        cp, npv, _, c, nv = _ml_state_scan(q_ref, k_ref, v_ref, a_rows, mn, decay, order,
                                            c_ref[...], nm_ref[0:1, :], eye)
        c_ref[...] = c
        nm_ref[0:1, :] = nv
        nm_ref[1:2, :] = jnp.broadcast_to(mn[order[-1]], (1, 128))
        _, rs = _ml_intra_all(q_ref, k_ref, v_ref, b_rows, imb_rows, mp, cp, npv, causal, eye)
        for n in range(ML_NB):
            h_ref[n * ML_CHUNK:(n + 1) * ML_CHUNK, :] = rs[n]['num'] / rs[n]['dn']

    return pl.pallas_call(
        body, name=name, grid=(ML_HEADS // ML_HPS, nblk),
        in_specs=[tile(0), tile(4), tile(24), gate, gate],
        out_specs=[tile(0), cchk, nmchk],
        out_shape=[jax.ShapeDtypeStruct((t, ML_W), F32),
                   jax.ShapeDtypeStruct((ML_HEADS, nblk, 128, 128), F32),
                   jax.ShapeDtypeStruct((ML_HEADS, nblk, 8, 128), F32)],
        scratch_shapes=[pltpu.VMEM((ML_HPS, 128, 128), F32), pltpu.VMEM((ML_HPS, 8, 128), F32)],
        compiler_params=_cparams(("parallel", "arbitrary")),
    )(qk_act, qk_act, proj, gi, gf)


def _ml_bwd(qk_act, proj, gi, gf, dh, cchk_a, nmchk_a, rev, name):
    t = qk_act.shape[0]
    nblk, _, tile, gate, cchk, nmchk = _ml_specs(t, not rev)

    def body(*refs):
        for hh in range(ML_HPS):
            one_head(*_ml_head_views(refs, hh))

    def one_head(q_ref, k_ref, v_ref, gi_ref, gf_ref, dh_ref, cchk_ref, nmchk_ref,
                 dq_ref, dk_ref, dv_ref, dgi_ref, dgf_ref, dc_ref, dn_ref, db_scr, dbl_scr, di_scr):
        @pl.when(pl.program_id(1) == 0)
        def _():
            dc_ref[...] = jnp.zeros_like(dc_ref)
            dn_ref[...] = jnp.zeros_like(dn_ref)
        eye, csum, csum_t, causal = _ml_consts(rev)
        gfv = gf_ref[...]
        b_rows, a_rows, imb_rows, mp, mn, decay, order = _ml_gates(
            gi_ref[...], gfv, nmchk_ref[0, 1:2, 0:1], csum, rev)
        cp, npv, wcol, _, _ = _ml_state_scan(q_ref, k_ref, v_ref, a_rows, mn, decay, order,
                                             cchk_ref[0], nmchk_ref[0, 0:1, :], eye)
        ns = range(ML_NB)
        ld, rs = _ml_intra_all(q_ref, k_ref, v_ref, b_rows, imb_rows, mp, cp, npv, causal, eye)
        sls = [ld[n][0] for n in ns]
        qbs = [ld[n][1] for n in ns]
        kbs = [ld[n][2] for n in ns]
        vbs = [ld[n][3].astype(BF16) for n in ns]
        rdn = [1.0 / rs[n]['dn'] for n in ns]
        dnum = [dh_ref[sls[n], :] * rdn[n] for n in ns]
        hsum = [jnp.sum(dnum[n] * rs[n]['num'], axis=1, keepdims=True) for n in ns]
        dden = [jnp.where(jnp.abs(rs[n]['den']) > rs[n]['floor'],
                          -hsum[n] * rdn[n] * jnp.sign(rs[n]['den']), 0.0) for n in ns]
        dnb = [dnum[n].astype(BF16) for n in ns]
        dsf = [_nt(dnb[n], vbs[n]) + dden[n] for n in ns]
        dv0 = [_tn(rs[n]['s'].astype(BF16), dnb[n]) for n in ns]
        gb = [(dsf[n] * rs[n]['pm']).astype(BF16) for n in ns]
        cpb = [cp[n].astype(BF16) for n in ns]
        idd = [rs[n]['inter'] * dden[n] for n in ns]
        dqa = [_nn(gb[n], kbs[n]) for n in ns]
        dqc = [_nn(dnb[n], cpb[n]) for n in ns]
        dk0 = [_tn(gb[n], qbs[n]) for n in ns]
        xs = [_tn((rs[n]['inter'] * dnum[n]).astype(BF16), qbs[n]) for n in ns]
        for n in ns:
            dq_ref[sls[n], :] = (dqa[n] + rs[n]['inter'] * dqc[n]
                                 + idd[n] * npv[n].astype(BF16).astype(F32))
        rr = [dsf[n] * rs[n]['s'] for n in ns]
        dinter = [jnp.sum(dnum[n] * rs[n]['cq'], axis=1, keepdims=True) + dden[n] * rs[n]['qn'] for n in ns]
        dbcol = [jnp.sum(rr[n], axis=1, keepdims=True) + dinter[n] * rs[n]['inter'] for n in ns]
        dimb = [jnp.sum(rr[n], axis=0, keepdims=True) for n in ns]
        xns = [jnp.sum(idd[n] * qbs[n].astype(F32), axis=0, keepdims=True) for n in ns]
        dcn, dnn = {}, {}
        dc, dn = dc_ref[...], dn_ref[0:1, :]
        for n in order[::-1]:
            dcn[n], dnn[n] = dc, dn
            dc = decay[n] * dc + xs[n]
            dn = decay[n] * dn + xns[n]
        dc_ref[...] = dc
        dn_ref[0:1, :] = dn
        kscale = ML_HEAD_DIM ** -0.5
        dcb = [dcn[n].astype(BF16) for n in ns]
        z = [_nn(vbs[n], dcb[n]) for n in ns]
        kd = [_nt(kbs[n], dcb[n]) for n in ns]
        ddecay = [jnp.sum(jnp.sum(dcn[n] * cp[n], axis=1, keepdims=True), axis=0, keepdims=True)
                  + jnp.sum(dnn[n] * npv[n], axis=1, keepdims=True) for n in ns]
        zd = [z[n] + dnn[n] for n in ns]
        dw = [jnp.sum(zd[n] * kbs[n].astype(F32), axis=1, keepdims=True) for n in ns]
        for n in ns:
            dv_ref[sls[n], :] = dv0[n] + wcol[n] * kd[n]
            dk_ref[sls[n], :] = (dk0[n] + wcol[n] * zd[n]) * kscale
        da = [dw[n] * wcol[n] for n in ns]
        dbl = [jnp.sum(da[n], axis=0, keepdims=True) + ddecay[n] * decay[n] for n in ns]
        da_row = [_rowof(da[n], eye) for n in ns]
        db_row = [_rowof(dbcol[n] - da[n], eye) for n in ns]
        for n in ns:
            db_scr[n:n + 1, :] = db_row[n] - dimb[n]
            di_scr[n:n + 1, :] = dimb[n] + da_row[n]
            dbl_scr[n:n + 1, :] = jnp.broadcast_to(dbl[n], (1, ML_CHUNK))
        dlf = jnp.dot(db_scr[...], csum_t, precision=HI, preferred_element_type=F32) + dbl_scr[...]
        dgf_ref[...] = dlf * jax.nn.sigmoid(-gfv)
        dgi_ref[...] = di_scr[...]

    nc = t // ML_CHUNK
    o512 = jax.ShapeDtypeStruct((t, ML_W), F32)
    og = jax.ShapeDtypeStruct((ML_HEADS, nc, ML_CHUNK), F32)
    return pl.pallas_call(
        body, name=name, grid=(ML_HEADS // ML_HPS, nblk),
        in_specs=[tile(0), tile(4), tile(24), gate, gate, tile(0), cchk, nmchk],
        out_specs=[tile(0), tile(0), tile(0), gate, gate],
        out_shape=[o512, o512, o512, og, og],
        scratch_shapes=[pltpu.VMEM((ML_HPS, 128, 128), F32), pltpu.VMEM((ML_HPS, 8, 128), F32)]
        + [pltpu.VMEM((ML_HPS, ML_NB, ML_CHUNK), F32)] * 3,
        compiler_params=_cparams(("parallel", "arbitrary")),
    )(qk_act, qk_act, proj, gi, gf, dh, cchk_a, nmchk_a)


def _gate_rows(gates16, t):
    g = gates16.reshape(t // ML_CHUNK, ML_CHUNK, 4, ML_HEADS).transpose(2, 3, 0, 1)
    return g[0], g[1], g[2], g[3]


def _gate_cols(dgi_f, dgf_f, dgi_b, dgf_b, t):
    g = jnp.stack([dgi_f, dgf_f, dgi_b, dgf_b]).transpose(2, 3, 0, 1).reshape(t, 4 * ML_HEADS)
    return jnp.pad(g, ((0, 0), (0, 128 - 4 * ML_HEADS)))


def _local_step(x, target, shift, scale, gate, norm_w, w_in_t, b_in_p, conv_w8, conv_b, rpb,
                ml_norm_w, w_out_b, final_norm_w):
    t = x.shape[0]
    rows = t // GRID_W
    tm = 512
    proj, gates = _in_proj(x, norm_w, scale, shift, w_in_t, b_in_p)
    tbl = _na_bias_table(rpb, rows)
    o_na = _na_fwd(proj, tbl)
    pre, qk_act = _conv_fwd(proj, conv_w8, conv_b, tm)
    gi_f, gf_f, gi_b, gf_b = _gate_rows(gates[:, :4 * ML_HEADS], t)
    h_f, cchk_f, nmchk_f = _ml_fwd(qk_act, proj, gi_f, gf_f, False, "ml_fwd_f")
    h_b, cchk_b, nmchk_b = _ml_fwd(qk_act, proj, gi_b, gf_b, True, "ml_fwd_b")
    (loss, dres, d_ona, d_naz, dhs, d_o, d_z, dgate, g_fnw, g_mlnw, g_w_out) = _tail(
        o_na, proj, h_f, h_b, x, target, gate, ml_norm_w, final_norm_w, w_out_b)
    dq_na, dk_na, dv_na, rpbacc = _na_bwd(proj, tbl, d_ona, o_na)
    g_rpb = _rpb_reduce(rpbacc, rows)
    dq_f, dk_f, dv_f, dgi_f, dgf_f = _ml_bwd(qk_act, proj, gi_f, gf_f, dhs, cchk_f, nmchk_f, False, "ml_bwd_f")
    dq_b, dk_b, dv_b, dgi_b, dgf_b = _ml_bwd(qk_act, proj, gi_b, gf_b, dhs, cchk_b, nmchk_b, True, "ml_bwd_b")
    dpre, dv_ml, g_conv_b = _conv_bwd1(dq_f, dq_b, dk_f, dk_b, dv_f, dv_b, pre, tm)
    du, g_conv_w = _conv_bwd2(dpre, proj, conv_w8, tm)
    dgates = _gate_cols(dgi_f, dgf_f, dgi_b, dgf_b, t)
    grad_x, g_w_in, g_b_in, dscale, dshift, g_nw = _in_bwd(
        [dq_na, dk_na, dv_na, d_naz, du, dv_ml, d_o, d_z, dgates], x, dres, w_in_t, norm_w, scale, shift)
    dmod = jnp.concatenate([dshift, dscale, dgate], axis=1)
    return (loss, grad_x, dmod, g_nw, g_w_in, g_b_in, g_conv_w, g_conv_b, g_rpb, g_mlnw, g_w_out, g_fnw)


MESH = pl.DeviceIdType.MESH
N_DEV = 8
ANY = pl.BlockSpec(memory_space=pl.ANY)
WHOLE_VMEM = pl.BlockSpec(memory_space=pltpu.VMEM)


def _allgather8(blocks, name):
    na = len(blocks)

    def body(*refs):
        x_refs = refs[:na]
        out_refs = refs[na:2 * na]
        send_sems, recv_sems, local_sems = refs[2 * na:]
        x, y, c = lax.axis_index("x"), lax.axis_index("y"), lax.axis_index("c")
        me, sibling = (x, y, c), (x, y, 1 - c)
        chips = [(1 - x, y), (x, 1 - y), (1 - x, 1 - y)]

        def rows(a, px, py, pc):
            return out_refs[a].at[4 * px + 2 * py + pc]

        def copy(a, k, block, to, src=None):
            return pltpu.make_async_remote_copy(
                src_ref=rows(a, *block) if src is None else src, dst_ref=rows(a, *block),
                send_sem=send_sems.at[a, k], recv_sem=recv_sems.at[a, k],
                device_id=to, device_id_type=MESH)

        mine, first, passed = [], [], []
        for a in range(na):
            cp = pltpu.make_async_copy(x_refs[a], rows(a, *me), local_sems.at[a])
            cp.start()
            mine.append(cp)
            first.append(copy(a, 0, me, sibling, src=x_refs[a]))
            first += [copy(a, 1 + j, me, (*chip, c), src=x_refs[a]) for j, chip in enumerate(chips)]
        for cp in first:
            cp.start()
        for a in range(na):
            for j, chip in enumerate(chips):
                copy(a, 1 + j, (*chip, c), me).wait_recv()
                fwd = copy(a, 4 + j, (*chip, c), sibling)
                fwd.start()
                passed.append(fwd)
        for a in range(na):
            copy(a, 0, sibling, me).wait_recv()
            for j, chip in enumerate(chips):
                copy(a, 4 + j, (*chip, 1 - c), me).wait_recv()
        for cp in first + passed:
            cp.wait_send()
        for cp in mine:
            cp.wait()

    return pl.pallas_call(
        body, name=name,
        out_shape=[jax.ShapeDtypeStruct((N_DEV,) + b.shape, b.dtype) for b in blocks],
        in_specs=[WHOLE_VMEM] * na, out_specs=[WHOLE_VMEM] * na,
        scratch_shapes=[pltpu.SemaphoreType.DMA((na, 7)), pltpu.SemaphoreType.DMA((na, 7)),
                        pltpu.SemaphoreType.DMA((na,))],
        compiler_params=pltpu.CompilerParams(vmem_limit_bytes=VMEM_LIMIT),
    )(*blocks)


def _pair_exchange(arrs, name):
    na = len(arrs)

    def body(*refs):
        in_refs = refs[:na]
        out_refs = refs[na:2 * na]
        send_sems, recv_sems = refs[2 * na:]
        sibling = (lax.axis_index("x"), lax.axis_index("y"), 1 - lax.axis_index("c"))
        copies = [pltpu.make_async_remote_copy(
            src_ref=in_refs[a], dst_ref=out_refs[a], send_sem=send_sems.at[a], recv_sem=recv_sems.at[a],
            device_id=sibling, device_id_type=MESH) for a in range(na)]
        for cp in copies:
            cp.start()
        for cp in copies:
            cp.wait()

    return pl.pallas_call(
        body, name=name,
        out_shape=[jax.ShapeDtypeStruct(a.shape, a.dtype) for a in arrs],
        in_specs=[ANY] * na, out_specs=[ANY] * na,
        scratch_shapes=[pltpu.SemaphoreType.DMA((na,)), pltpu.SemaphoreType.DMA((na,))],
    )(*arrs)


def _chip_exchange(arrs, name):
    na = len(arrs)

    def body(*refs):
        in_refs = refs[:na]
        out_refs = refs[na:2 * na]
        send_sems, recv_sems, local_sems = refs[2 * na:]
        x, y, c = lax.axis_index("x"), lax.axis_index("y"), lax.axis_index("c")
        my_chip = 2 * x + y
        chips = [(1 - x, y), (x, 1 - y), (1 - x, 1 - y)]
        local, remote = [], []
        for a in range(na):
            cp = pltpu.make_async_copy(in_refs[a].at[my_chip], out_refs[a].at[my_chip], local_sems.at[a])
            cp.start()
            local.append(cp)
            for j, (px, py) in enumerate(chips):
                cp = pltpu.make_async_remote_copy(
                    src_ref=in_refs[a].at[2 * px + py], dst_ref=out_refs[a].at[my_chip],
                    send_sem=send_sems.at[a, j], recv_sem=recv_sems.at[a, j],
                    device_id=(px, py, c), device_id_type=MESH)
                cp.start()
                remote.append(cp)
        for cp in remote:
            cp.wait()
        for cp in local:
            cp.wait()

    return pl.pallas_call(
        body, name=name,
        out_shape=[jax.ShapeDtypeStruct(a.shape, a.dtype) for a in arrs],
        in_specs=[ANY] * na, out_specs=[ANY] * na,
        scratch_shapes=[pltpu.SemaphoreType.DMA((na, 3)), pltpu.SemaphoreType.DMA((na, 3)),
                        pltpu.SemaphoreType.DMA((na,))],
    )(*arrs)


def _rows_tile(r):
    for cand in (512, 256, 128, 64, 32, 16, 8):
        if r % cand == 0:
            return cand
    return r


def _add2(a, b, name):
    s, r, n = a.shape
    tr = _rows_tile(r)

    def body(a_ref, b_ref, o_ref):
        o_ref[...] = a_ref[...] + b_ref[...]

    spec = pl.BlockSpec((1, tr, n), lambda i, j: (i, j, 0))
    return pl.pallas_call(
        body, name=name, grid=(s, r // tr), in_specs=[spec, spec], out_specs=spec,
        out_shape=jax.ShapeDtypeStruct(a.shape, a.dtype),
        compiler_params=_cparams(("parallel", "parallel")),
    )(a, b)


def _sum_slabs(a, name):
    s, r, n = a.shape
    tr = _rows_tile(r)

    def body(a_ref, o_ref):
        acc = a_ref[0]
        for k in range(1, s):
            acc = acc + a_ref[k]
        o_ref[...] = acc

    return pl.pallas_call(
        body, name=name, grid=(r // tr,),
        in_specs=[pl.BlockSpec((s, tr, n), lambda i: (0, i, 0))],
        out_specs=pl.BlockSpec((tr, n), lambda i: (i, 0)),
        out_shape=jax.ShapeDtypeStruct((r, n), a.dtype),
        compiler_params=_cparams(("parallel",)),
    )(a)


def _adamw(w, g, m, v, name):
    r, n = w.shape
    if r % 8 == 0:
        blk, grid, imap = (_rows_tile(r), n), (r // _rows_tile(r),), (lambda i: (i, 0))
    else:
        blk, grid, imap = (r, 128), (n // 128,), (lambda i: (0, i))
    c1 = 1.0 / (1.0 - ADAM_B1 ** ADAM_STEP)
    c2 = 1.0 / (1.0 - ADAM_B2 ** ADAM_STEP)

    def body(w_ref, g_ref, m_ref, v_ref, d_ref, nm_ref, nv_ref):
        gv = g_ref[...]
        nm = ADAM_B1 * m_ref[...] + (1.0 - ADAM_B1) * gv
        nv = ADAM_B2 * v_ref[...] + (1.0 - ADAM_B2) * (gv * gv)
        nm_ref[...] = nm
        nv_ref[...] = nv
        d_ref[...] = -ADAM_LR * ((nm * c1) / (jnp.sqrt(nv * c2) + ADAM_EPS) + ADAM_WD * w_ref[...])

    spec = pl.BlockSpec(blk, imap)
    o = jax.ShapeDtypeStruct((r, n), F32)
    return pl.pallas_call(
        body, name=name, grid=grid, in_specs=[spec] * 4, out_specs=[spec] * 3, out_shape=[o, o, o],
        compiler_params=_cparams(("parallel",)),
    )(w, g, m, v)


def _mod_fwd(c_all, w_ada_s, b_ada_s):
    def body(c_ref, w_ref, b_ref, o_ref):
        o_ref[...] = jnp.dot(_silu(c_ref[...]), w_ref[...], precision=HI, preferred_element_type=F32) + b_ref[...]

    return pl.pallas_call(
        body, name="mod_fwd", out_shape=jax.ShapeDtypeStruct((c_all.shape[0], w_ada_s.shape[1]), F32),
        in_specs=[WHOLE_VMEM] * 3, out_specs=WHOLE_VMEM,
        compiler_params=pltpu.CompilerParams(vmem_limit_bytes=VMEM_LIMIT),
    )(c_all, w_ada_s, b_ada_s)


def _wada_grad(c_all, dmod_s):
    def body(c_ref, d_ref, o_ref):
        o_ref[...] = lax.dot_general(_silu(c_ref[...]), d_ref[...], (((0,), (0,)), ((), ())),
                                     precision=HI, preferred_element_type=F32)

    return pl.pallas_call(
        body, name="w_ada_grad", out_shape=jax.ShapeDtypeStruct((c_all.shape[1], dmod_s.shape[1]), F32),
        in_specs=[WHOLE_VMEM] * 2, out_specs=WHOLE_VMEM,
        compiler_params=pltpu.CompilerParams(vmem_limit_bytes=VMEM_LIMIT),
    )(c_all, dmod_s)


SMALL_ROWS = 24


def _pad_rows(v, nrows):
    v = v.reshape(-1)
    return jnp.pad(v, (0, nrows * 1024 - v.shape[0])).reshape(nrows, 1024)


def _pack_small(b_ada, norm_w, b_in, conv_w_full, conv_b, rpb, ml_norm_w, final_norm_w, last):
    parts = [_pad_rows(b_ada, 3), _pad_rows(norm_w, 1), _pad_rows(b_in, 5), _pad_rows(conv_w_full, 5),
             _pad_rows(conv_b, 1), _pad_rows(rpb, 4), _pad_rows(ml_norm_w, 1), _pad_rows(final_norm_w, 1),
             _pad_rows(last, 3)]
    return jnp.concatenate(parts, axis=0)


def _unpack_small(p):
    return dict(b_ada=p[0:3].reshape(1, 3072), norm_w=p[3:4], b_in=p[4:9].reshape(-1)[:IN_W].reshape(1, IN_W),
                conv_w=p[9:14], conv_b=p[14:15],
                rpb=p[15:19].reshape(-1)[:NA_HEADS * 15 * 31].reshape(1, NA_HEADS, 15, 31),
                ml_norm_w=p[19:20, :ML_W], final_norm_w=p[20], last=p[21])


def kernel(x, c, w_ada, b_ada, norm_w, w_in, b_in, conv_w, conv_b, rpb, ml_norm_w, w_out, final_norm_w, loss_target, m_w_ada, m_b_ada, m_norm_w, m_w_in, m_b_in, m_conv_w, m_conv_b, m_rpb, m_ml_norm_w, m_w_out, m_final_norm_w, v_w_ada, v_b_ada, v_norm_w, v_w_in, v_b_in, v_conv_w, v_conv_b, v_rpb, v_ml_norm_w, v_w_out, v_final_norm_w):
    xi, yi, ci = lax.axis_index("x"), lax.axis_index("y"), lax.axis_index("c")
    chip = 2 * xi + yi
    dev = 2 * chip + ci
    t = x.shape[1]
    ada_n = w_ada.shape[2]
    in_n = w_in.shape[2]
    out_r = w_out.shape[1]

    c_blk = jnp.pad(c, ((0, 7), (0, 0)))
    w_in_t, m_w_in_t, v_w_in_t = w_in[0].T, m_w_in[0].T, v_w_in[0].T
    in_h = in_n // 2
    w_in_half = lax.dynamic_slice_in_dim(w_in_t, ci * in_h, in_h, axis=0).astype(BF16)
    w_out_half = lax.dynamic_slice_in_dim(w_out[0], ci * (out_r // 2), out_r // 2, axis=0).astype(BF16)
    conv_blk = jnp.pad(conv_w[0], ((0, 3), (0, 0)))
    c_g, conv_g, w_in_g, w_out_g = _allgather8([c_blk, conv_blk, w_in_half, w_out_half], "gather_c_weights")
    c_all = c_g[:, 0]
    w_out_g = w_out_g.reshape(D_MODEL, D_MODEL)
    b_ada_s = lax.dynamic_slice_in_dim(b_ada, chip * ada_n, ada_n, axis=1)
    mod_s = _mod_fwd(c_all, w_ada[0], b_ada_s)
    (mod_g,) = _allgather8([mod_s], "gather_mod")
    mod_mine = lax.dynamic_index_in_dim(mod_g, dev, axis=1, keepdims=False)
    mod = mod_mine[0::2].reshape(1, 3 * D_MODEL)
    shift, scale, gate = mod[:, :D_MODEL], mod[:, D_MODEL:2 * D_MODEL], mod[:, 2 * D_MODEL:]

    w_in_tp = jnp.pad(w_in_g.reshape(IN_W, D_MODEL), ((0, IN_PAD - IN_W), (0, 0)))
    b_in_p = jnp.pad(b_in, ((0, 0), (0, IN_PAD - IN_W)))
    conv_w8 = conv_g.reshape(4, 2, 8, conv_w.shape[2])[:, 0].transpose(1, 0, 2).reshape(8, D_MODEL)

    (loss, grad_x, dmod, g_nw, g_w_in, g_b_in, g_conv_w, g_conv_b, g_rpb, g_mlnw, g_w_out, g_fnw) = _local_step(
        x[0], loss_target[0], shift, scale, gate, norm_w, w_in_tp, b_in_p, conv_w8, conv_b, rpb[0],
        ml_norm_w, w_out_g, final_norm_w.reshape(1, D_MODEL))

    gi4 = g_w_in[:, :IN_W].T.reshape(4, 2, in_h, D_MODEL)
    go4 = g_w_out.reshape(4, 2, out_r // 2, D_MODEL)
    pick = lambda a, k: lax.dynamic_index_in_dim(a, k, axis=1, keepdims=False)
    ri, ro = _pair_exchange([pick(gi4, 1 - ci), pick(go4, 1 - ci)], "rs_pair")
    pi = _add2(pick(gi4, ci), ri, "rs_pair_add_in")
    po = _add2(pick(go4, ci), ro, "rs_pair_add_out")
    qi, qo = _chip_exchange([pi, po], "rs_chips")
    si = _sum_slabs(qi, "rs_sum_in")
    so = _sum_slabs(qo, "rs_sum_out")
    ti, to = _pair_exchange([si, so], "rs_share")
    g_w_in_s = jnp.where(ci == 0, jnp.concatenate([si, ti], axis=0), jnp.concatenate([ti, si], axis=0))
    g_w_out_s = jnp.where(ci == 0, jnp.concatenate([so, to], axis=0), jnp.concatenate([to, so], axis=0))

    small = _pack_small(dmod, g_nw, g_b_in[:, :IN_W], g_conv_w[:CONV_W], g_conv_b, g_rpb, g_mlnw, g_fnw,
                        jnp.pad(loss, ((0, 0), (0, 1024 - 128))))
    (small_g,) = _allgather8([small], "gather_small")
    small_sum = _sum_slabs(small_g, "small_sum")
    gs = _unpack_small(small_sum)
    dmod_all = small_g[:, 0:3].reshape(N_DEV, 3 * D_MODEL)
    g_w_ada_s = _wada_grad(c_all, lax.dynamic_slice_in_dim(dmod_all, chip * ada_n, ada_n, axis=1))
    g_conv_w_s = lax.dynamic_slice_in_dim(gs['conv_w'], chip * conv_w.shape[2], conv_w.shape[2], axis=1)
    loss_total = gs['last'][0]

    zeros3 = jnp.zeros((3, 1024), F32)
    zc = jnp.zeros((CONV_W, D_MODEL), F32)
    pw = _pack_small(b_ada, norm_w, b_in, zc, conv_b, rpb, ml_norm_w, final_norm_w, zeros3)
    pm = _pack_small(m_b_ada, m_norm_w, m_b_in, zc, m_conv_b, m_rpb, m_ml_norm_w, m_final_norm_w, zeros3)
    pv = _pack_small(v_b_ada, v_norm_w, v_b_in, zc, v_conv_b, v_rpb, v_ml_norm_w, v_final_norm_w, zeros3)
    ds_, nms, nvs = [_unpack_small(a) for a in _adamw(pw, small_sum, pm, pv, "adamw_small")]
    d_ada, nm_ada, nv_ada = _adamw(w_ada[0], g_w_ada_s, m_w_ada[0], v_w_ada[0], "adamw_w_ada")
    d_in, nm_in, nv_in = _adamw(w_in_t, g_w_in_s, m_w_in_t, v_w_in_t, "adamw_w_in")
    d_out, nm_out, nv_out = _adamw(w_out[0], g_w_out_s, m_w_out[0], v_w_out[0], "adamw_w_out")
    d_cw, nm_cw, nv_cw = _adamw(conv_w[0], g_conv_w_s, m_conv_w[0], v_conv_w[0], "adamw_conv_w")

    def group(big_ada, big_in, big_out, cw, sm):
        return (big_ada[None], sm['b_ada'], sm['norm_w'], big_in.T[None], sm['b_in'], cw[None], sm['conv_b'],
                sm['rpb'], sm['ml_norm_w'], big_out[None], sm['final_norm_w'])

    return ((loss_total, grad_x[None])
            + group(g_w_ada_s, g_w_in_s, g_w_out_s, g_conv_w_s, gs)
            + group(d_ada, d_in, d_out, d_cw, ds_)
            + group(nm_ada, nm_in, nm_out, nm_cw, nms)
            + group(nv_ada, nv_in, nv_out, nv_cw, nvs))
```

```python
import functools

import numpy as np
import jax
import jax.numpy as jnp
from jax import lax
from jax.experimental import pallas as pl
from jax.experimental.pallas import tpu as pltpu

F32 = jnp.float32
BF16 = jnp.bfloat16
HI = lax.Precision.HIGHEST

D_MODEL = 1024
GRID_W = 64
NA_W = 512
NA_HEAD_DIM = 64
NA_HEADS = 8
NA_KH = 8
NA_KW = 16
ML_W = 512
ML_HEADS = 4
ML_HEAD_DIM = 128
ML_CHUNK = 128
CONV_W = 5
EPS = 1e-6
IN_W = 4 * NA_W + 5 * ML_W + 4 * ML_HEADS
IN_MAIN = 4 * NA_W + 5 * ML_W
IN_PAD = IN_MAIN + 128
NEG = -1e30

ADAM_LR = 0.001
ADAM_B1 = 0.9
ADAM_B2 = 0.999
ADAM_EPS = 1e-08
ADAM_WD = 0.01
ADAM_STEP = 10

NA_QROWS = 8
NA_KROWS = 16
NA_QT = NA_QROWS * GRID_W
NA_KT = NA_KROWS * GRID_W
NA_KCH = 256
ML_NB = 8
ML_TB = ML_NB * ML_CHUNK
ML_HPS = 1

VMEM_LIMIT = 56 * 1024 * 1024


def _cparams(sem, vmem=VMEM_LIMIT):
    return pltpu.CompilerParams(dimension_semantics=sem, vmem_limit_bytes=vmem)


def _silu(x):
    return x * jax.nn.sigmoid(x)


def _dsilu(x):
    s = jax.nn.sigmoid(x)
    return s * (1.0 + x * (1.0 - s))


def _dot(a, b, dims):
    return lax.dot_general(a, b, (dims, ((), ())), preferred_element_type=F32)


def _nn(a, b):
    return _dot(a, b, ((1,), (0,)))


def _nt(a, b):
    return _dot(a, b, ((1,), (1,)))


def _tn(a, b):
    return _dot(a, b, ((0,), (0,)))


def _row(n):
    return pl.BlockSpec((1, n), lambda i: (0, 0))


def _modulated_norm(xv, nw, sc, sh):
    r = lax.rsqrt(jnp.mean(xv * xv, axis=-1, keepdims=True) + EPS)
    xn = xv * r
    return xn * nw * (1.0 + sc) + sh, xn, r


IN_TN = 1536


def _in_proj(x, norm_w, scale, shift, w_in_t, b_in_p):
    t, d = x.shape
    tm = 1024
    gcol = IN_MAIN // 128

    def body(x_ref, nw_ref, sc_ref, sh_ref, w_ref, b_ref, wg_ref, bg_ref, proj_ref, g_ref, h_scr):
        @pl.when(pl.program_id(1) == 0)
        def _():
            h, _, _ = _modulated_norm(x_ref[...], nw_ref[...], sc_ref[...], sh_ref[...])
            h_scr[...] = h.astype(BF16)
            g_ref[...] = _nt(h_scr[...], wg_ref[...]) + bg_ref[...]
        proj_ref[...] = _nt(h_scr[...], w_ref[...]) + b_ref[...]

    row = lambda n: pl.BlockSpec((1, n), lambda i, j: (0, 0))
    return pl.pallas_call(
        body, name="in_proj", grid=(t // tm, IN_MAIN // IN_TN),
        in_specs=[pl.BlockSpec((tm, d), lambda i, j: (i, 0)), row(d), row(d), row(d),
                  pl.BlockSpec((IN_TN, d), lambda i, j: (j, 0)), pl.BlockSpec((1, IN_TN), lambda i, j: (0, j)),
                  pl.BlockSpec((128, d), lambda i, j: (gcol, 0)), pl.BlockSpec((1, 128), lambda i, j: (0, gcol))],
        out_specs=[pl.BlockSpec((tm, IN_TN), lambda i, j: (i, j)), pl.BlockSpec((tm, 128), lambda i, j: (i, 0))],
        out_shape=[jax.ShapeDtypeStruct((t, IN_MAIN), F32), jax.ShapeDtypeStruct((t, 128), F32)],
        scratch_shapes=[pltpu.VMEM((tm, d), BF16)],
        compiler_params=_cparams(("parallel", "arbitrary")),
    )(x, norm_w, scale, shift, w_in_t, b_in_p, w_in_t, b_in_p)


def _ml_norm_parts(hs, o, z, nw):
    outs = []
    for hh in range(ML_HEADS):
        sl = slice(hh * ML_HEAD_DIM, (hh + 1) * ML_HEAD_DIM)
        hm = hs[:, sl] * jax.nn.sigmoid(o[:, sl])
        mu = jnp.mean(hm, axis=-1, keepdims=True)
        cen = hm - mu
        var = jnp.mean(cen * cen, axis=-1, keepdims=True)
        rs = lax.rsqrt(var + EPS)
        outs.append((sl, cen * rs, rs))
    return outs


def _tail(o_na, proj, h_f, h_b, x, target, gate, ml_norm_w, fnw, w_out_b):
    t, d = x.shape
    tm = 256

    def body(ona_ref, naz_ref, hf_ref, hb_ref, o_ref, z_ref, x_ref, tg_ref, g_ref, nw_ref, fw_ref, w_ref,
             loss_ref, dres_ref, dona_ref, dnaz_ref, dhs_ref, do_ref, dz_ref, dgate_ref, gfw_ref, gnw_ref,
             gwo_ref, mix_scr):
        @pl.when(pl.program_id(0) == 0)
        def _():
            for r in (loss_ref, dgate_ref, gfw_ref, gnw_ref, gwo_ref):
                r[...] = jnp.zeros_like(r)
        naz = naz_ref[...]
        ona = ona_ref[...]
        sna = _silu(naz)
        mix_scr[:, 0:NA_W] = (ona * sna).astype(BF16)
        hs = hf_ref[...] + hb_ref[...]
        z = z_ref[...]
        ov = o_ref[...]
        parts = _ml_norm_parts(hs, ov, z, nw_ref[...])
        szs = []
        for sl, xn, _ in parts:
            sz = _silu(z[:, sl])
            szs.append(sz)
            mix_scr[:, NA_W + sl.start:NA_W + sl.stop] = (xn * nw_ref[:, sl] * sz).astype(BF16)
        mixb = mix_scr[...]
        wv = w_ref[...]
        yv = _nn(mixb, wv)
        gate_v = g_ref[...]
        hres = x_ref[...] + gate_v * yv
        r = lax.rsqrt(jnp.mean(hres * hres, axis=-1, keepdims=True) + EPS)
        xnf = hres * r
        err = xnf * fw_ref[...] - tg_ref[...]
        loss_ref[...] += 0.5 * jnp.sum(jnp.sum(err * err, axis=-1, keepdims=True) * (1.0 / d), axis=0, keepdims=True)
        dout = err * (1.0 / d)
        gfw_ref[...] += jnp.sum(dout * xnf, axis=0, keepdims=True)
        dxn = dout * fw_ref[...]
        dres = r * (dxn - xnf * jnp.mean(dxn * xnf, axis=-1, keepdims=True))
        dres_ref[...] = dres
        dgate_ref[...] += jnp.sum(dres * yv, axis=0, keepdims=True)
        dyb = (dres * gate_v).astype(BF16)
        gwo_ref[...] += _tn(mixb, dyb)
        dmix = _nt(dyb, wv)
        dna = dmix[:, 0:NA_W]
        dona_ref[...] = dna * sna
        dnaz_ref[...] = dna * ona * _dsilu(naz)
        for (sl, xn, rs), sz in zip(parts, szs):
            dyv = dmix[:, NA_W + sl.start:NA_W + sl.stop]
            zz = z[:, sl]
            w = nw_ref[:, sl]
            dz_ref[:, sl] = dyv * xn * w * _dsilu(zz)
            gnw_ref[:, sl] += jnp.sum(dyv * xn * sz, axis=0, keepdims=True)
            dxm = dyv * w * sz
            dhm = rs * (dxm - jnp.mean(dxm, axis=-1, keepdims=True)
                        - xn * jnp.mean(dxm * xn, axis=-1, keepdims=True))
            so = jax.nn.sigmoid(ov[:, sl])
            dhs_ref[:, sl] = dhm * so
            do_ref[:, sl] = dhm * hs[:, sl] * so * (1.0 - so)

    blk = lambda c: pl.BlockSpec((tm, 512), lambda i, c=c: (i, c))
    full = pl.BlockSpec((tm, d), lambda i: (i, 0))
    o512 = jax.ShapeDtypeStruct((t, 512), F32)
    whole = pl.BlockSpec((d, d), lambda i: (0, 0))
    return pl.pallas_call(
        body, name="tail", grid=(t // tm,),
        in_specs=[blk(0), blk(3), blk(0), blk(0), blk(7), blk(8), full, full, _row(d), _row(ML_W), _row(d), whole],
        out_specs=[pl.BlockSpec((1, 128), lambda i: (0, 0)), full] + [blk(0)] * 5
        + [_row(d), _row(d), _row(ML_W), whole],
        out_shape=[jax.ShapeDtypeStruct((1, 128), F32), jax.ShapeDtypeStruct((t, d), F32)] + [o512] * 5
        + [jax.ShapeDtypeStruct((1, d), F32), jax.ShapeDtypeStruct((1, d), F32),
           jax.ShapeDtypeStruct((1, ML_W), F32), jax.ShapeDtypeStruct((d, d), F32)],
        scratch_shapes=[pltpu.VMEM((tm, d), BF16)],
        compiler_params=_cparams(("arbitrary",)),
    )(o_na, proj, h_f, h_b, proj, proj, x, target, gate, ml_norm_w, fnw, w_out_b)


def _in_bwd(pieces, x, dres, w_in_t, norm_w, scale, shift):
    t, d = x.shape
    tm = 256
    nt = t // tm
    widths = [p.shape[1] for p in pieces]
    offs = [sum(widths[:k]) for k in range(len(widths))]
    assert sum(widths) == IN_PAD
    npc = len(pieces)

    def body(*refs):
        p_refs = refs[:npc]
        (x_ref, dres_ref, w_hbm, nw_ref, sc_ref, sh_ref,
         gx_ref, gw_hbm, gb_ref, dsc_ref, dsh_ref, gnw_ref, w_vmem, acc, sem) = refs[npc:]
        i = pl.program_id(0)

        @pl.when(i == 0)
        def _():
            cp = pltpu.make_async_copy(w_hbm, w_vmem, sem.at[0])
            cp.start()
            acc[...] = jnp.zeros_like(acc)
            gb_ref[...] = jnp.zeros_like(gb_ref)
            dsc_ref[...] = jnp.zeros_like(dsc_ref)
            dsh_ref[...] = jnp.zeros_like(dsh_ref)
            gnw_ref[...] = jnp.zeros_like(gnw_ref)
            cp.wait()

        nw = nw_ref[...]
        s1 = 1.0 + sc_ref[...]
        h, xn, r = _modulated_norm(x_ref[...], nw, sc_ref[...], sh_ref[...])
        hb = h.astype(BF16)
        dhv = jnp.zeros((tm, d), F32)
        for p_ref, c0, w in zip(p_refs, offs, widths):
            pt = p_ref[...]
            pb = pt.astype(BF16)
            dhv = dhv + _nn(pb, w_vmem[c0:c0 + w, :])
            acc[:, c0:c0 + w] += _tn(hb, pb)
            gb_ref[:, c0:c0 + w] += jnp.sum(pt, axis=0, keepdims=True)
        dsh_ref[...] += jnp.sum(dhv, axis=0, keepdims=True)
        dsc_ref[...] += jnp.sum(dhv * xn * nw, axis=0, keepdims=True)
        gnw_ref[...] += jnp.sum(dhv * xn * s1, axis=0, keepdims=True)
        dxn = dhv * nw * s1
        gx_ref[...] = dres_ref[...] + r * (dxn - xn * jnp.mean(dxn * xn, axis=-1, keepdims=True))

        @pl.when(i == nt - 1)
        def _():
            cp = pltpu.make_async_copy(acc, gw_hbm, sem.at[1])
            cp.start()
            cp.wait()

    full = pl.BlockSpec((tm, d), lambda i: (i, 0))
    return pl.pallas_call(
        body, name="in_bwd", grid=(nt,),
        in_specs=[pl.BlockSpec((tm, w), lambda i: (i, 0)) for w in widths]
        + [full, full, pl.BlockSpec(memory_space=pl.ANY), _row(d), _row(d), _row(d)],
        out_specs=[full, pl.BlockSpec(memory_space=pl.ANY), _row(IN_PAD), _row(d), _row(d), _row(d)],
        out_shape=[jax.ShapeDtypeStruct((t, d), F32), jax.ShapeDtypeStruct((d, IN_PAD), F32),
                   jax.ShapeDtypeStruct((1, IN_PAD), F32)] + [jax.ShapeDtypeStruct((1, d), F32)] * 3,
        scratch_shapes=[pltpu.VMEM((IN_PAD, d), BF16), pltpu.VMEM((d, IN_PAD), F32),
                        pltpu.SemaphoreType.DMA((2,))],
        compiler_params=_cparams(("arbitrary",)),
    )(*pieces, x, dres, w_in_t, norm_w, scale, shift)


def _na_static(rows):
    cases = [(0, 0), (NA_QROWS, NA_QROWS - 4), (rows - NA_QROWS, rows - NA_KROWS)]
    dy = np.zeros((3, NA_QROWS, NA_KROWS), np.int32)
    rv = np.zeros((3, NA_QROWS, NA_KROWS), bool)
    for cs, (r0, kr0) in enumerate(cases):
        for i in range(NA_QROWS):
            for j in range(NA_KROWS):
                r, kr = r0 + i, kr0 + j
                rs = min(max(r - NA_KH // 2, 0), rows - NA_KH)
                rv[cs, i, j] = rs <= kr <= rs + NA_KH - 1
                dy[cs, i, j] = min(max(kr - r + NA_KH - 1, 0), 2 * NA_KH - 2)
    cq = np.arange(GRID_W)[:, None]
    ck = np.arange(GRID_W)[None, :]
    cs0 = np.clip(cq - NA_KW // 2, 0, GRID_W - NA_KW)
    cv = (ck >= cs0) & (ck < cs0 + NA_KW)
    dx = np.clip(ck - cq, -(NA_KW - 1), NA_KW - 1) + NA_KW - 1
    return dy, rv, dx.astype(np.int32), cv


def _na_bias_table(rpb, rows):
    _, _, dx, cv = _na_static(rows)
    ndy = 2 * NA_KH - 1
    onehot = (dx.reshape(1, -1) == np.arange(2 * NA_KW - 1)[:, None]).astype(np.float32)
    rpx = jnp.dot(rpb.reshape(NA_HEADS * ndy, 2 * NA_KW - 1), jnp.asarray(onehot), precision=HI)
    rpx = jnp.where(cv[None, None], rpx.reshape(NA_HEADS, ndy, GRID_W, GRID_W), NEG)
    neg = jnp.full((NA_HEADS, 1, GRID_W, GRID_W), NEG, F32)
    rpx = jnp.concatenate([rpx, neg], axis=1)
    nxt = jnp.concatenate([rpx[:, 1:], neg], axis=1)
    negs = jnp.broadcast_to(neg, rpx.shape)
    pairs = jnp.concatenate([jnp.concatenate([rpx, nxt], axis=3), jnp.concatenate([rpx, negs], axis=3),
                             jnp.concatenate([negs, rpx], axis=3)], axis=1)
    npair = pairs.shape[1]

    def body(m_ref, o_ref):
        cs = pl.program_id(1)
        r0 = jnp.where(cs == 0, 0, jnp.where(cs == 1, NA_QROWS, rows - NA_QROWS))
        kr0 = jnp.where(cs == 0, 0, jnp.where(cs == 1, NA_QROWS - NA_KH // 2, rows - NA_KROWS))
        for i in range(NA_QROWS):
            r = r0 + i
            rs = jnp.clip(r - NA_KH // 2, 0, rows - NA_KH)
            for jp in range(NA_KROWS // 2):
                kl = kr0 + 2 * jp
                vl = (kl >= rs) & (kl <= rs + NA_KH - 1)
                vr = (kl + 1 >= rs) & (kl + 1 <= rs + NA_KH - 1)
                dyl = jnp.clip(kl - r + NA_KH - 1, 0, ndy)
                dyr = jnp.clip(kl + 1 - r + NA_KH - 1, 0, ndy)
                idx = jnp.where(vl & vr, dyl, jnp.where(vl, 16 + dyl, jnp.where(vr, 32 + dyr, 16 + ndy)))
                o_ref[0, 0, i * GRID_W:(i + 1) * GRID_W, jp * 128:(jp + 1) * 128] = m_ref[0, idx]

    return pl.pallas_call(
        body, name="na_bias_table", grid=(NA_HEADS, 3),
        in_specs=[pl.BlockSpec((1, npair, GRID_W, 128), lambda h, cs: (h, 0, 0, 0))],
        out_specs=pl.BlockSpec((1, 1, NA_QT, NA_KT), lambda h, cs: (h, cs, 0, 0)),
        out_shape=jax.ShapeDtypeStruct((NA_HEADS, 3, NA_QT, NA_KT), F32),
        compiler_params=_cparams(("parallel", "parallel")),
    )(pairs)


def _na_specs(t):
    nb = t // NA_QT
    nkb = t // NA_KCH
    npieces = NA_KT // NA_KCH

    def kb0(b):
        return jnp.clip(b * (NA_QT // NA_KCH) - 1, 0, nkb - npieces)

    def case(b):
        return jnp.where(b == 0, 0, jnp.where(b == nb - 1, 2, 1))

    q_spec = pl.BlockSpec((NA_QT, 128), lambda p, b: (b, p))
    k_specs = [pl.BlockSpec((NA_KCH, 128), lambda p, b, i=i: (kb0(b) + i, 4 + p)) for i in range(npieces)]
    v_specs = [pl.BlockSpec((NA_KCH, 128), lambda p, b, i=i: (kb0(b) + i, 8 + p)) for i in range(npieces)]
    tbl_spec = pl.BlockSpec((2, 1, NA_QT, NA_KT), lambda p, b: (p, case(b), 0, 0))
    io_spec = pl.BlockSpec((NA_QT, 128), lambda p, b: (b, p))
    return nb, npieces, kb0, case, q_spec, k_specs, v_specs, tbl_spec, io_spec


def _na_probs(qh, ks, tbl_ref, hh, npieces):
    s = [_nt(qh, ks[i]) + tbl_ref[hh, 0, :, i * NA_KCH:(i + 1) * NA_KCH] for i in range(npieces)]
    m = functools.reduce(jnp.maximum, [jnp.max(si, axis=1, keepdims=True) for si in s])
    p = [jnp.exp(si - m) for si in s]
    l = functools.reduce(jnp.add, [jnp.sum(pi, axis=1, keepdims=True) for pi in p])
    return p, m, l


def _na_fwd(proj, tbl):
    t = proj.shape[0]
    nb, npieces, _, _, q_spec, k_specs, v_specs, tbl_spec, io_spec = _na_specs(t)
    lse_spec = pl.BlockSpec((1, NA_QT, 2), lambda p, b: (p, b, 0))

    def body(*refs):
        q_ref = refs[0]
        k_refs = refs[1:1 + npieces]
        v_refs = refs[1 + npieces:1 + 2 * npieces]
        tbl_ref, o_ref, lse_ref = refs[1 + 2 * npieces:]
        lane = lax.broadcasted_iota(jnp.int32, (1, 128), 1)
        qv = q_ref[...] * (NA_HEAD_DIM ** -0.5)
        ks = [r[...].astype(BF16) for r in k_refs]
        vs = [r[...].astype(BF16) for r in v_refs]
        hs = range(2)
        msk = [(lane // NA_HEAD_DIM) == hh for hh in hs]
        qh = [jnp.where(msk[hh], qv, 0.0).astype(BF16) for hh in hs]
        pml = [_na_probs(qh[hh], ks, tbl_ref, hh, npieces) for hh in hs]
        pb = [[pml[hh][0][i].astype(BF16) for i in range(npieces)] for hh in hs]
        o = [functools.reduce(jnp.add, [_nn(pb[hh][i], vs[i]) for i in range(npieces)]) for hh in hs]
        for hh in hs:
            lse_ref[0, :, hh:hh + 1] = pml[hh][1] + jnp.log(pml[hh][2])
        o_ref[...] = jnp.where(msk[0], o[0] / pml[0][2], o[1] / pml[1][2])

    return pl.pallas_call(
        body, name="na_fwd", grid=(4, nb),
        in_specs=[q_spec] + k_specs + v_specs + [tbl_spec],
        out_specs=[io_spec, lse_spec],
        out_shape=[jax.ShapeDtypeStruct((t, NA_W), F32), jax.ShapeDtypeStruct((4, t, 2), F32)],
        compiler_params=_cparams(("parallel", "arbitrary")),
    )(*([proj] * (1 + 2 * npieces)), tbl)


def _na_bwd(proj, tbl, d_o, o_na, lse):
    t = proj.shape[0]
    nb, npieces, kb0, case, q_spec, k_specs, v_specs, tbl_spec, io_spec = _na_specs(t)

    def body(*refs):
        q_ref = refs[0]
        k_refs = refs[1:1 + npieces]
        v_refs = refs[1 + npieces:1 + 2 * npieces]
        (tbl_ref, do_ref, o_ref, lse_ref, dq_ref, dk_hbm, dv_hbm, rpb_ref,
         dk_acc, dv_acc, ds_scr, sem) = refs[1 + 2 * npieces:]
        p_id = pl.program_id(0)
        b = pl.program_id(1)

        @pl.when(b == 0)
        def _():
            dk_acc[...] = jnp.zeros_like(dk_acc)
            dv_acc[...] = jnp.zeros_like(dv_acc)

        @pl.when((b == 0) | (b == 1) | (b == nb - 1))
        def _():
            rpb_ref[...] = jnp.zeros_like(rpb_ref)

        lane = lax.broadcasted_iota(jnp.int32, (1, 128), 1)
        scale = NA_HEAD_DIM ** -0.5
        qv = q_ref[...] * scale
        ks = [r[...].astype(BF16) for r in k_refs]
        vs = [r[...].astype(BF16) for r in v_refs]
        dov = do_ref[...]
        ov = o_ref[...]
        tok0 = kb0(b) * NA_KCH
        dq = jnp.zeros((NA_QT, 128), F32)
        for hh in range(2):
            msk = (lane // NA_HEAD_DIM) == hh
            qh = jnp.where(msk, qv, 0.0).astype(BF16)
            doh = jnp.where(msk, dov, 0.0)
            dd = jnp.sum(doh * ov, axis=1, keepdims=True)
            dohb = doh.astype(BF16)
            lse = lse_ref[0, :, hh:hh + 1]
            ns = range(npieces)
            pn = [jnp.exp(_nt(qh, ks[i]) + tbl_ref[hh, 0, :, i * NA_KCH:(i + 1) * NA_KCH] - lse) for i in ns]
            dp = [_nt(dohb, vs[i]) for i in ns]
            ds = [pn[i] * (dp[i] - dd) for i in ns]
            dsb = [ds[i].astype(BF16) for i in ns]
            pnb = [pn[i].astype(BF16) for i in ns]
            dqh = functools.reduce(jnp.add, [_nn(dsb[i], ks[i]) for i in ns])
            for i in ns:
                rows = pl.ds(pl.multiple_of(tok0 + i * NA_KCH, NA_KCH), NA_KCH)
                dk_acc[rows, :] += _tn(dsb[i], qh)
                dv_acc[rows, :] += _tn(pnb[i], dohb)
                ds_scr[:, i * NA_KCH:(i + 1) * NA_KCH] = ds[i]
            dq = jnp.where(msk, dqh * scale, dq)
            acc = ds_scr[0:GRID_W, :]
            for i in range(1, NA_QROWS):
                acc = acc + pltpu.roll(ds_scr[i * GRID_W:(i + 1) * GRID_W, :], NA_KT - i * GRID_W, 1)
            rpb_ref[0, 0, hh] += acc
        dq_ref[...] = dq

        @pl.when(b == nb - 1)
        def _():
            cols = pl.ds(pl.multiple_of(p_id * 128, 128), 128)
            ck = pltpu.make_async_copy(dk_acc, dk_hbm.at[:, cols], sem.at[0])
            cv = pltpu.make_async_copy(dv_acc, dv_hbm.at[:, cols], sem.at[1])
            ck.start()
            cv.start()
            ck.wait()
            cv.wait()

    o512 = jax.ShapeDtypeStruct((t, NA_W), F32)
    return pl.pallas_call(
        body, name="na_bwd", grid=(4, nb),
        in_specs=[q_spec] + k_specs + v_specs + [tbl_spec, io_spec, io_spec,
                                                 pl.BlockSpec((1, NA_QT, 2), lambda p, b: (p, b, 0))],
        out_specs=[io_spec, pl.BlockSpec(memory_space=pl.ANY), pl.BlockSpec(memory_space=pl.ANY),
                   pl.BlockSpec((1, 1, 2, GRID_W, NA_KT), lambda p, b: (p, case(b), 0, 0, 0))],
        out_shape=[o512, o512, o512, jax.ShapeDtypeStruct((4, 3, 2, GRID_W, NA_KT), F32)],
        scratch_shapes=[pltpu.VMEM((t, 128), F32), pltpu.VMEM((t, 128), F32),
                        pltpu.VMEM((NA_QT, NA_KT), F32), pltpu.SemaphoreType.DMA((2,))],
        compiler_params=_cparams(("arbitrary", "arbitrary")),
    )(*([proj] * (1 + 2 * npieces)), tbl, d_o, o_na, lse)


def _rpb_reduce(rpbacc, rows):
    nacc = 4 * 3 * 2

    def shift_body(a_ref, o_ref):
        acc = a_ref[0, 0:1, :]
        for cq in range(1, GRID_W):
            acc = acc + pltpu.roll(a_ref[0, cq:cq + 1, :], NA_KT - cq, 1)
        o_ref[0] = jnp.broadcast_to(acc, (8, NA_KT))

    vec = pl.pallas_call(
        shift_body, name="rpb_shift", grid=(nacc,),
        in_specs=[pl.BlockSpec((1, GRID_W, NA_KT), lambda a: (a, 0, 0))],
        out_specs=pl.BlockSpec((1, 8, NA_KT), lambda a: (a, 0, 0)),
        out_shape=jax.ShapeDtypeStruct((nacc, 8, NA_KT), F32),
        compiler_params=_cparams(("parallel",)),
    )(rpbacc.reshape(nacc, GRID_W, NA_KT))
    a = vec[:, 0].reshape(4, 3, 2, NA_KT).transpose(0, 2, 1, 3).reshape(NA_HEADS, 3, NA_KT)
    if rows // NA_QROWS < 3:
        a = a.at[:, 1].set(0.0)
    dd = np.arange(NA_KROWS)[:, None]
    dxo = np.arange(-(NA_KW - 1), NA_KW)[None, :]
    idx = ((dd * GRID_W + dxo) % NA_KT).reshape(-1)
    g = a[..., idx].reshape(NA_HEADS, 3 * NA_KROWS, 2 * NA_KW - 1)
    g = jnp.pad(g, ((0, 0), (0, 0), (0, 128 - (2 * NA_KW - 1))))
    nmat = np.zeros((16, 3 * NA_KROWS), np.float32)
    for cs, delta in enumerate((0, -(NA_KH // 2), -(NA_KROWS - NA_QROWS))):
        for d in range(NA_KROWS):
            jmi = d - NA_KROWS if (cs == 0 and d > NA_KH - 1) else d
            dy = jmi + delta + NA_KH - 1
            if 0 <= dy <= 2 * NA_KH - 2:
                nmat[dy, cs * NA_KROWS + d] = 1.0

    def body(n_ref, g_ref, o_ref):
        o_ref[0] = jnp.dot(n_ref[...], g_ref[0], precision=HI, preferred_element_type=F32)

    out = pl.pallas_call(
        body, name="rpb_reduce", grid=(NA_HEADS,),
        in_specs=[pl.BlockSpec((16, nmat.shape[1]), lambda h: (0, 0)),
                  pl.BlockSpec((1, nmat.shape[1], 128), lambda h: (h, 0, 0))],
        out_specs=pl.BlockSpec((1, 16, 128), lambda h: (h, 0, 0)),
        out_shape=jax.ShapeDtypeStruct((NA_HEADS, 16, 128), F32),
        compiler_params=_cparams(("parallel",)),
    )(jnp.asarray(nmat), g)
    return out[:, :2 * NA_KH - 1, :2 * NA_KW - 1]


def _halo_specs(tm, t, col, width=1024):
    nt8 = t // 8
    per = tm // 8
    return [pl.BlockSpec((tm, width), lambda i: (i, col)),
            pl.BlockSpec((8, width), lambda i: (jnp.maximum(i * per - 1, 0), col)),
            pl.BlockSpec((8, width), lambda i: (jnp.minimum((i + 1) * per, nt8 - 1), col))]


def _fill_ext(ext, cur_ref, prev_ref, next_ref, tm, nt):
    i = pl.program_id(0)
    ext[0:8, :] = jnp.where(i == 0, 0.0, prev_ref[...])
    ext[8:8 + tm, :] = cur_ref[...]
    ext[8 + tm:16 + tm, :] = jnp.where(i == nt - 1, 0.0, next_ref[...])


CONV_RC = 16
CONV_CB = 512


def _conv_chunks(tm):
    return [(slice(cb, cb + CONV_CB), slice(rb, rb + CONV_RC))
            for cb in range(0, 1024, CONV_CB) for rb in range(0, tm, CONV_RC)]


def _conv_fwd(proj, conv_w8, conv_b, tm):
    t = proj.shape[0]
    nt = t // tm

    def body(u_ref, up_ref, un_ref, w_ref, b_ref, pre_ref, act_ref, ext):
        _fill_ext(ext, u_ref, up_ref, un_ref, tm, nt)
        for cs, rs in _conv_chunks(tm):
            pre = b_ref[:, cs] + w_ref[0:1, cs] * ext[pl.ds(rs.start + 6, CONV_RC), cs]
            for j in range(1, CONV_W):
                pre = pre + w_ref[j:j + 1, cs] * ext[pl.ds(rs.start + 6 + j, CONV_RC), cs]
            pre_ref[rs, cs] = pre
            act_ref[rs, cs] = _silu(pre)

    full = pl.BlockSpec((tm, 1024), lambda i: (i, 0))
    o = jax.ShapeDtypeStruct((t, 1024), F32)
    return pl.pallas_call(
        body, name="conv_fwd", grid=(nt,),
        in_specs=_halo_specs(tm, t, 2) + [pl.BlockSpec((8, 1024), lambda i: (0, 0)), _row(1024)],
        out_specs=[full, full], out_shape=[o, o],
        scratch_shapes=[pltpu.VMEM((tm + 16, 1024), F32)],
        compiler_params=_cparams(("parallel",)),
    )(proj, proj, proj, conv_w8, conv_b)


def _conv_bwd(dq, dk, pre, proj, conv_w8, tm):
    t = pre.shape[0]
    nt = t // tm

    def body(dq_ref, dqp_ref, dqn_ref, dk_ref, dkp_ref, dkn_ref, pre_ref, prep_ref, pren_ref,
             u_ref, up_ref, un_ref, w_ref, du_ref, gw_ref, gb_ref, extd, extu):
        i = pl.program_id(0)

        @pl.when(i == 0)
        def _():
            gw_ref[...] = jnp.zeros_like(gw_ref)
            gb_ref[...] = jnp.zeros_like(gb_ref)
        for rows, dqr, dkr, prr, edge in ((slice(0, 8), dqp_ref, dkp_ref, prep_ref, i == 0),
                                          (slice(8, 8 + tm), dq_ref, dk_ref, pre_ref, None),
                                          (slice(8 + tm, 16 + tm), dqn_ref, dkn_ref, pren_ref, i == nt - 1)):
            ds = _dsilu(prr[...])
            dl = dqr[...] * ds[:, 0:ML_W]
            dr = dkr[...] * ds[:, ML_W:]
            if edge is not None:
                dl = jnp.where(edge, 0.0, dl)
                dr = jnp.where(edge, 0.0, dr)
            extd[rows, 0:ML_W] = dl
            extd[rows, ML_W:] = dr
        _fill_ext(extu, u_ref, up_ref, un_ref, tm, nt)
        gb_ref[...] += jnp.sum(extd[8:8 + tm, :], axis=0, keepdims=True)
        gacc = None
        for cs, rs in _conv_chunks(tm):
            if rs.start == 0:
                gacc = [jnp.zeros((8, CONV_CB), F32) for _ in range(CONV_W)]
            du = w_ref[0:1, cs] * extd[pl.ds(rs.start + 10, CONV_RC), cs]
            for j in range(1, CONV_W):
                du = du + w_ref[j:j + 1, cs] * extd[pl.ds(rs.start + 10 - j, CONV_RC), cs]
            du_ref[rs, cs] = du
            dcur = extd[pl.ds(rs.start + 8, CONV_RC), cs]
            for j in range(CONV_W):
                prod = dcur * extu[pl.ds(rs.start + 6 + j, CONV_RC), cs]
                gacc[j] = gacc[j] + functools.reduce(
                    jnp.add, [prod[k:k + 8] for k in range(0, CONV_RC, 8)])
            if rs.stop == tm:
                for j in range(CONV_W):
                    gw_ref[j:j + 1, cs] += jnp.sum(gacc[j], axis=0, keepdims=True)

    full = pl.BlockSpec((tm, 1024), lambda i: (i, 0))
    return pl.pallas_call(
        body, name="conv_bwd", grid=(nt,),
        in_specs=_halo_specs(tm, t, 0, ML_W) + _halo_specs(tm, t, 0, ML_W) + _halo_specs(tm, t, 0)
        + _halo_specs(tm, t, 2) + [pl.BlockSpec((8, 1024), lambda i: (0, 0))],
        out_specs=[full, pl.BlockSpec((8, 1024), lambda i: (0, 0)), _row(1024)],
        out_shape=[jax.ShapeDtypeStruct((t, 1024), F32), jax.ShapeDtypeStruct((8, 1024), F32),
                   jax.ShapeDtypeStruct((1, 1024), F32)],
        scratch_shapes=[pltpu.VMEM((tm + 16, 1024), F32), pltpu.VMEM((tm + 16, 1024), F32)],
        compiler_params=_cparams(("arbitrary",)),
    )(dq, dq, dq, dk, dk, dk, pre, pre, pre, proj, proj, proj, conv_w8)


def _ml_consts(rev):
    iu = lax.broadcasted_iota(jnp.int32, (ML_CHUNK, ML_CHUNK), 0)
    js = lax.broadcasted_iota(jnp.int32, (ML_CHUNK, ML_CHUNK), 1)
    eye = iu == js
    le = iu <= js
    ge = iu >= js
    csum, csum_t, causal = (ge, le, le) if rev else (le, ge, ge)
    return eye, csum.astype(F32), csum_t.astype(F32), causal


def _col(row, eye):
    return jnp.sum(jnp.where(eye, row, 0.0), axis=1, keepdims=True)


def _rowof(col, eye):
    return jnp.sum(jnp.where(eye, col, 0.0), axis=0, keepdims=True)


def _ml_gates(gi, gf, m0, csum, rev):
    lf = jax.nn.log_sigmoid(gf)
    b_rows = jnp.dot(lf, csum, precision=HI, preferred_element_type=F32)
    bl = jnp.sum(lf, axis=1, keepdims=True)
    a_rows = bl - b_rows + gi
    mloc = jnp.max(a_rows, axis=1, keepdims=True)
    order = list(range(ML_NB))[::-1] if rev else list(range(ML_NB))
    mp, mn, decay = {}, {}, {}
    m = m0
    for n in order:
        mp[n] = m
        m = jnp.maximum(bl[n:n + 1] + m, mloc[n:n + 1])
        mn[n] = m
    for n in order:
        decay[n] = jnp.exp(bl[n:n + 1] + mp[n] - mn[n])
    return b_rows, a_rows, gi - b_rows, mp, mn, decay, order


def _ml_load(q_ref, k_ref, v_ref, n):
    sl = slice(n * ML_CHUNK, (n + 1) * ML_CHUNK)
    qb = q_ref[sl, :].astype(BF16)
    kb = (k_ref[sl, :] * (ML_HEAD_DIM ** -0.5)).astype(BF16)
    vn = v_ref[sl, :]
    return sl, qb, kb, vn


def _ml_state_scan(q_ref, k_ref, v_ref, a_rows, mn, decay, order, c0, n0, eye):
    ns = range(ML_NB)
    ld = [_ml_load(q_ref, k_ref, v_ref, n) for n in ns]
    acol = [_col(a_rows[n:n + 1], eye) for n in ns]
    wcol = [jnp.exp(acol[n] - mn[n]) for n in ns]
    u = [_tn((wcol[n] * ld[n][3]).astype(BF16), ld[n][2]) for n in ns]
    nu = [jnp.sum(wcol[n] * ld[n][2].astype(F32), axis=0, keepdims=True) for n in ns]
    cp, npv = {}, {}
    c, nv = c0, n0
    for n in order:
        cp[n], npv[n] = c, nv
        c = decay[n] * c + u[n]
        nv = decay[n] * nv + nu[n]
    return cp, npv, wcol, c, nv


def _ml_intra_all(q_ref, k_ref, v_ref, b_rows, imb_rows, mp, cp, npv, causal, eye):
    ns = range(ML_NB)
    ld = [_ml_load(q_ref, k_ref, v_ref, n) for n in ns]
    qk = [_nt(ld[n][1], ld[n][2]) for n in ns]
    cq = [_nt(ld[n][1], cp[n].astype(BF16)) for n in ns]
    b_col = [_col(b_rows[n:n + 1], eye) for n in ns]
    dlog = [jnp.where(causal, b_col[n] + imb_rows[n:n + 1], NEG) for n in ns]
    m_inter = [b_col[n] + mp[n] for n in ns]
    m_t = [jnp.maximum(m_inter[n], jnp.max(dlog[n], axis=1, keepdims=True)) for n in ns]
    pm = [jnp.exp(dlog[n] - m_t[n]) for n in ns]
    inter = [jnp.exp(m_inter[n] - m_t[n]) for n in ns]
    floor = [jnp.exp(-m_t[n]) for n in ns]
    s = [qk[n] * pm[n] for n in ns]
    qn = [jnp.sum(ld[n][1].astype(F32) * npv[n].astype(BF16).astype(F32), axis=1, keepdims=True) for n in ns]
    sv = [_nn(s[n].astype(BF16), ld[n][3].astype(BF16)) for n in ns]
    den = [jnp.sum(s[n], axis=1, keepdims=True) + inter[n] * qn[n] for n in ns]
    num = [sv[n] + inter[n] * cq[n] for n in ns]
    dn = [jnp.maximum(jnp.abs(den[n]), floor[n]) for n in ns]
    return ld, [dict(pm=pm[n], s=s[n], inter=inter[n], cq=cq[n], qn=qn[n], num=num[n], den=den[n],
                     floor=floor[n], dn=dn[n]) for n in ns]


def _ml_specs(t, rev):
    nblk = t // ML_TB
    blk = (lambda g: nblk - 1 - g) if rev else (lambda g: g)
    hps = ML_HPS
    tile = lambda c0: pl.BlockSpec((ML_TB, 128 * hps), lambda hg, g, c0=c0: (blk(g), c0 // hps + hg))
    gate = pl.BlockSpec((hps, ML_NB, ML_CHUNK), lambda hg, g: (hg, blk(g), 0))
    cchk = pl.BlockSpec((hps, 1, 128, 128), lambda hg, g: (hg, blk(g), 0, 0))
    nmchk = pl.BlockSpec((hps, 1, 8, 128), lambda hg, g: (hg, blk(g), 0, 0))
    return nblk, blk, tile, gate, cchk, nmchk


def _ml_head_views(refs, hh):
    cols = slice(hh * ML_HEAD_DIM, (hh + 1) * ML_HEAD_DIM)
    return [r.at[:, cols] if len(r.shape) == 2 else r.at[hh] for r in refs]


def _ml_fwd(qk_act, proj, gi, gf, rev, name):
    t = qk_act.shape[0]
    nblk, _, tile, gate, cchk, nmchk = _ml_specs(t, rev)

    def body(*refs):
        for hh in range(ML_HPS):
            one_head(*_ml_head_views(refs, hh))

    def one_head(q_ref, k_ref, v_ref, gi_ref, gf_ref, h_ref, cchk_ref, nmchk_ref, c_ref, nm_ref):
        @pl.when(pl.program_id(1) == 0)
        def _():
            c_ref[...] = jnp.zeros_like(c_ref)
            nm_ref[...] = jnp.zeros_like(nm_ref)
        cchk_ref[0] = c_ref[...]
        nmchk_ref[0] = nm_ref[...]
        eye, csum, _, causal = _ml_consts(rev)
        b_rows, a_rows, imb_rows, mp, mn, decay, order = _ml_gates(
            gi_ref[...], gf_ref[...], nm_ref[1:2, 0:1], csum, rev)
        cp, npv, _, c, nv = _ml_state_scan(q_ref, k_ref, v_ref, a_rows, mn, decay, order,
                                            c_ref[...], nm_ref[0:1, :], eye)
        c_ref[...] = c
        nm_ref[0:1, :] = nv
        nm_ref[1:2, :] = jnp.broadcast_to(mn[order[-1]], (1, 128))
        _, rs = _ml_intra_all(q_ref, k_ref, v_ref, b_rows, imb_rows, mp, cp, npv, causal, eye)
        for n in range(ML_NB):
            h_ref[n * ML_CHUNK:(n + 1) * ML_CHUNK, :] = rs[n]['num'] / rs[n]['dn']

    return pl.pallas_call(
        body, name=name, grid=(ML_HEADS // ML_HPS, nblk),
        in_specs=[tile(0), tile(4), tile(24), gate, gate],
        out_specs=[tile(0), cchk, nmchk],
        out_shape=[jax.ShapeDtypeStruct((t, ML_W), F32),
                   jax.ShapeDtypeStruct((ML_HEADS, nblk, 128, 128), F32),
                   jax.ShapeDtypeStruct((ML_HEADS, nblk, 8, 128), F32)],
        scratch_shapes=[pltpu.VMEM((ML_HPS, 128, 128), F32), pltpu.VMEM((ML_HPS, 8, 128), F32)],
        compiler_params=_cparams(("parallel", "arbitrary")),
    )(qk_act, qk_act, proj, gi, gf)


def _ml_bwd(qk_act, proj, gi, gf, dh, cchk_a, nmchk_a, prev, rev, name):
    t = qk_act.shape[0]
    nblk, _, tile, gate, cchk, nmchk = _ml_specs(t, not rev)

    def body(*refs):
        for hh in range(ML_HPS):
            one_head(*_ml_head_views(refs, hh))

    def one_head(q_ref, k_ref, v_ref, gi_ref, gf_ref, dh_ref, cchk_ref, nmchk_ref, *rest):
        prev_refs = rest[:len(prev)]
        dq_ref, dk_ref, dv_ref, dgi_ref, dgf_ref, dc_ref, dn_ref, db_scr, dbl_scr, di_scr = rest[len(prev):]

        def plus_prev(val, which, rows):
            return val + prev_refs[which][rows, :] if prev else val

        @pl.when(pl.program_id(1) == 0)
        def _():
            dc_ref[...] = jnp.zeros_like(dc_ref)
            dn_ref[...] = jnp.zeros_like(dn_ref)
        eye, csum, csum_t, causal = _ml_consts(rev)
        gfv = gf_ref[...]
        b_rows, a_rows, imb_rows, mp, mn, decay, order = _ml_gates(
            gi_ref[...], gfv, nmchk_ref[0, 1:2, 0:1], csum, rev)
        cp, npv, wcol, _, _ = _ml_state_scan(q_ref, k_ref, v_ref, a_rows, mn, decay, order,
                                             cchk_ref[0], nmchk_ref[0, 0:1, :], eye)
        ns = range(ML_NB)
        ld, rs = _ml_intra_all(q_ref, k_ref, v_ref, b_rows, imb_rows, mp, cp, npv, causal, eye)
        sls = [ld[n][0] for n in ns]
        qbs = [ld[n][1] for n in ns]
        kbs = [ld[n][2] for n in ns]
        vbs = [ld[n][3].astype(BF16) for n in ns]
        rdn = [1.0 / rs[n]['dn'] for n in ns]
        dnum = [dh_ref[sls[n], :] * rdn[n] for n in ns]
        hsum = [jnp.sum(dnum[n] * rs[n]['num'], axis=1, keepdims=True) for n in ns]
        dden = [jnp.where(jnp.abs(rs[n]['den']) > rs[n]['floor'],
                          -hsum[n] * rdn[n] * jnp.sign(rs[n]['den']), 0.0) for n in ns]
        dnb = [dnum[n].astype(BF16) for n in ns]
        dsf = [_nt(dnb[n], vbs[n]) + dden[n] for n in ns]
        dv0 = [_tn(rs[n]['s'].astype(BF16), dnb[n]) for n in ns]
        gb = [(dsf[n] * rs[n]['pm']).astype(BF16) for n in ns]
        cpb = [cp[n].astype(BF16) for n in ns]
        idd = [rs[n]['inter'] * dden[n] for n in ns]
        dqa = [_nn(gb[n], kbs[n]) for n in ns]
        dqc = [_nn(dnb[n], cpb[n]) for n in ns]
        dk0 = [_tn(gb[n], qbs[n]) for n in ns]
        xs = [_tn((rs[n]['inter'] * dnum[n]).astype(BF16), qbs[n]) for n in ns]
        for n in ns:
            dq_ref[sls[n], :] = plus_prev(dqa[n] + rs[n]['inter'] * dqc[n]
                                          + idd[n] * npv[n].astype(BF16).astype(F32), 0, sls[n])
        rr = [dsf[n] * rs[n]['s'] for n in ns]
        dinter = [jnp.sum(dnum[n] * rs[n]['cq'], axis=1, keepdims=True) + dden[n] * rs[n]['qn'] for n in ns]
        dbcol = [jnp.sum(rr[n], axis=1, keepdims=True) + dinter[n] * rs[n]['inter'] for n in ns]
        dimb = [jnp.sum(rr[n], axis=0, keepdims=True) for n in ns]
        xns = [jnp.sum(idd[n] * qbs[n].astype(F32), axis=0, keepdims=True) for n in ns]
        dcn, dnn = {}, {}
        dc, dn = dc_ref[...], dn_ref[0:1, :]
        for n in order[::-1]:
            dcn[n], dnn[n] = dc, dn
            dc = decay[n] * dc + xs[n]
            dn = decay[n] * dn + xns[n]
        dc_ref[...] = dc
        dn_ref[0:1, :] = dn
        kscale = ML_HEAD_DIM ** -0.5
        dcb = [dcn[n].astype(BF16) for n in ns]
        z = [_nn(vbs[n], dcb[n]) for n in ns]
        kd = [_nt(kbs[n], dcb[n]) for n in ns]
        ddecay = [jnp.sum(jnp.sum(dcn[n] * cp[n], axis=1, keepdims=True), axis=0, keepdims=True)
                  + jnp.sum(dnn[n] * npv[n], axis=1, keepdims=True) for n in ns]
        zd = [z[n] + dnn[n] for n in ns]
        dw = [jnp.sum(zd[n] * kbs[n].astype(F32), axis=1, keepdims=True) for n in ns]
        for n in ns:
            dv_ref[sls[n], :] = plus_prev(dv0[n] + wcol[n] * kd[n], 2, sls[n])
            dk_ref[sls[n], :] = plus_prev((dk0[n] + wcol[n] * zd[n]) * kscale, 1, sls[n])
        da = [dw[n] * wcol[n] for n in ns]
        dbl = [jnp.sum(da[n], axis=0, keepdims=True) + ddecay[n] * decay[n] for n in ns]
        da_row = [_rowof(da[n], eye) for n in ns]
        db_row = [_rowof(dbcol[n] - da[n], eye) for n in ns]
        for n in ns:
            db_scr[n:n + 1, :] = db_row[n] - dimb[n]
            di_scr[n:n + 1, :] = dimb[n] + da_row[n]
            dbl_scr[n:n + 1, :] = jnp.broadcast_to(dbl[n], (1, ML_CHUNK))
        dlf = jnp.dot(db_scr[...], csum_t, precision=HI, preferred_element_type=F32) + dbl_scr[...]
        dgf_ref[...] = dlf * jax.nn.sigmoid(-gfv)
        dgi_ref[...] = di_scr[...]

    nc = t // ML_CHUNK
    o512 = jax.ShapeDtypeStruct((t, ML_W), F32)
    og = jax.ShapeDtypeStruct((ML_HEADS, nc, ML_CHUNK), F32)
    return pl.pallas_call(
        body, name=name, grid=(ML_HEADS // ML_HPS, nblk),
        in_specs=[tile(0), tile(4), tile(24), gate, gate, tile(0), cchk, nmchk] + [tile(0)] * len(prev),
        out_specs=[tile(0), tile(0), tile(0), gate, gate],
        out_shape=[o512, o512, o512, og, og],
        scratch_shapes=[pltpu.VMEM((ML_HPS, 128, 128), F32), pltpu.VMEM((ML_HPS, 8, 128), F32)]
        + [pltpu.VMEM((ML_HPS, ML_NB, ML_CHUNK), F32)] * 3,
        compiler_params=_cparams(("parallel", "arbitrary")),
    )(qk_act, qk_act, proj, gi, gf, dh, cchk_a, nmchk_a, *prev)


def _gate_rows(gates16, t):
    g = gates16.reshape(t // ML_CHUNK, ML_CHUNK, 4, ML_HEADS).transpose(2, 3, 0, 1)
    return g[0], g[1], g[2], g[3]


def _gate_cols(dgi_f, dgf_f, dgi_b, dgf_b, t):
    g = jnp.stack([dgi_f, dgf_f, dgi_b, dgf_b]).transpose(2, 3, 0, 1).reshape(t, 4 * ML_HEADS)
    return jnp.pad(g, ((0, 0), (0, 128 - 4 * ML_HEADS)))


def _local_step(x, target, shift, scale, gate, norm_w, w_in_t, b_in_p, conv_w8, conv_b, rpb,
                ml_norm_w, w_out_b, final_norm_w):
    t = x.shape[0]
    rows = t // GRID_W
    tm = 512
    proj, gates = _in_proj(x, norm_w, scale, shift, w_in_t, b_in_p)
    tbl = _na_bias_table(rpb, rows)
    o_na, lse_na = _na_fwd(proj, tbl)
    pre, qk_act = _conv_fwd(proj, conv_w8, conv_b, tm)
    gi_f, gf_f, gi_b, gf_b = _gate_rows(gates[:, :4 * ML_HEADS], t)
    h_f, cchk_f, nmchk_f = _ml_fwd(qk_act, proj, gi_f, gf_f, False, "ml_fwd_f")
    h_b, cchk_b, nmchk_b = _ml_fwd(qk_act, proj, gi_b, gf_b, True, "ml_fwd_b")
    (loss, dres, d_ona, d_naz, dhs, d_o, d_z, dgate, g_fnw, g_mlnw, g_w_out) = _tail(
        o_na, proj, h_f, h_b, x, target, gate, ml_norm_w, final_norm_w, w_out_b)
    dq_na, dk_na, dv_na, rpbacc = _na_bwd(proj, tbl, d_ona, o_na, lse_na)
    g_rpb = _rpb_reduce(rpbacc, rows)
    dq_f, dk_f, dv_f, dgi_f, dgf_f = _ml_bwd(qk_act, proj, gi_f, gf_f, dhs, cchk_f, nmchk_f, (),
                                             False, "ml_bwd_f")
    dq_ml, dk_ml, dv_ml, dgi_b, dgf_b = _ml_bwd(qk_act, proj, gi_b, gf_b, dhs, cchk_b, nmchk_b, (dq_f, dk_f, dv_f),
                                                True, "ml_bwd_b")
    du, g_conv_w, g_conv_b = _conv_bwd(dq_ml, dk_ml, pre, proj, conv_w8, tm)
    dgates = _gate_cols(dgi_f, dgf_f, dgi_b, dgf_b, t)
    grad_x, g_w_in, g_b_in, dscale, dshift, g_nw = _in_bwd(
        [dq_na, dk_na, dv_na, d_naz, du, dv_ml, d_o, d_z, dgates], x, dres, w_in_t, norm_w, scale, shift)
    dmod = jnp.concatenate([dshift, dscale, dgate], axis=1)
    return (loss, grad_x, dmod, g_nw, g_w_in, g_b_in, g_conv_w, g_conv_b, g_rpb, g_mlnw, g_w_out, g_fnw)


MESH = pl.DeviceIdType.MESH
N_DEV = 8
ANY = pl.BlockSpec(memory_space=pl.ANY)
WHOLE_VMEM = pl.BlockSpec(memory_space=pltpu.VMEM)


def _allgather8(blocks, name):
    na = len(blocks)

    def body(*refs):
        x_refs = refs[:na]
        out_refs = refs[na:2 * na]
        send_sems, recv_sems, local_sems = refs[2 * na:]
        x, y, c = lax.axis_index("x"), lax.axis_index("y"), lax.axis_index("c")
        me, sibling = (x, y, c), (x, y, 1 - c)
        chips = [(1 - x, y), (x, 1 - y), (1 - x, 1 - y)]

        def rows(a, px, py, pc):
            return out_refs[a].at[4 * px + 2 * py + pc]

        def copy(a, k, block, to, src=None):
            return pltpu.make_async_remote_copy(
                src_ref=rows(a, *block) if src is None else src, dst_ref=rows(a, *block),
                send_sem=send_sems.at[a, k], recv_sem=recv_sems.at[a, k],
                device_id=to, device_id_type=MESH)

        mine, first, passed = [], [], []
        for a in range(na):
            cp = pltpu.make_async_copy(x_refs[a], rows(a, *me), local_sems.at[a])
            cp.start()
            mine.append(cp)
            first.append(copy(a, 0, me, sibling, src=x_refs[a]))
            first += [copy(a, 1 + j, me, (*chip, c), src=x_refs[a]) for j, chip in enumerate(chips)]
        for cp in first:
            cp.start()
        for a in range(na):
            for j, chip in enumerate(chips):
                copy(a, 1 + j, (*chip, c), me).wait_recv()
                fwd = copy(a, 4 + j, (*chip, c), sibling)
                fwd.start()
                passed.append(fwd)
        for a in range(na):
            copy(a, 0, sibling, me).wait_recv()
            for j, chip in enumerate(chips):
                copy(a, 4 + j, (*chip, 1 - c), me).wait_recv()
        for cp in first + passed:
            cp.wait_send()
        for cp in mine:
            cp.wait()

    return pl.pallas_call(
        body, name=name,
        out_shape=[jax.ShapeDtypeStruct((N_DEV,) + b.shape, b.dtype) for b in blocks],
        in_specs=[WHOLE_VMEM] * na, out_specs=[WHOLE_VMEM] * na,
        scratch_shapes=[pltpu.SemaphoreType.DMA((na, 7)), pltpu.SemaphoreType.DMA((na, 7)),
                        pltpu.SemaphoreType.DMA((na,))],
        compiler_params=pltpu.CompilerParams(vmem_limit_bytes=VMEM_LIMIT),
    )(*blocks)


def _pair_exchange(arrs, name):
    na = len(arrs)

    def body(*refs):
        in_refs = refs[:na]
        out_refs = refs[na:2 * na]
        send_sems, recv_sems = refs[2 * na:]
        sibling = (lax.axis_index("x"), lax.axis_index("y"), 1 - lax.axis_index("c"))
        copies = [pltpu.make_async_remote_copy(
            src_ref=in_refs[a], dst_ref=out_refs[a], send_sem=send_sems.at[a], recv_sem=recv_sems.at[a],
            device_id=sibling, device_id_type=MESH) for a in range(na)]
        for cp in copies:
            cp.start()
        for cp in copies:
            cp.wait()

    return pl.pallas_call(
        body, name=name,
        out_shape=[jax.ShapeDtypeStruct(a.shape, a.dtype) for a in arrs],
        in_specs=[ANY] * na, out_specs=[ANY] * na,
        scratch_shapes=[pltpu.SemaphoreType.DMA((na,)), pltpu.SemaphoreType.DMA((na,))],
    )(*arrs)


def _chip_exchange(arrs, name):
    na = len(arrs)

    def body(*refs):
        in_refs = refs[:na]
        out_refs = refs[na:2 * na]
        send_sems, recv_sems, local_sems = refs[2 * na:]
        x, y, c = lax.axis_index("x"), lax.axis_index("y"), lax.axis_index("c")
        my_chip = 2 * x + y
        chips = [(1 - x, y), (x, 1 - y), (1 - x, 1 - y)]
        local, remote = [], []
        for a in range(na):
            cp = pltpu.make_async_copy(in_refs[a].at[my_chip], out_refs[a].at[my_chip], local_sems.at[a])
            cp.start()
            local.append(cp)
            for j, (px, py) in enumerate(chips):
                cp = pltpu.make_async_remote_copy(
                    src_ref=in_refs[a].at[2 * px + py], dst_ref=out_refs[a].at[my_chip],
                    send_sem=send_sems.at[a, j], recv_sem=recv_sems.at[a, j],
                    device_id=(px, py, c), device_id_type=MESH)
                cp.start()
                remote.append(cp)
        for cp in remote:
            cp.wait()
        for cp in local:
            cp.wait()

    return pl.pallas_call(
        body, name=name,
        out_shape=[jax.ShapeDtypeStruct(a.shape, a.dtype) for a in arrs],
        in_specs=[ANY] * na, out_specs=[ANY] * na,
        scratch_shapes=[pltpu.SemaphoreType.DMA((na, 3)), pltpu.SemaphoreType.DMA((na, 3)),
                        pltpu.SemaphoreType.DMA((na,))],
    )(*arrs)


def _rows_tile(r):
    for cand in (512, 256, 128, 64, 32, 16, 8):
        if r % cand == 0:
            return cand
    return r


def _add2(a, b, name, out_dtype):
    s, r, n = a.shape
    tr = _rows_tile(r)

    def body(a_ref, b_ref, o_ref):
        o_ref[...] = (a_ref[...] + b_ref[...]).astype(out_dtype)

    spec = pl.BlockSpec((1, tr, n), lambda i, j: (i, j, 0))
    return pl.pallas_call(
        body, name=name, grid=(s, r // tr), in_specs=[spec, spec], out_specs=spec,
        out_shape=jax.ShapeDtypeStruct(a.shape, out_dtype),
        compiler_params=_cparams(("parallel", "parallel")),
    )(a, b)


def _sum_slabs(a, name):
    s, r, n = a.shape
    tr = _rows_tile(r)

    def body(a_ref, o_ref):
        acc = a_ref[0].astype(F32)
        for k in range(1, s):
            acc = acc + a_ref[k].astype(F32)
        o_ref[...] = acc

    return pl.pallas_call(
        body, name=name, grid=(r // tr,),
        in_specs=[pl.BlockSpec((s, tr, n), lambda i: (0, i, 0))],
        out_specs=pl.BlockSpec((tr, n), lambda i: (i, 0)),
        out_shape=jax.ShapeDtypeStruct((r, n), F32),
        compiler_params=_cparams(("parallel",)),
    )(a)


def _adamw(w, g, m, v, name):
    r, n = w.shape
    if r % 8 == 0:
        blk, grid, imap = (_rows_tile(r), n), (r // _rows_tile(r),), (lambda i: (i, 0))
    else:
        blk, grid, imap = (r, 128), (n // 128,), (lambda i: (0, i))
    c1 = 1.0 / (1.0 - ADAM_B1 ** ADAM_STEP)
    c2 = 1.0 / (1.0 - ADAM_B2 ** ADAM_STEP)

    def body(w_ref, g_ref, m_ref, v_ref, d_ref, nm_ref, nv_ref):
        gv = g_ref[...]
        nm = ADAM_B1 * m_ref[...] + (1.0 - ADAM_B1) * gv
        nv = ADAM_B2 * v_ref[...] + (1.0 - ADAM_B2) * (gv * gv)
        nm_ref[...] = nm
        nv_ref[...] = nv
        d_ref[...] = -ADAM_LR * ((nm * c1) / (jnp.sqrt(nv * c2) + ADAM_EPS) + ADAM_WD * w_ref[...])

    spec = pl.BlockSpec(blk, imap)
    o = jax.ShapeDtypeStruct((r, n), F32)
    return pl.pallas_call(
        body, name=name, grid=grid, in_specs=[spec] * 4, out_specs=[spec] * 3, out_shape=[o, o, o],
        compiler_params=_cparams(("parallel",)),
    )(w, g, m, v)


def _mod_fwd(c_all, w_ada_s, b_ada_s):
    def body(c_ref, w_ref, b_ref, o_ref):
        o_ref[...] = jnp.dot(_silu(c_ref[...]), w_ref[...], precision=HI, preferred_element_type=F32) + b_ref[...]

    return pl.pallas_call(
        body, name="mod_fwd", out_shape=jax.ShapeDtypeStruct((c_all.shape[0], w_ada_s.shape[1]), F32),
        in_specs=[WHOLE_VMEM] * 3, out_specs=WHOLE_VMEM,
        compiler_params=pltpu.CompilerParams(vmem_limit_bytes=VMEM_LIMIT),
    )(c_all, w_ada_s, b_ada_s)


def _wada_grad(c_all, dmod_s):
    def body(c_ref, d_ref, o_ref):
        o_ref[...] = lax.dot_general(_silu(c_ref[...]), d_ref[...], (((0,), (0,)), ((), ())),
                                     precision=HI, preferred_element_type=F32)

    return pl.pallas_call(
        body, name="w_ada_grad", out_shape=jax.ShapeDtypeStruct((c_all.shape[1], dmod_s.shape[1]), F32),
        in_specs=[WHOLE_VMEM] * 2, out_specs=WHOLE_VMEM,
        compiler_params=pltpu.CompilerParams(vmem_limit_bytes=VMEM_LIMIT),
    )(c_all, dmod_s)


SMALL_ROWS = 24


def _pad_rows(v, nrows):
    v = v.reshape(-1)
    return jnp.pad(v, (0, nrows * 1024 - v.shape[0])).reshape(nrows, 1024)


def _pack_small(b_ada, norm_w, b_in, conv_w_full, conv_b, rpb, ml_norm_w, final_norm_w, last):
    parts = [_pad_rows(b_ada, 3), _pad_rows(norm_w, 1), _pad_rows(b_in, 5), _pad_rows(conv_w_full, 5),
             _pad_rows(conv_b, 1), _pad_rows(rpb, 4), _pad_rows(ml_norm_w, 1), _pad_rows(final_norm_w, 1),
             _pad_rows(last, 3)]
    return jnp.concatenate(parts, axis=0)


def _unpack_small(p):
    return dict(b_ada=p[0:3].reshape(1, 3072), norm_w=p[3:4], b_in=p[4:9].reshape(-1)[:IN_W].reshape(1, IN_W),
                conv_w=p[9:14], conv_b=p[14:15],
                rpb=p[15:19].reshape(-1)[:NA_HEADS * 15 * 31].reshape(1, NA_HEADS, 15, 31),
                ml_norm_w=p[19:20, :ML_W], final_norm_w=p[20], last=p[21])


def kernel(x, c, w_ada, b_ada, norm_w, w_in, b_in, conv_w, conv_b, rpb, ml_norm_w, w_out, final_norm_w, loss_target, m_w_ada, m_b_ada, m_norm_w, m_w_in, m_b_in, m_conv_w, m_conv_b, m_rpb, m_ml_norm_w, m_w_out, m_final_norm_w, v_w_ada, v_b_ada, v_norm_w, v_w_in, v_b_in, v_conv_w, v_conv_b, v_rpb, v_ml_norm_w, v_w_out, v_final_norm_w):
    xi, yi, ci = lax.axis_index("x"), lax.axis_index("y"), lax.axis_index("c")
    chip = 2 * xi + yi
    dev = 2 * chip + ci
    t = x.shape[1]
    ada_n = w_ada.shape[2]
    in_n = w_in.shape[2]
    out_r = w_out.shape[1]

    c_blk = jnp.pad(c, ((0, 7), (0, 0)))
    w_in_t, m_w_in_t, v_w_in_t = w_in[0].T, m_w_in[0].T, v_w_in[0].T
    in_h = in_n // 2
    w_in_half = lax.dynamic_slice_in_dim(w_in_t, ci * in_h, in_h, axis=0).astype(BF16)
    w_out_half = lax.dynamic_slice_in_dim(w_out[0], ci * (out_r // 2), out_r // 2, axis=0).astype(BF16)
    conv_blk = jnp.pad(conv_w[0], ((0, 3), (0, 0)))
    c_g, conv_g, w_in_g, w_out_g = _allgather8([c_blk, conv_blk, w_in_half, w_out_half], "gather_c_weights")
    c_all = c_g[:, 0]
    w_out_g = w_out_g.reshape(D_MODEL, D_MODEL)
    b_ada_s = lax.dynamic_slice_in_dim(b_ada, chip * ada_n, ada_n, axis=1)
    mod_s = _mod_fwd(c_all, w_ada[0], b_ada_s)
    (mod_g,) = _allgather8([mod_s], "gather_mod")
    mod_mine = lax.dynamic_index_in_dim(mod_g, dev, axis=1, keepdims=False)
    mod = mod_mine[0::2].reshape(1, 3 * D_MODEL)
    shift, scale, gate = mod[:, :D_MODEL], mod[:, D_MODEL:2 * D_MODEL], mod[:, 2 * D_MODEL:]

    w_in_tp = jnp.pad(w_in_g.reshape(IN_W, D_MODEL), ((0, IN_PAD - IN_W), (0, 0)))
    b_in_p = jnp.pad(b_in, ((0, 0), (0, IN_PAD - IN_W)))
    conv_w8 = conv_g.reshape(4, 2, 8, conv_w.shape[2])[:, 0].transpose(1, 0, 2).reshape(8, D_MODEL)

    (loss, grad_x, dmod, g_nw, g_w_in, g_b_in, g_conv_w, g_conv_b, g_rpb, g_mlnw, g_w_out, g_fnw) = _local_step(
        x[0], loss_target[0], shift, scale, gate, norm_w, w_in_tp, b_in_p, conv_w8, conv_b, rpb[0],
        ml_norm_w, w_out_g, final_norm_w.reshape(1, D_MODEL))

    g_in_t = g_w_in[:, :IN_W].T

    def halves(a, per_chip, h):
        return jnp.stack([lax.dynamic_slice_in_dim(a, k * per_chip + h * (per_chip // 2), per_chip // 2, axis=0)
                          for k in range(4)])

    ri, ro = _pair_exchange([halves(g_in_t, in_n, 1 - ci), halves(g_w_out, out_r, 1 - ci)], "rs_pair")
    pi = _add2(halves(g_in_t, in_n, ci), ri, "rs_pair_add_in", BF16)
    po = _add2(halves(g_w_out, out_r, ci), ro, "rs_pair_add_out", BF16)
    qi, qo = _chip_exchange([pi, po], "rs_chips")
    si = _sum_slabs(qi, "rs_sum_in")
    so = _sum_slabs(qo, "rs_sum_out")
    ti, to = _pair_exchange([si, so], "rs_share")
    g_w_in_s = jnp.where(ci == 0, jnp.concatenate([si, ti], axis=0), jnp.concatenate([ti, si], axis=0))
    g_w_out_s = jnp.where(ci == 0, jnp.concatenate([so, to], axis=0), jnp.concatenate([to, so], axis=0))

    small = _pack_small(dmod, g_nw, g_b_in[:, :IN_W], g_conv_w[:CONV_W], g_conv_b, g_rpb, g_mlnw, g_fnw,
                        jnp.pad(loss, ((0, 0), (0, 1024 - 128))))
    (small_g,) = _allgather8([small], "gather_small")
    small_sum = _sum_slabs(small_g, "small_sum")
    gs = _unpack_small(small_sum)
    dmod_all = small_g[:, 0:3].reshape(N_DEV, 3 * D_MODEL)
    g_w_ada_s = _wada_grad(c_all, lax.dynamic_slice_in_dim(dmod_all, chip * ada_n, ada_n, axis=1))
    g_conv_w_s = lax.dynamic_slice_in_dim(gs['conv_w'], chip * conv_w.shape[2], conv_w.shape[2], axis=1)
    loss_total = gs['last'][0]

    zeros3 = jnp.zeros((3, 1024), F32)
    zc = jnp.zeros((CONV_W, D_MODEL), F32)
    pw = _pack_small(b_ada, norm_w, b_in, zc, conv_b, rpb, ml_norm_w, final_norm_w, zeros3)
    pm = _pack_small(m_b_ada, m_norm_w, m_b_in, zc, m_conv_b, m_rpb, m_ml_norm_w, m_final_norm_w, zeros3)
    pv = _pack_small(v_b_ada, v_norm_w, v_b_in, zc, v_conv_b, v_rpb, v_ml_norm_w, v_final_norm_w, zeros3)
    ds_, nms, nvs = [_unpack_small(a) for a in _adamw(pw, small_sum, pm, pv, "adamw_small")]
    d_ada, nm_ada, nv_ada = _adamw(w_ada[0], g_w_ada_s, m_w_ada[0], v_w_ada[0], "adamw_w_ada")
    d_in, nm_in, nv_in = _adamw(w_in_t, g_w_in_s, m_w_in_t, v_w_in_t, "adamw_w_in")
    d_out, nm_out, nv_out = _adamw(w_out[0], g_w_out_s, m_w_out[0], v_w_out[0], "adamw_w_out")
    d_cw, nm_cw, nv_cw = _adamw(conv_w[0], g_conv_w_s, m_conv_w[0], v_conv_w[0], "adamw_conv_w")

    def group(big_ada, big_in, big_out, cw, sm):
        return (big_ada[None], sm['b_ada'], sm['norm_w'], big_in.T[None], sm['b_in'], cw[None], sm['conv_b'],
                sm['rpb'], sm['ml_norm_w'], big_out[None], sm['final_norm_w'])

    return ((loss_total, grad_x[None])
            + group(g_w_ada_s, g_w_in_s, g_w_out_s, g_conv_w_s, gs)
            + group(d_ada, d_in, d_out, d_cw, ds_)
            + group(nm_ada, nm_in, nm_out, nm_cw, nms)
            + group(nv_ada, nv_in, nv_out, nv_cw, nvs))
```

```python
import functools

import numpy as np
import jax
import jax.numpy as jnp
from jax import lax
from jax.experimental import pallas as pl
from jax.experimental.pallas import tpu as pltpu

F32 = jnp.float32
BF16 = jnp.bfloat16
HI = lax.Precision.HIGHEST

D_MODEL = 1024
GRID_W = 64
NA_W = 512
NA_HEAD_DIM = 64
NA_HEADS = 8
NA_KH = 8
NA_KW = 16
ML_W = 512
ML_HEADS = 4
ML_HEAD_DIM = 128
ML_CHUNK = 128
CONV_W = 5
EPS = 1e-6
IN_W = 4 * NA_W + 5 * ML_W + 4 * ML_HEADS
IN_MAIN = 4 * NA_W + 5 * ML_W
IN_PAD = IN_MAIN + 128
NEG = -1e30

ADAM_LR = 0.001
ADAM_B1 = 0.9
ADAM_B2 = 0.999
ADAM_EPS = 1e-08
ADAM_WD = 0.01
ADAM_STEP = 10

NA_QROWS = 8
NA_KROWS = 16
NA_QT = NA_QROWS * GRID_W
NA_KT = NA_KROWS * GRID_W
NA_KCH = 256
NA_RC = 32
ML_NB = 16
ML_TB = ML_NB * ML_CHUNK
ML_HPS = 1

VMEM_LIMIT = 56 * 1024 * 1024


def _cparams(sem, vmem=VMEM_LIMIT):
    return pltpu.CompilerParams(dimension_semantics=sem, vmem_limit_bytes=vmem)


def _silu(x):
    return x * jax.nn.sigmoid(x)


def _dsilu(x):
    s = jax.nn.sigmoid(x)
    return s * (1.0 + x * (1.0 - s))


def _dot(a, b, dims):
    return lax.dot_general(a, b, (dims, ((), ())), preferred_element_type=F32)


def _nn(a, b):
    return _dot(a, b, ((1,), (0,)))


def _nt(a, b):
    return _dot(a, b, ((1,), (1,)))


def _tn(a, b):
    return _dot(a, b, ((0,), (0,)))


def _row(n):
    return pl.BlockSpec((1, n), lambda i: (0, 0))


def _modulated_norm(xv, nw, sc, sh):
    r = lax.rsqrt(jnp.mean(xv * xv, axis=-1, keepdims=True) + EPS)
    xn = xv * r
    return xn * nw * (1.0 + sc) + sh, xn, r


IN_TN = 768


def _in_proj(x, norm_w, scale, shift, w_in_t, b_in_p):
    t, d = x.shape
    tm = 2048
    gcol = IN_MAIN // 128

    def body(x_ref, nw_ref, sc_ref, sh_ref, w_ref, b_ref, wg_ref, bg_ref, proj_ref, g_ref, h_scr):
        @pl.when(pl.program_id(1) == 0)
        def _():
            h, _, _ = _modulated_norm(x_ref[...], nw_ref[...], sc_ref[...], sh_ref[...])
            h_scr[...] = h.astype(BF16)
            g_ref[...] = _nt(h_scr[...], wg_ref[...]) + bg_ref[...]
        proj_ref[...] = _nt(h_scr[...], w_ref[...]) + b_ref[...]

    row = lambda n: pl.BlockSpec((1, n), lambda i, j: (0, 0))
    return pl.pallas_call(
        body, name="in_proj", grid=(t // tm, IN_MAIN // IN_TN),
        in_specs=[pl.BlockSpec((tm, d), lambda i, j: (i, 0)), row(d), row(d), row(d),
                  pl.BlockSpec((IN_TN, d), lambda i, j: (j, 0)), pl.BlockSpec((1, IN_TN), lambda i, j: (0, j)),
                  pl.BlockSpec((128, d), lambda i, j: (gcol, 0)), pl.BlockSpec((1, 128), lambda i, j: (0, gcol))],
        out_specs=[pl.BlockSpec((tm, IN_TN), lambda i, j: (i, j)), pl.BlockSpec((tm, 128), lambda i, j: (i, 0))],
        out_shape=[jax.ShapeDtypeStruct((t, IN_MAIN), F32), jax.ShapeDtypeStruct((t, 128), F32)],
        scratch_shapes=[pltpu.VMEM((tm, d), BF16)],
        compiler_params=_cparams(("parallel", "arbitrary")),
    )(x, norm_w, scale, shift, w_in_t, b_in_p, w_in_t, b_in_p)


def _ml_norm_parts(hs, o, z, nw):
    outs = []
    for hh in range(ML_HEADS):
        sl = slice(hh * ML_HEAD_DIM, (hh + 1) * ML_HEAD_DIM)
        hm = hs[:, sl] * jax.nn.sigmoid(o[:, sl])
        mu = jnp.mean(hm, axis=-1, keepdims=True)
        cen = hm - mu
        var = jnp.mean(cen * cen, axis=-1, keepdims=True)
        rs = lax.rsqrt(var + EPS)
        outs.append((sl, cen * rs, rs))
    return outs


def _tail(o_na, proj, h_f, h_b, x, target, gate, ml_norm_w, fnw, w_out_b):
    t, d = x.shape
    tm = 256

    def body(ona_ref, naz_ref, hf_ref, hb_ref, o_ref, z_ref, x_ref, tg_ref, g_ref, nw_ref, fw_ref, w_ref,
             loss_ref, dres_ref, dona_ref, dnaz_ref, dhs_ref, do_ref, dz_ref, dgate_ref, gfw_ref, gnw_ref,
             gwo_ref, mix_scr):
        @pl.when(pl.program_id(0) == 0)
        def _():
            for r in (loss_ref, dgate_ref, gfw_ref, gnw_ref, gwo_ref):
                r[...] = jnp.zeros_like(r)
        naz = naz_ref[...]
        ona = ona_ref[...]
        sna = _silu(naz)
        mix_scr[:, 0:NA_W] = (ona * sna).astype(BF16)
        hs = hf_ref[...] + hb_ref[...]
        z = z_ref[...]
        ov = o_ref[...]
        parts = _ml_norm_parts(hs, ov, z, nw_ref[...])
        szs = []
        for sl, xn, _ in parts:
            sz = _silu(z[:, sl])
            szs.append(sz)
            mix_scr[:, NA_W + sl.start:NA_W + sl.stop] = (xn * nw_ref[:, sl] * sz).astype(BF16)
        mixb = mix_scr[...]
        wv = w_ref[...]
        yv = _nn(mixb, wv)
        gate_v = g_ref[...]
        hres = x_ref[...] + gate_v * yv
        r = lax.rsqrt(jnp.mean(hres * hres, axis=-1, keepdims=True) + EPS)
        xnf = hres * r
        err = xnf * fw_ref[...] - tg_ref[...]
        loss_ref[...] += 0.5 * jnp.sum(jnp.sum(err * err, axis=-1, keepdims=True) * (1.0 / d), axis=0, keepdims=True)
        dout = err * (1.0 / d)
        gfw_ref[...] += jnp.sum(dout * xnf, axis=0, keepdims=True)
        dxn = dout * fw_ref[...]
        dres = r * (dxn - xnf * jnp.mean(dxn * xnf, axis=-1, keepdims=True))
        dres_ref[...] = dres
        dgate_ref[...] += jnp.sum(dres * yv, axis=0, keepdims=True)
        dyb = (dres * gate_v).astype(BF16)
        gwo_ref[...] += _tn(mixb, dyb)
        dmix = _nt(dyb, wv)
        dna = dmix[:, 0:NA_W]
        dona_ref[...] = dna * sna
        dnaz_ref[...] = dna * ona * _dsilu(naz)
        for (sl, xn, rs), sz in zip(parts, szs):
            dyv = dmix[:, NA_W + sl.start:NA_W + sl.stop]
            zz = z[:, sl]
            w = nw_ref[:, sl]
            dz_ref[:, sl] = dyv * xn * w * _dsilu(zz)
            gnw_ref[:, sl] += jnp.sum(dyv * xn * sz, axis=0, keepdims=True)
            dxm = dyv * w * sz
            dhm = rs * (dxm - jnp.mean(dxm, axis=-1, keepdims=True)
                        - xn * jnp.mean(dxm * xn, axis=-1, keepdims=True))
            so = jax.nn.sigmoid(ov[:, sl])
            dhs_ref[:, sl] = dhm * so
            do_ref[:, sl] = dhm * hs[:, sl] * so * (1.0 - so)

    blk = lambda c: pl.BlockSpec((tm, 512), lambda i, c=c: (i, c))
    full = pl.BlockSpec((tm, d), lambda i: (i, 0))
    o512 = jax.ShapeDtypeStruct((t, 512), F32)
    whole = pl.BlockSpec((d, d), lambda i: (0, 0))
    return pl.pallas_call(
        body, name="tail", grid=(t // tm,),
        in_specs=[blk(0), blk(3), blk(0), blk(0), blk(7), blk(8), full, full, _row(d), _row(ML_W), _row(d), whole],
        out_specs=[pl.BlockSpec((1, 128), lambda i: (0, 0)), full] + [blk(0)] * 5
        + [_row(d), _row(d), _row(ML_W), whole],
        out_shape=[jax.ShapeDtypeStruct((1, 128), F32), jax.ShapeDtypeStruct((t, d), F32)] + [o512] * 5
        + [jax.ShapeDtypeStruct((1, d), F32), jax.ShapeDtypeStruct((1, d), F32),
           jax.ShapeDtypeStruct((1, ML_W), F32), jax.ShapeDtypeStruct((d, d), F32)],
        scratch_shapes=[pltpu.VMEM((tm, d), BF16)],
        compiler_params=_cparams(("arbitrary",)),
    )(o_na, proj, h_f, h_b, proj, proj, x, target, gate, ml_norm_w, fnw, w_out_b)


def _in_bwd(pieces, x, dres, w_in_t, norm_w, scale, shift):
    t, d = x.shape
    tm = 256
    nt = t // tm
    widths = [p.shape[1] for p in pieces]
    offs = [sum(widths[:k]) for k in range(len(widths))]
    assert sum(widths) == IN_PAD
    npc = len(pieces)

    def body(*refs):
        p_refs = refs[:npc]
        (x_ref, dres_ref, w_hbm, nw_ref, sc_ref, sh_ref,
         gx_ref, gw_hbm, gb_ref, dsc_ref, dsh_ref, gnw_ref, w_vmem, acc, stage, sem) = refs[npc:]
        i = pl.program_id(0)

        @pl.when(i == 0)
        def _():
            cp = pltpu.make_async_copy(w_hbm, w_vmem, sem.at[0])
            cp.start()
            acc[...] = jnp.zeros_like(acc)
            gb_ref[...] = jnp.zeros_like(gb_ref)
            dsc_ref[...] = jnp.zeros_like(dsc_ref)
            dsh_ref[...] = jnp.zeros_like(dsh_ref)
            gnw_ref[...] = jnp.zeros_like(gnw_ref)
            cp.wait()

        nw = nw_ref[...]
        s1 = 1.0 + sc_ref[...]
        h, xn, r = _modulated_norm(x_ref[...], nw, sc_ref[...], sh_ref[...])
        hb = h.astype(BF16)
        dhv = jnp.zeros((tm, d), F32)
        for p_ref, c0, w in zip(p_refs, offs, widths):
            pt = p_ref[...]
            pb = pt.astype(BF16)
            dhv = dhv + _nn(pb, w_vmem[c0:c0 + w, :])
            acc[:, c0:c0 + w] += _tn(hb, pb)
            gb_ref[:, c0:c0 + w] += jnp.sum(pt, axis=0, keepdims=True)
        dsh_ref[...] += jnp.sum(dhv, axis=0, keepdims=True)
        dsc_ref[...] += jnp.sum(dhv * xn * nw, axis=0, keepdims=True)
        gnw_ref[...] += jnp.sum(dhv * xn * s1, axis=0, keepdims=True)
        dxn = dhv * nw * s1
        gx_ref[...] = dres_ref[...] + r * (dxn - xn * jnp.mean(dxn * xn, axis=-1, keepdims=True))

        @pl.when(i == nt - 1)
        def _():
            copies = []
            for blk in range(IN_PAD // 128):
                slot = blk % 2
                if blk >= 2:
                    copies[blk - 2].wait()
                stage[slot] = acc[:, blk * 128:(blk + 1) * 128].T
                cp = pltpu.make_async_copy(stage.at[slot], gw_hbm.at[pl.ds(blk * 128, 128), :], sem.at[1 + slot])
                cp.start()
                copies.append(cp)
            copies[-2].wait()
            copies[-1].wait()

    full = pl.BlockSpec((tm, d), lambda i: (i, 0))
    return pl.pallas_call(
        body, name="in_bwd", grid=(nt,),
        in_specs=[pl.BlockSpec((tm, w), lambda i: (i, 0)) for w in widths]
        + [full, full, pl.BlockSpec(memory_space=pl.ANY), _row(d), _row(d), _row(d)],
        out_specs=[full, pl.BlockSpec(memory_space=pl.ANY), _row(IN_PAD), _row(d), _row(d), _row(d)],
        out_shape=[jax.ShapeDtypeStruct((t, d), F32), jax.ShapeDtypeStruct((IN_PAD, d), F32),
                   jax.ShapeDtypeStruct((1, IN_PAD), F32)] + [jax.ShapeDtypeStruct((1, d), F32)] * 3,
        scratch_shapes=[pltpu.VMEM((IN_PAD, d), BF16), pltpu.VMEM((d, IN_PAD), F32),
                        pltpu.VMEM((2, 128, d), F32), pltpu.SemaphoreType.DMA((3,))],
        compiler_params=_cparams(("arbitrary",)),
    )(*pieces, x, dres, w_in_t, norm_w, scale, shift)


def _na_static(rows):
    cases = [(0, 0), (NA_QROWS, NA_QROWS - 4), (rows - NA_QROWS, rows - NA_KROWS)]
    dy = np.zeros((3, NA_QROWS, NA_KROWS), np.int32)
    rv = np.zeros((3, NA_QROWS, NA_KROWS), bool)
    for cs, (r0, kr0) in enumerate(cases):
        for i in range(NA_QROWS):
            for j in range(NA_KROWS):
                r, kr = r0 + i, kr0 + j
                rs = min(max(r - NA_KH // 2, 0), rows - NA_KH)
                rv[cs, i, j] = rs <= kr <= rs + NA_KH - 1
                dy[cs, i, j] = min(max(kr - r + NA_KH - 1, 0), 2 * NA_KH - 2)
    cq = np.arange(GRID_W)[:, None]
    ck = np.arange(GRID_W)[None, :]
    cs0 = np.clip(cq - NA_KW // 2, 0, GRID_W - NA_KW)
    cv = (ck >= cs0) & (ck < cs0 + NA_KW)
    dx = np.clip(ck - cq, -(NA_KW - 1), NA_KW - 1) + NA_KW - 1
    return dy, rv, dx.astype(np.int32), cv


def _na_bias_table(rpb, rows):
    _, _, dx, cv = _na_static(rows)
    ndy = 2 * NA_KH - 1
    onehot = (dx.reshape(1, -1) == np.arange(2 * NA_KW - 1)[:, None]).astype(np.float32)
    rpx = jnp.dot(rpb.reshape(NA_HEADS * ndy, 2 * NA_KW - 1), jnp.asarray(onehot), precision=HI)
    rpx = jnp.where(cv[None, None], rpx.reshape(NA_HEADS, ndy, GRID_W, GRID_W), NEG)
    neg = jnp.full((NA_HEADS, 1, GRID_W, GRID_W), NEG, F32)
    rpx = jnp.concatenate([rpx, neg], axis=1)
    nxt = jnp.concatenate([rpx[:, 1:], neg], axis=1)
    negs = jnp.broadcast_to(neg, rpx.shape)
    pairs = jnp.concatenate([jnp.concatenate([rpx, nxt], axis=3), jnp.concatenate([rpx, negs], axis=3),
                             jnp.concatenate([negs, rpx], axis=3)], axis=1)
    npair = pairs.shape[1]

    def body(m_ref, o_ref):
        cs = pl.program_id(1)
        r0 = jnp.where(cs == 0, 0, jnp.where(cs == 1, NA_QROWS, rows - NA_QROWS))
        kr0 = jnp.where(cs == 0, 0, jnp.where(cs == 1, NA_QROWS - NA_KH // 2, rows - NA_KROWS))
        for i in range(NA_QROWS):
            r = r0 + i
            rs = jnp.clip(r - NA_KH // 2, 0, rows - NA_KH)
            for jp in range(NA_KROWS // 2):
                kl = kr0 + 2 * jp
                vl = (kl >= rs) & (kl <= rs + NA_KH - 1)
                vr = (kl + 1 >= rs) & (kl + 1 <= rs + NA_KH - 1)
                dyl = jnp.clip(kl - r + NA_KH - 1, 0, ndy)
                dyr = jnp.clip(kl + 1 - r + NA_KH - 1, 0, ndy)
                idx = jnp.where(vl & vr, dyl, jnp.where(vl, 16 + dyl, jnp.where(vr, 32 + dyr, 16 + ndy)))
                o_ref[0, 0, i * GRID_W:(i + 1) * GRID_W, jp * 128:(jp + 1) * 128] = m_ref[0, idx]

    return pl.pallas_call(
        body, name="na_bias_table", grid=(NA_HEADS, 3),
        in_specs=[pl.BlockSpec((1, npair, GRID_W, 128), lambda h, cs: (h, 0, 0, 0))],
        out_specs=pl.BlockSpec((1, 1, NA_QT, NA_KT), lambda h, cs: (h, cs, 0, 0)),
        out_shape=jax.ShapeDtypeStruct((NA_HEADS, 3, NA_QT, NA_KT), F32),
        compiler_params=_cparams(("parallel", "parallel")),
    )(pairs)


def _na_specs(t):
    nb = t // NA_QT
    nkb = t // NA_KCH
    npieces = NA_KT // NA_KCH

    def kb0(b):
        return jnp.clip(b * (NA_QT // NA_KCH) - 1, 0, nkb - npieces)

    def case(b):
        return jnp.where(b == 0, 0, jnp.where(b == nb - 1, 2, 1))

    q_spec = pl.BlockSpec((NA_QT, 128), lambda p, b: (b, p))
    k_specs = [pl.BlockSpec((NA_KCH, 128), lambda p, b, i=i: (kb0(b) + i, 4 + p)) for i in range(npieces)]
    v_specs = [pl.BlockSpec((NA_KCH, 128), lambda p, b, i=i: (kb0(b) + i, 8 + p)) for i in range(npieces)]
    tbl_spec = pl.BlockSpec((2, 1, NA_QT, NA_KT), lambda p, b: (p, case(b), 0, 0))
    io_spec = pl.BlockSpec((NA_QT, 128), lambda p, b: (b, p))
    return nb, npieces, kb0, case, q_spec, k_specs, v_specs, tbl_spec, io_spec


def _na_probs(qh, ks, tbl_ref, hh, npieces):
    s = [_nt(qh, ks[i]) + tbl_ref[hh, 0, :, i * NA_KCH:(i + 1) * NA_KCH] for i in range(npieces)]
    m = functools.reduce(jnp.maximum, [jnp.max(si, axis=1, keepdims=True) for si in s])
    p = [jnp.exp(si - m) for si in s]
    l = functools.reduce(jnp.add, [jnp.sum(pi, axis=1, keepdims=True) for pi in p])
    return p, m, l


def _na_fwd(proj, tbl):
    t = proj.shape[0]
    nb, npieces, _, _, q_spec, k_specs, v_specs, tbl_spec, io_spec = _na_specs(t)
    lse_spec = pl.BlockSpec((1, NA_QT, 2), lambda p, b: (p, b, 0))

    def body(*refs):
        q_ref = refs[0]
        k_refs = refs[1:1 + npieces]
        v_refs = refs[1 + npieces:1 + 2 * npieces]
        tbl_ref, o_ref, lse_ref = refs[1 + 2 * npieces:]
        lane = lax.broadcasted_iota(jnp.int32, (1, 128), 1)
        qv = q_ref[...] * (NA_HEAD_DIM ** -0.5)
        ks = [r[...].astype(BF16) for r in k_refs]
        vs = [r[...].astype(BF16) for r in v_refs]
        hs = range(2)
        msk = [(lane // NA_HEAD_DIM) == hh for hh in hs]
        qh = [jnp.where(msk[hh], qv, 0.0).astype(BF16) for hh in hs]
        pml = [_na_probs(qh[hh], ks, tbl_ref, hh, npieces) for hh in hs]
        pb = [[pml[hh][0][i].astype(BF16) for i in range(npieces)] for hh in hs]
        o = [functools.reduce(jnp.add, [_nn(pb[hh][i], vs[i]) for i in range(npieces)]) for hh in hs]
        for hh in hs:
            lse_ref[0, :, hh:hh + 1] = pml[hh][1] + jnp.log(pml[hh][2])
        o_ref[...] = jnp.where(msk[0], o[0] / pml[0][2], o[1] / pml[1][2])

    return pl.pallas_call(
        body, name="na_fwd", grid=(4, nb),
        in_specs=[q_spec] + k_specs + v_specs + [tbl_spec],
        out_specs=[io_spec, lse_spec],
        out_shape=[jax.ShapeDtypeStruct((t, NA_W), F32), jax.ShapeDtypeStruct((4, t, 2), F32)],
        compiler_params=_cparams(("parallel", "arbitrary")),
    )(*([proj] * (1 + 2 * npieces)), tbl)


def _na_bwd(proj, tbl, d_o, o_na, lse):
    t = proj.shape[0]
    nb, npieces, kb0, case, q_spec, k_specs, v_specs, tbl_spec, io_spec = _na_specs(t)

    def body(*refs):
        q_ref = refs[0]
        k_refs = refs[1:1 + npieces]
        v_refs = refs[1 + npieces:1 + 2 * npieces]
        (tbl_ref, do_ref, o_ref, lse_ref, dq_ref, dk_hbm, dv_hbm, rpb_ref,
         dk_acc, dv_acc, s_scr, dp_scr, dsb_scr, pnb_scr, sem) = refs[1 + 2 * npieces:]
        p_id = pl.program_id(0)
        b = pl.program_id(1)

        @pl.when(b == 0)
        def _():
            dk_acc[...] = jnp.zeros_like(dk_acc)
            dv_acc[...] = jnp.zeros_like(dv_acc)

        @pl.when((b == 0) | (b == 1) | (b == nb - 1))
        def _():
            rpb_ref[...] = jnp.zeros_like(rpb_ref)

        lane = lax.broadcasted_iota(jnp.int32, (1, 128), 1)
        scale = NA_HEAD_DIM ** -0.5
        qv = q_ref[...] * scale
        ks = [r[...].astype(BF16) for r in k_refs]
        vs = [r[...].astype(BF16) for r in v_refs]
        dov = do_ref[...]
        ov = o_ref[...]
        tok0 = kb0(b) * NA_KCH
        hs = range(2)
        ns = range(npieces)
        msk = [(lane // NA_HEAD_DIM) == hh for hh in hs]
        qh = [jnp.where(msk[hh], qv, 0.0).astype(BF16) for hh in hs]
        doh = [jnp.where(msk[hh], dov, 0.0) for hh in hs]
        dohb = [doh[hh].astype(BF16) for hh in hs]
        dd = [jnp.sum(doh[hh] * ov, axis=1, keepdims=True) for hh in hs]
        for hh in hs:
            for i in ns:
                slot = (hh * npieces + i) % 2
                cols = slice(i * NA_KCH, (i + 1) * NA_KCH)
                s_scr[slot] = _nt(qh[hh], ks[i])
                dp_scr[slot] = _nt(dohb[hh], vs[i])
                for r0 in range(0, NA_QT, NA_RC):
                    rows = slice(r0, r0 + NA_RC)
                    p = jnp.exp(s_scr[slot, rows, :] + tbl_ref[hh, 0, rows, cols] - lse_ref[0, rows, hh:hh + 1])
                    d = p * (dp_scr[slot, rows, :] - dd[hh][rows])
                    pnb_scr[hh, rows, cols] = p.astype(BF16)
                    dsb_scr[hh, rows, cols] = d.astype(BF16)
        dqh = [functools.reduce(jnp.add, [_nn(dsb_scr[hh, :, i * NA_KCH:(i + 1) * NA_KCH], ks[i]) for i in ns])
               for hh in hs]
        dq_ref[...] = jnp.where(msk[0], dqh[0], dqh[1]) * scale
        for i in ns:
            rows = pl.ds(pl.multiple_of(tok0 + i * NA_KCH, NA_KCH), NA_KCH)
            cols = slice(i * NA_KCH, (i + 1) * NA_KCH)
            dk_acc[rows, :] += _tn(dsb_scr[0, :, cols], qh[0]) + _tn(dsb_scr[1, :, cols], qh[1])
            dv_acc[rows, :] += _tn(pnb_scr[0, :, cols], dohb[0]) + _tn(pnb_scr[1, :, cols], dohb[1])
        for hh in hs:
            acc = dsb_scr[hh, 0:GRID_W, :].astype(F32)
            for i in range(1, NA_QROWS):
                acc = acc + pltpu.roll(dsb_scr[hh, i * GRID_W:(i + 1) * GRID_W, :].astype(F32),
                                       NA_KT - i * GRID_W, 1)
            rpb_ref[0, 0, hh] += acc

        @pl.when(b == nb - 1)
        def _():
            cols = pl.ds(pl.multiple_of(p_id * 128, 128), 128)
            ck = pltpu.make_async_copy(dk_acc, dk_hbm.at[:, cols], sem.at[0])
            cv = pltpu.make_async_copy(dv_acc, dv_hbm.at[:, cols], sem.at[1])
            ck.start()
            cv.start()
            ck.wait()
            cv.wait()

    o512 = jax.ShapeDtypeStruct((t, NA_W), F32)
    return pl.pallas_call(
        body, name="na_bwd", grid=(4, nb),
        in_specs=[q_spec] + k_specs + v_specs + [tbl_spec, io_spec, io_spec,
                                                 pl.BlockSpec((1, NA_QT, 2), lambda p, b: (p, b, 0))],
        out_specs=[io_spec, pl.BlockSpec(memory_space=pl.ANY), pl.BlockSpec(memory_space=pl.ANY),
                   pl.BlockSpec((1, 1, 2, GRID_W, NA_KT), lambda p, b: (p, case(b), 0, 0, 0))],
        out_shape=[o512, o512, o512, jax.ShapeDtypeStruct((4, 3, 2, GRID_W, NA_KT), F32)],
        scratch_shapes=[pltpu.VMEM((t, 128), F32), pltpu.VMEM((t, 128), F32),
                        pltpu.VMEM((2, NA_QT, NA_KCH), F32), pltpu.VMEM((2, NA_QT, NA_KCH), F32),
                        pltpu.VMEM((2, NA_QT, NA_KT), BF16), pltpu.VMEM((2, NA_QT, NA_KT), BF16),
                        pltpu.SemaphoreType.DMA((2,))],
        compiler_params=_cparams(("arbitrary", "arbitrary")),
    )(*([proj] * (1 + 2 * npieces)), tbl, d_o, o_na, lse)


def _rpb_reduce(rpbacc, rows):
    nacc = 4 * 3 * 2

    def shift_body(a_ref, o_ref):
        acc = a_ref[0, 0:1, :]
        for cq in range(1, GRID_W):
            acc = acc + pltpu.roll(a_ref[0, cq:cq + 1, :], NA_KT - cq, 1)
        o_ref[0] = jnp.broadcast_to(acc, (8, NA_KT))

    vec = pl.pallas_call(
        shift_body, name="rpb_shift", grid=(nacc,),
        in_specs=[pl.BlockSpec((1, GRID_W, NA_KT), lambda a: (a, 0, 0))],
        out_specs=pl.BlockSpec((1, 8, NA_KT), lambda a: (a, 0, 0)),
        out_shape=jax.ShapeDtypeStruct((nacc, 8, NA_KT), F32),
        compiler_params=_cparams(("parallel",)),
    )(rpbacc.reshape(nacc, GRID_W, NA_KT))
    a = vec[:, 0].reshape(4, 3, 2, NA_KT).transpose(0, 2, 1, 3).reshape(NA_HEADS, 3, NA_KT)
    if rows // NA_QROWS < 3:
        a = a.at[:, 1].set(0.0)
    dd = np.arange(NA_KROWS)[:, None]
    dxo = np.arange(-(NA_KW - 1), NA_KW)[None, :]
    idx = ((dd * GRID_W + dxo) % NA_KT).reshape(-1)
    g = a[..., idx].reshape(NA_HEADS, 3 * NA_KROWS, 2 * NA_KW - 1)
    g = jnp.pad(g, ((0, 0), (0, 0), (0, 128 - (2 * NA_KW - 1))))
    nmat = np.zeros((16, 3 * NA_KROWS), np.float32)
    for cs, delta in enumerate((0, -(NA_KH // 2), -(NA_KROWS - NA_QROWS))):
        for d in range(NA_KROWS):
            jmi = d - NA_KROWS if (cs == 0 and d > NA_KH - 1) else d
            dy = jmi + delta + NA_KH - 1
            if 0 <= dy <= 2 * NA_KH - 2:
                nmat[dy, cs * NA_KROWS + d] = 1.0

    def body(n_ref, g_ref, o_ref):
        o_ref[0] = jnp.dot(n_ref[...], g_ref[0], precision=HI, preferred_element_type=F32)

    out = pl.pallas_call(
        body, name="rpb_reduce", grid=(NA_HEADS,),
        in_specs=[pl.BlockSpec((16, nmat.shape[1]), lambda h: (0, 0)),
                  pl.BlockSpec((1, nmat.shape[1], 128), lambda h: (h, 0, 0))],
        out_specs=pl.BlockSpec((1, 16, 128), lambda h: (h, 0, 0)),
        out_shape=jax.ShapeDtypeStruct((NA_HEADS, 16, 128), F32),
        compiler_params=_cparams(("parallel",)),
    )(jnp.asarray(nmat), g)
    return out[:, :2 * NA_KH - 1, :2 * NA_KW - 1]


def _halo_specs(tm, t, col, width=1024):
    nt8 = t // 8
    per = tm // 8
    return [pl.BlockSpec((tm, width), lambda i: (i, col)),
            pl.BlockSpec((8, width), lambda i: (jnp.maximum(i * per - 1, 0), col)),
            pl.BlockSpec((8, width), lambda i: (jnp.minimum((i + 1) * per, nt8 - 1), col))]


def _fill_ext(ext, cur_ref, prev_ref, next_ref, tm, nt):
    i = pl.program_id(0)
    ext[0:8, :] = jnp.where(i == 0, 0.0, prev_ref[...])
    ext[8:8 + tm, :] = cur_ref[...]
    ext[8 + tm:16 + tm, :] = jnp.where(i == nt - 1, 0.0, next_ref[...])


CONV_RC = 16
CONV_CB = 512


def _conv_chunks(tm):
    return [(slice(cb, cb + CONV_CB), slice(rb, rb + CONV_RC))
            for cb in range(0, 1024, CONV_CB) for rb in range(0, tm, CONV_RC)]


def _conv_fwd(proj, conv_w8, conv_b, tm):
    t = proj.shape[0]
    nt = t // tm

    def body(u_ref, up_ref, un_ref, w_ref, b_ref, pre_ref, act_ref, ext):
        _fill_ext(ext, u_ref, up_ref, un_ref, tm, nt)
        for cs, rs in _conv_chunks(tm):
            pre = b_ref[:, cs] + w_ref[0:1, cs] * ext[pl.ds(rs.start + 6, CONV_RC), cs]
            for j in range(1, CONV_W):
                pre = pre + w_ref[j:j + 1, cs] * ext[pl.ds(rs.start + 6 + j, CONV_RC), cs]
            pre_ref[rs, cs] = pre
            act_ref[rs, cs] = _silu(pre)

    full = pl.BlockSpec((tm, 1024), lambda i: (i, 0))
    o = jax.ShapeDtypeStruct((t, 1024), F32)
    return pl.pallas_call(
        body, name="conv_fwd", grid=(nt,),
        in_specs=_halo_specs(tm, t, 2) + [pl.BlockSpec((8, 1024), lambda i: (0, 0)), _row(1024)],
        out_specs=[full, full], out_shape=[o, o],
        scratch_shapes=[pltpu.VMEM((tm + 16, 1024), F32)],
        compiler_params=_cparams(("parallel",)),
    )(proj, proj, proj, conv_w8, conv_b)


def _conv_bwd(dq, dk, pre, proj, conv_w8, tm):
    t = pre.shape[0]
    nt = t // tm

    def body(dq_ref, dqp_ref, dqn_ref, dk_ref, dkp_ref, dkn_ref, pre_ref, prep_ref, pren_ref,
             u_ref, up_ref, un_ref, w_ref, du_ref, gw_ref, gb_ref, extd, extu):
        i = pl.program_id(0)

        @pl.when(i == 0)
        def _():
            gw_ref[...] = jnp.zeros_like(gw_ref)
            gb_ref[...] = jnp.zeros_like(gb_ref)
        for rows, dqr, dkr, prr, edge in ((slice(0, 8), dqp_ref, dkp_ref, prep_ref, i == 0),
                                          (slice(8, 8 + tm), dq_ref, dk_ref, pre_ref, None),
                                          (slice(8 + tm, 16 + tm), dqn_ref, dkn_ref, pren_ref, i == nt - 1)):
            ds = _dsilu(prr[...])
            dl = dqr[...] * ds[:, 0:ML_W]
            dr = dkr[...] * ds[:, ML_W:]
            if edge is not None:
                dl = jnp.where(edge, 0.0, dl)
                dr = jnp.where(edge, 0.0, dr)
            extd[rows, 0:ML_W] = dl
            extd[rows, ML_W:] = dr
        _fill_ext(extu, u_ref, up_ref, un_ref, tm, nt)
        gb_ref[...] += jnp.sum(extd[8:8 + tm, :], axis=0, keepdims=True)
        gacc = None
        for cs, rs in _conv_chunks(tm):
            if rs.start == 0:
                gacc = [jnp.zeros((8, CONV_CB), F32) for _ in range(CONV_W)]
            du = w_ref[0:1, cs] * extd[pl.ds(rs.start + 10, CONV_RC), cs]
            for j in range(1, CONV_W):
                du = du + w_ref[j:j + 1, cs] * extd[pl.ds(rs.start + 10 - j, CONV_RC), cs]
            du_ref[rs, cs] = du
            dcur = extd[pl.ds(rs.start + 8, CONV_RC), cs]
            for j in range(CONV_W):
                prod = dcur * extu[pl.ds(rs.start + 6 + j, CONV_RC), cs]
                gacc[j] = gacc[j] + functools.reduce(
                    jnp.add, [prod[k:k + 8] for k in range(0, CONV_RC, 8)])
            if rs.stop == tm:
                for j in range(CONV_W):
                    gw_ref[j:j + 1, cs] += jnp.sum(gacc[j], axis=0, keepdims=True)

    full = pl.BlockSpec((tm, 1024), lambda i: (i, 0))
    return pl.pallas_call(
        body, name="conv_bwd", grid=(nt,),
        in_specs=_halo_specs(tm, t, 0, ML_W) + _halo_specs(tm, t, 0, ML_W) + _halo_specs(tm, t, 0)
        + _halo_specs(tm, t, 2) + [pl.BlockSpec((8, 1024), lambda i: (0, 0))],
        out_specs=[full, pl.BlockSpec((8, 1024), lambda i: (0, 0)), _row(1024)],
        out_shape=[jax.ShapeDtypeStruct((t, 1024), F32), jax.ShapeDtypeStruct((8, 1024), F32),
                   jax.ShapeDtypeStruct((1, 1024), F32)],
        scratch_shapes=[pltpu.VMEM((tm + 16, 1024), F32), pltpu.VMEM((tm + 16, 1024), F32)],
        compiler_params=_cparams(("arbitrary",)),
    )(dq, dq, dq, dk, dk, dk, pre, pre, pre, proj, proj, proj, conv_w8)


def _ml_consts(rev):
    iu = lax.broadcasted_iota(jnp.int32, (ML_CHUNK, ML_CHUNK), 0)
    js = lax.broadcasted_iota(jnp.int32, (ML_CHUNK, ML_CHUNK), 1)
    eye = iu == js
    le = iu <= js
    ge = iu >= js
    csum, csum_t, causal = (ge, le, le) if rev else (le, ge, ge)
    return eye, csum.astype(F32), csum_t.astype(F32), causal


def _col(row, eye):
    return jnp.sum(jnp.where(eye, row, 0.0), axis=1, keepdims=True)


def _rowof(col, eye):
    return jnp.sum(jnp.where(eye, col, 0.0), axis=0, keepdims=True)


def _ml_gates(gi, gf, m0, csum, rev):
    lf = jax.nn.log_sigmoid(gf)
    b_rows = jnp.dot(lf, csum, precision=HI, preferred_element_type=F32)
    bl = jnp.sum(lf, axis=1, keepdims=True)
    a_rows = bl - b_rows + gi
    mloc = jnp.max(a_rows, axis=1, keepdims=True)
    order = list(range(ML_NB))[::-1] if rev else list(range(ML_NB))
    mp, mn, decay = {}, {}, {}
    m = m0
    for n in order:
        mp[n] = m
        m = jnp.maximum(bl[n:n + 1] + m, mloc[n:n + 1])
        mn[n] = m
    for n in order:
        decay[n] = jnp.exp(bl[n:n + 1] + mp[n] - mn[n])
    return b_rows, a_rows, gi - b_rows, mp, mn, decay, order


def _ml_load(q_ref, k_ref, v_ref, n):
    sl = slice(n * ML_CHUNK, (n + 1) * ML_CHUNK)
    qb = q_ref[sl, :].astype(BF16)
    kb = (k_ref[sl, :] * (ML_HEAD_DIM ** -0.5)).astype(BF16)
    vn = v_ref[sl, :]
    return sl, qb, kb, vn


def _ml_state_scan(q_ref, k_ref, v_ref, a_rows, mn, decay, order, c0, n0, eye):
    ns = range(ML_NB)
    ld = [_ml_load(q_ref, k_ref, v_ref, n) for n in ns]
    acol = [_col(a_rows[n:n + 1], eye) for n in ns]
    wcol = [jnp.exp(acol[n] - mn[n]) for n in ns]
    u = [_tn((wcol[n] * ld[n][3]).astype(BF16), ld[n][2]) for n in ns]
    nu = [jnp.sum(wcol[n] * ld[n][2].astype(F32), axis=0, keepdims=True) for n in ns]
    cp, npv = {}, {}
    c, nv = c0, n0
    for n in order:
        cp[n], npv[n] = c, nv
        c = decay[n] * c + u[n]
        nv = decay[n] * nv + nu[n]
    return cp, npv, wcol, c, nv


def _ml_intra_all(q_ref, k_ref, v_ref, b_rows, imb_rows, mp, cp, npv, causal, eye):
    ns = range(ML_NB)
    ld = [_ml_load(q_ref, k_ref, v_ref, n) for n in ns]
    qk = [_nt(ld[n][1], ld[n][2]) for n in ns]
    cq = [_nt(ld[n][1], cp[n].astype(BF16)) for n in ns]
    b_col = [_col(b_rows[n:n + 1], eye) for n in ns]
    dlog = [jnp.where(causal, b_col[n] + imb_rows[n:n + 1], NEG) for n in ns]
    m_inter = [b_col[n] + mp[n] for n in ns]
    m_t = [jnp.maximum(m_inter[n], jnp.max(dlog[n], axis=1, keepdims=True)) for n in ns]
    pm = [jnp.exp(dlog[n] - m_t[n]) for n in ns]
    inter = [jnp.exp(m_inter[n] - m_t[n]) for n in ns]
    floor = [jnp.exp(-m_t[n]) for n in ns]
    s = [qk[n] * pm[n] for n in ns]
    qn = [jnp.sum(ld[n][1].astype(F32) * npv[n].astype(BF16).astype(F32), axis=1, keepdims=True) for n in ns]
    sv = [_nn(s[n].astype(BF16), ld[n][3].astype(BF16)) for n in ns]
    den = [jnp.sum(s[n], axis=1, keepdims=True) + inter[n] * qn[n] for n in ns]
    num = [sv[n] + inter[n] * cq[n] for n in ns]
    dn = [jnp.maximum(jnp.abs(den[n]), floor[n]) for n in ns]
    return ld, [dict(pm=pm[n], s=s[n], inter=inter[n], cq=cq[n], qn=qn[n], num=num[n], den=den[n],
                     floor=floor[n], dn=dn[n]) for n in ns]


def _ml_specs(t, rev):
    nblk = t // ML_TB
    blk = (lambda g: nblk - 1 - g) if rev else (lambda g: g)
    hps = ML_HPS
    tile = lambda c0: pl.BlockSpec((ML_TB, 128 * hps), lambda hg, g, c0=c0: (blk(g), c0 // hps + hg))
    gate = pl.BlockSpec((hps, ML_NB, ML_CHUNK), lambda hg, g: (hg, blk(g), 0))
    cchk = pl.BlockSpec((hps, 1, 128, 128), lambda hg, g: (hg, blk(g), 0, 0))
    nmchk = pl.BlockSpec((hps, 1, 8, 128), lambda hg, g: (hg, blk(g), 0, 0))
    return nblk, blk, tile, gate, cchk, nmchk


def _ml_head_views(refs, hh):
    cols = slice(hh * ML_HEAD_DIM, (hh + 1) * ML_HEAD_DIM)
    return [r.at[:, cols] if len(r.shape) == 2 else r.at[hh] for r in refs]


def _ml_fwd(qk_act, proj, gi, gf, rev, name):
    t = qk_act.shape[0]
    nblk, _, tile, gate, cchk, nmchk = _ml_specs(t, rev)

    def body(*refs):
        for hh in range(ML_HPS):
            one_head(*_ml_head_views(refs, hh))

    def one_head(q_ref, k_ref, v_ref, gi_ref, gf_ref, h_ref, cchk_ref, nmchk_ref, c_ref, nm_ref):
        @pl.when(pl.program_id(1) == 0)
        def _():
            c_ref[...] = jnp.zeros_like(c_ref)
            nm_ref[...] = jnp.zeros_like(nm_ref)
        cchk_ref[0] = c_ref[...]
        nmchk_ref[0] = nm_ref[...]
        eye, csum, _, causal = _ml_consts(rev)
        b_rows, a_rows, imb_rows, mp, mn, decay, order = _ml_gates(
            gi_ref[...], gf_ref[...], nm_ref[1:2, 0:1], csum, rev)
        cp, npv, _, c, nv = _ml_state_scan(q_ref, k_ref, v_ref, a_rows, mn, decay, order,
                                            c_ref[...], nm_ref[0:1, :], eye)
        c_ref[...] = c
        nm_ref[0:1, :] = nv
        nm_ref[1:2, :] = jnp.broadcast_to(mn[order[-1]], (1, 128))
        _, rs = _ml_intra_all(q_ref, k_ref, v_ref, b_rows, imb_rows, mp, cp, npv, causal, eye)
        for n in range(ML_NB):
            h_ref[n * ML_CHUNK:(n + 1) * ML_CHUNK, :] = rs[n]['num'] / rs[n]['dn']

    return pl.pallas_call(
        body, name=name, grid=(ML_HEADS // ML_HPS, nblk),
        in_specs=[tile(0), tile(4), tile(24), gate, gate],
        out_specs=[tile(0), cchk, nmchk],
        out_shape=[jax.ShapeDtypeStruct((t, ML_W), F32),
                   jax.ShapeDtypeStruct((ML_HEADS, nblk, 128, 128), F32),
                   jax.ShapeDtypeStruct((ML_HEADS, nblk, 8, 128), F32)],
        scratch_shapes=[pltpu.VMEM((ML_HPS, 128, 128), F32), pltpu.VMEM((ML_HPS, 8, 128), F32)],
        compiler_params=_cparams(("parallel", "arbitrary")),
    )(qk_act, qk_act, proj, gi, gf)


def _ml_bwd(qk_act, proj, gi, gf, dh, cchk_a, nmchk_a, prev, rev, name):
    t = qk_act.shape[0]
    nblk, _, tile, gate, cchk, nmchk = _ml_specs(t, not rev)

    def body(*refs):
        for hh in range(ML_HPS):
            one_head(*_ml_head_views(refs, hh))

    def one_head(q_ref, k_ref, v_ref, gi_ref, gf_ref, dh_ref, cchk_ref, nmchk_ref, *rest):
        prev_refs = rest[:len(prev)]
        dq_ref, dk_ref, dv_ref, dgi_ref, dgf_ref, dc_ref, dn_ref, db_scr, dbl_scr, di_scr = rest[len(prev):]

        def plus_prev(val, which, rows):
            return val + prev_refs[which][rows, :] if prev else val

        @pl.when(pl.program_id(1) == 0)
        def _():
            dc_ref[...] = jnp.zeros_like(dc_ref)
            dn_ref[...] = jnp.zeros_like(dn_ref)
        eye, csum, csum_t, causal = _ml_consts(rev)
        gfv = gf_ref[...]
        b_rows, a_rows, imb_rows, mp, mn, decay, order = _ml_gates(
            gi_ref[...], gfv, nmchk_ref[0, 1:2, 0:1], csum, rev)
        cp, npv, wcol, _, _ = _ml_state_scan(q_ref, k_ref, v_ref, a_rows, mn, decay, order,
                                             cchk_ref[0], nmchk_ref[0, 0:1, :], eye)
        ns = range(ML_NB)
        ld, rs = _ml_intra_all(q_ref, k_ref, v_ref, b_rows, imb_rows, mp, cp, npv, causal, eye)
        sls = [ld[n][0] for n in ns]
        qbs = [ld[n][1] for n in ns]
        kbs = [ld[n][2] for n in ns]
        vbs = [ld[n][3].astype(BF16) for n in ns]
        rdn = [1.0 / rs[n]['dn'] for n in ns]
        dnum = [dh_ref[sls[n], :] * rdn[n] for n in ns]
        hsum = [jnp.sum(dnum[n] * rs[n]['num'], axis=1, keepdims=True) for n in ns]
        dden = [jnp.where(jnp.abs(rs[n]['den']) > rs[n]['floor'],
                          -hsum[n] * rdn[n] * jnp.sign(rs[n]['den']), 0.0) for n in ns]
        dnb = [dnum[n].astype(BF16) for n in ns]
        dsf = [_nt(dnb[n], vbs[n]) + dden[n] for n in ns]
        dv0 = [_tn(rs[n]['s'].astype(BF16), dnb[n]) for n in ns]
        gb = [(dsf[n] * rs[n]['pm']).astype(BF16) for n in ns]
        cpb = [cp[n].astype(BF16) for n in ns]
        idd = [rs[n]['inter'] * dden[n] for n in ns]
        dqa = [_nn(gb[n], kbs[n]) for n in ns]
        dqc = [_nn(dnb[n], cpb[n]) for n in ns]
        dk0 = [_tn(gb[n], qbs[n]) for n in ns]
        xs = [_tn((rs[n]['inter'] * dnum[n]).astype(BF16), qbs[n]) for n in ns]
        for n in ns:
            dq_ref[sls[n], :] = plus_prev(dqa[n] + rs[n]['inter'] * dqc[n]
                                          + idd[n] * npv[n].astype(BF16).astype(F32), 0, sls[n])
        rr = [dsf[n] * rs[n]['s'] for n in ns]
        dinter = [jnp.sum(dnum[n] * rs[n]['cq'], axis=1, keepdims=True) + dden[n] * rs[n]['qn'] for n in ns]
        dbcol = [jnp.sum(rr[n], axis=1, keepdims=True) + dinter[n] * rs[n]['inter'] for n in ns]
        dimb = [jnp.sum(rr[n], axis=0, keepdims=True) for n in ns]
        xns = [jnp.sum(idd[n] * qbs[n].astype(F32), axis=0, keepdims=True) for n in ns]
        dcn, dnn = {}, {}
        dc, dn = dc_ref[...], dn_ref[0:1, :]
        for n in order[::-1]:
            dcn[n], dnn[n] = dc, dn
            dc = decay[n] * dc + xs[n]
            dn = decay[n] * dn + xns[n]
        dc_ref[...] = dc
        dn_ref[0:1, :] = dn
        kscale = ML_HEAD_DIM ** -0.5
        dcb = [dcn[n].astype(BF16) for n in ns]
        z = [_nn(vbs[n], dcb[n]) for n in ns]
        kd = [_nt(kbs[n], dcb[n]) for n in ns]
        ddecay = [jnp.sum(jnp.sum(dcn[n] * cp[n], axis=1, keepdims=True), axis=0, keepdims=True)
                  + jnp.sum(dnn[n] * npv[n], axis=1, keepdims=True) for n in ns]
        zd = [z[n] + dnn[n] for n in ns]
        dw = [jnp.sum(zd[n] * kbs[n].astype(F32), axis=1, keepdims=True) for n in ns]
        for n in ns:
            dv_ref[sls[n], :] = plus_prev(dv0[n] + wcol[n] * kd[n], 2, sls[n])
            dk_ref[sls[n], :] = plus_prev((dk0[n] + wcol[n] * zd[n]) * kscale, 1, sls[n])
        da = [dw[n] * wcol[n] for n in ns]
        dbl = [jnp.sum(da[n], axis=0, keepdims=True) + ddecay[n] * decay[n] for n in ns]
        da_row = [_rowof(da[n], eye) for n in ns]
        db_row = [_rowof(dbcol[n] - da[n], eye) for n in ns]
        for n in ns:
            db_scr[n:n + 1, :] = db_row[n] - dimb[n]
            di_scr[n:n + 1, :] = dimb[n] + da_row[n]
            dbl_scr[n:n + 1, :] = jnp.broadcast_to(dbl[n], (1, ML_CHUNK))
        dlf = jnp.dot(db_scr[...], csum_t, precision=HI, preferred_element_type=F32) + dbl_scr[...]
        dgf_ref[...] = dlf * jax.nn.sigmoid(-gfv)
        dgi_ref[...] = di_scr[...]

    nc = t // ML_CHUNK
    o512 = jax.ShapeDtypeStruct((t, ML_W), F32)
    og = jax.ShapeDtypeStruct((ML_HEADS, nc, ML_CHUNK), F32)
    return pl.pallas_call(
        body, name=name, grid=(ML_HEADS // ML_HPS, nblk),
        in_specs=[tile(0), tile(4), tile(24), gate, gate, tile(0), cchk, nmchk] + [tile(0)] * len(prev),
        out_specs=[tile(0), tile(0), tile(0), gate, gate],
        out_shape=[o512, o512, o512, og, og],
        scratch_shapes=[pltpu.VMEM((ML_HPS, 128, 128), F32), pltpu.VMEM((ML_HPS, 8, 128), F32)]
        + [pltpu.VMEM((ML_HPS, ML_NB, ML_CHUNK), F32)] * 3,
        compiler_params=_cparams(("parallel", "arbitrary")),
    )(qk_act, qk_act, proj, gi, gf, dh, cchk_a, nmchk_a, *prev)


def _gate_rows(gates16, t):
    g = gates16.reshape(t // ML_CHUNK, ML_CHUNK, 4, ML_HEADS).transpose(2, 3, 0, 1)
    return g[0], g[1], g[2], g[3]


def _gate_cols(dgi_f, dgf_f, dgi_b, dgf_b, t):
    g = jnp.stack([dgi_f, dgf_f, dgi_b, dgf_b]).transpose(2, 3, 0, 1).reshape(t, 4 * ML_HEADS)
    return jnp.pad(g, ((0, 0), (0, 128 - 4 * ML_HEADS)))


def _local_step(x, target, shift, scale, gate, norm_w, w_in_t, b_in_p, conv_w8, conv_b, rpb,
                ml_norm_w, w_out_b, final_norm_w):
    t = x.shape[0]
    rows = t // GRID_W
    tm = 512
    proj, gates = _in_proj(x, norm_w, scale, shift, w_in_t, b_in_p)
    tbl = _na_bias_table(rpb, rows)
    o_na, lse_na = _na_fwd(proj, tbl)
    pre, qk_act = _conv_fwd(proj, conv_w8, conv_b, tm)
    gi_f, gf_f, gi_b, gf_b = _gate_rows(gates[:, :4 * ML_HEADS], t)
    h_f, cchk_f, nmchk_f = _ml_fwd(qk_act, proj, gi_f, gf_f, False, "ml_fwd_f")
    h_b, cchk_b, nmchk_b = _ml_fwd(qk_act, proj, gi_b, gf_b, True, "ml_fwd_b")
    (loss, dres, d_ona, d_naz, dhs, d_o, d_z, dgate, g_fnw, g_mlnw, g_w_out) = _tail(
        o_na, proj, h_f, h_b, x, target, gate, ml_norm_w, final_norm_w, w_out_b)
    dq_na, dk_na, dv_na, rpbacc = _na_bwd(proj, tbl, d_ona, o_na, lse_na)
    g_rpb = _rpb_reduce(rpbacc, rows)
    dq_f, dk_f, dv_f, dgi_f, dgf_f = _ml_bwd(qk_act, proj, gi_f, gf_f, dhs, cchk_f, nmchk_f, (),
                                             False, "ml_bwd_f")
    dq_ml, dk_ml, dv_ml, dgi_b, dgf_b = _ml_bwd(qk_act, proj, gi_b, gf_b, dhs, cchk_b, nmchk_b, (dq_f, dk_f, dv_f),
                                                True, "ml_bwd_b")
    du, g_conv_w, g_conv_b = _conv_bwd(dq_ml, dk_ml, pre, proj, conv_w8, tm)
    dgates = _gate_cols(dgi_f, dgf_f, dgi_b, dgf_b, t)
    grad_x, g_w_in, g_b_in, dscale, dshift, g_nw = _in_bwd(
        [dq_na, dk_na, dv_na, d_naz, du, dv_ml, d_o, d_z, dgates], x, dres, w_in_t, norm_w, scale, shift)
    dmod = jnp.concatenate([dshift, dscale, dgate], axis=1)
    return (loss, grad_x, dmod, g_nw, g_w_in, g_b_in, g_conv_w, g_conv_b, g_rpb, g_mlnw, g_w_out, g_fnw)


MESH = pl.DeviceIdType.MESH
N_DEV = 8
ANY = pl.BlockSpec(memory_space=pl.ANY)
WHOLE_VMEM = pl.BlockSpec(memory_space=pltpu.VMEM)


def _allgather8(blocks, name):
    na = len(blocks)

    def body(*refs):
        x_refs = refs[:na]
        out_refs = refs[na:2 * na]
        send_sems, recv_sems, local_sems = refs[2 * na:]
        x, y, c = lax.axis_index("x"), lax.axis_index("y"), lax.axis_index("c")
        me, sibling = (x, y, c), (x, y, 1 - c)
        chips = [(1 - x, y), (x, 1 - y), (1 - x, 1 - y)]

        def rows(a, px, py, pc):
            return out_refs[a].at[4 * px + 2 * py + pc]

        def copy(a, k, block, to, src=None):
            return pltpu.make_async_remote_copy(
                src_ref=rows(a, *block) if src is None else src, dst_ref=rows(a, *block),
                send_sem=send_sems.at[a, k], recv_sem=recv_sems.at[a, k],
                device_id=to, device_id_type=MESH)

        mine, first, passed = [], [], []
        for a in range(na):
            cp = pltpu.make_async_copy(x_refs[a], rows(a, *me), local_sems.at[a])
            cp.start()
            mine.append(cp)
            first.append(copy(a, 0, me, sibling, src=x_refs[a]))
            first += [copy(a, 1 + j, me, (*chip, c), src=x_refs[a]) for j, chip in enumerate(chips)]
        for cp in first:
            cp.start()
        for a in range(na):
            for j, chip in enumerate(chips):
                copy(a, 1 + j, (*chip, c), me).wait_recv()
                fwd = copy(a, 4 + j, (*chip, c), sibling)
                fwd.start()
                passed.append(fwd)
        for a in range(na):
            copy(a, 0, sibling, me).wait_recv()
            for j, chip in enumerate(chips):
                copy(a, 4 + j, (*chip, 1 - c), me).wait_recv()
        for cp in first + passed:
            cp.wait_send()
        for cp in mine:
            cp.wait()

    return pl.pallas_call(
        body, name=name,
        out_shape=[jax.ShapeDtypeStruct((N_DEV,) + b.shape, b.dtype) for b in blocks],
        in_specs=[WHOLE_VMEM] * na, out_specs=[WHOLE_VMEM] * na,
        scratch_shapes=[pltpu.SemaphoreType.DMA((na, 7)), pltpu.SemaphoreType.DMA((na, 7)),
                        pltpu.SemaphoreType.DMA((na,))],
        compiler_params=pltpu.CompilerParams(vmem_limit_bytes=VMEM_LIMIT),
    )(*blocks)


def _pair_exchange(arrs, name):
    na = len(arrs)

    def body(*refs):
        in_refs = refs[:na]
        out_refs = refs[na:2 * na]
        send_sems, recv_sems = refs[2 * na:]
        sibling = (lax.axis_index("x"), lax.axis_index("y"), 1 - lax.axis_index("c"))
        copies = [pltpu.make_async_remote_copy(
            src_ref=in_refs[a], dst_ref=out_refs[a], send_sem=send_sems.at[a], recv_sem=recv_sems.at[a],
            device_id=sibling, device_id_type=MESH) for a in range(na)]
        for cp in copies:
            cp.start()
        for cp in copies:
            cp.wait()

    return pl.pallas_call(
        body, name=name,
        out_shape=[jax.ShapeDtypeStruct(a.shape, a.dtype) for a in arrs],
        in_specs=[ANY] * na, out_specs=[ANY] * na,
        scratch_shapes=[pltpu.SemaphoreType.DMA((na,)), pltpu.SemaphoreType.DMA((na,))],
    )(*arrs)


def _chip_exchange(arrs, name):
    na = len(arrs)

    def body(*refs):
        in_refs = refs[:na]
        out_refs = refs[na:2 * na]
        send_sems, recv_sems, local_sems = refs[2 * na:]
        x, y, c = lax.axis_index("x"), lax.axis_index("y"), lax.axis_index("c")
        my_chip = 2 * x + y
        chips = [(1 - x, y), (x, 1 - y), (1 - x, 1 - y)]
        local, remote = [], []
        for a in range(na):
            cp = pltpu.make_async_copy(in_refs[a].at[my_chip], out_refs[a].at[my_chip], local_sems.at[a])
            cp.start()
            local.append(cp)
            for j, (px, py) in enumerate(chips):
                cp = pltpu.make_async_remote_copy(
                    src_ref=in_refs[a].at[2 * px + py], dst_ref=out_refs[a].at[my_chip],
                    send_sem=send_sems.at[a, j], recv_sem=recv_sems.at[a, j],
                    device_id=(px, py, c), device_id_type=MESH)
                cp.start()
                remote.append(cp)
        for cp in remote:
            cp.wait()
        for cp in local:
            cp.wait()

    return pl.pallas_call(
        body, name=name,
        out_shape=[jax.ShapeDtypeStruct(a.shape, a.dtype) for a in arrs],
        in_specs=[ANY] * na, out_specs=[ANY] * na,
        scratch_shapes=[pltpu.SemaphoreType.DMA((na, 3)), pltpu.SemaphoreType.DMA((na, 3)),
                        pltpu.SemaphoreType.DMA((na,))],
    )(*arrs)


def _rows_tile(r):
    for cand in (512, 256, 128, 64, 32, 16, 8):
        if r % cand == 0:
            return cand
    return r


def _add2(a, b, name, out_dtype):
    s, r, n = a.shape
    tr = _rows_tile(r)

    def body(a_ref, b_ref, o_ref):
        o_ref[...] = (a_ref[...] + b_ref[...]).astype(out_dtype)

    spec = pl.BlockSpec((1, tr, n), lambda i, j: (i, j, 0))
    return pl.pallas_call(
        body, name=name, grid=(s, r // tr), in_specs=[spec, spec], out_specs=spec,
        out_shape=jax.ShapeDtypeStruct(a.shape, out_dtype),
        compiler_params=_cparams(("parallel", "parallel")),
    )(a, b)


def _sum_slabs(a, name):
    s, r, n = a.shape
    tr = _rows_tile(r)

    def body(a_ref, o_ref):
        acc = a_ref[0].astype(F32)
        for k in range(1, s):
            acc = acc + a_ref[k].astype(F32)
        o_ref[...] = acc

    return pl.pallas_call(
        body, name=name, grid=(r // tr,),
        in_specs=[pl.BlockSpec((s, tr, n), lambda i: (0, i, 0))],
        out_specs=pl.BlockSpec((tr, n), lambda i: (i, 0)),
        out_shape=jax.ShapeDtypeStruct((r, n), F32),
        compiler_params=_cparams(("parallel",)),
    )(a)


def _adamw(w, g, m, v, name):
    r, n = w.shape
    if r % 8 == 0:
        blk, grid, imap = (_rows_tile(r), n), (r // _rows_tile(r),), (lambda i: (i, 0))
    else:
        blk, grid, imap = (r, 128), (n // 128,), (lambda i: (0, i))
    c1 = 1.0 / (1.0 - ADAM_B1 ** ADAM_STEP)
    c2 = 1.0 / (1.0 - ADAM_B2 ** ADAM_STEP)

    def body(w_ref, g_ref, m_ref, v_ref, d_ref, nm_ref, nv_ref):
        gv = g_ref[...]
        nm = ADAM_B1 * m_ref[...] + (1.0 - ADAM_B1) * gv
        nv = ADAM_B2 * v_ref[...] + (1.0 - ADAM_B2) * (gv * gv)
        nm_ref[...] = nm
        nv_ref[...] = nv
        d_ref[...] = -ADAM_LR * ((nm * c1) / (jnp.sqrt(nv * c2) + ADAM_EPS) + ADAM_WD * w_ref[...])

    spec = pl.BlockSpec(blk, imap)
    o = jax.ShapeDtypeStruct((r, n), F32)
    return pl.pallas_call(
        body, name=name, grid=grid, in_specs=[spec] * 4, out_specs=[spec] * 3, out_shape=[o, o, o],
        compiler_params=_cparams(("parallel",)),
    )(w, g, m, v)


def _mod_fwd(c_all, w_ada_s, b_ada_s):
    def body(c_ref, w_ref, b_ref, o_ref):
        o_ref[...] = jnp.dot(_silu(c_ref[...]), w_ref[...], precision=HI, preferred_element_type=F32) + b_ref[...]

    return pl.pallas_call(
        body, name="mod_fwd", out_shape=jax.ShapeDtypeStruct((c_all.shape[0], w_ada_s.shape[1]), F32),
        in_specs=[WHOLE_VMEM] * 3, out_specs=WHOLE_VMEM,
        compiler_params=pltpu.CompilerParams(vmem_limit_bytes=VMEM_LIMIT),
    )(c_all, w_ada_s, b_ada_s)


def _wada_grad(c_all, dmod_s):
    def body(c_ref, d_ref, o_ref):
        o_ref[...] = lax.dot_general(_silu(c_ref[...]), d_ref[...], (((0,), (0,)), ((), ())),
                                     precision=HI, preferred_element_type=F32)

    return pl.pallas_call(
        body, name="w_ada_grad", out_shape=jax.ShapeDtypeStruct((c_all.shape[1], dmod_s.shape[1]), F32),
        in_specs=[WHOLE_VMEM] * 2, out_specs=WHOLE_VMEM,
        compiler_params=pltpu.CompilerParams(vmem_limit_bytes=VMEM_LIMIT),
    )(c_all, dmod_s)


SMALL_ROWS = 24


def _pad_rows(v, nrows):
    v = v.reshape(-1)
    return jnp.pad(v, (0, nrows * 1024 - v.shape[0])).reshape(nrows, 1024)


def _pack_small(b_ada, norm_w, b_in, conv_w_full, conv_b, rpb, ml_norm_w, final_norm_w, last):
    parts = [_pad_rows(b_ada, 3), _pad_rows(norm_w, 1), _pad_rows(b_in, 5), _pad_rows(conv_w_full, 5),
             _pad_rows(conv_b, 1), _pad_rows(rpb, 4), _pad_rows(ml_norm_w, 1), _pad_rows(final_norm_w, 1),
             _pad_rows(last, 3)]
    return jnp.concatenate(parts, axis=0)


def _unpack_small(p):
    return dict(b_ada=p[0:3].reshape(1, 3072), norm_w=p[3:4], b_in=p[4:9].reshape(-1)[:IN_W].reshape(1, IN_W),
                conv_w=p[9:14], conv_b=p[14:15],
                rpb=p[15:19].reshape(-1)[:NA_HEADS * 15 * 31].reshape(1, NA_HEADS, 15, 31),
                ml_norm_w=p[19:20, :ML_W], final_norm_w=p[20], last=p[21])


def kernel(x, c, w_ada, b_ada, norm_w, w_in, b_in, conv_w, conv_b, rpb, ml_norm_w, w_out, final_norm_w, loss_target, m_w_ada, m_b_ada, m_norm_w, m_w_in, m_b_in, m_conv_w, m_conv_b, m_rpb, m_ml_norm_w, m_w_out, m_final_norm_w, v_w_ada, v_b_ada, v_norm_w, v_w_in, v_b_in, v_conv_w, v_conv_b, v_rpb, v_ml_norm_w, v_w_out, v_final_norm_w):
    xi, yi, ci = lax.axis_index("x"), lax.axis_index("y"), lax.axis_index("c")
    chip = 2 * xi + yi
    dev = 2 * chip + ci
    t = x.shape[1]
    ada_n = w_ada.shape[2]
    in_n = w_in.shape[2]
    out_r = w_out.shape[1]

    c_blk = jnp.pad(c, ((0, 7), (0, 0)))
    w_in_t, m_w_in_t, v_w_in_t = w_in[0].T, m_w_in[0].T, v_w_in[0].T
    in_h = in_n // 2
    w_in_half = lax.dynamic_slice_in_dim(w_in_t, ci * in_h, in_h, axis=0).astype(BF16)
    w_out_half = lax.dynamic_slice_in_dim(w_out[0], ci * (out_r // 2), out_r // 2, axis=0).astype(BF16)
    conv_blk = jnp.pad(conv_w[0], ((0, 3), (0, 0)))
    c_g, conv_g, w_in_g, w_out_g = _allgather8([c_blk, conv_blk, w_in_half, w_out_half], "gather_c_weights")
    c_all = c_g[:, 0]
    w_out_g = w_out_g.reshape(D_MODEL, D_MODEL)
    b_ada_s = lax.dynamic_slice_in_dim(b_ada, chip * ada_n, ada_n, axis=1)
    mod_s = _mod_fwd(c_all, w_ada[0], b_ada_s)
    (mod_g,) = _allgather8([mod_s], "gather_mod")
    mod_mine = lax.dynamic_index_in_dim(mod_g, dev, axis=1, keepdims=False)
    mod = mod_mine[0::2].reshape(1, 3 * D_MODEL)
    shift, scale, gate = mod[:, :D_MODEL], mod[:, D_MODEL:2 * D_MODEL], mod[:, 2 * D_MODEL:]

    w_in_tp = jnp.pad(w_in_g.reshape(IN_W, D_MODEL), ((0, IN_PAD - IN_W), (0, 0)))
    b_in_p = jnp.pad(b_in, ((0, 0), (0, IN_PAD - IN_W)))
    conv_w8 = conv_g.reshape(4, 2, 8, conv_w.shape[2])[:, 0].transpose(1, 0, 2).reshape(8, D_MODEL)

    (loss, grad_x, dmod, g_nw, g_w_in, g_b_in, g_conv_w, g_conv_b, g_rpb, g_mlnw, g_w_out, g_fnw) = _local_step(
        x[0], loss_target[0], shift, scale, gate, norm_w, w_in_tp, b_in_p, conv_w8, conv_b, rpb[0],
        ml_norm_w, w_out_g, final_norm_w.reshape(1, D_MODEL))

    g_in_t = g_w_in

    def halves(a, per_chip, h):
        return jnp.stack([lax.dynamic_slice_in_dim(a, k * per_chip + h * (per_chip // 2), per_chip // 2, axis=0)
                          for k in range(4)])

    ri, ro = _pair_exchange([halves(g_in_t, in_n, 1 - ci), halves(g_w_out, out_r, 1 - ci)], "rs_pair")
    pi = _add2(halves(g_in_t, in_n, ci), ri, "rs_pair_add_in", BF16)
    po = _add2(halves(g_w_out, out_r, ci), ro, "rs_pair_add_out", BF16)
    qi, qo = _chip_exchange([pi, po], "rs_chips")
    si = _sum_slabs(qi, "rs_sum_in")
    so = _sum_slabs(qo, "rs_sum_out")
    ti, to = _pair_exchange([si, so], "rs_share")
    g_w_in_s = jnp.where(ci == 0, jnp.concatenate([si, ti], axis=0), jnp.concatenate([ti, si], axis=0))
    g_w_out_s = jnp.where(ci == 0, jnp.concatenate([so, to], axis=0), jnp.concatenate([to, so], axis=0))

    small = _pack_small(dmod, g_nw, g_b_in[:, :IN_W], g_conv_w[:CONV_W], g_conv_b, g_rpb, g_mlnw, g_fnw,
                        jnp.pad(loss, ((0, 0), (0, 1024 - 128))))
    (small_g,) = _allgather8([small], "gather_small")
    small_sum = _sum_slabs(small_g, "small_sum")
    gs = _unpack_small(small_sum)
    dmod_all = small_g[:, 0:3].reshape(N_DEV, 3 * D_MODEL)
    g_w_ada_s = _wada_grad(c_all, lax.dynamic_slice_in_dim(dmod_all, chip * ada_n, ada_n, axis=1))
    g_conv_w_s = lax.dynamic_slice_in_dim(gs['conv_w'], chip * conv_w.shape[2], conv_w.shape[2], axis=1)
    loss_total = gs['last'][0]

    zeros3 = jnp.zeros((3, 1024), F32)
    zc = jnp.zeros((CONV_W, D_MODEL), F32)
    pw = _pack_small(b_ada, norm_w, b_in, zc, conv_b, rpb, ml_norm_w, final_norm_w, zeros3)
    pm = _pack_small(m_b_ada, m_norm_w, m_b_in, zc, m_conv_b, m_rpb, m_ml_norm_w, m_final_norm_w, zeros3)
    pv = _pack_small(v_b_ada, v_norm_w, v_b_in, zc, v_conv_b, v_rpb, v_ml_norm_w, v_final_norm_w, zeros3)
    ds_, nms, nvs = [_unpack_small(a) for a in _adamw(pw, small_sum, pm, pv, "adamw_small")]
    d_ada, nm_ada, nv_ada = _adamw(w_ada[0], g_w_ada_s, m_w_ada[0], v_w_ada[0], "adamw_w_ada")
    d_in, nm_in, nv_in = _adamw(w_in_t, g_w_in_s, m_w_in_t, v_w_in_t, "adamw_w_in")
    d_out, nm_out, nv_out = _adamw(w_out[0], g_w_out_s, m_w_out[0], v_w_out[0], "adamw_w_out")
    d_cw, nm_cw, nv_cw = _adamw(conv_w[0], g_conv_w_s, m_conv_w[0], v_conv_w[0], "adamw_conv_w")

    def group(big_ada, big_in, big_out, cw, sm):
        return (big_ada[None], sm['b_ada'], sm['norm_w'], big_in.T[None], sm['b_in'], cw[None], sm['conv_b'],
                sm['rpb'], sm['ml_norm_w'], big_out[None], sm['final_norm_w'])

    return ((loss_total, grad_x[None])
            + group(g_w_ada_s, g_w_in_s, g_w_out_s, g_conv_w_s, gs)
            + group(d_ada, d_in, d_out, d_cw, ds_)
            + group(nm_ada, nm_in, nm_out, nm_cw, nms)
            + group(nv_ada, nv_in, nv_out, nv_cw, nvs))
```

```python
import functools

import numpy as np
import jax
import jax.numpy as jnp
from jax import lax
from jax.experimental import pallas as pl
from jax.experimental.pallas import tpu as pltpu

F32 = jnp.float32
BF16 = jnp.bfloat16
HI = lax.Precision.HIGHEST

D_MODEL = 1024
GRID_W = 64
NA_W = 512
NA_HEAD_DIM = 64
NA_HEADS = 8
NA_KH = 8
NA_KW = 16
ML_W = 512
ML_HEADS = 4
ML_HEAD_DIM = 128
ML_CHUNK = 128
CONV_W = 5
EPS = 1e-6
IN_W = 4 * NA_W + 5 * ML_W + 4 * ML_HEADS
IN_MAIN = 4 * NA_W + 5 * ML_W
IN_PAD = IN_MAIN + 128
NEG = -1e30

ADAM_LR = 0.001
ADAM_B1 = 0.9
ADAM_B2 = 0.999
ADAM_EPS = 1e-08
ADAM_WD = 0.01
ADAM_STEP = 10

NA_QROWS = 8
NA_KROWS = 16
NA_QT = NA_QROWS * GRID_W
NA_KT = NA_KROWS * GRID_W
NA_KCH = 256
NA_RC = 32
ML_NB = 16
ML_TB = ML_NB * ML_CHUNK
ML_HPS = 1

VMEM_LIMIT = 56 * 1024 * 1024
IN_BWD_VMEM_LIMIT = 60 * 1024 * 1024


def _cparams(sem, vmem=VMEM_LIMIT):
    return pltpu.CompilerParams(dimension_semantics=sem, vmem_limit_bytes=vmem)


def _silu(x):
    return x * jax.nn.sigmoid(x)


def _dsilu(x):
    s = jax.nn.sigmoid(x)
    return s * (1.0 + x * (1.0 - s))


def _dot(a, b, dims):
    return lax.dot_general(a, b, (dims, ((), ())), preferred_element_type=F32)


def _nn(a, b):
    return _dot(a, b, ((1,), (0,)))


def _nt(a, b):
    return _dot(a, b, ((1,), (1,)))


def _tn(a, b):
    return _dot(a, b, ((0,), (0,)))


def _row(n):
    return pl.BlockSpec((1, n), lambda i: (0, 0))


def _modulated_norm(xv, nw, sc, sh):
    r = lax.rsqrt(jnp.mean(xv * xv, axis=-1, keepdims=True) + EPS)
    xn = xv * r
    return xn * nw * (1.0 + sc) + sh, xn, r


IN_TN = 768


def _in_proj(x, norm_w, scale, shift, w_in_t, b_in_p):
    t, d = x.shape
    tm = 2048
    gcol = IN_MAIN // 128

    def body(x_ref, nw_ref, sc_ref, sh_ref, w_ref, b_ref, wg_ref, bg_ref, proj_ref, g_ref, h_scr):
        @pl.when(pl.program_id(1) == 0)
        def _():
            h, _, _ = _modulated_norm(x_ref[...], nw_ref[...], sc_ref[...], sh_ref[...])
            h_scr[...] = h.astype(BF16)
            g_ref[...] = _nt(h_scr[...], wg_ref[...]) + bg_ref[...]
        proj_ref[...] = (_nt(h_scr[...], w_ref[...]) + b_ref[...]).astype(BF16)

    row = lambda n: pl.BlockSpec((1, n), lambda i, j: (0, 0))
    return pl.pallas_call(
        body, name="in_proj", grid=(t // tm, IN_MAIN // IN_TN),
        in_specs=[pl.BlockSpec((tm, d), lambda i, j: (i, 0)), row(d), row(d), row(d),
                  pl.BlockSpec((IN_TN, d), lambda i, j: (j, 0)), pl.BlockSpec((1, IN_TN), lambda i, j: (0, j)),
                  pl.BlockSpec((128, d), lambda i, j: (gcol, 0)), pl.BlockSpec((1, 128), lambda i, j: (0, gcol))],
        out_specs=[pl.BlockSpec((tm, IN_TN), lambda i, j: (i, j)), pl.BlockSpec((tm, 128), lambda i, j: (i, 0))],
        out_shape=[jax.ShapeDtypeStruct((t, IN_MAIN), BF16), jax.ShapeDtypeStruct((t, 128), F32)],
        scratch_shapes=[pltpu.VMEM((tm, d), BF16)],
        compiler_params=_cparams(("parallel", "arbitrary")),
    )(x, norm_w, scale, shift, w_in_t, b_in_p, w_in_t, b_in_p)


def _ml_norm_parts(hs, o, z, nw):
    outs = []
    for hh in range(ML_HEADS):
        sl = slice(hh * ML_HEAD_DIM, (hh + 1) * ML_HEAD_DIM)
        hm = hs[:, sl] * jax.nn.sigmoid(o[:, sl])
        mu = jnp.mean(hm, axis=-1, keepdims=True)
        cen = hm - mu
        var = jnp.mean(cen * cen, axis=-1, keepdims=True)
        rs = lax.rsqrt(var + EPS)
        outs.append((sl, cen * rs, rs))
    return outs


def _tail(o_na, proj, h_f, h_b, x, target, gate, ml_norm_w, fnw, w_out_b):
    t, d = x.shape
    tm = 256

    def body(ona_ref, naz_ref, hf_ref, hb_ref, o_ref, z_ref, x_ref, tg_ref, g_ref, nw_ref, fw_ref, w_ref,
             loss_ref, dres_ref, dona_ref, dnaz_ref, dhs_ref, do_ref, dz_ref, dgate_ref, gfw_ref, gnw_ref,
             gwo_ref, mix_scr):
        @pl.when(pl.program_id(0) == 0)
        def _():
            for r in (loss_ref, dgate_ref, gfw_ref, gnw_ref, gwo_ref):
                r[...] = jnp.zeros_like(r)
        naz = naz_ref[...].astype(F32)
        ona = ona_ref[...]
        sna = _silu(naz)
        mix_scr[:, 0:NA_W] = (ona * sna).astype(BF16)
        hs = hf_ref[...] + hb_ref[...]
        z = z_ref[...].astype(F32)
        ov = o_ref[...].astype(F32)
        parts = _ml_norm_parts(hs, ov, z, nw_ref[...])
        szs = []
        for sl, xn, _ in parts:
            sz = _silu(z[:, sl])
            szs.append(sz)
            mix_scr[:, NA_W + sl.start:NA_W + sl.stop] = (xn * nw_ref[:, sl] * sz).astype(BF16)
        mixb = mix_scr[...]
        wv = w_ref[...]
        yv = _nn(mixb, wv)
        gate_v = g_ref[...]
        hres = x_ref[...] + gate_v * yv
        r = lax.rsqrt(jnp.mean(hres * hres, axis=-1, keepdims=True) + EPS)
        xnf = hres * r
        err = xnf * fw_ref[...] - tg_ref[...]
        loss_ref[...] += 0.5 * jnp.sum(jnp.sum(err * err, axis=-1, keepdims=True) * (1.0 / d), axis=0, keepdims=True)
        dout = err * (1.0 / d)
        gfw_ref[...] += jnp.sum(dout * xnf, axis=0, keepdims=True)
        dxn = dout * fw_ref[...]
        dres = r * (dxn - xnf * jnp.mean(dxn * xnf, axis=-1, keepdims=True))
        dres_ref[...] = dres
        dgate_ref[...] += jnp.sum(dres * yv, axis=0, keepdims=True)
        dyb = (dres * gate_v).astype(BF16)
        gwo_ref[...] += _tn(mixb, dyb)
        dmix = _nt(dyb, wv)
        dna = dmix[:, 0:NA_W]
        dona_ref[...] = dna * sna
        dnaz_ref[...] = (dna * ona * _dsilu(naz)).astype(BF16)
        for (sl, xn, rs), sz in zip(parts, szs):
            dyv = dmix[:, NA_W + sl.start:NA_W + sl.stop]
            zz = z[:, sl]
            w = nw_ref[:, sl]
            dz_ref[:, sl] = (dyv * xn * w * _dsilu(zz)).astype(BF16)
            gnw_ref[:, sl] += jnp.sum(dyv * xn * sz, axis=0, keepdims=True)
            dxm = dyv * w * sz
            dhm = rs * (dxm - jnp.mean(dxm, axis=-1, keepdims=True)
                        - xn * jnp.mean(dxm * xn, axis=-1, keepdims=True))
            so = jax.nn.sigmoid(ov[:, sl])
            dhs_ref[:, sl] = dhm * so
            do_ref[:, sl] = (dhm * hs[:, sl] * so * (1.0 - so)).astype(BF16)

    blk = lambda c: pl.BlockSpec((tm, 512), lambda i, c=c: (i, c))
    full = pl.BlockSpec((tm, d), lambda i: (i, 0))
    o512 = jax.ShapeDtypeStruct((t, 512), F32)
    b512 = jax.ShapeDtypeStruct((t, 512), BF16)
    whole = pl.BlockSpec((d, d), lambda i: (0, 0))
    return pl.pallas_call(
        body, name="tail", grid=(t // tm,),
        in_specs=[blk(0), blk(3), blk(0), blk(0), blk(7), blk(8), full, full, _row(d), _row(ML_W), _row(d), whole],
        out_specs=[pl.BlockSpec((1, 128), lambda i: (0, 0)), full] + [blk(0)] * 5
        + [_row(d), _row(d), _row(ML_W), whole],
        out_shape=[jax.ShapeDtypeStruct((1, 128), F32), jax.ShapeDtypeStruct((t, d), F32),
                   o512, b512, o512, b512, b512]
        + [jax.ShapeDtypeStruct((1, d), F32), jax.ShapeDtypeStruct((1, d), F32),
           jax.ShapeDtypeStruct((1, ML_W), F32), jax.ShapeDtypeStruct((d, d), F32)],
        scratch_shapes=[pltpu.VMEM((tm, d), BF16)],
        compiler_params=_cparams(("arbitrary",)),
    )(o_na, proj, h_f, h_b, proj, proj, x, target, gate, ml_norm_w, fnw, w_out_b)


def _in_bwd(pieces, x, dres, w_in_t, norm_w, scale, shift):
    t, d = x.shape
    tm = 512
    nt = t // tm
    widths = [p.shape[1] for p in pieces]
    offs = [sum(widths[:k]) for k in range(len(widths))]
    assert sum(widths) == IN_PAD
    npc = len(pieces)

    def body(*refs):
        p_refs = refs[:npc]
        (x_ref, dres_ref, w_hbm, nw_ref, sc_ref, sh_ref,
         gx_ref, gw_hbm, gb_ref, dsc_ref, dsh_ref, gnw_ref, w_vmem, acc, stage, sem) = refs[npc:]
        i = pl.program_id(0)

        @pl.when(i == 0)
        def _():
            cp = pltpu.make_async_copy(w_hbm, w_vmem, sem.at[0])
            cp.start()
            acc[...] = jnp.zeros_like(acc)
            gb_ref[...] = jnp.zeros_like(gb_ref)
            dsc_ref[...] = jnp.zeros_like(dsc_ref)
            dsh_ref[...] = jnp.zeros_like(dsh_ref)
            gnw_ref[...] = jnp.zeros_like(gnw_ref)
            cp.wait()

        nw = nw_ref[...]
        s1 = 1.0 + sc_ref[...]
        h, xn, r = _modulated_norm(x_ref[...], nw, sc_ref[...], sh_ref[...])
        hb = h.astype(BF16)
        dhv = jnp.zeros((tm, d), F32)
        for p_ref, c0, w in zip(p_refs, offs, widths):
            pt = p_ref[...]
            pb = pt.astype(BF16)
            dhv = dhv + _nn(pb, w_vmem[c0:c0 + w, :])
            acc[:, c0:c0 + w] += _tn(hb, pb)
            gb_ref[:, c0:c0 + w] += jnp.sum(pt.astype(F32), axis=0, keepdims=True)
        dsh_ref[...] += jnp.sum(dhv, axis=0, keepdims=True)
        dsc_ref[...] += jnp.sum(dhv * xn * nw, axis=0, keepdims=True)
        gnw_ref[...] += jnp.sum(dhv * xn * s1, axis=0, keepdims=True)
        dxn = dhv * nw * s1
        gx_ref[...] = dres_ref[...] + r * (dxn - xn * jnp.mean(dxn * xn, axis=-1, keepdims=True))

        @pl.when(i == nt - 1)
        def _():
            copies = []
            for blk in range(IN_PAD // 128):
                slot = blk % 2
                if blk >= 2:
                    copies[blk - 2].wait()
                stage[slot] = acc[:, blk * 128:(blk + 1) * 128].T
                cp = pltpu.make_async_copy(stage.at[slot], gw_hbm.at[pl.ds(blk * 128, 128), :], sem.at[1 + slot])
                cp.start()
                copies.append(cp)
            copies[-2].wait()
            copies[-1].wait()

    full = pl.BlockSpec((tm, d), lambda i: (i, 0))
    return pl.pallas_call(
        body, name="in_bwd", grid=(nt,),
        in_specs=[pl.BlockSpec((tm, w), lambda i: (i, 0)) for w in widths]
        + [full, full, pl.BlockSpec(memory_space=pl.ANY), _row(d), _row(d), _row(d)],
        out_specs=[full, pl.BlockSpec(memory_space=pl.ANY), _row(IN_PAD), _row(d), _row(d), _row(d)],
        out_shape=[jax.ShapeDtypeStruct((t, d), F32), jax.ShapeDtypeStruct((IN_PAD, d), F32),
                   jax.ShapeDtypeStruct((1, IN_PAD), F32)] + [jax.ShapeDtypeStruct((1, d), F32)] * 3,
        scratch_shapes=[pltpu.VMEM((IN_PAD, d), BF16), pltpu.VMEM((d, IN_PAD), F32),
                        pltpu.VMEM((2, 128, d), F32), pltpu.SemaphoreType.DMA((3,))],
        compiler_params=_cparams(("arbitrary",), IN_BWD_VMEM_LIMIT),
    )(*pieces, x, dres, w_in_t, norm_w, scale, shift)


def _na_static(rows):
    cases = [(0, 0), (NA_QROWS, NA_QROWS - 4), (rows - NA_QROWS, rows - NA_KROWS)]
    dy = np.zeros((3, NA_QROWS, NA_KROWS), np.int32)
    rv = np.zeros((3, NA_QROWS, NA_KROWS), bool)
    for cs, (r0, kr0) in enumerate(cases):
        for i in range(NA_QROWS):
            for j in range(NA_KROWS):
                r, kr = r0 + i, kr0 + j
                rs = min(max(r - NA_KH // 2, 0), rows - NA_KH)
                rv[cs, i, j] = rs <= kr <= rs + NA_KH - 1
                dy[cs, i, j] = min(max(kr - r + NA_KH - 1, 0), 2 * NA_KH - 2)
    cq = np.arange(GRID_W)[:, None]
    ck = np.arange(GRID_W)[None, :]
    cs0 = np.clip(cq - NA_KW // 2, 0, GRID_W - NA_KW)
    cv = (ck >= cs0) & (ck < cs0 + NA_KW)
    dx = np.clip(ck - cq, -(NA_KW - 1), NA_KW - 1) + NA_KW - 1
    return dy, rv, dx.astype(np.int32), cv


def _na_bias_table(rpb, rows):
    _, _, dx, cv = _na_static(rows)
    ndy = 2 * NA_KH - 1
    onehot = (dx.reshape(1, -1) == np.arange(2 * NA_KW - 1)[:, None]).astype(np.float32)
    rpx = jnp.dot(rpb.reshape(NA_HEADS * ndy, 2 * NA_KW - 1), jnp.asarray(onehot), precision=HI)
    rpx = jnp.where(cv[None, None], rpx.reshape(NA_HEADS, ndy, GRID_W, GRID_W), NEG)
    neg = jnp.full((NA_HEADS, 1, GRID_W, GRID_W), NEG, F32)
    rpx = jnp.concatenate([rpx, neg], axis=1)
    nxt = jnp.concatenate([rpx[:, 1:], neg], axis=1)
    negs = jnp.broadcast_to(neg, rpx.shape)
    pairs = jnp.concatenate([jnp.concatenate([rpx, nxt], axis=3), jnp.concatenate([rpx, negs], axis=3),
                             jnp.concatenate([negs, rpx], axis=3)], axis=1)
    npair = pairs.shape[1]

    def body(m_ref, o_ref):
        cs = pl.program_id(1)
        r0 = jnp.where(cs == 0, 0, jnp.where(cs == 1, NA_QROWS, rows - NA_QROWS))
        kr0 = jnp.where(cs == 0, 0, jnp.where(cs == 1, NA_QROWS - NA_KH // 2, rows - NA_KROWS))
        for i in range(NA_QROWS):
            r = r0 + i
            rs = jnp.clip(r - NA_KH // 2, 0, rows - NA_KH)
            for jp in range(NA_KROWS // 2):
                kl = kr0 + 2 * jp
                vl = (kl >= rs) & (kl <= rs + NA_KH - 1)
                vr = (kl + 1 >= rs) & (kl + 1 <= rs + NA_KH - 1)
                dyl = jnp.clip(kl - r + NA_KH - 1, 0, ndy)
                dyr = jnp.clip(kl + 1 - r + NA_KH - 1, 0, ndy)
                idx = jnp.where(vl & vr, dyl, jnp.where(vl, 16 + dyl, jnp.where(vr, 32 + dyr, 16 + ndy)))
                o_ref[0, 0, i * GRID_W:(i + 1) * GRID_W, jp * 128:(jp + 1) * 128] = m_ref[0, idx]

    return pl.pallas_call(
        body, name="na_bias_table", grid=(NA_HEADS, 3),
        in_specs=[pl.BlockSpec((1, npair, GRID_W, 128), lambda h, cs: (h, 0, 0, 0))],
        out_specs=pl.BlockSpec((1, 1, NA_QT, NA_KT), lambda h, cs: (h, cs, 0, 0)),
        out_shape=jax.ShapeDtypeStruct((NA_HEADS, 3, NA_QT, NA_KT), F32),
        compiler_params=_cparams(("parallel", "parallel")),
    )(pairs)


def _na_specs(t):
    nb = t // NA_QT
    nkb = t // NA_KCH
    npieces = NA_KT // NA_KCH

    def kb0(b):
        return jnp.clip(b * (NA_QT // NA_KCH) - 1, 0, nkb - npieces)

    def case(b):
        return jnp.where(b == 0, 0, jnp.where(b == nb - 1, 2, 1))

    q_spec = pl.BlockSpec((NA_QT, 128), lambda p, b: (b, p))
    k_specs = [pl.BlockSpec((NA_KCH, 128), lambda p, b, i=i: (kb0(b) + i, 4 + p)) for i in range(npieces)]
    v_specs = [pl.BlockSpec((NA_KCH, 128), lambda p, b, i=i: (kb0(b) + i, 8 + p)) for i in range(npieces)]
    tbl_spec = pl.BlockSpec((2, 1, NA_QT, NA_KT), lambda p, b: (p, case(b), 0, 0))
    io_spec = pl.BlockSpec((NA_QT, 128), lambda p, b: (b, p))
    return nb, npieces, kb0, case, q_spec, k_specs, v_specs, tbl_spec, io_spec


def _na_probs(qh, ks, tbl_ref, hh, npieces):
    s = [_nt(qh, ks[i]) + tbl_ref[hh, 0, :, i * NA_KCH:(i + 1) * NA_KCH] for i in range(npieces)]
    m = functools.reduce(jnp.maximum, [jnp.max(si, axis=1, keepdims=True) for si in s])
    p = [jnp.exp(si - m) for si in s]
    l = functools.reduce(jnp.add, [jnp.sum(pi, axis=1, keepdims=True) for pi in p])
    return p, m, l


def _na_fwd(proj, tbl):
    t = proj.shape[0]
    nb, npieces, _, _, q_spec, k_specs, v_specs, tbl_spec, io_spec = _na_specs(t)
    lse_spec = pl.BlockSpec((1, NA_QT, 2), lambda p, b: (p, b, 0))

    def body(*refs):
        q_ref = refs[0]
        k_refs = refs[1:1 + npieces]
        v_refs = refs[1 + npieces:1 + 2 * npieces]
        tbl_ref, o_ref, lse_ref = refs[1 + 2 * npieces:]
        lane = lax.broadcasted_iota(jnp.int32, (1, 128), 1)
        qv = q_ref[...].astype(F32) * (NA_HEAD_DIM ** -0.5)
        ks = [r[...].astype(BF16) for r in k_refs]
        vs = [r[...].astype(BF16) for r in v_refs]
        hs = range(2)
        msk = [(lane // NA_HEAD_DIM) == hh for hh in hs]
        qh = [jnp.where(msk[hh], qv, 0.0).astype(BF16) for hh in hs]
        pml = [_na_probs(qh[hh], ks, tbl_ref, hh, npieces) for hh in hs]
        pb = [[pml[hh][0][i].astype(BF16) for i in range(npieces)] for hh in hs]
        o = [functools.reduce(jnp.add, [_nn(pb[hh][i], vs[i]) for i in range(npieces)]) for hh in hs]
        for hh in hs:
            lse_ref[0, :, hh:hh + 1] = pml[hh][1] + jnp.log(pml[hh][2])
        o_ref[...] = jnp.where(msk[0], o[0] / pml[0][2], o[1] / pml[1][2])

    return pl.pallas_call(
        body, name="na_fwd", grid=(4, nb),
        in_specs=[q_spec] + k_specs + v_specs + [tbl_spec],
        out_specs=[io_spec, lse_spec],
        out_shape=[jax.ShapeDtypeStruct((t, NA_W), F32), jax.ShapeDtypeStruct((4, t, 2), F32)],
        compiler_params=_cparams(("parallel", "arbitrary")),
    )(*([proj] * (1 + 2 * npieces)), tbl)


def _na_bwd(proj, tbl, d_o, o_na, lse):
    t = proj.shape[0]
    nb, npieces, kb0, case, q_spec, k_specs, v_specs, tbl_spec, io_spec = _na_specs(t)

    def body(*refs):
        q_ref = refs[0]
        k_refs = refs[1:1 + npieces]
        v_refs = refs[1 + npieces:1 + 2 * npieces]
        (tbl_ref, do_ref, o_ref, lse_ref, dq_ref, dk_hbm, dv_hbm, rpb_ref,
         dk_acc, dv_acc, dk_out, dv_out, s_scr, dp_scr, dsb_scr, pnb_scr, sem) = refs[1 + 2 * npieces:]
        p_id = pl.program_id(0)
        b = pl.program_id(1)

        @pl.when(b == 0)
        def _():
            dk_acc[...] = jnp.zeros_like(dk_acc)
            dv_acc[...] = jnp.zeros_like(dv_acc)

        @pl.when((b == 0) | (b == 1) | (b == nb - 1))
        def _():
            rpb_ref[...] = jnp.zeros_like(rpb_ref)

        lane = lax.broadcasted_iota(jnp.int32, (1, 128), 1)
        scale = NA_HEAD_DIM ** -0.5
        qv = q_ref[...].astype(F32) * scale
        ks = [r[...].astype(BF16) for r in k_refs]
        vs = [r[...].astype(BF16) for r in v_refs]
        dov = do_ref[...]
        ov = o_ref[...]
        tok0 = kb0(b) * NA_KCH
        hs = range(2)
        ns = range(npieces)
        msk = [(lane // NA_HEAD_DIM) == hh for hh in hs]
        qh = [jnp.where(msk[hh], qv, 0.0).astype(BF16) for hh in hs]
        doh = [jnp.where(msk[hh], dov, 0.0) for hh in hs]
        dohb = [doh[hh].astype(BF16) for hh in hs]
        dd = [jnp.sum(doh[hh] * ov, axis=1, keepdims=True) for hh in hs]
        for hh in hs:
            for i in ns:
                slot = (hh * npieces + i) % 2
                cols = slice(i * NA_KCH, (i + 1) * NA_KCH)
                s_scr[slot] = _nt(qh[hh], ks[i])
                dp_scr[slot] = _nt(dohb[hh], vs[i])
                for r0 in range(0, NA_QT, NA_RC):
                    rows = slice(r0, r0 + NA_RC)
                    p = jnp.exp(s_scr[slot, rows, :] + tbl_ref[hh, 0, rows, cols] - lse_ref[0, rows, hh:hh + 1])
                    d = p * (dp_scr[slot, rows, :] - dd[hh][rows])
                    pnb_scr[hh, rows, cols] = p.astype(BF16)
                    dsb_scr[hh, rows, cols] = d.astype(BF16)
        dqh = [functools.reduce(jnp.add, [_nn(dsb_scr[hh, :, i * NA_KCH:(i + 1) * NA_KCH], ks[i]) for i in ns])
               for hh in hs]
        dq_ref[...] = (jnp.where(msk[0], dqh[0], dqh[1]) * scale).astype(BF16)
        for i in ns:
            rows = pl.ds(pl.multiple_of(tok0 + i * NA_KCH, NA_KCH), NA_KCH)
            cols = slice(i * NA_KCH, (i + 1) * NA_KCH)
            dk_acc[rows, :] += _tn(dsb_scr[0, :, cols], qh[0]) + _tn(dsb_scr[1, :, cols], qh[1])
            dv_acc[rows, :] += _tn(pnb_scr[0, :, cols], dohb[0]) + _tn(pnb_scr[1, :, cols], dohb[1])
        for hh in hs:
            acc = dsb_scr[hh, 0:GRID_W, :].astype(F32)
            for i in range(1, NA_QROWS):
                acc = acc + pltpu.roll(dsb_scr[hh, i * GRID_W:(i + 1) * GRID_W, :].astype(F32),
                                       NA_KT - i * GRID_W, 1)
            rpb_ref[0, 0, hh] += acc

        @pl.when(b == nb - 1)
        def _():
            cols = pl.ds(pl.multiple_of(p_id * 128, 128), 128)
            dk_out[...] = dk_acc[...].astype(BF16)
            dv_out[...] = dv_acc[...].astype(BF16)
            ck = pltpu.make_async_copy(dk_out, dk_hbm.at[:, cols], sem.at[0])
            cv = pltpu.make_async_copy(dv_out, dv_hbm.at[:, cols], sem.at[1])
            ck.start()
            cv.start()
            ck.wait()
            cv.wait()

    o512 = jax.ShapeDtypeStruct((t, NA_W), BF16)
    return pl.pallas_call(
        body, name="na_bwd", grid=(4, nb),
        in_specs=[q_spec] + k_specs + v_specs + [tbl_spec, io_spec, io_spec,
                                                 pl.BlockSpec((1, NA_QT, 2), lambda p, b: (p, b, 0))],
        out_specs=[io_spec, pl.BlockSpec(memory_space=pl.ANY), pl.BlockSpec(memory_space=pl.ANY),
                   pl.BlockSpec((1, 1, 2, GRID_W, NA_KT), lambda p, b: (p, case(b), 0, 0, 0))],
        out_shape=[o512, o512, o512, jax.ShapeDtypeStruct((4, 3, 2, GRID_W, NA_KT), F32)],
        scratch_shapes=[pltpu.VMEM((t, 128), F32), pltpu.VMEM((t, 128), F32),
                        pltpu.VMEM((t, 128), BF16), pltpu.VMEM((t, 128), BF16),
                        pltpu.VMEM((2, NA_QT, NA_KCH), F32), pltpu.VMEM((2, NA_QT, NA_KCH), F32),
                        pltpu.VMEM((2, NA_QT, NA_KT), BF16), pltpu.VMEM((2, NA_QT, NA_KT), BF16),
                        pltpu.SemaphoreType.DMA((2,))],
        compiler_params=_cparams(("arbitrary", "arbitrary")),
    )(*([proj] * (1 + 2 * npieces)), tbl, d_o, o_na, lse)


def _rpb_reduce(rpbacc, rows):
    nacc = 4 * 3 * 2

    def shift_body(a_ref, o_ref):
        acc = a_ref[0, 0:1, :]
        for cq in range(1, GRID_W):
            acc = acc + pltpu.roll(a_ref[0, cq:cq + 1, :], NA_KT - cq, 1)
        o_ref[0] = jnp.broadcast_to(acc, (8, NA_KT))

    vec = pl.pallas_call(
        shift_body, name="rpb_shift", grid=(nacc,),
        in_specs=[pl.BlockSpec((1, GRID_W, NA_KT), lambda a: (a, 0, 0))],
        out_specs=pl.BlockSpec((1, 8, NA_KT), lambda a: (a, 0, 0)),
        out_shape=jax.ShapeDtypeStruct((nacc, 8, NA_KT), F32),
        compiler_params=_cparams(("parallel",)),
    )(rpbacc.reshape(nacc, GRID_W, NA_KT))
    a = vec[:, 0].reshape(4, 3, 2, NA_KT).transpose(0, 2, 1, 3).reshape(NA_HEADS, 3, NA_KT)
    if rows // NA_QROWS < 3:
        a = a.at[:, 1].set(0.0)
    dd = np.arange(NA_KROWS)[:, None]
    dxo = np.arange(-(NA_KW - 1), NA_KW)[None, :]
    idx = ((dd * GRID_W + dxo) % NA_KT).reshape(-1)
    g = a[..., idx].reshape(NA_HEADS, 3 * NA_KROWS, 2 * NA_KW - 1)
    g = jnp.pad(g, ((0, 0), (0, 0), (0, 128 - (2 * NA_KW - 1))))
    nmat = np.zeros((16, 3 * NA_KROWS), np.float32)
    for cs, delta in enumerate((0, -(NA_KH // 2), -(NA_KROWS - NA_QROWS))):
        for d in range(NA_KROWS):
            jmi = d - NA_KROWS if (cs == 0 and d > NA_KH - 1) else d
            dy = jmi + delta + NA_KH - 1
            if 0 <= dy <= 2 * NA_KH - 2:
                nmat[dy, cs * NA_KROWS + d] = 1.0

    def body(n_ref, g_ref, o_ref):
        o_ref[0] = jnp.dot(n_ref[...], g_ref[0], precision=HI, preferred_element_type=F32)

    out = pl.pallas_call(
        body, name="rpb_reduce", grid=(NA_HEADS,),
        in_specs=[pl.BlockSpec((16, nmat.shape[1]), lambda h: (0, 0)),
                  pl.BlockSpec((1, nmat.shape[1], 128), lambda h: (h, 0, 0))],
        out_specs=pl.BlockSpec((1, 16, 128), lambda h: (h, 0, 0)),
        out_shape=jax.ShapeDtypeStruct((NA_HEADS, 16, 128), F32),
        compiler_params=_cparams(("parallel",)),
    )(jnp.asarray(nmat), g)
    return out[:, :2 * NA_KH - 1, :2 * NA_KW - 1]


def _halo_specs(tm, t, col, width=1024):
    nth = t // CONV_HALO
    per = tm // CONV_HALO
    return [pl.BlockSpec((tm, width), lambda i: (i, col)),
            pl.BlockSpec((CONV_HALO, width), lambda i: (jnp.maximum(i * per - 1, 0), col)),
            pl.BlockSpec((CONV_HALO, width), lambda i: (jnp.minimum((i + 1) * per, nth - 1), col))]


def _fill_ext(ext, cur_ref, prev_ref, next_ref, tm, nt):
    i = pl.program_id(0)
    hl = CONV_HALO
    ext[0:hl, :] = jnp.where(i == 0, 0.0, prev_ref[...].astype(F32))
    ext[hl:hl + tm, :] = cur_ref[...].astype(F32)
    ext[hl + tm:2 * hl + tm, :] = jnp.where(i == nt - 1, 0.0, next_ref[...].astype(F32))


CONV_HALO = 16
CONV_RC = 16
CONV_CB = 512


def _conv_chunks(tm):
    return [(slice(cb, cb + CONV_CB), slice(rb, rb + CONV_RC))
            for cb in range(0, 1024, CONV_CB) for rb in range(0, tm, CONV_RC)]


def _conv_fwd(proj, conv_w8, conv_b, tm):
    t = proj.shape[0]
    nt = t // tm

    def body(u_ref, up_ref, un_ref, w_ref, b_ref, pre_ref, act_ref, ext):
        _fill_ext(ext, u_ref, up_ref, un_ref, tm, nt)
        for cs, rs in _conv_chunks(tm):
            pre = b_ref[:, cs] + w_ref[0:1, cs] * ext[pl.ds(rs.start + CONV_HALO - 2, CONV_RC), cs]
            for j in range(1, CONV_W):
                pre = pre + w_ref[j:j + 1, cs] * ext[pl.ds(rs.start + CONV_HALO - 2 + j, CONV_RC), cs]
            pre_ref[rs, cs] = pre
            act_ref[rs, cs] = _silu(pre)

    full = pl.BlockSpec((tm, 1024), lambda i: (i, 0))
    o = jax.ShapeDtypeStruct((t, 1024), F32)
    return pl.pallas_call(
        body, name="conv_fwd", grid=(nt,),
        in_specs=_halo_specs(tm, t, 2) + [pl.BlockSpec((8, 1024), lambda i: (0, 0)), _row(1024)],
        out_specs=[full, full], out_shape=[o, o],
        scratch_shapes=[pltpu.VMEM((tm + 2 * CONV_HALO, 1024), F32)],
        compiler_params=_cparams(("parallel",)),
    )(proj, proj, proj, conv_w8, conv_b)


def _conv_bwd(dq, dk, pre, proj, conv_w8, tm):
    t = pre.shape[0]
    nt = t // tm

    def body(dq_ref, dqp_ref, dqn_ref, dk_ref, dkp_ref, dkn_ref, pre_ref, prep_ref, pren_ref,
             u_ref, up_ref, un_ref, w_ref, du_ref, gw_ref, gb_ref, extd, extu):
        i = pl.program_id(0)
        hl = CONV_HALO

        @pl.when(i == 0)
        def _():
            gw_ref[...] = jnp.zeros_like(gw_ref)
            gb_ref[...] = jnp.zeros_like(gb_ref)
        for rows, dqr, dkr, prr, edge in ((slice(0, hl), dqp_ref, dkp_ref, prep_ref, i == 0),
                                          (slice(hl, hl + tm), dq_ref, dk_ref, pre_ref, None),
                                          (slice(hl + tm, 2 * hl + tm), dqn_ref, dkn_ref, pren_ref, i == nt - 1)):
            ds = _dsilu(prr[...])
            dl = dqr[...] * ds[:, 0:ML_W]
            dr = dkr[...] * ds[:, ML_W:]
            if edge is not None:
                dl = jnp.where(edge, 0.0, dl)
                dr = jnp.where(edge, 0.0, dr)
            extd[rows, 0:ML_W] = dl
            extd[rows, ML_W:] = dr
        _fill_ext(extu, u_ref, up_ref, un_ref, tm, nt)
        gb_ref[...] += jnp.sum(extd[hl:hl + tm, :], axis=0, keepdims=True)
        gacc = None
        for cs, rs in _conv_chunks(tm):
            if rs.start == 0:
                gacc = [jnp.zeros((8, CONV_CB), F32) for _ in range(CONV_W)]
            du = w_ref[0:1, cs] * extd[pl.ds(rs.start + hl + 2, CONV_RC), cs]
            for j in range(1, CONV_W):
                du = du + w_ref[j:j + 1, cs] * extd[pl.ds(rs.start + hl + 2 - j, CONV_RC), cs]
            du_ref[rs, cs] = du.astype(BF16)
            dcur = extd[pl.ds(rs.start + hl, CONV_RC), cs]
            for j in range(CONV_W):
                prod = dcur * extu[pl.ds(rs.start + hl - 2 + j, CONV_RC), cs]
                gacc[j] = gacc[j] + functools.reduce(
                    jnp.add, [prod[k:k + 8] for k in range(0, CONV_RC, 8)])
            if rs.stop == tm:
                for j in range(CONV_W):
                    gw_ref[j:j + 1, cs] += jnp.sum(gacc[j], axis=0, keepdims=True)

    full = pl.BlockSpec((tm, 1024), lambda i: (i, 0))
    return pl.pallas_call(
        body, name="conv_bwd", grid=(nt,),
        in_specs=_halo_specs(tm, t, 0, ML_W) + _halo_specs(tm, t, 0, ML_W) + _halo_specs(tm, t, 0)
        + _halo_specs(tm, t, 2) + [pl.BlockSpec((8, 1024), lambda i: (0, 0))],
        out_specs=[full, pl.BlockSpec((8, 1024), lambda i: (0, 0)), _row(1024)],
        out_shape=[jax.ShapeDtypeStruct((t, 1024), BF16), jax.ShapeDtypeStruct((8, 1024), F32),
                   jax.ShapeDtypeStruct((1, 1024), F32)],
        scratch_shapes=[pltpu.VMEM((tm + 2 * CONV_HALO, 1024), F32), pltpu.VMEM((tm + 2 * CONV_HALO, 1024), F32)],
        compiler_params=_cparams(("arbitrary",)),
    )(dq, dq, dq, dk, dk, dk, pre, pre, pre, proj, proj, proj, conv_w8)


def _ml_consts(rev):
    iu = lax.broadcasted_iota(jnp.int32, (ML_CHUNK, ML_CHUNK), 0)
    js = lax.broadcasted_iota(jnp.int32, (ML_CHUNK, ML_CHUNK), 1)
    eye = iu == js
    le = iu <= js
    ge = iu >= js
    csum, csum_t, causal = (ge, le, le) if rev else (le, ge, ge)
    return eye, csum.astype(F32), csum_t.astype(F32), causal


def _col(row, eye):
    return jnp.sum(jnp.where(eye, row, 0.0), axis=1, keepdims=True)


def _rowof(col, eye):
    return jnp.sum(jnp.where(eye, col, 0.0), axis=0, keepdims=True)


def _ml_gates(gi, gf, m0, csum, rev):
    lf = jax.nn.log_sigmoid(gf)
    b_rows = jnp.dot(lf, csum, precision=HI, preferred_element_type=F32)
    bl = jnp.sum(lf, axis=1, keepdims=True)
    a_rows = bl - b_rows + gi
    mloc = jnp.max(a_rows, axis=1, keepdims=True)
    order = list(range(ML_NB))[::-1] if rev else list(range(ML_NB))
    mp, mn, decay = {}, {}, {}
    m = m0
    for n in order:
        mp[n] = m
        m = jnp.maximum(bl[n:n + 1] + m, mloc[n:n + 1])
        mn[n] = m
    for n in order:
        decay[n] = jnp.exp(bl[n:n + 1] + mp[n] - mn[n])
    return b_rows, a_rows, gi - b_rows, mp, mn, decay, order


def _ml_load(q_ref, k_ref, v_ref, n):
    sl = slice(n * ML_CHUNK, (n + 1) * ML_CHUNK)
    qb = q_ref[sl, :].astype(BF16)
    kb = (k_ref[sl, :] * (ML_HEAD_DIM ** -0.5)).astype(BF16)
    vn = v_ref[sl, :].astype(F32)
    return sl, qb, kb, vn


def _ml_state_scan(q_ref, k_ref, v_ref, a_rows, mn, decay, order, c0, n0, eye):
    ns = range(ML_NB)
    ld = [_ml_load(q_ref, k_ref, v_ref, n) for n in ns]
    acol = [_col(a_rows[n:n + 1], eye) for n in ns]
    wcol = [jnp.exp(acol[n] - mn[n]) for n in ns]
    u = [_tn((wcol[n] * ld[n][3]).astype(BF16), ld[n][2]) for n in ns]
    nu = [jnp.sum(wcol[n] * ld[n][2].astype(F32), axis=0, keepdims=True) for n in ns]
    cp, npv = {}, {}
    c, nv = c0, n0
    for n in order:
        cp[n], npv[n] = c, nv
        c = decay[n] * c + u[n]
        nv = decay[n] * nv + nu[n]
    return cp, npv, wcol, c, nv


def _ml_intra_all(q_ref, k_ref, v_ref, b_rows, imb_rows, mp, cp, npv, causal, eye):
    ns = range(ML_NB)
    ld = [_ml_load(q_ref, k_ref, v_ref, n) for n in ns]
    qk = [_nt(ld[n][1], ld[n][2]) for n in ns]
    cq = [_nt(ld[n][1], cp[n].astype(BF16)) for n in ns]
    b_col = [_col(b_rows[n:n + 1], eye) for n in ns]
    dlog = [jnp.where(causal, b_col[n] + imb_rows[n:n + 1], NEG) for n in ns]
    m_inter = [b_col[n] + mp[n] for n in ns]
    m_t = [jnp.maximum(m_inter[n], jnp.max(dlog[n], axis=1, keepdims=True)) for n in ns]
    pm = [jnp.exp(dlog[n] - m_t[n]) for n in ns]
    inter = [jnp.exp(m_inter[n] - m_t[n]) for n in ns]
    floor = [jnp.exp(-m_t[n]) for n in ns]
    s = [qk[n] * pm[n] for n in ns]
    qn = [jnp.sum(ld[n][1].astype(F32) * npv[n].astype(BF16).astype(F32), axis=1, keepdims=True) for n in ns]
    sv = [_nn(s[n].astype(BF16), ld[n][3].astype(BF16)) for n in ns]
    den = [jnp.sum(s[n], axis=1, keepdims=True) + inter[n] * qn[n] for n in ns]
    num = [sv[n] + inter[n] * cq[n] for n in ns]
    dn = [jnp.maximum(jnp.abs(den[n]), floor[n]) for n in ns]
    return ld, [dict(pm=pm[n], s=s[n], inter=inter[n], cq=cq[n], qn=qn[n], num=num[n], den=den[n],
                     floor=floor[n], dn=dn[n]) for n in ns]


def _ml_specs(t, rev):
    nblk = t // ML_TB
    blk = (lambda g: nblk - 1 - g) if rev else (lambda g: g)
    hps = ML_HPS
    tile = lambda c0: pl.BlockSpec((ML_TB, 128 * hps), lambda hg, g, c0=c0: (blk(g), c0 // hps + hg))
    gate = pl.BlockSpec((hps, ML_NB, ML_CHUNK), lambda hg, g: (hg, blk(g), 0))
    cchk = pl.BlockSpec((hps, 1, 128, 128), lambda hg, g: (hg, blk(g), 0, 0))
    nmchk = pl.BlockSpec((hps, 1, 8, 128), lambda hg, g: (hg, blk(g), 0, 0))
    return nblk, blk, tile, gate, cchk, nmchk


def _ml_head_views(refs, hh):
    cols = slice(hh * ML_HEAD_DIM, (hh + 1) * ML_HEAD_DIM)
    return [r.at[:, cols] if len(r.shape) == 2 else r.at[hh] for r in refs]


def _ml_fwd(qk_act, proj, gi, gf, rev, name):
    t = qk_act.shape[0]
    nblk, _, tile, gate, cchk, nmchk = _ml_specs(t, rev)

    def body(*refs):
        for hh in range(ML_HPS):
            one_head(*_ml_head_views(refs, hh))

    def one_head(q_ref, k_ref, v_ref, gi_ref, gf_ref, h_ref, cchk_ref, nmchk_ref, c_ref, nm_ref):
        @pl.when(pl.program_id(1) == 0)
        def _():
            c_ref[...] = jnp.zeros_like(c_ref)
            nm_ref[...] = jnp.zeros_like(nm_ref)
        cchk_ref[0] = c_ref[...]
        nmchk_ref[0] = nm_ref[...]
        eye, csum, _, causal = _ml_consts(rev)
        b_rows, a_rows, imb_rows, mp, mn, decay, order = _ml_gates(
            gi_ref[...], gf_ref[...], nm_ref[1:2, 0:1], csum, rev)
        cp, npv, _, c, nv = _ml_state_scan(q_ref, k_ref, v_ref, a_rows, mn, decay, order,
                                            c_ref[...], nm_ref[0:1, :], eye)
        c_ref[...] = c
        nm_ref[0:1, :] = nv
        nm_ref[1:2, :] = jnp.broadcast_to(mn[order[-1]], (1, 128))
        _, rs = _ml_intra_all(q_ref, k_ref, v_ref, b_rows, imb_rows, mp, cp, npv, causal, eye)
        for n in range(ML_NB):
            h_ref[n * ML_CHUNK:(n + 1) * ML_CHUNK, :] = rs[n]['num'] / rs[n]['dn']

    return pl.pallas_call(
        body, name=name, grid=(ML_HEADS // ML_HPS, nblk),
        in_specs=[tile(0), tile(4), tile(24), gate, gate],
        out_specs=[tile(0), cchk, nmchk],
        out_shape=[jax.ShapeDtypeStruct((t, ML_W), F32),
                   jax.ShapeDtypeStruct((ML_HEADS, nblk, 128, 128), F32),
                   jax.ShapeDtypeStruct((ML_HEADS, nblk, 8, 128), F32)],
        scratch_shapes=[pltpu.VMEM((ML_HPS, 128, 128), F32), pltpu.VMEM((ML_HPS, 8, 128), F32)],
        compiler_params=_cparams(("parallel", "arbitrary")),
    )(qk_act, qk_act, proj, gi, gf)


def _ml_bwd(qk_act, proj, gi, gf, dh, cchk_a, nmchk_a, prev, rev, name):
    t = qk_act.shape[0]
    nblk, _, tile, gate, cchk, nmchk = _ml_specs(t, not rev)

    def body(*refs):
        for hh in range(ML_HPS):
            one_head(*_ml_head_views(refs, hh))

    def one_head(q_ref, k_ref, v_ref, gi_ref, gf_ref, dh_ref, cchk_ref, nmchk_ref, *rest):
        prev_refs = rest[:len(prev)]
        dq_ref, dk_ref, dv_ref, dgi_ref, dgf_ref, dc_ref, dn_ref, db_scr, dbl_scr, di_scr = rest[len(prev):]

        def plus_prev(val, which, rows):
            return val + prev_refs[which][rows, :] if prev else val

        @pl.when(pl.program_id(1) == 0)
        def _():
            dc_ref[...] = jnp.zeros_like(dc_ref)
            dn_ref[...] = jnp.zeros_like(dn_ref)
        eye, csum, csum_t, causal = _ml_consts(rev)
        gfv = gf_ref[...]
        b_rows, a_rows, imb_rows, mp, mn, decay, order = _ml_gates(
            gi_ref[...], gfv, nmchk_ref[0, 1:2, 0:1], csum, rev)
        cp, npv, wcol, _, _ = _ml_state_scan(q_ref, k_ref, v_ref, a_rows, mn, decay, order,
                                             cchk_ref[0], nmchk_ref[0, 0:1, :], eye)
        ns = range(ML_NB)
        ld, rs = _ml_intra_all(q_ref, k_ref, v_ref, b_rows, imb_rows, mp, cp, npv, causal, eye)
        sls = [ld[n][0] for n in ns]
        qbs = [ld[n][1] for n in ns]
        kbs = [ld[n][2] for n in ns]
        vbs = [ld[n][3].astype(BF16) for n in ns]
        rdn = [1.0 / rs[n]['dn'] for n in ns]
        dnum = [dh_ref[sls[n], :] * rdn[n] for n in ns]
        hsum = [jnp.sum(dnum[n] * rs[n]['num'], axis=1, keepdims=True) for n in ns]
        dden = [jnp.where(jnp.abs(rs[n]['den']) > rs[n]['floor'],
                          -hsum[n] * rdn[n] * jnp.sign(rs[n]['den']), 0.0) for n in ns]
        dnb = [dnum[n].astype(BF16) for n in ns]
        dsf = [_nt(dnb[n], vbs[n]) + dden[n] for n in ns]
        dv0 = [_tn(rs[n]['s'].astype(BF16), dnb[n]) for n in ns]
        gb = [(dsf[n] * rs[n]['pm']).astype(BF16) for n in ns]
        cpb = [cp[n].astype(BF16) for n in ns]
        idd = [rs[n]['inter'] * dden[n] for n in ns]
        dqa = [_nn(gb[n], kbs[n]) for n in ns]
        dqc = [_nn(dnb[n], cpb[n]) for n in ns]
        dk0 = [_tn(gb[n], qbs[n]) for n in ns]
        xs = [_tn((rs[n]['inter'] * dnum[n]).astype(BF16), qbs[n]) for n in ns]
        for n in ns:
            dq_ref[sls[n], :] = plus_prev(dqa[n] + rs[n]['inter'] * dqc[n]
                                          + idd[n] * npv[n].astype(BF16).astype(F32), 0, sls[n])
        rr = [dsf[n] * rs[n]['s'] for n in ns]
        dinter = [jnp.sum(dnum[n] * rs[n]['cq'], axis=1, keepdims=True) + dden[n] * rs[n]['qn'] for n in ns]
        dbcol = [jnp.sum(rr[n], axis=1, keepdims=True) + dinter[n] * rs[n]['inter'] for n in ns]
        dimb = [jnp.sum(rr[n], axis=0, keepdims=True) for n in ns]
        xns = [jnp.sum(idd[n] * qbs[n].astype(F32), axis=0, keepdims=True) for n in ns]
        dcn, dnn = {}, {}
        dc, dn = dc_ref[...], dn_ref[0:1, :]
        for n in order[::-1]:
            dcn[n], dnn[n] = dc, dn
            dc = decay[n] * dc + xs[n]
            dn = decay[n] * dn + xns[n]
        dc_ref[...] = dc
        dn_ref[0:1, :] = dn
        kscale = ML_HEAD_DIM ** -0.5
        dcb = [dcn[n].astype(BF16) for n in ns]
        z = [_nn(vbs[n], dcb[n]) for n in ns]
        kd = [_nt(kbs[n], dcb[n]) for n in ns]
        ddecay = [jnp.sum(jnp.sum(dcn[n] * cp[n], axis=1, keepdims=True), axis=0, keepdims=True)
                  + jnp.sum(dnn[n] * npv[n], axis=1, keepdims=True) for n in ns]
        zd = [z[n] + dnn[n] for n in ns]
        dw = [jnp.sum(zd[n] * kbs[n].astype(F32), axis=1, keepdims=True) for n in ns]
        for n in ns:
            dv_ref[sls[n], :] = plus_prev(dv0[n] + wcol[n] * kd[n], 2, sls[n]).astype(dv_ref.dtype)
            dk_ref[sls[n], :] = plus_prev((dk0[n] + wcol[n] * zd[n]) * kscale, 1, sls[n])
        da = [dw[n] * wcol[n] for n in ns]
        dbl = [jnp.sum(da[n], axis=0, keepdims=True) + ddecay[n] * decay[n] for n in ns]
        da_row = [_rowof(da[n], eye) for n in ns]
        db_row = [_rowof(dbcol[n] - da[n], eye) for n in ns]
        for n in ns:
            db_scr[n:n + 1, :] = db_row[n] - dimb[n]
            di_scr[n:n + 1, :] = dimb[n] + da_row[n]
            dbl_scr[n:n + 1, :] = jnp.broadcast_to(dbl[n], (1, ML_CHUNK))
        dlf = jnp.dot(db_scr[...], csum_t, precision=HI, preferred_element_type=F32) + dbl_scr[...]
        dgf_ref[...] = dlf * jax.nn.sigmoid(-gfv)
        dgi_ref[...] = di_scr[...]

    nc = t // ML_CHUNK
    o512 = jax.ShapeDtypeStruct((t, ML_W), F32)
    og = jax.ShapeDtypeStruct((ML_HEADS, nc, ML_CHUNK), F32)
    return pl.pallas_call(
        body, name=name, grid=(ML_HEADS // ML_HPS, nblk),
        in_specs=[tile(0), tile(4), tile(24), gate, gate, tile(0), cchk, nmchk] + [tile(0)] * len(prev),
        out_specs=[tile(0), tile(0), tile(0), gate, gate],
        out_shape=[o512, o512, jax.ShapeDtypeStruct((t, ML_W), BF16 if prev else F32), og, og],
        scratch_shapes=[pltpu.VMEM((ML_HPS, 128, 128), F32), pltpu.VMEM((ML_HPS, 8, 128), F32)]
        + [pltpu.VMEM((ML_HPS, ML_NB, ML_CHUNK), F32)] * 3,
        compiler_params=_cparams(("parallel", "arbitrary")),
    )(qk_act, qk_act, proj, gi, gf, dh, cchk_a, nmchk_a, *prev)


def _gate_rows(gates16, t):
    g = gates16.reshape(t // ML_CHUNK, ML_CHUNK, 4, ML_HEADS).transpose(2, 3, 0, 1)
    return g[0], g[1], g[2], g[3]


def _gate_cols(dgi_f, dgf_f, dgi_b, dgf_b, t):
    g = jnp.stack([dgi_f, dgf_f, dgi_b, dgf_b]).transpose(2, 3, 0, 1).reshape(t, 4 * ML_HEADS)
    return jnp.pad(g, ((0, 0), (0, 128 - 4 * ML_HEADS)))


def _local_step(x, target, shift, scale, gate, norm_w, w_in_t, b_in_p, conv_w8, conv_b, rpb,
                ml_norm_w, w_out_b, final_norm_w):
    t = x.shape[0]
    rows = t // GRID_W
    tm = 512
    proj, gates = _in_proj(x, norm_w, scale, shift, w_in_t, b_in_p)
    tbl = _na_bias_table(rpb, rows)
    o_na, lse_na = _na_fwd(proj, tbl)
    pre, qk_act = _conv_fwd(proj, conv_w8, conv_b, tm)
    gi_f, gf_f, gi_b, gf_b = _gate_rows(gates[:, :4 * ML_HEADS], t)
    h_f, cchk_f, nmchk_f = _ml_fwd(qk_act, proj, gi_f, gf_f, False, "ml_fwd_f")
    h_b, cchk_b, nmchk_b = _ml_fwd(qk_act, proj, gi_b, gf_b, True, "ml_fwd_b")
    (loss, dres, d_ona, d_naz, dhs, d_o, d_z, dgate, g_fnw, g_mlnw, g_w_out) = _tail(
        o_na, proj, h_f, h_b, x, target, gate, ml_norm_w, final_norm_w, w_out_b)
    dq_na, dk_na, dv_na, rpbacc = _na_bwd(proj, tbl, d_ona, o_na, lse_na)
    g_rpb = _rpb_reduce(rpbacc, rows)
    dq_f, dk_f, dv_f, dgi_f, dgf_f = _ml_bwd(qk_act, proj, gi_f, gf_f, dhs, cchk_f, nmchk_f, (),
                                             False, "ml_bwd_f")
    dq_ml, dk_ml, dv_ml, dgi_b, dgf_b = _ml_bwd(qk_act, proj, gi_b, gf_b, dhs, cchk_b, nmchk_b, (dq_f, dk_f, dv_f),
                                                True, "ml_bwd_b")
    du, g_conv_w, g_conv_b = _conv_bwd(dq_ml, dk_ml, pre, proj, conv_w8, tm)
    dgates = _gate_cols(dgi_f, dgf_f, dgi_b, dgf_b, t)
    grad_x, g_w_in, g_b_in, dscale, dshift, g_nw = _in_bwd(
        [dq_na, dk_na, dv_na, d_naz, du, dv_ml, d_o, d_z, dgates], x, dres, w_in_t, norm_w, scale, shift)
    dmod = jnp.concatenate([dshift, dscale, dgate], axis=1)
    return (loss, grad_x, dmod, g_nw, g_w_in, g_b_in, g_conv_w, g_conv_b, g_rpb, g_mlnw, g_w_out, g_fnw)


MESH = pl.DeviceIdType.MESH
N_DEV = 8
ANY = pl.BlockSpec(memory_space=pl.ANY)
WHOLE_VMEM = pl.BlockSpec(memory_space=pltpu.VMEM)


def _allgather8(blocks, name):
    na = len(blocks)

    def body(*refs):
        x_refs = refs[:na]
        out_refs = refs[na:2 * na]
        send_sems, recv_sems, local_sems = refs[2 * na:]
        x, y, c = lax.axis_index("x"), lax.axis_index("y"), lax.axis_index("c")
        me, sibling = (x, y, c), (x, y, 1 - c)
        chips = [(1 - x, y), (x, 1 - y), (1 - x, 1 - y)]

        def rows(a, px, py, pc):
            return out_refs[a].at[4 * px + 2 * py + pc]

        def copy(a, k, block, to, src=None):
            return pltpu.make_async_remote_copy(
                src_ref=rows(a, *block) if src is None else src, dst_ref=rows(a, *block),
                send_sem=send_sems.at[a, k], recv_sem=recv_sems.at[a, k],
                device_id=to, device_id_type=MESH)

        mine, first, passed = [], [], []
        for a in range(na):
            cp = pltpu.make_async_copy(x_refs[a], rows(a, *me), local_sems.at[a])
            cp.start()
            mine.append(cp)
            first.append(copy(a, 0, me, sibling, src=x_refs[a]))
            first += [copy(a, 1 + j, me, (*chip, c), src=x_refs[a]) for j, chip in enumerate(chips)]
        for cp in first:
            cp.start()
        for a in range(na):
            for j, chip in enumerate(chips):
                copy(a, 1 + j, (*chip, c), me).wait_recv()
                fwd = copy(a, 4 + j, (*chip, c), sibling)
                fwd.start()
                passed.append(fwd)
        for a in range(na):
            copy(a, 0, sibling, me).wait_recv()
            for j, chip in enumerate(chips):
                copy(a, 4 + j, (*chip, 1 - c), me).wait_recv()
        for cp in first + passed:
            cp.wait_send()
        for cp in mine:
            cp.wait()

    return pl.pallas_call(
        body, name=name,
        out_shape=[jax.ShapeDtypeStruct((N_DEV,) + b.shape, b.dtype) for b in blocks],
        in_specs=[WHOLE_VMEM] * na, out_specs=[WHOLE_VMEM] * na,
        scratch_shapes=[pltpu.SemaphoreType.DMA((na, 7)), pltpu.SemaphoreType.DMA((na, 7)),
                        pltpu.SemaphoreType.DMA((na,))],
        compiler_params=pltpu.CompilerParams(vmem_limit_bytes=VMEM_LIMIT),
    )(*blocks)


def _pair_exchange(arrs, name):
    na = len(arrs)

    def body(*refs):
        in_refs = refs[:na]
        out_refs = refs[na:2 * na]
        send_sems, recv_sems = refs[2 * na:]
        sibling = (lax.axis_index("x"), lax.axis_index("y"), 1 - lax.axis_index("c"))
        copies = [pltpu.make_async_remote_copy(
            src_ref=in_refs[a], dst_ref=out_refs[a], send_sem=send_sems.at[a], recv_sem=recv_sems.at[a],
            device_id=sibling, device_id_type=MESH) for a in range(na)]
        for cp in copies:
            cp.start()
        for cp in copies:
            cp.wait()

    return pl.pallas_call(
        body, name=name,
        out_shape=[jax.ShapeDtypeStruct(a.shape, a.dtype) for a in arrs],
        in_specs=[ANY] * na, out_specs=[ANY] * na,
        scratch_shapes=[pltpu.SemaphoreType.DMA((na,)), pltpu.SemaphoreType.DMA((na,))],
    )(*arrs)


def _chip_exchange(arrs, name):
    na = len(arrs)

    def body(*refs):
        in_refs = refs[:na]
        out_refs = refs[na:2 * na]
        send_sems, recv_sems, local_sems = refs[2 * na:]
        x, y, c = lax.axis_index("x"), lax.axis_index("y"), lax.axis_index("c")
        my_chip = 2 * x + y
        chips = [(1 - x, y), (x, 1 - y), (1 - x, 1 - y)]
        local, remote = [], []
        for a in range(na):
            cp = pltpu.make_async_copy(in_refs[a].at[my_chip], out_refs[a].at[my_chip], local_sems.at[a])
            cp.start()
            local.append(cp)
            for j, (px, py) in enumerate(chips):
                cp = pltpu.make_async_remote_copy(
                    src_ref=in_refs[a].at[2 * px + py], dst_ref=out_refs[a].at[my_chip],
                    send_sem=send_sems.at[a, j], recv_sem=recv_sems.at[a, j],
                    device_id=(px, py, c), device_id_type=MESH)
                cp.start()
                remote.append(cp)
        for cp in remote:
            cp.wait()
        for cp in local:
            cp.wait()

    return pl.pallas_call(
        body, name=name,
        out_shape=[jax.ShapeDtypeStruct(a.shape, a.dtype) for a in arrs],
        in_specs=[ANY] * na, out_specs=[ANY] * na,
        scratch_shapes=[pltpu.SemaphoreType.DMA((na, 3)), pltpu.SemaphoreType.DMA((na, 3)),
                        pltpu.SemaphoreType.DMA((na,))],
    )(*arrs)


def _rows_tile(r):
    for cand in (512, 256, 128, 64, 32, 16, 8):
        if r % cand == 0:
            return cand
    return r


def _add2(a, b, name, out_dtype):
    s, r, n = a.shape
    tr = _rows_tile(r)

    def body(a_ref, b_ref, o_ref):
        o_ref[...] = (a_ref[...] + b_ref[...]).astype(out_dtype)

    spec = pl.BlockSpec((1, tr, n), lambda i, j: (i, j, 0))
    return pl.pallas_call(
        body, name=name, grid=(s, r // tr), in_specs=[spec, spec], out_specs=spec,
        out_shape=jax.ShapeDtypeStruct(a.shape, out_dtype),
        compiler_params=_cparams(("parallel", "parallel")),
    )(a, b)


def _sum_slabs(a, name):
    s, r, n = a.shape
    tr = _rows_tile(r)

    def body(a_ref, o_ref):
        acc = a_ref[0].astype(F32)
        for k in range(1, s):
            acc = acc + a_ref[k].astype(F32)
        o_ref[...] = acc

    return pl.pallas_call(
        body, name=name, grid=(r // tr,),
        in_specs=[pl.BlockSpec((s, tr, n), lambda i: (0, i, 0))],
        out_specs=pl.BlockSpec((tr, n), lambda i: (i, 0)),
        out_shape=jax.ShapeDtypeStruct((r, n), F32),
        compiler_params=_cparams(("parallel",)),
    )(a)


def _adamw(w, g, m, v, name):
    r, n = w.shape
    if r % 8 == 0:
        blk, grid, imap = (_rows_tile(r), n), (r // _rows_tile(r),), (lambda i: (i, 0))
    else:
        blk, grid, imap = (r, 128), (n // 128,), (lambda i: (0, i))
    c1 = 1.0 / (1.0 - ADAM_B1 ** ADAM_STEP)
    c2 = 1.0 / (1.0 - ADAM_B2 ** ADAM_STEP)

    def body(w_ref, g_ref, m_ref, v_ref, d_ref, nm_ref, nv_ref):
        gv = g_ref[...]
        nm = ADAM_B1 * m_ref[...] + (1.0 - ADAM_B1) * gv
        nv = ADAM_B2 * v_ref[...] + (1.0 - ADAM_B2) * (gv * gv)
        nm_ref[...] = nm
        nv_ref[...] = nv
        d_ref[...] = -ADAM_LR * ((nm * c1) / (jnp.sqrt(nv * c2) + ADAM_EPS) + ADAM_WD * w_ref[...])

    spec = pl.BlockSpec(blk, imap)
    o = jax.ShapeDtypeStruct((r, n), F32)
    return pl.pallas_call(
        body, name=name, grid=grid, in_specs=[spec] * 4, out_specs=[spec] * 3, out_shape=[o, o, o],
        compiler_params=_cparams(("parallel",)),
    )(w, g, m, v)


def _mod_fwd(c_all, w_ada_s, b_ada_s):
    def body(c_ref, w_ref, b_ref, o_ref):
        o_ref[...] = jnp.dot(_silu(c_ref[...]), w_ref[...], precision=HI, preferred_element_type=F32) + b_ref[...]

    return pl.pallas_call(
        body, name="mod_fwd", out_shape=jax.ShapeDtypeStruct((c_all.shape[0], w_ada_s.shape[1]), F32),
        in_specs=[WHOLE_VMEM] * 3, out_specs=WHOLE_VMEM,
        compiler_params=pltpu.CompilerParams(vmem_limit_bytes=VMEM_LIMIT),
    )(c_all, w_ada_s, b_ada_s)


def _wada_grad(c_all, dmod_s):
    def body(c_ref, d_ref, o_ref):
        o_ref[...] = lax.dot_general(_silu(c_ref[...]), d_ref[...], (((0,), (0,)), ((), ())),
                                     precision=HI, preferred_element_type=F32)

    return pl.pallas_call(
        body, name="w_ada_grad", out_shape=jax.ShapeDtypeStruct((c_all.shape[1], dmod_s.shape[1]), F32),
        in_specs=[WHOLE_VMEM] * 2, out_specs=WHOLE_VMEM,
        compiler_params=pltpu.CompilerParams(vmem_limit_bytes=VMEM_LIMIT),
    )(c_all, dmod_s)


SMALL_ROWS = 24


def _pad_rows(v, nrows):
    v = v.reshape(-1)
    return jnp.pad(v, (0, nrows * 1024 - v.shape[0])).reshape(nrows, 1024)


def _pack_small(b_ada, norm_w, b_in, conv_w_full, conv_b, rpb, ml_norm_w, final_norm_w, last):
    parts = [_pad_rows(b_ada, 3), _pad_rows(norm_w, 1), _pad_rows(b_in, 5), _pad_rows(conv_w_full, 5),
             _pad_rows(conv_b, 1), _pad_rows(rpb, 4), _pad_rows(ml_norm_w, 1), _pad_rows(final_norm_w, 1),
             _pad_rows(last, 3)]
    return jnp.concatenate(parts, axis=0)


def _unpack_small(p):
    return dict(b_ada=p[0:3].reshape(1, 3072), norm_w=p[3:4], b_in=p[4:9].reshape(-1)[:IN_W].reshape(1, IN_W),
                conv_w=p[9:14], conv_b=p[14:15],
                rpb=p[15:19].reshape(-1)[:NA_HEADS * 15 * 31].reshape(1, NA_HEADS, 15, 31),
                ml_norm_w=p[19:20, :ML_W], final_norm_w=p[20], last=p[21])


def kernel(x, c, w_ada, b_ada, norm_w, w_in, b_in, conv_w, conv_b, rpb, ml_norm_w, w_out, final_norm_w, loss_target, m_w_ada, m_b_ada, m_norm_w, m_w_in, m_b_in, m_conv_w, m_conv_b, m_rpb, m_ml_norm_w, m_w_out, m_final_norm_w, v_w_ada, v_b_ada, v_norm_w, v_w_in, v_b_in, v_conv_w, v_conv_b, v_rpb, v_ml_norm_w, v_w_out, v_final_norm_w):
    xi, yi, ci = lax.axis_index("x"), lax.axis_index("y"), lax.axis_index("c")
    chip = 2 * xi + yi
    dev = 2 * chip + ci
    t = x.shape[1]
    ada_n = w_ada.shape[2]
    in_n = w_in.shape[2]
    out_r = w_out.shape[1]

    c_blk = jnp.pad(c, ((0, 7), (0, 0)))
    w_in_t, m_w_in_t, v_w_in_t = w_in[0].T, m_w_in[0].T, v_w_in[0].T
    in_h = in_n // 2
    w_in_half = lax.dynamic_slice_in_dim(w_in_t, ci * in_h, in_h, axis=0).astype(BF16)
    w_out_half = lax.dynamic_slice_in_dim(w_out[0], ci * (out_r // 2), out_r // 2, axis=0).astype(BF16)
    conv_blk = jnp.pad(conv_w[0], ((0, 3), (0, 0)))
    c_g, conv_g, w_in_g, w_out_g = _allgather8([c_blk, conv_blk, w_in_half, w_out_half], "gather_c_weights")
    c_all = c_g[:, 0]
    w_out_g = w_out_g.reshape(D_MODEL, D_MODEL)
    b_ada_s = lax.dynamic_slice_in_dim(b_ada, chip * ada_n, ada_n, axis=1)
    mod_s = _mod_fwd(c_all, w_ada[0], b_ada_s)
    (mod_g,) = _allgather8([mod_s], "gather_mod")
    mod_mine = lax.dynamic_index_in_dim(mod_g, dev, axis=1, keepdims=False)
    mod = mod_mine[0::2].reshape(1, 3 * D_MODEL)
    shift, scale, gate = mod[:, :D_MODEL], mod[:, D_MODEL:2 * D_MODEL], mod[:, 2 * D_MODEL:]

    w_in_tp = jnp.pad(w_in_g.reshape(IN_W, D_MODEL), ((0, IN_PAD - IN_W), (0, 0)))
    b_in_p = jnp.pad(b_in, ((0, 0), (0, IN_PAD - IN_W)))
    conv_w8 = conv_g.reshape(4, 2, 8, conv_w.shape[2])[:, 0].transpose(1, 0, 2).reshape(8, D_MODEL)

    (loss, grad_x, dmod, g_nw, g_w_in, g_b_in, g_conv_w, g_conv_b, g_rpb, g_mlnw, g_w_out, g_fnw) = _local_step(
        x[0], loss_target[0], shift, scale, gate, norm_w, w_in_tp, b_in_p, conv_w8, conv_b, rpb[0],
        ml_norm_w, w_out_g, final_norm_w.reshape(1, D_MODEL))

    g_in_t = g_w_in

    def halves(a, per_chip, h):
        return jnp.stack([lax.dynamic_slice_in_dim(a, k * per_chip + h * (per_chip // 2), per_chip // 2, axis=0)
                          for k in range(4)])

    ri, ro = _pair_exchange([halves(g_in_t, in_n, 1 - ci), halves(g_w_out, out_r, 1 - ci)], "rs_pair")
    pi = _add2(halves(g_in_t, in_n, ci), ri, "rs_pair_add_in", BF16)
    po = _add2(halves(g_w_out, out_r, ci), ro, "rs_pair_add_out", BF16)
    qi, qo = _chip_exchange([pi, po], "rs_chips")
    si = _sum_slabs(qi, "rs_sum_in")
    so = _sum_slabs(qo, "rs_sum_out")
    ti, to = _pair_exchange([si, so], "rs_share")
    g_w_in_s = jnp.where(ci == 0, jnp.concatenate([si, ti], axis=0), jnp.concatenate([ti, si], axis=0))
    g_w_out_s = jnp.where(ci == 0, jnp.concatenate([so, to], axis=0), jnp.concatenate([to, so], axis=0))

    small = _pack_small(dmod, g_nw, g_b_in[:, :IN_W], g_conv_w[:CONV_W], g_conv_b, g_rpb, g_mlnw, g_fnw,
                        jnp.pad(loss, ((0, 0), (0, 1024 - 128))))
    (small_g,) = _allgather8([small], "gather_small")
    small_sum = _sum_slabs(small_g, "small_sum")
    gs = _unpack_small(small_sum)
    dmod_all = small_g[:, 0:3].reshape(N_DEV, 3 * D_MODEL)
    g_w_ada_s = _wada_grad(c_all, lax.dynamic_slice_in_dim(dmod_all, chip * ada_n, ada_n, axis=1))
    g_conv_w_s = lax.dynamic_slice_in_dim(gs['conv_w'], chip * conv_w.shape[2], conv_w.shape[2], axis=1)
    loss_total = gs['last'][0]

    zeros3 = jnp.zeros((3, 1024), F32)
    zc = jnp.zeros((CONV_W, D_MODEL), F32)
    pw = _pack_small(b_ada, norm_w, b_in, zc, conv_b, rpb, ml_norm_w, final_norm_w, zeros3)
    pm = _pack_small(m_b_ada, m_norm_w, m_b_in, zc, m_conv_b, m_rpb, m_ml_norm_w, m_final_norm_w, zeros3)
    pv = _pack_small(v_b_ada, v_norm_w, v_b_in, zc, v_conv_b, v_rpb, v_ml_norm_w, v_final_norm_w, zeros3)
    ds_, nms, nvs = [_unpack_small(a) for a in _adamw(pw, small_sum, pm, pv, "adamw_small")]
    d_ada, nm_ada, nv_ada = _adamw(w_ada[0], g_w_ada_s, m_w_ada[0], v_w_ada[0], "adamw_w_ada")
    d_in, nm_in, nv_in = _adamw(w_in_t, g_w_in_s, m_w_in_t, v_w_in_t, "adamw_w_in")
    d_out, nm_out, nv_out = _adamw(w_out[0], g_w_out_s, m_w_out[0], v_w_out[0], "adamw_w_out")
    d_cw, nm_cw, nv_cw = _adamw(conv_w[0], g_conv_w_s, m_conv_w[0], v_conv_w[0], "adamw_conv_w")

    def group(big_ada, big_in, big_out, cw, sm):
        return (big_ada[None], sm['b_ada'], sm['norm_w'], big_in.T[None], sm['b_in'], cw[None], sm['conv_b'],
                sm['rpb'], sm['ml_norm_w'], big_out[None], sm['final_norm_w'])

    return ((loss_total, grad_x[None])
            + group(g_w_ada_s, g_w_in_s, g_w_out_s, g_conv_w_s, gs)
            + group(d_ada, d_in, d_out, d_cw, ds_)
            + group(nm_ada, nm_in, nm_out, nm_cw, nms)
            + group(nv_ada, nv_in, nv_out, nv_cw, nvs))
```

```python
import functools

import numpy as np
import jax
import jax.numpy as jnp
from jax import lax
from jax.experimental import pallas as pl
from jax.experimental.pallas import tpu as pltpu

F32 = jnp.float32
BF16 = jnp.bfloat16
HI = lax.Precision.HIGHEST

D_MODEL = 1024
GRID_W = 64
NA_W = 512
NA_HEAD_DIM = 64
NA_HEADS = 8
NA_KH = 8
NA_KW = 16
ML_W = 512
ML_HEADS = 4
ML_HEAD_DIM = 128
ML_CHUNK = 128
CONV_W = 5
EPS = 1e-6
IN_W = 4 * NA_W + 5 * ML_W + 4 * ML_HEADS
IN_MAIN = 4 * NA_W + 5 * ML_W
IN_PAD = IN_MAIN + 128
NEG = -1e30

ADAM_LR = 0.001
ADAM_B1 = 0.9
ADAM_B2 = 0.999
ADAM_EPS = 1e-08
ADAM_WD = 0.01
ADAM_STEP = 10

NA_QROWS = 8
NA_KROWS = 16
NA_QT = NA_QROWS * GRID_W
NA_KT = NA_KROWS * GRID_W
NA_KCH = 256
NA_RC = 32
ML_NB = 16
ML_TB = ML_NB * ML_CHUNK
ML_HPS = 1

VMEM_LIMIT = 56 * 1024 * 1024
IN_BWD_VMEM_LIMIT = 60 * 1024 * 1024


def _cparams(sem, vmem=VMEM_LIMIT):
    return pltpu.CompilerParams(dimension_semantics=sem, vmem_limit_bytes=vmem)


def _silu(x):
    return x * jax.nn.sigmoid(x)


def _dsilu(x):
    s = jax.nn.sigmoid(x)
    return s * (1.0 + x * (1.0 - s))


def _dot(a, b, dims):
    return lax.dot_general(a, b, (dims, ((), ())), preferred_element_type=F32)


def _nn(a, b):
    return _dot(a, b, ((1,), (0,)))


def _nt(a, b):
    return _dot(a, b, ((1,), (1,)))


def _tn(a, b):
    return _dot(a, b, ((0,), (0,)))


def _row(n):
    return pl.BlockSpec((1, n), lambda i: (0, 0))


def _modulated_norm(xv, nw, sc, sh):
    r = lax.rsqrt(jnp.mean(xv * xv, axis=-1, keepdims=True) + EPS)
    xn = xv * r
    return xn * nw * (1.0 + sc) + sh, xn, r


IN_TN = 768


def _in_proj(x, norm_w, scale, shift, w_in_t, b_in_p):
    t, d = x.shape
    tm = 2048
    gcol = IN_MAIN // 128

    def body(x_ref, nw_ref, sc_ref, sh_ref, w_ref, b_ref, wg_ref, bg_ref, proj_ref, g_ref, h_scr):
        @pl.when(pl.program_id(1) == 0)
        def _():
            h, _, _ = _modulated_norm(x_ref[...], nw_ref[...], sc_ref[...], sh_ref[...])
            h_scr[...] = h.astype(BF16)
            g_ref[...] = _nt(h_scr[...], wg_ref[...]) + bg_ref[...]
        proj_ref[...] = (_nt(h_scr[...], w_ref[...]) + b_ref[...]).astype(BF16)

    row = lambda n: pl.BlockSpec((1, n), lambda i, j: (0, 0))
    return pl.pallas_call(
        body, name="in_proj", grid=(t // tm, IN_MAIN // IN_TN),
        in_specs=[pl.BlockSpec((tm, d), lambda i, j: (i, 0)), row(d), row(d), row(d),
                  pl.BlockSpec((IN_TN, d), lambda i, j: (j, 0)), pl.BlockSpec((1, IN_TN), lambda i, j: (0, j)),
                  pl.BlockSpec((128, d), lambda i, j: (gcol, 0)), pl.BlockSpec((1, 128), lambda i, j: (0, gcol))],
        out_specs=[pl.BlockSpec((tm, IN_TN), lambda i, j: (i, j)), pl.BlockSpec((tm, 128), lambda i, j: (i, 0))],
        out_shape=[jax.ShapeDtypeStruct((t, IN_MAIN), BF16), jax.ShapeDtypeStruct((t, 128), F32)],
        scratch_shapes=[pltpu.VMEM((tm, d), BF16)],
        compiler_params=_cparams(("parallel", "arbitrary")),
    )(x, norm_w, scale, shift, w_in_t, b_in_p, w_in_t, b_in_p)


def _ml_norm_parts(hs, o, z, nw):
    outs = []
    for hh in range(ML_HEADS):
        sl = slice(hh * ML_HEAD_DIM, (hh + 1) * ML_HEAD_DIM)
        hm = hs[:, sl] * jax.nn.sigmoid(o[:, sl])
        mu = jnp.mean(hm, axis=-1, keepdims=True)
        cen = hm - mu
        var = jnp.mean(cen * cen, axis=-1, keepdims=True)
        rs = lax.rsqrt(var + EPS)
        outs.append((sl, cen * rs, rs))
    return outs


def _tail(o_na, proj, h_f, h_b, x, target, gate, ml_norm_w, fnw, w_out_b):
    t, d = x.shape
    tm = 256

    def body(ona_ref, naz_ref, hf_ref, hb_ref, o_ref, z_ref, x_ref, tg_ref, g_ref, nw_ref, fw_ref, w_ref,
             loss_ref, dres_ref, dona_ref, dnaz_ref, dhs_ref, do_ref, dz_ref, dgate_ref, gfw_ref, gnw_ref,
             gwo_ref, mix_scr):
        @pl.when(pl.program_id(0) == 0)
        def _():
            for r in (loss_ref, dgate_ref, gfw_ref, gnw_ref, gwo_ref):
                r[...] = jnp.zeros_like(r)
        naz = naz_ref[...].astype(F32)
        ona = ona_ref[...]
        sna = _silu(naz)
        mix_scr[:, 0:NA_W] = (ona * sna).astype(BF16)
        hs = hf_ref[...] + hb_ref[...]
        z = z_ref[...].astype(F32)
        ov = o_ref[...].astype(F32)
        parts = _ml_norm_parts(hs, ov, z, nw_ref[...])
        szs = []
        for sl, xn, _ in parts:
            sz = _silu(z[:, sl])
            szs.append(sz)
            mix_scr[:, NA_W + sl.start:NA_W + sl.stop] = (xn * nw_ref[:, sl] * sz).astype(BF16)
        mixb = mix_scr[...]
        wv = w_ref[...]
        yv = _nn(mixb, wv)
        gate_v = g_ref[...]
        hres = x_ref[...] + gate_v * yv
        r = lax.rsqrt(jnp.mean(hres * hres, axis=-1, keepdims=True) + EPS)
        xnf = hres * r
        err = xnf * fw_ref[...] - tg_ref[...]
        loss_ref[...] += 0.5 * jnp.sum(jnp.sum(err * err, axis=-1, keepdims=True) * (1.0 / d), axis=0, keepdims=True)
        dout = err * (1.0 / d)
        gfw_ref[...] += jnp.sum(dout * xnf, axis=0, keepdims=True)
        dxn = dout * fw_ref[...]
        dres = r * (dxn - xnf * jnp.mean(dxn * xnf, axis=-1, keepdims=True))
        dres_ref[...] = dres
        dgate_ref[...] += jnp.sum(dres * yv, axis=0, keepdims=True)
        dyb = (dres * gate_v).astype(BF16)
        gwo_ref[...] += _tn(mixb, dyb)
        dmix = _nt(dyb, wv)
        dna = dmix[:, 0:NA_W]
        dona_ref[...] = dna * sna
        dnaz_ref[...] = (dna * ona * _dsilu(naz)).astype(BF16)
        for (sl, xn, rs), sz in zip(parts, szs):
            dyv = dmix[:, NA_W + sl.start:NA_W + sl.stop]
            zz = z[:, sl]
            w = nw_ref[:, sl]
            dz_ref[:, sl] = (dyv * xn * w * _dsilu(zz)).astype(BF16)
            gnw_ref[:, sl] += jnp.sum(dyv * xn * sz, axis=0, keepdims=True)
            dxm = dyv * w * sz
            dhm = rs * (dxm - jnp.mean(dxm, axis=-1, keepdims=True)
                        - xn * jnp.mean(dxm * xn, axis=-1, keepdims=True))
            so = jax.nn.sigmoid(ov[:, sl])
            dhs_ref[:, sl] = dhm * so
            do_ref[:, sl] = (dhm * hs[:, sl] * so * (1.0 - so)).astype(BF16)

    blk = lambda c: pl.BlockSpec((tm, 512), lambda i, c=c: (i, c))
    full = pl.BlockSpec((tm, d), lambda i: (i, 0))
    o512 = jax.ShapeDtypeStruct((t, 512), F32)
    b512 = jax.ShapeDtypeStruct((t, 512), BF16)
    whole = pl.BlockSpec((d, d), lambda i: (0, 0))
    return pl.pallas_call(
        body, name="tail", grid=(t // tm,),
        in_specs=[blk(0), blk(3), blk(0), blk(0), blk(7), blk(8), full, full, _row(d), _row(ML_W), _row(d), whole],
        out_specs=[pl.BlockSpec((1, 128), lambda i: (0, 0)), full] + [blk(0)] * 5
        + [_row(d), _row(d), _row(ML_W), whole],
        out_shape=[jax.ShapeDtypeStruct((1, 128), F32), jax.ShapeDtypeStruct((t, d), F32),
                   o512, b512, o512, b512, b512]
        + [jax.ShapeDtypeStruct((1, d), F32), jax.ShapeDtypeStruct((1, d), F32),
           jax.ShapeDtypeStruct((1, ML_W), F32), jax.ShapeDtypeStruct((d, d), F32)],
        scratch_shapes=[pltpu.VMEM((tm, d), BF16)],
        compiler_params=_cparams(("arbitrary",)),
    )(o_na, proj, h_f, h_b, proj, proj, x, target, gate, ml_norm_w, fnw, w_out_b)


def _in_bwd(pieces, x, dres, w_in_t, norm_w, scale, shift):
    t, d = x.shape
    tm = 512
    nt = t // tm
    widths = [p.shape[1] for p in pieces]
    offs = [sum(widths[:k]) for k in range(len(widths))]
    assert sum(widths) == IN_PAD
    npc = len(pieces)

    def body(*refs):
        p_refs = refs[:npc]
        (x_ref, dres_ref, w_hbm, nw_ref, sc_ref, sh_ref,
         gx_ref, gw_hbm, gb_ref, dsc_ref, dsh_ref, gnw_ref, w_vmem, acc, stage, sem) = refs[npc:]
        i = pl.program_id(0)

        @pl.when(i == 0)
        def _():
            cp = pltpu.make_async_copy(w_hbm, w_vmem, sem.at[0])
            cp.start()
            acc[...] = jnp.zeros_like(acc)
            gb_ref[...] = jnp.zeros_like(gb_ref)
            dsc_ref[...] = jnp.zeros_like(dsc_ref)
            dsh_ref[...] = jnp.zeros_like(dsh_ref)
            gnw_ref[...] = jnp.zeros_like(gnw_ref)
            cp.wait()

        nw = nw_ref[...]
        s1 = 1.0 + sc_ref[...]
        h, xn, r = _modulated_norm(x_ref[...], nw, sc_ref[...], sh_ref[...])
        hb = h.astype(BF16)
        dhv = jnp.zeros((tm, d), F32)
        for p_ref, c0, w in zip(p_refs, offs, widths):
            pt = p_ref[...]
            pb = pt.astype(BF16)
            dhv = dhv + _nn(pb, w_vmem[c0:c0 + w, :])
            acc[:, c0:c0 + w] += _tn(hb, pb)
            gb_ref[:, c0:c0 + w] += jnp.sum(pt.astype(F32), axis=0, keepdims=True)
        dsh_ref[...] += jnp.sum(dhv, axis=0, keepdims=True)
        dsc_ref[...] += jnp.sum(dhv * xn * nw, axis=0, keepdims=True)
        gnw_ref[...] += jnp.sum(dhv * xn * s1, axis=0, keepdims=True)
        dxn = dhv * nw * s1
        gx_ref[...] = dres_ref[...] + r * (dxn - xn * jnp.mean(dxn * xn, axis=-1, keepdims=True))

        @pl.when(i == nt - 1)
        def _():
            copies = []
            for blk in range(IN_PAD // 128):
                slot = blk % 2
                if blk >= 2:
                    copies[blk - 2].wait()
                stage[slot] = acc[:, blk * 128:(blk + 1) * 128].T
                cp = pltpu.make_async_copy(stage.at[slot], gw_hbm.at[pl.ds(blk * 128, 128), :], sem.at[1 + slot])
                cp.start()
                copies.append(cp)
            copies[-2].wait()
            copies[-1].wait()

    full = pl.BlockSpec((tm, d), lambda i: (i, 0))
    return pl.pallas_call(
        body, name="in_bwd", grid=(nt,),
        in_specs=[pl.BlockSpec((tm, w), lambda i: (i, 0)) for w in widths]
        + [full, full, pl.BlockSpec(memory_space=pl.ANY), _row(d), _row(d), _row(d)],
        out_specs=[full, pl.BlockSpec(memory_space=pl.ANY), _row(IN_PAD), _row(d), _row(d), _row(d)],
        out_shape=[jax.ShapeDtypeStruct((t, d), F32), jax.ShapeDtypeStruct((IN_PAD, d), F32),
                   jax.ShapeDtypeStruct((1, IN_PAD), F32)] + [jax.ShapeDtypeStruct((1, d), F32)] * 3,
        scratch_shapes=[pltpu.VMEM((IN_PAD, d), BF16), pltpu.VMEM((d, IN_PAD), F32),
                        pltpu.VMEM((2, 128, d), F32), pltpu.SemaphoreType.DMA((3,))],
        compiler_params=_cparams(("arbitrary",), IN_BWD_VMEM_LIMIT),
    )(*pieces, x, dres, w_in_t, norm_w, scale, shift)


def _na_static(rows):
    cases = [(0, 0), (NA_QROWS, NA_QROWS - 4), (rows - NA_QROWS, rows - NA_KROWS)]
    dy = np.zeros((3, NA_QROWS, NA_KROWS), np.int32)
    rv = np.zeros((3, NA_QROWS, NA_KROWS), bool)
    for cs, (r0, kr0) in enumerate(cases):
        for i in range(NA_QROWS):
            for j in range(NA_KROWS):
                r, kr = r0 + i, kr0 + j
                rs = min(max(r - NA_KH // 2, 0), rows - NA_KH)
                rv[cs, i, j] = rs <= kr <= rs + NA_KH - 1
                dy[cs, i, j] = min(max(kr - r + NA_KH - 1, 0), 2 * NA_KH - 2)
    cq = np.arange(GRID_W)[:, None]
    ck = np.arange(GRID_W)[None, :]
    cs0 = np.clip(cq - NA_KW // 2, 0, GRID_W - NA_KW)
    cv = (ck >= cs0) & (ck < cs0 + NA_KW)
    dx = np.clip(ck - cq, -(NA_KW - 1), NA_KW - 1) + NA_KW - 1
    return dy, rv, dx.astype(np.int32), cv


def _na_bias_table(rpb, rows):
    _, _, dx, cv = _na_static(rows)
    ndy = 2 * NA_KH - 1
    onehot = (dx.reshape(1, -1) == np.arange(2 * NA_KW - 1)[:, None]).astype(np.float32)
    rpx = jnp.dot(rpb.reshape(NA_HEADS * ndy, 2 * NA_KW - 1), jnp.asarray(onehot), precision=HI)
    rpx = jnp.where(cv[None, None], rpx.reshape(NA_HEADS, ndy, GRID_W, GRID_W), NEG)
    neg = jnp.full((NA_HEADS, 1, GRID_W, GRID_W), NEG, F32)
    rpx = jnp.concatenate([rpx, neg], axis=1)
    nxt = jnp.concatenate([rpx[:, 1:], neg], axis=1)
    negs = jnp.broadcast_to(neg, rpx.shape)
    pairs = jnp.concatenate([jnp.concatenate([rpx, nxt], axis=3), jnp.concatenate([rpx, negs], axis=3),
                             jnp.concatenate([negs, rpx], axis=3)], axis=1)
    npair = pairs.shape[1]

    def body(m_ref, o_ref):
        cs = pl.program_id(1)
        r0 = jnp.where(cs == 0, 0, jnp.where(cs == 1, NA_QROWS, rows - NA_QROWS))
        kr0 = jnp.where(cs == 0, 0, jnp.where(cs == 1, NA_QROWS - NA_KH // 2, rows - NA_KROWS))
        for i in range(NA_QROWS):
            r = r0 + i
            rs = jnp.clip(r - NA_KH // 2, 0, rows - NA_KH)
            for jp in range(NA_KROWS // 2):
                kl = kr0 + 2 * jp
                vl = (kl >= rs) & (kl <= rs + NA_KH - 1)
                vr = (kl + 1 >= rs) & (kl + 1 <= rs + NA_KH - 1)
                dyl = jnp.clip(kl - r + NA_KH - 1, 0, ndy)
                dyr = jnp.clip(kl + 1 - r + NA_KH - 1, 0, ndy)
                idx = jnp.where(vl & vr, dyl, jnp.where(vl, 16 + dyl, jnp.where(vr, 32 + dyr, 16 + ndy)))
                o_ref[0, 0, i * GRID_W:(i + 1) * GRID_W, jp * 128:(jp + 1) * 128] = m_ref[0, idx]

    return pl.pallas_call(
        body, name="na_bias_table", grid=(NA_HEADS, 3),
        in_specs=[pl.BlockSpec((1, npair, GRID_W, 128), lambda h, cs: (h, 0, 0, 0))],
        out_specs=pl.BlockSpec((1, 1, NA_QT, NA_KT), lambda h, cs: (h, cs, 0, 0)),
        out_shape=jax.ShapeDtypeStruct((NA_HEADS, 3, NA_QT, NA_KT), F32),
        compiler_params=_cparams(("parallel", "parallel")),
    )(pairs)


def _na_specs(t):
    nb = t // NA_QT
    nkb = t // NA_KCH
    npieces = NA_KT // NA_KCH

    def kb0(b):
        return jnp.clip(b * (NA_QT // NA_KCH) - 1, 0, nkb - npieces)

    def case(b):
        return jnp.where(b == 0, 0, jnp.where(b == nb - 1, 2, 1))

    q_spec = pl.BlockSpec((NA_QT, 128), lambda p, b: (b, p))
    k_specs = [pl.BlockSpec((NA_KCH, 128), lambda p, b, i=i: (kb0(b) + i, 4 + p)) for i in range(npieces)]
    v_specs = [pl.BlockSpec((NA_KCH, 128), lambda p, b, i=i: (kb0(b) + i, 8 + p)) for i in range(npieces)]
    tbl_spec = pl.BlockSpec((2, 1, NA_QT, NA_KT), lambda p, b: (p, case(b), 0, 0))
    io_spec = pl.BlockSpec((NA_QT, 128), lambda p, b: (b, p))
    return nb, npieces, kb0, case, q_spec, k_specs, v_specs, tbl_spec, io_spec


NA_HALF = NA_QT // 2
NA_COMBOS_ALL = tuple((i, 0, NA_QT) for i in range(NA_KT // NA_KCH))
NA_COMBOS_INNER = ((0, 0, NA_HALF),) + tuple((i, 0, NA_QT) for i in range(1, NA_KT // NA_KCH - 1)) \
    + ((NA_KT // NA_KCH - 1, NA_HALF, NA_QT),)


def _na_place(val, r0, r1):
    if (r0, r1) == (0, NA_QT):
        return val
    z = jnp.zeros((NA_HALF, val.shape[1]), val.dtype)
    return jnp.concatenate([val, z] if r0 == 0 else [z, val], axis=0)


def _na_fwd(proj, tbl):
    t = proj.shape[0]
    nb, npieces, _, _, q_spec, k_specs, v_specs, tbl_spec, io_spec = _na_specs(t)
    lse_spec = pl.BlockSpec((1, NA_QT, 2), lambda p, b: (p, b, 0))

    def body(*refs):
        q_ref = refs[0]
        k_refs = refs[1:1 + npieces]
        v_refs = refs[1 + npieces:1 + 2 * npieces]
        tbl_ref, o_ref, lse_ref = refs[1 + 2 * npieces:]
        b = pl.program_id(1)

        def compute(combos):
            lane = lax.broadcasted_iota(jnp.int32, (1, 128), 1)
            qv = q_ref[...].astype(F32) * (NA_HEAD_DIM ** -0.5)
            ks = [r[...].astype(BF16) for r in k_refs]
            vs = [r[...].astype(BF16) for r in v_refs]
            hs = range(2)
            msk = [(lane // NA_HEAD_DIM) == hh for hh in hs]
            qh = [jnp.where(msk[hh], qv, 0.0).astype(BF16) for hh in hs]
            s = [[_nt(qh[hh][r0:r1], ks[i]) + tbl_ref[hh, 0, r0:r1, i * NA_KCH:(i + 1) * NA_KCH]
                  for i, r0, r1 in combos] for hh in hs]
            for h0 in (0, NA_HALF):
                rows = slice(h0, h0 + NA_HALF)
                cover = [(c, i, h0 - r0) for c, (i, r0, r1) in enumerate(combos) if r0 <= h0 < r1]
                part = [[s[hh][c][off:off + NA_HALF] for c, _, off in cover] for hh in hs]
                m = [functools.reduce(jnp.maximum, [jnp.max(v, axis=1, keepdims=True) for v in part[hh]]) for hh in hs]
                p = [[jnp.exp(v - m[hh]) for v in part[hh]] for hh in hs]
                l = [functools.reduce(jnp.add, [jnp.sum(v, axis=1, keepdims=True) for v in p[hh]]) for hh in hs]
                o = [functools.reduce(jnp.add, [_nn(p[hh][k].astype(BF16), vs[i]) for k, (_, i, _) in enumerate(cover)])
                     for hh in hs]
                for hh in hs:
                    lse_ref[0, rows, hh:hh + 1] = m[hh] + jnp.log(l[hh])
                o_ref[rows, :] = jnp.where(msk[0], o[0] / l[0], o[1] / l[1])

        inner = (b > 0) & (b < nb - 1)
        pl.when(inner)(lambda: compute(NA_COMBOS_INNER))
        pl.when(jnp.logical_not(inner))(lambda: compute(NA_COMBOS_ALL))

    return pl.pallas_call(
        body, name="na_fwd", grid=(4, nb),
        in_specs=[q_spec] + k_specs + v_specs + [tbl_spec],
        out_specs=[io_spec, lse_spec],
        out_shape=[jax.ShapeDtypeStruct((t, NA_W), F32), jax.ShapeDtypeStruct((4, t, 2), F32)],
        compiler_params=_cparams(("parallel", "arbitrary")),
    )(*([proj] * (1 + 2 * npieces)), tbl)


def _na_bwd(proj, tbl, d_o, o_na, lse):
    t = proj.shape[0]
    nb, npieces, kb0, case, q_spec, k_specs, v_specs, tbl_spec, io_spec = _na_specs(t)

    def body(*refs):
        q_ref = refs[0]
        k_refs = refs[1:1 + npieces]
        v_refs = refs[1 + npieces:1 + 2 * npieces]
        (tbl_ref, do_ref, o_ref, lse_ref, dq_ref, dk_hbm, dv_hbm, rpb_ref,
         dk_acc, dv_acc, dk_out, dv_out, s_scr, dp_scr, dsb_scr, pnb_scr, sem) = refs[1 + 2 * npieces:]
        p_id = pl.program_id(0)
        b = pl.program_id(1)

        @pl.when(b == 0)
        def _():
            dk_acc[...] = jnp.zeros_like(dk_acc)
            dv_acc[...] = jnp.zeros_like(dv_acc)

        @pl.when((b == 0) | (b == 1) | (b == nb - 1))
        def _():
            rpb_ref[...] = jnp.zeros_like(rpb_ref)

        def compute(combos):
            lane = lax.broadcasted_iota(jnp.int32, (1, 128), 1)
            scale = NA_HEAD_DIM ** -0.5
            qv = q_ref[...].astype(F32) * scale
            ks = [r[...].astype(BF16) for r in k_refs]
            vs = [r[...].astype(BF16) for r in v_refs]
            dov = do_ref[...]
            ov = o_ref[...]
            tok0 = kb0(b) * NA_KCH
            hs = range(2)
            msk = [(lane // NA_HEAD_DIM) == hh for hh in hs]
            qh = [jnp.where(msk[hh], qv, 0.0).astype(BF16) for hh in hs]
            doh = [jnp.where(msk[hh], dov, 0.0) for hh in hs]
            dohb = [doh[hh].astype(BF16) for hh in hs]
            dd = [jnp.sum(doh[hh] * ov, axis=1, keepdims=True) for hh in hs]
            for hh in hs:
                for c, (i, q0, q1) in enumerate(combos):
                    slot = (hh * len(combos) + c) % 2
                    cols = slice(i * NA_KCH, (i + 1) * NA_KCH)
                    s_scr[slot, 0:q1 - q0] = _nt(qh[hh][q0:q1], ks[i])
                    dp_scr[slot, 0:q1 - q0] = _nt(dohb[hh][q0:q1], vs[i])
                    for r0 in range(q0, q1, NA_RC):
                        rows = slice(r0, r0 + NA_RC)
                        loc = slice(r0 - q0, r0 - q0 + NA_RC)
                        p = jnp.exp(s_scr[slot, loc, :] + tbl_ref[hh, 0, rows, cols] - lse_ref[0, rows, hh:hh + 1])
                        d = p * (dp_scr[slot, loc, :] - dd[hh][rows])
                        pnb_scr[hh, rows, cols] = p.astype(BF16)
                        dsb_scr[hh, rows, cols] = d.astype(BF16)
                done = {(i, q0) for i, q0, _ in combos} | {(i, NA_HALF) for i, q0, q1 in combos if q1 - q0 == NA_QT}
                for i in range(npieces):
                    for q0 in (0, NA_HALF):
                        if (i, q0) not in done:
                            dsb_scr[hh, q0:q0 + NA_HALF, i * NA_KCH:(i + 1) * NA_KCH] = jnp.zeros(
                                (NA_HALF, NA_KCH), BF16)
            dqh = [functools.reduce(jnp.add, [_na_place(_nn(dsb_scr[hh, q0:q1, i * NA_KCH:(i + 1) * NA_KCH], ks[i]),
                                                        q0, q1) for i, q0, q1 in combos]) for hh in hs]
            dq_ref[...] = (jnp.where(msk[0], dqh[0], dqh[1]) * scale).astype(BF16)
            for i in range(npieces):
                rows = pl.ds(pl.multiple_of(tok0 + i * NA_KCH, NA_KCH), NA_KCH)
                cols = slice(i * NA_KCH, (i + 1) * NA_KCH)
                q0, q1 = [(a, e) for j, a, e in combos if j == i][0]
                dk_acc[rows, :] += (_tn(dsb_scr[0, q0:q1, cols], qh[0][q0:q1])
                                    + _tn(dsb_scr[1, q0:q1, cols], qh[1][q0:q1]))
                dv_acc[rows, :] += (_tn(pnb_scr[0, q0:q1, cols], dohb[0][q0:q1])
                                    + _tn(pnb_scr[1, q0:q1, cols], dohb[1][q0:q1]))
            for hh in hs:
                acc = dsb_scr[hh, 0:GRID_W, :].astype(F32)
                for i in range(1, NA_QROWS):
                    acc = acc + pltpu.roll(dsb_scr[hh, i * GRID_W:(i + 1) * GRID_W, :].astype(F32),
                                           NA_KT - i * GRID_W, 1)
                rpb_ref[0, 0, hh] += acc

        inner = (b > 0) & (b < nb - 1)
        pl.when(inner)(lambda: compute(NA_COMBOS_INNER))
        pl.when(jnp.logical_not(inner))(lambda: compute(NA_COMBOS_ALL))

        @pl.when(b == nb - 1)
        def _():
            cols = pl.ds(pl.multiple_of(p_id * 128, 128), 128)
            dk_out[...] = dk_acc[...].astype(BF16)
            dv_out[...] = dv_acc[...].astype(BF16)
            ck = pltpu.make_async_copy(dk_out, dk_hbm.at[:, cols], sem.at[0])
            cv = pltpu.make_async_copy(dv_out, dv_hbm.at[:, cols], sem.at[1])
            ck.start()
            cv.start()
            ck.wait()
            cv.wait()

    o512 = jax.ShapeDtypeStruct((t, NA_W), BF16)
    return pl.pallas_call(
        body, name="na_bwd", grid=(4, nb),
        in_specs=[q_spec] + k_specs + v_specs + [tbl_spec, io_spec, io_spec,
                                                 pl.BlockSpec((1, NA_QT, 2), lambda p, b: (p, b, 0))],
        out_specs=[io_spec, pl.BlockSpec(memory_space=pl.ANY), pl.BlockSpec(memory_space=pl.ANY),
                   pl.BlockSpec((1, 1, 2, GRID_W, NA_KT), lambda p, b: (p, case(b), 0, 0, 0))],
        out_shape=[o512, o512, o512, jax.ShapeDtypeStruct((4, 3, 2, GRID_W, NA_KT), F32)],
        scratch_shapes=[pltpu.VMEM((t, 128), F32), pltpu.VMEM((t, 128), F32),
                        pltpu.VMEM((t, 128), BF16), pltpu.VMEM((t, 128), BF16),
                        pltpu.VMEM((2, NA_QT, NA_KCH), F32), pltpu.VMEM((2, NA_QT, NA_KCH), F32),
                        pltpu.VMEM((2, NA_QT, NA_KT), BF16), pltpu.VMEM((2, NA_QT, NA_KT), BF16),
                        pltpu.SemaphoreType.DMA((2,))],
        compiler_params=_cparams(("arbitrary", "arbitrary")),
    )(*([proj] * (1 + 2 * npieces)), tbl, d_o, o_na, lse)


def _rpb_reduce(rpbacc, rows):
    nacc = 4 * 3 * 2

    def shift_body(a_ref, o_ref):
        acc = a_ref[0, 0:1, :]
        for cq in range(1, GRID_W):
            acc = acc + pltpu.roll(a_ref[0, cq:cq + 1, :], NA_KT - cq, 1)
        o_ref[0] = jnp.broadcast_to(acc, (8, NA_KT))

    vec = pl.pallas_call(
        shift_body, name="rpb_shift", grid=(nacc,),
        in_specs=[pl.BlockSpec((1, GRID_W, NA_KT), lambda a: (a, 0, 0))],
        out_specs=pl.BlockSpec((1, 8, NA_KT), lambda a: (a, 0, 0)),
        out_shape=jax.ShapeDtypeStruct((nacc, 8, NA_KT), F32),
        compiler_params=_cparams(("parallel",)),
    )(rpbacc.reshape(nacc, GRID_W, NA_KT))
    a = vec[:, 0].reshape(4, 3, 2, NA_KT).transpose(0, 2, 1, 3).reshape(NA_HEADS, 3, NA_KT)
    if rows // NA_QROWS < 3:
        a = a.at[:, 1].set(0.0)
    dd = np.arange(NA_KROWS)[:, None]
    dxo = np.arange(-(NA_KW - 1), NA_KW)[None, :]
    idx = ((dd * GRID_W + dxo) % NA_KT).reshape(-1)
    g = a[..., idx].reshape(NA_HEADS, 3 * NA_KROWS, 2 * NA_KW - 1)
    g = jnp.pad(g, ((0, 0), (0, 0), (0, 128 - (2 * NA_KW - 1))))
    nmat = np.zeros((16, 3 * NA_KROWS), np.float32)
    for cs, delta in enumerate((0, -(NA_KH // 2), -(NA_KROWS - NA_QROWS))):
        for d in range(NA_KROWS):
            jmi = d - NA_KROWS if (cs == 0 and d > NA_KH - 1) else d
            dy = jmi + delta + NA_KH - 1
            if 0 <= dy <= 2 * NA_KH - 2:
                nmat[dy, cs * NA_KROWS + d] = 1.0

    def body(n_ref, g_ref, o_ref):
        o_ref[0] = jnp.dot(n_ref[...], g_ref[0], precision=HI, preferred_element_type=F32)

    out = pl.pallas_call(
        body, name="rpb_reduce", grid=(NA_HEADS,),
        in_specs=[pl.BlockSpec((16, nmat.shape[1]), lambda h: (0, 0)),
                  pl.BlockSpec((1, nmat.shape[1], 128), lambda h: (h, 0, 0))],
        out_specs=pl.BlockSpec((1, 16, 128), lambda h: (h, 0, 0)),
        out_shape=jax.ShapeDtypeStruct((NA_HEADS, 16, 128), F32),
        compiler_params=_cparams(("parallel",)),
    )(jnp.asarray(nmat), g)
    return out[:, :2 * NA_KH - 1, :2 * NA_KW - 1]


def _halo_specs(tm, t, col, width=1024):
    nth = t // CONV_HALO
    per = tm // CONV_HALO
    return [pl.BlockSpec((tm, width), lambda i: (i, col)),
            pl.BlockSpec((CONV_HALO, width), lambda i: (jnp.maximum(i * per - 1, 0), col)),
            pl.BlockSpec((CONV_HALO, width), lambda i: (jnp.minimum((i + 1) * per, nth - 1), col))]


def _fill_ext(ext, cur_ref, prev_ref, next_ref, tm, nt):
    i = pl.program_id(0)
    hl = CONV_HALO
    ext[0:hl, :] = jnp.where(i == 0, 0.0, prev_ref[...].astype(F32))
    ext[hl:hl + tm, :] = cur_ref[...].astype(F32)
    ext[hl + tm:2 * hl + tm, :] = jnp.where(i == nt - 1, 0.0, next_ref[...].astype(F32))


CONV_HALO = 16
CONV_RC = 16
CONV_CB = 512


def _conv_chunks(tm):
    return [(slice(cb, cb + CONV_CB), slice(rb, rb + CONV_RC))
            for cb in range(0, 1024, CONV_CB) for rb in range(0, tm, CONV_RC)]


def _conv_fwd(proj, conv_w8, conv_b, tm):
    t = proj.shape[0]
    nt = t // tm

    def body(u_ref, up_ref, un_ref, w_ref, b_ref, pre_ref, act_ref, ext):
        _fill_ext(ext, u_ref, up_ref, un_ref, tm, nt)
        for cs, rs in _conv_chunks(tm):
            pre = b_ref[:, cs] + w_ref[0:1, cs] * ext[pl.ds(rs.start + CONV_HALO - 2, CONV_RC), cs]
            for j in range(1, CONV_W):
                pre = pre + w_ref[j:j + 1, cs] * ext[pl.ds(rs.start + CONV_HALO - 2 + j, CONV_RC), cs]
            pre_ref[rs, cs] = pre
            act_ref[rs, cs] = _silu(pre)

    full = pl.BlockSpec((tm, 1024), lambda i: (i, 0))
    o = jax.ShapeDtypeStruct((t, 1024), F32)
    return pl.pallas_call(
        body, name="conv_fwd", grid=(nt,),
        in_specs=_halo_specs(tm, t, 2) + [pl.BlockSpec((8, 1024), lambda i: (0, 0)), _row(1024)],
        out_specs=[full, full], out_shape=[o, o],
        scratch_shapes=[pltpu.VMEM((tm + 2 * CONV_HALO, 1024), F32)],
        compiler_params=_cparams(("parallel",)),
    )(proj, proj, proj, conv_w8, conv_b)


def _conv_bwd(dq, dk, pre, proj, conv_w8, tm):
    t = pre.shape[0]
    nt = t // tm

    def body(dq_ref, dqp_ref, dqn_ref, dk_ref, dkp_ref, dkn_ref, pre_ref, prep_ref, pren_ref,
             u_ref, up_ref, un_ref, w_ref, du_ref, gw_ref, gb_ref, extd, extu):
        i = pl.program_id(0)
        hl = CONV_HALO

        @pl.when(i == 0)
        def _():
            gw_ref[...] = jnp.zeros_like(gw_ref)
            gb_ref[...] = jnp.zeros_like(gb_ref)
        for rows, dqr, dkr, prr, edge in ((slice(0, hl), dqp_ref, dkp_ref, prep_ref, i == 0),
                                          (slice(hl, hl + tm), dq_ref, dk_ref, pre_ref, None),
                                          (slice(hl + tm, 2 * hl + tm), dqn_ref, dkn_ref, pren_ref, i == nt - 1)):
            ds = _dsilu(prr[...])
            dl = dqr[...] * ds[:, 0:ML_W]
            dr = dkr[...] * ds[:, ML_W:]
            if edge is not None:
                dl = jnp.where(edge, 0.0, dl)
                dr = jnp.where(edge, 0.0, dr)
            extd[rows, 0:ML_W] = dl
            extd[rows, ML_W:] = dr
        _fill_ext(extu, u_ref, up_ref, un_ref, tm, nt)
        gb_ref[...] += jnp.sum(extd[hl:hl + tm, :], axis=0, keepdims=True)
        gacc = None
        for cs, rs in _conv_chunks(tm):
            if rs.start == 0:
                gacc = [jnp.zeros((8, CONV_CB), F32) for _ in range(CONV_W)]
            du = w_ref[0:1, cs] * extd[pl.ds(rs.start + hl + 2, CONV_RC), cs]
            for j in range(1, CONV_W):
                du = du + w_ref[j:j + 1, cs] * extd[pl.ds(rs.start + hl + 2 - j, CONV_RC), cs]
            du_ref[rs, cs] = du.astype(BF16)
            dcur = extd[pl.ds(rs.start + hl, CONV_RC), cs]
            for j in range(CONV_W):
                prod = dcur * extu[pl.ds(rs.start + hl - 2 + j, CONV_RC), cs]
                gacc[j] = gacc[j] + functools.reduce(
                    jnp.add, [prod[k:k + 8] for k in range(0, CONV_RC, 8)])
            if rs.stop == tm:
                for j in range(CONV_W):
                    gw_ref[j:j + 1, cs] += jnp.sum(gacc[j], axis=0, keepdims=True)

    full = pl.BlockSpec((tm, 1024), lambda i: (i, 0))
    return pl.pallas_call(
        body, name="conv_bwd", grid=(nt,),
        in_specs=_halo_specs(tm, t, 0, ML_W) + _halo_specs(tm, t, 0, ML_W) + _halo_specs(tm, t, 0)
        + _halo_specs(tm, t, 2) + [pl.BlockSpec((8, 1024), lambda i: (0, 0))],
        out_specs=[full, pl.BlockSpec((8, 1024), lambda i: (0, 0)), _row(1024)],
        out_shape=[jax.ShapeDtypeStruct((t, 1024), BF16), jax.ShapeDtypeStruct((8, 1024), F32),
                   jax.ShapeDtypeStruct((1, 1024), F32)],
        scratch_shapes=[pltpu.VMEM((tm + 2 * CONV_HALO, 1024), F32), pltpu.VMEM((tm + 2 * CONV_HALO, 1024), F32)],
        compiler_params=_cparams(("arbitrary",)),
    )(dq, dq, dq, dk, dk, dk, pre, pre, pre, proj, proj, proj, conv_w8)


def _ml_consts(rev):
    iu = lax.broadcasted_iota(jnp.int32, (ML_CHUNK, ML_CHUNK), 0)
    js = lax.broadcasted_iota(jnp.int32, (ML_CHUNK, ML_CHUNK), 1)
    eye = iu == js
    le = iu <= js
    ge = iu >= js
    csum, csum_t, causal = (ge, le, le) if rev else (le, ge, ge)
    return eye, csum.astype(F32), csum_t.astype(F32), causal


def _col(row, eye):
    return jnp.sum(jnp.where(eye, row, 0.0), axis=1, keepdims=True)


def _rowof(col, eye):
    return jnp.sum(jnp.where(eye, col, 0.0), axis=0, keepdims=True)


def _ml_gates(gi, gf, m0, csum, rev):
    lf = jax.nn.log_sigmoid(gf)
    b_rows = jnp.dot(lf, csum, precision=HI, preferred_element_type=F32)
    bl = jnp.sum(lf, axis=1, keepdims=True)
    a_rows = bl - b_rows + gi
    mloc = jnp.max(a_rows, axis=1, keepdims=True)
    order = list(range(ML_NB))[::-1] if rev else list(range(ML_NB))
    mp, mn, decay = {}, {}, {}
    m = m0
    for n in order:
        mp[n] = m
        m = jnp.maximum(bl[n:n + 1] + m, mloc[n:n + 1])
        mn[n] = m
    for n in order:
        decay[n] = jnp.exp(bl[n:n + 1] + mp[n] - mn[n])
    return b_rows, a_rows, gi - b_rows, mp, mn, decay, order


def _ml_load(q_ref, k_ref, v_ref, n):
    sl = slice(n * ML_CHUNK, (n + 1) * ML_CHUNK)
    qb = q_ref[sl, :].astype(BF16)
    kb = (k_ref[sl, :] * (ML_HEAD_DIM ** -0.5)).astype(BF16)
    vn = v_ref[sl, :].astype(F32)
    return sl, qb, kb, vn


def _ml_state_scan(q_ref, k_ref, v_ref, a_rows, mn, decay, order, c0, n0, eye):
    ns = range(ML_NB)
    ld = [_ml_load(q_ref, k_ref, v_ref, n) for n in ns]
    acol = [_col(a_rows[n:n + 1], eye) for n in ns]
    wcol = [jnp.exp(acol[n] - mn[n]) for n in ns]
    u = [_tn((wcol[n] * ld[n][3]).astype(BF16), ld[n][2]) for n in ns]
    nu = [jnp.sum(wcol[n] * ld[n][2].astype(F32), axis=0, keepdims=True) for n in ns]
    cp, npv = {}, {}
    c, nv = c0, n0
    for n in order:
        cp[n], npv[n] = c, nv
        c = decay[n] * c + u[n]
        nv = decay[n] * nv + nu[n]
    return cp, npv, wcol, c, nv


def _ml_intra_all(q_ref, k_ref, v_ref, b_rows, imb_rows, mp, cp, npv, causal, eye):
    ns = range(ML_NB)
    ld = [_ml_load(q_ref, k_ref, v_ref, n) for n in ns]
    qk = [_nt(ld[n][1], ld[n][2]) for n in ns]
    cq = [_nt(ld[n][1], cp[n].astype(BF16)) for n in ns]
    b_col = [_col(b_rows[n:n + 1], eye) for n in ns]
    dlog = [jnp.where(causal, b_col[n] + imb_rows[n:n + 1], NEG) for n in ns]
    m_inter = [b_col[n] + mp[n] for n in ns]
    m_t = [jnp.maximum(m_inter[n], jnp.max(dlog[n], axis=1, keepdims=True)) for n in ns]
    pm = [jnp.exp(dlog[n] - m_t[n]) for n in ns]
    inter = [jnp.exp(m_inter[n] - m_t[n]) for n in ns]
    floor = [jnp.exp(-m_t[n]) for n in ns]
    s = [qk[n] * pm[n] for n in ns]
    qn = [jnp.sum(ld[n][1].astype(F32) * npv[n].astype(BF16).astype(F32), axis=1, keepdims=True) for n in ns]
    sv = [_nn(s[n].astype(BF16), ld[n][3].astype(BF16)) for n in ns]
    den = [jnp.sum(s[n], axis=1, keepdims=True) + inter[n] * qn[n] for n in ns]
    num = [sv[n] + inter[n] * cq[n] for n in ns]
    dn = [jnp.maximum(jnp.abs(den[n]), floor[n]) for n in ns]
    return ld, [dict(pm=pm[n], s=s[n], inter=inter[n], cq=cq[n], qn=qn[n], num=num[n], den=den[n],
                     floor=floor[n], dn=dn[n]) for n in ns]


def _ml_specs(t, rev):
    nblk = t // ML_TB
    blk = (lambda g: nblk - 1 - g) if rev else (lambda g: g)
    hps = ML_HPS
    tile = lambda c0: pl.BlockSpec((ML_TB, 128 * hps), lambda hg, g, c0=c0: (blk(g), c0 // hps + hg))
    gate = pl.BlockSpec((hps, ML_NB, ML_CHUNK), lambda hg, g: (hg, blk(g), 0))
    cchk = pl.BlockSpec((hps, 1, 128, 128), lambda hg, g: (hg, blk(g), 0, 0))
    nmchk = pl.BlockSpec((hps, 1, 8, 128), lambda hg, g: (hg, blk(g), 0, 0))
    return nblk, blk, tile, gate, cchk, nmchk


def _ml_head_views(refs, hh):
    cols = slice(hh * ML_HEAD_DIM, (hh + 1) * ML_HEAD_DIM)
    return [r.at[:, cols] if len(r.shape) == 2 else r.at[hh] for r in refs]


def _ml_fwd(qk_act, proj, gi, gf, rev, name):
    t = qk_act.shape[0]
    nblk, _, tile, gate, cchk, nmchk = _ml_specs(t, rev)

    def body(*refs):
        for hh in range(ML_HPS):
            one_head(*_ml_head_views(refs, hh))

    def one_head(q_ref, k_ref, v_ref, gi_ref, gf_ref, h_ref, cchk_ref, nmchk_ref, c_ref, nm_ref):
        @pl.when(pl.program_id(1) == 0)
        def _():
            c_ref[...] = jnp.zeros_like(c_ref)
            nm_ref[...] = jnp.zeros_like(nm_ref)
        cchk_ref[0] = c_ref[...]
        nmchk_ref[0] = nm_ref[...]
        eye, csum, _, causal = _ml_consts(rev)
        b_rows, a_rows, imb_rows, mp, mn, decay, order = _ml_gates(
            gi_ref[...], gf_ref[...], nm_ref[1:2, 0:1], csum, rev)
        cp, npv, _, c, nv = _ml_state_scan(q_ref, k_ref, v_ref, a_rows, mn, decay, order,
                                            c_ref[...], nm_ref[0:1, :], eye)
        c_ref[...] = c
        nm_ref[0:1, :] = nv
        nm_ref[1:2, :] = jnp.broadcast_to(mn[order[-1]], (1, 128))
        _, rs = _ml_intra_all(q_ref, k_ref, v_ref, b_rows, imb_rows, mp, cp, npv, causal, eye)
        for n in range(ML_NB):
            h_ref[n * ML_CHUNK:(n + 1) * ML_CHUNK, :] = rs[n]['num'] / rs[n]['dn']

    return pl.pallas_call(
        body, name=name, grid=(ML_HEADS // ML_HPS, nblk),
        in_specs=[tile(0), tile(4), tile(24), gate, gate],
        out_specs=[tile(0), cchk, nmchk],
        out_shape=[jax.ShapeDtypeStruct((t, ML_W), F32),
                   jax.ShapeDtypeStruct((ML_HEADS, nblk, 128, 128), F32),
                   jax.ShapeDtypeStruct((ML_HEADS, nblk, 8, 128), F32)],
        scratch_shapes=[pltpu.VMEM((ML_HPS, 128, 128), F32), pltpu.VMEM((ML_HPS, 8, 128), F32)],
        compiler_params=_cparams(("parallel", "arbitrary")),
    )(qk_act, qk_act, proj, gi, gf)


def _ml_bwd(qk_act, proj, gi, gf, dh, cchk_a, nmchk_a, prev, rev, name):
    t = qk_act.shape[0]
    nblk, _, tile, gate, cchk, nmchk = _ml_specs(t, not rev)

    def body(*refs):
        for hh in range(ML_HPS):
            one_head(*_ml_head_views(refs, hh))

    def one_head(q_ref, k_ref, v_ref, gi_ref, gf_ref, dh_ref, cchk_ref, nmchk_ref, *rest):
        prev_refs = rest[:len(prev)]
        dq_ref, dk_ref, dv_ref, dgi_ref, dgf_ref, dc_ref, dn_ref, db_scr, dbl_scr, di_scr = rest[len(prev):]

        def plus_prev(val, which, rows):
            return val + prev_refs[which][rows, :] if prev else val

        @pl.when(pl.program_id(1) == 0)
        def _():
            dc_ref[...] = jnp.zeros_like(dc_ref)
            dn_ref[...] = jnp.zeros_like(dn_ref)
        eye, csum, csum_t, causal = _ml_consts(rev)
        gfv = gf_ref[...]
        b_rows, a_rows, imb_rows, mp, mn, decay, order = _ml_gates(
            gi_ref[...], gfv, nmchk_ref[0, 1:2, 0:1], csum, rev)
        cp, npv, wcol, _, _ = _ml_state_scan(q_ref, k_ref, v_ref, a_rows, mn, decay, order,
                                             cchk_ref[0], nmchk_ref[0, 0:1, :], eye)
        ns = range(ML_NB)
        ld, rs = _ml_intra_all(q_ref, k_ref, v_ref, b_rows, imb_rows, mp, cp, npv, causal, eye)
        sls = [ld[n][0] for n in ns]
        qbs = [ld[n][1] for n in ns]
        kbs = [ld[n][2] for n in ns]
        vbs = [ld[n][3].astype(BF16) for n in ns]
        rdn = [1.0 / rs[n]['dn'] for n in ns]
        dnum = [dh_ref[sls[n], :] * rdn[n] for n in ns]
        hsum = [jnp.sum(dnum[n] * rs[n]['num'], axis=1, keepdims=True) for n in ns]
        dden = [jnp.where(jnp.abs(rs[n]['den']) > rs[n]['floor'],
                          -hsum[n] * rdn[n] * jnp.sign(rs[n]['den']), 0.0) for n in ns]
        dnb = [dnum[n].astype(BF16) for n in ns]
        dsf = [_nt(dnb[n], vbs[n]) + dden[n] for n in ns]
        dv0 = [_tn(rs[n]['s'].astype(BF16), dnb[n]) for n in ns]
        gb = [(dsf[n] * rs[n]['pm']).astype(BF16) for n in ns]
        cpb = [cp[n].astype(BF16) for n in ns]
        idd = [rs[n]['inter'] * dden[n] for n in ns]
        dqa = [_nn(gb[n], kbs[n]) for n in ns]
        dqc = [_nn(dnb[n], cpb[n]) for n in ns]
        dk0 = [_tn(gb[n], qbs[n]) for n in ns]
        xs = [_tn((rs[n]['inter'] * dnum[n]).astype(BF16), qbs[n]) for n in ns]
        for n in ns:
            dq_ref[sls[n], :] = plus_prev(dqa[n] + rs[n]['inter'] * dqc[n]
                                          + idd[n] * npv[n].astype(BF16).astype(F32), 0, sls[n])
        rr = [dsf[n] * rs[n]['s'] for n in ns]
        dinter = [jnp.sum(dnum[n] * rs[n]['cq'], axis=1, keepdims=True) + dden[n] * rs[n]['qn'] for n in ns]
        dbcol = [jnp.sum(rr[n], axis=1, keepdims=True) + dinter[n] * rs[n]['inter'] for n in ns]
        dimb = [jnp.sum(rr[n], axis=0, keepdims=True) for n in ns]
        xns = [jnp.sum(idd[n] * qbs[n].astype(F32), axis=0, keepdims=True) for n in ns]
        dcn, dnn = {}, {}
        dc, dn = dc_ref[...], dn_ref[0:1, :]
        for n in order[::-1]:
            dcn[n], dnn[n] = dc, dn
            dc = decay[n] * dc + xs[n]
            dn = decay[n] * dn + xns[n]
        dc_ref[...] = dc
        dn_ref[0:1, :] = dn
        kscale = ML_HEAD_DIM ** -0.5
        dcb = [dcn[n].astype(BF16) for n in ns]
        z = [_nn(vbs[n], dcb[n]) for n in ns]
        kd = [_nt(kbs[n], dcb[n]) for n in ns]
        ddecay = [jnp.sum(jnp.sum(dcn[n] * cp[n], axis=1, keepdims=True), axis=0, keepdims=True)
                  + jnp.sum(dnn[n] * npv[n], axis=1, keepdims=True) for n in ns]
        zd = [z[n] + dnn[n] for n in ns]
        dw = [jnp.sum(zd[n] * kbs[n].astype(F32), axis=1, keepdims=True) for n in ns]
        for n in ns:
            dv_ref[sls[n], :] = plus_prev(dv0[n] + wcol[n] * kd[n], 2, sls[n]).astype(dv_ref.dtype)
            dk_ref[sls[n], :] = plus_prev((dk0[n] + wcol[n] * zd[n]) * kscale, 1, sls[n])
        da = [dw[n] * wcol[n] for n in ns]
        dbl = [jnp.sum(da[n], axis=0, keepdims=True) + ddecay[n] * decay[n] for n in ns]
        da_row = [_rowof(da[n], eye) for n in ns]
        db_row = [_rowof(dbcol[n] - da[n], eye) for n in ns]
        for n in ns:
            db_scr[n:n + 1, :] = db_row[n] - dimb[n]
            di_scr[n:n + 1, :] = dimb[n] + da_row[n]
            dbl_scr[n:n + 1, :] = jnp.broadcast_to(dbl[n], (1, ML_CHUNK))
        dlf = jnp.dot(db_scr[...], csum_t, precision=HI, preferred_element_type=F32) + dbl_scr[...]
        dgf_ref[...] = dlf * jax.nn.sigmoid(-gfv)
        dgi_ref[...] = di_scr[...]

    nc = t // ML_CHUNK
    o512 = jax.ShapeDtypeStruct((t, ML_W), F32)
    og = jax.ShapeDtypeStruct((ML_HEADS, nc, ML_CHUNK), F32)
    return pl.pallas_call(
        body, name=name, grid=(ML_HEADS // ML_HPS, nblk),
        in_specs=[tile(0), tile(4), tile(24), gate, gate, tile(0), cchk, nmchk] + [tile(0)] * len(prev),
        out_specs=[tile(0), tile(0), tile(0), gate, gate],
        out_shape=[o512, o512, jax.ShapeDtypeStruct((t, ML_W), BF16 if prev else F32), og, og],
        scratch_shapes=[pltpu.VMEM((ML_HPS, 128, 128), F32), pltpu.VMEM((ML_HPS, 8, 128), F32)]
        + [pltpu.VMEM((ML_HPS, ML_NB, ML_CHUNK), F32)] * 3,
        compiler_params=_cparams(("parallel", "arbitrary")),
    )(qk_act, qk_act, proj, gi, gf, dh, cchk_a, nmchk_a, *prev)


def _gate_rows(gates16, t):
    g = gates16.reshape(t // ML_CHUNK, ML_CHUNK, 4, ML_HEADS).transpose(2, 3, 0, 1)
    return g[0], g[1], g[2], g[3]


def _gate_cols(dgi_f, dgf_f, dgi_b, dgf_b, t):
    g = jnp.stack([dgi_f, dgf_f, dgi_b, dgf_b]).transpose(2, 3, 0, 1).reshape(t, 4 * ML_HEADS)
    return jnp.pad(g, ((0, 0), (0, 128 - 4 * ML_HEADS)))


def _local_step(x, target, shift, scale, gate, norm_w, w_in_t, b_in_p, conv_w8, conv_b, rpb,
                ml_norm_w, w_out_b, final_norm_w):
    t = x.shape[0]
    rows = t // GRID_W
    tm = 512
    proj, gates = _in_proj(x, norm_w, scale, shift, w_in_t, b_in_p)
    tbl = _na_bias_table(rpb, rows)
    o_na, lse_na = _na_fwd(proj, tbl)
    pre, qk_act = _conv_fwd(proj, conv_w8, conv_b, tm)
    gi_f, gf_f, gi_b, gf_b = _gate_rows(gates[:, :4 * ML_HEADS], t)
    h_f, cchk_f, nmchk_f = _ml_fwd(qk_act, proj, gi_f, gf_f, False, "ml_fwd_f")
    h_b, cchk_b, nmchk_b = _ml_fwd(qk_act, proj, gi_b, gf_b, True, "ml_fwd_b")
    (loss, dres, d_ona, d_naz, dhs, d_o, d_z, dgate, g_fnw, g_mlnw, g_w_out) = _tail(
        o_na, proj, h_f, h_b, x, target, gate, ml_norm_w, final_norm_w, w_out_b)
    dq_na, dk_na, dv_na, rpbacc = _na_bwd(proj, tbl, d_ona, o_na, lse_na)
    g_rpb = _rpb_reduce(rpbacc, rows)
    dq_f, dk_f, dv_f, dgi_f, dgf_f = _ml_bwd(qk_act, proj, gi_f, gf_f, dhs, cchk_f, nmchk_f, (),
                                             False, "ml_bwd_f")
    dq_ml, dk_ml, dv_ml, dgi_b, dgf_b = _ml_bwd(qk_act, proj, gi_b, gf_b, dhs, cchk_b, nmchk_b, (dq_f, dk_f, dv_f),
                                                True, "ml_bwd_b")
    du, g_conv_w, g_conv_b = _conv_bwd(dq_ml, dk_ml, pre, proj, conv_w8, tm)
    dgates = _gate_cols(dgi_f, dgf_f, dgi_b, dgf_b, t)
    grad_x, g_w_in, g_b_in, dscale, dshift, g_nw = _in_bwd(
        [dq_na, dk_na, dv_na, d_naz, du, dv_ml, d_o, d_z, dgates], x, dres, w_in_t, norm_w, scale, shift)
    dmod = jnp.concatenate([dshift, dscale, dgate], axis=1)
    return (loss, grad_x, dmod, g_nw, g_w_in, g_b_in, g_conv_w, g_conv_b, g_rpb, g_mlnw, g_w_out, g_fnw)


MESH = pl.DeviceIdType.MESH
N_DEV = 8
ANY = pl.BlockSpec(memory_space=pl.ANY)
WHOLE_VMEM = pl.BlockSpec(memory_space=pltpu.VMEM)


def _allgather8(blocks, name):
    na = len(blocks)

    def body(*refs):
        x_refs = refs[:na]
        out_refs = refs[na:2 * na]
        send_sems, recv_sems, local_sems = refs[2 * na:]
        x, y, c = lax.axis_index("x"), lax.axis_index("y"), lax.axis_index("c")
        me, sibling = (x, y, c), (x, y, 1 - c)
        chips = [(1 - x, y), (x, 1 - y), (1 - x, 1 - y)]

        def rows(a, px, py, pc):
            return out_refs[a].at[4 * px + 2 * py + pc]

        def copy(a, k, block, to, src=None):
            return pltpu.make_async_remote_copy(
                src_ref=rows(a, *block) if src is None else src, dst_ref=rows(a, *block),
                send_sem=send_sems.at[a, k], recv_sem=recv_sems.at[a, k],
                device_id=to, device_id_type=MESH)

        mine, first, passed = [], [], []
        for a in range(na):
            cp = pltpu.make_async_copy(x_refs[a], rows(a, *me), local_sems.at[a])
            cp.start()
            mine.append(cp)
            first.append(copy(a, 0, me, sibling, src=x_refs[a]))
            first += [copy(a, 1 + j, me, (*chip, c), src=x_refs[a]) for j, chip in enumerate(chips)]
        for cp in first:
            cp.start()
        for a in range(na):
            for j, chip in enumerate(chips):
                copy(a, 1 + j, (*chip, c), me).wait_recv()
                fwd = copy(a, 4 + j, (*chip, c), sibling)
                fwd.start()
                passed.append(fwd)
        for a in range(na):
            copy(a, 0, sibling, me).wait_recv()
            for j, chip in enumerate(chips):
                copy(a, 4 + j, (*chip, 1 - c), me).wait_recv()
        for cp in first + passed:
            cp.wait_send()
        for cp in mine:
            cp.wait()

    return pl.pallas_call(
        body, name=name,
        out_shape=[jax.ShapeDtypeStruct((N_DEV,) + b.shape, b.dtype) for b in blocks],
        in_specs=[WHOLE_VMEM] * na, out_specs=[WHOLE_VMEM] * na,
        scratch_shapes=[pltpu.SemaphoreType.DMA((na, 7)), pltpu.SemaphoreType.DMA((na, 7)),
                        pltpu.SemaphoreType.DMA((na,))],
        compiler_params=pltpu.CompilerParams(vmem_limit_bytes=VMEM_LIMIT),
    )(*blocks)


def _pair_exchange(arrs, name):
    na = len(arrs)

    def body(*refs):
        in_refs = refs[:na]
        out_refs = refs[na:2 * na]
        send_sems, recv_sems = refs[2 * na:]
        sibling = (lax.axis_index("x"), lax.axis_index("y"), 1 - lax.axis_index("c"))
        copies = [pltpu.make_async_remote_copy(
            src_ref=in_refs[a], dst_ref=out_refs[a], send_sem=send_sems.at[a], recv_sem=recv_sems.at[a],
            device_id=sibling, device_id_type=MESH) for a in range(na)]
        for cp in copies:
            cp.start()
        for cp in copies:
            cp.wait()

    return pl.pallas_call(
        body, name=name,
        out_shape=[jax.ShapeDtypeStruct(a.shape, a.dtype) for a in arrs],
        in_specs=[ANY] * na, out_specs=[ANY] * na,
        scratch_shapes=[pltpu.SemaphoreType.DMA((na,)), pltpu.SemaphoreType.DMA((na,))],
    )(*arrs)


def _chip_exchange(arrs, name):
    na = len(arrs)

    def body(*refs):
        in_refs = refs[:na]
        out_refs = refs[na:2 * na]
        send_sems, recv_sems, local_sems = refs[2 * na:]
        x, y, c = lax.axis_index("x"), lax.axis_index("y"), lax.axis_index("c")
        my_chip = 2 * x + y
        chips = [(1 - x, y), (x, 1 - y), (1 - x, 1 - y)]
        local, remote = [], []
        for a in range(na):
            cp = pltpu.make_async_copy(in_refs[a].at[my_chip], out_refs[a].at[my_chip], local_sems.at[a])
            cp.start()
            local.append(cp)
            for j, (px, py) in enumerate(chips):
                cp = pltpu.make_async_remote_copy(
                    src_ref=in_refs[a].at[2 * px + py], dst_ref=out_refs[a].at[my_chip],
                    send_sem=send_sems.at[a, j], recv_sem=recv_sems.at[a, j],
                    device_id=(px, py, c), device_id_type=MESH)
                cp.start()
                remote.append(cp)
        for cp in remote:
            cp.wait()
        for cp in local:
            cp.wait()

    return pl.pallas_call(
        body, name=name,
        out_shape=[jax.ShapeDtypeStruct(a.shape, a.dtype) for a in arrs],
        in_specs=[ANY] * na, out_specs=[ANY] * na,
        scratch_shapes=[pltpu.SemaphoreType.DMA((na, 3)), pltpu.SemaphoreType.DMA((na, 3)),
                        pltpu.SemaphoreType.DMA((na,))],
    )(*arrs)


def _rows_tile(r):
    for cand in (512, 256, 128, 64, 32, 16, 8):
        if r % cand == 0:
            return cand
    return r


def _add2(a, b, name, out_dtype):
    s, r, n = a.shape
    tr = _rows_tile(r)

    def body(a_ref, b_ref, o_ref):
        o_ref[...] = (a_ref[...] + b_ref[...]).astype(out_dtype)

    spec = pl.BlockSpec((1, tr, n), lambda i, j: (i, j, 0))
    return pl.pallas_call(
        body, name=name, grid=(s, r // tr), in_specs=[spec, spec], out_specs=spec,
        out_shape=jax.ShapeDtypeStruct(a.shape, out_dtype),
        compiler_params=_cparams(("parallel", "parallel")),
    )(a, b)


def _sum_slabs(a, name):
    s, r, n = a.shape
    tr = _rows_tile(r)

    def body(a_ref, o_ref):
        acc = a_ref[0].astype(F32)
        for k in range(1, s):
            acc = acc + a_ref[k].astype(F32)
        o_ref[...] = acc

    return pl.pallas_call(
        body, name=name, grid=(r // tr,),
        in_specs=[pl.BlockSpec((s, tr, n), lambda i: (0, i, 0))],
        out_specs=pl.BlockSpec((tr, n), lambda i: (i, 0)),
        out_shape=jax.ShapeDtypeStruct((r, n), F32),
        compiler_params=_cparams(("parallel",)),
    )(a)


def _adamw(w, g, m, v, name):
    r, n = w.shape
    if r % 8 == 0:
        blk, grid, imap = (_rows_tile(r), n), (r // _rows_tile(r),), (lambda i: (i, 0))
    else:
        blk, grid, imap = (r, 128), (n // 128,), (lambda i: (0, i))
    c1 = 1.0 / (1.0 - ADAM_B1 ** ADAM_STEP)
    c2 = 1.0 / (1.0 - ADAM_B2 ** ADAM_STEP)

    def body(w_ref, g_ref, m_ref, v_ref, d_ref, nm_ref, nv_ref):
        gv = g_ref[...]
        nm = ADAM_B1 * m_ref[...] + (1.0 - ADAM_B1) * gv
        nv = ADAM_B2 * v_ref[...] + (1.0 - ADAM_B2) * (gv * gv)
        nm_ref[...] = nm
        nv_ref[...] = nv
        d_ref[...] = -ADAM_LR * ((nm * c1) / (jnp.sqrt(nv * c2) + ADAM_EPS) + ADAM_WD * w_ref[...])

    spec = pl.BlockSpec(blk, imap)
    o = jax.ShapeDtypeStruct((r, n), F32)
    return pl.pallas_call(
        body, name=name, grid=grid, in_specs=[spec] * 4, out_specs=[spec] * 3, out_shape=[o, o, o],
        compiler_params=_cparams(("parallel",)),
    )(w, g, m, v)


def _mod_fwd(c_all, w_ada_s, b_ada_s):
    def body(c_ref, w_ref, b_ref, o_ref):
        o_ref[...] = jnp.dot(_silu(c_ref[...]), w_ref[...], precision=HI, preferred_element_type=F32) + b_ref[...]

    return pl.pallas_call(
        body, name="mod_fwd", out_shape=jax.ShapeDtypeStruct((c_all.shape[0], w_ada_s.shape[1]), F32),
        in_specs=[WHOLE_VMEM] * 3, out_specs=WHOLE_VMEM,
        compiler_params=pltpu.CompilerParams(vmem_limit_bytes=VMEM_LIMIT),
    )(c_all, w_ada_s, b_ada_s)


def _wada_grad(c_all, dmod_s):
    def body(c_ref, d_ref, o_ref):
        o_ref[...] = lax.dot_general(_silu(c_ref[...]), d_ref[...], (((0,), (0,)), ((), ())),
                                     precision=HI, preferred_element_type=F32)

    return pl.pallas_call(
        body, name="w_ada_grad", out_shape=jax.ShapeDtypeStruct((c_all.shape[1], dmod_s.shape[1]), F32),
        in_specs=[WHOLE_VMEM] * 2, out_specs=WHOLE_VMEM,
        compiler_params=pltpu.CompilerParams(vmem_limit_bytes=VMEM_LIMIT),
    )(c_all, dmod_s)


SMALL_ROWS = 24


def _pad_rows(v, nrows):
    v = v.reshape(-1)
    return jnp.pad(v, (0, nrows * 1024 - v.shape[0])).reshape(nrows, 1024)


def _pack_small(b_ada, norm_w, b_in, conv_w_full, conv_b, rpb, ml_norm_w, final_norm_w, last):
    parts = [_pad_rows(b_ada, 3), _pad_rows(norm_w, 1), _pad_rows(b_in, 5), _pad_rows(conv_w_full, 5),
             _pad_rows(conv_b, 1), _pad_rows(rpb, 4), _pad_rows(ml_norm_w, 1), _pad_rows(final_norm_w, 1),
             _pad_rows(last, 3)]
    return jnp.concatenate(parts, axis=0)


def _unpack_small(p):
    return dict(b_ada=p[0:3].reshape(1, 3072), norm_w=p[3:4], b_in=p[4:9].reshape(-1)[:IN_W].reshape(1, IN_W),
                conv_w=p[9:14], conv_b=p[14:15],
                rpb=p[15:19].reshape(-1)[:NA_HEADS * 15 * 31].reshape(1, NA_HEADS, 15, 31),
                ml_norm_w=p[19:20, :ML_W], final_norm_w=p[20], last=p[21])


def kernel(x, c, w_ada, b_ada, norm_w, w_in, b_in, conv_w, conv_b, rpb, ml_norm_w, w_out, final_norm_w, loss_target, m_w_ada, m_b_ada, m_norm_w, m_w_in, m_b_in, m_conv_w, m_conv_b, m_rpb, m_ml_norm_w, m_w_out, m_final_norm_w, v_w_ada, v_b_ada, v_norm_w, v_w_in, v_b_in, v_conv_w, v_conv_b, v_rpb, v_ml_norm_w, v_w_out, v_final_norm_w):
    xi, yi, ci = lax.axis_index("x"), lax.axis_index("y"), lax.axis_index("c")
    chip = 2 * xi + yi
    dev = 2 * chip + ci
    t = x.shape[1]
    ada_n = w_ada.shape[2]
    in_n = w_in.shape[2]
    out_r = w_out.shape[1]

    c_blk = jnp.pad(c, ((0, 7), (0, 0)))
    w_in_t, m_w_in_t, v_w_in_t = w_in[0].T, m_w_in[0].T, v_w_in[0].T
    in_h = in_n // 2
    w_in_half = lax.dynamic_slice_in_dim(w_in_t, ci * in_h, in_h, axis=0).astype(BF16)
    w_out_half = lax.dynamic_slice_in_dim(w_out[0], ci * (out_r // 2), out_r // 2, axis=0).astype(BF16)
    conv_blk = jnp.pad(conv_w[0], ((0, 3), (0, 0)))
    c_g, conv_g, w_in_g, w_out_g = _allgather8([c_blk, conv_blk, w_in_half, w_out_half], "gather_c_weights")
    c_all = c_g[:, 0]
    w_out_g = w_out_g.reshape(D_MODEL, D_MODEL)
    b_ada_s = lax.dynamic_slice_in_dim(b_ada, chip * ada_n, ada_n, axis=1)
    mod_s = _mod_fwd(c_all, w_ada[0], b_ada_s)
    (mod_g,) = _allgather8([mod_s], "gather_mod")
    mod_mine = lax.dynamic_index_in_dim(mod_g, dev, axis=1, keepdims=False)
    mod = mod_mine[0::2].reshape(1, 3 * D_MODEL)
    shift, scale, gate = mod[:, :D_MODEL], mod[:, D_MODEL:2 * D_MODEL], mod[:, 2 * D_MODEL:]

    w_in_tp = jnp.pad(w_in_g.reshape(IN_W, D_MODEL), ((0, IN_PAD - IN_W), (0, 0)))
    b_in_p = jnp.pad(b_in, ((0, 0), (0, IN_PAD - IN_W)))
    conv_w8 = conv_g.reshape(4, 2, 8, conv_w.shape[2])[:, 0].transpose(1, 0, 2).reshape(8, D_MODEL)

    (loss, grad_x, dmod, g_nw, g_w_in, g_b_in, g_conv_w, g_conv_b, g_rpb, g_mlnw, g_w_out, g_fnw) = _local_step(
        x[0], loss_target[0], shift, scale, gate, norm_w, w_in_tp, b_in_p, conv_w8, conv_b, rpb[0],
        ml_norm_w, w_out_g, final_norm_w.reshape(1, D_MODEL))

    g_in_t = g_w_in

    def halves(a, per_chip, h):
        return jnp.stack([lax.dynamic_slice_in_dim(a, k * per_chip + h * (per_chip // 2), per_chip // 2, axis=0)
                          for k in range(4)])

    ri, ro = _pair_exchange([halves(g_in_t, in_n, 1 - ci), halves(g_w_out, out_r, 1 - ci)], "rs_pair")
    pi = _add2(halves(g_in_t, in_n, ci), ri, "rs_pair_add_in", BF16)
    po = _add2(halves(g_w_out, out_r, ci), ro, "rs_pair_add_out", BF16)
    qi, qo = _chip_exchange([pi, po], "rs_chips")
    si = _sum_slabs(qi, "rs_sum_in")
    so = _sum_slabs(qo, "rs_sum_out")
    ti, to = _pair_exchange([si, so], "rs_share")
    g_w_in_s = jnp.where(ci == 0, jnp.concatenate([si, ti], axis=0), jnp.concatenate([ti, si], axis=0))
    g_w_out_s = jnp.where(ci == 0, jnp.concatenate([so, to], axis=0), jnp.concatenate([to, so], axis=0))

    small = _pack_small(dmod, g_nw, g_b_in[:, :IN_W], g_conv_w[:CONV_W], g_conv_b, g_rpb, g_mlnw, g_fnw,
                        jnp.pad(loss, ((0, 0), (0, 1024 - 128))))
    (small_g,) = _allgather8([small], "gather_small")
    small_sum = _sum_slabs(small_g, "small_sum")
    gs = _unpack_small(small_sum)
    dmod_all = small_g[:, 0:3].reshape(N_DEV, 3 * D_MODEL)
    g_w_ada_s = _wada_grad(c_all, lax.dynamic_slice_in_dim(dmod_all, chip * ada_n, ada_n, axis=1))
    g_conv_w_s = lax.dynamic_slice_in_dim(gs['conv_w'], chip * conv_w.shape[2], conv_w.shape[2], axis=1)
    loss_total = gs['last'][0]

    zeros3 = jnp.zeros((3, 1024), F32)
    zc = jnp.zeros((CONV_W, D_MODEL), F32)
    pw = _pack_small(b_ada, norm_w, b_in, zc, conv_b, rpb, ml_norm_w, final_norm_w, zeros3)
    pm = _pack_small(m_b_ada, m_norm_w, m_b_in, zc, m_conv_b, m_rpb, m_ml_norm_w, m_final_norm_w, zeros3)
    pv = _pack_small(v_b_ada, v_norm_w, v_b_in, zc, v_conv_b, v_rpb, v_ml_norm_w, v_final_norm_w, zeros3)
    ds_, nms, nvs = [_unpack_small(a) for a in _adamw(pw, small_sum, pm, pv, "adamw_small")]
    d_ada, nm_ada, nv_ada = _adamw(w_ada[0], g_w_ada_s, m_w_ada[0], v_w_ada[0], "adamw_w_ada")
    d_in, nm_in, nv_in = _adamw(w_in_t, g_w_in_s, m_w_in_t, v_w_in_t, "adamw_w_in")
    d_out, nm_out, nv_out = _adamw(w_out[0], g_w_out_s, m_w_out[0], v_w_out[0], "adamw_w_out")
    d_cw, nm_cw, nv_cw = _adamw(conv_w[0], g_conv_w_s, m_conv_w[0], v_conv_w[0], "adamw_conv_w")

    def group(big_ada, big_in, big_out, cw, sm):
        return (big_ada[None], sm['b_ada'], sm['norm_w'], big_in.T[None], sm['b_in'], cw[None], sm['conv_b'],
                sm['rpb'], sm['ml_norm_w'], big_out[None], sm['final_norm_w'])

    return ((loss_total, grad_x[None])
            + group(g_w_ada_s, g_w_in_s, g_w_out_s, g_conv_w_s, gs)
            + group(d_ada, d_in, d_out, d_cw, ds_)
            + group(nm_ada, nm_in, nm_out, nm_cw, nms)
            + group(nv_ada, nv_in, nv_out, nv_cw, nvs))
```

```python
import functools

import numpy as np
import jax
import jax.numpy as jnp
from jax import lax
from jax.experimental import pallas as pl
from jax.experimental.pallas import tpu as pltpu

F32 = jnp.float32
BF16 = jnp.bfloat16
HI = lax.Precision.HIGHEST

D_MODEL = 1024
GRID_W = 64
NA_W = 512
NA_HEAD_DIM = 64
NA_HEADS = 8
NA_KH = 8
NA_KW = 16
ML_W = 512
ML_HEADS = 4
ML_HEAD_DIM = 128
ML_CHUNK = 128
CONV_W = 5
EPS = 1e-6
IN_W = 4 * NA_W + 5 * ML_W + 4 * ML_HEADS
IN_MAIN = 4 * NA_W + 5 * ML_W
IN_PAD = IN_MAIN + 128
NEG = -1e30

ADAM_LR = 0.001
ADAM_B1 = 0.9
ADAM_B2 = 0.999
ADAM_EPS = 1e-08
ADAM_WD = 0.01
ADAM_STEP = 10

NA_QROWS = 8
NA_KROWS = 16
NA_QT = NA_QROWS * GRID_W
NA_KT = NA_KROWS * GRID_W
NA_KCH = 256
NA_RC = 32
ML_NB = 16
ML_TB = ML_NB * ML_CHUNK
ML_HPS = 1

VMEM_LIMIT = 56 * 1024 * 1024
IN_BWD_VMEM_LIMIT = 60 * 1024 * 1024


def _cparams(sem, vmem=VMEM_LIMIT):
    return pltpu.CompilerParams(dimension_semantics=sem, vmem_limit_bytes=vmem)


def _silu(x):
    return x * jax.nn.sigmoid(x)


def _dsilu(x):
    s = jax.nn.sigmoid(x)
    return s * (1.0 + x * (1.0 - s))


def _dot(a, b, dims):
    return lax.dot_general(a, b, (dims, ((), ())), preferred_element_type=F32)


def _nn(a, b):
    return _dot(a, b, ((1,), (0,)))


def _nt(a, b):
    return _dot(a, b, ((1,), (1,)))


def _tn(a, b):
    return _dot(a, b, ((0,), (0,)))


def _row(n):
    return pl.BlockSpec((1, n), lambda i: (0, 0))


def _modulated_norm(xv, nw, sc, sh):
    r = lax.rsqrt(jnp.mean(xv * xv, axis=-1, keepdims=True) + EPS)
    xn = xv * r
    return xn * nw * (1.0 + sc) + sh, xn, r


IN_TN = 768


def _in_proj(x, norm_w, scale, shift, w_in_t, b_in_p):
    t, d = x.shape
    tm = 2048
    gcol = IN_MAIN // 128

    def body(x_ref, nw_ref, sc_ref, sh_ref, w_ref, b_ref, wg_ref, bg_ref, proj_ref, g_ref, h_scr):
        @pl.when(pl.program_id(1) == 0)
        def _():
            h, _, _ = _modulated_norm(x_ref[...], nw_ref[...], sc_ref[...], sh_ref[...])
            h_scr[...] = h.astype(BF16)
            g_ref[...] = _nt(h_scr[...], wg_ref[...]) + bg_ref[...]
        proj_ref[...] = (_nt(h_scr[...], w_ref[...]) + b_ref[...]).astype(BF16)

    row = lambda n: pl.BlockSpec((1, n), lambda i, j: (0, 0))
    return pl.pallas_call(
        body, name="in_proj", grid=(t // tm, IN_MAIN // IN_TN),
        in_specs=[pl.BlockSpec((tm, d), lambda i, j: (i, 0)), row(d), row(d), row(d),
                  pl.BlockSpec((IN_TN, d), lambda i, j: (j, 0)), pl.BlockSpec((1, IN_TN), lambda i, j: (0, j)),
                  pl.BlockSpec((128, d), lambda i, j: (gcol, 0)), pl.BlockSpec((1, 128), lambda i, j: (0, gcol))],
        out_specs=[pl.BlockSpec((tm, IN_TN), lambda i, j: (i, j)), pl.BlockSpec((tm, 128), lambda i, j: (i, 0))],
        out_shape=[jax.ShapeDtypeStruct((t, IN_MAIN), BF16), jax.ShapeDtypeStruct((t, 128), F32)],
        scratch_shapes=[pltpu.VMEM((tm, d), BF16)],
        compiler_params=_cparams(("parallel", "arbitrary")),
    )(x, norm_w, scale, shift, w_in_t, b_in_p, w_in_t, b_in_p)


def _ml_norm_parts(hs, o, z, nw):
    outs = []
    for hh in range(ML_HEADS):
        sl = slice(hh * ML_HEAD_DIM, (hh + 1) * ML_HEAD_DIM)
        hm = hs[:, sl] * jax.nn.sigmoid(o[:, sl])
        mu = jnp.mean(hm, axis=-1, keepdims=True)
        cen = hm - mu
        var = jnp.mean(cen * cen, axis=-1, keepdims=True)
        rs = lax.rsqrt(var + EPS)
        outs.append((sl, cen * rs, rs))
    return outs


def _tail(o_na, proj, h_f, h_b, x, target, gate, ml_norm_w, fnw, w_out_b):
    t, d = x.shape
    tm = 256

    def body(ona_ref, naz_ref, hf_ref, hb_ref, o_ref, z_ref, x_ref, tg_ref, g_ref, nw_ref, fw_ref, w_ref,
             loss_ref, dres_ref, dona_ref, dnaz_ref, dhs_ref, do_ref, dz_ref, dgate_ref, gfw_ref, gnw_ref,
             gwo_ref, mix_scr):
        @pl.when(pl.program_id(0) == 0)
        def _():
            for r in (loss_ref, dgate_ref, gfw_ref, gnw_ref, gwo_ref):
                r[...] = jnp.zeros_like(r)
        naz = naz_ref[...].astype(F32)
        ona = ona_ref[...]
        sna = _silu(naz)
        mix_scr[:, 0:NA_W] = (ona * sna).astype(BF16)
        hs = hf_ref[...] + hb_ref[...]
        z = z_ref[...].astype(F32)
        ov = o_ref[...].astype(F32)
        parts = _ml_norm_parts(hs, ov, z, nw_ref[...])
        szs = []
        for sl, xn, _ in parts:
            sz = _silu(z[:, sl])
            szs.append(sz)
            mix_scr[:, NA_W + sl.start:NA_W + sl.stop] = (xn * nw_ref[:, sl] * sz).astype(BF16)
        mixb = mix_scr[...]
        wv = w_ref[...]
        yv = _nn(mixb, wv)
        gate_v = g_ref[...]
        hres = x_ref[...] + gate_v * yv
        r = lax.rsqrt(jnp.mean(hres * hres, axis=-1, keepdims=True) + EPS)
        xnf = hres * r
        err = xnf * fw_ref[...] - tg_ref[...]
        loss_ref[...] += 0.5 * jnp.sum(jnp.sum(err * err, axis=-1, keepdims=True) * (1.0 / d), axis=0, keepdims=True)
        dout = err * (1.0 / d)
        gfw_ref[...] += jnp.sum(dout * xnf, axis=0, keepdims=True)
        dxn = dout * fw_ref[...]
        dres = r * (dxn - xnf * jnp.mean(dxn * xnf, axis=-1, keepdims=True))
        dres_ref[...] = dres
        dgate_ref[...] += jnp.sum(dres * yv, axis=0, keepdims=True)
        dyb = (dres * gate_v).astype(BF16)
        gwo_ref[...] += _tn(mixb, dyb)
        dmix = _nt(dyb, wv)
        dna = dmix[:, 0:NA_W]
        dona_ref[...] = dna * sna
        dnaz_ref[...] = (dna * ona * _dsilu(naz)).astype(BF16)
        for (sl, xn, rs), sz in zip(parts, szs):
            dyv = dmix[:, NA_W + sl.start:NA_W + sl.stop]
            zz = z[:, sl]
            w = nw_ref[:, sl]
            dz_ref[:, sl] = (dyv * xn * w * _dsilu(zz)).astype(BF16)
            gnw_ref[:, sl] += jnp.sum(dyv * xn * sz, axis=0, keepdims=True)
            dxm = dyv * w * sz
            dhm = rs * (dxm - jnp.mean(dxm, axis=-1, keepdims=True)
                        - xn * jnp.mean(dxm * xn, axis=-1, keepdims=True))
            so = jax.nn.sigmoid(ov[:, sl])
            dhs_ref[:, sl] = dhm * so
            do_ref[:, sl] = (dhm * hs[:, sl] * so * (1.0 - so)).astype(BF16)

    blk = lambda c: pl.BlockSpec((tm, 512), lambda i, c=c: (i, c))
    full = pl.BlockSpec((tm, d), lambda i: (i, 0))
    o512 = jax.ShapeDtypeStruct((t, 512), F32)
    b512 = jax.ShapeDtypeStruct((t, 512), BF16)
    whole = pl.BlockSpec((d, d), lambda i: (0, 0))
    return pl.pallas_call(
        body, name="tail", grid=(t // tm,),
        in_specs=[blk(0), blk(3), blk(0), blk(0), blk(7), blk(8), full, full, _row(d), _row(ML_W), _row(d), whole],
        out_specs=[pl.BlockSpec((1, 128), lambda i: (0, 0)), full] + [blk(0)] * 5
        + [_row(d), _row(d), _row(ML_W), whole],
        out_shape=[jax.ShapeDtypeStruct((1, 128), F32), jax.ShapeDtypeStruct((t, d), F32),
                   o512, b512, o512, b512, b512]
        + [jax.ShapeDtypeStruct((1, d), F32), jax.ShapeDtypeStruct((1, d), F32),
           jax.ShapeDtypeStruct((1, ML_W), F32), jax.ShapeDtypeStruct((d, d), F32)],
        scratch_shapes=[pltpu.VMEM((tm, d), BF16)],
        compiler_params=_cparams(("arbitrary",)),
    )(o_na, proj, h_f, h_b, proj, proj, x, target, gate, ml_norm_w, fnw, w_out_b)


def _in_bwd(pieces, x, dres, w_in_t, norm_w, scale, shift):
    t, d = x.shape
    tm = 512
    nt = t // tm
    widths = [p.shape[1] for p in pieces]
    offs = [sum(widths[:k]) for k in range(len(widths))]
    assert sum(widths) == IN_PAD
    npc = len(pieces)

    def body(*refs):
        p_refs = refs[:npc]
        (x_ref, dres_ref, w_hbm, nw_ref, sc_ref, sh_ref,
         gx_ref, gw_hbm, gb_ref, dsc_ref, dsh_ref, gnw_ref, w_vmem, acc, stage, sem) = refs[npc:]
        i = pl.program_id(0)

        @pl.when(i == 0)
        def _():
            cp = pltpu.make_async_copy(w_hbm, w_vmem, sem.at[0])
            cp.start()
            acc[...] = jnp.zeros_like(acc)
            gb_ref[...] = jnp.zeros_like(gb_ref)
            dsc_ref[...] = jnp.zeros_like(dsc_ref)
            dsh_ref[...] = jnp.zeros_like(dsh_ref)
            gnw_ref[...] = jnp.zeros_like(gnw_ref)
            cp.wait()

        nw = nw_ref[...]
        s1 = 1.0 + sc_ref[...]
        h, xn, r = _modulated_norm(x_ref[...], nw, sc_ref[...], sh_ref[...])
        hb = h.astype(BF16)
        dhv = jnp.zeros((tm, d), F32)
        for p_ref, c0, w in zip(p_refs, offs, widths):
            pt = p_ref[...]
            pb = pt.astype(BF16)
            dhv = dhv + _nn(pb, w_vmem[c0:c0 + w, :])
            acc[:, c0:c0 + w] += _tn(hb, pb)
            gb_ref[:, c0:c0 + w] += jnp.sum(pt.astype(F32), axis=0, keepdims=True)
        dsh_ref[...] += jnp.sum(dhv, axis=0, keepdims=True)
        dsc_ref[...] += jnp.sum(dhv * xn * nw, axis=0, keepdims=True)
        gnw_ref[...] += jnp.sum(dhv * xn * s1, axis=0, keepdims=True)
        dxn = dhv * nw * s1
        gx_ref[...] = dres_ref[...] + r * (dxn - xn * jnp.mean(dxn * xn, axis=-1, keepdims=True))

        @pl.when(i == nt - 1)
        def _():
            copies = []
            for blk in range(IN_PAD // 128):
                slot = blk % 2
                if blk >= 2:
                    copies[blk - 2].wait()
                stage[slot] = acc[:, blk * 128:(blk + 1) * 128].T
                cp = pltpu.make_async_copy(stage.at[slot], gw_hbm.at[pl.ds(blk * 128, 128), :], sem.at[1 + slot])
                cp.start()
                copies.append(cp)
            copies[-2].wait()
            copies[-1].wait()

    full = pl.BlockSpec((tm, d), lambda i: (i, 0))
    return pl.pallas_call(
        body, name="in_bwd", grid=(nt,),
        in_specs=[pl.BlockSpec((tm, w), lambda i: (i, 0)) for w in widths]
        + [full, full, pl.BlockSpec(memory_space=pl.ANY), _row(d), _row(d), _row(d)],
        out_specs=[full, pl.BlockSpec(memory_space=pl.ANY), _row(IN_PAD), _row(d), _row(d), _row(d)],
        out_shape=[jax.ShapeDtypeStruct((t, d), F32), jax.ShapeDtypeStruct((IN_PAD, d), F32),
                   jax.ShapeDtypeStruct((1, IN_PAD), F32)] + [jax.ShapeDtypeStruct((1, d), F32)] * 3,
        scratch_shapes=[pltpu.VMEM((IN_PAD, d), BF16), pltpu.VMEM((d, IN_PAD), F32),
                        pltpu.VMEM((2, 128, d), F32), pltpu.SemaphoreType.DMA((3,))],
        compiler_params=_cparams(("arbitrary",), IN_BWD_VMEM_LIMIT),
    )(*pieces, x, dres, w_in_t, norm_w, scale, shift)


def _na_static(rows):
    cases = [(0, 0), (NA_QROWS, NA_QROWS - 4), (rows - NA_QROWS, rows - NA_KROWS)]
    dy = np.zeros((3, NA_QROWS, NA_KROWS), np.int32)
    rv = np.zeros((3, NA_QROWS, NA_KROWS), bool)
    for cs, (r0, kr0) in enumerate(cases):
        for i in range(NA_QROWS):
            for j in range(NA_KROWS):
                r, kr = r0 + i, kr0 + j
                rs = min(max(r - NA_KH // 2, 0), rows - NA_KH)
                rv[cs, i, j] = rs <= kr <= rs + NA_KH - 1
                dy[cs, i, j] = min(max(kr - r + NA_KH - 1, 0), 2 * NA_KH - 2)
    cq = np.arange(GRID_W)[:, None]
    ck = np.arange(GRID_W)[None, :]
    cs0 = np.clip(cq - NA_KW // 2, 0, GRID_W - NA_KW)
    cv = (ck >= cs0) & (ck < cs0 + NA_KW)
    dx = np.clip(ck - cq, -(NA_KW - 1), NA_KW - 1) + NA_KW - 1
    return dy, rv, dx.astype(np.int32), cv


def _na_bias_table(rpb, rows):
    _, _, dx, cv = _na_static(rows)
    ndy = 2 * NA_KH - 1
    onehot = (dx.reshape(1, -1) == np.arange(2 * NA_KW - 1)[:, None]).astype(np.float32)
    rpx = jnp.dot(rpb.reshape(NA_HEADS * ndy, 2 * NA_KW - 1), jnp.asarray(onehot), precision=HI)
    rpx = jnp.where(cv[None, None], rpx.reshape(NA_HEADS, ndy, GRID_W, GRID_W), NEG)
    neg = jnp.full((NA_HEADS, 1, GRID_W, GRID_W), NEG, F32)
    rpx = jnp.concatenate([rpx, neg], axis=1)
    nxt = jnp.concatenate([rpx[:, 1:], neg], axis=1)
    negs = jnp.broadcast_to(neg, rpx.shape)
    pairs = jnp.concatenate([jnp.concatenate([rpx, nxt], axis=3), jnp.concatenate([rpx, negs], axis=3),
                             jnp.concatenate([negs, rpx], axis=3)], axis=1)
    npair = pairs.shape[1]

    def body(m_ref, o_ref):
        cs = pl.program_id(1)
        r0 = jnp.where(cs == 0, 0, jnp.where(cs == 1, NA_QROWS, rows - NA_QROWS))
        kr0 = jnp.where(cs == 0, 0, jnp.where(cs == 1, NA_QROWS - NA_KH // 2, rows - NA_KROWS))
        for i in range(NA_QROWS):
            r = r0 + i
            rs = jnp.clip(r - NA_KH // 2, 0, rows - NA_KH)
            for jp in range(NA_KROWS // 2):
                kl = kr0 + 2 * jp
                vl = (kl >= rs) & (kl <= rs + NA_KH - 1)
                vr = (kl + 1 >= rs) & (kl + 1 <= rs + NA_KH - 1)
                dyl = jnp.clip(kl - r + NA_KH - 1, 0, ndy)
                dyr = jnp.clip(kl + 1 - r + NA_KH - 1, 0, ndy)
                idx = jnp.where(vl & vr, dyl, jnp.where(vl, 16 + dyl, jnp.where(vr, 32 + dyr, 16 + ndy)))
                o_ref[0, 0, i * GRID_W:(i + 1) * GRID_W, jp * 128:(jp + 1) * 128] = m_ref[0, idx]

    return pl.pallas_call(
        body, name="na_bias_table", grid=(NA_HEADS, 3),
        in_specs=[pl.BlockSpec((1, npair, GRID_W, 128), lambda h, cs: (h, 0, 0, 0))],
        out_specs=pl.BlockSpec((1, 1, NA_QT, NA_KT), lambda h, cs: (h, cs, 0, 0)),
        out_shape=jax.ShapeDtypeStruct((NA_HEADS, 3, NA_QT, NA_KT), F32),
        compiler_params=_cparams(("parallel", "parallel")),
    )(pairs)


def _na_specs(t):
    nb = t // NA_QT
    nkb = t // NA_KCH
    npieces = NA_KT // NA_KCH

    def kb0(b):
        return jnp.clip(b * (NA_QT // NA_KCH) - 1, 0, nkb - npieces)

    def case(b):
        return jnp.where(b == 0, 0, jnp.where(b == nb - 1, 2, 1))

    q_spec = pl.BlockSpec((NA_QT, 128), lambda p, b: (b, p))
    k_specs = [pl.BlockSpec((NA_KCH, 128), lambda p, b, i=i: (kb0(b) + i, 4 + p)) for i in range(npieces)]
    v_specs = [pl.BlockSpec((NA_KCH, 128), lambda p, b, i=i: (kb0(b) + i, 8 + p)) for i in range(npieces)]
    tbl_spec = pl.BlockSpec((2, 1, NA_QT, NA_KT), lambda p, b: (p, case(b), 0, 0))
    io_spec = pl.BlockSpec((NA_QT, 128), lambda p, b: (b, p))
    return nb, npieces, kb0, case, q_spec, k_specs, v_specs, tbl_spec, io_spec


NA_HALF = NA_QT // 2
NA_COMBOS_ALL = tuple((i, 0, NA_QT) for i in range(NA_KT // NA_KCH))
NA_COMBOS_INNER = ((0, 0, NA_HALF),) + tuple((i, 0, NA_QT) for i in range(1, NA_KT // NA_KCH - 1)) \
    + ((NA_KT // NA_KCH - 1, NA_HALF, NA_QT),)


def _na_place(val, r0, r1):
    if (r0, r1) == (0, NA_QT):
        return val
    z = jnp.zeros((NA_HALF, val.shape[1]), val.dtype)
    return jnp.concatenate([val, z] if r0 == 0 else [z, val], axis=0)


def _na_fwd(proj, tbl):
    t = proj.shape[0]
    nb, npieces, _, _, q_spec, k_specs, v_specs, tbl_spec, io_spec = _na_specs(t)
    lse_spec = pl.BlockSpec((1, NA_QT, 2), lambda p, b: (p, b, 0))

    def body(*refs):
        q_ref = refs[0]
        k_refs = refs[1:1 + npieces]
        v_refs = refs[1 + npieces:1 + 2 * npieces]
        tbl_ref, o_ref, lse_ref = refs[1 + 2 * npieces:]
        b = pl.program_id(1)

        def compute(combos):
            lane = lax.broadcasted_iota(jnp.int32, (1, 128), 1)
            qv = q_ref[...].astype(F32) * (NA_HEAD_DIM ** -0.5)
            ks = [r[...].astype(BF16) for r in k_refs]
            vs = [r[...].astype(BF16) for r in v_refs]
            hs = range(2)
            msk = [(lane // NA_HEAD_DIM) == hh for hh in hs]
            qh = [jnp.where(msk[hh], qv, 0.0).astype(BF16) for hh in hs]
            s = [[_nt(qh[hh][r0:r1], ks[i]) + tbl_ref[hh, 0, r0:r1, i * NA_KCH:(i + 1) * NA_KCH]
                  for i, r0, r1 in combos] for hh in hs]
            for h0 in (0, NA_HALF):
                rows = slice(h0, h0 + NA_HALF)
                cover = [(c, i, h0 - r0) for c, (i, r0, r1) in enumerate(combos) if r0 <= h0 < r1]
                part = [[s[hh][c][off:off + NA_HALF] for c, _, off in cover] for hh in hs]
                m = [functools.reduce(jnp.maximum, [jnp.max(v, axis=1, keepdims=True) for v in part[hh]]) for hh in hs]
                p = [[jnp.exp(v - m[hh]) for v in part[hh]] for hh in hs]
                l = [functools.reduce(jnp.add, [jnp.sum(v, axis=1, keepdims=True) for v in p[hh]]) for hh in hs]
                o = [functools.reduce(jnp.add, [_nn(p[hh][k].astype(BF16), vs[i]) for k, (_, i, _) in enumerate(cover)])
                     for hh in hs]
                for hh in hs:
                    lse_ref[0, rows, hh:hh + 1] = m[hh] + jnp.log(l[hh])
                o_ref[rows, :] = jnp.where(msk[0], o[0] / l[0], o[1] / l[1])

        inner = (b > 0) & (b < nb - 1)
        pl.when(inner)(lambda: compute(NA_COMBOS_INNER))
        pl.when(jnp.logical_not(inner))(lambda: compute(NA_COMBOS_ALL))

    return pl.pallas_call(
        body, name="na_fwd", grid=(4, nb),
        in_specs=[q_spec] + k_specs + v_specs + [tbl_spec],
        out_specs=[io_spec, lse_spec],
        out_shape=[jax.ShapeDtypeStruct((t, NA_W), F32), jax.ShapeDtypeStruct((4, t, 2), F32)],
        compiler_params=_cparams(("parallel", "arbitrary")),
    )(*([proj] * (1 + 2 * npieces)), tbl)


def _na_bwd(proj, tbl, d_o, o_na, lse):
    t = proj.shape[0]
    nb, npieces, kb0, case, q_spec, k_specs, v_specs, tbl_spec, io_spec = _na_specs(t)

    def body(*refs):
        q_ref = refs[0]
        k_refs = refs[1:1 + npieces]
        v_refs = refs[1 + npieces:1 + 2 * npieces]
        (tbl_ref, do_ref, o_ref, lse_ref, dq_ref, dk_hbm, dv_hbm, rpb_ref,
         dk_acc, dv_acc, dk_out, dv_out, s_scr, dp_scr, dsb_scr, pnb_scr, sem) = refs[1 + 2 * npieces:]
        p_id = pl.program_id(0)
        b = pl.program_id(1)

        @pl.when(b == 0)
        def _():
            dk_acc[...] = jnp.zeros_like(dk_acc)
            dv_acc[...] = jnp.zeros_like(dv_acc)

        @pl.when((b == 0) | (b == 1) | (b == nb - 1))
        def _():
            rpb_ref[...] = jnp.zeros_like(rpb_ref)

        def compute(combos):
            lane = lax.broadcasted_iota(jnp.int32, (1, 128), 1)
            scale = NA_HEAD_DIM ** -0.5
            qv = q_ref[...].astype(F32) * scale
            ks = [r[...].astype(BF16) for r in k_refs]
            vs = [r[...].astype(BF16) for r in v_refs]
            dov = do_ref[...]
            ov = o_ref[...]
            tok0 = kb0(b) * NA_KCH
            hs = range(2)
            msk = [(lane // NA_HEAD_DIM) == hh for hh in hs]
            qh = [jnp.where(msk[hh], qv, 0.0).astype(BF16) for hh in hs]
            doh = [jnp.where(msk[hh], dov, 0.0) for hh in hs]
            dohb = [doh[hh].astype(BF16) for hh in hs]
            dd = [jnp.sum(doh[hh] * ov, axis=1, keepdims=True) for hh in hs]
            for hh in hs:
                for c, (i, q0, q1) in enumerate(combos):
                    slot = (hh * len(combos) + c) % 2
                    cols = slice(i * NA_KCH, (i + 1) * NA_KCH)
                    s_scr[slot, 0:q1 - q0] = _nt(qh[hh][q0:q1], ks[i])
                    dp_scr[slot, 0:q1 - q0] = _nt(dohb[hh][q0:q1], vs[i])
                    for r0 in range(q0, q1, NA_RC):
                        rows = slice(r0, r0 + NA_RC)
                        loc = slice(r0 - q0, r0 - q0 + NA_RC)
                        p = jnp.exp(s_scr[slot, loc, :] + tbl_ref[hh, 0, rows, cols] - lse_ref[0, rows, hh:hh + 1])
                        d = p * (dp_scr[slot, loc, :] - dd[hh][rows])
                        pnb_scr[hh, rows, cols] = p.astype(BF16)
                        dsb_scr[hh, rows, cols] = d.astype(BF16)
                done = {(i, q0) for i, q0, _ in combos} | {(i, NA_HALF) for i, q0, q1 in combos if q1 - q0 == NA_QT}
                for i in range(npieces):
                    for q0 in (0, NA_HALF):
                        if (i, q0) not in done:
                            dsb_scr[hh, q0:q0 + NA_HALF, i * NA_KCH:(i + 1) * NA_KCH] = jnp.zeros(
                                (NA_HALF, NA_KCH), BF16)
            dqh = [functools.reduce(jnp.add, [_na_place(_nn(dsb_scr[hh, q0:q1, i * NA_KCH:(i + 1) * NA_KCH], ks[i]),
                                                        q0, q1) for i, q0, q1 in combos]) for hh in hs]
            dq_ref[...] = (jnp.where(msk[0], dqh[0], dqh[1]) * scale).astype(BF16)
            for i in range(npieces):
                rows = pl.ds(pl.multiple_of(tok0 + i * NA_KCH, NA_KCH), NA_KCH)
                cols = slice(i * NA_KCH, (i + 1) * NA_KCH)
                q0, q1 = [(a, e) for j, a, e in combos if j == i][0]
                dk_acc[rows, :] += (_tn(dsb_scr[0, q0:q1, cols], qh[0][q0:q1])
                                    + _tn(dsb_scr[1, q0:q1, cols], qh[1][q0:q1]))
                dv_acc[rows, :] += (_tn(pnb_scr[0, q0:q1, cols], dohb[0][q0:q1])
                                    + _tn(pnb_scr[1, q0:q1, cols], dohb[1][q0:q1]))
            for hh in hs:
                acc = dsb_scr[hh, 0:GRID_W, :].astype(F32)
                for i in range(1, NA_QROWS):
                    acc = acc + pltpu.roll(dsb_scr[hh, i * GRID_W:(i + 1) * GRID_W, :].astype(F32),
                                           NA_KT - i * GRID_W, 1)
                rpb_ref[0, 0, hh] += acc

        inner = (b > 0) & (b < nb - 1)
        pl.when(inner)(lambda: compute(NA_COMBOS_INNER))
        pl.when(jnp.logical_not(inner))(lambda: compute(NA_COMBOS_ALL))

        @pl.when(b == nb - 1)
        def _():
            cols = pl.ds(pl.multiple_of(p_id * 128, 128), 128)
            dk_out[...] = dk_acc[...].astype(BF16)
            dv_out[...] = dv_acc[...].astype(BF16)
            ck = pltpu.make_async_copy(dk_out, dk_hbm.at[:, cols], sem.at[0])
            cv = pltpu.make_async_copy(dv_out, dv_hbm.at[:, cols], sem.at[1])
            ck.start()
            cv.start()
            ck.wait()
            cv.wait()

    o512 = jax.ShapeDtypeStruct((t, NA_W), BF16)
    return pl.pallas_call(
        body, name="na_bwd", grid=(4, nb),
        in_specs=[q_spec] + k_specs + v_specs + [tbl_spec, io_spec, io_spec,
                                                 pl.BlockSpec((1, NA_QT, 2), lambda p, b: (p, b, 0))],
        out_specs=[io_spec, pl.BlockSpec(memory_space=pl.ANY), pl.BlockSpec(memory_space=pl.ANY),
                   pl.BlockSpec((1, 1, 2, GRID_W, NA_KT), lambda p, b: (p, case(b), 0, 0, 0))],
        out_shape=[o512, o512, o512, jax.ShapeDtypeStruct((4, 3, 2, GRID_W, NA_KT), F32)],
        scratch_shapes=[pltpu.VMEM((t, 128), F32), pltpu.VMEM((t, 128), F32),
                        pltpu.VMEM((t, 128), BF16), pltpu.VMEM((t, 128), BF16),
                        pltpu.VMEM((2, NA_QT, NA_KCH), F32), pltpu.VMEM((2, NA_QT, NA_KCH), F32),
                        pltpu.VMEM((2, NA_QT, NA_KT), BF16), pltpu.VMEM((2, NA_QT, NA_KT), BF16),
                        pltpu.SemaphoreType.DMA((2,))],
        compiler_params=_cparams(("arbitrary", "arbitrary")),
    )(*([proj] * (1 + 2 * npieces)), tbl, d_o, o_na, lse)


def _rpb_reduce(rpbacc, rows):
    nacc = 4 * 3 * 2

    def shift_body(a_ref, o_ref):
        acc = a_ref[0, 0:1, :]
        for cq in range(1, GRID_W):
            acc = acc + pltpu.roll(a_ref[0, cq:cq + 1, :], NA_KT - cq, 1)
        o_ref[0] = jnp.broadcast_to(acc, (8, NA_KT))

    vec = pl.pallas_call(
        shift_body, name="rpb_shift", grid=(nacc,),
        in_specs=[pl.BlockSpec((1, GRID_W, NA_KT), lambda a: (a, 0, 0))],
        out_specs=pl.BlockSpec((1, 8, NA_KT), lambda a: (a, 0, 0)),
        out_shape=jax.ShapeDtypeStruct((nacc, 8, NA_KT), F32),
        compiler_params=_cparams(("parallel",)),
    )(rpbacc.reshape(nacc, GRID_W, NA_KT))
    a = vec[:, 0].reshape(4, 3, 2, NA_KT).transpose(0, 2, 1, 3).reshape(NA_HEADS, 3, NA_KT)
    if rows // NA_QROWS < 3:
        a = a.at[:, 1].set(0.0)
    dd = np.arange(NA_KROWS)[:, None]
    dxo = np.arange(-(NA_KW - 1), NA_KW)[None, :]
    idx = ((dd * GRID_W + dxo) % NA_KT).reshape(-1)
    g = a[..., idx].reshape(NA_HEADS, 3 * NA_KROWS, 2 * NA_KW - 1)
    g = jnp.pad(g, ((0, 0), (0, 0), (0, 128 - (2 * NA_KW - 1))))
    nmat = np.zeros((16, 3 * NA_KROWS), np.float32)
    for cs, delta in enumerate((0, -(NA_KH // 2), -(NA_KROWS - NA_QROWS))):
        for d in range(NA_KROWS):
            jmi = d - NA_KROWS if (cs == 0 and d > NA_KH - 1) else d
            dy = jmi + delta + NA_KH - 1
            if 0 <= dy <= 2 * NA_KH - 2:
                nmat[dy, cs * NA_KROWS + d] = 1.0

    def body(n_ref, g_ref, o_ref):
        o_ref[0] = jnp.dot(n_ref[...], g_ref[0], precision=HI, preferred_element_type=F32)

    out = pl.pallas_call(
        body, name="rpb_reduce", grid=(NA_HEADS,),
        in_specs=[pl.BlockSpec((16, nmat.shape[1]), lambda h: (0, 0)),
                  pl.BlockSpec((1, nmat.shape[1], 128), lambda h: (h, 0, 0))],
        out_specs=pl.BlockSpec((1, 16, 128), lambda h: (h, 0, 0)),
        out_shape=jax.ShapeDtypeStruct((NA_HEADS, 16, 128), F32),
        compiler_params=_cparams(("parallel",)),
    )(jnp.asarray(nmat), g)
    return out[:, :2 * NA_KH - 1, :2 * NA_KW - 1]


def _halo_specs(tm, t, col, width=1024):
    nth = t // CONV_HALO
    per = tm // CONV_HALO
    return [pl.BlockSpec((tm, width), lambda i: (i, col)),
            pl.BlockSpec((CONV_HALO, width), lambda i: (jnp.maximum(i * per - 1, 0), col)),
            pl.BlockSpec((CONV_HALO, width), lambda i: (jnp.minimum((i + 1) * per, nth - 1), col))]


def _fill_ext(ext, cur_ref, prev_ref, next_ref, tm, nt):
    i = pl.program_id(0)
    hl = CONV_HALO
    ext[0:hl, :] = jnp.where(i == 0, 0.0, prev_ref[...].astype(F32))
    ext[hl:hl + tm, :] = cur_ref[...].astype(F32)
    ext[hl + tm:2 * hl + tm, :] = jnp.where(i == nt - 1, 0.0, next_ref[...].astype(F32))


CONV_HALO = 16
CONV_RC = 16
CONV_CB = 512


def _conv_chunks(tm):
    return [(slice(cb, cb + CONV_CB), slice(rb, rb + CONV_RC))
            for cb in range(0, 1024, CONV_CB) for rb in range(0, tm, CONV_RC)]


def _conv_fwd(proj, conv_w8, conv_b, tm):
    t = proj.shape[0]
    nt = t // tm

    def body(u_ref, up_ref, un_ref, w_ref, b_ref, pre_ref, act_ref, ext):
        _fill_ext(ext, u_ref, up_ref, un_ref, tm, nt)
        for cs, rs in _conv_chunks(tm):
            pre = b_ref[:, cs] + w_ref[0:1, cs] * ext[pl.ds(rs.start + CONV_HALO - 2, CONV_RC), cs]
            for j in range(1, CONV_W):
                pre = pre + w_ref[j:j + 1, cs] * ext[pl.ds(rs.start + CONV_HALO - 2 + j, CONV_RC), cs]
            pre_ref[rs, cs] = pre
            act_ref[rs, cs] = _silu(pre)

    full = pl.BlockSpec((tm, 1024), lambda i: (i, 0))
    o = jax.ShapeDtypeStruct((t, 1024), F32)
    return pl.pallas_call(
        body, name="conv_fwd", grid=(nt,),
        in_specs=_halo_specs(tm, t, 2) + [pl.BlockSpec((8, 1024), lambda i: (0, 0)), _row(1024)],
        out_specs=[full, full], out_shape=[o, o],
        scratch_shapes=[pltpu.VMEM((tm + 2 * CONV_HALO, 1024), F32)],
        compiler_params=_cparams(("parallel",)),
    )(proj, proj, proj, conv_w8, conv_b)


def _conv_bwd(dq, dk, pre, proj, conv_w8, tm):
    t = pre.shape[0]
    nt = t // tm

    def body(dq_ref, dqp_ref, dqn_ref, dk_ref, dkp_ref, dkn_ref, pre_ref, prep_ref, pren_ref,
             u_ref, up_ref, un_ref, w_ref, du_ref, gw_ref, gb_ref, extd, extu):
        i = pl.program_id(0)
        hl = CONV_HALO

        @pl.when(i == 0)
        def _():
            gw_ref[...] = jnp.zeros_like(gw_ref)
            gb_ref[...] = jnp.zeros_like(gb_ref)
        for rows, dqr, dkr, prr, edge in ((slice(0, hl), dqp_ref, dkp_ref, prep_ref, i == 0),
                                          (slice(hl, hl + tm), dq_ref, dk_ref, pre_ref, None),
                                          (slice(hl + tm, 2 * hl + tm), dqn_ref, dkn_ref, pren_ref, i == nt - 1)):
            ds = _dsilu(prr[...])
            dl = dqr[...] * ds[:, 0:ML_W]
            dr = dkr[...] * ds[:, ML_W:]
            if edge is not None:
                dl = jnp.where(edge, 0.0, dl)
                dr = jnp.where(edge, 0.0, dr)
            extd[rows, 0:ML_W] = dl
            extd[rows, ML_W:] = dr
        _fill_ext(extu, u_ref, up_ref, un_ref, tm, nt)
        gb_ref[...] += jnp.sum(extd[hl:hl + tm, :], axis=0, keepdims=True)
        gacc = None
        for cs, rs in _conv_chunks(tm):
            if rs.start == 0:
                gacc = [jnp.zeros((8, CONV_CB), F32) for _ in range(CONV_W)]
            du = w_ref[0:1, cs] * extd[pl.ds(rs.start + hl + 2, CONV_RC), cs]
            for j in range(1, CONV_W):
                du = du + w_ref[j:j + 1, cs] * extd[pl.ds(rs.start + hl + 2 - j, CONV_RC), cs]
            du_ref[rs, cs] = du.astype(BF16)
            dcur = extd[pl.ds(rs.start + hl, CONV_RC), cs]
            for j in range(CONV_W):
                prod = dcur * extu[pl.ds(rs.start + hl - 2 + j, CONV_RC), cs]
                gacc[j] = gacc[j] + functools.reduce(
                    jnp.add, [prod[k:k + 8] for k in range(0, CONV_RC, 8)])
            if rs.stop == tm:
                for j in range(CONV_W):
                    gw_ref[j:j + 1, cs] += jnp.sum(gacc[j], axis=0, keepdims=True)

    full = pl.BlockSpec((tm, 1024), lambda i: (i, 0))
    return pl.pallas_call(
        body, name="conv_bwd", grid=(nt,),
        in_specs=_halo_specs(tm, t, 0, ML_W) + _halo_specs(tm, t, 0, ML_W) + _halo_specs(tm, t, 0)
        + _halo_specs(tm, t, 2) + [pl.BlockSpec((8, 1024), lambda i: (0, 0))],
        out_specs=[full, pl.BlockSpec((8, 1024), lambda i: (0, 0)), _row(1024)],
        out_shape=[jax.ShapeDtypeStruct((t, 1024), BF16), jax.ShapeDtypeStruct((8, 1024), F32),
                   jax.ShapeDtypeStruct((1, 1024), F32)],
        scratch_shapes=[pltpu.VMEM((tm + 2 * CONV_HALO, 1024), F32), pltpu.VMEM((tm + 2 * CONV_HALO, 1024), F32)],
        compiler_params=_cparams(("arbitrary",)),
    )(dq, dq, dq, dk, dk, dk, pre, pre, pre, proj, proj, proj, conv_w8)


def _ml_consts(rev):
    iu = lax.broadcasted_iota(jnp.int32, (ML_CHUNK, ML_CHUNK), 0)
    js = lax.broadcasted_iota(jnp.int32, (ML_CHUNK, ML_CHUNK), 1)
    eye = iu == js
    le = iu <= js
    ge = iu >= js
    csum, csum_t, sees = (ge, le, ge) if rev else (le, ge, le)
    return eye, csum.astype(F32), csum_t.astype(F32), sees


def _col(row, eye):
    return jnp.sum(jnp.where(eye, row, 0.0), axis=1, keepdims=True)


def _rowof(col, eye):
    return jnp.sum(jnp.where(eye, col, 0.0), axis=0, keepdims=True)


def _row8(row):
    top = lax.broadcasted_iota(jnp.int32, (8, row.shape[1]), 0) == 0
    return jnp.where(top, row, jnp.zeros_like(row))


def _outer_rows(a_row, b_row_bf16):
    hi = a_row.astype(BF16)
    lo = (a_row - hi.astype(F32)).astype(BF16)
    r_a = lax.broadcasted_iota(jnp.int32, (8, a_row.shape[1]), 0)
    r_b = lax.broadcasted_iota(jnp.int32, (8, b_row_bf16.shape[1]), 0)
    lhs = jnp.where(r_a == 0, hi, jnp.where(r_a == 1, lo, jnp.zeros_like(hi)))
    rhs = jnp.where(r_b < 2, b_row_bf16, jnp.zeros_like(b_row_bf16))
    return _tn(lhs, rhs)


def _ml_gates(gi, gf, m0, csum, rev):
    lf = jax.nn.log_sigmoid(gf)
    b_rows = jnp.dot(lf, csum, precision=HI, preferred_element_type=F32)
    bl = jnp.sum(lf, axis=1, keepdims=True)
    a_rows = bl - b_rows + gi
    mloc = jnp.max(a_rows, axis=1, keepdims=True)
    order = list(range(ML_NB))[::-1] if rev else list(range(ML_NB))
    mp, mn, decay = {}, {}, {}
    m = m0
    for n in order:
        mp[n] = m
        m = jnp.maximum(bl[n:n + 1] + m, mloc[n:n + 1])
        mn[n] = m
    for n in order:
        decay[n] = jnp.exp(bl[n:n + 1] + mp[n] - mn[n])
    return b_rows, a_rows, gi - b_rows, mp, mn, decay, order


def _ml_load(q_ref, k_ref, v_ref, n):
    sl = slice(n * ML_CHUNK, (n + 1) * ML_CHUNK)
    qb = q_ref[sl, :].astype(BF16)
    kb = (k_ref[sl, :] * (ML_HEAD_DIM ** -0.5)).astype(BF16)
    vn = v_ref[sl, :].astype(F32)
    return sl, qb, kb, vn


def _ml_state_scan(q_ref, k_ref, v_ref, a_rows, mn, decay, order, c0, n0):
    ns = range(ML_NB)
    ld = [_ml_load(q_ref, k_ref, v_ref, n) for n in ns]
    vt = [ld[n][3].T for n in ns]
    w_row = [jnp.exp(a_rows[n:n + 1] - mn[n]) for n in ns]
    u = [_nn((vt[n] * w_row[n]).astype(BF16), ld[n][2]) for n in ns]
    nu = [_nn(_row8(w_row[n]).astype(BF16), ld[n][2])[0:1] for n in ns]
    cp, npv = {}, {}
    c, nv = c0, n0
    for n in order:
        cp[n], npv[n] = c, nv
        c = decay[n] * c + u[n]
        nv = decay[n] * nv + nu[n]
    return ld, vt, cp, npv, w_row, c, nv


def _ml_intra_all(ld, vt, b_rows, imb_rows, mp, cp, npv, sees, eye):
    ns = range(ML_NB)
    qk = [_nt(ld[n][2], ld[n][1]) for n in ns]
    cq = [_nt(cp[n].astype(BF16), ld[n][1]) for n in ns]
    qn = [_nt(_row8(npv[n]).astype(BF16), ld[n][1])[0:1] for n in ns]
    imb_col = [_col(imb_rows[n:n + 1], eye) for n in ns]
    dlog = [jnp.where(sees, b_rows[n:n + 1] + imb_col[n], NEG) for n in ns]
    m_inter = [b_rows[n:n + 1] + mp[n] for n in ns]
    m_t = [jnp.maximum(m_inter[n], jnp.max(dlog[n], axis=0, keepdims=True)) for n in ns]
    pm = [jnp.exp(dlog[n] - m_t[n]) for n in ns]
    inter = [jnp.exp(m_inter[n] - m_t[n]) for n in ns]
    floor = [jnp.exp(-m_t[n]) for n in ns]
    s = [qk[n] * pm[n] for n in ns]
    sv = [_nn(vt[n].astype(BF16), s[n].astype(BF16)) for n in ns]
    den = [jnp.sum(s[n], axis=0, keepdims=True) + inter[n] * qn[n] for n in ns]
    num = [sv[n] + inter[n] * cq[n] for n in ns]
    dn = [jnp.maximum(jnp.abs(den[n]), floor[n]) for n in ns]
    return [dict(pm=pm[n], s=s[n], inter=inter[n], cq=cq[n], qn=qn[n], num=num[n], den=den[n],
                 floor=floor[n], dn=dn[n]) for n in ns]


def _ml_specs(t, rev):
    nblk = t // ML_TB
    blk = (lambda g: nblk - 1 - g) if rev else (lambda g: g)
    hps = ML_HPS
    tile = lambda c0: pl.BlockSpec((ML_TB, 128 * hps), lambda hg, g, c0=c0: (blk(g), c0 // hps + hg))
    gate = pl.BlockSpec((hps, ML_NB, ML_CHUNK), lambda hg, g: (hg, blk(g), 0))
    cchk = pl.BlockSpec((hps, 1, 128, 128), lambda hg, g: (hg, blk(g), 0, 0))
    nmchk = pl.BlockSpec((hps, 1, 8, 128), lambda hg, g: (hg, blk(g), 0, 0))
    return nblk, blk, tile, gate, cchk, nmchk


def _ml_head_views(refs, hh):
    cols = slice(hh * ML_HEAD_DIM, (hh + 1) * ML_HEAD_DIM)
    return [r.at[:, cols] if len(r.shape) == 2 else r.at[hh] for r in refs]


def _ml_fwd(qk_act, proj, gi, gf, rev, name):
    t = qk_act.shape[0]
    nblk, _, tile, gate, cchk, nmchk = _ml_specs(t, rev)

    def body(*refs):
        for hh in range(ML_HPS):
            one_head(*_ml_head_views(refs, hh))

    def one_head(q_ref, k_ref, v_ref, gi_ref, gf_ref, h_ref, cchk_ref, nmchk_ref, c_ref, nm_ref):
        @pl.when(pl.program_id(1) == 0)
        def _():
            c_ref[...] = jnp.zeros_like(c_ref)
            nm_ref[...] = jnp.zeros_like(nm_ref)
        cchk_ref[0] = c_ref[...]
        nmchk_ref[0] = nm_ref[...]
        eye, csum, _, sees = _ml_consts(rev)
        b_rows, a_rows, imb_rows, mp, mn, decay, order = _ml_gates(
            gi_ref[...], gf_ref[...], nm_ref[1:2, 0:1], csum, rev)
        ld, vt, cp, npv, _, c, nv = _ml_state_scan(q_ref, k_ref, v_ref, a_rows, mn, decay, order,
                                                   c_ref[...], nm_ref[0:1, :])
        c_ref[...] = c
        nm_ref[0:1, :] = nv
        nm_ref[1:2, :] = jnp.broadcast_to(mn[order[-1]], (1, 128))
        rs = _ml_intra_all(ld, vt, b_rows, imb_rows, mp, cp, npv, sees, eye)
        ht = [rs[n]['num'] / rs[n]['dn'] for n in range(ML_NB)]
        for n in range(ML_NB):
            h_ref[n * ML_CHUNK:(n + 1) * ML_CHUNK, :] = ht[n].T

    return pl.pallas_call(
        body, name=name, grid=(ML_HEADS // ML_HPS, nblk),
        in_specs=[tile(0), tile(4), tile(24), gate, gate],
        out_specs=[tile(0), cchk, nmchk],
        out_shape=[jax.ShapeDtypeStruct((t, ML_W), F32),
                   jax.ShapeDtypeStruct((ML_HEADS, nblk, 128, 128), F32),
                   jax.ShapeDtypeStruct((ML_HEADS, nblk, 8, 128), F32)],
        scratch_shapes=[pltpu.VMEM((ML_HPS, 128, 128), F32), pltpu.VMEM((ML_HPS, 8, 128), F32)],
        compiler_params=_cparams(("parallel", "arbitrary")),
    )(qk_act, qk_act, proj, gi, gf)


def _ml_bwd(qk_act, proj, gi, gf, dh, cchk_a, nmchk_a, prev, rev, name):
    t = qk_act.shape[0]
    nblk, _, tile, gate, cchk, nmchk = _ml_specs(t, not rev)

    def body(*refs):
        for hh in range(ML_HPS):
            one_head(*_ml_head_views(refs, hh))

    def one_head(q_ref, k_ref, v_ref, gi_ref, gf_ref, dh_ref, cchk_ref, nmchk_ref, *rest):
        prev_refs = rest[:len(prev)]
        dq_ref, dk_ref, dv_ref, dgi_ref, dgf_ref, dc_ref, dn_ref, db_scr, dbl_scr, di_scr = rest[len(prev):]

        def plus_prev(val, which, rows):
            return val + prev_refs[which][rows, :] if prev else val

        @pl.when(pl.program_id(1) == 0)
        def _():
            dc_ref[...] = jnp.zeros_like(dc_ref)
            dn_ref[...] = jnp.zeros_like(dn_ref)
        eye, csum, csum_t, sees = _ml_consts(rev)
        gfv = gf_ref[...]
        b_rows, a_rows, imb_rows, mp, mn, decay, order = _ml_gates(
            gi_ref[...], gfv, nmchk_ref[0, 1:2, 0:1], csum, rev)
        ld, vt, cp, npv, w_row, _, _ = _ml_state_scan(q_ref, k_ref, v_ref, a_rows, mn, decay, order,
                                                      cchk_ref[0], nmchk_ref[0, 0:1, :])
        ns = range(ML_NB)
        rs = _ml_intra_all(ld, vt, b_rows, imb_rows, mp, cp, npv, sees, eye)
        sls = [ld[n][0] for n in ns]
        qbs = [ld[n][1] for n in ns]
        kbs = [ld[n][2] for n in ns]
        vbs = [ld[n][3].astype(BF16) for n in ns]
        rdn = [1.0 / rs[n]['dn'] for n in ns]
        dnum = [dh_ref[sls[n], :].T * rdn[n] for n in ns]
        hsum = [jnp.sum(dnum[n] * rs[n]['num'], axis=0, keepdims=True) for n in ns]
        dden = [jnp.where(jnp.abs(rs[n]['den']) > rs[n]['floor'],
                          -hsum[n] * rdn[n] * jnp.sign(rs[n]['den']), 0.0) for n in ns]
        dnb = [dnum[n].astype(BF16) for n in ns]
        dsf = [_nn(vbs[n], dnb[n]) + dden[n] for n in ns]
        dv0 = [_nt(rs[n]['s'].astype(BF16), dnb[n]) for n in ns]
        gb = [(dsf[n] * rs[n]['pm']).astype(BF16) for n in ns]
        cpb = [cp[n].astype(BF16) for n in ns]
        idd = [rs[n]['inter'] * dden[n] for n in ns]
        idn = [(rs[n]['inter'] * dnum[n]).astype(BF16) for n in ns]
        dqa = [_tn(gb[n], kbs[n]) for n in ns]
        dqc = [_tn(idn[n], cpb[n]) for n in ns]
        dqn = [_outer_rows(idd[n], npv[n].astype(BF16)) for n in ns]
        dk0 = [_nn(gb[n], qbs[n]) for n in ns]
        xs = [_nn(idn[n], qbs[n]) for n in ns]
        for n in ns:
            dq_ref[sls[n], :] = plus_prev(dqa[n] + dqc[n] + dqn[n], 0, sls[n])
        rr = [dsf[n] * rs[n]['s'] for n in ns]
        dinter = [jnp.sum(dnum[n] * rs[n]['cq'], axis=0, keepdims=True) + dden[n] * rs[n]['qn'] for n in ns]
        dbt = [jnp.sum(rr[n], axis=0, keepdims=True) + dinter[n] * rs[n]['inter'] for n in ns]
        dimb = [jnp.sum(rr[n], axis=1, keepdims=True) for n in ns]
        xns = [_nn(_row8(idd[n]).astype(BF16), qbs[n])[0:1] for n in ns]
        dcn, dnn = {}, {}
        dc, dn = dc_ref[...], dn_ref[0:1, :]
        for n in order[::-1]:
            dcn[n], dnn[n] = dc, dn
            dc = decay[n] * dc + xs[n]
            dn = decay[n] * dn + xns[n]
        dc_ref[...] = dc
        dn_ref[0:1, :] = dn
        kscale = ML_HEAD_DIM ** -0.5
        dcb = [dcn[n].astype(BF16) for n in ns]
        z = [_nn(vbs[n], dcb[n]) for n in ns]
        kd = [_nt(kbs[n], dcb[n]) for n in ns]
        ddecay = [jnp.sum(jnp.sum(dcn[n] * cp[n], axis=1, keepdims=True), axis=0, keepdims=True)
                  + jnp.sum(dnn[n] * npv[n], axis=1, keepdims=True) for n in ns]
        zd = [z[n] + dnn[n] for n in ns]
        dw = [jnp.sum(zd[n] * kbs[n].astype(F32), axis=1, keepdims=True) for n in ns]
        wcol = [_col(w_row[n], eye) for n in ns]
        for n in ns:
            dv_ref[sls[n], :] = plus_prev(dv0[n] + wcol[n] * kd[n], 2, sls[n]).astype(dv_ref.dtype)
            dk_ref[sls[n], :] = plus_prev((dk0[n] + wcol[n] * zd[n]) * kscale, 1, sls[n])
        da = [dw[n] * wcol[n] for n in ns]
        dbl = [jnp.sum(da[n], axis=0, keepdims=True) + ddecay[n] * decay[n] for n in ns]
        key_row = [_rowof(dimb[n] + da[n], eye) for n in ns]
        for n in ns:
            db_scr[n:n + 1, :] = dbt[n] - key_row[n]
            di_scr[n:n + 1, :] = key_row[n]
            dbl_scr[n:n + 1, :] = jnp.broadcast_to(dbl[n], (1, ML_CHUNK))
        dlf = jnp.dot(db_scr[...], csum_t, precision=HI, preferred_element_type=F32) + dbl_scr[...]
        dgf_ref[...] = dlf * jax.nn.sigmoid(-gfv)
        dgi_ref[...] = di_scr[...]

    nc = t // ML_CHUNK
    o512 = jax.ShapeDtypeStruct((t, ML_W), F32)
    og = jax.ShapeDtypeStruct((ML_HEADS, nc, ML_CHUNK), F32)
    return pl.pallas_call(
        body, name=name, grid=(ML_HEADS // ML_HPS, nblk),
        in_specs=[tile(0), tile(4), tile(24), gate, gate, tile(0), cchk, nmchk] + [tile(0)] * len(prev),
        out_specs=[tile(0), tile(0), tile(0), gate, gate],
        out_shape=[o512, o512, jax.ShapeDtypeStruct((t, ML_W), BF16 if prev else F32), og, og],
        scratch_shapes=[pltpu.VMEM((ML_HPS, 128, 128), F32), pltpu.VMEM((ML_HPS, 8, 128), F32)]
        + [pltpu.VMEM((ML_HPS, ML_NB, ML_CHUNK), F32)] * 3,
        compiler_params=_cparams(("parallel", "arbitrary")),
    )(qk_act, qk_act, proj, gi, gf, dh, cchk_a, nmchk_a, *prev)


def _gate_rows(gates16, t):
    g = gates16.reshape(t // ML_CHUNK, ML_CHUNK, 4, ML_HEADS).transpose(2, 3, 0, 1)
    return g[0], g[1], g[2], g[3]


def _gate_cols(dgi_f, dgf_f, dgi_b, dgf_b, t):
    g = jnp.stack([dgi_f, dgf_f, dgi_b, dgf_b]).transpose(2, 3, 0, 1).reshape(t, 4 * ML_HEADS)
    return jnp.pad(g, ((0, 0), (0, 128 - 4 * ML_HEADS)))


def _local_step(x, target, shift, scale, gate, norm_w, w_in_t, b_in_p, conv_w8, conv_b, rpb,
                ml_norm_w, w_out_b, final_norm_w):
    t = x.shape[0]
    rows = t // GRID_W
    tm = 512
    proj, gates = _in_proj(x, norm_w, scale, shift, w_in_t, b_in_p)
    tbl = _na_bias_table(rpb, rows)
    o_na, lse_na = _na_fwd(proj, tbl)
    pre, qk_act = _conv_fwd(proj, conv_w8, conv_b, tm)
    gi_f, gf_f, gi_b, gf_b = _gate_rows(gates[:, :4 * ML_HEADS], t)
    h_f, cchk_f, nmchk_f = _ml_fwd(qk_act, proj, gi_f, gf_f, False, "ml_fwd_f")
    h_b, cchk_b, nmchk_b = _ml_fwd(qk_act, proj, gi_b, gf_b, True, "ml_fwd_b")
    (loss, dres, d_ona, d_naz, dhs, d_o, d_z, dgate, g_fnw, g_mlnw, g_w_out) = _tail(
        o_na, proj, h_f, h_b, x, target, gate, ml_norm_w, final_norm_w, w_out_b)
    dq_na, dk_na, dv_na, rpbacc = _na_bwd(proj, tbl, d_ona, o_na, lse_na)
    g_rpb = _rpb_reduce(rpbacc, rows)
    dq_f, dk_f, dv_f, dgi_f, dgf_f = _ml_bwd(qk_act, proj, gi_f, gf_f, dhs, cchk_f, nmchk_f, (),
                                             False, "ml_bwd_f")
    dq_ml, dk_ml, dv_ml, dgi_b, dgf_b = _ml_bwd(qk_act, proj, gi_b, gf_b, dhs, cchk_b, nmchk_b, (dq_f, dk_f, dv_f),
                                                True, "ml_bwd_b")
    du, g_conv_w, g_conv_b = _conv_bwd(dq_ml, dk_ml, pre, proj, conv_w8, tm)
    dgates = _gate_cols(dgi_f, dgf_f, dgi_b, dgf_b, t)
    grad_x, g_w_in, g_b_in, dscale, dshift, g_nw = _in_bwd(
        [dq_na, dk_na, dv_na, d_naz, du, dv_ml, d_o, d_z, dgates], x, dres, w_in_t, norm_w, scale, shift)
    dmod = jnp.concatenate([dshift, dscale, dgate], axis=1)
    return (loss, grad_x, dmod, g_nw, g_w_in, g_b_in, g_conv_w, g_conv_b, g_rpb, g_mlnw, g_w_out, g_fnw)


MESH = pl.DeviceIdType.MESH
N_DEV = 8
ANY = pl.BlockSpec(memory_space=pl.ANY)
WHOLE_VMEM = pl.BlockSpec(memory_space=pltpu.VMEM)


def _allgather8(blocks, name):
    na = len(blocks)

    def body(*refs):
        x_refs = refs[:na]
        out_refs = refs[na:2 * na]
        send_sems, recv_sems, local_sems = refs[2 * na:]
        x, y, c = lax.axis_index("x"), lax.axis_index("y"), lax.axis_index("c")
        me, sibling = (x, y, c), (x, y, 1 - c)
        chips = [(1 - x, y), (x, 1 - y), (1 - x, 1 - y)]

        def rows(a, px, py, pc):
            return out_refs[a].at[4 * px + 2 * py + pc]

        def copy(a, k, block, to, src=None):
            return pltpu.make_async_remote_copy(
                src_ref=rows(a, *block) if src is None else src, dst_ref=rows(a, *block),
                send_sem=send_sems.at[a, k], recv_sem=recv_sems.at[a, k],
                device_id=to, device_id_type=MESH)

        mine, first, passed = [], [], []
        for a in range(na):
            cp = pltpu.make_async_copy(x_refs[a], rows(a, *me), local_sems.at[a])
            cp.start()
            mine.append(cp)
            first.append(copy(a, 0, me, sibling, src=x_refs[a]))
            first += [copy(a, 1 + j, me, (*chip, c), src=x_refs[a]) for j, chip in enumerate(chips)]
        for cp in first:
            cp.start()
        for a in range(na):
            for j, chip in enumerate(chips):
                copy(a, 1 + j, (*chip, c), me).wait_recv()
                fwd = copy(a, 4 + j, (*chip, c), sibling)
                fwd.start()
                passed.append(fwd)
        for a in range(na):
            copy(a, 0, sibling, me).wait_recv()
            for j, chip in enumerate(chips):
                copy(a, 4 + j, (*chip, 1 - c), me).wait_recv()
        for cp in first + passed:
            cp.wait_send()
        for cp in mine:
            cp.wait()

    return pl.pallas_call(
        body, name=name,
        out_shape=[jax.ShapeDtypeStruct((N_DEV,) + b.shape, b.dtype) for b in blocks],
        in_specs=[WHOLE_VMEM] * na, out_specs=[WHOLE_VMEM] * na,
        scratch_shapes=[pltpu.SemaphoreType.DMA((na, 7)), pltpu.SemaphoreType.DMA((na, 7)),
                        pltpu.SemaphoreType.DMA((na,))],
        compiler_params=pltpu.CompilerParams(vmem_limit_bytes=VMEM_LIMIT),
    )(*blocks)


def _pair_exchange(arrs, name):
    na = len(arrs)

    def body(*refs):
        in_refs = refs[:na]
        out_refs = refs[na:2 * na]
        send_sems, recv_sems = refs[2 * na:]
        sibling = (lax.axis_index("x"), lax.axis_index("y"), 1 - lax.axis_index("c"))
        copies = [pltpu.make_async_remote_copy(
            src_ref=in_refs[a], dst_ref=out_refs[a], send_sem=send_sems.at[a], recv_sem=recv_sems.at[a],
            device_id=sibling, device_id_type=MESH) for a in range(na)]
        for cp in copies:
            cp.start()
        for cp in copies:
            cp.wait()

    return pl.pallas_call(
        body, name=name,
        out_shape=[jax.ShapeDtypeStruct(a.shape, a.dtype) for a in arrs],
        in_specs=[ANY] * na, out_specs=[ANY] * na,
        scratch_shapes=[pltpu.SemaphoreType.DMA((na,)), pltpu.SemaphoreType.DMA((na,))],
    )(*arrs)


def _chip_exchange(arrs, name):
    na = len(arrs)

    def body(*refs):
        in_refs = refs[:na]
        out_refs = refs[na:2 * na]
        send_sems, recv_sems, local_sems = refs[2 * na:]
        x, y, c = lax.axis_index("x"), lax.axis_index("y"), lax.axis_index("c")
        my_chip = 2 * x + y
        chips = [(1 - x, y), (x, 1 - y), (1 - x, 1 - y)]
        local, remote = [], []
        for a in range(na):
            cp = pltpu.make_async_copy(in_refs[a].at[my_chip], out_refs[a].at[my_chip], local_sems.at[a])
            cp.start()
            local.append(cp)
            for j, (px, py) in enumerate(chips):
                cp = pltpu.make_async_remote_copy(
                    src_ref=in_refs[a].at[2 * px + py], dst_ref=out_refs[a].at[my_chip],
                    send_sem=send_sems.at[a, j], recv_sem=recv_sems.at[a, j],
                    device_id=(px, py, c), device_id_type=MESH)
                cp.start()
                remote.append(cp)
        for cp in remote:
            cp.wait()
        for cp in local:
            cp.wait()

    return pl.pallas_call(
        body, name=name,
        out_shape=[jax.ShapeDtypeStruct(a.shape, a.dtype) for a in arrs],
        in_specs=[ANY] * na, out_specs=[ANY] * na,
        scratch_shapes=[pltpu.SemaphoreType.DMA((na, 3)), pltpu.SemaphoreType.DMA((na, 3)),
                        pltpu.SemaphoreType.DMA((na,))],
    )(*arrs)


def _rows_tile(r):
    for cand in (512, 256, 128, 64, 32, 16, 8):
        if r % cand == 0:
            return cand
    return r


def _add2(a, b, name, out_dtype):
    s, r, n = a.shape
    tr = _rows_tile(r)

    def body(a_ref, b_ref, o_ref):
        o_ref[...] = (a_ref[...] + b_ref[...]).astype(out_dtype)

    spec = pl.BlockSpec((1, tr, n), lambda i, j: (i, j, 0))
    return pl.pallas_call(
        body, name=name, grid=(s, r // tr), in_specs=[spec, spec], out_specs=spec,
        out_shape=jax.ShapeDtypeStruct(a.shape, out_dtype),
        compiler_params=_cparams(("parallel", "parallel")),
    )(a, b)


def _sum_slabs(a, name):
    s, r, n = a.shape
    tr = _rows_tile(r)

    def body(a_ref, o_ref):
        acc = a_ref[0].astype(F32)
        for k in range(1, s):
            acc = acc + a_ref[k].astype(F32)
        o_ref[...] = acc

    return pl.pallas_call(
        body, name=name, grid=(r // tr,),
        in_specs=[pl.BlockSpec((s, tr, n), lambda i: (0, i, 0))],
        out_specs=pl.BlockSpec((tr, n), lambda i: (i, 0)),
        out_shape=jax.ShapeDtypeStruct((r, n), F32),
        compiler_params=_cparams(("parallel",)),
    )(a)


def _adamw(w, g, m, v, name):
    r, n = w.shape
    if r % 8 == 0:
        blk, grid, imap = (_rows_tile(r), n), (r // _rows_tile(r),), (lambda i: (i, 0))
    else:
        blk, grid, imap = (r, 128), (n // 128,), (lambda i: (0, i))
    c1 = 1.0 / (1.0 - ADAM_B1 ** ADAM_STEP)
    c2 = 1.0 / (1.0 - ADAM_B2 ** ADAM_STEP)

    def body(w_ref, g_ref, m_ref, v_ref, d_ref, nm_ref, nv_ref):
        gv = g_ref[...]
        nm = ADAM_B1 * m_ref[...] + (1.0 - ADAM_B1) * gv
        nv = ADAM_B2 * v_ref[...] + (1.0 - ADAM_B2) * (gv * gv)
        nm_ref[...] = nm
        nv_ref[...] = nv
        d_ref[...] = -ADAM_LR * ((nm * c1) / (jnp.sqrt(nv * c2) + ADAM_EPS) + ADAM_WD * w_ref[...])

    spec = pl.BlockSpec(blk, imap)
    o = jax.ShapeDtypeStruct((r, n), F32)
    return pl.pallas_call(
        body, name=name, grid=grid, in_specs=[spec] * 4, out_specs=[spec] * 3, out_shape=[o, o, o],
        compiler_params=_cparams(("parallel",)),
    )(w, g, m, v)


def _mod_fwd(c_all, w_ada_s, b_ada_s):
    def body(c_ref, w_ref, b_ref, o_ref):
        o_ref[...] = jnp.dot(_silu(c_ref[...]), w_ref[...], precision=HI, preferred_element_type=F32) + b_ref[...]

    return pl.pallas_call(
        body, name="mod_fwd", out_shape=jax.ShapeDtypeStruct((c_all.shape[0], w_ada_s.shape[1]), F32),
        in_specs=[WHOLE_VMEM] * 3, out_specs=WHOLE_VMEM,
        compiler_params=pltpu.CompilerParams(vmem_limit_bytes=VMEM_LIMIT),
    )(c_all, w_ada_s, b_ada_s)


def _wada_grad(c_all, dmod_s):
    def body(c_ref, d_ref, o_ref):
        o_ref[...] = lax.dot_general(_silu(c_ref[...]), d_ref[...], (((0,), (0,)), ((), ())),
                                     precision=HI, preferred_element_type=F32)

    return pl.pallas_call(
        body, name="w_ada_grad", out_shape=jax.ShapeDtypeStruct((c_all.shape[1], dmod_s.shape[1]), F32),
        in_specs=[WHOLE_VMEM] * 2, out_specs=WHOLE_VMEM,
        compiler_params=pltpu.CompilerParams(vmem_limit_bytes=VMEM_LIMIT),
    )(c_all, dmod_s)


SMALL_ROWS = 24


def _pad_rows(v, nrows):
    v = v.reshape(-1)
    return jnp.pad(v, (0, nrows * 1024 - v.shape[0])).reshape(nrows, 1024)


def _pack_small(b_ada, norm_w, b_in, conv_w_full, conv_b, rpb, ml_norm_w, final_norm_w, last):
    parts = [_pad_rows(b_ada, 3), _pad_rows(norm_w, 1), _pad_rows(b_in, 5), _pad_rows(conv_w_full, 5),
             _pad_rows(conv_b, 1), _pad_rows(rpb, 4), _pad_rows(ml_norm_w, 1), _pad_rows(final_norm_w, 1),
             _pad_rows(last, 3)]
    return jnp.concatenate(parts, axis=0)


def _unpack_small(p):
    return dict(b_ada=p[0:3].reshape(1, 3072), norm_w=p[3:4], b_in=p[4:9].reshape(-1)[:IN_W].reshape(1, IN_W),
                conv_w=p[9:14], conv_b=p[14:15],
                rpb=p[15:19].reshape(-1)[:NA_HEADS * 15 * 31].reshape(1, NA_HEADS, 15, 31),
                ml_norm_w=p[19:20, :ML_W], final_norm_w=p[20], last=p[21])


def kernel(x, c, w_ada, b_ada, norm_w, w_in, b_in, conv_w, conv_b, rpb, ml_norm_w, w_out, final_norm_w, loss_target, m_w_ada, m_b_ada, m_norm_w, m_w_in, m_b_in, m_conv_w, m_conv_b, m_rpb, m_ml_norm_w, m_w_out, m_final_norm_w, v_w_ada, v_b_ada, v_norm_w, v_w_in, v_b_in, v_conv_w, v_conv_b, v_rpb, v_ml_norm_w, v_w_out, v_final_norm_w):
    xi, yi, ci = lax.axis_index("x"), lax.axis_index("y"), lax.axis_index("c")
    chip = 2 * xi + yi
    dev = 2 * chip + ci
    t = x.shape[1]
    ada_n = w_ada.shape[2]
    in_n = w_in.shape[2]
    out_r = w_out.shape[1]

    c_blk = jnp.pad(c, ((0, 7), (0, 0)))
    w_in_t, m_w_in_t, v_w_in_t = w_in[0].T, m_w_in[0].T, v_w_in[0].T
    in_h = in_n // 2
    w_in_half = lax.dynamic_slice_in_dim(w_in_t, ci * in_h, in_h, axis=0).astype(BF16)
    w_out_half = lax.dynamic_slice_in_dim(w_out[0], ci * (out_r // 2), out_r // 2, axis=0).astype(BF16)
    conv_blk = jnp.pad(conv_w[0], ((0, 3), (0, 0)))
    c_g, conv_g, w_in_g, w_out_g = _allgather8([c_blk, conv_blk, w_in_half, w_out_half], "gather_c_weights")
    c_all = c_g[:, 0]
    w_out_g = w_out_g.reshape(D_MODEL, D_MODEL)
    b_ada_s = lax.dynamic_slice_in_dim(b_ada, chip * ada_n, ada_n, axis=1)
    mod_s = _mod_fwd(c_all, w_ada[0], b_ada_s)
    (mod_g,) = _allgather8([mod_s], "gather_mod")
    mod_mine = lax.dynamic_index_in_dim(mod_g, dev, axis=1, keepdims=False)
    mod = mod_mine[0::2].reshape(1, 3 * D_MODEL)
    shift, scale, gate = mod[:, :D_MODEL], mod[:, D_MODEL:2 * D_MODEL], mod[:, 2 * D_MODEL:]

    w_in_tp = jnp.pad(w_in_g.reshape(IN_W, D_MODEL), ((0, IN_PAD - IN_W), (0, 0)))
    b_in_p = jnp.pad(b_in, ((0, 0), (0, IN_PAD - IN_W)))
    conv_w8 = conv_g.reshape(4, 2, 8, conv_w.shape[2])[:, 0].transpose(1, 0, 2).reshape(8, D_MODEL)

    (loss, grad_x, dmod, g_nw, g_w_in, g_b_in, g_conv_w, g_conv_b, g_rpb, g_mlnw, g_w_out, g_fnw) = _local_step(
        x[0], loss_target[0], shift, scale, gate, norm_w, w_in_tp, b_in_p, conv_w8, conv_b, rpb[0],
        ml_norm_w, w_out_g, final_norm_w.reshape(1, D_MODEL))

    g_in_t = g_w_in

    def halves(a, per_chip, h):
        return jnp.stack([lax.dynamic_slice_in_dim(a, k * per_chip + h * (per_chip // 2), per_chip // 2, axis=0)
                          for k in range(4)])

    ri, ro = _pair_exchange([halves(g_in_t, in_n, 1 - ci), halves(g_w_out, out_r, 1 - ci)], "rs_pair")
    pi = _add2(halves(g_in_t, in_n, ci), ri, "rs_pair_add_in", BF16)
    po = _add2(halves(g_w_out, out_r, ci), ro, "rs_pair_add_out", BF16)
    qi, qo = _chip_exchange([pi, po], "rs_chips")
    si = _sum_slabs(qi, "rs_sum_in")
    so = _sum_slabs(qo, "rs_sum_out")
    ti, to = _pair_exchange([si, so], "rs_share")
    g_w_in_s = jnp.where(ci == 0, jnp.concatenate([si, ti], axis=0), jnp.concatenate([ti, si], axis=0))
    g_w_out_s = jnp.where(ci == 0, jnp.concatenate([so, to], axis=0), jnp.concatenate([to, so], axis=0))

    small = _pack_small(dmod, g_nw, g_b_in[:, :IN_W], g_conv_w[:CONV_W], g_conv_b, g_rpb, g_mlnw, g_fnw,
                        jnp.pad(loss, ((0, 0), (0, 1024 - 128))))
    (small_g,) = _allgather8([small], "gather_small")
    small_sum = _sum_slabs(small_g, "small_sum")
    gs = _unpack_small(small_sum)
    dmod_all = small_g[:, 0:3].reshape(N_DEV, 3 * D_MODEL)
    g_w_ada_s = _wada_grad(c_all, lax.dynamic_slice_in_dim(dmod_all, chip * ada_n, ada_n, axis=1))
    g_conv_w_s = lax.dynamic_slice_in_dim(gs['conv_w'], chip * conv_w.shape[2], conv_w.shape[2], axis=1)
    loss_total = gs['last'][0]

    zeros3 = jnp.zeros((3, 1024), F32)
    zc = jnp.zeros((CONV_W, D_MODEL), F32)
    pw = _pack_small(b_ada, norm_w, b_in, zc, conv_b, rpb, ml_norm_w, final_norm_w, zeros3)
    pm = _pack_small(m_b_ada, m_norm_w, m_b_in, zc, m_conv_b, m_rpb, m_ml_norm_w, m_final_norm_w, zeros3)
    pv = _pack_small(v_b_ada, v_norm_w, v_b_in, zc, v_conv_b, v_rpb, v_ml_norm_w, v_final_norm_w, zeros3)
    ds_, nms, nvs = [_unpack_small(a) for a in _adamw(pw, small_sum, pm, pv, "adamw_small")]
    d_ada, nm_ada, nv_ada = _adamw(w_ada[0], g_w_ada_s, m_w_ada[0], v_w_ada[0], "adamw_w_ada")
    d_in, nm_in, nv_in = _adamw(w_in_t, g_w_in_s, m_w_in_t, v_w_in_t, "adamw_w_in")
    d_out, nm_out, nv_out = _adamw(w_out[0], g_w_out_s, m_w_out[0], v_w_out[0], "adamw_w_out")
    d_cw, nm_cw, nv_cw = _adamw(conv_w[0], g_conv_w_s, m_conv_w[0], v_conv_w[0], "adamw_conv_w")

    def group(big_ada, big_in, big_out, cw, sm):
        return (big_ada[None], sm['b_ada'], sm['norm_w'], big_in.T[None], sm['b_in'], cw[None], sm['conv_b'],
                sm['rpb'], sm['ml_norm_w'], big_out[None], sm['final_norm_w'])

    return ((loss_total, grad_x[None])
            + group(g_w_ada_s, g_w_in_s, g_w_out_s, g_conv_w_s, gs)
            + group(d_ada, d_in, d_out, d_cw, ds_)
            + group(nm_ada, nm_in, nm_out, nm_cw, nms)
            + group(nv_ada, nv_in, nv_out, nv_cw, nvs))
```

```python
import functools

import numpy as np
import jax
import jax.numpy as jnp
from jax import lax
from jax.experimental import pallas as pl
from jax.experimental.pallas import tpu as pltpu

F32 = jnp.float32
BF16 = jnp.bfloat16
HI = lax.Precision.HIGHEST

D_MODEL = 1024
GRID_W = 64
NA_W = 512
NA_HEAD_DIM = 64
NA_HEADS = 8
NA_KH = 8
NA_KW = 16
ML_W = 512
ML_HEADS = 4
ML_HEAD_DIM = 128
ML_CHUNK = 128
CONV_W = 5
EPS = 1e-6
IN_W = 4 * NA_W + 5 * ML_W + 4 * ML_HEADS
IN_MAIN = 4 * NA_W + 5 * ML_W
IN_PAD = IN_MAIN + 128
NEG = -1e30

ADAM_LR = 0.001
ADAM_B1 = 0.9
ADAM_B2 = 0.999
ADAM_EPS = 1e-08
ADAM_WD = 0.01
ADAM_STEP = 10

NA_QROWS = 8
NA_KROWS = 16
NA_QT = NA_QROWS * GRID_W
NA_KT = NA_KROWS * GRID_W
NA_KCH = 256
NA_RC = 32
ML_NB = 16
ML_TB = ML_NB * ML_CHUNK
ML_HPS = 1

VMEM_LIMIT = 56 * 1024 * 1024
IN_BWD_VMEM_LIMIT = 60 * 1024 * 1024


def _cparams(sem, vmem=VMEM_LIMIT):
    return pltpu.CompilerParams(dimension_semantics=sem, vmem_limit_bytes=vmem)


def _silu(x):
    return x * jax.nn.sigmoid(x)


def _dsilu(x):
    s = jax.nn.sigmoid(x)
    return s * (1.0 + x * (1.0 - s))


def _dot(a, b, dims):
    return lax.dot_general(a, b, (dims, ((), ())), preferred_element_type=F32)


def _nn(a, b):
    return _dot(a, b, ((1,), (0,)))


def _nt(a, b):
    return _dot(a, b, ((1,), (1,)))


def _tn(a, b):
    return _dot(a, b, ((0,), (0,)))


def _row(n):
    return pl.BlockSpec((1, n), lambda i: (0, 0))


def _modulated_norm(xv, nw, sc, sh):
    r = lax.rsqrt(jnp.mean(xv * xv, axis=-1, keepdims=True) + EPS)
    xn = xv * r
    return xn * nw * (1.0 + sc) + sh, xn, r


IN_TN = 768


def _in_proj(x, norm_w, scale, shift, w_in_t, b_in_p):
    t, d = x.shape
    tm = 2048
    gcol = IN_MAIN // 128

    def body(x_ref, nw_ref, sc_ref, sh_ref, w_ref, b_ref, wg_ref, bg_ref, proj_ref, g_ref, h_scr):
        @pl.when(pl.program_id(1) == 0)
        def _():
            h, _, _ = _modulated_norm(x_ref[...], nw_ref[...], sc_ref[...], sh_ref[...])
            h_scr[...] = h.astype(BF16)
            g_ref[...] = _nt(h_scr[...], wg_ref[...]) + bg_ref[...]
        proj_ref[...] = (_nt(h_scr[...], w_ref[...]) + b_ref[...]).astype(BF16)

    row = lambda n: pl.BlockSpec((1, n), lambda i, j: (0, 0))
    return pl.pallas_call(
        body, name="in_proj", grid=(t // tm, IN_MAIN // IN_TN),
        in_specs=[pl.BlockSpec((tm, d), lambda i, j: (i, 0)), row(d), row(d), row(d),
                  pl.BlockSpec((IN_TN, d), lambda i, j: (j, 0)), pl.BlockSpec((1, IN_TN), lambda i, j: (0, j)),
                  pl.BlockSpec((128, d), lambda i, j: (gcol, 0)), pl.BlockSpec((1, 128), lambda i, j: (0, gcol))],
        out_specs=[pl.BlockSpec((tm, IN_TN), lambda i, j: (i, j)), pl.BlockSpec((tm, 128), lambda i, j: (i, 0))],
        out_shape=[jax.ShapeDtypeStruct((t, IN_MAIN), BF16), jax.ShapeDtypeStruct((t, 128), F32)],
        scratch_shapes=[pltpu.VMEM((tm, d), BF16)],
        compiler_params=_cparams(("parallel", "arbitrary")),
    )(x, norm_w, scale, shift, w_in_t, b_in_p, w_in_t, b_in_p)


def _ml_norm_parts(hs, o, z, nw):
    outs = []
    for hh in range(ML_HEADS):
        sl = slice(hh * ML_HEAD_DIM, (hh + 1) * ML_HEAD_DIM)
        hm = hs[:, sl] * jax.nn.sigmoid(o[:, sl])
        mu = jnp.mean(hm, axis=-1, keepdims=True)
        cen = hm - mu
        var = jnp.mean(cen * cen, axis=-1, keepdims=True)
        rs = lax.rsqrt(var + EPS)
        outs.append((sl, cen * rs, rs))
    return outs


def _tail(o_na, proj, h_f, h_b, x, target, gate, ml_norm_w, fnw, w_out_b):
    t, d = x.shape
    tm = 256

    def body(ona_ref, naz_ref, hf_ref, hb_ref, o_ref, z_ref, x_ref, tg_ref, g_ref, nw_ref, fw_ref, w_ref,
             loss_ref, dres_ref, dona_ref, dnaz_ref, dhs_ref, do_ref, dz_ref, dgate_ref, gfw_ref, gnw_ref,
             gwo_ref, mix_scr):
        @pl.when(pl.program_id(0) == 0)
        def _():
            for r in (loss_ref, dgate_ref, gfw_ref, gnw_ref, gwo_ref):
                r[...] = jnp.zeros_like(r)
        naz = naz_ref[...].astype(F32)
        ona = ona_ref[...]
        sna = _silu(naz)
        mix_scr[:, 0:NA_W] = (ona * sna).astype(BF16)
        hs = hf_ref[...] + hb_ref[...]
        z = z_ref[...].astype(F32)
        ov = o_ref[...].astype(F32)
        parts = _ml_norm_parts(hs, ov, z, nw_ref[...])
        szs = []
        for sl, xn, _ in parts:
            sz = _silu(z[:, sl])
            szs.append(sz)
            mix_scr[:, NA_W + sl.start:NA_W + sl.stop] = (xn * nw_ref[:, sl] * sz).astype(BF16)
        mixb = mix_scr[...]
        wv = w_ref[...]
        yv = _nn(mixb, wv)
        gate_v = g_ref[...]
        hres = x_ref[...] + gate_v * yv
        r = lax.rsqrt(jnp.mean(hres * hres, axis=-1, keepdims=True) + EPS)
        xnf = hres * r
        err = xnf * fw_ref[...] - tg_ref[...]
        loss_ref[...] += 0.5 * jnp.sum(jnp.sum(err * err, axis=-1, keepdims=True) * (1.0 / d), axis=0, keepdims=True)
        dout = err * (1.0 / d)
        gfw_ref[...] += jnp.sum(dout * xnf, axis=0, keepdims=True)
        dxn = dout * fw_ref[...]
        dres = r * (dxn - xnf * jnp.mean(dxn * xnf, axis=-1, keepdims=True))
        dres_ref[...] = dres
        dgate_ref[...] += jnp.sum(dres * yv, axis=0, keepdims=True)
        dyb = (dres * gate_v).astype(BF16)
        gwo_ref[...] += _tn(mixb, dyb)
        dmix = _nt(dyb, wv)
        dna = dmix[:, 0:NA_W]
        dona_ref[...] = dna * sna
        dnaz_ref[...] = (dna * ona * _dsilu(naz)).astype(BF16)
        for (sl, xn, rs), sz in zip(parts, szs):
            dyv = dmix[:, NA_W + sl.start:NA_W + sl.stop]
            zz = z[:, sl]
            w = nw_ref[:, sl]
            dz_ref[:, sl] = (dyv * xn * w * _dsilu(zz)).astype(BF16)
            gnw_ref[:, sl] += jnp.sum(dyv * xn * sz, axis=0, keepdims=True)
            dxm = dyv * w * sz
            dhm = rs * (dxm - jnp.mean(dxm, axis=-1, keepdims=True)
                        - xn * jnp.mean(dxm * xn, axis=-1, keepdims=True))
            so = jax.nn.sigmoid(ov[:, sl])
            dhs_ref[:, sl] = dhm * so
            do_ref[:, sl] = (dhm * hs[:, sl] * so * (1.0 - so)).astype(BF16)

    blk = lambda c: pl.BlockSpec((tm, 512), lambda i, c=c: (i, c))
    full = pl.BlockSpec((tm, d), lambda i: (i, 0))
    o512 = jax.ShapeDtypeStruct((t, 512), F32)
    b512 = jax.ShapeDtypeStruct((t, 512), BF16)
    whole = pl.BlockSpec((d, d), lambda i: (0, 0))
    return pl.pallas_call(
        body, name="tail", grid=(t // tm,),
        in_specs=[blk(0), blk(3), blk(0), blk(0), blk(7), blk(8), full, full, _row(d), _row(ML_W), _row(d), whole],
        out_specs=[pl.BlockSpec((1, 128), lambda i: (0, 0)), full] + [blk(0)] * 5
        + [_row(d), _row(d), _row(ML_W), whole],
        out_shape=[jax.ShapeDtypeStruct((1, 128), F32), jax.ShapeDtypeStruct((t, d), F32),
                   o512, b512, o512, b512, b512]
        + [jax.ShapeDtypeStruct((1, d), F32), jax.ShapeDtypeStruct((1, d), F32),
           jax.ShapeDtypeStruct((1, ML_W), F32), jax.ShapeDtypeStruct((d, d), F32)],
        scratch_shapes=[pltpu.VMEM((tm, d), BF16)],
        compiler_params=_cparams(("arbitrary",)),
    )(o_na, proj, h_f, h_b, proj, proj, x, target, gate, ml_norm_w, fnw, w_out_b)


def _in_bwd(pieces, x, dres, w_in_t, norm_w, scale, shift):
    t, d = x.shape
    tm = 512
    nt = t // tm
    widths = [p.shape[1] for p in pieces]
    offs = [sum(widths[:k]) for k in range(len(widths))]
    assert sum(widths) == IN_PAD
    npc = len(pieces)

    def body(*refs):
        p_refs = refs[:npc]
        (x_ref, dres_ref, w_hbm, nw_ref, sc_ref, sh_ref,
         gx_ref, gw_hbm, gb_ref, dsc_ref, dsh_ref, gnw_ref, w_vmem, acc, stage, sem) = refs[npc:]
        i = pl.program_id(0)

        @pl.when(i == 0)
        def _():
            cp = pltpu.make_async_copy(w_hbm, w_vmem, sem.at[0])
            cp.start()
            acc[...] = jnp.zeros_like(acc)
            gb_ref[...] = jnp.zeros_like(gb_ref)
            dsc_ref[...] = jnp.zeros_like(dsc_ref)
            dsh_ref[...] = jnp.zeros_like(dsh_ref)
            gnw_ref[...] = jnp.zeros_like(gnw_ref)
            cp.wait()

        nw = nw_ref[...]
        s1 = 1.0 + sc_ref[...]
        h, xn, r = _modulated_norm(x_ref[...], nw, sc_ref[...], sh_ref[...])
        hb = h.astype(BF16)
        dhv = jnp.zeros((tm, d), F32)
        for p_ref, c0, w in zip(p_refs, offs, widths):
            pt = p_ref[...]
            pb = pt.astype(BF16)
            dhv = dhv + _nn(pb, w_vmem[c0:c0 + w, :])
            acc[:, c0:c0 + w] += _tn(hb, pb)
            gb_ref[:, c0:c0 + w] += jnp.sum(pt.astype(F32), axis=0, keepdims=True)
        dsh_ref[...] += jnp.sum(dhv, axis=0, keepdims=True)
        dsc_ref[...] += jnp.sum(dhv * xn * nw, axis=0, keepdims=True)
        gnw_ref[...] += jnp.sum(dhv * xn * s1, axis=0, keepdims=True)
        dxn = dhv * nw * s1
        gx_ref[...] = dres_ref[...] + r * (dxn - xn * jnp.mean(dxn * xn, axis=-1, keepdims=True))

        @pl.when(i == nt - 1)
        def _():
            copies = []
            for blk in range(IN_PAD // 128):
                slot = blk % 2
                if blk >= 2:
                    copies[blk - 2].wait()
                stage[slot] = acc[:, blk * 128:(blk + 1) * 128].T
                cp = pltpu.make_async_copy(stage.at[slot], gw_hbm.at[pl.ds(blk * 128, 128), :], sem.at[1 + slot])
                cp.start()
                copies.append(cp)
            copies[-2].wait()
            copies[-1].wait()

    full = pl.BlockSpec((tm, d), lambda i: (i, 0))
    return pl.pallas_call(
        body, name="in_bwd", grid=(nt,),
        in_specs=[pl.BlockSpec((tm, w), lambda i: (i, 0)) for w in widths]
        + [full, full, pl.BlockSpec(memory_space=pl.ANY), _row(d), _row(d), _row(d)],
        out_specs=[full, pl.BlockSpec(memory_space=pl.ANY), _row(IN_PAD), _row(d), _row(d), _row(d)],
        out_shape=[jax.ShapeDtypeStruct((t, d), F32), jax.ShapeDtypeStruct((IN_PAD, d), F32),
                   jax.ShapeDtypeStruct((1, IN_PAD), F32)] + [jax.ShapeDtypeStruct((1, d), F32)] * 3,
        scratch_shapes=[pltpu.VMEM((IN_PAD, d), BF16), pltpu.VMEM((d, IN_PAD), F32),
                        pltpu.VMEM((2, 128, d), F32), pltpu.SemaphoreType.DMA((3,))],
        compiler_params=_cparams(("arbitrary",), IN_BWD_VMEM_LIMIT),
    )(*pieces, x, dres, w_in_t, norm_w, scale, shift)


def _na_static(rows):
    cases = [(0, 0), (NA_QROWS, NA_QROWS - 4), (rows - NA_QROWS, rows - NA_KROWS)]
    dy = np.zeros((3, NA_QROWS, NA_KROWS), np.int32)
    rv = np.zeros((3, NA_QROWS, NA_KROWS), bool)
    for cs, (r0, kr0) in enumerate(cases):
        for i in range(NA_QROWS):
            for j in range(NA_KROWS):
                r, kr = r0 + i, kr0 + j
                rs = min(max(r - NA_KH // 2, 0), rows - NA_KH)
                rv[cs, i, j] = rs <= kr <= rs + NA_KH - 1
                dy[cs, i, j] = min(max(kr - r + NA_KH - 1, 0), 2 * NA_KH - 2)
    cq = np.arange(GRID_W)[:, None]
    ck = np.arange(GRID_W)[None, :]
    cs0 = np.clip(cq - NA_KW // 2, 0, GRID_W - NA_KW)
    cv = (ck >= cs0) & (ck < cs0 + NA_KW)
    dx = np.clip(ck - cq, -(NA_KW - 1), NA_KW - 1) + NA_KW - 1
    return dy, rv, dx.astype(np.int32), cv


def _na_bias_table(rpb, rows):
    _, _, dx, cv = _na_static(rows)
    ndy = 2 * NA_KH - 1
    onehot = (dx.reshape(1, -1) == np.arange(2 * NA_KW - 1)[:, None]).astype(np.float32)
    rpx = jnp.dot(rpb.reshape(NA_HEADS * ndy, 2 * NA_KW - 1), jnp.asarray(onehot), precision=HI)
    rpx = jnp.where(cv[None, None], rpx.reshape(NA_HEADS, ndy, GRID_W, GRID_W), NEG)
    neg = jnp.full((NA_HEADS, 1, GRID_W, GRID_W), NEG, F32)
    rpx = jnp.concatenate([rpx, neg], axis=1)
    nxt = jnp.concatenate([rpx[:, 1:], neg], axis=1)
    negs = jnp.broadcast_to(neg, rpx.shape)
    pairs = jnp.concatenate([jnp.concatenate([rpx, nxt], axis=3), jnp.concatenate([rpx, negs], axis=3),
                             jnp.concatenate([negs, rpx], axis=3)], axis=1)
    npair = pairs.shape[1]

    def body(m_ref, o_ref):
        cs = pl.program_id(1)
        r0 = jnp.where(cs == 0, 0, jnp.where(cs == 1, NA_QROWS, rows - NA_QROWS))
        kr0 = jnp.where(cs == 0, 0, jnp.where(cs == 1, NA_QROWS - NA_KH // 2, rows - NA_KROWS))
        for i in range(NA_QROWS):
            r = r0 + i
            rs = jnp.clip(r - NA_KH // 2, 0, rows - NA_KH)
            for jp in range(NA_KROWS // 2):
                kl = kr0 + 2 * jp
                vl = (kl >= rs) & (kl <= rs + NA_KH - 1)
                vr = (kl + 1 >= rs) & (kl + 1 <= rs + NA_KH - 1)
                dyl = jnp.clip(kl - r + NA_KH - 1, 0, ndy)
                dyr = jnp.clip(kl + 1 - r + NA_KH - 1, 0, ndy)
                idx = jnp.where(vl & vr, dyl, jnp.where(vl, 16 + dyl, jnp.where(vr, 32 + dyr, 16 + ndy)))
                o_ref[0, 0, i * GRID_W:(i + 1) * GRID_W, jp * 128:(jp + 1) * 128] = m_ref[0, idx]

    return pl.pallas_call(
        body, name="na_bias_table", grid=(NA_HEADS, 3),
        in_specs=[pl.BlockSpec((1, npair, GRID_W, 128), lambda h, cs: (h, 0, 0, 0))],
        out_specs=pl.BlockSpec((1, 1, NA_QT, NA_KT), lambda h, cs: (h, cs, 0, 0)),
        out_shape=jax.ShapeDtypeStruct((NA_HEADS, 3, NA_QT, NA_KT), F32),
        compiler_params=_cparams(("parallel", "parallel")),
    )(pairs)


def _na_specs(t):
    nb = t // NA_QT
    nkb = t // NA_KCH
    npieces = NA_KT // NA_KCH

    def kb0(b):
        return jnp.clip(b * (NA_QT // NA_KCH) - 1, 0, nkb - npieces)

    def case(b):
        return jnp.where(b == 0, 0, jnp.where(b == nb - 1, 2, 1))

    q_spec = pl.BlockSpec((NA_QT, 128), lambda p, b: (b, p))
    k_specs = [pl.BlockSpec((NA_KCH, 128), lambda p, b, i=i: (kb0(b) + i, 4 + p)) for i in range(npieces)]
    v_specs = [pl.BlockSpec((NA_KCH, 128), lambda p, b, i=i: (kb0(b) + i, 8 + p)) for i in range(npieces)]
    tbl_spec = pl.BlockSpec((2, 1, NA_QT, NA_KT), lambda p, b: (p, case(b), 0, 0))
    io_spec = pl.BlockSpec((NA_QT, 128), lambda p, b: (b, p))
    return nb, npieces, kb0, case, q_spec, k_specs, v_specs, tbl_spec, io_spec


NA_HALF = NA_QT // 2
NA_COMBOS_ALL = tuple((i, 0, NA_QT) for i in range(NA_KT // NA_KCH))
NA_COMBOS_INNER = ((0, 0, NA_HALF),) + tuple((i, 0, NA_QT) for i in range(1, NA_KT // NA_KCH - 1)) \
    + ((NA_KT // NA_KCH - 1, NA_HALF, NA_QT),)


def _na_place(val, r0, r1):
    if (r0, r1) == (0, NA_QT):
        return val
    z = jnp.zeros((NA_HALF, val.shape[1]), val.dtype)
    return jnp.concatenate([val, z] if r0 == 0 else [z, val], axis=0)


def _na_fwd(proj, tbl):
    t = proj.shape[0]
    nb, npieces, _, _, q_spec, k_specs, v_specs, tbl_spec, io_spec = _na_specs(t)
    lse_spec = pl.BlockSpec((1, NA_QT, 2), lambda p, b: (p, b, 0))

    def body(*refs):
        q_ref = refs[0]
        k_refs = refs[1:1 + npieces]
        v_refs = refs[1 + npieces:1 + 2 * npieces]
        tbl_ref, o_ref, lse_ref = refs[1 + 2 * npieces:]
        b = pl.program_id(1)

        def compute(combos):
            lane = lax.broadcasted_iota(jnp.int32, (1, 128), 1)
            qv = q_ref[...].astype(F32) * (NA_HEAD_DIM ** -0.5)
            ks = [r[...].astype(BF16) for r in k_refs]
            vs = [r[...].astype(BF16) for r in v_refs]
            hs = range(2)
            msk = [(lane // NA_HEAD_DIM) == hh for hh in hs]
            qh = [jnp.where(msk[hh], qv, 0.0).astype(BF16) for hh in hs]
            s = [[_nt(qh[hh][r0:r1], ks[i]) + tbl_ref[hh, 0, r0:r1, i * NA_KCH:(i + 1) * NA_KCH]
                  for i, r0, r1 in combos] for hh in hs]
            for h0 in (0, NA_HALF):
                rows = slice(h0, h0 + NA_HALF)
                cover = [(c, i, h0 - r0) for c, (i, r0, r1) in enumerate(combos) if r0 <= h0 < r1]
                part = [[s[hh][c][off:off + NA_HALF] for c, _, off in cover] for hh in hs]
                m = [functools.reduce(jnp.maximum, [jnp.max(v, axis=1, keepdims=True) for v in part[hh]]) for hh in hs]
                p = [[jnp.exp(v - m[hh]) for v in part[hh]] for hh in hs]
                l = [functools.reduce(jnp.add, [jnp.sum(v, axis=1, keepdims=True) for v in p[hh]]) for hh in hs]
                o = [functools.reduce(jnp.add, [_nn(p[hh][k].astype(BF16), vs[i]) for k, (_, i, _) in enumerate(cover)])
                     for hh in hs]
                for hh in hs:
                    lse_ref[0, rows, hh:hh + 1] = m[hh] + jnp.log(l[hh])
                o_ref[rows, :] = jnp.where(msk[0], o[0] / l[0], o[1] / l[1])

        inner = (b > 0) & (b < nb - 1)
        pl.when(inner)(lambda: compute(NA_COMBOS_INNER))
        pl.when(jnp.logical_not(inner))(lambda: compute(NA_COMBOS_ALL))

    return pl.pallas_call(
        body, name="na_fwd", grid=(4, nb),
        in_specs=[q_spec] + k_specs + v_specs + [tbl_spec],
        out_specs=[io_spec, lse_spec],
        out_shape=[jax.ShapeDtypeStruct((t, NA_W), F32), jax.ShapeDtypeStruct((4, t, 2), F32)],
        compiler_params=_cparams(("parallel", "arbitrary")),
    )(*([proj] * (1 + 2 * npieces)), tbl)


def _na_bwd(proj, tbl, d_o, o_na, lse):
    t = proj.shape[0]
    nb, npieces, kb0, case, q_spec, k_specs, v_specs, tbl_spec, io_spec = _na_specs(t)

    def body(*refs):
        q_ref = refs[0]
        k_refs = refs[1:1 + npieces]
        v_refs = refs[1 + npieces:1 + 2 * npieces]
        (tbl_ref, do_ref, o_ref, lse_ref, dq_ref, dk_hbm, dv_hbm, rpb_ref,
         dk_acc, dv_acc, dk_out, dv_out, s_scr, dp_scr, dsb_scr, pnb_scr, sem) = refs[1 + 2 * npieces:]
        p_id = pl.program_id(0)
        b = pl.program_id(1)

        @pl.when(b == 0)
        def _():
            dk_acc[...] = jnp.zeros_like(dk_acc)
            dv_acc[...] = jnp.zeros_like(dv_acc)

        @pl.when((b == 0) | (b == 1) | (b == nb - 1))
        def _():
            rpb_ref[...] = jnp.zeros_like(rpb_ref)

        def compute(combos):
            lane = lax.broadcasted_iota(jnp.int32, (1, 128), 1)
            scale = NA_HEAD_DIM ** -0.5
            qv = q_ref[...].astype(F32) * scale
            ks = [r[...].astype(BF16) for r in k_refs]
            vs = [r[...].astype(BF16) for r in v_refs]
            dov = do_ref[...]
            ov = o_ref[...]
            tok0 = kb0(b) * NA_KCH
            hs = range(2)
            msk = [(lane // NA_HEAD_DIM) == hh for hh in hs]
            qh = [jnp.where(msk[hh], qv, 0.0).astype(BF16) for hh in hs]
            doh = [jnp.where(msk[hh], dov, 0.0) for hh in hs]
            dohb = [doh[hh].astype(BF16) for hh in hs]
            dd = [jnp.sum(doh[hh] * ov, axis=1, keepdims=True) for hh in hs]
            for hh in hs:
                for c, (i, q0, q1) in enumerate(combos):
                    slot = (hh * len(combos) + c) % 2
                    cols = slice(i * NA_KCH, (i + 1) * NA_KCH)
                    s_scr[slot, 0:q1 - q0] = _nt(qh[hh][q0:q1], ks[i])
                    dp_scr[slot, 0:q1 - q0] = _nt(dohb[hh][q0:q1], vs[i])
                    for r0 in range(q0, q1, NA_RC):
                        rows = slice(r0, r0 + NA_RC)
                        loc = slice(r0 - q0, r0 - q0 + NA_RC)
                        p = jnp.exp(s_scr[slot, loc, :] + tbl_ref[hh, 0, rows, cols] - lse_ref[0, rows, hh:hh + 1])
                        d = p * (dp_scr[slot, loc, :] - dd[hh][rows])
                        pnb_scr[hh, rows, cols] = p.astype(BF16)
                        dsb_scr[hh, rows, cols] = d.astype(BF16)
                done = {(i, q0) for i, q0, _ in combos} | {(i, NA_HALF) for i, q0, q1 in combos if q1 - q0 == NA_QT}
                for i in range(npieces):
                    for q0 in (0, NA_HALF):
                        if (i, q0) not in done:
                            dsb_scr[hh, q0:q0 + NA_HALF, i * NA_KCH:(i + 1) * NA_KCH] = jnp.zeros(
                                (NA_HALF, NA_KCH), BF16)
            dqh = [functools.reduce(jnp.add, [_na_place(_nn(dsb_scr[hh, q0:q1, i * NA_KCH:(i + 1) * NA_KCH], ks[i]),
                                                        q0, q1) for i, q0, q1 in combos]) for hh in hs]
            dq_ref[...] = (jnp.where(msk[0], dqh[0], dqh[1]) * scale).astype(BF16)
            for i in range(npieces):
                rows = pl.ds(pl.multiple_of(tok0 + i * NA_KCH, NA_KCH), NA_KCH)
                cols = slice(i * NA_KCH, (i + 1) * NA_KCH)
                q0, q1 = [(a, e) for j, a, e in combos if j == i][0]
                dk_acc[rows, :] += (_tn(dsb_scr[0, q0:q1, cols], qh[0][q0:q1])
                                    + _tn(dsb_scr[1, q0:q1, cols], qh[1][q0:q1]))
                dv_acc[rows, :] += (_tn(pnb_scr[0, q0:q1, cols], dohb[0][q0:q1])
                                    + _tn(pnb_scr[1, q0:q1, cols], dohb[1][q0:q1]))
            for hh in hs:
                acc = dsb_scr[hh, 0:GRID_W, :].astype(F32)
                for i in range(1, NA_QROWS):
                    acc = acc + pltpu.roll(dsb_scr[hh, i * GRID_W:(i + 1) * GRID_W, :].astype(F32),
                                           NA_KT - i * GRID_W, 1)
                rpb_ref[0, 0, hh] += acc

        inner = (b > 0) & (b < nb - 1)
        pl.when(inner)(lambda: compute(NA_COMBOS_INNER))
        pl.when(jnp.logical_not(inner))(lambda: compute(NA_COMBOS_ALL))

        @pl.when(b == nb - 1)
        def _():
            cols = pl.ds(pl.multiple_of(p_id * 128, 128), 128)
            dk_out[...] = dk_acc[...].astype(BF16)
            dv_out[...] = dv_acc[...].astype(BF16)
            ck = pltpu.make_async_copy(dk_out, dk_hbm.at[:, cols], sem.at[0])
            cv = pltpu.make_async_copy(dv_out, dv_hbm.at[:, cols], sem.at[1])
            ck.start()
            cv.start()
            ck.wait()
            cv.wait()

    o512 = jax.ShapeDtypeStruct((t, NA_W), BF16)
    return pl.pallas_call(
        body, name="na_bwd", grid=(4, nb),
        in_specs=[q_spec] + k_specs + v_specs + [tbl_spec, io_spec, io_spec,
                                                 pl.BlockSpec((1, NA_QT, 2), lambda p, b: (p, b, 0))],
        out_specs=[io_spec, pl.BlockSpec(memory_space=pl.ANY), pl.BlockSpec(memory_space=pl.ANY),
                   pl.BlockSpec((1, 1, 2, GRID_W, NA_KT), lambda p, b: (p, case(b), 0, 0, 0))],
        out_shape=[o512, o512, o512, jax.ShapeDtypeStruct((4, 3, 2, GRID_W, NA_KT), F32)],
        scratch_shapes=[pltpu.VMEM((t, 128), F32), pltpu.VMEM((t, 128), F32),
                        pltpu.VMEM((t, 128), BF16), pltpu.VMEM((t, 128), BF16),
                        pltpu.VMEM((2, NA_QT, NA_KCH), F32), pltpu.VMEM((2, NA_QT, NA_KCH), F32),
                        pltpu.VMEM((2, NA_QT, NA_KT), BF16), pltpu.VMEM((2, NA_QT, NA_KT), BF16),
                        pltpu.SemaphoreType.DMA((2,))],
        compiler_params=_cparams(("arbitrary", "arbitrary")),
    )(*([proj] * (1 + 2 * npieces)), tbl, d_o, o_na, lse)


def _rpb_reduce(rpbacc, rows):
    nacc = 4 * 3 * 2

    def shift_body(a_ref, o_ref):
        acc = a_ref[0, 0:1, :]
        for cq in range(1, GRID_W):
            acc = acc + pltpu.roll(a_ref[0, cq:cq + 1, :], NA_KT - cq, 1)
        o_ref[0] = jnp.broadcast_to(acc, (8, NA_KT))

    vec = pl.pallas_call(
        shift_body, name="rpb_shift", grid=(nacc,),
        in_specs=[pl.BlockSpec((1, GRID_W, NA_KT), lambda a: (a, 0, 0))],
        out_specs=pl.BlockSpec((1, 8, NA_KT), lambda a: (a, 0, 0)),
        out_shape=jax.ShapeDtypeStruct((nacc, 8, NA_KT), F32),
        compiler_params=_cparams(("parallel",)),
    )(rpbacc.reshape(nacc, GRID_W, NA_KT))
    a = vec[:, 0].reshape(4, 3, 2, NA_KT).transpose(0, 2, 1, 3).reshape(NA_HEADS, 3, NA_KT)
    if rows // NA_QROWS < 3:
        a = a.at[:, 1].set(0.0)
    dd = np.arange(NA_KROWS)[:, None]
    dxo = np.arange(-(NA_KW - 1), NA_KW)[None, :]
    idx = ((dd * GRID_W + dxo) % NA_KT).reshape(-1)
    g = a[..., idx].reshape(NA_HEADS, 3 * NA_KROWS, 2 * NA_KW - 1)
    g = jnp.pad(g, ((0, 0), (0, 0), (0, 128 - (2 * NA_KW - 1))))
    nmat = np.zeros((16, 3 * NA_KROWS), np.float32)
    for cs, delta in enumerate((0, -(NA_KH // 2), -(NA_KROWS - NA_QROWS))):
        for d in range(NA_KROWS):
            jmi = d - NA_KROWS if (cs == 0 and d > NA_KH - 1) else d
            dy = jmi + delta + NA_KH - 1
            if 0 <= dy <= 2 * NA_KH - 2:
                nmat[dy, cs * NA_KROWS + d] = 1.0

    def body(n_ref, g_ref, o_ref):
        o_ref[0] = jnp.dot(n_ref[...], g_ref[0], precision=HI, preferred_element_type=F32)

    out = pl.pallas_call(
        body, name="rpb_reduce", grid=(NA_HEADS,),
        in_specs=[pl.BlockSpec((16, nmat.shape[1]), lambda h: (0, 0)),
                  pl.BlockSpec((1, nmat.shape[1], 128), lambda h: (h, 0, 0))],
        out_specs=pl.BlockSpec((1, 16, 128), lambda h: (h, 0, 0)),
        out_shape=jax.ShapeDtypeStruct((NA_HEADS, 16, 128), F32),
        compiler_params=_cparams(("parallel",)),
    )(jnp.asarray(nmat), g)
    return out[:, :2 * NA_KH - 1, :2 * NA_KW - 1]


def _halo_specs(tm, t, col, width=1024):
    nth = t // CONV_HALO
    per = tm // CONV_HALO
    return [pl.BlockSpec((tm, width), lambda i: (i, col)),
            pl.BlockSpec((CONV_HALO, width), lambda i: (jnp.maximum(i * per - 1, 0), col)),
            pl.BlockSpec((CONV_HALO, width), lambda i: (jnp.minimum((i + 1) * per, nth - 1), col))]


def _fill_ext(ext, cur_ref, prev_ref, next_ref, tm, nt):
    i = pl.program_id(0)
    hl = CONV_HALO
    ext[0:hl, :] = jnp.where(i == 0, 0.0, prev_ref[...].astype(F32))
    ext[hl:hl + tm, :] = cur_ref[...].astype(F32)
    ext[hl + tm:2 * hl + tm, :] = jnp.where(i == nt - 1, 0.0, next_ref[...].astype(F32))


CONV_HALO = 16
CONV_RC = 16
CONV_CB = 512


def _conv_chunks(tm):
    return [(slice(cb, cb + CONV_CB), slice(rb, rb + CONV_RC))
            for cb in range(0, 1024, CONV_CB) for rb in range(0, tm, CONV_RC)]


def _conv_fwd(proj, conv_w8, conv_b, tm):
    t = proj.shape[0]
    nt = t // tm

    def body(u_ref, up_ref, un_ref, w_ref, b_ref, pre_ref, act_ref, ext):
        _fill_ext(ext, u_ref, up_ref, un_ref, tm, nt)
        for cs, rs in _conv_chunks(tm):
            pre = b_ref[:, cs] + w_ref[0:1, cs] * ext[pl.ds(rs.start + CONV_HALO - 2, CONV_RC), cs]
            for j in range(1, CONV_W):
                pre = pre + w_ref[j:j + 1, cs] * ext[pl.ds(rs.start + CONV_HALO - 2 + j, CONV_RC), cs]
            pre_ref[rs, cs] = pre
            act_ref[rs, cs] = _silu(pre)

    full = pl.BlockSpec((tm, 1024), lambda i: (i, 0))
    o = jax.ShapeDtypeStruct((t, 1024), F32)
    return pl.pallas_call(
        body, name="conv_fwd", grid=(nt,),
        in_specs=_halo_specs(tm, t, 2) + [pl.BlockSpec((8, 1024), lambda i: (0, 0)), _row(1024)],
        out_specs=[full, full], out_shape=[o, o],
        scratch_shapes=[pltpu.VMEM((tm + 2 * CONV_HALO, 1024), F32)],
        compiler_params=_cparams(("parallel",)),
    )(proj, proj, proj, conv_w8, conv_b)


def _conv_bwd(dq, dk, pre, proj, conv_w8, tm):
    t = pre.shape[0]
    nt = t // tm

    def body(dq_ref, dqp_ref, dqn_ref, dk_ref, dkp_ref, dkn_ref, pre_ref, prep_ref, pren_ref,
             u_ref, up_ref, un_ref, w_ref, du_ref, gw_ref, gb_ref, extd, extu):
        i = pl.program_id(0)
        hl = CONV_HALO

        @pl.when(i == 0)
        def _():
            gw_ref[...] = jnp.zeros_like(gw_ref)
            gb_ref[...] = jnp.zeros_like(gb_ref)
        for rows, dqr, dkr, prr, edge in ((slice(0, hl), dqp_ref, dkp_ref, prep_ref, i == 0),
                                          (slice(hl, hl + tm), dq_ref, dk_ref, pre_ref, None),
                                          (slice(hl + tm, 2 * hl + tm), dqn_ref, dkn_ref, pren_ref, i == nt - 1)):
            ds = _dsilu(prr[...])
            dl = dqr[...] * ds[:, 0:ML_W]
            dr = dkr[...] * ds[:, ML_W:]
            if edge is not None:
                dl = jnp.where(edge, 0.0, dl)
                dr = jnp.where(edge, 0.0, dr)
            extd[rows, 0:ML_W] = dl
            extd[rows, ML_W:] = dr
        _fill_ext(extu, u_ref, up_ref, un_ref, tm, nt)
        gb_ref[...] += jnp.sum(extd[hl:hl + tm, :], axis=0, keepdims=True)
        gacc = None
        for cs, rs in _conv_chunks(tm):
            if rs.start == 0:
                gacc = [jnp.zeros((8, CONV_CB), F32) for _ in range(CONV_W)]
            du = w_ref[0:1, cs] * extd[pl.ds(rs.start + hl + 2, CONV_RC), cs]
            for j in range(1, CONV_W):
                du = du + w_ref[j:j + 1, cs] * extd[pl.ds(rs.start + hl + 2 - j, CONV_RC), cs]
            du_ref[rs, cs] = du.astype(BF16)
            dcur = extd[pl.ds(rs.start + hl, CONV_RC), cs]
            for j in range(CONV_W):
                prod = dcur * extu[pl.ds(rs.start + hl - 2 + j, CONV_RC), cs]
                gacc[j] = gacc[j] + functools.reduce(
                    jnp.add, [prod[k:k + 8] for k in range(0, CONV_RC, 8)])
            if rs.stop == tm:
                for j in range(CONV_W):
                    gw_ref[j:j + 1, cs] += jnp.sum(gacc[j], axis=0, keepdims=True)

    full = pl.BlockSpec((tm, 1024), lambda i: (i, 0))
    return pl.pallas_call(
        body, name="conv_bwd", grid=(nt,),
        in_specs=_halo_specs(tm, t, 0, ML_W) + _halo_specs(tm, t, 0, ML_W) + _halo_specs(tm, t, 0)
        + _halo_specs(tm, t, 2) + [pl.BlockSpec((8, 1024), lambda i: (0, 0))],
        out_specs=[full, pl.BlockSpec((8, 1024), lambda i: (0, 0)), _row(1024)],
        out_shape=[jax.ShapeDtypeStruct((t, 1024), BF16), jax.ShapeDtypeStruct((8, 1024), F32),
                   jax.ShapeDtypeStruct((1, 1024), F32)],
        scratch_shapes=[pltpu.VMEM((tm + 2 * CONV_HALO, 1024), F32), pltpu.VMEM((tm + 2 * CONV_HALO, 1024), F32)],
        compiler_params=_cparams(("arbitrary",)),
    )(dq, dq, dq, dk, dk, dk, pre, pre, pre, proj, proj, proj, conv_w8)


def _ml_consts(rev):
    iu = lax.broadcasted_iota(jnp.int32, (ML_CHUNK, ML_CHUNK), 0)
    js = lax.broadcasted_iota(jnp.int32, (ML_CHUNK, ML_CHUNK), 1)
    eye = iu == js
    le = iu <= js
    ge = iu >= js
    csum, csum_t, sees = (ge, le, ge) if rev else (le, ge, le)
    return eye, csum.astype(F32), csum_t.astype(F32), sees


def _col(row, eye):
    return jnp.sum(jnp.where(eye, row, 0.0), axis=1, keepdims=True)


def _rowof(col, eye):
    return jnp.sum(jnp.where(eye, col, 0.0), axis=0, keepdims=True)


def _row8(row):
    top = lax.broadcasted_iota(jnp.int32, (8, row.shape[1]), 0) == 0
    return jnp.where(top, row, jnp.zeros_like(row))


def _outer_rows(a_row, b_row_bf16):
    hi = a_row.astype(BF16)
    lo = (a_row - hi.astype(F32)).astype(BF16)
    r_a = lax.broadcasted_iota(jnp.int32, (8, a_row.shape[1]), 0)
    r_b = lax.broadcasted_iota(jnp.int32, (8, b_row_bf16.shape[1]), 0)
    lhs = jnp.where(r_a == 0, hi, jnp.where(r_a == 1, lo, jnp.zeros_like(hi)))
    rhs = jnp.where(r_b < 2, b_row_bf16, jnp.zeros_like(b_row_bf16))
    return _tn(lhs, rhs)


def _ml_gates(gi, gf, m0, csum, rev):
    lf = jax.nn.log_sigmoid(gf)
    b_rows = jnp.dot(lf, csum, precision=HI, preferred_element_type=F32)
    bl = jnp.sum(lf, axis=1, keepdims=True)
    a_rows = bl - b_rows + gi
    mloc = jnp.max(a_rows, axis=1, keepdims=True)
    order = list(range(ML_NB))[::-1] if rev else list(range(ML_NB))
    mp, mn, decay = {}, {}, {}
    m = m0
    for n in order:
        mp[n] = m
        m = jnp.maximum(bl[n:n + 1] + m, mloc[n:n + 1])
        mn[n] = m
    for n in order:
        decay[n] = jnp.exp(bl[n:n + 1] + mp[n] - mn[n])
    return b_rows, a_rows, gi - b_rows, mp, mn, decay, order


def _ml_load(q_ref, k_ref, v_ref, n):
    sl = slice(n * ML_CHUNK, (n + 1) * ML_CHUNK)
    qb = q_ref[sl, :].astype(BF16)
    kb = (k_ref[sl, :] * (ML_HEAD_DIM ** -0.5)).astype(BF16)
    vn = v_ref[sl, :].astype(F32)
    return sl, qb, kb, vn


def _ml_state_scan(q_ref, k_ref, v_ref, a_rows, mn, decay, order, c0, n0):
    ns = range(ML_NB)
    ld = [_ml_load(q_ref, k_ref, v_ref, n) for n in ns]
    vt = [ld[n][3].T for n in ns]
    w_row = [jnp.exp(a_rows[n:n + 1] - mn[n]) for n in ns]
    u = [_nn((vt[n] * w_row[n]).astype(BF16), ld[n][2]) for n in ns]
    nu = [_nn(_row8(w_row[n]).astype(BF16), ld[n][2])[0:1] for n in ns]
    cp, npv = {}, {}
    c, nv = c0, n0
    for n in order:
        cp[n], npv[n] = c, nv
        c = decay[n] * c + u[n]
        nv = decay[n] * nv + nu[n]
    return ld, vt, cp, npv, w_row, c, nv


def _ml_intra_all(ld, vt, b_rows, imb_rows, mp, cp, npv, sees, eye):
    ns = range(ML_NB)
    qk = [_nt(ld[n][2], ld[n][1]) for n in ns]
    cq = [_nt(cp[n].astype(BF16), ld[n][1]) for n in ns]
    qn = [_nt(_row8(npv[n]).astype(BF16), ld[n][1])[0:1] for n in ns]
    imb_col = [_col(imb_rows[n:n + 1], eye) for n in ns]
    dlog = [jnp.where(sees, b_rows[n:n + 1] + imb_col[n], NEG) for n in ns]
    m_inter = [b_rows[n:n + 1] + mp[n] for n in ns]
    m_t = [jnp.maximum(m_inter[n], jnp.max(dlog[n], axis=0, keepdims=True)) for n in ns]
    pm = [jnp.exp(dlog[n] - m_t[n]) for n in ns]
    inter = [jnp.exp(m_inter[n] - m_t[n]) for n in ns]
    floor = [jnp.exp(-m_t[n]) for n in ns]
    s = [qk[n] * pm[n] for n in ns]
    sv = [_nn(vt[n].astype(BF16), s[n].astype(BF16)) for n in ns]
    den = [jnp.sum(s[n], axis=0, keepdims=True) + inter[n] * qn[n] for n in ns]
    num = [sv[n] + inter[n] * cq[n] for n in ns]
    dn = [jnp.maximum(jnp.abs(den[n]), floor[n]) for n in ns]
    return [dict(pm=pm[n], s=s[n], inter=inter[n], cq=cq[n], qn=qn[n], num=num[n], den=den[n],
                 floor=floor[n], dn=dn[n]) for n in ns]


def _ml_specs(t, rev):
    nblk = t // ML_TB
    blk = (lambda g: nblk - 1 - g) if rev else (lambda g: g)
    hps = ML_HPS
    tile = lambda c0: pl.BlockSpec((ML_TB, 128 * hps), lambda hg, g, c0=c0: (blk(g), c0 // hps + hg))
    gate = pl.BlockSpec((hps, ML_NB, ML_CHUNK), lambda hg, g: (hg, blk(g), 0))
    cchk = pl.BlockSpec((hps, 1, 128, 128), lambda hg, g: (hg, blk(g), 0, 0))
    nmchk = pl.BlockSpec((hps, 1, 8, 128), lambda hg, g: (hg, blk(g), 0, 0))
    return nblk, blk, tile, gate, cchk, nmchk


def _ml_head_views(refs, hh):
    cols = slice(hh * ML_HEAD_DIM, (hh + 1) * ML_HEAD_DIM)
    return [r.at[:, cols] if len(r.shape) == 2 else r.at[hh] for r in refs]


def _ml_fwd(qk_act, proj, gi, gf, rev, name):
    t = qk_act.shape[0]
    nblk, _, tile, gate, cchk, nmchk = _ml_specs(t, rev)

    def body(*refs):
        for hh in range(ML_HPS):
            one_head(*_ml_head_views(refs, hh))

    def one_head(q_ref, k_ref, v_ref, gi_ref, gf_ref, h_ref, cchk_ref, nmchk_ref, c_ref, nm_ref):
        @pl.when(pl.program_id(1) == 0)
        def _():
            c_ref[...] = jnp.zeros_like(c_ref)
            nm_ref[...] = jnp.zeros_like(nm_ref)
        cchk_ref[0] = c_ref[...]
        nmchk_ref[0] = nm_ref[...]
        eye, csum, _, sees = _ml_consts(rev)
        b_rows, a_rows, imb_rows, mp, mn, decay, order = _ml_gates(
            gi_ref[...], gf_ref[...], nm_ref[1:2, 0:1], csum, rev)
        ld, vt, cp, npv, _, c, nv = _ml_state_scan(q_ref, k_ref, v_ref, a_rows, mn, decay, order,
                                                   c_ref[...], nm_ref[0:1, :])
        c_ref[...] = c
        nm_ref[0:1, :] = nv
        nm_ref[1:2, :] = jnp.broadcast_to(mn[order[-1]], (1, 128))
        rs = _ml_intra_all(ld, vt, b_rows, imb_rows, mp, cp, npv, sees, eye)
        ht = [rs[n]['num'] / rs[n]['dn'] for n in range(ML_NB)]
        for n in range(ML_NB):
            h_ref[n * ML_CHUNK:(n + 1) * ML_CHUNK, :] = ht[n].T

    return pl.pallas_call(
        body, name=name, grid=(ML_HEADS // ML_HPS, nblk),
        in_specs=[tile(0), tile(4), tile(24), gate, gate],
        out_specs=[tile(0), cchk, nmchk],
        out_shape=[jax.ShapeDtypeStruct((t, ML_W), F32),
                   jax.ShapeDtypeStruct((ML_HEADS, nblk, 128, 128), F32),
                   jax.ShapeDtypeStruct((ML_HEADS, nblk, 8, 128), F32)],
        scratch_shapes=[pltpu.VMEM((ML_HPS, 128, 128), F32), pltpu.VMEM((ML_HPS, 8, 128), F32)],
        compiler_params=_cparams(("parallel", "arbitrary")),
    )(qk_act, qk_act, proj, gi, gf)


def _ml_bwd(qk_act, proj, gi, gf, dh, cchk_a, nmchk_a, prev, rev, name):
    t = qk_act.shape[0]
    nblk, _, tile, gate, cchk, nmchk = _ml_specs(t, not rev)

    def body(*refs):
        for hh in range(ML_HPS):
            one_head(*_ml_head_views(refs, hh))

    def one_head(q_ref, k_ref, v_ref, gi_ref, gf_ref, dh_ref, cchk_ref, nmchk_ref, *rest):
        prev_refs = rest[:len(prev)]
        dq_ref, dk_ref, dv_ref, dgi_ref, dgf_ref, dc_ref, dn_ref, db_scr, dbl_scr, di_scr = rest[len(prev):]

        def plus_prev(val, which, rows):
            return val + prev_refs[which][rows, :] if prev else val

        @pl.when(pl.program_id(1) == 0)
        def _():
            dc_ref[...] = jnp.zeros_like(dc_ref)
            dn_ref[...] = jnp.zeros_like(dn_ref)
        eye, csum, csum_t, sees = _ml_consts(rev)
        gfv = gf_ref[...]
        b_rows, a_rows, imb_rows, mp, mn, decay, order = _ml_gates(
            gi_ref[...], gfv, nmchk_ref[0, 1:2, 0:1], csum, rev)
        ld, vt, cp, npv, w_row, _, _ = _ml_state_scan(q_ref, k_ref, v_ref, a_rows, mn, decay, order,
                                                      cchk_ref[0], nmchk_ref[0, 0:1, :])
        ns = range(ML_NB)
        rs = _ml_intra_all(ld, vt, b_rows, imb_rows, mp, cp, npv, sees, eye)
        sls = [ld[n][0] for n in ns]
        qbs = [ld[n][1] for n in ns]
        kbs = [ld[n][2] for n in ns]
        vbs = [ld[n][3].astype(BF16) for n in ns]
        rdn = [1.0 / rs[n]['dn'] for n in ns]
        dnum = [dh_ref[sls[n], :].T * rdn[n] for n in ns]
        hsum = [jnp.sum(dnum[n] * rs[n]['num'], axis=0, keepdims=True) for n in ns]
        dden = [jnp.where(jnp.abs(rs[n]['den']) > rs[n]['floor'],
                          -hsum[n] * rdn[n] * jnp.sign(rs[n]['den']), 0.0) for n in ns]
        dnb = [dnum[n].astype(BF16) for n in ns]
        dsf = [_nn(vbs[n], dnb[n]) + dden[n] for n in ns]
        dv0 = [_nt(rs[n]['s'].astype(BF16), dnb[n]) for n in ns]
        gb = [(dsf[n] * rs[n]['pm']).astype(BF16) for n in ns]
        cpb = [cp[n].astype(BF16) for n in ns]
        idd = [rs[n]['inter'] * dden[n] for n in ns]
        idn = [(rs[n]['inter'] * dnum[n]).astype(BF16) for n in ns]
        dqa = [_tn(gb[n], kbs[n]) for n in ns]
        dqc = [_tn(idn[n], cpb[n]) for n in ns]
        dqn = [_outer_rows(idd[n], npv[n].astype(BF16)) for n in ns]
        dk0 = [_nn(gb[n], qbs[n]) for n in ns]
        xs = [_nn(idn[n], qbs[n]) for n in ns]
        for n in ns:
            dq_ref[sls[n], :] = plus_prev(dqa[n] + dqc[n] + dqn[n], 0, sls[n])
        rr = [dsf[n] * rs[n]['s'] for n in ns]
        dinter = [jnp.sum(dnum[n] * rs[n]['cq'], axis=0, keepdims=True) + dden[n] * rs[n]['qn'] for n in ns]
        dbt = [jnp.sum(rr[n], axis=0, keepdims=True) + dinter[n] * rs[n]['inter'] for n in ns]
        dimb = [jnp.sum(rr[n], axis=1, keepdims=True) for n in ns]
        xns = [_nn(_row8(idd[n]).astype(BF16), qbs[n])[0:1] for n in ns]
        dcn, dnn = {}, {}
        dc, dn = dc_ref[...], dn_ref[0:1, :]
        for n in order[::-1]:
            dcn[n], dnn[n] = dc, dn
            dc = decay[n] * dc + xs[n]
            dn = decay[n] * dn + xns[n]
        dc_ref[...] = dc
        dn_ref[0:1, :] = dn
        kscale = ML_HEAD_DIM ** -0.5
        dcb = [dcn[n].astype(BF16) for n in ns]
        z = [_nn(vbs[n], dcb[n]) for n in ns]
        kd = [_nt(kbs[n], dcb[n]) for n in ns]
        ddecay = [jnp.sum(jnp.sum(dcn[n] * cp[n], axis=1, keepdims=True), axis=0, keepdims=True)
                  + jnp.sum(dnn[n] * npv[n], axis=1, keepdims=True) for n in ns]
        zd = [z[n] + dnn[n] for n in ns]
        dw = [jnp.sum(zd[n] * kbs[n].astype(F32), axis=1, keepdims=True) for n in ns]
        wcol = [_col(w_row[n], eye) for n in ns]
        for n in ns:
            dv_ref[sls[n], :] = plus_prev(dv0[n] + wcol[n] * kd[n], 2, sls[n]).astype(dv_ref.dtype)
            dk_ref[sls[n], :] = plus_prev((dk0[n] + wcol[n] * zd[n]) * kscale, 1, sls[n])
        da = [dw[n] * wcol[n] for n in ns]
        dbl = [jnp.sum(da[n], axis=0, keepdims=True) + ddecay[n] * decay[n] for n in ns]
        key_row = [_rowof(dimb[n] + da[n], eye) for n in ns]
        for n in ns:
            db_scr[n:n + 1, :] = dbt[n] - key_row[n]
            di_scr[n:n + 1, :] = key_row[n]
            dbl_scr[n:n + 1, :] = jnp.broadcast_to(dbl[n], (1, ML_CHUNK))
        dlf = jnp.dot(db_scr[...], csum_t, precision=HI, preferred_element_type=F32) + dbl_scr[...]
        dgf_ref[...] = dlf * jax.nn.sigmoid(-gfv)
        dgi_ref[...] = di_scr[...]

    nc = t // ML_CHUNK
    o512 = jax.ShapeDtypeStruct((t, ML_W), F32)
    og = jax.ShapeDtypeStruct((ML_HEADS, nc, ML_CHUNK), F32)
    return pl.pallas_call(
        body, name=name, grid=(ML_HEADS // ML_HPS, nblk),
        in_specs=[tile(0), tile(4), tile(24), gate, gate, tile(0), cchk, nmchk] + [tile(0)] * len(prev),
        out_specs=[tile(0), tile(0), tile(0), gate, gate],
        out_shape=[o512, o512, jax.ShapeDtypeStruct((t, ML_W), BF16 if prev else F32), og, og],
        scratch_shapes=[pltpu.VMEM((ML_HPS, 128, 128), F32), pltpu.VMEM((ML_HPS, 8, 128), F32)]
        + [pltpu.VMEM((ML_HPS, ML_NB, ML_CHUNK), F32)] * 3,
        compiler_params=_cparams(("parallel", "arbitrary")),
    )(qk_act, qk_act, proj, gi, gf, dh, cchk_a, nmchk_a, *prev)


def _gate_rows(gates16, t):
    g = gates16.reshape(t // ML_CHUNK, ML_CHUNK, 4, ML_HEADS).transpose(2, 3, 0, 1)
    return g[0], g[1], g[2], g[3]


def _gate_cols(dgi_f, dgf_f, dgi_b, dgf_b, t):
    g = jnp.stack([dgi_f, dgf_f, dgi_b, dgf_b]).transpose(2, 3, 0, 1).reshape(t, 4 * ML_HEADS)
    return jnp.pad(g, ((0, 0), (0, 128 - 4 * ML_HEADS)))


def _local_step(x, target, shift, scale, gate, norm_w, w_in_t, b_in_p, conv_w8, conv_b, rpb,
                ml_norm_w, w_out_b, final_norm_w):
    t = x.shape[0]
    rows = t // GRID_W
    tm = 512
    proj, gates = _in_proj(x, norm_w, scale, shift, w_in_t, b_in_p)
    tbl = _na_bias_table(rpb, rows)
    o_na, lse_na = _na_fwd(proj, tbl)
    pre, qk_act = _conv_fwd(proj, conv_w8, conv_b, tm)
    gi_f, gf_f, gi_b, gf_b = _gate_rows(gates[:, :4 * ML_HEADS], t)
    h_f, cchk_f, nmchk_f = _ml_fwd(qk_act, proj, gi_f, gf_f, False, "ml_fwd_f")
    h_b, cchk_b, nmchk_b = _ml_fwd(qk_act, proj, gi_b, gf_b, True, "ml_fwd_b")
    (loss, dres, d_ona, d_naz, dhs, d_o, d_z, dgate, g_fnw, g_mlnw, g_w_out) = _tail(
        o_na, proj, h_f, h_b, x, target, gate, ml_norm_w, final_norm_w, w_out_b)
    dq_na, dk_na, dv_na, rpbacc = _na_bwd(proj, tbl, d_ona, o_na, lse_na)
    g_rpb = _rpb_reduce(rpbacc, rows)
    dq_f, dk_f, dv_f, dgi_f, dgf_f = _ml_bwd(qk_act, proj, gi_f, gf_f, dhs, cchk_f, nmchk_f, (),
                                             False, "ml_bwd_f")
    dq_ml, dk_ml, dv_ml, dgi_b, dgf_b = _ml_bwd(qk_act, proj, gi_b, gf_b, dhs, cchk_b, nmchk_b, (dq_f, dk_f, dv_f),
                                                True, "ml_bwd_b")
    du, g_conv_w, g_conv_b = _conv_bwd(dq_ml, dk_ml, pre, proj, conv_w8, tm)
    dgates = _gate_cols(dgi_f, dgf_f, dgi_b, dgf_b, t)
    grad_x, g_w_in, g_b_in, dscale, dshift, g_nw = _in_bwd(
        [dq_na, dk_na, dv_na, d_naz, du, dv_ml, d_o, d_z, dgates], x, dres, w_in_t, norm_w, scale, shift)
    dmod = jnp.concatenate([dshift, dscale, dgate], axis=1)
    return (loss, grad_x, dmod, g_nw, g_w_in, g_b_in, g_conv_w, g_conv_b, g_rpb, g_mlnw, g_w_out, g_fnw)


MESH = pl.DeviceIdType.MESH
N_DEV = 8
ANY = pl.BlockSpec(memory_space=pl.ANY)
WHOLE_VMEM = pl.BlockSpec(memory_space=pltpu.VMEM)


def _allgather8(blocks, name):
    na = len(blocks)

    def body(*refs):
        x_refs = refs[:na]
        out_refs = refs[na:2 * na]
        send_sems, recv_sems, local_sems = refs[2 * na:]
        x, y, c = lax.axis_index("x"), lax.axis_index("y"), lax.axis_index("c")
        me, sibling = (x, y, c), (x, y, 1 - c)
        chips = [(1 - x, y), (x, 1 - y), (1 - x, 1 - y)]

        def rows(a, px, py, pc):
            return out_refs[a].at[4 * px + 2 * py + pc]

        def copy(a, k, block, to, src=None):
            return pltpu.make_async_remote_copy(
                src_ref=rows(a, *block) if src is None else src, dst_ref=rows(a, *block),
                send_sem=send_sems.at[a, k], recv_sem=recv_sems.at[a, k],
                device_id=to, device_id_type=MESH)

        mine, first, passed = [], [], []
        for a in range(na):
            cp = pltpu.make_async_copy(x_refs[a], rows(a, *me), local_sems.at[a])
            cp.start()
            mine.append(cp)
            first.append(copy(a, 0, me, sibling, src=x_refs[a]))
            first += [copy(a, 1 + j, me, (*chip, c), src=x_refs[a]) for j, chip in enumerate(chips)]
        for cp in first:
            cp.start()
        for a in range(na):
            for j, chip in enumerate(chips):
                copy(a, 1 + j, (*chip, c), me).wait_recv()
                fwd = copy(a, 4 + j, (*chip, c), sibling)
                fwd.start()
                passed.append(fwd)
        for a in range(na):
            copy(a, 0, sibling, me).wait_recv()
            for j, chip in enumerate(chips):
                copy(a, 4 + j, (*chip, 1 - c), me).wait_recv()
        for cp in first + passed:
            cp.wait_send()
        for cp in mine:
            cp.wait()

    return pl.pallas_call(
        body, name=name,
        out_shape=[jax.ShapeDtypeStruct((N_DEV,) + b.shape, b.dtype) for b in blocks],
        in_specs=[WHOLE_VMEM] * na, out_specs=[WHOLE_VMEM] * na,
        scratch_shapes=[pltpu.SemaphoreType.DMA((na, 7)), pltpu.SemaphoreType.DMA((na, 7)),
                        pltpu.SemaphoreType.DMA((na,))],
        compiler_params=pltpu.CompilerParams(vmem_limit_bytes=VMEM_LIMIT),
    )(*blocks)


def _pair_exchange(arrs, name):
    na = len(arrs)

    def body(*refs):
        in_refs = refs[:na]
        out_refs = refs[na:2 * na]
        send_sems, recv_sems = refs[2 * na:]
        sibling = (lax.axis_index("x"), lax.axis_index("y"), 1 - lax.axis_index("c"))
        copies = [pltpu.make_async_remote_copy(
            src_ref=in_refs[a], dst_ref=out_refs[a], send_sem=send_sems.at[a], recv_sem=recv_sems.at[a],
            device_id=sibling, device_id_type=MESH) for a in range(na)]
        for cp in copies:
            cp.start()
        for cp in copies:
            cp.wait()

    return pl.pallas_call(
        body, name=name,
        out_shape=[jax.ShapeDtypeStruct(a.shape, a.dtype) for a in arrs],
        in_specs=[ANY] * na, out_specs=[ANY] * na,
        scratch_shapes=[pltpu.SemaphoreType.DMA((na,)), pltpu.SemaphoreType.DMA((na,))],
    )(*arrs)


def _share_halves(arrs, name):
    na = len(arrs)

    def body(*refs):
        in_refs = refs[:na]
        out_refs = refs[na:2 * na]
        send_sems, recv_sems, local_sems = refs[2 * na:]
        c = lax.axis_index("c")
        sibling = (lax.axis_index("x"), lax.axis_index("y"), 1 - c)
        local, remote = [], []
        for a in range(na):
            lc = pltpu.make_async_copy(in_refs[a], out_refs[a].at[c], local_sems.at[a])
            lc.start()
            local.append(lc)
            rc = pltpu.make_async_remote_copy(
                src_ref=in_refs[a], dst_ref=out_refs[a].at[c], send_sem=send_sems.at[a], recv_sem=recv_sems.at[a],
                device_id=sibling, device_id_type=MESH)
            rc.start()
            remote.append(rc)
        for a in range(na):
            remote[a].wait_send()
            pltpu.make_async_remote_copy(
                src_ref=in_refs[a], dst_ref=out_refs[a].at[1 - c], send_sem=send_sems.at[a],
                recv_sem=recv_sems.at[a], device_id=sibling, device_id_type=MESH).wait_recv()
            local[a].wait()

    return pl.pallas_call(
        body, name=name,
        out_shape=[jax.ShapeDtypeStruct((2,) + a.shape, a.dtype) for a in arrs],
        in_specs=[ANY] * na, out_specs=[ANY] * na,
        scratch_shapes=[pltpu.SemaphoreType.DMA((na,)), pltpu.SemaphoreType.DMA((na,)),
                        pltpu.SemaphoreType.DMA((na,))],
    )(*arrs)


def _chip_exchange(arrs, name):
    na = len(arrs)

    def body(*refs):
        in_refs = refs[:na]
        out_refs = refs[na:2 * na]
        send_sems, recv_sems, local_sems = refs[2 * na:]
        x, y, c = lax.axis_index("x"), lax.axis_index("y"), lax.axis_index("c")
        my_chip = 2 * x + y
        chips = [(1 - x, y), (x, 1 - y), (1 - x, 1 - y)]
        local, remote = [], []
        for a in range(na):
            cp = pltpu.make_async_copy(in_refs[a].at[my_chip], out_refs[a].at[my_chip], local_sems.at[a])
            cp.start()
            local.append(cp)
            for j, (px, py) in enumerate(chips):
                cp = pltpu.make_async_remote_copy(
                    src_ref=in_refs[a].at[2 * px + py], dst_ref=out_refs[a].at[my_chip],
                    send_sem=send_sems.at[a, j], recv_sem=recv_sems.at[a, j],
                    device_id=(px, py, c), device_id_type=MESH)
                cp.start()
                remote.append(cp)
        for cp in remote:
            cp.wait()
        for cp in local:
            cp.wait()

    return pl.pallas_call(
        body, name=name,
        out_shape=[jax.ShapeDtypeStruct(a.shape, a.dtype) for a in arrs],
        in_specs=[ANY] * na, out_specs=[ANY] * na,
        scratch_shapes=[pltpu.SemaphoreType.DMA((na, 3)), pltpu.SemaphoreType.DMA((na, 3)),
                        pltpu.SemaphoreType.DMA((na,))],
    )(*arrs)


def _rows_tile(r):
    for cand in (512, 256, 128, 64, 32, 16, 8):
        if r % cand == 0:
            return cand
    return r


def _add2(a, b, name, out_dtype):
    s, r, n = a.shape
    tr = _rows_tile(r)

    def body(a_ref, b_ref, o_ref):
        o_ref[...] = (a_ref[...] + b_ref[...]).astype(out_dtype)

    spec = pl.BlockSpec((1, tr, n), lambda i, j: (i, j, 0))
    return pl.pallas_call(
        body, name=name, grid=(s, r // tr), in_specs=[spec, spec], out_specs=spec,
        out_shape=jax.ShapeDtypeStruct(a.shape, out_dtype),
        compiler_params=_cparams(("parallel", "parallel")),
    )(a, b)


def _sum_slabs(a, name):
    s, r, n = a.shape
    tr = _rows_tile(r)

    def body(a_ref, o_ref):
        acc = a_ref[0].astype(F32)
        for k in range(1, s):
            acc = acc + a_ref[k].astype(F32)
        o_ref[...] = acc

    return pl.pallas_call(
        body, name=name, grid=(r // tr,),
        in_specs=[pl.BlockSpec((s, tr, n), lambda i: (0, i, 0))],
        out_specs=pl.BlockSpec((tr, n), lambda i: (i, 0)),
        out_shape=jax.ShapeDtypeStruct((r, n), F32),
        compiler_params=_cparams(("parallel",)),
    )(a)


ADAMW_WHOLE = 64 * 1024


def _adamw(w, g, m, v, name):
    r, n = w.shape
    if r * n <= ADAMW_WHOLE:
        blk, grid, imap = (r, n), (1,), (lambda i: (0, 0))
    elif r % 8 == 0:
        blk, grid, imap = (_rows_tile(r), n), (r // _rows_tile(r),), (lambda i: (i, 0))
    else:
        blk, grid, imap = (r, 128), (n // 128,), (lambda i: (0, i))
    c1 = 1.0 / (1.0 - ADAM_B1 ** ADAM_STEP)
    c2 = 1.0 / (1.0 - ADAM_B2 ** ADAM_STEP)

    def body(w_ref, g_ref, m_ref, v_ref, d_ref, nm_ref, nv_ref):
        gv = g_ref[...]
        nm = ADAM_B1 * m_ref[...] + (1.0 - ADAM_B1) * gv
        nv = ADAM_B2 * v_ref[...] + (1.0 - ADAM_B2) * (gv * gv)
        nm_ref[...] = nm
        nv_ref[...] = nv
        d_ref[...] = -ADAM_LR * ((nm * c1) / (jnp.sqrt(nv * c2) + ADAM_EPS) + ADAM_WD * w_ref[...])

    spec = pl.BlockSpec(blk, imap)
    o = jax.ShapeDtypeStruct((r, n), F32)
    return pl.pallas_call(
        body, name=name, grid=grid, in_specs=[spec] * 4, out_specs=[spec] * 3, out_shape=[o, o, o],
        compiler_params=_cparams(("parallel",)),
    )(w, g, m, v)


def _mod_fwd(c_all, w_ada_s, b_ada_s):
    def body(c_ref, w_ref, b_ref, o_ref):
        o_ref[...] = jnp.dot(_silu(c_ref[...]), w_ref[...], precision=HI, preferred_element_type=F32) + b_ref[...]

    return pl.pallas_call(
        body, name="mod_fwd", out_shape=jax.ShapeDtypeStruct((c_all.shape[0], w_ada_s.shape[1]), F32),
        in_specs=[WHOLE_VMEM] * 3, out_specs=WHOLE_VMEM,
        compiler_params=pltpu.CompilerParams(vmem_limit_bytes=VMEM_LIMIT),
    )(c_all, w_ada_s, b_ada_s)


def _wada_grad(c_all, dmod_s):
    def body(c_ref, d_ref, o_ref):
        o_ref[...] = lax.dot_general(_silu(c_ref[...]), d_ref[...], (((0,), (0,)), ((), ())),
                                     precision=HI, preferred_element_type=F32)

    return pl.pallas_call(
        body, name="w_ada_grad", out_shape=jax.ShapeDtypeStruct((c_all.shape[1], dmod_s.shape[1]), F32),
        in_specs=[WHOLE_VMEM] * 2, out_specs=WHOLE_VMEM,
        compiler_params=pltpu.CompilerParams(vmem_limit_bytes=VMEM_LIMIT),
    )(c_all, dmod_s)


SMALL_ROWS = 24


def _pad_rows(v, nrows):
    v = v.reshape(-1)
    return jnp.pad(v, (0, nrows * 1024 - v.shape[0])).reshape(nrows, 1024)


def _pack_small(b_ada, norm_w, b_in, conv_w_full, conv_b, rpb, ml_norm_w, final_norm_w, last):
    parts = [_pad_rows(b_ada, 3), _pad_rows(norm_w, 1), _pad_rows(b_in, 5), _pad_rows(conv_w_full, 5),
             _pad_rows(conv_b, 1), _pad_rows(rpb, 4), _pad_rows(ml_norm_w, 1), _pad_rows(final_norm_w, 1),
             _pad_rows(last, 3)]
    return jnp.concatenate(parts, axis=0)


def _unpack_small(p):
    return dict(b_ada=p[0:3].reshape(1, 3072), norm_w=p[3:4], b_in=p[4:9].reshape(-1)[:IN_W].reshape(1, IN_W),
                conv_w=p[9:14], conv_b=p[14:15],
                rpb=p[15:19].reshape(-1)[:NA_HEADS * 15 * 31].reshape(1, NA_HEADS, 15, 31),
                ml_norm_w=p[19:20, :ML_W], final_norm_w=p[20], last=p[21])


def kernel(x, c, w_ada, b_ada, norm_w, w_in, b_in, conv_w, conv_b, rpb, ml_norm_w, w_out, final_norm_w, loss_target, m_w_ada, m_b_ada, m_norm_w, m_w_in, m_b_in, m_conv_w, m_conv_b, m_rpb, m_ml_norm_w, m_w_out, m_final_norm_w, v_w_ada, v_b_ada, v_norm_w, v_w_in, v_b_in, v_conv_w, v_conv_b, v_rpb, v_ml_norm_w, v_w_out, v_final_norm_w):
    xi, yi, ci = lax.axis_index("x"), lax.axis_index("y"), lax.axis_index("c")
    chip = 2 * xi + yi
    dev = 2 * chip + ci
    t = x.shape[1]
    ada_n = w_ada.shape[2]
    in_n = w_in.shape[2]
    out_r = w_out.shape[1]

    c_blk = jnp.pad(c, ((0, 7), (0, 0)))
    w_in_t, m_w_in_t, v_w_in_t = w_in[0].T, m_w_in[0].T, v_w_in[0].T
    in_h = in_n // 2
    w_in_half = lax.dynamic_slice_in_dim(w_in_t, ci * in_h, in_h, axis=0).astype(BF16)
    w_out_half = lax.dynamic_slice_in_dim(w_out[0], ci * (out_r // 2), out_r // 2, axis=0).astype(BF16)
    conv_blk = jnp.pad(conv_w[0], ((0, 3), (0, 0)))
    c_g, conv_g, w_in_g, w_out_g = _allgather8([c_blk, conv_blk, w_in_half, w_out_half], "gather_c_weights")
    c_all = c_g[:, 0]
    w_out_g = w_out_g.reshape(D_MODEL, D_MODEL)
    b_ada_s = lax.dynamic_slice_in_dim(b_ada, chip * ada_n, ada_n, axis=1)
    mod_s = _mod_fwd(c_all, w_ada[0], b_ada_s)
    (mod_g,) = _allgather8([mod_s], "gather_mod")
    mod_mine = lax.dynamic_index_in_dim(mod_g, dev, axis=1, keepdims=False)
    mod = mod_mine[0::2].reshape(1, 3 * D_MODEL)
    shift, scale, gate = mod[:, :D_MODEL], mod[:, D_MODEL:2 * D_MODEL], mod[:, 2 * D_MODEL:]

    w_in_tp = jnp.pad(w_in_g.reshape(IN_W, D_MODEL), ((0, IN_PAD - IN_W), (0, 0)))
    b_in_p = jnp.pad(b_in, ((0, 0), (0, IN_PAD - IN_W)))
    conv_w8 = conv_g.reshape(4, 2, 8, conv_w.shape[2])[:, 0].transpose(1, 0, 2).reshape(8, D_MODEL)

    (loss, grad_x, dmod, g_nw, g_w_in, g_b_in, g_conv_w, g_conv_b, g_rpb, g_mlnw, g_w_out, g_fnw) = _local_step(
        x[0], loss_target[0], shift, scale, gate, norm_w, w_in_tp, b_in_p, conv_w8, conv_b, rpb[0],
        ml_norm_w, w_out_g, final_norm_w.reshape(1, D_MODEL))

    g_in_t = g_w_in

    def halves(a, per_chip, h):
        return jnp.stack([lax.dynamic_slice_in_dim(a, k * per_chip + h * (per_chip // 2), per_chip // 2, axis=0)
                          for k in range(4)])

    ri, ro = _pair_exchange([halves(g_in_t, in_n, 1 - ci), halves(g_w_out, out_r, 1 - ci)], "rs_pair")
    pi = _add2(halves(g_in_t, in_n, ci), ri, "rs_pair_add_in", BF16)
    po = _add2(halves(g_w_out, out_r, ci), ro, "rs_pair_add_out", BF16)
    qi, qo = _chip_exchange([pi, po], "rs_chips")
    si = _sum_slabs(qi, "rs_sum_in")
    so = _sum_slabs(qo, "rs_sum_out")
    gi2, go2 = _share_halves([si, so], "rs_share")
    g_w_in_s = gi2.reshape(in_n, D_MODEL)
    g_w_out_s = go2.reshape(out_r, D_MODEL)

    small = _pack_small(dmod, g_nw, g_b_in[:, :IN_W], g_conv_w[:CONV_W], g_conv_b, g_rpb, g_mlnw, g_fnw,
                        jnp.pad(loss, ((0, 0), (0, 1024 - 128))))
    (small_g,) = _allgather8([small], "gather_small")
    small_sum = _sum_slabs(small_g, "small_sum")
    gs = _unpack_small(small_sum)
    dmod_all = small_g[:, 0:3].reshape(N_DEV, 3 * D_MODEL)
    g_w_ada_s = _wada_grad(c_all, lax.dynamic_slice_in_dim(dmod_all, chip * ada_n, ada_n, axis=1))
    g_conv_w_s = lax.dynamic_slice_in_dim(gs['conv_w'], chip * conv_w.shape[2], conv_w.shape[2], axis=1)
    loss_total = gs['last'][0]

    small_names = ('b_ada', 'norm_w', 'b_in', 'conv_b', 'rpb', 'ml_norm_w', 'final_norm_w')
    small_w = (b_ada, norm_w, b_in, conv_b, rpb, ml_norm_w, final_norm_w)
    small_m = (m_b_ada, m_norm_w, m_b_in, m_conv_b, m_rpb, m_ml_norm_w, m_final_norm_w)
    small_v = (v_b_ada, v_norm_w, v_b_in, v_conv_b, v_rpb, v_ml_norm_w, v_final_norm_w)
    ds_, nms, nvs = {}, {}, {}
    for nm_, w_, m_, v_ in zip(small_names, small_w, small_m, small_v):
        two_d = (NA_HEADS, w_.size // NA_HEADS) if nm_ == 'rpb' else (1, w_.size)
        outs = _adamw(w_.reshape(two_d), gs[nm_].reshape(two_d), m_.reshape(two_d), v_.reshape(two_d),
                      "adamw_" + nm_)
        ds_[nm_], nms[nm_], nvs[nm_] = [o.reshape(w_.shape) for o in outs]
    d_ada, nm_ada, nv_ada = _adamw(w_ada[0], g_w_ada_s, m_w_ada[0], v_w_ada[0], "adamw_w_ada")
    d_in, nm_in, nv_in = _adamw(w_in_t, g_w_in_s, m_w_in_t, v_w_in_t, "adamw_w_in")
    d_out, nm_out, nv_out = _adamw(w_out[0], g_w_out_s, m_w_out[0], v_w_out[0], "adamw_w_out")
    d_cw, nm_cw, nv_cw = _adamw(conv_w[0], g_conv_w_s, m_conv_w[0], v_conv_w[0], "adamw_conv_w")

    def group(big_ada, big_in, big_out, cw, sm):
        return (big_ada[None], sm['b_ada'], sm['norm_w'], big_in.T[None], sm['b_in'], cw[None], sm['conv_b'],
                sm['rpb'], sm['ml_norm_w'], big_out[None], sm['final_norm_w'])

    return ((loss_total, grad_x[None])
            + group(g_w_ada_s, g_w_in_s, g_w_out_s, g_conv_w_s, gs)
            + group(d_ada, d_in, d_out, d_cw, ds_)
            + group(nm_ada, nm_in, nm_out, nm_cw, nms)
            + group(nv_ada, nv_in, nv_out, nv_cw, nvs))
```

```python
import functools

import numpy as np
import jax
import jax.numpy as jnp
from jax import lax
from jax.experimental import pallas as pl
from jax.experimental.pallas import tpu as pltpu

F32 = jnp.float32
BF16 = jnp.bfloat16
HI = lax.Precision.HIGHEST

D_MODEL = 1024
GRID_W = 64
NA_W = 512
NA_HEAD_DIM = 64
NA_HEADS = 8
NA_KH = 8
NA_KW = 16
ML_W = 512
ML_HEADS = 4
ML_HEAD_DIM = 128
ML_CHUNK = 128
CONV_W = 5
EPS = 1e-6
IN_W = 4 * NA_W + 5 * ML_W + 4 * ML_HEADS
IN_MAIN = 4 * NA_W + 5 * ML_W
IN_PAD = IN_MAIN + 128
NEG = -1e30

ADAM_LR = 0.001
ADAM_B1 = 0.9
ADAM_B2 = 0.999
ADAM_EPS = 1e-08
ADAM_WD = 0.01
ADAM_STEP = 10

NA_QROWS = 8
NA_KROWS = 16
NA_QT = NA_QROWS * GRID_W
NA_KT = NA_KROWS * GRID_W
NA_KCH = 256
NA_RC = 32
ML_NB = 16
ML_TB = ML_NB * ML_CHUNK
ML_HPS = 1

VMEM_LIMIT = 56 * 1024 * 1024
IN_BWD_VMEM_LIMIT = 60 * 1024 * 1024


def _cparams(sem, vmem=VMEM_LIMIT):
    return pltpu.CompilerParams(dimension_semantics=sem, vmem_limit_bytes=vmem)


def _silu(x):
    return x * jax.nn.sigmoid(x)


def _dsilu(x):
    s = jax.nn.sigmoid(x)
    return s * (1.0 + x * (1.0 - s))


def _dot(a, b, dims):
    return lax.dot_general(a, b, (dims, ((), ())), preferred_element_type=F32)


def _nn(a, b):
    return _dot(a, b, ((1,), (0,)))


def _nt(a, b):
    return _dot(a, b, ((1,), (1,)))


def _tn(a, b):
    return _dot(a, b, ((0,), (0,)))


def _row(n):
    return pl.BlockSpec((1, n), lambda i: (0, 0))


def _modulated_norm(xv, nw, sc, sh):
    r = lax.rsqrt(jnp.mean(xv * xv, axis=-1, keepdims=True) + EPS)
    xn = xv * r
    return xn * nw * (1.0 + sc) + sh, xn, r


IN_TN = 768


def _in_proj(x, norm_w, scale, shift, w_in_t, b_in_p):
    t, d = x.shape
    tm = 2048
    gcol = IN_MAIN // 128

    def body(x_ref, nw_ref, sc_ref, sh_ref, w_ref, b_ref, wg_ref, bg_ref, proj_ref, g_ref, h_scr):
        @pl.when(pl.program_id(1) == 0)
        def _():
            h, _, _ = _modulated_norm(x_ref[...], nw_ref[...], sc_ref[...], sh_ref[...])
            h_scr[...] = h.astype(BF16)
            g_ref[...] = _nt(h_scr[...], wg_ref[...]) + bg_ref[...]
        proj_ref[...] = (_nt(h_scr[...], w_ref[...]) + b_ref[...]).astype(BF16)

    row = lambda n: pl.BlockSpec((1, n), lambda i, j: (0, 0))
    return pl.pallas_call(
        body, name="in_proj", grid=(t // tm, IN_MAIN // IN_TN),
        in_specs=[pl.BlockSpec((tm, d), lambda i, j: (i, 0)), row(d), row(d), row(d),
                  pl.BlockSpec((IN_TN, d), lambda i, j: (j, 0)), pl.BlockSpec((1, IN_TN), lambda i, j: (0, j)),
                  pl.BlockSpec((128, d), lambda i, j: (gcol, 0)), pl.BlockSpec((1, 128), lambda i, j: (0, gcol))],
        out_specs=[pl.BlockSpec((tm, IN_TN), lambda i, j: (i, j)), pl.BlockSpec((tm, 128), lambda i, j: (i, 0))],
        out_shape=[jax.ShapeDtypeStruct((t, IN_MAIN), BF16), jax.ShapeDtypeStruct((t, 128), F32)],
        scratch_shapes=[pltpu.VMEM((tm, d), BF16)],
        compiler_params=_cparams(("parallel", "arbitrary")),
    )(x, norm_w, scale, shift, w_in_t, b_in_p, w_in_t, b_in_p)


def _ml_norm_parts(hs, o, z, nw):
    outs = []
    for hh in range(ML_HEADS):
        sl = slice(hh * ML_HEAD_DIM, (hh + 1) * ML_HEAD_DIM)
        hm = hs[:, sl] * jax.nn.sigmoid(o[:, sl])
        mu = jnp.mean(hm, axis=-1, keepdims=True)
        cen = hm - mu
        var = jnp.mean(cen * cen, axis=-1, keepdims=True)
        rs = lax.rsqrt(var + EPS)
        outs.append((sl, cen * rs, rs))
    return outs


def _tail(o_na, proj, h_f, h_b, x, target, gate, ml_norm_w, fnw, w_out_b):
    t, d = x.shape
    tm = 256

    def body(ona_ref, naz_ref, hf_ref, hb_ref, o_ref, z_ref, x_ref, tg_ref, g_ref, nw_ref, fw_ref, w_ref,
             loss_ref, dres_ref, dona_ref, dnaz_ref, dhs_ref, do_ref, dz_ref, dgate_ref, gfw_ref, gnw_ref,
             gwo_ref, mix_scr):
        @pl.when(pl.program_id(0) == 0)
        def _():
            for r in (loss_ref, dgate_ref, gfw_ref, gnw_ref, gwo_ref):
                r[...] = jnp.zeros_like(r)
        naz = naz_ref[...].astype(F32)
        ona = ona_ref[...]
        sna = _silu(naz)
        mix_scr[:, 0:NA_W] = (ona * sna).astype(BF16)
        hs = hf_ref[...] + hb_ref[...]
        z = z_ref[...].astype(F32)
        ov = o_ref[...].astype(F32)
        parts = _ml_norm_parts(hs, ov, z, nw_ref[...])
        szs = []
        for sl, xn, _ in parts:
            sz = _silu(z[:, sl])
            szs.append(sz)
            mix_scr[:, NA_W + sl.start:NA_W + sl.stop] = (xn * nw_ref[:, sl] * sz).astype(BF16)
        mixb = mix_scr[...]
        wv = w_ref[...]
        yv = _nn(mixb, wv)
        gate_v = g_ref[...]
        hres = x_ref[...] + gate_v * yv
        r = lax.rsqrt(jnp.mean(hres * hres, axis=-1, keepdims=True) + EPS)
        xnf = hres * r
        err = xnf * fw_ref[...] - tg_ref[...]
        loss_ref[...] += 0.5 * jnp.sum(jnp.sum(err * err, axis=-1, keepdims=True) * (1.0 / d), axis=0, keepdims=True)
        dout = err * (1.0 / d)
        gfw_ref[...] += jnp.sum(dout * xnf, axis=0, keepdims=True)
        dxn = dout * fw_ref[...]
        dres = r * (dxn - xnf * jnp.mean(dxn * xnf, axis=-1, keepdims=True))
        dres_ref[...] = dres
        dgate_ref[...] += jnp.sum(dres * yv, axis=0, keepdims=True)
        dyb = (dres * gate_v).astype(BF16)
        gwo_ref[...] += _tn(mixb, dyb)
        dmix = _nt(dyb, wv)
        dna = dmix[:, 0:NA_W]
        dona_ref[...] = dna * sna
        dnaz_ref[...] = (dna * ona * _dsilu(naz)).astype(BF16)
        for (sl, xn, rs), sz in zip(parts, szs):
            dyv = dmix[:, NA_W + sl.start:NA_W + sl.stop]
            zz = z[:, sl]
            w = nw_ref[:, sl]
            dz_ref[:, sl] = (dyv * xn * w * _dsilu(zz)).astype(BF16)
            gnw_ref[:, sl] += jnp.sum(dyv * xn * sz, axis=0, keepdims=True)
            dxm = dyv * w * sz
            dhm = rs * (dxm - jnp.mean(dxm, axis=-1, keepdims=True)
                        - xn * jnp.mean(dxm * xn, axis=-1, keepdims=True))
            so = jax.nn.sigmoid(ov[:, sl])
            dhs_ref[:, sl] = dhm * so
            do_ref[:, sl] = (dhm * hs[:, sl] * so * (1.0 - so)).astype(BF16)

    blk = lambda c: pl.BlockSpec((tm, 512), lambda i, c=c: (i, c))
    full = pl.BlockSpec((tm, d), lambda i: (i, 0))
    o512 = jax.ShapeDtypeStruct((t, 512), F32)
    b512 = jax.ShapeDtypeStruct((t, 512), BF16)
    whole = pl.BlockSpec((d, d), lambda i: (0, 0))
    return pl.pallas_call(
        body, name="tail", grid=(t // tm,),
        in_specs=[blk(0), blk(3), blk(0), blk(0), blk(7), blk(8), full, full, _row(d), _row(ML_W), _row(d), whole],
        out_specs=[pl.BlockSpec((1, 128), lambda i: (0, 0)), full] + [blk(0)] * 5
        + [_row(d), _row(d), _row(ML_W), whole],
        out_shape=[jax.ShapeDtypeStruct((1, 128), F32), jax.ShapeDtypeStruct((t, d), F32),
                   o512, b512, o512, b512, b512]
        + [jax.ShapeDtypeStruct((1, d), F32), jax.ShapeDtypeStruct((1, d), F32),
           jax.ShapeDtypeStruct((1, ML_W), F32), jax.ShapeDtypeStruct((d, d), F32)],
        scratch_shapes=[pltpu.VMEM((tm, d), BF16)],
        compiler_params=_cparams(("arbitrary",)),
    )(o_na, proj, h_f, h_b, proj, proj, x, target, gate, ml_norm_w, fnw, w_out_b)


def _in_bwd(pieces, x, dres, w_in_t, norm_w, scale, shift):
    t, d = x.shape
    tm = 512
    nt = t // tm
    widths = [p.shape[1] for p in pieces]
    offs = [sum(widths[:k]) for k in range(len(widths))]
    assert sum(widths) == IN_PAD
    npc = len(pieces)

    def body(*refs):
        p_refs = refs[:npc]
        (x_ref, dres_ref, w_hbm, nw_ref, sc_ref, sh_ref,
         gx_ref, gw_hbm, gb_ref, dsc_ref, dsh_ref, gnw_ref, w_vmem, acc, stage, sem) = refs[npc:]
        i = pl.program_id(0)

        @pl.when(i == 0)
        def _():
            cp = pltpu.make_async_copy(w_hbm, w_vmem, sem.at[0])
            cp.start()
            acc[...] = jnp.zeros_like(acc)
            gb_ref[...] = jnp.zeros_like(gb_ref)
            dsc_ref[...] = jnp.zeros_like(dsc_ref)
            dsh_ref[...] = jnp.zeros_like(dsh_ref)
            gnw_ref[...] = jnp.zeros_like(gnw_ref)
            cp.wait()

        nw = nw_ref[...]
        s1 = 1.0 + sc_ref[...]
        h, xn, r = _modulated_norm(x_ref[...], nw, sc_ref[...], sh_ref[...])
        hb = h.astype(BF16)
        dhv = jnp.zeros((tm, d), F32)
        for p_ref, c0, w in zip(p_refs, offs, widths):
            pt = p_ref[...]
            pb = pt.astype(BF16)
            dhv = dhv + _nn(pb, w_vmem[c0:c0 + w, :])
            acc[:, c0:c0 + w] += _tn(hb, pb)
            gb_ref[:, c0:c0 + w] += jnp.sum(pt.astype(F32), axis=0, keepdims=True)
        dsh_ref[...] += jnp.sum(dhv, axis=0, keepdims=True)
        dsc_ref[...] += jnp.sum(dhv * xn * nw, axis=0, keepdims=True)
        gnw_ref[...] += jnp.sum(dhv * xn * s1, axis=0, keepdims=True)
        dxn = dhv * nw * s1
        gx_ref[...] = dres_ref[...] + r * (dxn - xn * jnp.mean(dxn * xn, axis=-1, keepdims=True))

        @pl.when(i == nt - 1)
        def _():
            copies = []
            for blk in range(IN_PAD // 128):
                slot = blk % 2
                if blk >= 2:
                    copies[blk - 2].wait()
                stage[slot] = acc[:, blk * 128:(blk + 1) * 128].T
                cp = pltpu.make_async_copy(stage.at[slot], gw_hbm.at[pl.ds(blk * 128, 128), :], sem.at[1 + slot])
                cp.start()
                copies.append(cp)
            copies[-2].wait()
            copies[-1].wait()

    full = pl.BlockSpec((tm, d), lambda i: (i, 0))
    return pl.pallas_call(
        body, name="in_bwd", grid=(nt,),
        in_specs=[pl.BlockSpec((tm, w), lambda i: (i, 0)) for w in widths]
        + [full, full, pl.BlockSpec(memory_space=pl.ANY), _row(d), _row(d), _row(d)],
        out_specs=[full, pl.BlockSpec(memory_space=pl.ANY), _row(IN_PAD), _row(d), _row(d), _row(d)],
        out_shape=[jax.ShapeDtypeStruct((t, d), F32), jax.ShapeDtypeStruct((IN_PAD, d), F32),
                   jax.ShapeDtypeStruct((1, IN_PAD), F32)] + [jax.ShapeDtypeStruct((1, d), F32)] * 3,
        scratch_shapes=[pltpu.VMEM((IN_PAD, d), BF16), pltpu.VMEM((d, IN_PAD), F32),
                        pltpu.VMEM((2, 128, d), F32), pltpu.SemaphoreType.DMA((3,))],
        compiler_params=_cparams(("arbitrary",), IN_BWD_VMEM_LIMIT),
    )(*pieces, x, dres, w_in_t, norm_w, scale, shift)


def _na_static(rows):
    cases = [(0, 0), (NA_QROWS, NA_QROWS - 4), (rows - NA_QROWS, rows - NA_KROWS)]
    dy = np.zeros((3, NA_QROWS, NA_KROWS), np.int32)
    rv = np.zeros((3, NA_QROWS, NA_KROWS), bool)
    for cs, (r0, kr0) in enumerate(cases):
        for i in range(NA_QROWS):
            for j in range(NA_KROWS):
                r, kr = r0 + i, kr0 + j
                rs = min(max(r - NA_KH // 2, 0), rows - NA_KH)
                rv[cs, i, j] = rs <= kr <= rs + NA_KH - 1
                dy[cs, i, j] = min(max(kr - r + NA_KH - 1, 0), 2 * NA_KH - 2)
    cq = np.arange(GRID_W)[:, None]
    ck = np.arange(GRID_W)[None, :]
    cs0 = np.clip(cq - NA_KW // 2, 0, GRID_W - NA_KW)
    cv = (ck >= cs0) & (ck < cs0 + NA_KW)
    dx = np.clip(ck - cq, -(NA_KW - 1), NA_KW - 1) + NA_KW - 1
    return dy, rv, dx.astype(np.int32), cv


def _na_bias_table(rpb, rows):
    _, _, dx, cv = _na_static(rows)
    ndy = 2 * NA_KH - 1
    onehot = (dx.reshape(1, -1) == np.arange(2 * NA_KW - 1)[:, None]).astype(np.float32)
    rpx = jnp.dot(rpb.reshape(NA_HEADS * ndy, 2 * NA_KW - 1), jnp.asarray(onehot), precision=HI)
    rpx = jnp.where(cv[None, None], rpx.reshape(NA_HEADS, ndy, GRID_W, GRID_W), NEG)
    neg = jnp.full((NA_HEADS, 1, GRID_W, GRID_W), NEG, F32)
    rpx = jnp.concatenate([rpx, neg], axis=1)
    nxt = jnp.concatenate([rpx[:, 1:], neg], axis=1)
    negs = jnp.broadcast_to(neg, rpx.shape)
    pairs = jnp.concatenate([jnp.concatenate([rpx, nxt], axis=3), jnp.concatenate([rpx, negs], axis=3),
                             jnp.concatenate([negs, rpx], axis=3)], axis=1)
    npair = pairs.shape[1]

    def body(m_ref, o_ref):
        cs = pl.program_id(1)
        r0 = jnp.where(cs == 0, 0, jnp.where(cs == 1, NA_QROWS, rows - NA_QROWS))
        kr0 = jnp.where(cs == 0, 0, jnp.where(cs == 1, NA_QROWS - NA_KH // 2, rows - NA_KROWS))
        for i in range(NA_QROWS):
            r = r0 + i
            rs = jnp.clip(r - NA_KH // 2, 0, rows - NA_KH)
            for jp in range(NA_KROWS // 2):
                kl = kr0 + 2 * jp
                vl = (kl >= rs) & (kl <= rs + NA_KH - 1)
                vr = (kl + 1 >= rs) & (kl + 1 <= rs + NA_KH - 1)
                dyl = jnp.clip(kl - r + NA_KH - 1, 0, ndy)
                dyr = jnp.clip(kl + 1 - r + NA_KH - 1, 0, ndy)
                idx = jnp.where(vl & vr, dyl, jnp.where(vl, 16 + dyl, jnp.where(vr, 32 + dyr, 16 + ndy)))
                o_ref[0, 0, i * GRID_W:(i + 1) * GRID_W, jp * 128:(jp + 1) * 128] = m_ref[0, idx]

    return pl.pallas_call(
        body, name="na_bias_table", grid=(NA_HEADS, 3),
        in_specs=[pl.BlockSpec((1, npair, GRID_W, 128), lambda h, cs: (h, 0, 0, 0))],
        out_specs=pl.BlockSpec((1, 1, NA_QT, NA_KT), lambda h, cs: (h, cs, 0, 0)),
        out_shape=jax.ShapeDtypeStruct((NA_HEADS, 3, NA_QT, NA_KT), F32),
        compiler_params=_cparams(("parallel", "parallel")),
    )(pairs)


def _na_specs(t):
    nb = t // NA_QT
    nkb = t // NA_KCH
    npieces = NA_KT // NA_KCH

    def kb0(b):
        return jnp.clip(b * (NA_QT // NA_KCH) - 1, 0, nkb - npieces)

    def case(b):
        return jnp.where(b == 0, 0, jnp.where(b == nb - 1, 2, 1))

    q_spec = pl.BlockSpec((NA_QT, 128), lambda p, b: (b, p))
    k_specs = [pl.BlockSpec((NA_KCH, 128), lambda p, b, i=i: (kb0(b) + i, 4 + p)) for i in range(npieces)]
    v_specs = [pl.BlockSpec((NA_KCH, 128), lambda p, b, i=i: (kb0(b) + i, 8 + p)) for i in range(npieces)]
    tbl_spec = pl.BlockSpec((2, 1, NA_QT, NA_KT), lambda p, b: (p, case(b), 0, 0))
    io_spec = pl.BlockSpec((NA_QT, 128), lambda p, b: (b, p))
    return nb, npieces, kb0, case, q_spec, k_specs, v_specs, tbl_spec, io_spec


NA_HALF = NA_QT // 2
NA_COMBOS_ALL = tuple((i, 0, NA_QT) for i in range(NA_KT // NA_KCH))
NA_COMBOS_INNER = ((0, 0, NA_HALF),) + tuple((i, 0, NA_QT) for i in range(1, NA_KT // NA_KCH - 1)) \
    + ((NA_KT // NA_KCH - 1, NA_HALF, NA_QT),)


def _na_place(val, r0, r1):
    if (r0, r1) == (0, NA_QT):
        return val
    z = jnp.zeros((NA_HALF, val.shape[1]), val.dtype)
    return jnp.concatenate([val, z] if r0 == 0 else [z, val], axis=0)


def _na_fwd(proj, tbl):
    t = proj.shape[0]
    nb, npieces, _, _, q_spec, k_specs, v_specs, tbl_spec, io_spec = _na_specs(t)
    lse_spec = pl.BlockSpec((1, NA_QT, 2), lambda p, b: (p, b, 0))

    def body(*refs):
        q_ref = refs[0]
        k_refs = refs[1:1 + npieces]
        v_refs = refs[1 + npieces:1 + 2 * npieces]
        tbl_ref, o_ref, lse_ref = refs[1 + 2 * npieces:]
        b = pl.program_id(1)

        def compute(combos):
            lane = lax.broadcasted_iota(jnp.int32, (1, 128), 1)
            qv = q_ref[...].astype(F32) * (NA_HEAD_DIM ** -0.5)
            ks = [r[...].astype(BF16) for r in k_refs]
            vs = [r[...].astype(BF16) for r in v_refs]
            hs = range(2)
            msk = [(lane // NA_HEAD_DIM) == hh for hh in hs]
            qh = [jnp.where(msk[hh], qv, 0.0).astype(BF16) for hh in hs]
            s = [[_nt(qh[hh][r0:r1], ks[i]) + tbl_ref[hh, 0, r0:r1, i * NA_KCH:(i + 1) * NA_KCH]
                  for i, r0, r1 in combos] for hh in hs]
            for h0 in (0, NA_HALF):
                rows = slice(h0, h0 + NA_HALF)
                cover = [(c, i, h0 - r0) for c, (i, r0, r1) in enumerate(combos) if r0 <= h0 < r1]
                part = [[s[hh][c][off:off + NA_HALF] for c, _, off in cover] for hh in hs]
                m = [functools.reduce(jnp.maximum, [jnp.max(v, axis=1, keepdims=True) for v in part[hh]]) for hh in hs]
                p = [[jnp.exp(v - m[hh]) for v in part[hh]] for hh in hs]
                l = [functools.reduce(jnp.add, [jnp.sum(v, axis=1, keepdims=True) for v in p[hh]]) for hh in hs]
                o = [functools.reduce(jnp.add, [_nn(p[hh][k].astype(BF16), vs[i]) for k, (_, i, _) in enumerate(cover)])
                     for hh in hs]
                for hh in hs:
                    lse_ref[0, rows, hh:hh + 1] = m[hh] + jnp.log(l[hh])
                o_ref[rows, :] = jnp.where(msk[0], o[0] / l[0], o[1] / l[1])

        inner = (b > 0) & (b < nb - 1)
        pl.when(inner)(lambda: compute(NA_COMBOS_INNER))
        pl.when(jnp.logical_not(inner))(lambda: compute(NA_COMBOS_ALL))

    return pl.pallas_call(
        body, name="na_fwd", grid=(4, nb),
        in_specs=[q_spec] + k_specs + v_specs + [tbl_spec],
        out_specs=[io_spec, lse_spec],
        out_shape=[jax.ShapeDtypeStruct((t, NA_W), F32), jax.ShapeDtypeStruct((4, t, 2), F32)],
        compiler_params=_cparams(("parallel", "arbitrary")),
    )(*([proj] * (1 + 2 * npieces)), tbl)


def _na_bwd(proj, tbl, d_o, o_na, lse):
    t = proj.shape[0]
    nb, npieces, kb0, case, q_spec, k_specs, v_specs, tbl_spec, io_spec = _na_specs(t)

    def body(*refs):
        q_ref = refs[0]
        k_refs = refs[1:1 + npieces]
        v_refs = refs[1 + npieces:1 + 2 * npieces]
        (tbl_ref, do_ref, o_ref, lse_ref, dq_ref, dk_hbm, dv_hbm, rpb_ref,
         dk_acc, dv_acc, dk_out, dv_out, s_scr, dp_scr, dsb_scr, pnb_scr, sem) = refs[1 + 2 * npieces:]
        p_id = pl.program_id(0)
        b = pl.program_id(1)

        @pl.when(b == 0)
        def _():
            dk_acc[...] = jnp.zeros_like(dk_acc)
            dv_acc[...] = jnp.zeros_like(dv_acc)

        @pl.when((b == 0) | (b == 1) | (b == nb - 1))
        def _():
            rpb_ref[...] = jnp.zeros_like(rpb_ref)

        def compute(combos):
            lane = lax.broadcasted_iota(jnp.int32, (1, 128), 1)
            scale = NA_HEAD_DIM ** -0.5
            qv = q_ref[...].astype(F32) * scale
            ks = [r[...].astype(BF16) for r in k_refs]
            vs = [r[...].astype(BF16) for r in v_refs]
            dov = do_ref[...]
            ov = o_ref[...]
            tok0 = kb0(b) * NA_KCH
            hs = range(2)
            msk = [(lane // NA_HEAD_DIM) == hh for hh in hs]
            qh = [jnp.where(msk[hh], qv, 0.0).astype(BF16) for hh in hs]
            doh = [jnp.where(msk[hh], dov, 0.0) for hh in hs]
            dohb = [doh[hh].astype(BF16) for hh in hs]
            dd = [jnp.sum(doh[hh] * ov, axis=1, keepdims=True) for hh in hs]
            for hh in hs:
                for c, (i, q0, q1) in enumerate(combos):
                    slot = (hh * len(combos) + c) % 2
                    cols = slice(i * NA_KCH, (i + 1) * NA_KCH)
                    s_scr[slot, 0:q1 - q0] = _nt(qh[hh][q0:q1], ks[i])
                    dp_scr[slot, 0:q1 - q0] = _nt(dohb[hh][q0:q1], vs[i])
                    for r0 in range(q0, q1, NA_RC):
                        rows = slice(r0, r0 + NA_RC)
                        loc = slice(r0 - q0, r0 - q0 + NA_RC)
                        p = jnp.exp(s_scr[slot, loc, :] + tbl_ref[hh, 0, rows, cols] - lse_ref[0, rows, hh:hh + 1])
                        d = p * (dp_scr[slot, loc, :] - dd[hh][rows])
                        pnb_scr[hh, rows, cols] = p.astype(BF16)
                        dsb_scr[hh, rows, cols] = d.astype(BF16)
                done = {(i, q0) for i, q0, _ in combos} | {(i, NA_HALF) for i, q0, q1 in combos if q1 - q0 == NA_QT}
                for i in range(npieces):
                    for q0 in (0, NA_HALF):
                        if (i, q0) not in done:
                            dsb_scr[hh, q0:q0 + NA_HALF, i * NA_KCH:(i + 1) * NA_KCH] = jnp.zeros(
                                (NA_HALF, NA_KCH), BF16)
            dqh = [functools.reduce(jnp.add, [_na_place(_nn(dsb_scr[hh, q0:q1, i * NA_KCH:(i + 1) * NA_KCH], ks[i]),
                                                        q0, q1) for i, q0, q1 in combos]) for hh in hs]
            dq_ref[...] = (jnp.where(msk[0], dqh[0], dqh[1]) * scale).astype(BF16)
            for i in range(npieces):
                rows = pl.ds(pl.multiple_of(tok0 + i * NA_KCH, NA_KCH), NA_KCH)
                cols = slice(i * NA_KCH, (i + 1) * NA_KCH)
                q0, q1 = [(a, e) for j, a, e in combos if j == i][0]
                dk_acc[rows, :] += (_tn(dsb_scr[0, q0:q1, cols], qh[0][q0:q1])
                                    + _tn(dsb_scr[1, q0:q1, cols], qh[1][q0:q1]))
                dv_acc[rows, :] += (_tn(pnb_scr[0, q0:q1, cols], dohb[0][q0:q1])
                                    + _tn(pnb_scr[1, q0:q1, cols], dohb[1][q0:q1]))
            for hh in hs:
                acc = dsb_scr[hh, 0:GRID_W, :].astype(F32)
                for i in range(1, NA_QROWS):
                    acc = acc + pltpu.roll(dsb_scr[hh, i * GRID_W:(i + 1) * GRID_W, :].astype(F32),
                                           NA_KT - i * GRID_W, 1)
                rpb_ref[0, 0, hh] += acc

        inner = (b > 0) & (b < nb - 1)
        pl.when(inner)(lambda: compute(NA_COMBOS_INNER))
        pl.when(jnp.logical_not(inner))(lambda: compute(NA_COMBOS_ALL))

        @pl.when(b == nb - 1)
        def _():
            cols = pl.ds(pl.multiple_of(p_id * 128, 128), 128)
            dk_out[...] = dk_acc[...].astype(BF16)
            dv_out[...] = dv_acc[...].astype(BF16)
            ck = pltpu.make_async_copy(dk_out, dk_hbm.at[:, cols], sem.at[0])
            cv = pltpu.make_async_copy(dv_out, dv_hbm.at[:, cols], sem.at[1])
            ck.start()
            cv.start()
            ck.wait()
            cv.wait()

    o512 = jax.ShapeDtypeStruct((t, NA_W), BF16)
    return pl.pallas_call(
        body, name="na_bwd", grid=(4, nb),
        in_specs=[q_spec] + k_specs + v_specs + [tbl_spec, io_spec, io_spec,
                                                 pl.BlockSpec((1, NA_QT, 2), lambda p, b: (p, b, 0))],
        out_specs=[io_spec, pl.BlockSpec(memory_space=pl.ANY), pl.BlockSpec(memory_space=pl.ANY),
                   pl.BlockSpec((1, 1, 2, GRID_W, NA_KT), lambda p, b: (p, case(b), 0, 0, 0))],
        out_shape=[o512, o512, o512, jax.ShapeDtypeStruct((4, 3, 2, GRID_W, NA_KT), F32)],
        scratch_shapes=[pltpu.VMEM((t, 128), F32), pltpu.VMEM((t, 128), F32),
                        pltpu.VMEM((t, 128), BF16), pltpu.VMEM((t, 128), BF16),
                        pltpu.VMEM((2, NA_QT, NA_KCH), F32), pltpu.VMEM((2, NA_QT, NA_KCH), F32),
                        pltpu.VMEM((2, NA_QT, NA_KT), BF16), pltpu.VMEM((2, NA_QT, NA_KT), BF16),
                        pltpu.SemaphoreType.DMA((2,))],
        compiler_params=_cparams(("arbitrary", "arbitrary")),
    )(*([proj] * (1 + 2 * npieces)), tbl, d_o, o_na, lse)


def _rpb_reduce(rpbacc, rows):
    nacc = 4 * 3 * 2

    def shift_body(a_ref, o_ref):
        acc = a_ref[0, 0:1, :]
        for cq in range(1, GRID_W):
            acc = acc + pltpu.roll(a_ref[0, cq:cq + 1, :], NA_KT - cq, 1)
        o_ref[0] = jnp.broadcast_to(acc, (8, NA_KT))

    vec = pl.pallas_call(
        shift_body, name="rpb_shift", grid=(nacc,),
        in_specs=[pl.BlockSpec((1, GRID_W, NA_KT), lambda a: (a, 0, 0))],
        out_specs=pl.BlockSpec((1, 8, NA_KT), lambda a: (a, 0, 0)),
        out_shape=jax.ShapeDtypeStruct((nacc, 8, NA_KT), F32),
        compiler_params=_cparams(("parallel",)),
    )(rpbacc.reshape(nacc, GRID_W, NA_KT))
    a = vec[:, 0].reshape(4, 3, 2, NA_KT).transpose(0, 2, 1, 3).reshape(NA_HEADS, 3, NA_KT)
    if rows // NA_QROWS < 3:
        a = a.at[:, 1].set(0.0)
    dd = np.arange(NA_KROWS)[:, None]
    dxo = np.arange(-(NA_KW - 1), NA_KW)[None, :]
    idx = ((dd * GRID_W + dxo) % NA_KT).reshape(-1)
    g = a[..., idx].reshape(NA_HEADS, 3 * NA_KROWS, 2 * NA_KW - 1)
    g = jnp.pad(g, ((0, 0), (0, 0), (0, 128 - (2 * NA_KW - 1))))
    nmat = np.zeros((16, 3 * NA_KROWS), np.float32)
    for cs, delta in enumerate((0, -(NA_KH // 2), -(NA_KROWS - NA_QROWS))):
        for d in range(NA_KROWS):
            jmi = d - NA_KROWS if (cs == 0 and d > NA_KH - 1) else d
            dy = jmi + delta + NA_KH - 1
            if 0 <= dy <= 2 * NA_KH - 2:
                nmat[dy, cs * NA_KROWS + d] = 1.0

    def body(n_ref, g_ref, o_ref):
        o_ref[0] = jnp.dot(n_ref[...], g_ref[0], precision=HI, preferred_element_type=F32)

    out = pl.pallas_call(
        body, name="rpb_reduce", grid=(NA_HEADS,),
        in_specs=[pl.BlockSpec((16, nmat.shape[1]), lambda h: (0, 0)),
                  pl.BlockSpec((1, nmat.shape[1], 128), lambda h: (h, 0, 0))],
        out_specs=pl.BlockSpec((1, 16, 128), lambda h: (h, 0, 0)),
        out_shape=jax.ShapeDtypeStruct((NA_HEADS, 16, 128), F32),
        compiler_params=_cparams(("parallel",)),
    )(jnp.asarray(nmat), g)
    return out[:, :2 * NA_KH - 1, :2 * NA_KW - 1]


def _halo_specs(tm, t, col, width=1024):
    nth = t // CONV_HALO
    per = tm // CONV_HALO
    return [pl.BlockSpec((tm, width), lambda i: (i, col)),
            pl.BlockSpec((CONV_HALO, width), lambda i: (jnp.maximum(i * per - 1, 0), col)),
            pl.BlockSpec((CONV_HALO, width), lambda i: (jnp.minimum((i + 1) * per, nth - 1), col))]


def _fill_ext(ext, cur_ref, prev_ref, next_ref, tm, nt):
    i = pl.program_id(0)
    hl = CONV_HALO
    ext[0:hl, :] = jnp.where(i == 0, 0.0, prev_ref[...].astype(F32))
    ext[hl:hl + tm, :] = cur_ref[...].astype(F32)
    ext[hl + tm:2 * hl + tm, :] = jnp.where(i == nt - 1, 0.0, next_ref[...].astype(F32))


CONV_HALO = 16
CONV_RC = 16
CONV_CB = 512


def _conv_chunks(tm):
    return [(slice(cb, cb + CONV_CB), slice(rb, rb + CONV_RC))
            for cb in range(0, 1024, CONV_CB) for rb in range(0, tm, CONV_RC)]


def _conv_fwd(proj, conv_w8, conv_b, tm):
    t = proj.shape[0]
    nt = t // tm

    def body(u_ref, up_ref, un_ref, w_ref, b_ref, pre_ref, act_ref, ext):
        _fill_ext(ext, u_ref, up_ref, un_ref, tm, nt)
        for cs, rs in _conv_chunks(tm):
            pre = b_ref[:, cs] + w_ref[0:1, cs] * ext[pl.ds(rs.start + CONV_HALO - 2, CONV_RC), cs]
            for j in range(1, CONV_W):
                pre = pre + w_ref[j:j + 1, cs] * ext[pl.ds(rs.start + CONV_HALO - 2 + j, CONV_RC), cs]
            pre_ref[rs, cs] = pre
            act_ref[rs, cs] = _silu(pre)

    full = pl.BlockSpec((tm, 1024), lambda i: (i, 0))
    o = jax.ShapeDtypeStruct((t, 1024), F32)
    return pl.pallas_call(
        body, name="conv_fwd", grid=(nt,),
        in_specs=_halo_specs(tm, t, 2) + [pl.BlockSpec((8, 1024), lambda i: (0, 0)), _row(1024)],
        out_specs=[full, full], out_shape=[o, o],
        scratch_shapes=[pltpu.VMEM((tm + 2 * CONV_HALO, 1024), F32)],
        compiler_params=_cparams(("parallel",)),
    )(proj, proj, proj, conv_w8, conv_b)


def _conv_bwd(dq, dk, pre, proj, conv_w8, tm):
    t = pre.shape[0]
    nt = t // tm

    def body(dq_ref, dqp_ref, dqn_ref, dk_ref, dkp_ref, dkn_ref, pre_ref, prep_ref, pren_ref,
             u_ref, up_ref, un_ref, w_ref, du_ref, gw_ref, gb_ref, extd, extu):
        i = pl.program_id(0)
        hl = CONV_HALO

        @pl.when(i == 0)
        def _():
            gw_ref[...] = jnp.zeros_like(gw_ref)
            gb_ref[...] = jnp.zeros_like(gb_ref)
        for rows, dqr, dkr, prr, edge in ((slice(0, hl), dqp_ref, dkp_ref, prep_ref, i == 0),
                                          (slice(hl, hl + tm), dq_ref, dk_ref, pre_ref, None),
                                          (slice(hl + tm, 2 * hl + tm), dqn_ref, dkn_ref, pren_ref, i == nt - 1)):
            ds = _dsilu(prr[...])
            dl = dqr[...] * ds[:, 0:ML_W]
            dr = dkr[...] * ds[:, ML_W:]
            if edge is not None:
                dl = jnp.where(edge, 0.0, dl)
                dr = jnp.where(edge, 0.0, dr)
            extd[rows, 0:ML_W] = dl
            extd[rows, ML_W:] = dr
        _fill_ext(extu, u_ref, up_ref, un_ref, tm, nt)
        gb_ref[...] += jnp.sum(extd[hl:hl + tm, :], axis=0, keepdims=True)
        gacc = None
        for cs, rs in _conv_chunks(tm):
            if rs.start == 0:
                gacc = [jnp.zeros((8, CONV_CB), F32) for _ in range(CONV_W)]
            du = w_ref[0:1, cs] * extd[pl.ds(rs.start + hl + 2, CONV_RC), cs]
            for j in range(1, CONV_W):
                du = du + w_ref[j:j + 1, cs] * extd[pl.ds(rs.start + hl + 2 - j, CONV_RC), cs]
            du_ref[rs, cs] = du.astype(BF16)
            dcur = extd[pl.ds(rs.start + hl, CONV_RC), cs]
            for j in range(CONV_W):
                prod = dcur * extu[pl.ds(rs.start + hl - 2 + j, CONV_RC), cs]
                gacc[j] = gacc[j] + functools.reduce(
                    jnp.add, [prod[k:k + 8] for k in range(0, CONV_RC, 8)])
            if rs.stop == tm:
                for j in range(CONV_W):
                    gw_ref[j:j + 1, cs] += jnp.sum(gacc[j], axis=0, keepdims=True)

    full = pl.BlockSpec((tm, 1024), lambda i: (i, 0))
    return pl.pallas_call(
        body, name="conv_bwd", grid=(nt,),
        in_specs=_halo_specs(tm, t, 0, ML_W) + _halo_specs(tm, t, 0, ML_W) + _halo_specs(tm, t, 0)
        + _halo_specs(tm, t, 2) + [pl.BlockSpec((8, 1024), lambda i: (0, 0))],
        out_specs=[full, pl.BlockSpec((8, 1024), lambda i: (0, 0)), _row(1024)],
        out_shape=[jax.ShapeDtypeStruct((t, 1024), BF16), jax.ShapeDtypeStruct((8, 1024), F32),
                   jax.ShapeDtypeStruct((1, 1024), F32)],
        scratch_shapes=[pltpu.VMEM((tm + 2 * CONV_HALO, 1024), F32), pltpu.VMEM((tm + 2 * CONV_HALO, 1024), F32)],
        compiler_params=_cparams(("arbitrary",)),
    )(dq, dq, dq, dk, dk, dk, pre, pre, pre, proj, proj, proj, conv_w8)


def _ml_consts(rev):
    iu = lax.broadcasted_iota(jnp.int32, (ML_CHUNK, ML_CHUNK), 0)
    js = lax.broadcasted_iota(jnp.int32, (ML_CHUNK, ML_CHUNK), 1)
    eye = iu == js
    le = iu <= js
    ge = iu >= js
    csum, csum_t, sees = (ge, le, ge) if rev else (le, ge, le)
    return eye, csum.astype(F32), csum_t.astype(F32), sees


def _col(row, eye):
    return jnp.sum(jnp.where(eye, row, 0.0), axis=1, keepdims=True)


def _rowof(col, eye):
    return jnp.sum(jnp.where(eye, col, 0.0), axis=0, keepdims=True)


def _row8(row):
    top = lax.broadcasted_iota(jnp.int32, (8, row.shape[1]), 0) == 0
    return jnp.where(top, row, jnp.zeros_like(row))


def _outer_rows(a_row, b_row_bf16):
    hi = a_row.astype(BF16)
    lo = (a_row - hi.astype(F32)).astype(BF16)
    r_a = lax.broadcasted_iota(jnp.int32, (8, a_row.shape[1]), 0)
    r_b = lax.broadcasted_iota(jnp.int32, (8, b_row_bf16.shape[1]), 0)
    lhs = jnp.where(r_a == 0, hi, jnp.where(r_a == 1, lo, jnp.zeros_like(hi)))
    rhs = jnp.where(r_b < 2, b_row_bf16, jnp.zeros_like(b_row_bf16))
    return _tn(lhs, rhs)


def _ml_gates(gi, gf, m0, csum, rev):
    lf = jax.nn.log_sigmoid(gf)
    b_rows = jnp.dot(lf, csum, precision=HI, preferred_element_type=F32)
    bl = jnp.sum(lf, axis=1, keepdims=True)
    a_rows = bl - b_rows + gi
    mloc = jnp.max(a_rows, axis=1, keepdims=True)
    order = list(range(ML_NB))[::-1] if rev else list(range(ML_NB))
    mp, mn, decay = {}, {}, {}
    m = m0
    for n in order:
        mp[n] = m
        m = jnp.maximum(bl[n:n + 1] + m, mloc[n:n + 1])
        mn[n] = m
    for n in order:
        decay[n] = jnp.exp(bl[n:n + 1] + mp[n] - mn[n])
    return b_rows, a_rows, gi - b_rows, mp, mn, decay, order


def _ml_load(q_ref, k_ref, v_ref, n):
    sl = slice(n * ML_CHUNK, (n + 1) * ML_CHUNK)
    qb = q_ref[sl, :].astype(BF16)
    kb = (k_ref[sl, :] * (ML_HEAD_DIM ** -0.5)).astype(BF16)
    vn = v_ref[sl, :].astype(F32)
    return sl, qb, kb, vn


def _ml_state_scan(q_ref, k_ref, v_ref, a_rows, mn, decay, order, c0, n0):
    ns = range(ML_NB)
    ld = [_ml_load(q_ref, k_ref, v_ref, n) for n in ns]
    vt = [ld[n][3].T for n in ns]
    w_row = [jnp.exp(a_rows[n:n + 1] - mn[n]) for n in ns]
    u = [_nn((vt[n] * w_row[n]).astype(BF16), ld[n][2]) for n in ns]
    nu = [_nn(_row8(w_row[n]).astype(BF16), ld[n][2])[0:1] for n in ns]
    cp, npv = {}, {}
    c, nv = c0, n0
    for n in order:
        cp[n], npv[n] = c, nv
        c = decay[n] * c + u[n]
        nv = decay[n] * nv + nu[n]
    return ld, vt, cp, npv, w_row, c, nv


def _ml_intra_all(ld, vt, b_rows, imb_rows, mp, cp, npv, sees, eye):
    ns = range(ML_NB)
    qk = [_nt(ld[n][2], ld[n][1]) for n in ns]
    cq = [_nt(cp[n].astype(BF16), ld[n][1]) for n in ns]
    qn = [_nt(_row8(npv[n]).astype(BF16), ld[n][1])[0:1] for n in ns]
    imb_col = [_col(imb_rows[n:n + 1], eye) for n in ns]
    dlog = [jnp.where(sees, b_rows[n:n + 1] + imb_col[n], NEG) for n in ns]
    m_inter = [b_rows[n:n + 1] + mp[n] for n in ns]
    m_t = [jnp.maximum(m_inter[n], jnp.max(dlog[n], axis=0, keepdims=True)) for n in ns]
    pm = [jnp.exp(dlog[n] - m_t[n]) for n in ns]
    inter = [jnp.exp(m_inter[n] - m_t[n]) for n in ns]
    floor = [jnp.exp(-m_t[n]) for n in ns]
    s = [qk[n] * pm[n] for n in ns]
    sv = [_nn(vt[n].astype(BF16), s[n].astype(BF16)) for n in ns]
    den = [jnp.sum(s[n], axis=0, keepdims=True) + inter[n] * qn[n] for n in ns]
    num = [sv[n] + inter[n] * cq[n] for n in ns]
    dn = [jnp.maximum(jnp.abs(den[n]), floor[n]) for n in ns]
    return [dict(pm=pm[n], s=s[n], inter=inter[n], cq=cq[n], qn=qn[n], num=num[n], den=den[n],
                 floor=floor[n], dn=dn[n]) for n in ns]


def _ml_specs(t, rev):
    nblk = t // ML_TB
    blk = (lambda g: nblk - 1 - g) if rev else (lambda g: g)
    hps = ML_HPS
    tile = lambda c0: pl.BlockSpec((ML_TB, 128 * hps), lambda hg, g, c0=c0: (blk(g), c0 // hps + hg))
    gate = pl.BlockSpec((hps, ML_NB, ML_CHUNK), lambda hg, g: (hg, blk(g), 0))
    cchk = pl.BlockSpec((hps, 1, 128, 128), lambda hg, g: (hg, blk(g), 0, 0))
    nmchk = pl.BlockSpec((hps, 1, 8, 128), lambda hg, g: (hg, blk(g), 0, 0))
    return nblk, blk, tile, gate, cchk, nmchk


def _ml_head_views(refs, hh):
    cols = slice(hh * ML_HEAD_DIM, (hh + 1) * ML_HEAD_DIM)
    return [r.at[:, cols] if len(r.shape) == 2 else r.at[hh] for r in refs]


def _ml_fwd(qk_act, proj, gi, gf, rev, name):
    t = qk_act.shape[0]
    nblk, _, tile, gate, cchk, nmchk = _ml_specs(t, rev)

    def body(*refs):
        for hh in range(ML_HPS):
            one_head(*_ml_head_views(refs, hh))

    def one_head(q_ref, k_ref, v_ref, gi_ref, gf_ref, h_ref, cchk_ref, nmchk_ref, c_ref, nm_ref):
        @pl.when(pl.program_id(1) == 0)
        def _():
            c_ref[...] = jnp.zeros_like(c_ref)
            nm_ref[...] = jnp.zeros_like(nm_ref)
        cchk_ref[0] = c_ref[...]
        nmchk_ref[0] = nm_ref[...]
        eye, csum, _, sees = _ml_consts(rev)
        b_rows, a_rows, imb_rows, mp, mn, decay, order = _ml_gates(
            gi_ref[...], gf_ref[...], nm_ref[1:2, 0:1], csum, rev)
        ld, vt, cp, npv, _, c, nv = _ml_state_scan(q_ref, k_ref, v_ref, a_rows, mn, decay, order,
                                                   c_ref[...], nm_ref[0:1, :])
        c_ref[...] = c
        nm_ref[0:1, :] = nv
        nm_ref[1:2, :] = jnp.broadcast_to(mn[order[-1]], (1, 128))
        rs = _ml_intra_all(ld, vt, b_rows, imb_rows, mp, cp, npv, sees, eye)
        ht = [rs[n]['num'] / rs[n]['dn'] for n in range(ML_NB)]
        for n in range(ML_NB):
            h_ref[n * ML_CHUNK:(n + 1) * ML_CHUNK, :] = ht[n].T

    return pl.pallas_call(
        body, name=name, grid=(ML_HEADS // ML_HPS, nblk),
        in_specs=[tile(0), tile(4), tile(24), gate, gate],
        out_specs=[tile(0), cchk, nmchk],
        out_shape=[jax.ShapeDtypeStruct((t, ML_W), F32),
                   jax.ShapeDtypeStruct((ML_HEADS, nblk, 128, 128), F32),
                   jax.ShapeDtypeStruct((ML_HEADS, nblk, 8, 128), F32)],
        scratch_shapes=[pltpu.VMEM((ML_HPS, 128, 128), F32), pltpu.VMEM((ML_HPS, 8, 128), F32)],
        compiler_params=_cparams(("parallel", "arbitrary")),
    )(qk_act, qk_act, proj, gi, gf)


def _ml_bwd(qk_act, proj, gi, gf, dh, cchk_a, nmchk_a, prev, rev, name):
    t = qk_act.shape[0]
    nblk, _, tile, gate, cchk, nmchk = _ml_specs(t, not rev)

    def body(*refs):
        for hh in range(ML_HPS):
            one_head(*_ml_head_views(refs, hh))

    def one_head(q_ref, k_ref, v_ref, gi_ref, gf_ref, dh_ref, cchk_ref, nmchk_ref, *rest):
        prev_refs = rest[:len(prev)]
        dq_ref, dk_ref, dv_ref, dgi_ref, dgf_ref, dc_ref, dn_ref, db_scr, dbl_scr, di_scr = rest[len(prev):]

        def plus_prev(val, which, rows):
            return val + prev_refs[which][rows, :] if prev else val

        @pl.when(pl.program_id(1) == 0)
        def _():
            dc_ref[...] = jnp.zeros_like(dc_ref)
            dn_ref[...] = jnp.zeros_like(dn_ref)
        eye, csum, csum_t, sees = _ml_consts(rev)
        gfv = gf_ref[...]
        b_rows, a_rows, imb_rows, mp, mn, decay, order = _ml_gates(
            gi_ref[...], gfv, nmchk_ref[0, 1:2, 0:1], csum, rev)
        ld, vt, cp, npv, w_row, _, _ = _ml_state_scan(q_ref, k_ref, v_ref, a_rows, mn, decay, order,
                                                      cchk_ref[0], nmchk_ref[0, 0:1, :])
        ns = range(ML_NB)
        rs = _ml_intra_all(ld, vt, b_rows, imb_rows, mp, cp, npv, sees, eye)
        sls = [ld[n][0] for n in ns]
        qbs = [ld[n][1] for n in ns]
        kbs = [ld[n][2] for n in ns]
        vbs = [ld[n][3].astype(BF16) for n in ns]
        rdn = [1.0 / rs[n]['dn'] for n in ns]
        dnum = [dh_ref[sls[n], :].T * rdn[n] for n in ns]
        hsum = [jnp.sum(dnum[n] * rs[n]['num'], axis=0, keepdims=True) for n in ns]
        dden = [jnp.where(jnp.abs(rs[n]['den']) > rs[n]['floor'],
                          -hsum[n] * rdn[n] * jnp.sign(rs[n]['den']), 0.0) for n in ns]
        dnb = [dnum[n].astype(BF16) for n in ns]
        dsf = [_nn(vbs[n], dnb[n]) + dden[n] for n in ns]
        dv0 = [_nt(rs[n]['s'].astype(BF16), dnb[n]) for n in ns]
        gb = [(dsf[n] * rs[n]['pm']).astype(BF16) for n in ns]
        cpb = [cp[n].astype(BF16) for n in ns]
        idd = [rs[n]['inter'] * dden[n] for n in ns]
        idn = [(rs[n]['inter'] * dnum[n]).astype(BF16) for n in ns]
        dqa = [_tn(gb[n], kbs[n]) for n in ns]
        dqc = [_tn(idn[n], cpb[n]) for n in ns]
        dqn = [_outer_rows(idd[n], npv[n].astype(BF16)) for n in ns]
        dk0 = [_nn(gb[n], qbs[n]) for n in ns]
        xs = [_nn(idn[n], qbs[n]) for n in ns]
        for n in ns:
            dq_ref[sls[n], :] = plus_prev(dqa[n] + dqc[n] + dqn[n], 0, sls[n])
        rr = [dsf[n] * rs[n]['s'] for n in ns]
        dinter = [jnp.sum(dnum[n] * rs[n]['cq'], axis=0, keepdims=True) + dden[n] * rs[n]['qn'] for n in ns]
        dbt = [jnp.sum(rr[n], axis=0, keepdims=True) + dinter[n] * rs[n]['inter'] for n in ns]
        dimb = [jnp.sum(rr[n], axis=1, keepdims=True) for n in ns]
        xns = [_nn(_row8(idd[n]).astype(BF16), qbs[n])[0:1] for n in ns]
        dcn, dnn = {}, {}
        dc, dn = dc_ref[...], dn_ref[0:1, :]
        for n in order[::-1]:
            dcn[n], dnn[n] = dc, dn
            dc = decay[n] * dc + xs[n]
            dn = decay[n] * dn + xns[n]
        dc_ref[...] = dc
        dn_ref[0:1, :] = dn
        kscale = ML_HEAD_DIM ** -0.5
        dcb = [dcn[n].astype(BF16) for n in ns]
        z = [_nn(vbs[n], dcb[n]) for n in ns]
        kd = [_nt(kbs[n], dcb[n]) for n in ns]
        ddecay = [jnp.sum(jnp.sum(dcn[n] * cp[n], axis=1, keepdims=True), axis=0, keepdims=True)
                  + jnp.sum(dnn[n] * npv[n], axis=1, keepdims=True) for n in ns]
        zd = [z[n] + dnn[n] for n in ns]
        dw = [jnp.sum(zd[n] * kbs[n].astype(F32), axis=1, keepdims=True) for n in ns]
        wcol = [_col(w_row[n], eye) for n in ns]
        for n in ns:
            dv_ref[sls[n], :] = plus_prev(dv0[n] + wcol[n] * kd[n], 2, sls[n]).astype(dv_ref.dtype)
            dk_ref[sls[n], :] = plus_prev((dk0[n] + wcol[n] * zd[n]) * kscale, 1, sls[n])
        da = [dw[n] * wcol[n] for n in ns]
        dbl = [jnp.sum(da[n], axis=0, keepdims=True) + ddecay[n] * decay[n] for n in ns]
        key_row = [_rowof(dimb[n] + da[n], eye) for n in ns]
        for n in ns:
            db_scr[n:n + 1, :] = dbt[n] - key_row[n]
            di_scr[n:n + 1, :] = key_row[n]
            dbl_scr[n:n + 1, :] = jnp.broadcast_to(dbl[n], (1, ML_CHUNK))
        dlf = jnp.dot(db_scr[...], csum_t, precision=HI, preferred_element_type=F32) + dbl_scr[...]
        dgf_ref[...] = dlf * jax.nn.sigmoid(-gfv)
        dgi_ref[...] = di_scr[...]

    nc = t // ML_CHUNK
    o512 = jax.ShapeDtypeStruct((t, ML_W), F32)
    og = jax.ShapeDtypeStruct((ML_HEADS, nc, ML_CHUNK), F32)
    return pl.pallas_call(
        body, name=name, grid=(ML_HEADS // ML_HPS, nblk),
        in_specs=[tile(0), tile(4), tile(24), gate, gate, tile(0), cchk, nmchk] + [tile(0)] * len(prev),
        out_specs=[tile(0), tile(0), tile(0), gate, gate],
        out_shape=[o512, o512, jax.ShapeDtypeStruct((t, ML_W), BF16 if prev else F32), og, og],
        scratch_shapes=[pltpu.VMEM((ML_HPS, 128, 128), F32), pltpu.VMEM((ML_HPS, 8, 128), F32)]
        + [pltpu.VMEM((ML_HPS, ML_NB, ML_CHUNK), F32)] * 3,
        compiler_params=_cparams(("parallel", "arbitrary")),
    )(qk_act, qk_act, proj, gi, gf, dh, cchk_a, nmchk_a, *prev)


def _gate_rows(gates16, t):
    g = gates16.reshape(t // ML_CHUNK, ML_CHUNK, 4, ML_HEADS).transpose(2, 3, 0, 1)
    return g[0], g[1], g[2], g[3]


def _gate_cols(dgi_f, dgf_f, dgi_b, dgf_b, t):
    g = jnp.stack([dgi_f, dgf_f, dgi_b, dgf_b]).transpose(2, 3, 0, 1).reshape(t, 4 * ML_HEADS)
    return jnp.pad(g, ((0, 0), (0, 128 - 4 * ML_HEADS)))


def _local_step(x, target, shift, scale, gate, norm_w, w_in_t, b_in_p, conv_w8, conv_b, rpb,
                ml_norm_w, w_out_b, final_norm_w):
    t = x.shape[0]
    rows = t // GRID_W
    tm = 512
    proj, gates = _in_proj(x, norm_w, scale, shift, w_in_t, b_in_p)
    tbl = _na_bias_table(rpb, rows)
    o_na, lse_na = _na_fwd(proj, tbl)
    pre, qk_act = _conv_fwd(proj, conv_w8, conv_b, tm)
    gi_f, gf_f, gi_b, gf_b = _gate_rows(gates[:, :4 * ML_HEADS], t)
    h_f, cchk_f, nmchk_f = _ml_fwd(qk_act, proj, gi_f, gf_f, False, "ml_fwd_f")
    h_b, cchk_b, nmchk_b = _ml_fwd(qk_act, proj, gi_b, gf_b, True, "ml_fwd_b")
    (loss, dres, d_ona, d_naz, dhs, d_o, d_z, dgate, g_fnw, g_mlnw, g_w_out) = _tail(
        o_na, proj, h_f, h_b, x, target, gate, ml_norm_w, final_norm_w, w_out_b)
    dq_na, dk_na, dv_na, rpbacc = _na_bwd(proj, tbl, d_ona, o_na, lse_na)
    g_rpb = _rpb_reduce(rpbacc, rows)
    dq_f, dk_f, dv_f, dgi_f, dgf_f = _ml_bwd(qk_act, proj, gi_f, gf_f, dhs, cchk_f, nmchk_f, (),
                                             False, "ml_bwd_f")
    dq_ml, dk_ml, dv_ml, dgi_b, dgf_b = _ml_bwd(qk_act, proj, gi_b, gf_b, dhs, cchk_b, nmchk_b, (dq_f, dk_f, dv_f),
                                                True, "ml_bwd_b")
    du, g_conv_w, g_conv_b = _conv_bwd(dq_ml, dk_ml, pre, proj, conv_w8, tm)
    dgates = _gate_cols(dgi_f, dgf_f, dgi_b, dgf_b, t)
    grad_x, g_w_in, g_b_in, dscale, dshift, g_nw = _in_bwd(
        [dq_na, dk_na, dv_na, d_naz, du, dv_ml, d_o, d_z, dgates], x, dres, w_in_t, norm_w, scale, shift)
    dmod = jnp.concatenate([dshift, dscale, dgate], axis=1)
    return (loss, grad_x, dmod, g_nw, g_w_in, g_b_in, g_conv_w, g_conv_b, g_rpb, g_mlnw, g_w_out, g_fnw)


MESH = pl.DeviceIdType.MESH
N_DEV = 8
ANY = pl.BlockSpec(memory_space=pl.ANY)
WHOLE_VMEM = pl.BlockSpec(memory_space=pltpu.VMEM)


def _allgather8(blocks, name):
    na = len(blocks)

    def body(*refs):
        x_refs = refs[:na]
        out_refs = refs[na:2 * na]
        send_sems, recv_sems, local_sems = refs[2 * na:]
        x, y, c = lax.axis_index("x"), lax.axis_index("y"), lax.axis_index("c")
        me, sibling = (x, y, c), (x, y, 1 - c)
        chips = [(1 - x, y), (x, 1 - y), (1 - x, 1 - y)]

        def rows(a, px, py, pc):
            return out_refs[a].at[4 * px + 2 * py + pc]

        def copy(a, k, block, to, src=None):
            return pltpu.make_async_remote_copy(
                src_ref=rows(a, *block) if src is None else src, dst_ref=rows(a, *block),
                send_sem=send_sems.at[a, k], recv_sem=recv_sems.at[a, k],
                device_id=to, device_id_type=MESH)

        mine, first, passed = [], [], []
        for a in range(na):
            cp = pltpu.make_async_copy(x_refs[a], rows(a, *me), local_sems.at[a])
            cp.start()
            mine.append(cp)
            first.append(copy(a, 0, me, sibling, src=x_refs[a]))
            first += [copy(a, 1 + j, me, (*chip, c), src=x_refs[a]) for j, chip in enumerate(chips)]
        for cp in first:
            cp.start()
        for a in range(na):
            for j, chip in enumerate(chips):
                copy(a, 1 + j, (*chip, c), me).wait_recv()
                fwd = copy(a, 4 + j, (*chip, c), sibling)
                fwd.start()
                passed.append(fwd)
        for a in range(na):
            copy(a, 0, sibling, me).wait_recv()
            for j, chip in enumerate(chips):
                copy(a, 4 + j, (*chip, 1 - c), me).wait_recv()
        for cp in first + passed:
            cp.wait_send()
        for cp in mine:
            cp.wait()

    return pl.pallas_call(
        body, name=name,
        out_shape=[jax.ShapeDtypeStruct((N_DEV,) + b.shape, b.dtype) for b in blocks],
        in_specs=[WHOLE_VMEM] * na, out_specs=[WHOLE_VMEM] * na,
        scratch_shapes=[pltpu.SemaphoreType.DMA((na, 7)), pltpu.SemaphoreType.DMA((na, 7)),
                        pltpu.SemaphoreType.DMA((na,))],
        compiler_params=pltpu.CompilerParams(vmem_limit_bytes=VMEM_LIMIT),
    )(*blocks)


def _pair_exchange(arrs, name):
    na = len(arrs)

    def body(*refs):
        in_refs = refs[:na]
        out_refs = refs[na:2 * na]
        send_sems, recv_sems = refs[2 * na:]
        sibling = (lax.axis_index("x"), lax.axis_index("y"), 1 - lax.axis_index("c"))
        copies = [pltpu.make_async_remote_copy(
            src_ref=in_refs[a], dst_ref=out_refs[a], send_sem=send_sems.at[a], recv_sem=recv_sems.at[a],
            device_id=sibling, device_id_type=MESH) for a in range(na)]
        for cp in copies:
            cp.start()
        for cp in copies:
            cp.wait()

    return pl.pallas_call(
        body, name=name,
        out_shape=[jax.ShapeDtypeStruct(a.shape, a.dtype) for a in arrs],
        in_specs=[ANY] * na, out_specs=[ANY] * na,
        scratch_shapes=[pltpu.SemaphoreType.DMA((na,)), pltpu.SemaphoreType.DMA((na,))],
    )(*arrs)


def _chip_exchange(arrs, name):
    na = len(arrs)

    def body(*refs):
        in_refs = refs[:na]
        out_refs = refs[na:2 * na]
        send_sems, recv_sems, local_sems = refs[2 * na:]
        x, y, c = lax.axis_index("x"), lax.axis_index("y"), lax.axis_index("c")
        my_chip = 2 * x + y
        chips = [(1 - x, y), (x, 1 - y), (1 - x, 1 - y)]
        local, remote = [], []
        for a in range(na):
            cp = pltpu.make_async_copy(in_refs[a].at[my_chip], out_refs[a].at[my_chip], local_sems.at[a])
            cp.start()
            local.append(cp)
            for j, (px, py) in enumerate(chips):
                cp = pltpu.make_async_remote_copy(
                    src_ref=in_refs[a].at[2 * px + py], dst_ref=out_refs[a].at[my_chip],
                    send_sem=send_sems.at[a, j], recv_sem=recv_sems.at[a, j],
                    device_id=(px, py, c), device_id_type=MESH)
                cp.start()
                remote.append(cp)
        for cp in remote:
            cp.wait()
        for cp in local:
            cp.wait()

    return pl.pallas_call(
        body, name=name,
        out_shape=[jax.ShapeDtypeStruct(a.shape, a.dtype) for a in arrs],
        in_specs=[ANY] * na, out_specs=[ANY] * na,
        scratch_shapes=[pltpu.SemaphoreType.DMA((na, 3)), pltpu.SemaphoreType.DMA((na, 3)),
                        pltpu.SemaphoreType.DMA((na,))],
    )(*arrs)


def _rows_tile(r):
    for cand in (512, 256, 128, 64, 32, 16, 8):
        if r % cand == 0:
            return cand
    return r


def _add2(a, b, name, out_dtype):
    s, r, n = a.shape
    tr = _rows_tile(r)

    def body(a_ref, b_ref, o_ref):
        o_ref[...] = (a_ref[...] + b_ref[...]).astype(out_dtype)

    spec = pl.BlockSpec((1, tr, n), lambda i, j: (i, j, 0))
    return pl.pallas_call(
        body, name=name, grid=(s, r // tr), in_specs=[spec, spec], out_specs=spec,
        out_shape=jax.ShapeDtypeStruct(a.shape, out_dtype),
        compiler_params=_cparams(("parallel", "parallel")),
    )(a, b)


def _sum_slabs(a, name):
    s, r, n = a.shape
    tr = _rows_tile(r)

    def body(a_ref, o_ref):
        acc = a_ref[0].astype(F32)
        for k in range(1, s):
            acc = acc + a_ref[k].astype(F32)
        o_ref[...] = acc

    return pl.pallas_call(
        body, name=name, grid=(r // tr,),
        in_specs=[pl.BlockSpec((s, tr, n), lambda i: (0, i, 0))],
        out_specs=pl.BlockSpec((tr, n), lambda i: (i, 0)),
        out_shape=jax.ShapeDtypeStruct((r, n), F32),
        compiler_params=_cparams(("parallel",)),
    )(a)


ADAMW_WHOLE = 64 * 1024


def _adamw(w, g, m, v, name):
    r, n = w.shape
    if r * n <= ADAMW_WHOLE:
        blk, grid, imap = (r, n), (1,), (lambda i: (0, 0))
    elif r % 8 == 0:
        blk, grid, imap = (_rows_tile(r), n), (r // _rows_tile(r),), (lambda i: (i, 0))
    else:
        blk, grid, imap = (r, 128), (n // 128,), (lambda i: (0, i))
    c1 = 1.0 / (1.0 - ADAM_B1 ** ADAM_STEP)
    c2 = 1.0 / (1.0 - ADAM_B2 ** ADAM_STEP)

    def body(w_ref, g_ref, m_ref, v_ref, d_ref, nm_ref, nv_ref):
        gv = g_ref[...]
        nm = ADAM_B1 * m_ref[...] + (1.0 - ADAM_B1) * gv
        nv = ADAM_B2 * v_ref[...] + (1.0 - ADAM_B2) * (gv * gv)
        nm_ref[...] = nm
        nv_ref[...] = nv
        d_ref[...] = -ADAM_LR * ((nm * c1) / (jnp.sqrt(nv * c2) + ADAM_EPS) + ADAM_WD * w_ref[...])

    spec = pl.BlockSpec(blk, imap)
    o = jax.ShapeDtypeStruct((r, n), F32)
    return pl.pallas_call(
        body, name=name, grid=grid, in_specs=[spec] * 4, out_specs=[spec] * 3, out_shape=[o, o, o],
        compiler_params=_cparams(("parallel",)),
    )(w, g, m, v)


def _mod_fwd(c_all, w_ada_s, b_ada_s):
    def body(c_ref, w_ref, b_ref, o_ref):
        o_ref[...] = jnp.dot(_silu(c_ref[...]), w_ref[...], precision=HI, preferred_element_type=F32) + b_ref[...]

    return pl.pallas_call(
        body, name="mod_fwd", out_shape=jax.ShapeDtypeStruct((c_all.shape[0], w_ada_s.shape[1]), F32),
        in_specs=[WHOLE_VMEM] * 3, out_specs=WHOLE_VMEM,
        compiler_params=pltpu.CompilerParams(vmem_limit_bytes=VMEM_LIMIT),
    )(c_all, w_ada_s, b_ada_s)


def _wada_grad(c_all, dmod_s):
    def body(c_ref, d_ref, o_ref):
        o_ref[...] = lax.dot_general(_silu(c_ref[...]), d_ref[...], (((0,), (0,)), ((), ())),
                                     precision=HI, preferred_element_type=F32)

    return pl.pallas_call(
        body, name="w_ada_grad", out_shape=jax.ShapeDtypeStruct((c_all.shape[1], dmod_s.shape[1]), F32),
        in_specs=[WHOLE_VMEM] * 2, out_specs=WHOLE_VMEM,
        compiler_params=pltpu.CompilerParams(vmem_limit_bytes=VMEM_LIMIT),
    )(c_all, dmod_s)


SMALL_ROWS = 24


def _pad_rows(v, nrows):
    v = v.reshape(-1)
    return jnp.pad(v, (0, nrows * 1024 - v.shape[0])).reshape(nrows, 1024)


def _pack_small(b_ada, norm_w, b_in, conv_w_full, conv_b, rpb, ml_norm_w, final_norm_w, last):
    parts = [_pad_rows(b_ada, 3), _pad_rows(norm_w, 1), _pad_rows(b_in, 5), _pad_rows(conv_w_full, 5),
             _pad_rows(conv_b, 1), _pad_rows(rpb, 4), _pad_rows(ml_norm_w, 1), _pad_rows(final_norm_w, 1),
             _pad_rows(last, 3)]
    return jnp.concatenate(parts, axis=0)


def _unpack_small(p):
    return dict(b_ada=p[0:3].reshape(1, 3072), norm_w=p[3:4], b_in=p[4:9].reshape(-1)[:IN_W].reshape(1, IN_W),
                conv_w=p[9:14], conv_b=p[14:15],
                rpb=p[15:19].reshape(-1)[:NA_HEADS * 15 * 31].reshape(1, NA_HEADS, 15, 31),
                ml_norm_w=p[19:20, :ML_W], final_norm_w=p[20], last=p[21])


def kernel(x, c, w_ada, b_ada, norm_w, w_in, b_in, conv_w, conv_b, rpb, ml_norm_w, w_out, final_norm_w, loss_target, m_w_ada, m_b_ada, m_norm_w, m_w_in, m_b_in, m_conv_w, m_conv_b, m_rpb, m_ml_norm_w, m_w_out, m_final_norm_w, v_w_ada, v_b_ada, v_norm_w, v_w_in, v_b_in, v_conv_w, v_conv_b, v_rpb, v_ml_norm_w, v_w_out, v_final_norm_w):
    xi, yi, ci = lax.axis_index("x"), lax.axis_index("y"), lax.axis_index("c")
    chip = 2 * xi + yi
    dev = 2 * chip + ci
    t = x.shape[1]
    ada_n = w_ada.shape[2]
    in_n = w_in.shape[2]
    out_r = w_out.shape[1]

    c_blk = jnp.pad(c, ((0, 7), (0, 0)))
    w_in_t, m_w_in_t, v_w_in_t = w_in[0].T, m_w_in[0].T, v_w_in[0].T
    in_h = in_n // 2
    w_in_half = lax.dynamic_slice_in_dim(w_in_t, ci * in_h, in_h, axis=0).astype(BF16)
    w_out_half = lax.dynamic_slice_in_dim(w_out[0], ci * (out_r // 2), out_r // 2, axis=0).astype(BF16)
    conv_blk = jnp.pad(conv_w[0], ((0, 3), (0, 0)))
    c_g, conv_g, w_in_g, w_out_g = _allgather8([c_blk, conv_blk, w_in_half, w_out_half], "gather_c_weights")
    c_all = c_g[:, 0]
    w_out_g = w_out_g.reshape(D_MODEL, D_MODEL)
    b_ada_s = lax.dynamic_slice_in_dim(b_ada, chip * ada_n, ada_n, axis=1)
    mod_s = _mod_fwd(c_all, w_ada[0], b_ada_s)
    (mod_g,) = _allgather8([mod_s], "gather_mod")
    mod_mine = lax.dynamic_index_in_dim(mod_g, dev, axis=1, keepdims=False)
    mod = mod_mine[0::2].reshape(1, 3 * D_MODEL)
    shift, scale, gate = mod[:, :D_MODEL], mod[:, D_MODEL:2 * D_MODEL], mod[:, 2 * D_MODEL:]

    w_in_tp = jnp.pad(w_in_g.reshape(IN_W, D_MODEL), ((0, IN_PAD - IN_W), (0, 0)))
    b_in_p = jnp.pad(b_in, ((0, 0), (0, IN_PAD - IN_W)))
    conv_w8 = conv_g.reshape(4, 2, 8, conv_w.shape[2])[:, 0].transpose(1, 0, 2).reshape(8, D_MODEL)

    (loss, grad_x, dmod, g_nw, g_w_in, g_b_in, g_conv_w, g_conv_b, g_rpb, g_mlnw, g_w_out, g_fnw) = _local_step(
        x[0], loss_target[0], shift, scale, gate, norm_w, w_in_tp, b_in_p, conv_w8, conv_b, rpb[0],
        ml_norm_w, w_out_g, final_norm_w.reshape(1, D_MODEL))

    g_in_t = g_w_in

    def halves(a, per_chip, h):
        return jnp.stack([lax.dynamic_slice_in_dim(a, k * per_chip + h * (per_chip // 2), per_chip // 2, axis=0)
                          for k in range(4)])

    ri, ro = _pair_exchange([halves(g_in_t, in_n, 1 - ci), halves(g_w_out, out_r, 1 - ci)], "rs_pair")
    pi = _add2(halves(g_in_t, in_n, ci), ri, "rs_pair_add_in", BF16)
    po = _add2(halves(g_w_out, out_r, ci), ro, "rs_pair_add_out", BF16)
    qi, qo = _chip_exchange([pi, po], "rs_chips")
    si = _sum_slabs(qi, "rs_sum_in")
    so = _sum_slabs(qo, "rs_sum_out")
    ti, to = _pair_exchange([si, so], "rs_share")
    g_w_in_s = jnp.where(ci == 0, jnp.concatenate([si, ti], axis=0), jnp.concatenate([ti, si], axis=0))
    g_w_out_s = jnp.where(ci == 0, jnp.concatenate([so, to], axis=0), jnp.concatenate([to, so], axis=0))

    small = _pack_small(dmod, g_nw, g_b_in[:, :IN_W], g_conv_w[:CONV_W], g_conv_b, g_rpb, g_mlnw, g_fnw,
                        jnp.pad(loss, ((0, 0), (0, 1024 - 128))))
    (small_g,) = _allgather8([small], "gather_small")
    small_sum = _sum_slabs(small_g, "small_sum")
    gs = _unpack_small(small_sum)
    dmod_all = small_g[:, 0:3].reshape(N_DEV, 3 * D_MODEL)
    g_w_ada_s = _wada_grad(c_all, lax.dynamic_slice_in_dim(dmod_all, chip * ada_n, ada_n, axis=1))
    g_conv_w_s = lax.dynamic_slice_in_dim(gs['conv_w'], chip * conv_w.shape[2], conv_w.shape[2], axis=1)
    loss_total = gs['last'][0]

    small_names = ('b_ada', 'norm_w', 'b_in', 'conv_b', 'rpb', 'ml_norm_w', 'final_norm_w')
    small_w = (b_ada, norm_w, b_in, conv_b, rpb, ml_norm_w, final_norm_w)
    small_m = (m_b_ada, m_norm_w, m_b_in, m_conv_b, m_rpb, m_ml_norm_w, m_final_norm_w)
    small_v = (v_b_ada, v_norm_w, v_b_in, v_conv_b, v_rpb, v_ml_norm_w, v_final_norm_w)
    ds_, nms, nvs = {}, {}, {}
    for nm_, w_, m_, v_ in zip(small_names, small_w, small_m, small_v):
        two_d = (NA_HEADS, w_.size // NA_HEADS) if nm_ == 'rpb' else (1, w_.size)
        outs = _adamw(w_.reshape(two_d), gs[nm_].reshape(two_d), m_.reshape(two_d), v_.reshape(two_d),
                      "adamw_" + nm_)
        ds_[nm_], nms[nm_], nvs[nm_] = [o.reshape(w_.shape) for o in outs]
    d_ada, nm_ada, nv_ada = _adamw(w_ada[0], g_w_ada_s, m_w_ada[0], v_w_ada[0], "adamw_w_ada")
    d_in, nm_in, nv_in = _adamw(w_in_t, g_w_in_s, m_w_in_t, v_w_in_t, "adamw_w_in")
    d_out, nm_out, nv_out = _adamw(w_out[0], g_w_out_s, m_w_out[0], v_w_out[0], "adamw_w_out")
    d_cw, nm_cw, nv_cw = _adamw(conv_w[0], g_conv_w_s, m_conv_w[0], v_conv_w[0], "adamw_conv_w")

    def group(big_ada, big_in, big_out, cw, sm):
        return (big_ada[None], sm['b_ada'], sm['norm_w'], big_in.T[None], sm['b_in'], cw[None], sm['conv_b'],
                sm['rpb'], sm['ml_norm_w'], big_out[None], sm['final_norm_w'])

    return ((loss_total, grad_x[None])
            + group(g_w_ada_s, g_w_in_s, g_w_out_s, g_conv_w_s, gs)
            + group(d_ada, d_in, d_out, d_cw, ds_)
            + group(nm_ada, nm_in, nm_out, nm_cw, nms)
            + group(nv_ada, nv_in, nv_out, nv_cw, nvs))
```

```python
import functools

import numpy as np
import jax
import jax.numpy as jnp
from jax import lax
from jax.experimental import pallas as pl
from jax.experimental.pallas import tpu as pltpu

F32 = jnp.float32
BF16 = jnp.bfloat16
HI = lax.Precision.HIGHEST

D_MODEL = 1024
GRID_W = 64
NA_W = 512
NA_HEAD_DIM = 64
NA_HEADS = 8
NA_KH = 8
NA_KW = 16
ML_W = 512
ML_HEADS = 4
ML_HEAD_DIM = 128
ML_CHUNK = 128
CONV_W = 5
EPS = 1e-6
IN_W = 4 * NA_W + 5 * ML_W + 4 * ML_HEADS
IN_MAIN = 4 * NA_W + 5 * ML_W
IN_PAD = IN_MAIN + 128
NEG = -1e30

ADAM_LR = 0.001
ADAM_B1 = 0.9
ADAM_B2 = 0.999
ADAM_EPS = 1e-08
ADAM_WD = 0.01
ADAM_STEP = 10

NA_QROWS = 8
NA_KROWS = 16
NA_QT = NA_QROWS * GRID_W
NA_KT = NA_KROWS * GRID_W
NA_KCH = 256
NA_RC = 32
ML_NB = 32
ML_TB = ML_NB * ML_CHUNK
ML_HPS = 1

VMEM_LIMIT = 56 * 1024 * 1024
IN_BWD_VMEM_LIMIT = 60 * 1024 * 1024


def _cparams(sem, vmem=VMEM_LIMIT):
    return pltpu.CompilerParams(dimension_semantics=sem, vmem_limit_bytes=vmem)


def _silu(x):
    return x * jax.nn.sigmoid(x)


def _dsilu(x):
    s = jax.nn.sigmoid(x)
    return s * (1.0 + x * (1.0 - s))


def _dot(a, b, dims):
    return lax.dot_general(a, b, (dims, ((), ())), preferred_element_type=F32)


def _nn(a, b):
    return _dot(a, b, ((1,), (0,)))


def _nt(a, b):
    return _dot(a, b, ((1,), (1,)))


def _tn(a, b):
    return _dot(a, b, ((0,), (0,)))


def _row(n):
    return pl.BlockSpec((1, n), lambda i: (0, 0))


def _modulated_norm(xv, nw, sc, sh):
    r = lax.rsqrt(jnp.mean(xv * xv, axis=-1, keepdims=True) + EPS)
    xn = xv * r
    return xn * nw * (1.0 + sc) + sh, xn, r


IN_TN = 768


def _in_proj(x, norm_w, scale, shift, w_in_t, b_in_p):
    t, d = x.shape
    tm = 2048
    gcol = IN_MAIN // 128

    def body(x_ref, nw_ref, sc_ref, sh_ref, w_ref, b_ref, wg_ref, bg_ref, proj_ref, g_ref, h_scr):
        @pl.when(pl.program_id(1) == 0)
        def _():
            h, _, _ = _modulated_norm(x_ref[...], nw_ref[...], sc_ref[...], sh_ref[...])
            h_scr[...] = h.astype(BF16)
            g_ref[...] = _nt(h_scr[...], wg_ref[...]) + bg_ref[...]
        proj_ref[...] = (_nt(h_scr[...], w_ref[...]) + b_ref[...]).astype(BF16)

    row = lambda n: pl.BlockSpec((1, n), lambda i, j: (0, 0))
    return pl.pallas_call(
        body, name="in_proj", grid=(t // tm, IN_MAIN // IN_TN),
        in_specs=[pl.BlockSpec((tm, d), lambda i, j: (i, 0)), row(d), row(d), row(d),
                  pl.BlockSpec((IN_TN, d), lambda i, j: (j, 0)), pl.BlockSpec((1, IN_TN), lambda i, j: (0, j)),
                  pl.BlockSpec((128, d), lambda i, j: (gcol, 0)), pl.BlockSpec((1, 128), lambda i, j: (0, gcol))],
        out_specs=[pl.BlockSpec((tm, IN_TN), lambda i, j: (i, j)), pl.BlockSpec((tm, 128), lambda i, j: (i, 0))],
        out_shape=[jax.ShapeDtypeStruct((t, IN_MAIN), BF16), jax.ShapeDtypeStruct((t, 128), F32)],
        scratch_shapes=[pltpu.VMEM((tm, d), BF16)],
        compiler_params=_cparams(("parallel", "arbitrary")),
    )(x, norm_w, scale, shift, w_in_t, b_in_p, w_in_t, b_in_p)


def _ml_norm_parts(hs, o, z, nw):
    outs = []
    for hh in range(ML_HEADS):
        sl = slice(hh * ML_HEAD_DIM, (hh + 1) * ML_HEAD_DIM)
        so = jax.nn.sigmoid(o[:, sl])
        hm = hs[:, sl] * so
        mu = jnp.mean(hm, axis=-1, keepdims=True)
        cen = hm - mu
        var = jnp.mean(cen * cen, axis=-1, keepdims=True)
        rs = lax.rsqrt(var + EPS)
        outs.append((sl, cen * rs, rs, so))
    return outs


def _tail(o_na, proj, h_f, h_b, x, target, gate, ml_norm_w, fnw, w_out_b):
    t, d = x.shape
    tm = 256

    def body(ona_ref, naz_ref, hf_ref, hb_ref, o_ref, z_ref, x_ref, tg_ref, g_ref, nw_ref, fw_ref, w_ref,
             loss_ref, dres_ref, dona_ref, dnaz_ref, dhs_ref, do_ref, dz_ref, dgate_ref, gfw_ref, gnw_ref,
             gwo_ref, mix_scr):
        @pl.when(pl.program_id(0) == 0)
        def _():
            for r in (loss_ref, dgate_ref, gfw_ref, gnw_ref, gwo_ref):
                r[...] = jnp.zeros_like(r)
        naz = naz_ref[...].astype(F32)
        ona = ona_ref[...]
        sg_naz = jax.nn.sigmoid(naz)
        sna = naz * sg_naz
        mix_scr[:, 0:NA_W] = (ona * sna).astype(BF16)
        hs = hf_ref[...] + hb_ref[...]
        z = z_ref[...].astype(F32)
        ov = o_ref[...].astype(F32)
        parts = _ml_norm_parts(hs, ov, z, nw_ref[...])
        sgz = [jax.nn.sigmoid(z[:, sl]) for sl, _, _, _ in parts]
        for (sl, xn, _, _), sg in zip(parts, sgz):
            mix_scr[:, NA_W + sl.start:NA_W + sl.stop] = (xn * nw_ref[:, sl] * (z[:, sl] * sg)).astype(BF16)
        mixb = mix_scr[...]
        wv = w_ref[...]
        yv = _nn(mixb, wv)
        gate_v = g_ref[...]
        hres = x_ref[...] + gate_v * yv
        r = lax.rsqrt(jnp.mean(hres * hres, axis=-1, keepdims=True) + EPS)
        xnf = hres * r
        err = xnf * fw_ref[...] - tg_ref[...]
        loss_ref[...] += 0.5 * jnp.sum(jnp.sum(err * err, axis=-1, keepdims=True) * (1.0 / d), axis=0, keepdims=True)
        dout = err * (1.0 / d)
        gfw_ref[...] += jnp.sum(dout * xnf, axis=0, keepdims=True)
        dxn = dout * fw_ref[...]
        dres = r * (dxn - xnf * jnp.mean(dxn * xnf, axis=-1, keepdims=True))
        dres_ref[...] = dres
        dgate_ref[...] += jnp.sum(dres * yv, axis=0, keepdims=True)
        dyb = (dres * gate_v).astype(BF16)
        gwo_ref[...] += _tn(mixb, dyb)
        dmix = _nt(dyb, wv)
        dna = dmix[:, 0:NA_W]
        dona_ref[...] = dna * sna
        dnaz_ref[...] = (dna * ona * (sg_naz * (1.0 + naz * (1.0 - sg_naz)))).astype(BF16)
        for (sl, xn, rs, so), sg in zip(parts, sgz):
            dyv = dmix[:, NA_W + sl.start:NA_W + sl.stop]
            zz = z[:, sl]
            sz = zz * sg
            w = nw_ref[:, sl]
            dz_ref[:, sl] = (dyv * xn * w * (sg * (1.0 + zz * (1.0 - sg)))).astype(BF16)
            gnw_ref[:, sl] += jnp.sum(dyv * xn * sz, axis=0, keepdims=True)
            dxm = dyv * w * sz
            dhm = rs * (dxm - jnp.mean(dxm, axis=-1, keepdims=True)
                        - xn * jnp.mean(dxm * xn, axis=-1, keepdims=True))
            dhs_ref[:, sl] = dhm * so
            do_ref[:, sl] = (dhm * hs[:, sl] * so * (1.0 - so)).astype(BF16)

    blk = lambda c: pl.BlockSpec((tm, 512), lambda i, c=c: (i, c))
    full = pl.BlockSpec((tm, d), lambda i: (i, 0))
    o512 = jax.ShapeDtypeStruct((t, 512), F32)
    b512 = jax.ShapeDtypeStruct((t, 512), BF16)
    whole = pl.BlockSpec((d, d), lambda i: (0, 0))
    return pl.pallas_call(
        body, name="tail", grid=(t // tm,),
        in_specs=[blk(0), blk(3), blk(0), blk(0), blk(7), blk(8), full, full, _row(d), _row(ML_W), _row(d), whole],
        out_specs=[pl.BlockSpec((1, 128), lambda i: (0, 0)), full] + [blk(0)] * 5
        + [_row(d), _row(d), _row(ML_W), whole],
        out_shape=[jax.ShapeDtypeStruct((1, 128), F32), jax.ShapeDtypeStruct((t, d), F32),
                   o512, b512, o512, b512, b512]
        + [jax.ShapeDtypeStruct((1, d), F32), jax.ShapeDtypeStruct((1, d), F32),
           jax.ShapeDtypeStruct((1, ML_W), F32), jax.ShapeDtypeStruct((d, d), F32)],
        scratch_shapes=[pltpu.VMEM((tm, d), BF16)],
        compiler_params=_cparams(("arbitrary",)),
    )(o_na, proj, h_f, h_b, proj, proj, x, target, gate, ml_norm_w, fnw, w_out_b)


def _in_bwd(pieces, x, dres, w_in_t, norm_w, scale, shift):
    t, d = x.shape
    tm = 512
    nt = t // tm
    widths = [p.shape[1] for p in pieces]
    offs = [sum(widths[:k]) for k in range(len(widths))]
    assert sum(widths) == IN_PAD
    npc = len(pieces)

    def body(*refs):
        p_refs = refs[:npc]
        (x_ref, dres_ref, w_hbm, nw_ref, sc_ref, sh_ref,
         gx_ref, gw_hbm, gb_ref, dsc_ref, dsh_ref, gnw_ref, w_vmem, acc, stage, sem) = refs[npc:]
        i = pl.program_id(0)

        @pl.when(i == 0)
        def _():
            cp = pltpu.make_async_copy(w_hbm, w_vmem, sem.at[0])
            cp.start()
            acc[...] = jnp.zeros_like(acc)
            gb_ref[...] = jnp.zeros_like(gb_ref)
            dsc_ref[...] = jnp.zeros_like(dsc_ref)
            dsh_ref[...] = jnp.zeros_like(dsh_ref)
            gnw_ref[...] = jnp.zeros_like(gnw_ref)
            cp.wait()

        nw = nw_ref[...]
        s1 = 1.0 + sc_ref[...]
        h, xn, r = _modulated_norm(x_ref[...], nw, sc_ref[...], sh_ref[...])
        hb = h.astype(BF16)
        dhv = jnp.zeros((tm, d), F32)
        for p_ref, c0, w in zip(p_refs, offs, widths):
            pt = p_ref[...]
            pb = pt.astype(BF16)
            dhv = dhv + _nn(pb, w_vmem[c0:c0 + w, :])
            acc[:, c0:c0 + w] += _tn(hb, pb)
            gb_ref[:, c0:c0 + w] += jnp.sum(pt.astype(F32), axis=0, keepdims=True)
        dsh_ref[...] += jnp.sum(dhv, axis=0, keepdims=True)
        dsc_ref[...] += jnp.sum(dhv * xn * nw, axis=0, keepdims=True)
        gnw_ref[...] += jnp.sum(dhv * xn * s1, axis=0, keepdims=True)
        dxn = dhv * nw * s1
        gx_ref[...] = dres_ref[...] + r * (dxn - xn * jnp.mean(dxn * xn, axis=-1, keepdims=True))

        @pl.when(i == nt - 1)
        def _():
            copies = []
            for blk in range(IN_PAD // 128):
                slot = blk % 2
                if blk >= 2:
                    copies[blk - 2].wait()
                stage[slot] = acc[:, blk * 128:(blk + 1) * 128].T
                cp = pltpu.make_async_copy(stage.at[slot], gw_hbm.at[pl.ds(blk * 128, 128), :], sem.at[1 + slot])
                cp.start()
                copies.append(cp)
            copies[-2].wait()
            copies[-1].wait()

    full = pl.BlockSpec((tm, d), lambda i: (i, 0))
    return pl.pallas_call(
        body, name="in_bwd", grid=(nt,),
        in_specs=[pl.BlockSpec((tm, w), lambda i: (i, 0)) for w in widths]
        + [full, full, pl.BlockSpec(memory_space=pl.ANY), _row(d), _row(d), _row(d)],
        out_specs=[full, pl.BlockSpec(memory_space=pl.ANY), _row(IN_PAD), _row(d), _row(d), _row(d)],
        out_shape=[jax.ShapeDtypeStruct((t, d), F32), jax.ShapeDtypeStruct((IN_PAD, d), F32),
                   jax.ShapeDtypeStruct((1, IN_PAD), F32)] + [jax.ShapeDtypeStruct((1, d), F32)] * 3,
        scratch_shapes=[pltpu.VMEM((IN_PAD, d), BF16), pltpu.VMEM((d, IN_PAD), F32),
                        pltpu.VMEM((2, 128, d), F32), pltpu.SemaphoreType.DMA((3,))],
        compiler_params=_cparams(("arbitrary",), IN_BWD_VMEM_LIMIT),
    )(*pieces, x, dres, w_in_t, norm_w, scale, shift)


def _na_static(rows):
    cases = [(0, 0), (NA_QROWS, NA_QROWS - 4), (rows - NA_QROWS, rows - NA_KROWS)]
    dy = np.zeros((3, NA_QROWS, NA_KROWS), np.int32)
    rv = np.zeros((3, NA_QROWS, NA_KROWS), bool)
    for cs, (r0, kr0) in enumerate(cases):
        for i in range(NA_QROWS):
            for j in range(NA_KROWS):
                r, kr = r0 + i, kr0 + j
                rs = min(max(r - NA_KH // 2, 0), rows - NA_KH)
                rv[cs, i, j] = rs <= kr <= rs + NA_KH - 1
                dy[cs, i, j] = min(max(kr - r + NA_KH - 1, 0), 2 * NA_KH - 2)
    cq = np.arange(GRID_W)[:, None]
    ck = np.arange(GRID_W)[None, :]
    cs0 = np.clip(cq - NA_KW // 2, 0, GRID_W - NA_KW)
    cv = (ck >= cs0) & (ck < cs0 + NA_KW)
    dx = np.clip(ck - cq, -(NA_KW - 1), NA_KW - 1) + NA_KW - 1
    return dy, rv, dx.astype(np.int32), cv


def _na_bias_table(rpb, rows):
    _, _, dx, cv = _na_static(rows)
    ndy = 2 * NA_KH - 1
    onehot = (dx.reshape(1, -1) == np.arange(2 * NA_KW - 1)[:, None]).astype(np.float32)
    rpx = jnp.dot(rpb.reshape(NA_HEADS * ndy, 2 * NA_KW - 1), jnp.asarray(onehot), precision=HI)
    rpx = jnp.where(cv[None, None], rpx.reshape(NA_HEADS, ndy, GRID_W, GRID_W), NEG)
    neg = jnp.full((NA_HEADS, 1, GRID_W, GRID_W), NEG, F32)
    rpx = jnp.concatenate([rpx, neg], axis=1)
    nxt = jnp.concatenate([rpx[:, 1:], neg], axis=1)
    negs = jnp.broadcast_to(neg, rpx.shape)
    pairs = jnp.concatenate([jnp.concatenate([rpx, nxt], axis=3), jnp.concatenate([rpx, negs], axis=3),
                             jnp.concatenate([negs, rpx], axis=3)], axis=1)
    npair = pairs.shape[1]

    def body(m_ref, o_ref):
        cs = pl.program_id(1)
        r0 = jnp.where(cs == 0, 0, jnp.where(cs == 1, NA_QROWS, rows - NA_QROWS))
        kr0 = jnp.where(cs == 0, 0, jnp.where(cs == 1, NA_QROWS - NA_KH // 2, rows - NA_KROWS))
        for i in range(NA_QROWS):
            r = r0 + i
            rs = jnp.clip(r - NA_KH // 2, 0, rows - NA_KH)
            for jp in range(NA_KROWS // 2):
                kl = kr0 + 2 * jp
                vl = (kl >= rs) & (kl <= rs + NA_KH - 1)
                vr = (kl + 1 >= rs) & (kl + 1 <= rs + NA_KH - 1)
                dyl = jnp.clip(kl - r + NA_KH - 1, 0, ndy)
                dyr = jnp.clip(kl + 1 - r + NA_KH - 1, 0, ndy)
                idx = jnp.where(vl & vr, dyl, jnp.where(vl, 16 + dyl, jnp.where(vr, 32 + dyr, 16 + ndy)))
                o_ref[0, 0, i * GRID_W:(i + 1) * GRID_W, jp * 128:(jp + 1) * 128] = m_ref[0, idx]

    return pl.pallas_call(
        body, name="na_bias_table", grid=(NA_HEADS, 3),
        in_specs=[pl.BlockSpec((1, npair, GRID_W, 128), lambda h, cs: (h, 0, 0, 0))],
        out_specs=pl.BlockSpec((1, 1, NA_QT, NA_KT), lambda h, cs: (h, cs, 0, 0)),
        out_shape=jax.ShapeDtypeStruct((NA_HEADS, 3, NA_QT, NA_KT), F32),
        compiler_params=_cparams(("parallel", "parallel")),
    )(pairs)


def _na_specs(t):
    nb = t // NA_QT
    nkb = t // NA_KCH
    npieces = NA_KT // NA_KCH

    def kb0(b):
        return jnp.clip(b * (NA_QT // NA_KCH) - 1, 0, nkb - npieces)

    def case(b):
        return jnp.where(b == 0, 0, jnp.where(b == nb - 1, 2, 1))

    q_spec = pl.BlockSpec((NA_QT, 128), lambda p, b: (b, p))
    k_specs = [pl.BlockSpec((NA_KCH, 128), lambda p, b, i=i: (kb0(b) + i, 4 + p)) for i in range(npieces)]
    v_specs = [pl.BlockSpec((NA_KCH, 128), lambda p, b, i=i: (kb0(b) + i, 8 + p)) for i in range(npieces)]
    tbl_spec = pl.BlockSpec((2, 1, NA_QT, NA_KT), lambda p, b: (p, case(b), 0, 0))
    io_spec = pl.BlockSpec((NA_QT, 128), lambda p, b: (b, p))
    return nb, npieces, kb0, case, q_spec, k_specs, v_specs, tbl_spec, io_spec


NA_HALF = NA_QT // 2
NA_COMBOS_ALL = tuple((i, 0, NA_QT) for i in range(NA_KT // NA_KCH))
NA_COMBOS_INNER = ((0, 0, NA_HALF),) + tuple((i, 0, NA_QT) for i in range(1, NA_KT // NA_KCH - 1)) \
    + ((NA_KT // NA_KCH - 1, NA_HALF, NA_QT),)


def _na_place(val, r0, r1):
    if (r0, r1) == (0, NA_QT):
        return val
    z = jnp.zeros((NA_HALF, val.shape[1]), val.dtype)
    return jnp.concatenate([val, z] if r0 == 0 else [z, val], axis=0)


def _na_fwd(proj, tbl):
    t = proj.shape[0]
    nb, npieces, _, _, q_spec, k_specs, v_specs, tbl_spec, io_spec = _na_specs(t)
    lse_spec = pl.BlockSpec((1, NA_QT, 2), lambda p, b: (p, b, 0))

    def body(*refs):
        q_ref = refs[0]
        k_refs = refs[1:1 + npieces]
        v_refs = refs[1 + npieces:1 + 2 * npieces]
        tbl_ref, o_ref, lse_ref = refs[1 + 2 * npieces:]
        b = pl.program_id(1)

        def compute(combos):
            lane = lax.broadcasted_iota(jnp.int32, (1, 128), 1)
            qv = q_ref[...].astype(F32) * (NA_HEAD_DIM ** -0.5)
            ks = [r[...].astype(BF16) for r in k_refs]
            vs = [r[...].astype(BF16) for r in v_refs]
            hs = range(2)
            msk = [(lane // NA_HEAD_DIM) == hh for hh in hs]
            qh = [jnp.where(msk[hh], qv, 0.0).astype(BF16) for hh in hs]
            s = [[_nt(qh[hh][r0:r1], ks[i]) + tbl_ref[hh, 0, r0:r1, i * NA_KCH:(i + 1) * NA_KCH]
                  for i, r0, r1 in combos] for hh in hs]
            for h0 in (0, NA_HALF):
                rows = slice(h0, h0 + NA_HALF)
                cover = [(c, i, h0 - r0) for c, (i, r0, r1) in enumerate(combos) if r0 <= h0 < r1]
                part = [[s[hh][c][off:off + NA_HALF] for c, _, off in cover] for hh in hs]
                m = [functools.reduce(jnp.maximum, [jnp.max(v, axis=1, keepdims=True) for v in part[hh]]) for hh in hs]
                p = [[jnp.exp(v - m[hh]) for v in part[hh]] for hh in hs]
                l = [functools.reduce(jnp.add, [jnp.sum(v, axis=1, keepdims=True) for v in p[hh]]) for hh in hs]
                o = [functools.reduce(jnp.add, [_nn(p[hh][k].astype(BF16), vs[i]) for k, (_, i, _) in enumerate(cover)])
                     for hh in hs]
                for hh in hs:
                    lse_ref[0, rows, hh:hh + 1] = m[hh] + jnp.log(l[hh])
                o_ref[rows, :] = jnp.where(msk[0], o[0] / l[0], o[1] / l[1])

        inner = (b > 0) & (b < nb - 1)
        pl.when(inner)(lambda: compute(NA_COMBOS_INNER))
        pl.when(jnp.logical_not(inner))(lambda: compute(NA_COMBOS_ALL))

    return pl.pallas_call(
        body, name="na_fwd", grid=(4, nb),
        in_specs=[q_spec] + k_specs + v_specs + [tbl_spec],
        out_specs=[io_spec, lse_spec],
        out_shape=[jax.ShapeDtypeStruct((t, NA_W), F32), jax.ShapeDtypeStruct((4, t, 2), F32)],
        compiler_params=_cparams(("parallel", "arbitrary")),
    )(*([proj] * (1 + 2 * npieces)), tbl)


def _na_bwd(proj, tbl, d_o, o_na, lse):
    t = proj.shape[0]
    nb, npieces, kb0, case, q_spec, k_specs, v_specs, tbl_spec, io_spec = _na_specs(t)

    def body(*refs):
        q_ref = refs[0]
        k_refs = refs[1:1 + npieces]
        v_refs = refs[1 + npieces:1 + 2 * npieces]
        (tbl_ref, do_ref, o_ref, lse_ref, dq_ref, dk_hbm, dv_hbm, rpb_ref,
         dk_acc, dv_acc, dk_out, dv_out, s_scr, dp_scr, dsb_scr, pnb_scr, sem) = refs[1 + 2 * npieces:]
        p_id = pl.program_id(0)
        b = pl.program_id(1)

        @pl.when(b == 0)
        def _():
            dk_acc[...] = jnp.zeros_like(dk_acc)
            dv_acc[...] = jnp.zeros_like(dv_acc)

        @pl.when((b == 0) | (b == 1) | (b == nb - 1))
        def _():
            rpb_ref[...] = jnp.zeros_like(rpb_ref)

        def compute(combos):
            lane = lax.broadcasted_iota(jnp.int32, (1, 128), 1)
            scale = NA_HEAD_DIM ** -0.5
            qv = q_ref[...].astype(F32) * scale
            ks = [r[...].astype(BF16) for r in k_refs]
            vs = [r[...].astype(BF16) for r in v_refs]
            dov = do_ref[...]
            ov = o_ref[...]
            tok0 = kb0(b) * NA_KCH
            hs = range(2)
            msk = [(lane // NA_HEAD_DIM) == hh for hh in hs]
            qh = [jnp.where(msk[hh], qv, 0.0).astype(BF16) for hh in hs]
            doh = [jnp.where(msk[hh], dov, 0.0) for hh in hs]
            dohb = [doh[hh].astype(BF16) for hh in hs]
            dd = [jnp.sum(doh[hh] * ov, axis=1, keepdims=True) for hh in hs]
            for hh in hs:
                for c, (i, q0, q1) in enumerate(combos):
                    slot = (hh * len(combos) + c) % 2
                    cols = slice(i * NA_KCH, (i + 1) * NA_KCH)
                    s_scr[slot, 0:q1 - q0] = _nt(qh[hh][q0:q1], ks[i])
                    dp_scr[slot, 0:q1 - q0] = _nt(dohb[hh][q0:q1], vs[i])
                    for r0 in range(q0, q1, NA_RC):
                        rows = slice(r0, r0 + NA_RC)
                        loc = slice(r0 - q0, r0 - q0 + NA_RC)
                        p = jnp.exp(s_scr[slot, loc, :] + tbl_ref[hh, 0, rows, cols] - lse_ref[0, rows, hh:hh + 1])
                        d = p * (dp_scr[slot, loc, :] - dd[hh][rows])
                        pnb_scr[hh, rows, cols] = p.astype(BF16)
                        dsb_scr[hh, rows, cols] = d.astype(BF16)
                done = {(i, q0) for i, q0, _ in combos} | {(i, NA_HALF) for i, q0, q1 in combos if q1 - q0 == NA_QT}
                for i in range(npieces):
                    for q0 in (0, NA_HALF):
                        if (i, q0) not in done:
                            dsb_scr[hh, q0:q0 + NA_HALF, i * NA_KCH:(i + 1) * NA_KCH] = jnp.zeros(
                                (NA_HALF, NA_KCH), BF16)
            dqh = [functools.reduce(jnp.add, [_na_place(_nn(dsb_scr[hh, q0:q1, i * NA_KCH:(i + 1) * NA_KCH], ks[i]),
                                                        q0, q1) for i, q0, q1 in combos]) for hh in hs]
            dq_ref[...] = (jnp.where(msk[0], dqh[0], dqh[1]) * scale).astype(BF16)
            for i in range(npieces):
                rows = pl.ds(pl.multiple_of(tok0 + i * NA_KCH, NA_KCH), NA_KCH)
                cols = slice(i * NA_KCH, (i + 1) * NA_KCH)
                q0, q1 = [(a, e) for j, a, e in combos if j == i][0]
                dk_acc[rows, :] += (_tn(dsb_scr[0, q0:q1, cols], qh[0][q0:q1])
                                    + _tn(dsb_scr[1, q0:q1, cols], qh[1][q0:q1]))
                dv_acc[rows, :] += (_tn(pnb_scr[0, q0:q1, cols], dohb[0][q0:q1])
                                    + _tn(pnb_scr[1, q0:q1, cols], dohb[1][q0:q1]))
            for hh in hs:
                acc = dsb_scr[hh, 0:GRID_W, :].astype(F32)
                for i in range(1, NA_QROWS):
                    acc = acc + pltpu.roll(dsb_scr[hh, i * GRID_W:(i + 1) * GRID_W, :].astype(F32),
                                           NA_KT - i * GRID_W, 1)
                rpb_ref[0, 0, hh] += acc

        inner = (b > 0) & (b < nb - 1)
        pl.when(inner)(lambda: compute(NA_COMBOS_INNER))
        pl.when(jnp.logical_not(inner))(lambda: compute(NA_COMBOS_ALL))

        @pl.when(b == nb - 1)
        def _():
            cols = pl.ds(pl.multiple_of(p_id * 128, 128), 128)
            dk_out[...] = dk_acc[...].astype(BF16)
            dv_out[...] = dv_acc[...].astype(BF16)
            ck = pltpu.make_async_copy(dk_out, dk_hbm.at[:, cols], sem.at[0])
            cv = pltpu.make_async_copy(dv_out, dv_hbm.at[:, cols], sem.at[1])
            ck.start()
            cv.start()
            ck.wait()
            cv.wait()

    o512 = jax.ShapeDtypeStruct((t, NA_W), BF16)
    return pl.pallas_call(
        body, name="na_bwd", grid=(4, nb),
        in_specs=[q_spec] + k_specs + v_specs + [tbl_spec, io_spec, io_spec,
                                                 pl.BlockSpec((1, NA_QT, 2), lambda p, b: (p, b, 0))],
        out_specs=[io_spec, pl.BlockSpec(memory_space=pl.ANY), pl.BlockSpec(memory_space=pl.ANY),
                   pl.BlockSpec((1, 1, 2, GRID_W, NA_KT), lambda p, b: (p, case(b), 0, 0, 0))],
        out_shape=[o512, o512, o512, jax.ShapeDtypeStruct((4, 3, 2, GRID_W, NA_KT), F32)],
        scratch_shapes=[pltpu.VMEM((t, 128), F32), pltpu.VMEM((t, 128), F32),
                        pltpu.VMEM((t, 128), BF16), pltpu.VMEM((t, 128), BF16),
                        pltpu.VMEM((2, NA_QT, NA_KCH), F32), pltpu.VMEM((2, NA_QT, NA_KCH), F32),
                        pltpu.VMEM((2, NA_QT, NA_KT), BF16), pltpu.VMEM((2, NA_QT, NA_KT), BF16),
                        pltpu.SemaphoreType.DMA((2,))],
        compiler_params=_cparams(("arbitrary", "arbitrary")),
    )(*([proj] * (1 + 2 * npieces)), tbl, d_o, o_na, lse)


def _rpb_reduce(rpbacc, rows):
    nacc = 4 * 3 * 2

    def shift_body(a_ref, o_ref):
        acc = a_ref[0, 0:1, :]
        for cq in range(1, GRID_W):
            acc = acc + pltpu.roll(a_ref[0, cq:cq + 1, :], NA_KT - cq, 1)
        o_ref[0] = jnp.broadcast_to(acc, (8, NA_KT))

    vec = pl.pallas_call(
        shift_body, name="rpb_shift", grid=(nacc,),
        in_specs=[pl.BlockSpec((1, GRID_W, NA_KT), lambda a: (a, 0, 0))],
        out_specs=pl.BlockSpec((1, 8, NA_KT), lambda a: (a, 0, 0)),
        out_shape=jax.ShapeDtypeStruct((nacc, 8, NA_KT), F32),
        compiler_params=_cparams(("parallel",)),
    )(rpbacc.reshape(nacc, GRID_W, NA_KT))
    a = vec[:, 0].reshape(4, 3, 2, NA_KT).transpose(0, 2, 1, 3).reshape(NA_HEADS, 3, NA_KT)
    if rows // NA_QROWS < 3:
        a = a.at[:, 1].set(0.0)
    dd = np.arange(NA_KROWS)[:, None]
    dxo = np.arange(-(NA_KW - 1), NA_KW)[None, :]
    idx = ((dd * GRID_W + dxo) % NA_KT).reshape(-1)
    g = a[..., idx].reshape(NA_HEADS, 3 * NA_KROWS, 2 * NA_KW - 1)
    g = jnp.pad(g, ((0, 0), (0, 0), (0, 128 - (2 * NA_KW - 1))))
    nmat = np.zeros((16, 3 * NA_KROWS), np.float32)
    for cs, delta in enumerate((0, -(NA_KH // 2), -(NA_KROWS - NA_QROWS))):
        for d in range(NA_KROWS):
            jmi = d - NA_KROWS if (cs == 0 and d > NA_KH - 1) else d
            dy = jmi + delta + NA_KH - 1
            if 0 <= dy <= 2 * NA_KH - 2:
                nmat[dy, cs * NA_KROWS + d] = 1.0

    def body(n_ref, g_ref, o_ref):
        o_ref[0] = jnp.dot(n_ref[...], g_ref[0], precision=HI, preferred_element_type=F32)

    out = pl.pallas_call(
        body, name="rpb_reduce", grid=(NA_HEADS,),
        in_specs=[pl.BlockSpec((16, nmat.shape[1]), lambda h: (0, 0)),
                  pl.BlockSpec((1, nmat.shape[1], 128), lambda h: (h, 0, 0))],
        out_specs=pl.BlockSpec((1, 16, 128), lambda h: (h, 0, 0)),
        out_shape=jax.ShapeDtypeStruct((NA_HEADS, 16, 128), F32),
        compiler_params=_cparams(("parallel",)),
    )(jnp.asarray(nmat), g)
    return out[:, :2 * NA_KH - 1, :2 * NA_KW - 1]


def _halo_specs(tm, t, col, width=1024):
    nth = t // CONV_HALO
    per = tm // CONV_HALO
    return [pl.BlockSpec((tm, width), lambda i: (i, col)),
            pl.BlockSpec((CONV_HALO, width), lambda i: (jnp.maximum(i * per - 1, 0), col)),
            pl.BlockSpec((CONV_HALO, width), lambda i: (jnp.minimum((i + 1) * per, nth - 1), col))]


def _fill_ext(ext, cur_ref, prev_ref, next_ref, tm, nt):
    i = pl.program_id(0)
    hl = CONV_HALO
    ext[0:hl, :] = jnp.where(i == 0, 0.0, prev_ref[...].astype(F32))
    ext[hl:hl + tm, :] = cur_ref[...].astype(F32)
    ext[hl + tm:2 * hl + tm, :] = jnp.where(i == nt - 1, 0.0, next_ref[...].astype(F32))


CONV_HALO = 16
CONV_RC = 16
CONV_CB = 512


def _conv_chunks(tm):
    return [(slice(cb, cb + CONV_CB), slice(rb, rb + CONV_RC))
            for cb in range(0, 1024, CONV_CB) for rb in range(0, tm, CONV_RC)]


def _conv_fwd(proj, conv_w8, conv_b, tm):
    t = proj.shape[0]
    nt = t // tm

    def body(u_ref, up_ref, un_ref, w_ref, b_ref, pre_ref, act_ref, ext):
        _fill_ext(ext, u_ref, up_ref, un_ref, tm, nt)
        for cs, rs in _conv_chunks(tm):
            pre = b_ref[:, cs] + w_ref[0:1, cs] * ext[pl.ds(rs.start + CONV_HALO - 2, CONV_RC), cs]
            for j in range(1, CONV_W):
                pre = pre + w_ref[j:j + 1, cs] * ext[pl.ds(rs.start + CONV_HALO - 2 + j, CONV_RC), cs]
            pre_ref[rs, cs] = pre
            act_ref[rs, cs] = _silu(pre)

    full = pl.BlockSpec((tm, 1024), lambda i: (i, 0))
    o = jax.ShapeDtypeStruct((t, 1024), F32)
    return pl.pallas_call(
        body, name="conv_fwd", grid=(nt,),
        in_specs=_halo_specs(tm, t, 2) + [pl.BlockSpec((8, 1024), lambda i: (0, 0)), _row(1024)],
        out_specs=[full, full], out_shape=[o, o],
        scratch_shapes=[pltpu.VMEM((tm + 2 * CONV_HALO, 1024), F32)],
        compiler_params=_cparams(("parallel",)),
    )(proj, proj, proj, conv_w8, conv_b)


def _conv_bwd(dq, dk, pre, proj, conv_w8, tm):
    t = pre.shape[0]
    nt = t // tm

    def body(dq_ref, dqp_ref, dqn_ref, dk_ref, dkp_ref, dkn_ref, pre_ref, prep_ref, pren_ref,
             u_ref, up_ref, un_ref, w_ref, du_ref, gw_ref, gb_ref, extd, extu):
        i = pl.program_id(0)
        hl = CONV_HALO

        @pl.when(i == 0)
        def _():
            gw_ref[...] = jnp.zeros_like(gw_ref)
            gb_ref[...] = jnp.zeros_like(gb_ref)
        for rows, dqr, dkr, prr, edge in ((slice(0, hl), dqp_ref, dkp_ref, prep_ref, i == 0),
                                          (slice(hl, hl + tm), dq_ref, dk_ref, pre_ref, None),
                                          (slice(hl + tm, 2 * hl + tm), dqn_ref, dkn_ref, pren_ref, i == nt - 1)):
            ds = _dsilu(prr[...])
            dl = dqr[...] * ds[:, 0:ML_W]
            dr = dkr[...] * ds[:, ML_W:]
            if edge is not None:
                dl = jnp.where(edge, 0.0, dl)
                dr = jnp.where(edge, 0.0, dr)
            extd[rows, 0:ML_W] = dl
            extd[rows, ML_W:] = dr
        _fill_ext(extu, u_ref, up_ref, un_ref, tm, nt)
        gb_ref[...] += jnp.sum(extd[hl:hl + tm, :], axis=0, keepdims=True)
        gacc = None
        for cs, rs in _conv_chunks(tm):
            if rs.start == 0:
                gacc = [jnp.zeros((8, CONV_CB), F32) for _ in range(CONV_W)]
            du = w_ref[0:1, cs] * extd[pl.ds(rs.start + hl + 2, CONV_RC), cs]
            for j in range(1, CONV_W):
                du = du + w_ref[j:j + 1, cs] * extd[pl.ds(rs.start + hl + 2 - j, CONV_RC), cs]
            du_ref[rs, cs] = du.astype(BF16)
            dcur = extd[pl.ds(rs.start + hl, CONV_RC), cs]
            for j in range(CONV_W):
                prod = dcur * extu[pl.ds(rs.start + hl - 2 + j, CONV_RC), cs]
                gacc[j] = gacc[j] + functools.reduce(
                    jnp.add, [prod[k:k + 8] for k in range(0, CONV_RC, 8)])
            if rs.stop == tm:
                for j in range(CONV_W):
                    gw_ref[j:j + 1, cs] += jnp.sum(gacc[j], axis=0, keepdims=True)

    full = pl.BlockSpec((tm, 1024), lambda i: (i, 0))
    return pl.pallas_call(
        body, name="conv_bwd", grid=(nt,),
        in_specs=_halo_specs(tm, t, 0, ML_W) + _halo_specs(tm, t, 0, ML_W) + _halo_specs(tm, t, 0)
        + _halo_specs(tm, t, 2) + [pl.BlockSpec((8, 1024), lambda i: (0, 0))],
        out_specs=[full, pl.BlockSpec((8, 1024), lambda i: (0, 0)), _row(1024)],
        out_shape=[jax.ShapeDtypeStruct((t, 1024), BF16), jax.ShapeDtypeStruct((8, 1024), F32),
                   jax.ShapeDtypeStruct((1, 1024), F32)],
        scratch_shapes=[pltpu.VMEM((tm + 2 * CONV_HALO, 1024), F32), pltpu.VMEM((tm + 2 * CONV_HALO, 1024), F32)],
        compiler_params=_cparams(("arbitrary",)),
    )(dq, dq, dq, dk, dk, dk, pre, pre, pre, proj, proj, proj, conv_w8)


def _ml_consts(rev):
    iu = lax.broadcasted_iota(jnp.int32, (ML_CHUNK, ML_CHUNK), 0)
    js = lax.broadcasted_iota(jnp.int32, (ML_CHUNK, ML_CHUNK), 1)
    eye = iu == js
    le = iu <= js
    ge = iu >= js
    csum, csum_t, sees = (ge, le, ge) if rev else (le, ge, le)
    return eye, csum.astype(F32), csum_t.astype(F32), sees


def _col(row, eye):
    return jnp.sum(jnp.where(eye, row, 0.0), axis=1, keepdims=True)


def _rowof(col, eye):
    return jnp.sum(jnp.where(eye, col, 0.0), axis=0, keepdims=True)


def _row8(row):
    top = lax.broadcasted_iota(jnp.int32, (8, row.shape[1]), 0) == 0
    return jnp.where(top, row, jnp.zeros_like(row))


def _outer_rows(a_row, b_row_bf16):
    hi = a_row.astype(BF16)
    lo = (a_row - hi.astype(F32)).astype(BF16)
    r_a = lax.broadcasted_iota(jnp.int32, (8, a_row.shape[1]), 0)
    r_b = lax.broadcasted_iota(jnp.int32, (8, b_row_bf16.shape[1]), 0)
    lhs = jnp.where(r_a == 0, hi, jnp.where(r_a == 1, lo, jnp.zeros_like(hi)))
    rhs = jnp.where(r_b < 2, b_row_bf16, jnp.zeros_like(b_row_bf16))
    return _tn(lhs, rhs)


def _ml_gates(gi, gf, m0, csum, rev):
    lf = jax.nn.log_sigmoid(gf)
    b_rows = jnp.dot(lf, csum, precision=HI, preferred_element_type=F32)
    bl = jnp.sum(lf, axis=1, keepdims=True)
    a_rows = bl - b_rows + gi
    mloc = jnp.max(a_rows, axis=1, keepdims=True)
    order = list(range(ML_NB))[::-1] if rev else list(range(ML_NB))
    mp, mn, decay = {}, {}, {}
    m = m0
    for n in order:
        mp[n] = m
        m = jnp.maximum(bl[n:n + 1] + m, mloc[n:n + 1])
        mn[n] = m
    for n in order:
        decay[n] = jnp.exp(bl[n:n + 1] + mp[n] - mn[n])
    return b_rows, a_rows, gi - b_rows, mp, mn, decay, order


def _ml_load(q_ref, k_ref, v_ref, n):
    sl = slice(n * ML_CHUNK, (n + 1) * ML_CHUNK)
    qb = q_ref[sl, :].astype(BF16)
    kb = (k_ref[sl, :] * (ML_HEAD_DIM ** -0.5)).astype(BF16)
    vn = v_ref[sl, :].astype(F32)
    return sl, qb, kb, vn


def _ml_state_scan(q_ref, k_ref, v_ref, a_rows, mn, decay, order, c0, n0):
    ns = range(ML_NB)
    ld = [_ml_load(q_ref, k_ref, v_ref, n) for n in ns]
    vt = [ld[n][3].T for n in ns]
    w_row = [jnp.exp(a_rows[n:n + 1] - mn[n]) for n in ns]
    u = [_nn((vt[n] * w_row[n]).astype(BF16), ld[n][2]) for n in ns]
    nu = [_nn(_row8(w_row[n]).astype(BF16), ld[n][2])[0:1] for n in ns]
    cp, npv = {}, {}
    c, nv = c0, n0
    for n in order:
        cp[n], npv[n] = c, nv
        c = decay[n] * c + u[n]
        nv = decay[n] * nv + nu[n]
    return ld, vt, cp, npv, w_row, c, nv


def _ml_intra_all(ld, vt, b_rows, imb_rows, mp, cp, npv, sees, eye):
    ns = range(ML_NB)
    qk = [_nt(ld[n][2], ld[n][1]) for n in ns]
    cq = [_nt(cp[n].astype(BF16), ld[n][1]) for n in ns]
    qn = [_nt(_row8(npv[n]).astype(BF16), ld[n][1])[0:1] for n in ns]
    imb_col = [_col(imb_rows[n:n + 1], eye) for n in ns]
    dlog = [jnp.where(sees, b_rows[n:n + 1] + imb_col[n], NEG) for n in ns]
    m_inter = [b_rows[n:n + 1] + mp[n] for n in ns]
    m_t = [jnp.maximum(m_inter[n], jnp.max(dlog[n], axis=0, keepdims=True)) for n in ns]
    pm = [jnp.exp(dlog[n] - m_t[n]) for n in ns]
    inter = [jnp.exp(m_inter[n] - m_t[n]) for n in ns]
    floor = [jnp.exp(-m_t[n]) for n in ns]
    s = [qk[n] * pm[n] for n in ns]
    sv = [_nn(vt[n].astype(BF16), s[n].astype(BF16)) for n in ns]
    den = [jnp.sum(s[n], axis=0, keepdims=True) + inter[n] * qn[n] for n in ns]
    num = [sv[n] + inter[n] * cq[n] for n in ns]
    dn = [jnp.maximum(jnp.abs(den[n]), floor[n]) for n in ns]
    return [dict(pm=pm[n], s=s[n], inter=inter[n], cq=cq[n], qn=qn[n], num=num[n], den=den[n],
                 floor=floor[n], dn=dn[n]) for n in ns]


def _ml_specs(t, rev):
    nblk = t // ML_TB
    blk = (lambda g: nblk - 1 - g) if rev else (lambda g: g)
    hps = ML_HPS
    tile = lambda c0: pl.BlockSpec((ML_TB, 128 * hps), lambda hg, g, c0=c0: (blk(g), c0 // hps + hg))
    gate = pl.BlockSpec((hps, ML_NB, ML_CHUNK), lambda hg, g: (hg, blk(g), 0))
    cchk = pl.BlockSpec((hps, 1, 128, 128), lambda hg, g: (hg, blk(g), 0, 0))
    nmchk = pl.BlockSpec((hps, 1, 8, 128), lambda hg, g: (hg, blk(g), 0, 0))
    return nblk, blk, tile, gate, cchk, nmchk


def _ml_head_views(refs, hh):
    cols = slice(hh * ML_HEAD_DIM, (hh + 1) * ML_HEAD_DIM)
    return [r.at[:, cols] if len(r.shape) == 2 else r.at[hh] for r in refs]


def _ml_fwd(qk_act, proj, gi, gf, rev, name):
    t = qk_act.shape[0]
    nblk, _, tile, gate, cchk, nmchk = _ml_specs(t, rev)

    def body(*refs):
        for hh in range(ML_HPS):
            one_head(*_ml_head_views(refs, hh))

    def one_head(q_ref, k_ref, v_ref, gi_ref, gf_ref, h_ref, cchk_ref, nmchk_ref, c_ref, nm_ref):
        @pl.when(pl.program_id(1) == 0)
        def _():
            c_ref[...] = jnp.zeros_like(c_ref)
            nm_ref[...] = jnp.zeros_like(nm_ref)
        cchk_ref[0] = c_ref[...]
        nmchk_ref[0] = nm_ref[...]
        eye, csum, _, sees = _ml_consts(rev)
        b_rows, a_rows, imb_rows, mp, mn, decay, order = _ml_gates(
            gi_ref[...], gf_ref[...], nm_ref[1:2, 0:1], csum, rev)
        ld, vt, cp, npv, _, c, nv = _ml_state_scan(q_ref, k_ref, v_ref, a_rows, mn, decay, order,
                                                   c_ref[...], nm_ref[0:1, :])
        c_ref[...] = c
        nm_ref[0:1, :] = nv
        nm_ref[1:2, :] = jnp.broadcast_to(mn[order[-1]], (1, 128))
        rs = _ml_intra_all(ld, vt, b_rows, imb_rows, mp, cp, npv, sees, eye)
        ht = [rs[n]['num'] / rs[n]['dn'] for n in range(ML_NB)]
        for n in range(ML_NB):
            h_ref[n * ML_CHUNK:(n + 1) * ML_CHUNK, :] = ht[n].T

    return pl.pallas_call(
        body, name=name, grid=(ML_HEADS // ML_HPS, nblk),
        in_specs=[tile(0), tile(4), tile(24), gate, gate],
        out_specs=[tile(0), cchk, nmchk],
        out_shape=[jax.ShapeDtypeStruct((t, ML_W), F32),
                   jax.ShapeDtypeStruct((ML_HEADS, nblk, 128, 128), F32),
                   jax.ShapeDtypeStruct((ML_HEADS, nblk, 8, 128), F32)],
        scratch_shapes=[pltpu.VMEM((ML_HPS, 128, 128), F32), pltpu.VMEM((ML_HPS, 8, 128), F32)],
        compiler_params=_cparams(("parallel", "arbitrary")),
    )(qk_act, qk_act, proj, gi, gf)


def _ml_bwd(qk_act, proj, gi, gf, dh, cchk_a, nmchk_a, prev, rev, name):
    t = qk_act.shape[0]
    nblk, _, tile, gate, cchk, nmchk = _ml_specs(t, not rev)

    def body(*refs):
        for hh in range(ML_HPS):
            one_head(*_ml_head_views(refs, hh))

    def one_head(q_ref, k_ref, v_ref, gi_ref, gf_ref, dh_ref, cchk_ref, nmchk_ref, *rest):
        prev_refs = rest[:len(prev)]
        dq_ref, dk_ref, dv_ref, dgi_ref, dgf_ref, dc_ref, dn_ref, db_scr, dbl_scr, di_scr = rest[len(prev):]

        def plus_prev(val, which, rows):
            return val + prev_refs[which][rows, :] if prev else val

        @pl.when(pl.program_id(1) == 0)
        def _():
            dc_ref[...] = jnp.zeros_like(dc_ref)
            dn_ref[...] = jnp.zeros_like(dn_ref)
        eye, csum, csum_t, sees = _ml_consts(rev)
        gfv = gf_ref[...]
        b_rows, a_rows, imb_rows, mp, mn, decay, order = _ml_gates(
            gi_ref[...], gfv, nmchk_ref[0, 1:2, 0:1], csum, rev)
        ld, vt, cp, npv, w_row, _, _ = _ml_state_scan(q_ref, k_ref, v_ref, a_rows, mn, decay, order,
                                                      cchk_ref[0], nmchk_ref[0, 0:1, :])
        ns = range(ML_NB)
        rs = _ml_intra_all(ld, vt, b_rows, imb_rows, mp, cp, npv, sees, eye)
        sls = [ld[n][0] for n in ns]
        qbs = [ld[n][1] for n in ns]
        kbs = [ld[n][2] for n in ns]
        vbs = [ld[n][3].astype(BF16) for n in ns]
        rdn = [1.0 / rs[n]['dn'] for n in ns]
        dnum = [dh_ref[sls[n], :].T * rdn[n] for n in ns]
        hsum = [jnp.sum(dnum[n] * rs[n]['num'], axis=0, keepdims=True) for n in ns]
        dden = [jnp.where(jnp.abs(rs[n]['den']) > rs[n]['floor'],
                          -hsum[n] * rdn[n] * jnp.sign(rs[n]['den']), 0.0) for n in ns]
        dnb = [dnum[n].astype(BF16) for n in ns]
        dsf = [_nn(vbs[n], dnb[n]) + dden[n] for n in ns]
        dv0 = [_nt(rs[n]['s'].astype(BF16), dnb[n]) for n in ns]
        gb = [(dsf[n] * rs[n]['pm']).astype(BF16) for n in ns]
        cpb = [cp[n].astype(BF16) for n in ns]
        idd = [rs[n]['inter'] * dden[n] for n in ns]
        idn = [(rs[n]['inter'] * dnum[n]).astype(BF16) for n in ns]
        dqa = [_tn(gb[n], kbs[n]) for n in ns]
        dqc = [_tn(idn[n], cpb[n]) for n in ns]
        dqn = [_outer_rows(idd[n], npv[n].astype(BF16)) for n in ns]
        dk0 = [_nn(gb[n], qbs[n]) for n in ns]
        xs = [_nn(idn[n], qbs[n]) for n in ns]
        for n in ns:
            dq_ref[sls[n], :] = plus_prev(dqa[n] + dqc[n] + dqn[n], 0, sls[n])
        rr = [dsf[n] * rs[n]['s'] for n in ns]
        dinter = [jnp.sum(dnum[n] * rs[n]['cq'], axis=0, keepdims=True) + dden[n] * rs[n]['qn'] for n in ns]
        dbt = [jnp.sum(rr[n], axis=0, keepdims=True) + dinter[n] * rs[n]['inter'] for n in ns]
        dimb = [jnp.sum(rr[n], axis=1, keepdims=True) for n in ns]
        xns = [_nn(_row8(idd[n]).astype(BF16), qbs[n])[0:1] for n in ns]
        dcn, dnn = {}, {}
        dc, dn = dc_ref[...], dn_ref[0:1, :]
        for n in order[::-1]:
            dcn[n], dnn[n] = dc, dn
            dc = decay[n] * dc + xs[n]
            dn = decay[n] * dn + xns[n]
        dc_ref[...] = dc
        dn_ref[0:1, :] = dn
        kscale = ML_HEAD_DIM ** -0.5
        dcb = [dcn[n].astype(BF16) for n in ns]
        z = [_nn(vbs[n], dcb[n]) for n in ns]
        kd = [_nt(kbs[n], dcb[n]) for n in ns]
        ddecay = [jnp.sum(jnp.sum(dcn[n] * cp[n], axis=1, keepdims=True), axis=0, keepdims=True)
                  + jnp.sum(dnn[n] * npv[n], axis=1, keepdims=True) for n in ns]
        zd = [z[n] + dnn[n] for n in ns]
        dw = [jnp.sum(zd[n] * kbs[n].astype(F32), axis=1, keepdims=True) for n in ns]
        wcol = [_col(w_row[n], eye) for n in ns]
        for n in ns:
            dv_ref[sls[n], :] = plus_prev(dv0[n] + wcol[n] * kd[n], 2, sls[n]).astype(dv_ref.dtype)
            dk_ref[sls[n], :] = plus_prev((dk0[n] + wcol[n] * zd[n]) * kscale, 1, sls[n])
        da = [dw[n] * wcol[n] for n in ns]
        dbl = [jnp.sum(da[n], axis=0, keepdims=True) + ddecay[n] * decay[n] for n in ns]
        key_row = [_rowof(dimb[n] + da[n], eye) for n in ns]
        for n in ns:
            db_scr[n:n + 1, :] = dbt[n] - key_row[n]
            di_scr[n:n + 1, :] = key_row[n]
            dbl_scr[n:n + 1, :] = jnp.broadcast_to(dbl[n], (1, ML_CHUNK))
        dlf = jnp.dot(db_scr[...], csum_t, precision=HI, preferred_element_type=F32) + dbl_scr[...]
        dgf_ref[...] = dlf * jax.nn.sigmoid(-gfv)
        dgi_ref[...] = di_scr[...]

    nc = t // ML_CHUNK
    o512 = jax.ShapeDtypeStruct((t, ML_W), F32)
    og = jax.ShapeDtypeStruct((ML_HEADS, nc, ML_CHUNK), F32)
    return pl.pallas_call(
        body, name=name, grid=(ML_HEADS // ML_HPS, nblk),
        in_specs=[tile(0), tile(4), tile(24), gate, gate, tile(0), cchk, nmchk] + [tile(0)] * len(prev),
        out_specs=[tile(0), tile(0), tile(0), gate, gate],
        out_shape=[o512, o512, jax.ShapeDtypeStruct((t, ML_W), BF16 if prev else F32), og, og],
        scratch_shapes=[pltpu.VMEM((ML_HPS, 128, 128), F32), pltpu.VMEM((ML_HPS, 8, 128), F32)]
        + [pltpu.VMEM((ML_HPS, ML_NB, ML_CHUNK), F32)] * 3,
        compiler_params=_cparams(("parallel", "arbitrary")),
    )(qk_act, qk_act, proj, gi, gf, dh, cchk_a, nmchk_a, *prev)


def _gate_rows(gates16, t):
    g = gates16.reshape(t // ML_CHUNK, ML_CHUNK, 4, ML_HEADS).transpose(2, 3, 0, 1)
    return g[0], g[1], g[2], g[3]


def _gate_cols(dgi_f, dgf_f, dgi_b, dgf_b, t):
    g = jnp.stack([dgi_f, dgf_f, dgi_b, dgf_b]).transpose(2, 3, 0, 1).reshape(t, 4 * ML_HEADS)
    return jnp.pad(g, ((0, 0), (0, 128 - 4 * ML_HEADS)))


def _local_step(x, target, shift, scale, gate, norm_w, w_in_t, b_in_p, conv_w8, conv_b, rpb,
                ml_norm_w, w_out_b, final_norm_w):
    t = x.shape[0]
    rows = t // GRID_W
    tm = 512
    proj, gates = _in_proj(x, norm_w, scale, shift, w_in_t, b_in_p)
    tbl = _na_bias_table(rpb, rows)
    o_na, lse_na = _na_fwd(proj, tbl)
    pre, qk_act = _conv_fwd(proj, conv_w8, conv_b, tm)
    gi_f, gf_f, gi_b, gf_b = _gate_rows(gates[:, :4 * ML_HEADS], t)
    h_f, cchk_f, nmchk_f = _ml_fwd(qk_act, proj, gi_f, gf_f, False, "ml_fwd_f")
    h_b, cchk_b, nmchk_b = _ml_fwd(qk_act, proj, gi_b, gf_b, True, "ml_fwd_b")
    (loss, dres, d_ona, d_naz, dhs, d_o, d_z, dgate, g_fnw, g_mlnw, g_w_out) = _tail(
        o_na, proj, h_f, h_b, x, target, gate, ml_norm_w, final_norm_w, w_out_b)
    dq_na, dk_na, dv_na, rpbacc = _na_bwd(proj, tbl, d_ona, o_na, lse_na)
    g_rpb = _rpb_reduce(rpbacc, rows)
    dq_f, dk_f, dv_f, dgi_f, dgf_f = _ml_bwd(qk_act, proj, gi_f, gf_f, dhs, cchk_f, nmchk_f, (),
                                             False, "ml_bwd_f")
    dq_ml, dk_ml, dv_ml, dgi_b, dgf_b = _ml_bwd(qk_act, proj, gi_b, gf_b, dhs, cchk_b, nmchk_b, (dq_f, dk_f, dv_f),
                                                True, "ml_bwd_b")
    du, g_conv_w, g_conv_b = _conv_bwd(dq_ml, dk_ml, pre, proj, conv_w8, tm)
    dgates = _gate_cols(dgi_f, dgf_f, dgi_b, dgf_b, t)
    grad_x, g_w_in, g_b_in, dscale, dshift, g_nw = _in_bwd(
        [dq_na, dk_na, dv_na, d_naz, du, dv_ml, d_o, d_z, dgates], x, dres, w_in_t, norm_w, scale, shift)
    dmod = jnp.concatenate([dshift, dscale, dgate], axis=1)
    return (loss, grad_x, dmod, g_nw, g_w_in, g_b_in, g_conv_w, g_conv_b, g_rpb, g_mlnw, g_w_out, g_fnw)


MESH = pl.DeviceIdType.MESH
N_DEV = 8
ANY = pl.BlockSpec(memory_space=pl.ANY)
WHOLE_VMEM = pl.BlockSpec(memory_space=pltpu.VMEM)


def _allgather8(blocks, name):
    na = len(blocks)

    def body(*refs):
        x_refs = refs[:na]
        out_refs = refs[na:2 * na]
        send_sems, recv_sems, local_sems = refs[2 * na:]
        x, y, c = lax.axis_index("x"), lax.axis_index("y"), lax.axis_index("c")
        me, sibling = (x, y, c), (x, y, 1 - c)
        chips = [(1 - x, y), (x, 1 - y), (1 - x, 1 - y)]

        def rows(a, px, py, pc):
            return out_refs[a].at[4 * px + 2 * py + pc]

        def copy(a, k, block, to, src=None):
            return pltpu.make_async_remote_copy(
                src_ref=rows(a, *block) if src is None else src, dst_ref=rows(a, *block),
                send_sem=send_sems.at[a, k], recv_sem=recv_sems.at[a, k],
                device_id=to, device_id_type=MESH)

        mine, first, passed = [], [], []
        for a in range(na):
            cp = pltpu.make_async_copy(x_refs[a], rows(a, *me), local_sems.at[a])
            cp.start()
            mine.append(cp)
            first.append(copy(a, 0, me, sibling, src=x_refs[a]))
            first += [copy(a, 1 + j, me, (*chip, c), src=x_refs[a]) for j, chip in enumerate(chips)]
        for cp in first:
            cp.start()
        for a in range(na):
            for j, chip in enumerate(chips):
                copy(a, 1 + j, (*chip, c), me).wait_recv()
                fwd = copy(a, 4 + j, (*chip, c), sibling)
                fwd.start()
                passed.append(fwd)
        for a in range(na):
            copy(a, 0, sibling, me).wait_recv()
            for j, chip in enumerate(chips):
                copy(a, 4 + j, (*chip, 1 - c), me).wait_recv()
        for cp in first + passed:
            cp.wait_send()
        for cp in mine:
            cp.wait()

    return pl.pallas_call(
        body, name=name,
        out_shape=[jax.ShapeDtypeStruct((N_DEV,) + b.shape, b.dtype) for b in blocks],
        in_specs=[WHOLE_VMEM] * na, out_specs=[WHOLE_VMEM] * na,
        scratch_shapes=[pltpu.SemaphoreType.DMA((na, 7)), pltpu.SemaphoreType.DMA((na, 7)),
                        pltpu.SemaphoreType.DMA((na,))],
        compiler_params=pltpu.CompilerParams(vmem_limit_bytes=VMEM_LIMIT),
    )(*blocks)


def _pair_exchange(arrs, name):
    na = len(arrs)

    def body(*refs):
        in_refs = refs[:na]
        out_refs = refs[na:2 * na]
        send_sems, recv_sems = refs[2 * na:]
        sibling = (lax.axis_index("x"), lax.axis_index("y"), 1 - lax.axis_index("c"))
        copies = [pltpu.make_async_remote_copy(
            src_ref=in_refs[a], dst_ref=out_refs[a], send_sem=send_sems.at[a], recv_sem=recv_sems.at[a],
            device_id=sibling, device_id_type=MESH) for a in range(na)]
        for cp in copies:
            cp.start()
        for cp in copies:
            cp.wait()

    return pl.pallas_call(
        body, name=name,
        out_shape=[jax.ShapeDtypeStruct(a.shape, a.dtype) for a in arrs],
        in_specs=[ANY] * na, out_specs=[ANY] * na,
        scratch_shapes=[pltpu.SemaphoreType.DMA((na,)), pltpu.SemaphoreType.DMA((na,))],
    )(*arrs)


def _chip_exchange(arrs, name):
    na = len(arrs)

    def body(*refs):
        in_refs = refs[:na]
        out_refs = refs[na:2 * na]
        send_sems, recv_sems, local_sems = refs[2 * na:]
        x, y, c = lax.axis_index("x"), lax.axis_index("y"), lax.axis_index("c")
        my_chip = 2 * x + y
        chips = [(1 - x, y), (x, 1 - y), (1 - x, 1 - y)]
        local, remote = [], []
        for a in range(na):
            cp = pltpu.make_async_copy(in_refs[a].at[my_chip], out_refs[a].at[my_chip], local_sems.at[a])
            cp.start()
            local.append(cp)
            for j, (px, py) in enumerate(chips):
                cp = pltpu.make_async_remote_copy(
                    src_ref=in_refs[a].at[2 * px + py], dst_ref=out_refs[a].at[my_chip],
                    send_sem=send_sems.at[a, j], recv_sem=recv_sems.at[a, j],
                    device_id=(px, py, c), device_id_type=MESH)
                cp.start()
                remote.append(cp)
        for cp in remote:
            cp.wait()
        for cp in local:
            cp.wait()

    return pl.pallas_call(
        body, name=name,
        out_shape=[jax.ShapeDtypeStruct(a.shape, a.dtype) for a in arrs],
        in_specs=[ANY] * na, out_specs=[ANY] * na,
        scratch_shapes=[pltpu.SemaphoreType.DMA((na, 3)), pltpu.SemaphoreType.DMA((na, 3)),
                        pltpu.SemaphoreType.DMA((na,))],
    )(*arrs)


def _rows_tile(r):
    for cand in (512, 256, 128, 64, 32, 16, 8):
        if r % cand == 0:
            return cand
    return r


def _add2(a, b, name, out_dtype):
    s, r, n = a.shape
    tr = _rows_tile(r)

    def body(a_ref, b_ref, o_ref):
        o_ref[...] = (a_ref[...] + b_ref[...]).astype(out_dtype)

    spec = pl.BlockSpec((1, tr, n), lambda i, j: (i, j, 0))
    return pl.pallas_call(
        body, name=name, grid=(s, r // tr), in_specs=[spec, spec], out_specs=spec,
        out_shape=jax.ShapeDtypeStruct(a.shape, out_dtype),
        compiler_params=_cparams(("parallel", "parallel")),
    )(a, b)


def _sum_slabs(a, name):
    s, r, n = a.shape
    tr = _rows_tile(r)

    def body(a_ref, o_ref):
        acc = a_ref[0].astype(F32)
        for k in range(1, s):
            acc = acc + a_ref[k].astype(F32)
        o_ref[...] = acc

    return pl.pallas_call(
        body, name=name, grid=(r // tr,),
        in_specs=[pl.BlockSpec((s, tr, n), lambda i: (0, i, 0))],
        out_specs=pl.BlockSpec((tr, n), lambda i: (i, 0)),
        out_shape=jax.ShapeDtypeStruct((r, n), F32),
        compiler_params=_cparams(("parallel",)),
    )(a)


ADAMW_WHOLE = 64 * 1024


def _adamw(w, g, m, v, name):
    r, n = w.shape
    if r * n <= ADAMW_WHOLE:
        blk, grid, imap = (r, n), (1,), (lambda i: (0, 0))
    elif r % 8 == 0:
        blk, grid, imap = (_rows_tile(r), n), (r // _rows_tile(r),), (lambda i: (i, 0))
    else:
        blk, grid, imap = (r, 128), (n // 128,), (lambda i: (0, i))
    c1 = 1.0 / (1.0 - ADAM_B1 ** ADAM_STEP)
    c2 = 1.0 / (1.0 - ADAM_B2 ** ADAM_STEP)

    def body(w_ref, g_ref, m_ref, v_ref, d_ref, nm_ref, nv_ref):
        gv = g_ref[...]
        nm = ADAM_B1 * m_ref[...] + (1.0 - ADAM_B1) * gv
        nv = ADAM_B2 * v_ref[...] + (1.0 - ADAM_B2) * (gv * gv)
        nm_ref[...] = nm
        nv_ref[...] = nv
        d_ref[...] = -ADAM_LR * ((nm * c1) / (jnp.sqrt(nv * c2) + ADAM_EPS) + ADAM_WD * w_ref[...])

    spec = pl.BlockSpec(blk, imap)
    o = jax.ShapeDtypeStruct((r, n), F32)
    return pl.pallas_call(
        body, name=name, grid=grid, in_specs=[spec] * 4, out_specs=[spec] * 3, out_shape=[o, o, o],
        compiler_params=_cparams(("parallel",)),
    )(w, g, m, v)


def _mod_fwd(c_all, w_ada_s, b_ada_s):
    def body(c_ref, w_ref, b_ref, o_ref):
        o_ref[...] = jnp.dot(_silu(c_ref[...]), w_ref[...], precision=HI, preferred_element_type=F32) + b_ref[...]

    return pl.pallas_call(
        body, name="mod_fwd", out_shape=jax.ShapeDtypeStruct((c_all.shape[0], w_ada_s.shape[1]), F32),
        in_specs=[WHOLE_VMEM] * 3, out_specs=WHOLE_VMEM,
        compiler_params=pltpu.CompilerParams(vmem_limit_bytes=VMEM_LIMIT),
    )(c_all, w_ada_s, b_ada_s)


def _wada_grad(c_all, dmod_s):
    def body(c_ref, d_ref, o_ref):
        o_ref[...] = lax.dot_general(_silu(c_ref[...]), d_ref[...], (((0,), (0,)), ((), ())),
                                     precision=HI, preferred_element_type=F32)

    return pl.pallas_call(
        body, name="w_ada_grad", out_shape=jax.ShapeDtypeStruct((c_all.shape[1], dmod_s.shape[1]), F32),
        in_specs=[WHOLE_VMEM] * 2, out_specs=WHOLE_VMEM,
        compiler_params=pltpu.CompilerParams(vmem_limit_bytes=VMEM_LIMIT),
    )(c_all, dmod_s)


SMALL_ROWS = 24


def _pad_rows(v, nrows):
    v = v.reshape(-1)
    return jnp.pad(v, (0, nrows * 1024 - v.shape[0])).reshape(nrows, 1024)


def _pack_small(b_ada, norm_w, b_in, conv_w_full, conv_b, rpb, ml_norm_w, final_norm_w, last):
    parts = [_pad_rows(b_ada, 3), _pad_rows(norm_w, 1), _pad_rows(b_in, 5), _pad_rows(conv_w_full, 5),
             _pad_rows(conv_b, 1), _pad_rows(rpb, 4), _pad_rows(ml_norm_w, 1), _pad_rows(final_norm_w, 1),
             _pad_rows(last, 3)]
    return jnp.concatenate(parts, axis=0)


def _unpack_small(p):
    return dict(b_ada=p[0:3].reshape(1, 3072), norm_w=p[3:4], b_in=p[4:9].reshape(-1)[:IN_W].reshape(1, IN_W),
                conv_w=p[9:14], conv_b=p[14:15],
                rpb=p[15:19].reshape(-1)[:NA_HEADS * 15 * 31].reshape(1, NA_HEADS, 15, 31),
                ml_norm_w=p[19:20, :ML_W], final_norm_w=p[20], last=p[21])


def kernel(x, c, w_ada, b_ada, norm_w, w_in, b_in, conv_w, conv_b, rpb, ml_norm_w, w_out, final_norm_w, loss_target, m_w_ada, m_b_ada, m_norm_w, m_w_in, m_b_in, m_conv_w, m_conv_b, m_rpb, m_ml_norm_w, m_w_out, m_final_norm_w, v_w_ada, v_b_ada, v_norm_w, v_w_in, v_b_in, v_conv_w, v_conv_b, v_rpb, v_ml_norm_w, v_w_out, v_final_norm_w):
    xi, yi, ci = lax.axis_index("x"), lax.axis_index("y"), lax.axis_index("c")
    chip = 2 * xi + yi
    dev = 2 * chip + ci
    t = x.shape[1]
    ada_n = w_ada.shape[2]
    in_n = w_in.shape[2]
    out_r = w_out.shape[1]

    c_blk = jnp.pad(c, ((0, 7), (0, 0)))
    w_in_t, m_w_in_t, v_w_in_t = w_in[0].T, m_w_in[0].T, v_w_in[0].T
    in_h = in_n // 2
    w_in_half = lax.dynamic_slice_in_dim(w_in_t, ci * in_h, in_h, axis=0).astype(BF16)
    w_out_half = lax.dynamic_slice_in_dim(w_out[0], ci * (out_r // 2), out_r // 2, axis=0).astype(BF16)
    conv_blk = jnp.pad(conv_w[0], ((0, 3), (0, 0)))
    c_g, conv_g, w_in_g, w_out_g = _allgather8([c_blk, conv_blk, w_in_half, w_out_half], "gather_c_weights")
    c_all = c_g[:, 0]
    w_out_g = w_out_g.reshape(D_MODEL, D_MODEL)
    b_ada_s = lax.dynamic_slice_in_dim(b_ada, chip * ada_n, ada_n, axis=1)
    mod_s = _mod_fwd(c_all, w_ada[0], b_ada_s)
    (mod_g,) = _allgather8([mod_s], "gather_mod")
    mod_mine = lax.dynamic_index_in_dim(mod_g, dev, axis=1, keepdims=False)
    mod = mod_mine[0::2].reshape(1, 3 * D_MODEL)
    shift, scale, gate = mod[:, :D_MODEL], mod[:, D_MODEL:2 * D_MODEL], mod[:, 2 * D_MODEL:]

    w_in_tp = jnp.pad(w_in_g.reshape(IN_W, D_MODEL), ((0, IN_PAD - IN_W), (0, 0)))
    b_in_p = jnp.pad(b_in, ((0, 0), (0, IN_PAD - IN_W)))
    conv_w8 = conv_g.reshape(4, 2, 8, conv_w.shape[2])[:, 0].transpose(1, 0, 2).reshape(8, D_MODEL)

    (loss, grad_x, dmod, g_nw, g_w_in, g_b_in, g_conv_w, g_conv_b, g_rpb, g_mlnw, g_w_out, g_fnw) = _local_step(
        x[0], loss_target[0], shift, scale, gate, norm_w, w_in_tp, b_in_p, conv_w8, conv_b, rpb[0],
        ml_norm_w, w_out_g, final_norm_w.reshape(1, D_MODEL))

    g_in_t = g_w_in

    def halves(a, per_chip, h):
        return jnp.stack([lax.dynamic_slice_in_dim(a, k * per_chip + h * (per_chip // 2), per_chip // 2, axis=0)
                          for k in range(4)])

    ri, ro = _pair_exchange([halves(g_in_t, in_n, 1 - ci), halves(g_w_out, out_r, 1 - ci)], "rs_pair")
    pi = _add2(halves(g_in_t, in_n, ci), ri, "rs_pair_add_in", BF16)
    po = _add2(halves(g_w_out, out_r, ci), ro, "rs_pair_add_out", BF16)
    qi, qo = _chip_exchange([pi, po], "rs_chips")
    si = _sum_slabs(qi, "rs_sum_in")
    so = _sum_slabs(qo, "rs_sum_out")
    ti, to = _pair_exchange([si, so], "rs_share")
    g_w_in_s = jnp.where(ci == 0, jnp.concatenate([si, ti], axis=0), jnp.concatenate([ti, si], axis=0))
    g_w_out_s = jnp.where(ci == 0, jnp.concatenate([so, to], axis=0), jnp.concatenate([to, so], axis=0))

    small = _pack_small(dmod, g_nw, g_b_in[:, :IN_W], g_conv_w[:CONV_W], g_conv_b, g_rpb, g_mlnw, g_fnw,
                        jnp.pad(loss, ((0, 0), (0, 1024 - 128))))
    (small_g,) = _allgather8([small], "gather_small")
    small_sum = _sum_slabs(small_g, "small_sum")
    gs = _unpack_small(small_sum)
    dmod_all = small_g[:, 0:3].reshape(N_DEV, 3 * D_MODEL)
    g_w_ada_s = _wada_grad(c_all, lax.dynamic_slice_in_dim(dmod_all, chip * ada_n, ada_n, axis=1))
    g_conv_w_s = lax.dynamic_slice_in_dim(gs['conv_w'], chip * conv_w.shape[2], conv_w.shape[2], axis=1)
    loss_total = gs['last'][0]

    small_names = ('b_ada', 'norm_w', 'b_in', 'conv_b', 'rpb', 'ml_norm_w', 'final_norm_w')
    small_w = (b_ada, norm_w, b_in, conv_b, rpb, ml_norm_w, final_norm_w)
    small_m = (m_b_ada, m_norm_w, m_b_in, m_conv_b, m_rpb, m_ml_norm_w, m_final_norm_w)
    small_v = (v_b_ada, v_norm_w, v_b_in, v_conv_b, v_rpb, v_ml_norm_w, v_final_norm_w)
    ds_, nms, nvs = {}, {}, {}
    for nm_, w_, m_, v_ in zip(small_names, small_w, small_m, small_v):
        two_d = (NA_HEADS, w_.size // NA_HEADS) if nm_ == 'rpb' else (1, w_.size)
        outs = _adamw(w_.reshape(two_d), gs[nm_].reshape(two_d), m_.reshape(two_d), v_.reshape(two_d),
                      "adamw_" + nm_)
        ds_[nm_], nms[nm_], nvs[nm_] = [o.reshape(w_.shape) for o in outs]
    d_ada, nm_ada, nv_ada = _adamw(w_ada[0], g_w_ada_s, m_w_ada[0], v_w_ada[0], "adamw_w_ada")
    d_in, nm_in, nv_in = _adamw(w_in_t, g_w_in_s, m_w_in_t, v_w_in_t, "adamw_w_in")
    d_out, nm_out, nv_out = _adamw(w_out[0], g_w_out_s, m_w_out[0], v_w_out[0], "adamw_w_out")
    d_cw, nm_cw, nv_cw = _adamw(conv_w[0], g_conv_w_s, m_conv_w[0], v_conv_w[0], "adamw_conv_w")

    def group(big_ada, big_in, big_out, cw, sm):
        return (big_ada[None], sm['b_ada'], sm['norm_w'], big_in.T[None], sm['b_in'], cw[None], sm['conv_b'],
                sm['rpb'], sm['ml_norm_w'], big_out[None], sm['final_norm_w'])

    return ((loss_total, grad_x[None])
            + group(g_w_ada_s, g_w_in_s, g_w_out_s, g_conv_w_s, gs)
            + group(d_ada, d_in, d_out, d_cw, ds_)
            + group(nm_ada, nm_in, nm_out, nm_cw, nms)
            + group(nv_ada, nv_in, nv_out, nv_cw, nvs))
```

```python
import functools

import numpy as np
import jax
import jax.numpy as jnp
from jax import lax
from jax.experimental import pallas as pl
from jax.experimental.pallas import tpu as pltpu

F32 = jnp.float32
BF16 = jnp.bfloat16
HI = lax.Precision.HIGHEST

D_MODEL = 1024
GRID_W = 64
NA_W = 512
NA_HEAD_DIM = 64
NA_HEADS = 8
NA_KH = 8
NA_KW = 16
ML_W = 512
ML_HEADS = 4
ML_HEAD_DIM = 128
ML_CHUNK = 128
CONV_W = 5
EPS = 1e-6
IN_W = 4 * NA_W + 5 * ML_W + 4 * ML_HEADS
IN_MAIN = 4 * NA_W + 5 * ML_W
IN_PAD = IN_MAIN + 128
NEG = -1e30

ADAM_LR = 0.001
ADAM_B1 = 0.9
ADAM_B2 = 0.999
ADAM_EPS = 1e-08
ADAM_WD = 0.01
ADAM_STEP = 10

NA_QROWS = 8
NA_KROWS = 16
NA_QT = NA_QROWS * GRID_W
NA_KT = NA_KROWS * GRID_W
NA_KCH = 256
NA_RC = 32
ML_NB = 32
ML_TB = ML_NB * ML_CHUNK
ML_HPS = 1

VMEM_LIMIT = 56 * 1024 * 1024
IN_BWD_VMEM_LIMIT = 60 * 1024 * 1024


def _cparams(sem, vmem=VMEM_LIMIT):
    return pltpu.CompilerParams(dimension_semantics=sem, vmem_limit_bytes=vmem)


def _silu(x):
    return x * jax.nn.sigmoid(x)


def _dsilu(x):
    s = jax.nn.sigmoid(x)
    return s * (1.0 + x * (1.0 - s))


def _dot(a, b, dims):
    return lax.dot_general(a, b, (dims, ((), ())), preferred_element_type=F32)


def _nn(a, b):
    return _dot(a, b, ((1,), (0,)))


def _nt(a, b):
    return _dot(a, b, ((1,), (1,)))


def _tn(a, b):
    return _dot(a, b, ((0,), (0,)))


def _row(n):
    return pl.BlockSpec((1, n), lambda i: (0, 0))


def _modulated_norm(xv, nw, sc, sh):
    r = lax.rsqrt(jnp.mean(xv * xv, axis=-1, keepdims=True) + EPS)
    xn = xv * r
    return xn * nw * (1.0 + sc) + sh, xn, r


IN_TN = 768


def _in_proj(x, norm_w, scale, shift, w_in_t, b_in_p):
    t, d = x.shape
    tm = 2048
    gcol = IN_MAIN // 128

    def body(x_ref, nw_ref, sc_ref, sh_ref, w_ref, b_ref, wg_ref, bg_ref, proj_ref, g_ref, h_scr):
        @pl.when(pl.program_id(1) == 0)
        def _():
            h, _, _ = _modulated_norm(x_ref[...], nw_ref[...], sc_ref[...], sh_ref[...])
            h_scr[...] = h.astype(BF16)
            g_ref[...] = _nt(h_scr[...], wg_ref[...]) + bg_ref[...]
        proj_ref[...] = (_nt(h_scr[...], w_ref[...]) + b_ref[...]).astype(BF16)

    row = lambda n: pl.BlockSpec((1, n), lambda i, j: (0, 0))
    return pl.pallas_call(
        body, name="in_proj", grid=(t // tm, IN_MAIN // IN_TN),
        in_specs=[pl.BlockSpec((tm, d), lambda i, j: (i, 0)), row(d), row(d), row(d),
                  pl.BlockSpec((IN_TN, d), lambda i, j: (j, 0)), pl.BlockSpec((1, IN_TN), lambda i, j: (0, j)),
                  pl.BlockSpec((128, d), lambda i, j: (gcol, 0)), pl.BlockSpec((1, 128), lambda i, j: (0, gcol))],
        out_specs=[pl.BlockSpec((tm, IN_TN), lambda i, j: (i, j)), pl.BlockSpec((tm, 128), lambda i, j: (i, 0))],
        out_shape=[jax.ShapeDtypeStruct((t, IN_MAIN), BF16), jax.ShapeDtypeStruct((t, 128), F32)],
        scratch_shapes=[pltpu.VMEM((tm, d), BF16)],
        compiler_params=_cparams(("parallel", "arbitrary")),
    )(x, norm_w, scale, shift, w_in_t, b_in_p, w_in_t, b_in_p)


def _ml_norm_parts(hs, o, z, nw):
    outs = []
    for hh in range(ML_HEADS):
        sl = slice(hh * ML_HEAD_DIM, (hh + 1) * ML_HEAD_DIM)
        so = jax.nn.sigmoid(o[:, sl])
        hm = hs[:, sl] * so
        mu = jnp.mean(hm, axis=-1, keepdims=True)
        cen = hm - mu
        var = jnp.mean(cen * cen, axis=-1, keepdims=True)
        rs = lax.rsqrt(var + EPS)
        outs.append((sl, cen * rs, rs, so))
    return outs


def _tail(o_na, proj, h_f, h_b, x, target, gate, ml_norm_w, fnw, w_out_b):
    t, d = x.shape
    tm = 512

    def body(ona_ref, naz_ref, hf_ref, hb_ref, o_ref, z_ref, x_ref, tg_ref, g_ref, nw_ref, fw_ref, w_ref,
             loss_ref, dres_ref, dona_ref, dnaz_ref, dhs_ref, do_ref, dz_ref, dgate_ref, gfw_ref, gnw_ref,
             gwo_ref, mix_scr):
        @pl.when(pl.program_id(0) == 0)
        def _():
            for r in (loss_ref, dgate_ref, gfw_ref, gnw_ref, gwo_ref):
                r[...] = jnp.zeros_like(r)
        naz = naz_ref[...].astype(F32)
        ona = ona_ref[...]
        sg_naz = jax.nn.sigmoid(naz)
        sna = naz * sg_naz
        mix_scr[:, 0:NA_W] = (ona * sna).astype(BF16)
        hs = hf_ref[...] + hb_ref[...]
        z = z_ref[...].astype(F32)
        ov = o_ref[...].astype(F32)
        parts = _ml_norm_parts(hs, ov, z, nw_ref[...])
        sgz = [jax.nn.sigmoid(z[:, sl]) for sl, _, _, _ in parts]
        for (sl, xn, _, _), sg in zip(parts, sgz):
            mix_scr[:, NA_W + sl.start:NA_W + sl.stop] = (xn * nw_ref[:, sl] * (z[:, sl] * sg)).astype(BF16)
        mixb = mix_scr[...]
        wv = w_ref[...]
        yv = _nn(mixb, wv)
        gate_v = g_ref[...]
        hres = x_ref[...] + gate_v * yv
        r = lax.rsqrt(jnp.mean(hres * hres, axis=-1, keepdims=True) + EPS)
        xnf = hres * r
        err = xnf * fw_ref[...] - tg_ref[...]
        loss_ref[...] += 0.5 * jnp.sum(jnp.sum(err * err, axis=-1, keepdims=True) * (1.0 / d), axis=0, keepdims=True)
        dout = err * (1.0 / d)
        gfw_ref[...] += jnp.sum(dout * xnf, axis=0, keepdims=True)
        dxn = dout * fw_ref[...]
        dres = r * (dxn - xnf * jnp.mean(dxn * xnf, axis=-1, keepdims=True))
        dres_ref[...] = dres
        dgate_ref[...] += jnp.sum(dres * yv, axis=0, keepdims=True)
        dyb = (dres * gate_v).astype(BF16)
        gwo_ref[...] += _tn(mixb, dyb)
        dmix = _nt(dyb, wv)
        dna = dmix[:, 0:NA_W]
        dona_ref[...] = dna * sna
        dnaz_ref[...] = (dna * ona * (sg_naz * (1.0 + naz * (1.0 - sg_naz)))).astype(BF16)
        for (sl, xn, rs, so), sg in zip(parts, sgz):
            dyv = dmix[:, NA_W + sl.start:NA_W + sl.stop]
            zz = z[:, sl]
            sz = zz * sg
            w = nw_ref[:, sl]
            dz_ref[:, sl] = (dyv * xn * w * (sg * (1.0 + zz * (1.0 - sg)))).astype(BF16)
            gnw_ref[:, sl] += jnp.sum(dyv * xn * sz, axis=0, keepdims=True)
            dxm = dyv * w * sz
            dhm = rs * (dxm - jnp.mean(dxm, axis=-1, keepdims=True)
                        - xn * jnp.mean(dxm * xn, axis=-1, keepdims=True))
            dhs_ref[:, sl] = dhm * so
            do_ref[:, sl] = (dhm * hs[:, sl] * so * (1.0 - so)).astype(BF16)

    blk = lambda c: pl.BlockSpec((tm, 512), lambda i, c=c: (i, c))
    full = pl.BlockSpec((tm, d), lambda i: (i, 0))
    o512 = jax.ShapeDtypeStruct((t, 512), F32)
    b512 = jax.ShapeDtypeStruct((t, 512), BF16)
    whole = pl.BlockSpec((d, d), lambda i: (0, 0))
    return pl.pallas_call(
        body, name="tail", grid=(t // tm,),
        in_specs=[blk(0), blk(3), blk(0), blk(0), blk(7), blk(8), full, full, _row(d), _row(ML_W), _row(d), whole],
        out_specs=[pl.BlockSpec((1, 128), lambda i: (0, 0)), full] + [blk(0)] * 5
        + [_row(d), _row(d), _row(ML_W), whole],
        out_shape=[jax.ShapeDtypeStruct((1, 128), F32), jax.ShapeDtypeStruct((t, d), F32),
                   o512, b512, o512, b512, b512]
        + [jax.ShapeDtypeStruct((1, d), F32), jax.ShapeDtypeStruct((1, d), F32),
           jax.ShapeDtypeStruct((1, ML_W), F32), jax.ShapeDtypeStruct((d, d), F32)],
        scratch_shapes=[pltpu.VMEM((tm, d), BF16)],
        compiler_params=_cparams(("arbitrary",)),
    )(o_na, proj, h_f, h_b, proj, proj, x, target, gate, ml_norm_w, fnw, w_out_b)


def _in_bwd(pieces, x, dres, w_in_t, norm_w, scale, shift):
    t, d = x.shape
    tm = 512
    nt = t // tm
    widths = [p.shape[1] for p in pieces]
    offs = [sum(widths[:k]) for k in range(len(widths))]
    assert sum(widths) == IN_PAD
    npc = len(pieces)

    def body(*refs):
        p_refs = refs[:npc]
        (x_ref, dres_ref, w_hbm, nw_ref, sc_ref, sh_ref,
         gx_ref, gw_hbm, gb_ref, dsc_ref, dsh_ref, gnw_ref, w_vmem, acc, stage, sem) = refs[npc:]
        i = pl.program_id(0)

        @pl.when(i == 0)
        def _():
            cp = pltpu.make_async_copy(w_hbm, w_vmem, sem.at[0])
            cp.start()
            acc[...] = jnp.zeros_like(acc)
            gb_ref[...] = jnp.zeros_like(gb_ref)
            dsc_ref[...] = jnp.zeros_like(dsc_ref)
            dsh_ref[...] = jnp.zeros_like(dsh_ref)
            gnw_ref[...] = jnp.zeros_like(gnw_ref)
            cp.wait()

        nw = nw_ref[...]
        s1 = 1.0 + sc_ref[...]
        h, xn, r = _modulated_norm(x_ref[...], nw, sc_ref[...], sh_ref[...])
        hb = h.astype(BF16)
        dhv = jnp.zeros((tm, d), F32)
        for p_ref, c0, w in zip(p_refs, offs, widths):
            pt = p_ref[...]
            pb = pt.astype(BF16)
            dhv = dhv + _nn(pb, w_vmem[c0:c0 + w, :])
            acc[:, c0:c0 + w] += _tn(hb, pb)
            gb_ref[:, c0:c0 + w] += jnp.sum(pt.astype(F32), axis=0, keepdims=True)
        dsh_ref[...] += jnp.sum(dhv, axis=0, keepdims=True)
        dsc_ref[...] += jnp.sum(dhv * xn * nw, axis=0, keepdims=True)
        gnw_ref[...] += jnp.sum(dhv * xn * s1, axis=0, keepdims=True)
        dxn = dhv * nw * s1
        gx_ref[...] = dres_ref[...] + r * (dxn - xn * jnp.mean(dxn * xn, axis=-1, keepdims=True))

        @pl.when(i == nt - 1)
        def _():
            copies = []
            for blk in range(IN_PAD // 128):
                slot = blk % 2
                if blk >= 2:
                    copies[blk - 2].wait()
                stage[slot] = acc[:, blk * 128:(blk + 1) * 128].T
                cp = pltpu.make_async_copy(stage.at[slot], gw_hbm.at[pl.ds(blk * 128, 128), :], sem.at[1 + slot])
                cp.start()
                copies.append(cp)
            copies[-2].wait()
            copies[-1].wait()

    full = pl.BlockSpec((tm, d), lambda i: (i, 0))
    return pl.pallas_call(
        body, name="in_bwd", grid=(nt,),
        in_specs=[pl.BlockSpec((tm, w), lambda i: (i, 0)) for w in widths]
        + [full, full, pl.BlockSpec(memory_space=pl.ANY), _row(d), _row(d), _row(d)],
        out_specs=[full, pl.BlockSpec(memory_space=pl.ANY), _row(IN_PAD), _row(d), _row(d), _row(d)],
        out_shape=[jax.ShapeDtypeStruct((t, d), F32), jax.ShapeDtypeStruct((IN_PAD, d), F32),
                   jax.ShapeDtypeStruct((1, IN_PAD), F32)] + [jax.ShapeDtypeStruct((1, d), F32)] * 3,
        scratch_shapes=[pltpu.VMEM((IN_PAD, d), BF16), pltpu.VMEM((d, IN_PAD), F32),
                        pltpu.VMEM((2, 128, d), F32), pltpu.SemaphoreType.DMA((3,))],
        compiler_params=_cparams(("arbitrary",), IN_BWD_VMEM_LIMIT),
    )(*pieces, x, dres, w_in_t, norm_w, scale, shift)


def _na_static(rows):
    cases = [(0, 0), (NA_QROWS, NA_QROWS - 4), (rows - NA_QROWS, rows - NA_KROWS)]
    dy = np.zeros((3, NA_QROWS, NA_KROWS), np.int32)
    rv = np.zeros((3, NA_QROWS, NA_KROWS), bool)
    for cs, (r0, kr0) in enumerate(cases):
        for i in range(NA_QROWS):
            for j in range(NA_KROWS):
                r, kr = r0 + i, kr0 + j
                rs = min(max(r - NA_KH // 2, 0), rows - NA_KH)
                rv[cs, i, j] = rs <= kr <= rs + NA_KH - 1
                dy[cs, i, j] = min(max(kr - r + NA_KH - 1, 0), 2 * NA_KH - 2)
    cq = np.arange(GRID_W)[:, None]
    ck = np.arange(GRID_W)[None, :]
    cs0 = np.clip(cq - NA_KW // 2, 0, GRID_W - NA_KW)
    cv = (ck >= cs0) & (ck < cs0 + NA_KW)
    dx = np.clip(ck - cq, -(NA_KW - 1), NA_KW - 1) + NA_KW - 1
    return dy, rv, dx.astype(np.int32), cv


def _na_bias_table(rpb, rows):
    _, _, dx, cv = _na_static(rows)
    ndy = 2 * NA_KH - 1
    onehot = (dx.reshape(1, -1) == np.arange(2 * NA_KW - 1)[:, None]).astype(np.float32)
    rpx = jnp.dot(rpb.reshape(NA_HEADS * ndy, 2 * NA_KW - 1), jnp.asarray(onehot), precision=HI)
    rpx = jnp.where(cv[None, None], rpx.reshape(NA_HEADS, ndy, GRID_W, GRID_W), NEG)
    neg = jnp.full((NA_HEADS, 1, GRID_W, GRID_W), NEG, F32)
    rpx = jnp.concatenate([rpx, neg], axis=1)
    nxt = jnp.concatenate([rpx[:, 1:], neg], axis=1)
    negs = jnp.broadcast_to(neg, rpx.shape)
    pairs = jnp.concatenate([jnp.concatenate([rpx, nxt], axis=3), jnp.concatenate([rpx, negs], axis=3),
                             jnp.concatenate([negs, rpx], axis=3)], axis=1)
    npair = pairs.shape[1]

    def body(m_ref, o_ref):
        cs = pl.program_id(1)
        r0 = jnp.where(cs == 0, 0, jnp.where(cs == 1, NA_QROWS, rows - NA_QROWS))
        kr0 = jnp.where(cs == 0, 0, jnp.where(cs == 1, NA_QROWS - NA_KH // 2, rows - NA_KROWS))
        for i in range(NA_QROWS):
            r = r0 + i
            rs = jnp.clip(r - NA_KH // 2, 0, rows - NA_KH)
            for jp in range(NA_KROWS // 2):
                kl = kr0 + 2 * jp
                vl = (kl >= rs) & (kl <= rs + NA_KH - 1)
                vr = (kl + 1 >= rs) & (kl + 1 <= rs + NA_KH - 1)
                dyl = jnp.clip(kl - r + NA_KH - 1, 0, ndy)
                dyr = jnp.clip(kl + 1 - r + NA_KH - 1, 0, ndy)
                idx = jnp.where(vl & vr, dyl, jnp.where(vl, 16 + dyl, jnp.where(vr, 32 + dyr, 16 + ndy)))
                o_ref[0, 0, i * GRID_W:(i + 1) * GRID_W, jp * 128:(jp + 1) * 128] = m_ref[0, idx]

    return pl.pallas_call(
        body, name="na_bias_table", grid=(NA_HEADS, 3),
        in_specs=[pl.BlockSpec((1, npair, GRID_W, 128), lambda h, cs: (h, 0, 0, 0))],
        out_specs=pl.BlockSpec((1, 1, NA_QT, NA_KT), lambda h, cs: (h, cs, 0, 0)),
        out_shape=jax.ShapeDtypeStruct((NA_HEADS, 3, NA_QT, NA_KT), F32),
        compiler_params=_cparams(("parallel", "parallel")),
    )(pairs)


def _na_specs(t):
    nb = t // NA_QT
    nkb = t // NA_KCH
    npieces = NA_KT // NA_KCH

    def kb0(b):
        return jnp.clip(b * (NA_QT // NA_KCH) - 1, 0, nkb - npieces)

    def case(b):
        return jnp.where(b == 0, 0, jnp.where(b == nb - 1, 2, 1))

    q_spec = pl.BlockSpec((NA_QT, 128), lambda p, b: (b, p))
    k_specs = [pl.BlockSpec((NA_KCH, 128), lambda p, b, i=i: (kb0(b) + i, 4 + p)) for i in range(npieces)]
    v_specs = [pl.BlockSpec((NA_KCH, 128), lambda p, b, i=i: (kb0(b) + i, 8 + p)) for i in range(npieces)]
    tbl_spec = pl.BlockSpec((2, 1, NA_QT, NA_KT), lambda p, b: (p, case(b), 0, 0))
    io_spec = pl.BlockSpec((NA_QT, 128), lambda p, b: (b, p))
    return nb, npieces, kb0, case, q_spec, k_specs, v_specs, tbl_spec, io_spec


NA_HALF = NA_QT // 2
NA_COMBOS_ALL = tuple((i, 0, NA_QT) for i in range(NA_KT // NA_KCH))
NA_COMBOS_INNER = ((0, 0, NA_HALF),) + tuple((i, 0, NA_QT) for i in range(1, NA_KT // NA_KCH - 1)) \
    + ((NA_KT // NA_KCH - 1, NA_HALF, NA_QT),)


def _na_place(val, r0, r1):
    if (r0, r1) == (0, NA_QT):
        return val
    z = jnp.zeros((NA_HALF, val.shape[1]), val.dtype)
    return jnp.concatenate([val, z] if r0 == 0 else [z, val], axis=0)


def _na_fwd(proj, tbl):
    t = proj.shape[0]
    nb, npieces, _, _, q_spec, k_specs, v_specs, tbl_spec, io_spec = _na_specs(t)
    lse_spec = pl.BlockSpec((1, NA_QT, 2), lambda p, b: (p, b, 0))

    def body(*refs):
        q_ref = refs[0]
        k_refs = refs[1:1 + npieces]
        v_refs = refs[1 + npieces:1 + 2 * npieces]
        tbl_ref, o_ref, lse_ref = refs[1 + 2 * npieces:]
        b = pl.program_id(1)

        def compute(combos):
            lane = lax.broadcasted_iota(jnp.int32, (1, 128), 1)
            qv = q_ref[...].astype(F32) * (NA_HEAD_DIM ** -0.5)
            ks = [r[...].astype(BF16) for r in k_refs]
            vs = [r[...].astype(BF16) for r in v_refs]
            hs = range(2)
            msk = [(lane // NA_HEAD_DIM) == hh for hh in hs]
            qh = [jnp.where(msk[hh], qv, 0.0).astype(BF16) for hh in hs]
            s = [[_nt(qh[hh][r0:r1], ks[i]) + tbl_ref[hh, 0, r0:r1, i * NA_KCH:(i + 1) * NA_KCH]
                  for i, r0, r1 in combos] for hh in hs]
            for h0 in (0, NA_HALF):
                rows = slice(h0, h0 + NA_HALF)
                cover = [(c, i, h0 - r0) for c, (i, r0, r1) in enumerate(combos) if r0 <= h0 < r1]
                part = [[s[hh][c][off:off + NA_HALF] for c, _, off in cover] for hh in hs]
                m = [functools.reduce(jnp.maximum, [jnp.max(v, axis=1, keepdims=True) for v in part[hh]]) for hh in hs]
                p = [[jnp.exp(v - m[hh]) for v in part[hh]] for hh in hs]
                l = [functools.reduce(jnp.add, [jnp.sum(v, axis=1, keepdims=True) for v in p[hh]]) for hh in hs]
                o = [functools.reduce(jnp.add, [_nn(p[hh][k].astype(BF16), vs[i]) for k, (_, i, _) in enumerate(cover)])
                     for hh in hs]
                for hh in hs:
                    lse_ref[0, rows, hh:hh + 1] = m[hh] + jnp.log(l[hh])
                o_ref[rows, :] = jnp.where(msk[0], o[0] / l[0], o[1] / l[1])

        inner = (b > 0) & (b < nb - 1)
        pl.when(inner)(lambda: compute(NA_COMBOS_INNER))
        pl.when(jnp.logical_not(inner))(lambda: compute(NA_COMBOS_ALL))

    return pl.pallas_call(
        body, name="na_fwd", grid=(4, nb),
        in_specs=[q_spec] + k_specs + v_specs + [tbl_spec],
        out_specs=[io_spec, lse_spec],
        out_shape=[jax.ShapeDtypeStruct((t, NA_W), F32), jax.ShapeDtypeStruct((4, t, 2), F32)],
        compiler_params=_cparams(("parallel", "arbitrary")),
    )(*([proj] * (1 + 2 * npieces)), tbl)


def _na_bwd(proj, tbl, d_o, o_na, lse):
    t = proj.shape[0]
    nb, npieces, kb0, case, q_spec, k_specs, v_specs, tbl_spec, io_spec = _na_specs(t)

    def body(*refs):
        q_ref = refs[0]
        k_refs = refs[1:1 + npieces]
        v_refs = refs[1 + npieces:1 + 2 * npieces]
        (tbl_ref, do_ref, o_ref, lse_ref, dq_ref, dk_hbm, dv_hbm, rpb_ref,
         dk_acc, dv_acc, dk_out, dv_out, s_scr, dp_scr, dsb_scr, pnb_scr, sem) = refs[1 + 2 * npieces:]
        p_id = pl.program_id(0)
        b = pl.program_id(1)

        @pl.when(b == 0)
        def _():
            dk_acc[...] = jnp.zeros_like(dk_acc)
            dv_acc[...] = jnp.zeros_like(dv_acc)

        @pl.when((b == 0) | (b == 1) | (b == nb - 1))
        def _():
            rpb_ref[...] = jnp.zeros_like(rpb_ref)

        def compute(combos):
            lane = lax.broadcasted_iota(jnp.int32, (1, 128), 1)
            scale = NA_HEAD_DIM ** -0.5
            qv = q_ref[...].astype(F32) * scale
            ks = [r[...].astype(BF16) for r in k_refs]
            vs = [r[...].astype(BF16) for r in v_refs]
            dov = do_ref[...]
            ov = o_ref[...]
            tok0 = kb0(b) * NA_KCH
            hs = range(2)
            msk = [(lane // NA_HEAD_DIM) == hh for hh in hs]
            qh = [jnp.where(msk[hh], qv, 0.0).astype(BF16) for hh in hs]
            doh = [jnp.where(msk[hh], dov, 0.0) for hh in hs]
            dohb = [doh[hh].astype(BF16) for hh in hs]
            dd = [jnp.sum(doh[hh] * ov, axis=1, keepdims=True) for hh in hs]
            for hh in hs:
                for c, (i, q0, q1) in enumerate(combos):
                    slot = (hh * len(combos) + c) % 2
                    cols = slice(i * NA_KCH, (i + 1) * NA_KCH)
                    s_scr[slot, 0:q1 - q0] = _nt(qh[hh][q0:q1], ks[i])
                    dp_scr[slot, 0:q1 - q0] = _nt(dohb[hh][q0:q1], vs[i])
                    for r0 in range(q0, q1, NA_RC):
                        rows = slice(r0, r0 + NA_RC)
                        loc = slice(r0 - q0, r0 - q0 + NA_RC)
                        p = jnp.exp(s_scr[slot, loc, :] + tbl_ref[hh, 0, rows, cols] - lse_ref[0, rows, hh:hh + 1])
                        d = p * (dp_scr[slot, loc, :] - dd[hh][rows])
                        pnb_scr[hh, rows, cols] = p.astype(BF16)
                        dsb_scr[hh, rows, cols] = d.astype(BF16)
                done = {(i, q0) for i, q0, _ in combos} | {(i, NA_HALF) for i, q0, q1 in combos if q1 - q0 == NA_QT}
                for i in range(npieces):
                    for q0 in (0, NA_HALF):
                        if (i, q0) not in done:
                            dsb_scr[hh, q0:q0 + NA_HALF, i * NA_KCH:(i + 1) * NA_KCH] = jnp.zeros(
                                (NA_HALF, NA_KCH), BF16)
            dqh = [functools.reduce(jnp.add, [_na_place(_nn(dsb_scr[hh, q0:q1, i * NA_KCH:(i + 1) * NA_KCH], ks[i]),
                                                        q0, q1) for i, q0, q1 in combos]) for hh in hs]
            dq_ref[...] = (jnp.where(msk[0], dqh[0], dqh[1]) * scale).astype(BF16)
            for i in range(npieces):
                rows = pl.ds(pl.multiple_of(tok0 + i * NA_KCH, NA_KCH), NA_KCH)
                cols = slice(i * NA_KCH, (i + 1) * NA_KCH)
                q0, q1 = [(a, e) for j, a, e in combos if j == i][0]
                dk_acc[rows, :] += (_tn(dsb_scr[0, q0:q1, cols], qh[0][q0:q1])
                                    + _tn(dsb_scr[1, q0:q1, cols], qh[1][q0:q1]))
                dv_acc[rows, :] += (_tn(pnb_scr[0, q0:q1, cols], dohb[0][q0:q1])
                                    + _tn(pnb_scr[1, q0:q1, cols], dohb[1][q0:q1]))
            for hh in hs:
                acc = dsb_scr[hh, 0:GRID_W, :].astype(F32)
                for i in range(1, NA_QROWS):
                    acc = acc + pltpu.roll(dsb_scr[hh, i * GRID_W:(i + 1) * GRID_W, :].astype(F32),
                                           NA_KT - i * GRID_W, 1)
                rpb_ref[0, 0, hh] += acc

        inner = (b > 0) & (b < nb - 1)
        pl.when(inner)(lambda: compute(NA_COMBOS_INNER))
        pl.when(jnp.logical_not(inner))(lambda: compute(NA_COMBOS_ALL))

        @pl.when(b == nb - 1)
        def _():
            cols = pl.ds(pl.multiple_of(p_id * 128, 128), 128)
            dk_out[...] = dk_acc[...].astype(BF16)
            dv_out[...] = dv_acc[...].astype(BF16)
            ck = pltpu.make_async_copy(dk_out, dk_hbm.at[:, cols], sem.at[0])
            cv = pltpu.make_async_copy(dv_out, dv_hbm.at[:, cols], sem.at[1])
            ck.start()
            cv.start()
            ck.wait()
            cv.wait()

    o512 = jax.ShapeDtypeStruct((t, NA_W), BF16)
    return pl.pallas_call(
        body, name="na_bwd", grid=(4, nb),
        in_specs=[q_spec] + k_specs + v_specs + [tbl_spec, io_spec, io_spec,
                                                 pl.BlockSpec((1, NA_QT, 2), lambda p, b: (p, b, 0))],
        out_specs=[io_spec, pl.BlockSpec(memory_space=pl.ANY), pl.BlockSpec(memory_space=pl.ANY),
                   pl.BlockSpec((1, 1, 2, GRID_W, NA_KT), lambda p, b: (p, case(b), 0, 0, 0))],
        out_shape=[o512, o512, o512, jax.ShapeDtypeStruct((4, 3, 2, GRID_W, NA_KT), F32)],
        scratch_shapes=[pltpu.VMEM((t, 128), F32), pltpu.VMEM((t, 128), F32),
                        pltpu.VMEM((t, 128), BF16), pltpu.VMEM((t, 128), BF16),
                        pltpu.VMEM((2, NA_QT, NA_KCH), F32), pltpu.VMEM((2, NA_QT, NA_KCH), F32),
                        pltpu.VMEM((2, NA_QT, NA_KT), BF16), pltpu.VMEM((2, NA_QT, NA_KT), BF16),
                        pltpu.SemaphoreType.DMA((2,))],
        compiler_params=_cparams(("arbitrary", "arbitrary")),
    )(*([proj] * (1 + 2 * npieces)), tbl, d_o, o_na, lse)


def _rpb_reduce(rpbacc, rows):
    nacc = 4 * 3 * 2

    def shift_body(a_ref, o_ref):
        acc = a_ref[0, 0:1, :]
        for cq in range(1, GRID_W):
            acc = acc + pltpu.roll(a_ref[0, cq:cq + 1, :], NA_KT - cq, 1)
        o_ref[0] = jnp.broadcast_to(acc, (8, NA_KT))

    vec = pl.pallas_call(
        shift_body, name="rpb_shift", grid=(nacc,),
        in_specs=[pl.BlockSpec((1, GRID_W, NA_KT), lambda a: (a, 0, 0))],
        out_specs=pl.BlockSpec((1, 8, NA_KT), lambda a: (a, 0, 0)),
        out_shape=jax.ShapeDtypeStruct((nacc, 8, NA_KT), F32),
        compiler_params=_cparams(("parallel",)),
    )(rpbacc.reshape(nacc, GRID_W, NA_KT))
    a = vec[:, 0].reshape(4, 3, 2, NA_KT).transpose(0, 2, 1, 3).reshape(NA_HEADS, 3, NA_KT)
    if rows // NA_QROWS < 3:
        a = a.at[:, 1].set(0.0)
    dd = np.arange(NA_KROWS)[:, None]
    dxo = np.arange(-(NA_KW - 1), NA_KW)[None, :]
    idx = ((dd * GRID_W + dxo) % NA_KT).reshape(-1)
    g = a[..., idx].reshape(NA_HEADS, 3 * NA_KROWS, 2 * NA_KW - 1)
    g = jnp.pad(g, ((0, 0), (0, 0), (0, 128 - (2 * NA_KW - 1))))
    nmat = np.zeros((16, 3 * NA_KROWS), np.float32)
    for cs, delta in enumerate((0, -(NA_KH // 2), -(NA_KROWS - NA_QROWS))):
        for d in range(NA_KROWS):
            jmi = d - NA_KROWS if (cs == 0 and d > NA_KH - 1) else d
            dy = jmi + delta + NA_KH - 1
            if 0 <= dy <= 2 * NA_KH - 2:
                nmat[dy, cs * NA_KROWS + d] = 1.0

    def body(n_ref, g_ref, o_ref):
        o_ref[0] = jnp.dot(n_ref[...], g_ref[0], precision=HI, preferred_element_type=F32)

    out = pl.pallas_call(
        body, name="rpb_reduce", grid=(NA_HEADS,),
        in_specs=[pl.BlockSpec((16, nmat.shape[1]), lambda h: (0, 0)),
                  pl.BlockSpec((1, nmat.shape[1], 128), lambda h: (h, 0, 0))],
        out_specs=pl.BlockSpec((1, 16, 128), lambda h: (h, 0, 0)),
        out_shape=jax.ShapeDtypeStruct((NA_HEADS, 16, 128), F32),
        compiler_params=_cparams(("parallel",)),
    )(jnp.asarray(nmat), g)
    return out[:, :2 * NA_KH - 1, :2 * NA_KW - 1]


def _halo_specs(tm, t, col, width=1024):
    nth = t // CONV_HALO
    per = tm // CONV_HALO
    return [pl.BlockSpec((tm, width), lambda i: (i, col)),
            pl.BlockSpec((CONV_HALO, width), lambda i: (jnp.maximum(i * per - 1, 0), col)),
            pl.BlockSpec((CONV_HALO, width), lambda i: (jnp.minimum((i + 1) * per, nth - 1), col))]


def _fill_ext(ext, cur_ref, prev_ref, next_ref, tm, nt):
    i = pl.program_id(0)
    hl = CONV_HALO
    ext[0:hl, :] = jnp.where(i == 0, 0.0, prev_ref[...].astype(F32))
    ext[hl:hl + tm, :] = cur_ref[...].astype(F32)
    ext[hl + tm:2 * hl + tm, :] = jnp.where(i == nt - 1, 0.0, next_ref[...].astype(F32))


CONV_HALO = 16
CONV_RC = 16
CONV_CB = 512


def _conv_chunks(tm):
    return [(slice(cb, cb + CONV_CB), slice(rb, rb + CONV_RC))
            for cb in range(0, 1024, CONV_CB) for rb in range(0, tm, CONV_RC)]


def _conv_fwd(proj, conv_w8, conv_b, tm):
    t = proj.shape[0]
    nt = t // tm

    def body(u_ref, up_ref, un_ref, w_ref, b_ref, pre_ref, act_ref, ext):
        _fill_ext(ext, u_ref, up_ref, un_ref, tm, nt)
        for cs, rs in _conv_chunks(tm):
            pre = b_ref[:, cs] + w_ref[0:1, cs] * ext[pl.ds(rs.start + CONV_HALO - 2, CONV_RC), cs]
            for j in range(1, CONV_W):
                pre = pre + w_ref[j:j + 1, cs] * ext[pl.ds(rs.start + CONV_HALO - 2 + j, CONV_RC), cs]
            pre_ref[rs, cs] = pre
            act_ref[rs, cs] = _silu(pre)

    full = pl.BlockSpec((tm, 1024), lambda i: (i, 0))
    o = jax.ShapeDtypeStruct((t, 1024), F32)
    return pl.pallas_call(
        body, name="conv_fwd", grid=(nt,),
        in_specs=_halo_specs(tm, t, 2) + [pl.BlockSpec((8, 1024), lambda i: (0, 0)), _row(1024)],
        out_specs=[full, full], out_shape=[o, o],
        scratch_shapes=[pltpu.VMEM((tm + 2 * CONV_HALO, 1024), F32)],
        compiler_params=_cparams(("parallel",)),
    )(proj, proj, proj, conv_w8, conv_b)


def _conv_bwd(dq, dk, pre, proj, conv_w8, tm):
    t = pre.shape[0]
    nt = t // tm

    def body(dq_ref, dqp_ref, dqn_ref, dk_ref, dkp_ref, dkn_ref, pre_ref, prep_ref, pren_ref,
             u_ref, up_ref, un_ref, w_ref, du_ref, gw_ref, gb_ref, extd, extu):
        i = pl.program_id(0)
        hl = CONV_HALO

        @pl.when(i == 0)
        def _():
            gw_ref[...] = jnp.zeros_like(gw_ref)
            gb_ref[...] = jnp.zeros_like(gb_ref)
        for rows, dqr, dkr, prr, edge in ((slice(0, hl), dqp_ref, dkp_ref, prep_ref, i == 0),
                                          (slice(hl, hl + tm), dq_ref, dk_ref, pre_ref, None),
                                          (slice(hl + tm, 2 * hl + tm), dqn_ref, dkn_ref, pren_ref, i == nt - 1)):
            ds = _dsilu(prr[...])
            dl = dqr[...] * ds[:, 0:ML_W]
            dr = dkr[...] * ds[:, ML_W:]
            if edge is not None:
                dl = jnp.where(edge, 0.0, dl)
                dr = jnp.where(edge, 0.0, dr)
            extd[rows, 0:ML_W] = dl
            extd[rows, ML_W:] = dr
        _fill_ext(extu, u_ref, up_ref, un_ref, tm, nt)
        gb_ref[...] += jnp.sum(extd[hl:hl + tm, :], axis=0, keepdims=True)
        gacc = None
        for cs, rs in _conv_chunks(tm):
            if rs.start == 0:
                gacc = [jnp.zeros((8, CONV_CB), F32) for _ in range(CONV_W)]
            du = w_ref[0:1, cs] * extd[pl.ds(rs.start + hl + 2, CONV_RC), cs]
            for j in range(1, CONV_W):
                du = du + w_ref[j:j + 1, cs] * extd[pl.ds(rs.start + hl + 2 - j, CONV_RC), cs]
            du_ref[rs, cs] = du.astype(BF16)
            dcur = extd[pl.ds(rs.start + hl, CONV_RC), cs]
            for j in range(CONV_W):
                prod = dcur * extu[pl.ds(rs.start + hl - 2 + j, CONV_RC), cs]
                gacc[j] = gacc[j] + functools.reduce(
                    jnp.add, [prod[k:k + 8] for k in range(0, CONV_RC, 8)])
            if rs.stop == tm:
                for j in range(CONV_W):
                    gw_ref[j:j + 1, cs] += jnp.sum(gacc[j], axis=0, keepdims=True)

    full = pl.BlockSpec((tm, 1024), lambda i: (i, 0))
    return pl.pallas_call(
        body, name="conv_bwd", grid=(nt,),
        in_specs=_halo_specs(tm, t, 0, ML_W) + _halo_specs(tm, t, 0, ML_W) + _halo_specs(tm, t, 0)
        + _halo_specs(tm, t, 2) + [pl.BlockSpec((8, 1024), lambda i: (0, 0))],
        out_specs=[full, pl.BlockSpec((8, 1024), lambda i: (0, 0)), _row(1024)],
        out_shape=[jax.ShapeDtypeStruct((t, 1024), BF16), jax.ShapeDtypeStruct((8, 1024), F32),
                   jax.ShapeDtypeStruct((1, 1024), F32)],
        scratch_shapes=[pltpu.VMEM((tm + 2 * CONV_HALO, 1024), F32), pltpu.VMEM((tm + 2 * CONV_HALO, 1024), F32)],
        compiler_params=_cparams(("arbitrary",)),
    )(dq, dq, dq, dk, dk, dk, pre, pre, pre, proj, proj, proj, conv_w8)


def _ml_consts(rev):
    iu = lax.broadcasted_iota(jnp.int32, (ML_CHUNK, ML_CHUNK), 0)
    js = lax.broadcasted_iota(jnp.int32, (ML_CHUNK, ML_CHUNK), 1)
    eye = iu == js
    le = iu <= js
    ge = iu >= js
    csum, csum_t, sees = (ge, le, ge) if rev else (le, ge, le)
    return eye, csum.astype(F32), csum_t.astype(F32), sees


def _col(row, eye):
    return jnp.sum(jnp.where(eye, row, 0.0), axis=1, keepdims=True)


def _rowof(col, eye):
    return jnp.sum(jnp.where(eye, col, 0.0), axis=0, keepdims=True)


def _row8(row):
    top = lax.broadcasted_iota(jnp.int32, (8, row.shape[1]), 0) == 0
    return jnp.where(top, row, jnp.zeros_like(row))


def _outer_rows(a_row, b_row_bf16):
    hi = a_row.astype(BF16)
    lo = (a_row - hi.astype(F32)).astype(BF16)
    r_a = lax.broadcasted_iota(jnp.int32, (8, a_row.shape[1]), 0)
    r_b = lax.broadcasted_iota(jnp.int32, (8, b_row_bf16.shape[1]), 0)
    lhs = jnp.where(r_a == 0, hi, jnp.where(r_a == 1, lo, jnp.zeros_like(hi)))
    rhs = jnp.where(r_b < 2, b_row_bf16, jnp.zeros_like(b_row_bf16))
    return _tn(lhs, rhs)


def _ml_gates(gi, gf, m0, csum, rev):
    lf = jax.nn.log_sigmoid(gf)
    b_rows = jnp.dot(lf, csum, precision=HI, preferred_element_type=F32)
    bl = jnp.sum(lf, axis=1, keepdims=True)
    a_rows = bl - b_rows + gi
    mloc = jnp.max(a_rows, axis=1, keepdims=True)
    order = list(range(ML_NB))[::-1] if rev else list(range(ML_NB))
    mp, mn, decay = {}, {}, {}
    m = m0
    for n in order:
        mp[n] = m
        m = jnp.maximum(bl[n:n + 1] + m, mloc[n:n + 1])
        mn[n] = m
    for n in order:
        decay[n] = jnp.exp(bl[n:n + 1] + mp[n] - mn[n])
    return b_rows, a_rows, gi - b_rows, mp, mn, decay, order


def _ml_load(q_ref, k_ref, v_ref, n):
    sl = slice(n * ML_CHUNK, (n + 1) * ML_CHUNK)
    qb = q_ref[sl, :].astype(BF16)
    kb = (k_ref[sl, :] * (ML_HEAD_DIM ** -0.5)).astype(BF16)
    vn = v_ref[sl, :].astype(F32)
    return sl, qb, kb, vn


def _ml_state_scan(q_ref, k_ref, v_ref, a_rows, mn, decay, order, c0, n0):
    ns = range(ML_NB)
    ld = [_ml_load(q_ref, k_ref, v_ref, n) for n in ns]
    vt = [ld[n][3].T for n in ns]
    w_row = [jnp.exp(a_rows[n:n + 1] - mn[n]) for n in ns]
    u = [_nn((vt[n] * w_row[n]).astype(BF16), ld[n][2]) for n in ns]
    nu = [_nn(_row8(w_row[n]).astype(BF16), ld[n][2])[0:1] for n in ns]
    cp, npv = {}, {}
    c, nv = c0, n0
    for n in order:
        cp[n], npv[n] = c, nv
        c = decay[n] * c + u[n]
        nv = decay[n] * nv + nu[n]
    return ld, vt, cp, npv, w_row, c, nv


def _ml_intra_all(ld, vt, b_rows, imb_rows, mp, cp, npv, sees, eye):
    ns = range(ML_NB)
    qk = [_nt(ld[n][2], ld[n][1]) for n in ns]
    cq = [_nt(cp[n].astype(BF16), ld[n][1]) for n in ns]
    qn = [_nt(_row8(npv[n]).astype(BF16), ld[n][1])[0:1] for n in ns]
    imb_col = [_col(imb_rows[n:n + 1], eye) for n in ns]
    dlog = [jnp.where(sees, b_rows[n:n + 1] + imb_col[n], NEG) for n in ns]
    m_inter = [b_rows[n:n + 1] + mp[n] for n in ns]
    m_t = [jnp.maximum(m_inter[n], jnp.max(dlog[n], axis=0, keepdims=True)) for n in ns]
    pm = [jnp.exp(dlog[n] - m_t[n]) for n in ns]
    inter = [jnp.exp(m_inter[n] - m_t[n]) for n in ns]
    floor = [jnp.exp(-m_t[n]) for n in ns]
    s = [qk[n] * pm[n] for n in ns]
    sv = [_nn(vt[n].astype(BF16), s[n].astype(BF16)) for n in ns]
    den = [jnp.sum(s[n], axis=0, keepdims=True) + inter[n] * qn[n] for n in ns]
    num = [sv[n] + inter[n] * cq[n] for n in ns]
    dn = [jnp.maximum(jnp.abs(den[n]), floor[n]) for n in ns]
    return [dict(pm=pm[n], s=s[n], inter=inter[n], cq=cq[n], qn=qn[n], num=num[n], den=den[n],
                 floor=floor[n], dn=dn[n]) for n in ns]


def _ml_specs(t, rev):
    nblk = t // ML_TB
    blk = (lambda g: nblk - 1 - g) if rev else (lambda g: g)
    hps = ML_HPS
    tile = lambda c0: pl.BlockSpec((ML_TB, 128 * hps), lambda hg, g, c0=c0: (blk(g), c0 // hps + hg))
    gate = pl.BlockSpec((hps, ML_NB, ML_CHUNK), lambda hg, g: (hg, blk(g), 0))
    cchk = pl.BlockSpec((hps, 1, 128, 128), lambda hg, g: (hg, blk(g), 0, 0))
    nmchk = pl.BlockSpec((hps, 1, 8, 128), lambda hg, g: (hg, blk(g), 0, 0))
    return nblk, blk, tile, gate, cchk, nmchk


def _ml_head_views(refs, hh):
    cols = slice(hh * ML_HEAD_DIM, (hh + 1) * ML_HEAD_DIM)
    return [r.at[:, cols] if len(r.shape) == 2 else r.at[hh] for r in refs]


def _ml_fwd(qk_act, proj, gi, gf, rev, name):
    t = qk_act.shape[0]
    nblk, _, tile, gate, cchk, nmchk = _ml_specs(t, rev)

    def body(*refs):
        for hh in range(ML_HPS):
            one_head(*_ml_head_views(refs, hh))

    def one_head(q_ref, k_ref, v_ref, gi_ref, gf_ref, h_ref, cchk_ref, nmchk_ref, c_ref, nm_ref):
        @pl.when(pl.program_id(1) == 0)
        def _():
            c_ref[...] = jnp.zeros_like(c_ref)
            nm_ref[...] = jnp.zeros_like(nm_ref)
        cchk_ref[0] = c_ref[...]
        nmchk_ref[0] = nm_ref[...]
        eye, csum, _, sees = _ml_consts(rev)
        b_rows, a_rows, imb_rows, mp, mn, decay, order = _ml_gates(
            gi_ref[...], gf_ref[...], nm_ref[1:2, 0:1], csum, rev)
        ld, vt, cp, npv, _, c, nv = _ml_state_scan(q_ref, k_ref, v_ref, a_rows, mn, decay, order,
                                                   c_ref[...], nm_ref[0:1, :])
        c_ref[...] = c
        nm_ref[0:1, :] = nv
        nm_ref[1:2, :] = jnp.broadcast_to(mn[order[-1]], (1, 128))
        rs = _ml_intra_all(ld, vt, b_rows, imb_rows, mp, cp, npv, sees, eye)
        ht = [rs[n]['num'] / rs[n]['dn'] for n in range(ML_NB)]
        for n in range(ML_NB):
            h_ref[n * ML_CHUNK:(n + 1) * ML_CHUNK, :] = ht[n].T

    return pl.pallas_call(
        body, name=name, grid=(ML_HEADS // ML_HPS, nblk),
        in_specs=[tile(0), tile(4), tile(24), gate, gate],
        out_specs=[tile(0), cchk, nmchk],
        out_shape=[jax.ShapeDtypeStruct((t, ML_W), F32),
                   jax.ShapeDtypeStruct((ML_HEADS, nblk, 128, 128), F32),
                   jax.ShapeDtypeStruct((ML_HEADS, nblk, 8, 128), F32)],
        scratch_shapes=[pltpu.VMEM((ML_HPS, 128, 128), F32), pltpu.VMEM((ML_HPS, 8, 128), F32)],
        compiler_params=_cparams(("parallel", "arbitrary")),
    )(qk_act, qk_act, proj, gi, gf)


def _ml_bwd(qk_act, proj, gi, gf, dh, cchk_a, nmchk_a, prev, rev, name):
    t = qk_act.shape[0]
    nblk, _, tile, gate, cchk, nmchk = _ml_specs(t, not rev)

    def body(*refs):
        for hh in range(ML_HPS):
            one_head(*_ml_head_views(refs, hh))

    def one_head(q_ref, k_ref, v_ref, gi_ref, gf_ref, dh_ref, cchk_ref, nmchk_ref, *rest):
        prev_refs = rest[:len(prev)]
        dq_ref, dk_ref, dv_ref, dgi_ref, dgf_ref, dc_ref, dn_ref, db_scr, dbl_scr, di_scr = rest[len(prev):]

        def plus_prev(val, which, rows):
            return val + prev_refs[which][rows, :] if prev else val

        @pl.when(pl.program_id(1) == 0)
        def _():
            dc_ref[...] = jnp.zeros_like(dc_ref)
            dn_ref[...] = jnp.zeros_like(dn_ref)
        eye, csum, csum_t, sees = _ml_consts(rev)
        gfv = gf_ref[...]
        b_rows, a_rows, imb_rows, mp, mn, decay, order = _ml_gates(
            gi_ref[...], gfv, nmchk_ref[0, 1:2, 0:1], csum, rev)
        ld, vt, cp, npv, w_row, _, _ = _ml_state_scan(q_ref, k_ref, v_ref, a_rows, mn, decay, order,
                                                      cchk_ref[0], nmchk_ref[0, 0:1, :])
        ns = range(ML_NB)
        rs = _ml_intra_all(ld, vt, b_rows, imb_rows, mp, cp, npv, sees, eye)
        sls = [ld[n][0] for n in ns]
        qbs = [ld[n][1] for n in ns]
        kbs = [ld[n][2] for n in ns]
        vbs = [ld[n][3].astype(BF16) for n in ns]
        rdn = [1.0 / rs[n]['dn'] for n in ns]
        dnum = [dh_ref[sls[n], :].T * rdn[n] for n in ns]
        hsum = [jnp.sum(dnum[n] * rs[n]['num'], axis=0, keepdims=True) for n in ns]
        dden = [jnp.where(jnp.abs(rs[n]['den']) > rs[n]['floor'],
                          -hsum[n] * rdn[n] * jnp.sign(rs[n]['den']), 0.0) for n in ns]
        dnb = [dnum[n].astype(BF16) for n in ns]
        dsf = [_nn(vbs[n], dnb[n]) + dden[n] for n in ns]
        dv0 = [_nt(rs[n]['s'].astype(BF16), dnb[n]) for n in ns]
        gb = [(dsf[n] * rs[n]['pm']).astype(BF16) for n in ns]
        cpb = [cp[n].astype(BF16) for n in ns]
        idd = [rs[n]['inter'] * dden[n] for n in ns]
        idn = [(rs[n]['inter'] * dnum[n]).astype(BF16) for n in ns]
        dqa = [_tn(gb[n], kbs[n]) for n in ns]
        dqc = [_tn(idn[n], cpb[n]) for n in ns]
        dqn = [_outer_rows(idd[n], npv[n].astype(BF16)) for n in ns]
        dk0 = [_nn(gb[n], qbs[n]) for n in ns]
        xs = [_nn(idn[n], qbs[n]) for n in ns]
        for n in ns:
            dq_ref[sls[n], :] = plus_prev(dqa[n] + dqc[n] + dqn[n], 0, sls[n])
        rr = [dsf[n] * rs[n]['s'] for n in ns]
        dinter = [jnp.sum(dnum[n] * rs[n]['cq'], axis=0, keepdims=True) + dden[n] * rs[n]['qn'] for n in ns]
        dbt = [jnp.sum(rr[n], axis=0, keepdims=True) + dinter[n] * rs[n]['inter'] for n in ns]
        dimb = [jnp.sum(rr[n], axis=1, keepdims=True) for n in ns]
        xns = [_nn(_row8(idd[n]).astype(BF16), qbs[n])[0:1] for n in ns]
        dcn, dnn = {}, {}
        dc, dn = dc_ref[...], dn_ref[0:1, :]
        for n in order[::-1]:
            dcn[n], dnn[n] = dc, dn
            dc = decay[n] * dc + xs[n]
            dn = decay[n] * dn + xns[n]
        dc_ref[...] = dc
        dn_ref[0:1, :] = dn
        kscale = ML_HEAD_DIM ** -0.5
        dcb = [dcn[n].astype(BF16) for n in ns]
        z = [_nn(vbs[n], dcb[n]) for n in ns]
        kd = [_nt(kbs[n], dcb[n]) for n in ns]
        ddecay = [jnp.sum(jnp.sum(dcn[n] * cp[n], axis=1, keepdims=True), axis=0, keepdims=True)
                  + jnp.sum(dnn[n] * npv[n], axis=1, keepdims=True) for n in ns]
        zd = [z[n] + dnn[n] for n in ns]
        dw = [jnp.sum(zd[n] * kbs[n].astype(F32), axis=1, keepdims=True) for n in ns]
        wcol = [_col(w_row[n], eye) for n in ns]
        for n in ns:
            dv_ref[sls[n], :] = plus_prev(dv0[n] + wcol[n] * kd[n], 2, sls[n]).astype(dv_ref.dtype)
            dk_ref[sls[n], :] = plus_prev((dk0[n] + wcol[n] * zd[n]) * kscale, 1, sls[n])
        da = [dw[n] * wcol[n] for n in ns]
        dbl = [jnp.sum(da[n], axis=0, keepdims=True) + ddecay[n] * decay[n] for n in ns]
        key_row = [_rowof(dimb[n] + da[n], eye) for n in ns]
        for n in ns:
            db_scr[n:n + 1, :] = dbt[n] - key_row[n]
            di_scr[n:n + 1, :] = key_row[n]
            dbl_scr[n:n + 1, :] = jnp.broadcast_to(dbl[n], (1, ML_CHUNK))
        dlf = jnp.dot(db_scr[...], csum_t, precision=HI, preferred_element_type=F32) + dbl_scr[...]
        dgf_ref[...] = dlf * jax.nn.sigmoid(-gfv)
        dgi_ref[...] = di_scr[...]

    nc = t // ML_CHUNK
    o512 = jax.ShapeDtypeStruct((t, ML_W), F32)
    og = jax.ShapeDtypeStruct((ML_HEADS, nc, ML_CHUNK), F32)
    return pl.pallas_call(
        body, name=name, grid=(ML_HEADS // ML_HPS, nblk),
        in_specs=[tile(0), tile(4), tile(24), gate, gate, tile(0), cchk, nmchk] + [tile(0)] * len(prev),
        out_specs=[tile(0), tile(0), tile(0), gate, gate],
        out_shape=[o512, o512, jax.ShapeDtypeStruct((t, ML_W), BF16 if prev else F32), og, og],
        scratch_shapes=[pltpu.VMEM((ML_HPS, 128, 128), F32), pltpu.VMEM((ML_HPS, 8, 128), F32)]
        + [pltpu.VMEM((ML_HPS, ML_NB, ML_CHUNK), F32)] * 3,
        compiler_params=_cparams(("parallel", "arbitrary")),
    )(qk_act, qk_act, proj, gi, gf, dh, cchk_a, nmchk_a, *prev)


def _gate_rows(gates16, t):
    g = gates16.reshape(t // ML_CHUNK, ML_CHUNK, 4, ML_HEADS).transpose(2, 3, 0, 1)
    return g[0], g[1], g[2], g[3]


def _gate_cols(dgi_f, dgf_f, dgi_b, dgf_b, t):
    g = jnp.stack([dgi_f, dgf_f, dgi_b, dgf_b]).transpose(2, 3, 0, 1).reshape(t, 4 * ML_HEADS)
    return jnp.pad(g, ((0, 0), (0, 128 - 4 * ML_HEADS)))


def _local_step(x, target, shift, scale, gate, norm_w, w_in_t, b_in_p, conv_w8, conv_b, rpb,
                ml_norm_w, w_out_b, final_norm_w):
    t = x.shape[0]
    rows = t // GRID_W
    tm = 512
    proj, gates = _in_proj(x, norm_w, scale, shift, w_in_t, b_in_p)
    tbl = _na_bias_table(rpb, rows)
    o_na, lse_na = _na_fwd(proj, tbl)
    pre, qk_act = _conv_fwd(proj, conv_w8, conv_b, 2 * tm)
    gi_f, gf_f, gi_b, gf_b = _gate_rows(gates[:, :4 * ML_HEADS], t)
    h_f, cchk_f, nmchk_f = _ml_fwd(qk_act, proj, gi_f, gf_f, False, "ml_fwd_f")
    h_b, cchk_b, nmchk_b = _ml_fwd(qk_act, proj, gi_b, gf_b, True, "ml_fwd_b")
    (loss, dres, d_ona, d_naz, dhs, d_o, d_z, dgate, g_fnw, g_mlnw, g_w_out) = _tail(
        o_na, proj, h_f, h_b, x, target, gate, ml_norm_w, final_norm_w, w_out_b)
    dq_na, dk_na, dv_na, rpbacc = _na_bwd(proj, tbl, d_ona, o_na, lse_na)
    g_rpb = _rpb_reduce(rpbacc, rows)
    dq_f, dk_f, dv_f, dgi_f, dgf_f = _ml_bwd(qk_act, proj, gi_f, gf_f, dhs, cchk_f, nmchk_f, (),
                                             False, "ml_bwd_f")
    dq_ml, dk_ml, dv_ml, dgi_b, dgf_b = _ml_bwd(qk_act, proj, gi_b, gf_b, dhs, cchk_b, nmchk_b, (dq_f, dk_f, dv_f),
                                                True, "ml_bwd_b")
    du, g_conv_w, g_conv_b = _conv_bwd(dq_ml, dk_ml, pre, proj, conv_w8, tm)
    dgates = _gate_cols(dgi_f, dgf_f, dgi_b, dgf_b, t)
    grad_x, g_w_in, g_b_in, dscale, dshift, g_nw = _in_bwd(
        [dq_na, dk_na, dv_na, d_naz, du, dv_ml, d_o, d_z, dgates], x, dres, w_in_t, norm_w, scale, shift)
    dmod = jnp.concatenate([dshift, dscale, dgate], axis=1)
    return (loss, grad_x, dmod, g_nw, g_w_in, g_b_in, g_conv_w, g_conv_b, g_rpb, g_mlnw, g_w_out, g_fnw)


MESH = pl.DeviceIdType.MESH
N_DEV = 8
ANY = pl.BlockSpec(memory_space=pl.ANY)
WHOLE_VMEM = pl.BlockSpec(memory_space=pltpu.VMEM)


def _allgather8(blocks, name):
    na = len(blocks)

    def body(*refs):
        x_refs = refs[:na]
        out_refs = refs[na:2 * na]
        send_sems, recv_sems, local_sems = refs[2 * na:]
        x, y, c = lax.axis_index("x"), lax.axis_index("y"), lax.axis_index("c")
        me, sibling = (x, y, c), (x, y, 1 - c)
        chips = [(1 - x, y), (x, 1 - y), (1 - x, 1 - y)]

        def rows(a, px, py, pc):
            return out_refs[a].at[4 * px + 2 * py + pc]

        def copy(a, k, block, to, src=None):
            return pltpu.make_async_remote_copy(
                src_ref=rows(a, *block) if src is None else src, dst_ref=rows(a, *block),
                send_sem=send_sems.at[a, k], recv_sem=recv_sems.at[a, k],
                device_id=to, device_id_type=MESH)

        mine, first, passed = [], [], []
        for a in range(na):
            cp = pltpu.make_async_copy(x_refs[a], rows(a, *me), local_sems.at[a])
            cp.start()
            mine.append(cp)
            first.append(copy(a, 0, me, sibling, src=x_refs[a]))
            first += [copy(a, 1 + j, me, (*chip, c), src=x_refs[a]) for j, chip in enumerate(chips)]
        for cp in first:
            cp.start()
        for a in range(na):
            for j, chip in enumerate(chips):
                copy(a, 1 + j, (*chip, c), me).wait_recv()
                fwd = copy(a, 4 + j, (*chip, c), sibling)
                fwd.start()
                passed.append(fwd)
        for a in range(na):
            copy(a, 0, sibling, me).wait_recv()
            for j, chip in enumerate(chips):
                copy(a, 4 + j, (*chip, 1 - c), me).wait_recv()
        for cp in first + passed:
            cp.wait_send()
        for cp in mine:
            cp.wait()

    return pl.pallas_call(
        body, name=name,
        out_shape=[jax.ShapeDtypeStruct((N_DEV,) + b.shape, b.dtype) for b in blocks],
        in_specs=[WHOLE_VMEM] * na, out_specs=[WHOLE_VMEM] * na,
        scratch_shapes=[pltpu.SemaphoreType.DMA((na, 7)), pltpu.SemaphoreType.DMA((na, 7)),
                        pltpu.SemaphoreType.DMA((na,))],
        compiler_params=pltpu.CompilerParams(vmem_limit_bytes=VMEM_LIMIT),
    )(*blocks)


def _pair_exchange(arrs, name):
    na = len(arrs)

    def body(*refs):
        in_refs = refs[:na]
        out_refs = refs[na:2 * na]
        send_sems, recv_sems = refs[2 * na:]
        sibling = (lax.axis_index("x"), lax.axis_index("y"), 1 - lax.axis_index("c"))
        copies = [pltpu.make_async_remote_copy(
            src_ref=in_refs[a], dst_ref=out_refs[a], send_sem=send_sems.at[a], recv_sem=recv_sems.at[a],
            device_id=sibling, device_id_type=MESH) for a in range(na)]
        for cp in copies:
            cp.start()
        for cp in copies:
            cp.wait()

    return pl.pallas_call(
        body, name=name,
        out_shape=[jax.ShapeDtypeStruct(a.shape, a.dtype) for a in arrs],
        in_specs=[ANY] * na, out_specs=[ANY] * na,
        scratch_shapes=[pltpu.SemaphoreType.DMA((na,)), pltpu.SemaphoreType.DMA((na,))],
    )(*arrs)


def _chip_exchange(arrs, name):
    na = len(arrs)

    def body(*refs):
        in_refs = refs[:na]
        out_refs = refs[na:2 * na]
        send_sems, recv_sems, local_sems = refs[2 * na:]
        x, y, c = lax.axis_index("x"), lax.axis_index("y"), lax.axis_index("c")
        my_chip = 2 * x + y
        chips = [(1 - x, y), (x, 1 - y), (1 - x, 1 - y)]
        local, remote = [], []
        for a in range(na):
            cp = pltpu.make_async_copy(in_refs[a].at[my_chip], out_refs[a].at[my_chip], local_sems.at[a])
            cp.start()
            local.append(cp)
            for j, (px, py) in enumerate(chips):
                cp = pltpu.make_async_remote_copy(
                    src_ref=in_refs[a].at[2 * px + py], dst_ref=out_refs[a].at[my_chip],
                    send_sem=send_sems.at[a, j], recv_sem=recv_sems.at[a, j],
                    device_id=(px, py, c), device_id_type=MESH)
                cp.start()
                remote.append(cp)
        for cp in remote:
            cp.wait()
        for cp in local:
            cp.wait()

    return pl.pallas_call(
        body, name=name,
        out_shape=[jax.ShapeDtypeStruct(a.shape, a.dtype) for a in arrs],
        in_specs=[ANY] * na, out_specs=[ANY] * na,
        scratch_shapes=[pltpu.SemaphoreType.DMA((na, 3)), pltpu.SemaphoreType.DMA((na, 3)),
                        pltpu.SemaphoreType.DMA((na,))],
    )(*arrs)


def _rows_tile(r):
    for cand in (512, 256, 128, 64, 32, 16, 8):
        if r % cand == 0:
            return cand
    return r


def _add2(a, b, name, out_dtype):
    s, r, n = a.shape
    tr = _rows_tile(r)

    def body(a_ref, b_ref, o_ref):
        o_ref[...] = (a_ref[...] + b_ref[...]).astype(out_dtype)

    spec = pl.BlockSpec((1, tr, n), lambda i, j: (i, j, 0))
    return pl.pallas_call(
        body, name=name, grid=(s, r // tr), in_specs=[spec, spec], out_specs=spec,
        out_shape=jax.ShapeDtypeStruct(a.shape, out_dtype),
        compiler_params=_cparams(("parallel", "parallel")),
    )(a, b)


def _sum_slabs(a, name):
    s, r, n = a.shape
    tr = _rows_tile(r)

    def body(a_ref, o_ref):
        acc = a_ref[0].astype(F32)
        for k in range(1, s):
            acc = acc + a_ref[k].astype(F32)
        o_ref[...] = acc

    return pl.pallas_call(
        body, name=name, grid=(r // tr,),
        in_specs=[pl.BlockSpec((s, tr, n), lambda i: (0, i, 0))],
        out_specs=pl.BlockSpec((tr, n), lambda i: (i, 0)),
        out_shape=jax.ShapeDtypeStruct((r, n), F32),
        compiler_params=_cparams(("parallel",)),
    )(a)


ADAMW_WHOLE = 64 * 1024


def _adamw(w, g, m, v, name):
    r, n = w.shape
    if r * n <= ADAMW_WHOLE:
        blk, grid, imap = (r, n), (1,), (lambda i: (0, 0))
    elif r % 8 == 0:
        blk, grid, imap = (_rows_tile(r), n), (r // _rows_tile(r),), (lambda i: (i, 0))
    else:
        blk, grid, imap = (r, 128), (n // 128,), (lambda i: (0, i))
    c1 = 1.0 / (1.0 - ADAM_B1 ** ADAM_STEP)
    c2 = 1.0 / (1.0 - ADAM_B2 ** ADAM_STEP)

    def body(w_ref, g_ref, m_ref, v_ref, d_ref, nm_ref, nv_ref):
        gv = g_ref[...]
        nm = ADAM_B1 * m_ref[...] + (1.0 - ADAM_B1) * gv
        nv = ADAM_B2 * v_ref[...] + (1.0 - ADAM_B2) * (gv * gv)
        nm_ref[...] = nm
        nv_ref[...] = nv
        d_ref[...] = -ADAM_LR * ((nm * c1) / (jnp.sqrt(nv * c2) + ADAM_EPS) + ADAM_WD * w_ref[...])

    spec = pl.BlockSpec(blk, imap)
    o = jax.ShapeDtypeStruct((r, n), F32)
    return pl.pallas_call(
        body, name=name, grid=grid, in_specs=[spec] * 4, out_specs=[spec] * 3, out_shape=[o, o, o],
        compiler_params=_cparams(("parallel",)),
    )(w, g, m, v)


def _mod_fwd(c_all, w_ada_s, b_ada_s):
    def body(c_ref, w_ref, b_ref, o_ref):
        o_ref[...] = jnp.dot(_silu(c_ref[...]), w_ref[...], precision=HI, preferred_element_type=F32) + b_ref[...]

    return pl.pallas_call(
        body, name="mod_fwd", out_shape=jax.ShapeDtypeStruct((c_all.shape[0], w_ada_s.shape[1]), F32),
        in_specs=[WHOLE_VMEM] * 3, out_specs=WHOLE_VMEM,
        compiler_params=pltpu.CompilerParams(vmem_limit_bytes=VMEM_LIMIT),
    )(c_all, w_ada_s, b_ada_s)


def _wada_grad(c_all, dmod_s):
    def body(c_ref, d_ref, o_ref):
        o_ref[...] = lax.dot_general(_silu(c_ref[...]), d_ref[...], (((0,), (0,)), ((), ())),
                                     precision=HI, preferred_element_type=F32)

    return pl.pallas_call(
        body, name="w_ada_grad", out_shape=jax.ShapeDtypeStruct((c_all.shape[1], dmod_s.shape[1]), F32),
        in_specs=[WHOLE_VMEM] * 2, out_specs=WHOLE_VMEM,
        compiler_params=pltpu.CompilerParams(vmem_limit_bytes=VMEM_LIMIT),
    )(c_all, dmod_s)


SMALL_ROWS = 24


def _pad_rows(v, nrows):
    v = v.reshape(-1)
    return jnp.pad(v, (0, nrows * 1024 - v.shape[0])).reshape(nrows, 1024)


def _pack_small(b_ada, norm_w, b_in, conv_w_full, conv_b, rpb, ml_norm_w, final_norm_w, last):
    parts = [_pad_rows(b_ada, 3), _pad_rows(norm_w, 1), _pad_rows(b_in, 5), _pad_rows(conv_w_full, 5),
             _pad_rows(conv_b, 1), _pad_rows(rpb, 4), _pad_rows(ml_norm_w, 1), _pad_rows(final_norm_w, 1),
             _pad_rows(last, 3)]
    return jnp.concatenate(parts, axis=0)


def _unpack_small(p):
    return dict(b_ada=p[0:3].reshape(1, 3072), norm_w=p[3:4], b_in=p[4:9].reshape(-1)[:IN_W].reshape(1, IN_W),
                conv_w=p[9:14], conv_b=p[14:15],
                rpb=p[15:19].reshape(-1)[:NA_HEADS * 15 * 31].reshape(1, NA_HEADS, 15, 31),
                ml_norm_w=p[19:20, :ML_W], final_norm_w=p[20], last=p[21])


def kernel(x, c, w_ada, b_ada, norm_w, w_in, b_in, conv_w, conv_b, rpb, ml_norm_w, w_out, final_norm_w, loss_target, m_w_ada, m_b_ada, m_norm_w, m_w_in, m_b_in, m_conv_w, m_conv_b, m_rpb, m_ml_norm_w, m_w_out, m_final_norm_w, v_w_ada, v_b_ada, v_norm_w, v_w_in, v_b_in, v_conv_w, v_conv_b, v_rpb, v_ml_norm_w, v_w_out, v_final_norm_w):
    xi, yi, ci = lax.axis_index("x"), lax.axis_index("y"), lax.axis_index("c")
    chip = 2 * xi + yi
    dev = 2 * chip + ci
    t = x.shape[1]
    ada_n = w_ada.shape[2]
    in_n = w_in.shape[2]
    out_r = w_out.shape[1]

    c_blk = jnp.pad(c, ((0, 7), (0, 0)))
    w_in_t, m_w_in_t, v_w_in_t = w_in[0].T, m_w_in[0].T, v_w_in[0].T
    in_h = in_n // 2
    w_in_half = lax.dynamic_slice_in_dim(w_in_t, ci * in_h, in_h, axis=0).astype(BF16)
    w_out_half = lax.dynamic_slice_in_dim(w_out[0], ci * (out_r // 2), out_r // 2, axis=0).astype(BF16)
    conv_blk = jnp.pad(conv_w[0], ((0, 3), (0, 0)))
    c_g, conv_g, w_in_g, w_out_g = _allgather8([c_blk, conv_blk, w_in_half, w_out_half], "gather_c_weights")
    c_all = c_g[:, 0]
    w_out_g = w_out_g.reshape(D_MODEL, D_MODEL)
    b_ada_s = lax.dynamic_slice_in_dim(b_ada, chip * ada_n, ada_n, axis=1)
    mod_s = _mod_fwd(c_all, w_ada[0], b_ada_s)
    (mod_g,) = _allgather8([mod_s], "gather_mod")
    mod_mine = lax.dynamic_index_in_dim(mod_g, dev, axis=1, keepdims=False)
    mod = mod_mine[0::2].reshape(1, 3 * D_MODEL)
    shift, scale, gate = mod[:, :D_MODEL], mod[:, D_MODEL:2 * D_MODEL], mod[:, 2 * D_MODEL:]

    w_in_tp = jnp.pad(w_in_g.reshape(IN_W, D_MODEL), ((0, IN_PAD - IN_W), (0, 0)))
    b_in_p = jnp.pad(b_in, ((0, 0), (0, IN_PAD - IN_W)))
    conv_w8 = conv_g.reshape(4, 2, 8, conv_w.shape[2])[:, 0].transpose(1, 0, 2).reshape(8, D_MODEL)

    (loss, grad_x, dmod, g_nw, g_w_in, g_b_in, g_conv_w, g_conv_b, g_rpb, g_mlnw, g_w_out, g_fnw) = _local_step(
        x[0], loss_target[0], shift, scale, gate, norm_w, w_in_tp, b_in_p, conv_w8, conv_b, rpb[0],
        ml_norm_w, w_out_g, final_norm_w.reshape(1, D_MODEL))

    g_in_t = g_w_in

    def halves(a, per_chip, h):
        return jnp.stack([lax.dynamic_slice_in_dim(a, k * per_chip + h * (per_chip // 2), per_chip // 2, axis=0)
                          for k in range(4)])

    ri, ro = _pair_exchange([halves(g_in_t, in_n, 1 - ci), halves(g_w_out, out_r, 1 - ci)], "rs_pair")
    pi = _add2(halves(g_in_t, in_n, ci), ri, "rs_pair_add_in", BF16)
    po = _add2(halves(g_w_out, out_r, ci), ro, "rs_pair_add_out", BF16)
    qi, qo = _chip_exchange([pi, po], "rs_chips")
    si = _sum_slabs(qi, "rs_sum_in")
    so = _sum_slabs(qo, "rs_sum_out")
    ti, to = _pair_exchange([si, so], "rs_share")
    g_w_in_s = jnp.where(ci == 0, jnp.concatenate([si, ti], axis=0), jnp.concatenate([ti, si], axis=0))
    g_w_out_s = jnp.where(ci == 0, jnp.concatenate([so, to], axis=0), jnp.concatenate([to, so], axis=0))

    small = _pack_small(dmod, g_nw, g_b_in[:, :IN_W], g_conv_w[:CONV_W], g_conv_b, g_rpb, g_mlnw, g_fnw,
                        jnp.pad(loss, ((0, 0), (0, 1024 - 128))))
    (small_g,) = _allgather8([small], "gather_small")
    small_sum = _sum_slabs(small_g, "small_sum")
    gs = _unpack_small(small_sum)
    dmod_all = small_g[:, 0:3].reshape(N_DEV, 3 * D_MODEL)
    g_w_ada_s = _wada_grad(c_all, lax.dynamic_slice_in_dim(dmod_all, chip * ada_n, ada_n, axis=1))
    g_conv_w_s = lax.dynamic_slice_in_dim(gs['conv_w'], chip * conv_w.shape[2], conv_w.shape[2], axis=1)
    loss_total = gs['last'][0]

    small_names = ('b_ada', 'norm_w', 'b_in', 'conv_b', 'rpb', 'ml_norm_w', 'final_norm_w')
    small_w = (b_ada, norm_w, b_in, conv_b, rpb, ml_norm_w, final_norm_w)
    small_m = (m_b_ada, m_norm_w, m_b_in, m_conv_b, m_rpb, m_ml_norm_w, m_final_norm_w)
    small_v = (v_b_ada, v_norm_w, v_b_in, v_conv_b, v_rpb, v_ml_norm_w, v_final_norm_w)
    ds_, nms, nvs = {}, {}, {}
    for nm_, w_, m_, v_ in zip(small_names, small_w, small_m, small_v):
        two_d = (NA_HEADS, w_.size // NA_HEADS) if nm_ == 'rpb' else (1, w_.size)
        outs = _adamw(w_.reshape(two_d), gs[nm_].reshape(two_d), m_.reshape(two_d), v_.reshape(two_d),
                      "adamw_" + nm_)
        ds_[nm_], nms[nm_], nvs[nm_] = [o.reshape(w_.shape) for o in outs]
    d_ada, nm_ada, nv_ada = _adamw(w_ada[0], g_w_ada_s, m_w_ada[0], v_w_ada[0], "adamw_w_ada")
    d_in, nm_in, nv_in = _adamw(w_in_t, g_w_in_s, m_w_in_t, v_w_in_t, "adamw_w_in")
    d_out, nm_out, nv_out = _adamw(w_out[0], g_w_out_s, m_w_out[0], v_w_out[0], "adamw_w_out")
    d_cw, nm_cw, nv_cw = _adamw(conv_w[0], g_conv_w_s, m_conv_w[0], v_conv_w[0], "adamw_conv_w")

    def group(big_ada, big_in, big_out, cw, sm):
        return (big_ada[None], sm['b_ada'], sm['norm_w'], big_in.T[None], sm['b_in'], cw[None], sm['conv_b'],
                sm['rpb'], sm['ml_norm_w'], big_out[None], sm['final_norm_w'])

    return ((loss_total, grad_x[None])
            + group(g_w_ada_s, g_w_in_s, g_w_out_s, g_conv_w_s, gs)
            + group(d_ada, d_in, d_out, d_cw, ds_)
            + group(nm_ada, nm_in, nm_out, nm_cw, nms)
            + group(nv_ada, nv_in, nv_out, nv_cw, nvs))
```

```python
import functools

import numpy as np
import jax
import jax.numpy as jnp
from jax import lax
from jax.experimental import pallas as pl
from jax.experimental.pallas import tpu as pltpu

F32 = jnp.float32
BF16 = jnp.bfloat16
HI = lax.Precision.HIGHEST

D_MODEL = 1024
GRID_W = 64
NA_W = 512
NA_HEAD_DIM = 64
NA_HEADS = 8
NA_KH = 8
NA_KW = 16
ML_W = 512
ML_HEADS = 4
ML_HEAD_DIM = 128
ML_CHUNK = 128
CONV_W = 5
EPS = 1e-6
IN_W = 4 * NA_W + 5 * ML_W + 4 * ML_HEADS
IN_MAIN = 4 * NA_W + 5 * ML_W
IN_PAD = IN_MAIN + 128
NEG = -1e30

ADAM_LR = 0.001
ADAM_B1 = 0.9
ADAM_B2 = 0.999
ADAM_EPS = 1e-08
ADAM_WD = 0.01
ADAM_STEP = 10

NA_QROWS = 8
NA_KROWS = 16
NA_QT = NA_QROWS * GRID_W
NA_KT = NA_KROWS * GRID_W
NA_KCH = 256
NA_RC = 32
ML_NB = 32
ML_TB = ML_NB * ML_CHUNK
ML_HPS = 1

VMEM_LIMIT = 56 * 1024 * 1024
IN_BWD_VMEM_LIMIT = 60 * 1024 * 1024


def _cparams(sem, vmem=VMEM_LIMIT):
    return pltpu.CompilerParams(dimension_semantics=sem, vmem_limit_bytes=vmem)


def _silu(x):
    return x * jax.nn.sigmoid(x)


def _dsilu(x):
    s = jax.nn.sigmoid(x)
    return s * (1.0 + x * (1.0 - s))


def _dot(a, b, dims):
    return lax.dot_general(a, b, (dims, ((), ())), preferred_element_type=F32)


def _nn(a, b):
    return _dot(a, b, ((1,), (0,)))


def _nt(a, b):
    return _dot(a, b, ((1,), (1,)))


def _tn(a, b):
    return _dot(a, b, ((0,), (0,)))


def _row(n):
    return pl.BlockSpec((1, n), lambda i: (0, 0))


def _modulated_norm(xv, nw, sc, sh):
    r = lax.rsqrt(jnp.mean(xv * xv, axis=-1, keepdims=True) + EPS)
    xn = xv * r
    return xn * nw * (1.0 + sc) + sh, xn, r


IN_TN = 768


def _in_proj(x, norm_w, scale, shift, w_in_t, b_in_p):
    t, d = x.shape
    tm = 2048
    gcol = IN_MAIN // 128

    slabs = IN_TN // 128
    na_tiles = 3 * NA_W // IN_TN

    def body(x_ref, nw_ref, sc_ref, sh_ref, w_ref, b_ref, wg_ref, bg_ref, proj_ref, g_ref, na_ref, h_scr):
        j = pl.program_id(1)

        @pl.when(j == 0)
        def _():
            h, _, _ = _modulated_norm(x_ref[...], nw_ref[...], sc_ref[...], sh_ref[...])
            h_scr[...] = h.astype(BF16)
            g_ref[...] = _nt(h_scr[...], wg_ref[...]) + bg_ref[...]
        val = (_nt(h_scr[...], w_ref[...]) + b_ref[...]).astype(BF16)
        proj_ref[...] = val

        @pl.when(j < na_tiles)
        def _():
            for k in range(slabs):
                na_ref[k] = val[:, k * 128:(k + 1) * 128]

    row = lambda n: pl.BlockSpec((1, n), lambda i, j: (0, 0))
    return pl.pallas_call(
        body, name="in_proj", grid=(t // tm, IN_MAIN // IN_TN),
        in_specs=[pl.BlockSpec((tm, d), lambda i, j: (i, 0)), row(d), row(d), row(d),
                  pl.BlockSpec((IN_TN, d), lambda i, j: (j, 0)), pl.BlockSpec((1, IN_TN), lambda i, j: (0, j)),
                  pl.BlockSpec((128, d), lambda i, j: (gcol, 0)), pl.BlockSpec((1, 128), lambda i, j: (0, gcol))],
        out_specs=[pl.BlockSpec((tm, IN_TN), lambda i, j: (i, j)), pl.BlockSpec((tm, 128), lambda i, j: (i, 0)),
                   pl.BlockSpec((slabs, tm, 128), lambda i, j: (jnp.minimum(j, na_tiles - 1), i, 0))],
        out_shape=[jax.ShapeDtypeStruct((t, IN_MAIN), BF16), jax.ShapeDtypeStruct((t, 128), F32),
                   jax.ShapeDtypeStruct((na_tiles * slabs, t, 128), BF16)],
        scratch_shapes=[pltpu.VMEM((tm, d), BF16)],
        compiler_params=_cparams(("parallel", "arbitrary")),
    )(x, norm_w, scale, shift, w_in_t, b_in_p, w_in_t, b_in_p)


def _ml_norm_parts(hs, o, z, nw):
    outs = []
    for hh in range(ML_HEADS):
        sl = slice(hh * ML_HEAD_DIM, (hh + 1) * ML_HEAD_DIM)
        so = jax.nn.sigmoid(o[:, sl])
        hm = hs[:, sl] * so
        mu = jnp.mean(hm, axis=-1, keepdims=True)
        cen = hm - mu
        var = jnp.mean(cen * cen, axis=-1, keepdims=True)
        rs = lax.rsqrt(var + EPS)
        outs.append((sl, cen * rs, rs, so))
    return outs


def _tail(o_na, proj, h_f, h_b, x, target, gate, ml_norm_w, fnw, w_out_b):
    t, d = x.shape
    tm = 512

    def body(ona_ref, naz_ref, hf_ref, hb_ref, o_ref, z_ref, x_ref, tg_ref, g_ref, nw_ref, fw_ref, w_ref,
             loss_ref, dres_ref, dona_ref, dnaz_ref, dhs_ref, do_ref, dz_ref, dgate_ref, gfw_ref, gnw_ref,
             gwo_ref, mix_scr):
        @pl.when(pl.program_id(0) == 0)
        def _():
            for r in (loss_ref, dgate_ref, gfw_ref, gnw_ref, gwo_ref):
                r[...] = jnp.zeros_like(r)
        naz = naz_ref[...].astype(F32)
        ona = ona_ref[...]
        sg_naz = jax.nn.sigmoid(naz)
        sna = naz * sg_naz
        mix_scr[:, 0:NA_W] = (ona * sna).astype(BF16)
        hs = hf_ref[...] + hb_ref[...]
        z = z_ref[...].astype(F32)
        ov = o_ref[...].astype(F32)
        parts = _ml_norm_parts(hs, ov, z, nw_ref[...])
        sgz = [jax.nn.sigmoid(z[:, sl]) for sl, _, _, _ in parts]
        for (sl, xn, _, _), sg in zip(parts, sgz):
            mix_scr[:, NA_W + sl.start:NA_W + sl.stop] = (xn * nw_ref[:, sl] * (z[:, sl] * sg)).astype(BF16)
        mixb = mix_scr[...]
        wv = w_ref[...]
        yv = _nn(mixb, wv)
        gate_v = g_ref[...]
        hres = x_ref[...] + gate_v * yv
        r = lax.rsqrt(jnp.mean(hres * hres, axis=-1, keepdims=True) + EPS)
        xnf = hres * r
        err = xnf * fw_ref[...] - tg_ref[...]
        loss_ref[...] += 0.5 * jnp.sum(jnp.sum(err * err, axis=-1, keepdims=True) * (1.0 / d), axis=0, keepdims=True)
        dout = err * (1.0 / d)
        gfw_ref[...] += jnp.sum(dout * xnf, axis=0, keepdims=True)
        dxn = dout * fw_ref[...]
        dres = r * (dxn - xnf * jnp.mean(dxn * xnf, axis=-1, keepdims=True))
        dres_ref[...] = dres
        dgate_ref[...] += jnp.sum(dres * yv, axis=0, keepdims=True)
        dyb = (dres * gate_v).astype(BF16)
        gwo_ref[...] += _tn(mixb, dyb)
        dmix = _nt(dyb, wv)
        dna = dmix[:, 0:NA_W]
        dona_ref[...] = dna * sna
        dnaz_ref[...] = (dna * ona * (sg_naz * (1.0 + naz * (1.0 - sg_naz)))).astype(BF16)
        for (sl, xn, rs, so), sg in zip(parts, sgz):
            dyv = dmix[:, NA_W + sl.start:NA_W + sl.stop]
            zz = z[:, sl]
            sz = zz * sg
            w = nw_ref[:, sl]
            dz_ref[:, sl] = (dyv * xn * w * (sg * (1.0 + zz * (1.0 - sg)))).astype(BF16)
            gnw_ref[:, sl] += jnp.sum(dyv * xn * sz, axis=0, keepdims=True)
            dxm = dyv * w * sz
            dhm = rs * (dxm - jnp.mean(dxm, axis=-1, keepdims=True)
                        - xn * jnp.mean(dxm * xn, axis=-1, keepdims=True))
            dhs_ref[:, sl] = dhm * so
            do_ref[:, sl] = (dhm * hs[:, sl] * so * (1.0 - so)).astype(BF16)

    blk = lambda c: pl.BlockSpec((tm, 512), lambda i, c=c: (i, c))
    full = pl.BlockSpec((tm, d), lambda i: (i, 0))
    o512 = jax.ShapeDtypeStruct((t, 512), F32)
    b512 = jax.ShapeDtypeStruct((t, 512), BF16)
    whole = pl.BlockSpec((d, d), lambda i: (0, 0))
    return pl.pallas_call(
        body, name="tail", grid=(t // tm,),
        in_specs=[blk(0), blk(3), blk(0), blk(0), blk(7), blk(8), full, full, _row(d), _row(ML_W), _row(d), whole],
        out_specs=[pl.BlockSpec((1, 128), lambda i: (0, 0)), full] + [blk(0)] * 5
        + [_row(d), _row(d), _row(ML_W), whole],
        out_shape=[jax.ShapeDtypeStruct((1, 128), F32), jax.ShapeDtypeStruct((t, d), F32),
                   o512, b512, o512, b512, b512]
        + [jax.ShapeDtypeStruct((1, d), F32), jax.ShapeDtypeStruct((1, d), F32),
           jax.ShapeDtypeStruct((1, ML_W), F32), jax.ShapeDtypeStruct((d, d), F32)],
        scratch_shapes=[pltpu.VMEM((tm, d), BF16)],
        compiler_params=_cparams(("arbitrary",)),
    )(o_na, proj, h_f, h_b, proj, proj, x, target, gate, ml_norm_w, fnw, w_out_b)


def _in_bwd(pieces, x, dres, w_in_t, norm_w, scale, shift):
    t, d = x.shape
    tm = 512
    nt = t // tm
    widths = [p.shape[1] for p in pieces]
    offs = [sum(widths[:k]) for k in range(len(widths))]
    assert sum(widths) == IN_PAD
    npc = len(pieces)

    def body(*refs):
        p_refs = refs[:npc]
        (x_ref, dres_ref, w_hbm, nw_ref, sc_ref, sh_ref,
         gx_ref, gw_hbm, gb_ref, dsc_ref, dsh_ref, gnw_ref, w_vmem, acc, stage, sem) = refs[npc:]
        i = pl.program_id(0)

        @pl.when(i == 0)
        def _():
            cp = pltpu.make_async_copy(w_hbm, w_vmem, sem.at[0])
            cp.start()
            acc[...] = jnp.zeros_like(acc)
            gb_ref[...] = jnp.zeros_like(gb_ref)
            dsc_ref[...] = jnp.zeros_like(dsc_ref)
            dsh_ref[...] = jnp.zeros_like(dsh_ref)
            gnw_ref[...] = jnp.zeros_like(gnw_ref)
            cp.wait()

        nw = nw_ref[...]
        s1 = 1.0 + sc_ref[...]
        h, xn, r = _modulated_norm(x_ref[...], nw, sc_ref[...], sh_ref[...])
        hb = h.astype(BF16)
        dhv = jnp.zeros((tm, d), F32)
        for p_ref, c0, w in zip(p_refs, offs, widths):
            pt = p_ref[...]
            pb = pt.astype(BF16)
            dhv = dhv + _nn(pb, w_vmem[c0:c0 + w, :])
            acc[:, c0:c0 + w] += _tn(hb, pb)
            gb_ref[:, c0:c0 + w] += jnp.sum(pt.astype(F32), axis=0, keepdims=True)
        dsh_ref[...] += jnp.sum(dhv, axis=0, keepdims=True)
        dsc_ref[...] += jnp.sum(dhv * xn * nw, axis=0, keepdims=True)
        gnw_ref[...] += jnp.sum(dhv * xn * s1, axis=0, keepdims=True)
        dxn = dhv * nw * s1
        gx_ref[...] = dres_ref[...] + r * (dxn - xn * jnp.mean(dxn * xn, axis=-1, keepdims=True))

        @pl.when(i == nt - 1)
        def _():
            copies = []
            for blk in range(IN_PAD // 128):
                slot = blk % 2
                if blk >= 2:
                    copies[blk - 2].wait()
                stage[slot] = acc[:, blk * 128:(blk + 1) * 128].T
                cp = pltpu.make_async_copy(stage.at[slot], gw_hbm.at[pl.ds(blk * 128, 128), :], sem.at[1 + slot])
                cp.start()
                copies.append(cp)
            copies[-2].wait()
            copies[-1].wait()

    full = pl.BlockSpec((tm, d), lambda i: (i, 0))
    return pl.pallas_call(
        body, name="in_bwd", grid=(nt,),
        in_specs=[pl.BlockSpec((tm, w), lambda i: (i, 0)) for w in widths]
        + [full, full, pl.BlockSpec(memory_space=pl.ANY), _row(d), _row(d), _row(d)],
        out_specs=[full, pl.BlockSpec(memory_space=pl.ANY), _row(IN_PAD), _row(d), _row(d), _row(d)],
        out_shape=[jax.ShapeDtypeStruct((t, d), F32), jax.ShapeDtypeStruct((IN_PAD, d), F32),
                   jax.ShapeDtypeStruct((1, IN_PAD), F32)] + [jax.ShapeDtypeStruct((1, d), F32)] * 3,
        scratch_shapes=[pltpu.VMEM((IN_PAD, d), BF16), pltpu.VMEM((d, IN_PAD), F32),
                        pltpu.VMEM((2, 128, d), F32), pltpu.SemaphoreType.DMA((3,))],
        compiler_params=_cparams(("arbitrary",), IN_BWD_VMEM_LIMIT),
    )(*pieces, x, dres, w_in_t, norm_w, scale, shift)


def _na_static(rows):
    cases = [(0, 0), (NA_QROWS, NA_QROWS - 4), (rows - NA_QROWS, rows - NA_KROWS)]
    dy = np.zeros((3, NA_QROWS, NA_KROWS), np.int32)
    rv = np.zeros((3, NA_QROWS, NA_KROWS), bool)
    for cs, (r0, kr0) in enumerate(cases):
        for i in range(NA_QROWS):
            for j in range(NA_KROWS):
                r, kr = r0 + i, kr0 + j
                rs = min(max(r - NA_KH // 2, 0), rows - NA_KH)
                rv[cs, i, j] = rs <= kr <= rs + NA_KH - 1
                dy[cs, i, j] = min(max(kr - r + NA_KH - 1, 0), 2 * NA_KH - 2)
    cq = np.arange(GRID_W)[:, None]
    ck = np.arange(GRID_W)[None, :]
    cs0 = np.clip(cq - NA_KW // 2, 0, GRID_W - NA_KW)
    cv = (ck >= cs0) & (ck < cs0 + NA_KW)
    dx = np.clip(ck - cq, -(NA_KW - 1), NA_KW - 1) + NA_KW - 1
    return dy, rv, dx.astype(np.int32), cv


def _na_bias_table(rpb, rows):
    _, _, dx, cv = _na_static(rows)
    ndy = 2 * NA_KH - 1
    onehot = (dx.reshape(1, -1) == np.arange(2 * NA_KW - 1)[:, None]).astype(np.float32)
    rpx = jnp.dot(rpb.reshape(NA_HEADS * ndy, 2 * NA_KW - 1), jnp.asarray(onehot), precision=HI)
    rpx = jnp.where(cv[None, None], rpx.reshape(NA_HEADS, ndy, GRID_W, GRID_W), NEG)
    neg = jnp.full((NA_HEADS, 1, GRID_W, GRID_W), NEG, F32)
    rpx = jnp.concatenate([rpx, neg], axis=1)
    nxt = jnp.concatenate([rpx[:, 1:], neg], axis=1)
    negs = jnp.broadcast_to(neg, rpx.shape)
    pairs = jnp.concatenate([jnp.concatenate([rpx, nxt], axis=3), jnp.concatenate([rpx, negs], axis=3),
                             jnp.concatenate([negs, rpx], axis=3)], axis=1)
    npair = pairs.shape[1]

    def body(m_ref, o_ref):
        cs = pl.program_id(1)
        r0 = jnp.where(cs == 0, 0, jnp.where(cs == 1, NA_QROWS, rows - NA_QROWS))
        kr0 = jnp.where(cs == 0, 0, jnp.where(cs == 1, NA_QROWS - NA_KH // 2, rows - NA_KROWS))
        for i in range(NA_QROWS):
            r = r0 + i
            rs = jnp.clip(r - NA_KH // 2, 0, rows - NA_KH)
            for jp in range(NA_KROWS // 2):
                kl = kr0 + 2 * jp
                vl = (kl >= rs) & (kl <= rs + NA_KH - 1)
                vr = (kl + 1 >= rs) & (kl + 1 <= rs + NA_KH - 1)
                dyl = jnp.clip(kl - r + NA_KH - 1, 0, ndy)
                dyr = jnp.clip(kl + 1 - r + NA_KH - 1, 0, ndy)
                idx = jnp.where(vl & vr, dyl, jnp.where(vl, 16 + dyl, jnp.where(vr, 32 + dyr, 16 + ndy)))
                o_ref[0, 0, i * GRID_W:(i + 1) * GRID_W, jp * 128:(jp + 1) * 128] = m_ref[0, idx]

    return pl.pallas_call(
        body, name="na_bias_table", grid=(NA_HEADS, 3),
        in_specs=[pl.BlockSpec((1, npair, GRID_W, 128), lambda h, cs: (h, 0, 0, 0))],
        out_specs=pl.BlockSpec((1, 1, NA_QT, NA_KT), lambda h, cs: (h, cs, 0, 0)),
        out_shape=jax.ShapeDtypeStruct((NA_HEADS, 3, NA_QT, NA_KT), F32),
        compiler_params=_cparams(("parallel", "parallel")),
    )(pairs)


def _na_specs(t):
    nb = t // NA_QT
    nkb = t // NA_KCH
    npieces = NA_KT // NA_KCH

    def kb0(b):
        return jnp.clip(b * (NA_QT // NA_KCH) - 1, 0, nkb - npieces)

    def case(b):
        return jnp.where(b == 0, 0, jnp.where(b == nb - 1, 2, 1))

    q_spec = pl.BlockSpec((None, NA_QT, 128), lambda p, b: (p, b, 0))
    k_specs = [pl.BlockSpec((None, NA_KCH, 128), lambda p, b, i=i: (4 + p, kb0(b) + i, 0)) for i in range(npieces)]
    v_specs = [pl.BlockSpec((None, NA_KCH, 128), lambda p, b, i=i: (8 + p, kb0(b) + i, 0)) for i in range(npieces)]
    tbl_spec = pl.BlockSpec((2, 1, NA_QT, NA_KT), lambda p, b: (p, case(b), 0, 0))
    io_spec = pl.BlockSpec((NA_QT, 128), lambda p, b: (b, p))
    return nb, npieces, kb0, case, q_spec, k_specs, v_specs, tbl_spec, io_spec


NA_HALF = NA_QT // 2
NA_COMBOS_ALL = tuple((i, 0, NA_QT) for i in range(NA_KT // NA_KCH))
NA_COMBOS_INNER = ((0, 0, NA_HALF),) + tuple((i, 0, NA_QT) for i in range(1, NA_KT // NA_KCH - 1)) \
    + ((NA_KT // NA_KCH - 1, NA_HALF, NA_QT),)


def _na_place(val, r0, r1):
    if (r0, r1) == (0, NA_QT):
        return val
    z = jnp.zeros((NA_HALF, val.shape[1]), val.dtype)
    return jnp.concatenate([val, z] if r0 == 0 else [z, val], axis=0)


def _na_fwd(proj, tbl):
    t = proj.shape[1]
    nb, npieces, _, _, q_spec, k_specs, v_specs, tbl_spec, io_spec = _na_specs(t)
    lse_spec = pl.BlockSpec((1, NA_QT, 2), lambda p, b: (p, b, 0))

    def body(*refs):
        q_ref = refs[0]
        k_refs = refs[1:1 + npieces]
        v_refs = refs[1 + npieces:1 + 2 * npieces]
        tbl_ref, o_ref, lse_ref = refs[1 + 2 * npieces:]
        b = pl.program_id(1)

        def compute(combos):
            lane = lax.broadcasted_iota(jnp.int32, (1, 128), 1)
            qv = q_ref[...].astype(F32) * (NA_HEAD_DIM ** -0.5)
            ks = [r[...].astype(BF16) for r in k_refs]
            vs = [r[...].astype(BF16) for r in v_refs]
            hs = range(2)
            msk = [(lane // NA_HEAD_DIM) == hh for hh in hs]
            qh = [jnp.where(msk[hh], qv, 0.0).astype(BF16) for hh in hs]
            s = [[_nt(qh[hh][r0:r1], ks[i]) + tbl_ref[hh, 0, r0:r1, i * NA_KCH:(i + 1) * NA_KCH]
                  for i, r0, r1 in combos] for hh in hs]
            for h0 in (0, NA_HALF):
                rows = slice(h0, h0 + NA_HALF)
                cover = [(c, i, h0 - r0) for c, (i, r0, r1) in enumerate(combos) if r0 <= h0 < r1]
                part = [[s[hh][c][off:off + NA_HALF] for c, _, off in cover] for hh in hs]
                m = [functools.reduce(jnp.maximum, [jnp.max(v, axis=1, keepdims=True) for v in part[hh]]) for hh in hs]
                p = [[jnp.exp(v - m[hh]) for v in part[hh]] for hh in hs]
                l = [functools.reduce(jnp.add, [jnp.sum(v, axis=1, keepdims=True) for v in p[hh]]) for hh in hs]
                o = [functools.reduce(jnp.add, [_nn(p[hh][k].astype(BF16), vs[i]) for k, (_, i, _) in enumerate(cover)])
                     for hh in hs]
                for hh in hs:
                    lse_ref[0, rows, hh:hh + 1] = m[hh] + jnp.log(l[hh])
                o_ref[rows, :] = jnp.where(msk[0], o[0] / l[0], o[1] / l[1])

        inner = (b > 0) & (b < nb - 1)
        pl.when(inner)(lambda: compute(NA_COMBOS_INNER))
        pl.when(jnp.logical_not(inner))(lambda: compute(NA_COMBOS_ALL))

    return pl.pallas_call(
        body, name="na_fwd", grid=(4, nb),
        in_specs=[q_spec] + k_specs + v_specs + [tbl_spec],
        out_specs=[io_spec, lse_spec],
        out_shape=[jax.ShapeDtypeStruct((t, NA_W), F32), jax.ShapeDtypeStruct((4, t, 2), F32)],
        compiler_params=_cparams(("parallel", "arbitrary")),
    )(*([proj] * (1 + 2 * npieces)), tbl)


def _na_bwd(proj, tbl, d_o, o_na, lse):
    t = proj.shape[1]
    nb, npieces, kb0, case, q_spec, k_specs, v_specs, tbl_spec, io_spec = _na_specs(t)

    def body(*refs):
        q_ref = refs[0]
        k_refs = refs[1:1 + npieces]
        v_refs = refs[1 + npieces:1 + 2 * npieces]
        (tbl_ref, do_ref, o_ref, lse_ref, dq_ref, dk_hbm, dv_hbm, rpb_ref,
         dk_acc, dv_acc, dk_out, dv_out, s_scr, dp_scr, dsb_scr, pnb_scr, sem) = refs[1 + 2 * npieces:]
        p_id = pl.program_id(0)
        b = pl.program_id(1)

        @pl.when(b == 0)
        def _():
            dk_acc[...] = jnp.zeros_like(dk_acc)
            dv_acc[...] = jnp.zeros_like(dv_acc)

        @pl.when((b == 0) | (b == 1) | (b == nb - 1))
        def _():
            rpb_ref[...] = jnp.zeros_like(rpb_ref)

        def compute(combos):
            lane = lax.broadcasted_iota(jnp.int32, (1, 128), 1)
            scale = NA_HEAD_DIM ** -0.5
            qv = q_ref[...].astype(F32) * scale
            ks = [r[...].astype(BF16) for r in k_refs]
            vs = [r[...].astype(BF16) for r in v_refs]
            dov = do_ref[...]
            ov = o_ref[...]
            tok0 = kb0(b) * NA_KCH
            hs = range(2)
            msk = [(lane // NA_HEAD_DIM) == hh for hh in hs]
            qh = [jnp.where(msk[hh], qv, 0.0).astype(BF16) for hh in hs]
            doh = [jnp.where(msk[hh], dov, 0.0) for hh in hs]
            dohb = [doh[hh].astype(BF16) for hh in hs]
            dd = [jnp.sum(doh[hh] * ov, axis=1, keepdims=True) for hh in hs]
            for hh in hs:
                for c, (i, q0, q1) in enumerate(combos):
                    slot = (hh * len(combos) + c) % 2
                    cols = slice(i * NA_KCH, (i + 1) * NA_KCH)
                    s_scr[slot, 0:q1 - q0] = _nt(qh[hh][q0:q1], ks[i])
                    dp_scr[slot, 0:q1 - q0] = _nt(dohb[hh][q0:q1], vs[i])
                    for r0 in range(q0, q1, NA_RC):
                        rows = slice(r0, r0 + NA_RC)
                        loc = slice(r0 - q0, r0 - q0 + NA_RC)
                        p = jnp.exp(s_scr[slot, loc, :] + tbl_ref[hh, 0, rows, cols] - lse_ref[0, rows, hh:hh + 1])
                        d = p * (dp_scr[slot, loc, :] - dd[hh][rows])
                        pnb_scr[hh, rows, cols] = p.astype(BF16)
                        dsb_scr[hh, rows, cols] = d.astype(BF16)
                done = {(i, q0) for i, q0, _ in combos} | {(i, NA_HALF) for i, q0, q1 in combos if q1 - q0 == NA_QT}
                for i in range(npieces):
                    for q0 in (0, NA_HALF):
                        if (i, q0) not in done:
                            dsb_scr[hh, q0:q0 + NA_HALF, i * NA_KCH:(i + 1) * NA_KCH] = jnp.zeros(
                                (NA_HALF, NA_KCH), BF16)
            dqh = [functools.reduce(jnp.add, [_na_place(_nn(dsb_scr[hh, q0:q1, i * NA_KCH:(i + 1) * NA_KCH], ks[i]),
                                                        q0, q1) for i, q0, q1 in combos]) for hh in hs]
            dq_ref[...] = (jnp.where(msk[0], dqh[0], dqh[1]) * scale).astype(BF16)
            for i in range(npieces):
                rows = pl.ds(pl.multiple_of(tok0 + i * NA_KCH, NA_KCH), NA_KCH)
                cols = slice(i * NA_KCH, (i + 1) * NA_KCH)
                q0, q1 = [(a, e) for j, a, e in combos if j == i][0]
                dk_acc[rows, :] += (_tn(dsb_scr[0, q0:q1, cols], qh[0][q0:q1])
                                    + _tn(dsb_scr[1, q0:q1, cols], qh[1][q0:q1]))
                dv_acc[rows, :] += (_tn(pnb_scr[0, q0:q1, cols], dohb[0][q0:q1])
                                    + _tn(pnb_scr[1, q0:q1, cols], dohb[1][q0:q1]))
            for hh in hs:
                acc = dsb_scr[hh, 0:GRID_W, :].astype(F32)
                for i in range(1, NA_QROWS):
                    acc = acc + pltpu.roll(dsb_scr[hh, i * GRID_W:(i + 1) * GRID_W, :].astype(F32),
                                           NA_KT - i * GRID_W, 1)
                rpb_ref[0, 0, hh] += acc

        inner = (b > 0) & (b < nb - 1)
        pl.when(inner)(lambda: compute(NA_COMBOS_INNER))
        pl.when(jnp.logical_not(inner))(lambda: compute(NA_COMBOS_ALL))

        @pl.when(b == nb - 1)
        def _():
            cols = pl.ds(pl.multiple_of(p_id * 128, 128), 128)
            dk_out[...] = dk_acc[...].astype(BF16)
            dv_out[...] = dv_acc[...].astype(BF16)
            ck = pltpu.make_async_copy(dk_out, dk_hbm.at[:, cols], sem.at[0])
            cv = pltpu.make_async_copy(dv_out, dv_hbm.at[:, cols], sem.at[1])
            ck.start()
            cv.start()
            ck.wait()
            cv.wait()

    o512 = jax.ShapeDtypeStruct((t, NA_W), BF16)
    return pl.pallas_call(
        body, name="na_bwd", grid=(4, nb),
        in_specs=[q_spec] + k_specs + v_specs + [tbl_spec, io_spec, io_spec,
                                                 pl.BlockSpec((1, NA_QT, 2), lambda p, b: (p, b, 0))],
        out_specs=[io_spec, pl.BlockSpec(memory_space=pl.ANY), pl.BlockSpec(memory_space=pl.ANY),
                   pl.BlockSpec((1, 1, 2, GRID_W, NA_KT), lambda p, b: (p, case(b), 0, 0, 0))],
        out_shape=[o512, o512, o512, jax.ShapeDtypeStruct((4, 3, 2, GRID_W, NA_KT), F32)],
        scratch_shapes=[pltpu.VMEM((t, 128), F32), pltpu.VMEM((t, 128), F32),
                        pltpu.VMEM((t, 128), BF16), pltpu.VMEM((t, 128), BF16),
                        pltpu.VMEM((2, NA_QT, NA_KCH), F32), pltpu.VMEM((2, NA_QT, NA_KCH), F32),
                        pltpu.VMEM((2, NA_QT, NA_KT), BF16), pltpu.VMEM((2, NA_QT, NA_KT), BF16),
                        pltpu.SemaphoreType.DMA((2,))],
        compiler_params=_cparams(("arbitrary", "arbitrary")),
    )(*([proj] * (1 + 2 * npieces)), tbl, d_o, o_na, lse)


def _rpb_reduce(rpbacc, rows):
    nacc = 4 * 3 * 2

    def shift_body(a_ref, o_ref):
        acc = a_ref[0, 0:1, :]
        for cq in range(1, GRID_W):
            acc = acc + pltpu.roll(a_ref[0, cq:cq + 1, :], NA_KT - cq, 1)
        o_ref[0] = jnp.broadcast_to(acc, (8, NA_KT))

    vec = pl.pallas_call(
        shift_body, name="rpb_shift", grid=(nacc,),
        in_specs=[pl.BlockSpec((1, GRID_W, NA_KT), lambda a: (a, 0, 0))],
        out_specs=pl.BlockSpec((1, 8, NA_KT), lambda a: (a, 0, 0)),
        out_shape=jax.ShapeDtypeStruct((nacc, 8, NA_KT), F32),
        compiler_params=_cparams(("parallel",)),
    )(rpbacc.reshape(nacc, GRID_W, NA_KT))
    a = vec[:, 0].reshape(4, 3, 2, NA_KT).transpose(0, 2, 1, 3).reshape(NA_HEADS, 3, NA_KT)
    if rows // NA_QROWS < 3:
        a = a.at[:, 1].set(0.0)
    dd = np.arange(NA_KROWS)[:, None]
    dxo = np.arange(-(NA_KW - 1), NA_KW)[None, :]
    idx = ((dd * GRID_W + dxo) % NA_KT).reshape(-1)
    g = a[..., idx].reshape(NA_HEADS, 3 * NA_KROWS, 2 * NA_KW - 1)
    g = jnp.pad(g, ((0, 0), (0, 0), (0, 128 - (2 * NA_KW - 1))))
    nmat = np.zeros((16, 3 * NA_KROWS), np.float32)
    for cs, delta in enumerate((0, -(NA_KH // 2), -(NA_KROWS - NA_QROWS))):
        for d in range(NA_KROWS):
            jmi = d - NA_KROWS if (cs == 0 and d > NA_KH - 1) else d
            dy = jmi + delta + NA_KH - 1
            if 0 <= dy <= 2 * NA_KH - 2:
                nmat[dy, cs * NA_KROWS + d] = 1.0

    def body(n_ref, g_ref, o_ref):
        o_ref[0] = jnp.dot(n_ref[...], g_ref[0], precision=HI, preferred_element_type=F32)

    out = pl.pallas_call(
        body, name="rpb_reduce", grid=(NA_HEADS,),
        in_specs=[pl.BlockSpec((16, nmat.shape[1]), lambda h: (0, 0)),
                  pl.BlockSpec((1, nmat.shape[1], 128), lambda h: (h, 0, 0))],
        out_specs=pl.BlockSpec((1, 16, 128), lambda h: (h, 0, 0)),
        out_shape=jax.ShapeDtypeStruct((NA_HEADS, 16, 128), F32),
        compiler_params=_cparams(("parallel",)),
    )(jnp.asarray(nmat), g)
    return out[:, :2 * NA_KH - 1, :2 * NA_KW - 1]


def _halo_specs(tm, t, col, width=1024):
    nth = t // CONV_HALO
    per = tm // CONV_HALO
    return [pl.BlockSpec((tm, width), lambda i: (i, col)),
            pl.BlockSpec((CONV_HALO, width), lambda i: (jnp.maximum(i * per - 1, 0), col)),
            pl.BlockSpec((CONV_HALO, width), lambda i: (jnp.minimum((i + 1) * per, nth - 1), col))]


def _fill_ext(ext, cur_ref, prev_ref, next_ref, tm, nt):
    i = pl.program_id(0)
    hl = CONV_HALO
    ext[0:hl, :] = jnp.where(i == 0, 0.0, prev_ref[...].astype(F32))
    ext[hl:hl + tm, :] = cur_ref[...].astype(F32)
    ext[hl + tm:2 * hl + tm, :] = jnp.where(i == nt - 1, 0.0, next_ref[...].astype(F32))


CONV_HALO = 16
CONV_RC = 16
CONV_CB = 512


def _conv_chunks(tm):
    return [(slice(cb, cb + CONV_CB), slice(rb, rb + CONV_RC))
            for cb in range(0, 1024, CONV_CB) for rb in range(0, tm, CONV_RC)]


def _conv_fwd(proj, conv_w8, conv_b, tm):
    t = proj.shape[0]
    nt = t // tm

    def body(u_ref, up_ref, un_ref, w_ref, b_ref, pre_ref, act_ref, ext):
        _fill_ext(ext, u_ref, up_ref, un_ref, tm, nt)
        for cs, rs in _conv_chunks(tm):
            pre = b_ref[:, cs] + w_ref[0:1, cs] * ext[pl.ds(rs.start + CONV_HALO - 2, CONV_RC), cs]
            for j in range(1, CONV_W):
                pre = pre + w_ref[j:j + 1, cs] * ext[pl.ds(rs.start + CONV_HALO - 2 + j, CONV_RC), cs]
            pre_ref[rs, cs] = pre
            act_ref[rs, cs] = _silu(pre)

    full = pl.BlockSpec((tm, 1024), lambda i: (i, 0))
    o = jax.ShapeDtypeStruct((t, 1024), F32)
    return pl.pallas_call(
        body, name="conv_fwd", grid=(nt,),
        in_specs=_halo_specs(tm, t, 2) + [pl.BlockSpec((8, 1024), lambda i: (0, 0)), _row(1024)],
        out_specs=[full, full], out_shape=[o, o],
        scratch_shapes=[pltpu.VMEM((tm + 2 * CONV_HALO, 1024), F32)],
        compiler_params=_cparams(("parallel",)),
    )(proj, proj, proj, conv_w8, conv_b)


def _conv_bwd(dq, dk, pre, proj, conv_w8, tm):
    t = pre.shape[0]
    nt = t // tm

    def body(dq_ref, dqp_ref, dqn_ref, dk_ref, dkp_ref, dkn_ref, pre_ref, prep_ref, pren_ref,
             u_ref, up_ref, un_ref, w_ref, du_ref, gw_ref, gb_ref, extd, extu):
        i = pl.program_id(0)
        hl = CONV_HALO

        @pl.when(i == 0)
        def _():
            gw_ref[...] = jnp.zeros_like(gw_ref)
            gb_ref[...] = jnp.zeros_like(gb_ref)
        for rows, dqr, dkr, prr, edge in ((slice(0, hl), dqp_ref, dkp_ref, prep_ref, i == 0),
                                          (slice(hl, hl + tm), dq_ref, dk_ref, pre_ref, None),
                                          (slice(hl + tm, 2 * hl + tm), dqn_ref, dkn_ref, pren_ref, i == nt - 1)):
            ds = _dsilu(prr[...])
            dl = dqr[...] * ds[:, 0:ML_W]
            dr = dkr[...] * ds[:, ML_W:]
            if edge is not None:
                dl = jnp.where(edge, 0.0, dl)
                dr = jnp.where(edge, 0.0, dr)
            extd[rows, 0:ML_W] = dl
            extd[rows, ML_W:] = dr
        _fill_ext(extu, u_ref, up_ref, un_ref, tm, nt)
        gb_ref[...] += jnp.sum(extd[hl:hl + tm, :], axis=0, keepdims=True)
        gacc = None
        for cs, rs in _conv_chunks(tm):
            if rs.start == 0:
                gacc = [jnp.zeros((8, CONV_CB), F32) for _ in range(CONV_W)]
            du = w_ref[0:1, cs] * extd[pl.ds(rs.start + hl + 2, CONV_RC), cs]
            for j in range(1, CONV_W):
                du = du + w_ref[j:j + 1, cs] * extd[pl.ds(rs.start + hl + 2 - j, CONV_RC), cs]
            du_ref[rs, cs] = du.astype(BF16)
            dcur = extd[pl.ds(rs.start + hl, CONV_RC), cs]
            for j in range(CONV_W):
                prod = dcur * extu[pl.ds(rs.start + hl - 2 + j, CONV_RC), cs]
                gacc[j] = gacc[j] + functools.reduce(
                    jnp.add, [prod[k:k + 8] for k in range(0, CONV_RC, 8)])
            if rs.stop == tm:
                for j in range(CONV_W):
                    gw_ref[j:j + 1, cs] += jnp.sum(gacc[j], axis=0, keepdims=True)

    full = pl.BlockSpec((tm, 1024), lambda i: (i, 0))
    return pl.pallas_call(
        body, name="conv_bwd", grid=(nt,),
        in_specs=_halo_specs(tm, t, 0, ML_W) + _halo_specs(tm, t, 0, ML_W) + _halo_specs(tm, t, 0)
        + _halo_specs(tm, t, 2) + [pl.BlockSpec((8, 1024), lambda i: (0, 0))],
        out_specs=[full, pl.BlockSpec((8, 1024), lambda i: (0, 0)), _row(1024)],
        out_shape=[jax.ShapeDtypeStruct((t, 1024), BF16), jax.ShapeDtypeStruct((8, 1024), F32),
                   jax.ShapeDtypeStruct((1, 1024), F32)],
        scratch_shapes=[pltpu.VMEM((tm + 2 * CONV_HALO, 1024), F32), pltpu.VMEM((tm + 2 * CONV_HALO, 1024), F32)],
        compiler_params=_cparams(("arbitrary",)),
    )(dq, dq, dq, dk, dk, dk, pre, pre, pre, proj, proj, proj, conv_w8)


def _ml_consts(rev):
    iu = lax.broadcasted_iota(jnp.int32, (ML_CHUNK, ML_CHUNK), 0)
    js = lax.broadcasted_iota(jnp.int32, (ML_CHUNK, ML_CHUNK), 1)
    eye = iu == js
    le = iu <= js
    ge = iu >= js
    csum, csum_t, sees = (ge, le, ge) if rev else (le, ge, le)
    return eye, csum.astype(F32), csum_t.astype(F32), sees


def _col(row, eye):
    return jnp.sum(jnp.where(eye, row, 0.0), axis=1, keepdims=True)


def _rowof(col, eye):
    return jnp.sum(jnp.where(eye, col, 0.0), axis=0, keepdims=True)


def _row8(row):
    top = lax.broadcasted_iota(jnp.int32, (8, row.shape[1]), 0) == 0
    return jnp.where(top, row, jnp.zeros_like(row))


def _outer_rows(a_row, b_row_bf16):
    hi = a_row.astype(BF16)
    lo = (a_row - hi.astype(F32)).astype(BF16)
    r_a = lax.broadcasted_iota(jnp.int32, (8, a_row.shape[1]), 0)
    r_b = lax.broadcasted_iota(jnp.int32, (8, b_row_bf16.shape[1]), 0)
    lhs = jnp.where(r_a == 0, hi, jnp.where(r_a == 1, lo, jnp.zeros_like(hi)))
    rhs = jnp.where(r_b < 2, b_row_bf16, jnp.zeros_like(b_row_bf16))
    return _tn(lhs, rhs)


def _ml_gates(gi, gf, m0, csum, rev):
    lf = jax.nn.log_sigmoid(gf)
    b_rows = jnp.dot(lf, csum, precision=HI, preferred_element_type=F32)
    bl = jnp.sum(lf, axis=1, keepdims=True)
    a_rows = bl - b_rows + gi
    mloc = jnp.max(a_rows, axis=1, keepdims=True)
    order = list(range(ML_NB))[::-1] if rev else list(range(ML_NB))
    mp, mn, decay = {}, {}, {}
    m = m0
    for n in order:
        mp[n] = m
        m = jnp.maximum(bl[n:n + 1] + m, mloc[n:n + 1])
        mn[n] = m
    for n in order:
        decay[n] = jnp.exp(bl[n:n + 1] + mp[n] - mn[n])
    return b_rows, a_rows, gi - b_rows, mp, mn, decay, order


def _ml_load(q_ref, k_ref, v_ref, n):
    sl = slice(n * ML_CHUNK, (n + 1) * ML_CHUNK)
    qb = q_ref[sl, :].astype(BF16)
    kb = (k_ref[sl, :] * (ML_HEAD_DIM ** -0.5)).astype(BF16)
    vn = v_ref[sl, :].astype(F32)
    return sl, qb, kb, vn


def _ml_state_scan(q_ref, k_ref, v_ref, a_rows, mn, decay, order, c0, n0):
    ns = range(ML_NB)
    ld = [_ml_load(q_ref, k_ref, v_ref, n) for n in ns]
    vt = [ld[n][3].T for n in ns]
    w_row = [jnp.exp(a_rows[n:n + 1] - mn[n]) for n in ns]
    u = [_nn((vt[n] * w_row[n]).astype(BF16), ld[n][2]) for n in ns]
    nu = [_nn(_row8(w_row[n]).astype(BF16), ld[n][2])[0:1] for n in ns]
    cp, npv = {}, {}
    c, nv = c0, n0
    for n in order:
        cp[n], npv[n] = c, nv
        c = decay[n] * c + u[n]
        nv = decay[n] * nv + nu[n]
    return ld, vt, cp, npv, w_row, c, nv


def _ml_intra_all(ld, vt, b_rows, imb_rows, mp, cp, npv, sees, eye):
    ns = range(ML_NB)
    qk = [_nt(ld[n][2], ld[n][1]) for n in ns]
    cq = [_nt(cp[n].astype(BF16), ld[n][1]) for n in ns]
    qn = [_nt(_row8(npv[n]).astype(BF16), ld[n][1])[0:1] for n in ns]
    imb_col = [_col(imb_rows[n:n + 1], eye) for n in ns]
    dlog = [jnp.where(sees, b_rows[n:n + 1] + imb_col[n], NEG) for n in ns]
    m_inter = [b_rows[n:n + 1] + mp[n] for n in ns]
    m_t = [jnp.maximum(m_inter[n], jnp.max(dlog[n], axis=0, keepdims=True)) for n in ns]
    pm = [jnp.exp(dlog[n] - m_t[n]) for n in ns]
    inter = [jnp.exp(m_inter[n] - m_t[n]) for n in ns]
    floor = [jnp.exp(-m_t[n]) for n in ns]
    s = [qk[n] * pm[n] for n in ns]
    sv = [_nn(vt[n].astype(BF16), s[n].astype(BF16)) for n in ns]
    den = [jnp.sum(s[n], axis=0, keepdims=True) + inter[n] * qn[n] for n in ns]
    num = [sv[n] + inter[n] * cq[n] for n in ns]
    dn = [jnp.maximum(jnp.abs(den[n]), floor[n]) for n in ns]
    return [dict(pm=pm[n], s=s[n], inter=inter[n], cq=cq[n], qn=qn[n], num=num[n], den=den[n],
                 floor=floor[n], dn=dn[n]) for n in ns]


def _ml_specs(t, rev):
    nblk = t // ML_TB
    blk = (lambda g: nblk - 1 - g) if rev else (lambda g: g)
    hps = ML_HPS
    tile = lambda c0: pl.BlockSpec((ML_TB, 128 * hps), lambda hg, g, c0=c0: (blk(g), c0 // hps + hg))
    gate = pl.BlockSpec((hps, ML_NB, ML_CHUNK), lambda hg, g: (hg, blk(g), 0))
    cchk = pl.BlockSpec((hps, 1, 128, 128), lambda hg, g: (hg, blk(g), 0, 0))
    nmchk = pl.BlockSpec((hps, 1, 8, 128), lambda hg, g: (hg, blk(g), 0, 0))
    return nblk, blk, tile, gate, cchk, nmchk


def _ml_head_views(refs, hh):
    cols = slice(hh * ML_HEAD_DIM, (hh + 1) * ML_HEAD_DIM)
    return [r.at[:, cols] if len(r.shape) == 2 else r.at[hh] for r in refs]


def _ml_fwd(qk_act, proj, gi, gf, rev, name):
    t = qk_act.shape[0]
    nblk, _, tile, gate, cchk, nmchk = _ml_specs(t, rev)

    def body(*refs):
        for hh in range(ML_HPS):
            one_head(*_ml_head_views(refs, hh))

    def one_head(q_ref, k_ref, v_ref, gi_ref, gf_ref, h_ref, cchk_ref, nmchk_ref, c_ref, nm_ref):
        @pl.when(pl.program_id(1) == 0)
        def _():
            c_ref[...] = jnp.zeros_like(c_ref)
            nm_ref[...] = jnp.zeros_like(nm_ref)
        cchk_ref[0] = c_ref[...]
        nmchk_ref[0] = nm_ref[...]
        eye, csum, _, sees = _ml_consts(rev)
        b_rows, a_rows, imb_rows, mp, mn, decay, order = _ml_gates(
            gi_ref[...], gf_ref[...], nm_ref[1:2, 0:1], csum, rev)
        ld, vt, cp, npv, _, c, nv = _ml_state_scan(q_ref, k_ref, v_ref, a_rows, mn, decay, order,
                                                   c_ref[...], nm_ref[0:1, :])
        c_ref[...] = c
        nm_ref[0:1, :] = nv
        nm_ref[1:2, :] = jnp.broadcast_to(mn[order[-1]], (1, 128))
        rs = _ml_intra_all(ld, vt, b_rows, imb_rows, mp, cp, npv, sees, eye)
        ht = [rs[n]['num'] / rs[n]['dn'] for n in range(ML_NB)]
        for n in range(ML_NB):
            h_ref[n * ML_CHUNK:(n + 1) * ML_CHUNK, :] = ht[n].T

    return pl.pallas_call(
        body, name=name, grid=(ML_HEADS // ML_HPS, nblk),
        in_specs=[tile(0), tile(4), tile(24), gate, gate],
        out_specs=[tile(0), cchk, nmchk],
        out_shape=[jax.ShapeDtypeStruct((t, ML_W), F32),
                   jax.ShapeDtypeStruct((ML_HEADS, nblk, 128, 128), F32),
                   jax.ShapeDtypeStruct((ML_HEADS, nblk, 8, 128), F32)],
        scratch_shapes=[pltpu.VMEM((ML_HPS, 128, 128), F32), pltpu.VMEM((ML_HPS, 8, 128), F32)],
        compiler_params=_cparams(("parallel", "arbitrary")),
    )(qk_act, qk_act, proj, gi, gf)


def _ml_bwd(qk_act, proj, gi, gf, dh, cchk_a, nmchk_a, prev, rev, name):
    t = qk_act.shape[0]
    nblk, _, tile, gate, cchk, nmchk = _ml_specs(t, not rev)

    def body(*refs):
        for hh in range(ML_HPS):
            one_head(*_ml_head_views(refs, hh))

    def one_head(q_ref, k_ref, v_ref, gi_ref, gf_ref, dh_ref, cchk_ref, nmchk_ref, *rest):
        prev_refs = rest[:len(prev)]
        dq_ref, dk_ref, dv_ref, dgi_ref, dgf_ref, dc_ref, dn_ref, db_scr, dbl_scr, di_scr = rest[len(prev):]

        def plus_prev(val, which, rows):
            return val + prev_refs[which][rows, :] if prev else val

        @pl.when(pl.program_id(1) == 0)
        def _():
            dc_ref[...] = jnp.zeros_like(dc_ref)
            dn_ref[...] = jnp.zeros_like(dn_ref)
        eye, csum, csum_t, sees = _ml_consts(rev)
        gfv = gf_ref[...]
        b_rows, a_rows, imb_rows, mp, mn, decay, order = _ml_gates(
            gi_ref[...], gfv, nmchk_ref[0, 1:2, 0:1], csum, rev)
        ld, vt, cp, npv, w_row, _, _ = _ml_state_scan(q_ref, k_ref, v_ref, a_rows, mn, decay, order,
                                                      cchk_ref[0], nmchk_ref[0, 0:1, :])
        ns = range(ML_NB)
        rs = _ml_intra_all(ld, vt, b_rows, imb_rows, mp, cp, npv, sees, eye)
        sls = [ld[n][0] for n in ns]
        qbs = [ld[n][1] for n in ns]
        kbs = [ld[n][2] for n in ns]
        vbs = [ld[n][3].astype(BF16) for n in ns]
        rdn = [1.0 / rs[n]['dn'] for n in ns]
        dnum = [dh_ref[sls[n], :].T * rdn[n] for n in ns]
        hsum = [jnp.sum(dnum[n] * rs[n]['num'], axis=0, keepdims=True) for n in ns]
        dden = [jnp.where(jnp.abs(rs[n]['den']) > rs[n]['floor'],
                          -hsum[n] * rdn[n] * jnp.sign(rs[n]['den']), 0.0) for n in ns]
        dnb = [dnum[n].astype(BF16) for n in ns]
        dsf = [_nn(vbs[n], dnb[n]) + dden[n] for n in ns]
        dv0 = [_nt(rs[n]['s'].astype(BF16), dnb[n]) for n in ns]
        gb = [(dsf[n] * rs[n]['pm']).astype(BF16) for n in ns]
        cpb = [cp[n].astype(BF16) for n in ns]
        idd = [rs[n]['inter'] * dden[n] for n in ns]
        idn = [(rs[n]['inter'] * dnum[n]).astype(BF16) for n in ns]
        dqa = [_tn(gb[n], kbs[n]) for n in ns]
        dqc = [_tn(idn[n], cpb[n]) for n in ns]
        dqn = [_outer_rows(idd[n], npv[n].astype(BF16)) for n in ns]
        dk0 = [_nn(gb[n], qbs[n]) for n in ns]
        xs = [_nn(idn[n], qbs[n]) for n in ns]
        for n in ns:
            dq_ref[sls[n], :] = plus_prev(dqa[n] + dqc[n] + dqn[n], 0, sls[n])
        rr = [dsf[n] * rs[n]['s'] for n in ns]
        dinter = [jnp.sum(dnum[n] * rs[n]['cq'], axis=0, keepdims=True) + dden[n] * rs[n]['qn'] for n in ns]
        dbt = [jnp.sum(rr[n], axis=0, keepdims=True) + dinter[n] * rs[n]['inter'] for n in ns]
        dimb = [jnp.sum(rr[n], axis=1, keepdims=True) for n in ns]
        xns = [_nn(_row8(idd[n]).astype(BF16), qbs[n])[0:1] for n in ns]
        dcn, dnn = {}, {}
        dc, dn = dc_ref[...], dn_ref[0:1, :]
        for n in order[::-1]:
            dcn[n], dnn[n] = dc, dn
            dc = decay[n] * dc + xs[n]
            dn = decay[n] * dn + xns[n]
        dc_ref[...] = dc
        dn_ref[0:1, :] = dn
        kscale = ML_HEAD_DIM ** -0.5
        dcb = [dcn[n].astype(BF16) for n in ns]
        z = [_nn(vbs[n], dcb[n]) for n in ns]
        kd = [_nt(kbs[n], dcb[n]) for n in ns]
        ddecay = [jnp.sum(jnp.sum(dcn[n] * cp[n], axis=1, keepdims=True), axis=0, keepdims=True)
                  + jnp.sum(dnn[n] * npv[n], axis=1, keepdims=True) for n in ns]
        zd = [z[n] + dnn[n] for n in ns]
        dw = [jnp.sum(zd[n] * kbs[n].astype(F32), axis=1, keepdims=True) for n in ns]
        wcol = [_col(w_row[n], eye) for n in ns]
        for n in ns:
            dv_ref[sls[n], :] = plus_prev(dv0[n] + wcol[n] * kd[n], 2, sls[n]).astype(dv_ref.dtype)
            dk_ref[sls[n], :] = plus_prev((dk0[n] + wcol[n] * zd[n]) * kscale, 1, sls[n])
        da = [dw[n] * wcol[n] for n in ns]
        dbl = [jnp.sum(da[n], axis=0, keepdims=True) + ddecay[n] * decay[n] for n in ns]
        key_row = [_rowof(dimb[n] + da[n], eye) for n in ns]
        for n in ns:
            db_scr[n:n + 1, :] = dbt[n] - key_row[n]
            di_scr[n:n + 1, :] = key_row[n]
            dbl_scr[n:n + 1, :] = jnp.broadcast_to(dbl[n], (1, ML_CHUNK))
        dlf = jnp.dot(db_scr[...], csum_t, precision=HI, preferred_element_type=F32) + dbl_scr[...]
        dgf_ref[...] = dlf * jax.nn.sigmoid(-gfv)
        dgi_ref[...] = di_scr[...]

    nc = t // ML_CHUNK
    o512 = jax.ShapeDtypeStruct((t, ML_W), F32)
    og = jax.ShapeDtypeStruct((ML_HEADS, nc, ML_CHUNK), F32)
    return pl.pallas_call(
        body, name=name, grid=(ML_HEADS // ML_HPS, nblk),
        in_specs=[tile(0), tile(4), tile(24), gate, gate, tile(0), cchk, nmchk] + [tile(0)] * len(prev),
        out_specs=[tile(0), tile(0), tile(0), gate, gate],
        out_shape=[o512, o512, jax.ShapeDtypeStruct((t, ML_W), BF16 if prev else F32), og, og],
        scratch_shapes=[pltpu.VMEM((ML_HPS, 128, 128), F32), pltpu.VMEM((ML_HPS, 8, 128), F32)]
        + [pltpu.VMEM((ML_HPS, ML_NB, ML_CHUNK), F32)] * 3,
        compiler_params=_cparams(("parallel", "arbitrary")),
    )(qk_act, qk_act, proj, gi, gf, dh, cchk_a, nmchk_a, *prev)


def _gate_rows(gates16, t):
    g = gates16.reshape(t // ML_CHUNK, ML_CHUNK, 4, ML_HEADS).transpose(2, 3, 0, 1)
    return g[0], g[1], g[2], g[3]


def _gate_cols(dgi_f, dgf_f, dgi_b, dgf_b, t):
    g = jnp.stack([dgi_f, dgf_f, dgi_b, dgf_b]).transpose(2, 3, 0, 1).reshape(t, 4 * ML_HEADS)
    return jnp.pad(g, ((0, 0), (0, 128 - 4 * ML_HEADS)))


def _local_step(x, target, shift, scale, gate, norm_w, w_in_t, b_in_p, conv_w8, conv_b, rpb,
                ml_norm_w, w_out_b, final_norm_w):
    t = x.shape[0]
    rows = t // GRID_W
    tm = 512
    proj, gates, qkv_na = _in_proj(x, norm_w, scale, shift, w_in_t, b_in_p)
    tbl = _na_bias_table(rpb, rows)
    o_na, lse_na = _na_fwd(qkv_na, tbl)
    pre, qk_act = _conv_fwd(proj, conv_w8, conv_b, 2 * tm)
    gi_f, gf_f, gi_b, gf_b = _gate_rows(gates[:, :4 * ML_HEADS], t)
    h_f, cchk_f, nmchk_f = _ml_fwd(qk_act, proj, gi_f, gf_f, False, "ml_fwd_f")
    h_b, cchk_b, nmchk_b = _ml_fwd(qk_act, proj, gi_b, gf_b, True, "ml_fwd_b")
    (loss, dres, d_ona, d_naz, dhs, d_o, d_z, dgate, g_fnw, g_mlnw, g_w_out) = _tail(
        o_na, proj, h_f, h_b, x, target, gate, ml_norm_w, final_norm_w, w_out_b)
    dq_na, dk_na, dv_na, rpbacc = _na_bwd(qkv_na, tbl, d_ona, o_na, lse_na)
    g_rpb = _rpb_reduce(rpbacc, rows)
    dq_f, dk_f, dv_f, dgi_f, dgf_f = _ml_bwd(qk_act, proj, gi_f, gf_f, dhs, cchk_f, nmchk_f, (),
                                             False, "ml_bwd_f")
    dq_ml, dk_ml, dv_ml, dgi_b, dgf_b = _ml_bwd(qk_act, proj, gi_b, gf_b, dhs, cchk_b, nmchk_b, (dq_f, dk_f, dv_f),
                                                True, "ml_bwd_b")
    du, g_conv_w, g_conv_b = _conv_bwd(dq_ml, dk_ml, pre, proj, conv_w8, tm)
    dgates = _gate_cols(dgi_f, dgf_f, dgi_b, dgf_b, t)
    grad_x, g_w_in, g_b_in, dscale, dshift, g_nw = _in_bwd(
        [dq_na, dk_na, dv_na, d_naz, du, dv_ml, d_o, d_z, dgates], x, dres, w_in_t, norm_w, scale, shift)
    dmod = jnp.concatenate([dshift, dscale, dgate], axis=1)
    return (loss, grad_x, dmod, g_nw, g_w_in, g_b_in, g_conv_w, g_conv_b, g_rpb, g_mlnw, g_w_out, g_fnw)


MESH = pl.DeviceIdType.MESH
N_DEV = 8
ANY = pl.BlockSpec(memory_space=pl.ANY)
WHOLE_VMEM = pl.BlockSpec(memory_space=pltpu.VMEM)


def _allgather8(blocks, name):
    na = len(blocks)

    def body(*refs):
        x_refs = refs[:na]
        out_refs = refs[na:2 * na]
        send_sems, recv_sems, local_sems = refs[2 * na:]
        x, y, c = lax.axis_index("x"), lax.axis_index("y"), lax.axis_index("c")
        me, sibling = (x, y, c), (x, y, 1 - c)
        chips = [(1 - x, y), (x, 1 - y), (1 - x, 1 - y)]

        def rows(a, px, py, pc):
            return out_refs[a].at[4 * px + 2 * py + pc]

        def copy(a, k, block, to, src=None):
            return pltpu.make_async_remote_copy(
                src_ref=rows(a, *block) if src is None else src, dst_ref=rows(a, *block),
                send_sem=send_sems.at[a, k], recv_sem=recv_sems.at[a, k],
                device_id=to, device_id_type=MESH)

        mine, first, passed = [], [], []
        for a in range(na):
            cp = pltpu.make_async_copy(x_refs[a], rows(a, *me), local_sems.at[a])
            cp.start()
            mine.append(cp)
            first.append(copy(a, 0, me, sibling, src=x_refs[a]))
            first += [copy(a, 1 + j, me, (*chip, c), src=x_refs[a]) for j, chip in enumerate(chips)]
        for cp in first:
            cp.start()
        for a in range(na):
            for j, chip in enumerate(chips):
                copy(a, 1 + j, (*chip, c), me).wait_recv()
                fwd = copy(a, 4 + j, (*chip, c), sibling)
                fwd.start()
                passed.append(fwd)
        for a in range(na):
            copy(a, 0, sibling, me).wait_recv()
            for j, chip in enumerate(chips):
                copy(a, 4 + j, (*chip, 1 - c), me).wait_recv()
        for cp in first + passed:
            cp.wait_send()
        for cp in mine:
            cp.wait()

    return pl.pallas_call(
        body, name=name,
        out_shape=[jax.ShapeDtypeStruct((N_DEV,) + b.shape, b.dtype) for b in blocks],
        in_specs=[WHOLE_VMEM] * na, out_specs=[WHOLE_VMEM] * na,
        scratch_shapes=[pltpu.SemaphoreType.DMA((na, 7)), pltpu.SemaphoreType.DMA((na, 7)),
                        pltpu.SemaphoreType.DMA((na,))],
        compiler_params=pltpu.CompilerParams(vmem_limit_bytes=VMEM_LIMIT),
    )(*blocks)


def _pair_exchange(arrs, name):
    na = len(arrs)

    def body(*refs):
        in_refs = refs[:na]
        out_refs = refs[na:2 * na]
        send_sems, recv_sems = refs[2 * na:]
        sibling = (lax.axis_index("x"), lax.axis_index("y"), 1 - lax.axis_index("c"))
        copies = [pltpu.make_async_remote_copy(
            src_ref=in_refs[a], dst_ref=out_refs[a], send_sem=send_sems.at[a], recv_sem=recv_sems.at[a],
            device_id=sibling, device_id_type=MESH) for a in range(na)]
        for cp in copies:
            cp.start()
        for cp in copies:
            cp.wait()

    return pl.pallas_call(
        body, name=name,
        out_shape=[jax.ShapeDtypeStruct(a.shape, a.dtype) for a in arrs],
        in_specs=[ANY] * na, out_specs=[ANY] * na,
        scratch_shapes=[pltpu.SemaphoreType.DMA((na,)), pltpu.SemaphoreType.DMA((na,))],
    )(*arrs)


def _chip_exchange(arrs, name):
    na = len(arrs)

    def body(*refs):
        in_refs = refs[:na]
        out_refs = refs[na:2 * na]
        send_sems, recv_sems, local_sems = refs[2 * na:]
        x, y, c = lax.axis_index("x"), lax.axis_index("y"), lax.axis_index("c")
        my_chip = 2 * x + y
        chips = [(1 - x, y), (x, 1 - y), (1 - x, 1 - y)]
        local, remote = [], []
        for a in range(na):
            cp = pltpu.make_async_copy(in_refs[a].at[my_chip], out_refs[a].at[my_chip], local_sems.at[a])
            cp.start()
            local.append(cp)
            for j, (px, py) in enumerate(chips):
                cp = pltpu.make_async_remote_copy(
                    src_ref=in_refs[a].at[2 * px + py], dst_ref=out_refs[a].at[my_chip],
                    send_sem=send_sems.at[a, j], recv_sem=recv_sems.at[a, j],
                    device_id=(px, py, c), device_id_type=MESH)
                cp.start()
                remote.append(cp)
        for cp in remote:
            cp.wait()
        for cp in local:
            cp.wait()

    return pl.pallas_call(
        body, name=name,
        out_shape=[jax.ShapeDtypeStruct(a.shape, a.dtype) for a in arrs],
        in_specs=[ANY] * na, out_specs=[ANY] * na,
        scratch_shapes=[pltpu.SemaphoreType.DMA((na, 3)), pltpu.SemaphoreType.DMA((na, 3)),
                        pltpu.SemaphoreType.DMA((na,))],
    )(*arrs)


def _rows_tile(r):
    for cand in (512, 256, 128, 64, 32, 16, 8):
        if r % cand == 0:
            return cand
    return r


def _add2(a, b, name, out_dtype):
    s, r, n = a.shape
    tr = _rows_tile(r)

    def body(a_ref, b_ref, o_ref):
        o_ref[...] = (a_ref[...] + b_ref[...]).astype(out_dtype)

    spec = pl.BlockSpec((1, tr, n), lambda i, j: (i, j, 0))
    return pl.pallas_call(
        body, name=name, grid=(s, r // tr), in_specs=[spec, spec], out_specs=spec,
        out_shape=jax.ShapeDtypeStruct(a.shape, out_dtype),
        compiler_params=_cparams(("parallel", "parallel")),
    )(a, b)


def _sum_slabs(a, name):
    s, r, n = a.shape
    tr = _rows_tile(r)

    def body(a_ref, o_ref):
        acc = a_ref[0].astype(F32)
        for k in range(1, s):
            acc = acc + a_ref[k].astype(F32)
        o_ref[...] = acc

    return pl.pallas_call(
        body, name=name, grid=(r // tr,),
        in_specs=[pl.BlockSpec((s, tr, n), lambda i: (0, i, 0))],
        out_specs=pl.BlockSpec((tr, n), lambda i: (i, 0)),
        out_shape=jax.ShapeDtypeStruct((r, n), F32),
        compiler_params=_cparams(("parallel",)),
    )(a)


ADAMW_WHOLE = 64 * 1024


def _adamw(w, g, m, v, name):
    r, n = w.shape
    if r * n <= ADAMW_WHOLE:
        blk, grid, imap = (r, n), (1,), (lambda i: (0, 0))
    elif r % 8 == 0:
        blk, grid, imap = (_rows_tile(r), n), (r // _rows_tile(r),), (lambda i: (i, 0))
    else:
        blk, grid, imap = (r, 128), (n // 128,), (lambda i: (0, i))
    c1 = 1.0 / (1.0 - ADAM_B1 ** ADAM_STEP)
    c2 = 1.0 / (1.0 - ADAM_B2 ** ADAM_STEP)

    def body(w_ref, g_ref, m_ref, v_ref, d_ref, nm_ref, nv_ref):
        gv = g_ref[...]
        nm = ADAM_B1 * m_ref[...] + (1.0 - ADAM_B1) * gv
        nv = ADAM_B2 * v_ref[...] + (1.0 - ADAM_B2) * (gv * gv)
        nm_ref[...] = nm
        nv_ref[...] = nv
        d_ref[...] = -ADAM_LR * ((nm * c1) / (jnp.sqrt(nv * c2) + ADAM_EPS) + ADAM_WD * w_ref[...])

    spec = pl.BlockSpec(blk, imap)
    o = jax.ShapeDtypeStruct((r, n), F32)
    return pl.pallas_call(
        body, name=name, grid=grid, in_specs=[spec] * 4, out_specs=[spec] * 3, out_shape=[o, o, o],
        compiler_params=_cparams(("parallel",)),
    )(w, g, m, v)


def _mod_fwd(c_all, w_ada_s, b_ada_s):
    def body(c_ref, w_ref, b_ref, o_ref):
        o_ref[...] = jnp.dot(_silu(c_ref[...]), w_ref[...], precision=HI, preferred_element_type=F32) + b_ref[...]

    return pl.pallas_call(
        body, name="mod_fwd", out_shape=jax.ShapeDtypeStruct((c_all.shape[0], w_ada_s.shape[1]), F32),
        in_specs=[WHOLE_VMEM] * 3, out_specs=WHOLE_VMEM,
        compiler_params=pltpu.CompilerParams(vmem_limit_bytes=VMEM_LIMIT),
    )(c_all, w_ada_s, b_ada_s)


def _wada_grad(c_all, dmod_s):
    def body(c_ref, d_ref, o_ref):
        o_ref[...] = lax.dot_general(_silu(c_ref[...]), d_ref[...], (((0,), (0,)), ((), ())),
                                     precision=HI, preferred_element_type=F32)

    return pl.pallas_call(
        body, name="w_ada_grad", out_shape=jax.ShapeDtypeStruct((c_all.shape[1], dmod_s.shape[1]), F32),
        in_specs=[WHOLE_VMEM] * 2, out_specs=WHOLE_VMEM,
        compiler_params=pltpu.CompilerParams(vmem_limit_bytes=VMEM_LIMIT),
    )(c_all, dmod_s)


SMALL_ROWS = 24


def _pad_rows(v, nrows):
    v = v.reshape(-1)
    return jnp.pad(v, (0, nrows * 1024 - v.shape[0])).reshape(nrows, 1024)


def _pack_small(b_ada, norm_w, b_in, conv_w_full, conv_b, rpb, ml_norm_w, final_norm_w, last):
    parts = [_pad_rows(b_ada, 3), _pad_rows(norm_w, 1), _pad_rows(b_in, 5), _pad_rows(conv_w_full, 5),
             _pad_rows(conv_b, 1), _pad_rows(rpb, 4), _pad_rows(ml_norm_w, 1), _pad_rows(final_norm_w, 1),
             _pad_rows(last, 3)]
    return jnp.concatenate(parts, axis=0)


def _unpack_small(p):
    return dict(b_ada=p[0:3].reshape(1, 3072), norm_w=p[3:4], b_in=p[4:9].reshape(-1)[:IN_W].reshape(1, IN_W),
                conv_w=p[9:14], conv_b=p[14:15],
                rpb=p[15:19].reshape(-1)[:NA_HEADS * 15 * 31].reshape(1, NA_HEADS, 15, 31),
                ml_norm_w=p[19:20, :ML_W], final_norm_w=p[20], last=p[21])


def kernel(x, c, w_ada, b_ada, norm_w, w_in, b_in, conv_w, conv_b, rpb, ml_norm_w, w_out, final_norm_w, loss_target, m_w_ada, m_b_ada, m_norm_w, m_w_in, m_b_in, m_conv_w, m_conv_b, m_rpb, m_ml_norm_w, m_w_out, m_final_norm_w, v_w_ada, v_b_ada, v_norm_w, v_w_in, v_b_in, v_conv_w, v_conv_b, v_rpb, v_ml_norm_w, v_w_out, v_final_norm_w):
    xi, yi, ci = lax.axis_index("x"), lax.axis_index("y"), lax.axis_index("c")
    chip = 2 * xi + yi
    dev = 2 * chip + ci
    t = x.shape[1]
    ada_n = w_ada.shape[2]
    in_n = w_in.shape[2]
    out_r = w_out.shape[1]

    c_blk = jnp.pad(c, ((0, 7), (0, 0)))
    w_in_t, m_w_in_t, v_w_in_t = w_in[0].T, m_w_in[0].T, v_w_in[0].T
    in_h = in_n // 2
    w_in_half = lax.dynamic_slice_in_dim(w_in_t, ci * in_h, in_h, axis=0).astype(BF16)
    w_out_half = lax.dynamic_slice_in_dim(w_out[0], ci * (out_r // 2), out_r // 2, axis=0).astype(BF16)
    conv_blk = jnp.pad(conv_w[0], ((0, 3), (0, 0)))
    c_g, conv_g, w_in_g, w_out_g = _allgather8([c_blk, conv_blk, w_in_half, w_out_half], "gather_c_weights")
    c_all = c_g[:, 0]
    w_out_g = w_out_g.reshape(D_MODEL, D_MODEL)
    b_ada_s = lax.dynamic_slice_in_dim(b_ada, chip * ada_n, ada_n, axis=1)
    mod_s = _mod_fwd(c_all, w_ada[0], b_ada_s)
    (mod_g,) = _allgather8([mod_s], "gather_mod")
    mod_mine = lax.dynamic_index_in_dim(mod_g, dev, axis=1, keepdims=False)
    mod = mod_mine[0::2].reshape(1, 3 * D_MODEL)
    shift, scale, gate = mod[:, :D_MODEL], mod[:, D_MODEL:2 * D_MODEL], mod[:, 2 * D_MODEL:]

    w_in_tp = jnp.pad(w_in_g.reshape(IN_W, D_MODEL), ((0, IN_PAD - IN_W), (0, 0)))
    b_in_p = jnp.pad(b_in, ((0, 0), (0, IN_PAD - IN_W)))
    conv_w8 = conv_g.reshape(4, 2, 8, conv_w.shape[2])[:, 0].transpose(1, 0, 2).reshape(8, D_MODEL)

    (loss, grad_x, dmod, g_nw, g_w_in, g_b_in, g_conv_w, g_conv_b, g_rpb, g_mlnw, g_w_out, g_fnw) = _local_step(
        x[0], loss_target[0], shift, scale, gate, norm_w, w_in_tp, b_in_p, conv_w8, conv_b, rpb[0],
        ml_norm_w, w_out_g, final_norm_w.reshape(1, D_MODEL))

    g_in_t = g_w_in

    def halves(a, per_chip, h):
        return jnp.stack([lax.dynamic_slice_in_dim(a, k * per_chip + h * (per_chip // 2), per_chip // 2, axis=0)
                          for k in range(4)])

    ri, ro = _pair_exchange([halves(g_in_t, in_n, 1 - ci), halves(g_w_out, out_r, 1 - ci)], "rs_pair")
    pi = _add2(halves(g_in_t, in_n, ci), ri, "rs_pair_add_in", BF16)
    po = _add2(halves(g_w_out, out_r, ci), ro, "rs_pair_add_out", BF16)
    qi, qo = _chip_exchange([pi, po], "rs_chips")
    si = _sum_slabs(qi, "rs_sum_in")
    so = _sum_slabs(qo, "rs_sum_out")
    ti, to = _pair_exchange([si, so], "rs_share")
    g_w_in_s = jnp.where(ci == 0, jnp.concatenate([si, ti], axis=0), jnp.concatenate([ti, si], axis=0))
    g_w_out_s = jnp.where(ci == 0, jnp.concatenate([so, to], axis=0), jnp.concatenate([to, so], axis=0))

    small = _pack_small(dmod, g_nw, g_b_in[:, :IN_W], g_conv_w[:CONV_W], g_conv_b, g_rpb, g_mlnw, g_fnw,
                        jnp.pad(loss, ((0, 0), (0, 1024 - 128))))
    (small_g,) = _allgather8([small], "gather_small")
    small_sum = _sum_slabs(small_g, "small_sum")
    gs = _unpack_small(small_sum)
    dmod_all = small_g[:, 0:3].reshape(N_DEV, 3 * D_MODEL)
    g_w_ada_s = _wada_grad(c_all, lax.dynamic_slice_in_dim(dmod_all, chip * ada_n, ada_n, axis=1))
    g_conv_w_s = lax.dynamic_slice_in_dim(gs['conv_w'], chip * conv_w.shape[2], conv_w.shape[2], axis=1)
    loss_total = gs['last'][0]

    small_names = ('b_ada', 'norm_w', 'b_in', 'conv_b', 'rpb', 'ml_norm_w', 'final_norm_w')
    small_w = (b_ada, norm_w, b_in, conv_b, rpb, ml_norm_w, final_norm_w)
    small_m = (m_b_ada, m_norm_w, m_b_in, m_conv_b, m_rpb, m_ml_norm_w, m_final_norm_w)
    small_v = (v_b_ada, v_norm_w, v_b_in, v_conv_b, v_rpb, v_ml_norm_w, v_final_norm_w)
    ds_, nms, nvs = {}, {}, {}
    for nm_, w_, m_, v_ in zip(small_names, small_w, small_m, small_v):
        two_d = (NA_HEADS, w_.size // NA_HEADS) if nm_ == 'rpb' else (1, w_.size)
        outs = _adamw(w_.reshape(two_d), gs[nm_].reshape(two_d), m_.reshape(two_d), v_.reshape(two_d),
                      "adamw_" + nm_)
        ds_[nm_], nms[nm_], nvs[nm_] = [o.reshape(w_.shape) for o in outs]
    d_ada, nm_ada, nv_ada = _adamw(w_ada[0], g_w_ada_s, m_w_ada[0], v_w_ada[0], "adamw_w_ada")
    d_in, nm_in, nv_in = _adamw(w_in_t, g_w_in_s, m_w_in_t, v_w_in_t, "adamw_w_in")
    d_out, nm_out, nv_out = _adamw(w_out[0], g_w_out_s, m_w_out[0], v_w_out[0], "adamw_w_out")
    d_cw, nm_cw, nv_cw = _adamw(conv_w[0], g_conv_w_s, m_conv_w[0], v_conv_w[0], "adamw_conv_w")

    def group(big_ada, big_in, big_out, cw, sm):
        return (big_ada[None], sm['b_ada'], sm['norm_w'], big_in.T[None], sm['b_in'], cw[None], sm['conv_b'],
                sm['rpb'], sm['ml_norm_w'], big_out[None], sm['final_norm_w'])

    return ((loss_total, grad_x[None])
            + group(g_w_ada_s, g_w_in_s, g_w_out_s, g_conv_w_s, gs)
            + group(d_ada, d_in, d_out, d_cw, ds_)
            + group(nm_ada, nm_in, nm_out, nm_cw, nms)
            + group(nv_ada, nv_in, nv_out, nv_cw, nvs))
```

```python
import functools

import numpy as np
import jax
import jax.numpy as jnp
from jax import lax
from jax.experimental import pallas as pl
from jax.experimental.pallas import tpu as pltpu

F32 = jnp.float32
BF16 = jnp.bfloat16
HI = lax.Precision.HIGHEST

D_MODEL = 1024
GRID_W = 64
NA_W = 512
NA_HEAD_DIM = 64
NA_HEADS = 8
NA_KH = 8
NA_KW = 16
ML_W = 512
ML_HEADS = 4
ML_HEAD_DIM = 128
ML_CHUNK = 128
CONV_W = 5
EPS = 1e-6
IN_W = 4 * NA_W + 5 * ML_W + 4 * ML_HEADS
IN_MAIN = 4 * NA_W + 5 * ML_W
IN_PAD = IN_MAIN + 128
NEG = -1e30

ADAM_LR = 0.001
ADAM_B1 = 0.9
ADAM_B2 = 0.999
ADAM_EPS = 1e-08
ADAM_WD = 0.01
ADAM_STEP = 10

NA_QROWS = 8
NA_KROWS = 16
NA_QT = NA_QROWS * GRID_W
NA_KT = NA_KROWS * GRID_W
NA_KCH = 256
NA_RC = 32
ML_NB = 32
ML_TB = ML_NB * ML_CHUNK
ML_HPS = 1

VMEM_LIMIT = 56 * 1024 * 1024
IN_BWD_VMEM_LIMIT = 60 * 1024 * 1024


def _cparams(sem, vmem=VMEM_LIMIT):
    return pltpu.CompilerParams(dimension_semantics=sem, vmem_limit_bytes=vmem)


def _silu(x):
    return x * jax.nn.sigmoid(x)


def _dsilu(x):
    s = jax.nn.sigmoid(x)
    return s * (1.0 + x * (1.0 - s))


def _dot(a, b, dims):
    return lax.dot_general(a, b, (dims, ((), ())), preferred_element_type=F32)


def _nn(a, b):
    return _dot(a, b, ((1,), (0,)))


def _nt(a, b):
    return _dot(a, b, ((1,), (1,)))


def _tn(a, b):
    return _dot(a, b, ((0,), (0,)))


def _row(n):
    return pl.BlockSpec((1, n), lambda i: (0, 0))


def _modulated_norm(xv, nw, sc, sh):
    r = lax.rsqrt(jnp.mean(xv * xv, axis=-1, keepdims=True) + EPS)
    xn = xv * r
    return xn * nw * (1.0 + sc) + sh, xn, r


IN_TN = 768


def _in_proj(x, norm_w, scale, shift, w_in_t, b_in_p):
    t, d = x.shape
    tm = 2048
    gcol = IN_MAIN // 128

    def body(x_ref, nw_ref, sc_ref, sh_ref, w_ref, b_ref, wg_ref, bg_ref, proj_ref, g_ref, h_scr):
        @pl.when(pl.program_id(1) == 0)
        def _():
            h, _, _ = _modulated_norm(x_ref[...], nw_ref[...], sc_ref[...], sh_ref[...])
            h_scr[...] = h.astype(BF16)
            g_ref[...] = _nt(h_scr[...], wg_ref[...]) + bg_ref[...]
        proj_ref[...] = (_nt(h_scr[...], w_ref[...]) + b_ref[...]).astype(BF16)

    row = lambda n: pl.BlockSpec((1, n), lambda i, j: (0, 0))
    return pl.pallas_call(
        body, name="in_proj", grid=(t // tm, IN_MAIN // IN_TN),
        in_specs=[pl.BlockSpec((tm, d), lambda i, j: (i, 0)), row(d), row(d), row(d),
                  pl.BlockSpec((IN_TN, d), lambda i, j: (j, 0)), pl.BlockSpec((1, IN_TN), lambda i, j: (0, j)),
                  pl.BlockSpec((128, d), lambda i, j: (gcol, 0)), pl.BlockSpec((1, 128), lambda i, j: (0, gcol))],
        out_specs=[pl.BlockSpec((tm, IN_TN), lambda i, j: (i, j)), pl.BlockSpec((tm, 128), lambda i, j: (i, 0))],
        out_shape=[jax.ShapeDtypeStruct((t, IN_MAIN), BF16), jax.ShapeDtypeStruct((t, 128), F32)],
        scratch_shapes=[pltpu.VMEM((tm, d), BF16)],
        compiler_params=_cparams(("parallel", "arbitrary")),
    )(x, norm_w, scale, shift, w_in_t, b_in_p, w_in_t, b_in_p)


def _ml_norm_parts(hs, o, z, nw):
    outs = []
    for hh in range(ML_HEADS):
        sl = slice(hh * ML_HEAD_DIM, (hh + 1) * ML_HEAD_DIM)
        so = jax.nn.sigmoid(o[:, sl])
        hm = hs[:, sl] * so
        mu = jnp.mean(hm, axis=-1, keepdims=True)
        cen = hm - mu
        var = jnp.mean(cen * cen, axis=-1, keepdims=True)
        rs = lax.rsqrt(var + EPS)
        outs.append((sl, cen * rs, rs, so))
    return outs


def _tail(o_na, proj, h_f, h_b, x, target, gate, ml_norm_w, fnw, w_out_b):
    t, d = x.shape
    tm = 512

    def body(ona_ref, naz_ref, hf_ref, hb_ref, o_ref, z_ref, x_ref, tg_ref, g_ref, nw_ref, fw_ref, w_ref,
             loss_ref, dres_ref, dona_ref, dnaz_ref, dhs_ref, do_ref, dz_ref, dgate_ref, gfw_ref, gnw_ref,
             gwo_ref, mix_scr):
        @pl.when(pl.program_id(0) == 0)
        def _():
            for r in (loss_ref, dgate_ref, gfw_ref, gnw_ref, gwo_ref):
                r[...] = jnp.zeros_like(r)
        naz = naz_ref[...].astype(F32)
        ona = ona_ref[...]
        sg_naz = jax.nn.sigmoid(naz)
        sna = naz * sg_naz
        mix_scr[:, 0:NA_W] = (ona * sna).astype(BF16)
        hs = hf_ref[...] + hb_ref[...]
        z = z_ref[...].astype(F32)
        ov = o_ref[...].astype(F32)
        parts = _ml_norm_parts(hs, ov, z, nw_ref[...])
        sgz = [jax.nn.sigmoid(z[:, sl]) for sl, _, _, _ in parts]
        for (sl, xn, _, _), sg in zip(parts, sgz):
            mix_scr[:, NA_W + sl.start:NA_W + sl.stop] = (xn * nw_ref[:, sl] * (z[:, sl] * sg)).astype(BF16)
        mixb = mix_scr[...]
        wv = w_ref[...]
        yv = _nn(mixb, wv)
        gate_v = g_ref[...]
        hres = x_ref[...] + gate_v * yv
        r = lax.rsqrt(jnp.mean(hres * hres, axis=-1, keepdims=True) + EPS)
        xnf = hres * r
        err = xnf * fw_ref[...] - tg_ref[...]
        loss_ref[...] += 0.5 * jnp.sum(jnp.sum(err * err, axis=-1, keepdims=True) * (1.0 / d), axis=0, keepdims=True)
        dout = err * (1.0 / d)
        gfw_ref[...] += jnp.sum(dout * xnf, axis=0, keepdims=True)
        dxn = dout * fw_ref[...]
        dres = r * (dxn - xnf * jnp.mean(dxn * xnf, axis=-1, keepdims=True))
        dres_ref[...] = dres
        dgate_ref[...] += jnp.sum(dres * yv, axis=0, keepdims=True)
        dyb = (dres * gate_v).astype(BF16)
        gwo_ref[...] += _tn(mixb, dyb)
        dmix = _nt(dyb, wv)
        dna = dmix[:, 0:NA_W]
        dona_ref[...] = dna * sna
        dnaz_ref[...] = (dna * ona * (sg_naz * (1.0 + naz * (1.0 - sg_naz)))).astype(BF16)
        for (sl, xn, rs, so), sg in zip(parts, sgz):
            dyv = dmix[:, NA_W + sl.start:NA_W + sl.stop]
            zz = z[:, sl]
            sz = zz * sg
            w = nw_ref[:, sl]
            dz_ref[:, sl] = (dyv * xn * w * (sg * (1.0 + zz * (1.0 - sg)))).astype(BF16)
            gnw_ref[:, sl] += jnp.sum(dyv * xn * sz, axis=0, keepdims=True)
            dxm = dyv * w * sz
            dhm = rs * (dxm - jnp.mean(dxm, axis=-1, keepdims=True)
                        - xn * jnp.mean(dxm * xn, axis=-1, keepdims=True))
            dhs_ref[:, sl] = dhm * so
            do_ref[:, sl] = (dhm * hs[:, sl] * so * (1.0 - so)).astype(BF16)

    blk = lambda c: pl.BlockSpec((tm, 512), lambda i, c=c: (i, c))
    full = pl.BlockSpec((tm, d), lambda i: (i, 0))
    o512 = jax.ShapeDtypeStruct((t, 512), F32)
    b512 = jax.ShapeDtypeStruct((t, 512), BF16)
    whole = pl.BlockSpec((d, d), lambda i: (0, 0))
    return pl.pallas_call(
        body, name="tail", grid=(t // tm,),
        in_specs=[blk(0), blk(3), blk(0), blk(0), blk(7), blk(8), full, full, _row(d), _row(ML_W), _row(d), whole],
        out_specs=[pl.BlockSpec((1, 128), lambda i: (0, 0)), full] + [blk(0)] * 5
        + [_row(d), _row(d), _row(ML_W), whole],
        out_shape=[jax.ShapeDtypeStruct((1, 128), F32), jax.ShapeDtypeStruct((t, d), F32),
                   o512, b512, o512, b512, b512]
        + [jax.ShapeDtypeStruct((1, d), F32), jax.ShapeDtypeStruct((1, d), F32),
           jax.ShapeDtypeStruct((1, ML_W), F32), jax.ShapeDtypeStruct((d, d), F32)],
        scratch_shapes=[pltpu.VMEM((tm, d), BF16)],
        compiler_params=_cparams(("arbitrary",)),
    )(o_na, proj, h_f, h_b, proj, proj, x, target, gate, ml_norm_w, fnw, w_out_b)


def _in_bwd(pieces, x, dres, w_in_t, norm_w, scale, shift):
    t, d = x.shape
    tm = 512
    nt = t // tm
    widths = [p.shape[1] for p in pieces]
    offs = [sum(widths[:k]) for k in range(len(widths))]
    assert sum(widths) == IN_PAD
    npc = len(pieces)

    def body(*refs):
        p_refs = refs[:npc]
        (x_ref, dres_ref, w_hbm, nw_ref, sc_ref, sh_ref,
         gx_ref, gw_hbm, gb_ref, dsc_ref, dsh_ref, gnw_ref, w_vmem, acc, stage, sem) = refs[npc:]
        i = pl.program_id(0)

        @pl.when(i == 0)
        def _():
            cp = pltpu.make_async_copy(w_hbm, w_vmem, sem.at[0])
            cp.start()
            acc[...] = jnp.zeros_like(acc)
            gb_ref[...] = jnp.zeros_like(gb_ref)
            dsc_ref[...] = jnp.zeros_like(dsc_ref)
            dsh_ref[...] = jnp.zeros_like(dsh_ref)
            gnw_ref[...] = jnp.zeros_like(gnw_ref)
            cp.wait()

        nw = nw_ref[...]
        s1 = 1.0 + sc_ref[...]
        h, xn, r = _modulated_norm(x_ref[...], nw, sc_ref[...], sh_ref[...])
        hb = h.astype(BF16)
        dhv = jnp.zeros((tm, d), F32)
        for p_ref, c0, w in zip(p_refs, offs, widths):
            pt = p_ref[...]
            pb = pt.astype(BF16)
            dhv = dhv + _nn(pb, w_vmem[c0:c0 + w, :])
            acc[:, c0:c0 + w] += _tn(hb, pb)
            gb_ref[:, c0:c0 + w] += jnp.sum(pt.astype(F32), axis=0, keepdims=True)
        dsh_ref[...] += jnp.sum(dhv, axis=0, keepdims=True)
        dsc_ref[...] += jnp.sum(dhv * xn * nw, axis=0, keepdims=True)
        gnw_ref[...] += jnp.sum(dhv * xn * s1, axis=0, keepdims=True)
        dxn = dhv * nw * s1
        gx_ref[...] = dres_ref[...] + r * (dxn - xn * jnp.mean(dxn * xn, axis=-1, keepdims=True))

        @pl.when(i == nt - 1)
        def _():
            copies = []
            for blk in range(IN_PAD // 128):
                slot = blk % 2
                if blk >= 2:
                    copies[blk - 2].wait()
                stage[slot] = acc[:, blk * 128:(blk + 1) * 128].T
                cp = pltpu.make_async_copy(stage.at[slot], gw_hbm.at[pl.ds(blk * 128, 128), :], sem.at[1 + slot])
                cp.start()
                copies.append(cp)
            copies[-2].wait()
            copies[-1].wait()

    full = pl.BlockSpec((tm, d), lambda i: (i, 0))
    return pl.pallas_call(
        body, name="in_bwd", grid=(nt,),
        in_specs=[pl.BlockSpec((tm, w), lambda i: (i, 0)) for w in widths]
        + [full, full, pl.BlockSpec(memory_space=pl.ANY), _row(d), _row(d), _row(d)],
        out_specs=[full, pl.BlockSpec(memory_space=pl.ANY), _row(IN_PAD), _row(d), _row(d), _row(d)],
        out_shape=[jax.ShapeDtypeStruct((t, d), F32), jax.ShapeDtypeStruct((IN_PAD, d), F32),
                   jax.ShapeDtypeStruct((1, IN_PAD), F32)] + [jax.ShapeDtypeStruct((1, d), F32)] * 3,
        scratch_shapes=[pltpu.VMEM((IN_PAD, d), BF16), pltpu.VMEM((d, IN_PAD), F32),
                        pltpu.VMEM((2, 128, d), F32), pltpu.SemaphoreType.DMA((3,))],
        compiler_params=_cparams(("arbitrary",), IN_BWD_VMEM_LIMIT),
    )(*pieces, x, dres, w_in_t, norm_w, scale, shift)


def _na_static(rows):
    cases = [(0, 0), (NA_QROWS, NA_QROWS - 4), (rows - NA_QROWS, rows - NA_KROWS)]
    dy = np.zeros((3, NA_QROWS, NA_KROWS), np.int32)
    rv = np.zeros((3, NA_QROWS, NA_KROWS), bool)
    for cs, (r0, kr0) in enumerate(cases):
        for i in range(NA_QROWS):
            for j in range(NA_KROWS):
                r, kr = r0 + i, kr0 + j
                rs = min(max(r - NA_KH // 2, 0), rows - NA_KH)
                rv[cs, i, j] = rs <= kr <= rs + NA_KH - 1
                dy[cs, i, j] = min(max(kr - r + NA_KH - 1, 0), 2 * NA_KH - 2)
    cq = np.arange(GRID_W)[:, None]
    ck = np.arange(GRID_W)[None, :]
    cs0 = np.clip(cq - NA_KW // 2, 0, GRID_W - NA_KW)
    cv = (ck >= cs0) & (ck < cs0 + NA_KW)
    dx = np.clip(ck - cq, -(NA_KW - 1), NA_KW - 1) + NA_KW - 1
    return dy, rv, dx.astype(np.int32), cv


def _na_bias_table(rpb, rows):
    _, _, dx, cv = _na_static(rows)
    ndy = 2 * NA_KH - 1
    onehot = (dx.reshape(1, -1) == np.arange(2 * NA_KW - 1)[:, None]).astype(np.float32)
    rpx = jnp.dot(rpb.reshape(NA_HEADS * ndy, 2 * NA_KW - 1), jnp.asarray(onehot), precision=HI)
    rpx = jnp.where(cv[None, None], rpx.reshape(NA_HEADS, ndy, GRID_W, GRID_W), NEG)
    neg = jnp.full((NA_HEADS, 1, GRID_W, GRID_W), NEG, F32)
    rpx = jnp.concatenate([rpx, neg], axis=1)
    nxt = jnp.concatenate([rpx[:, 1:], neg], axis=1)
    negs = jnp.broadcast_to(neg, rpx.shape)
    pairs = jnp.concatenate([jnp.concatenate([rpx, nxt], axis=3), jnp.concatenate([rpx, negs], axis=3),
                             jnp.concatenate([negs, rpx], axis=3)], axis=1)
    npair = pairs.shape[1]

    def body(m_ref, o_ref):
        cs = pl.program_id(1)
        r0 = jnp.where(cs == 0, 0, jnp.where(cs == 1, NA_QROWS, rows - NA_QROWS))
        kr0 = jnp.where(cs == 0, 0, jnp.where(cs == 1, NA_QROWS - NA_KH // 2, rows - NA_KROWS))
        for i in range(NA_QROWS):
            r = r0 + i
            rs = jnp.clip(r - NA_KH // 2, 0, rows - NA_KH)
            for jp in range(NA_KROWS // 2):
                kl = kr0 + 2 * jp
                vl = (kl >= rs) & (kl <= rs + NA_KH - 1)
                vr = (kl + 1 >= rs) & (kl + 1 <= rs + NA_KH - 1)
                dyl = jnp.clip(kl - r + NA_KH - 1, 0, ndy)
                dyr = jnp.clip(kl + 1 - r + NA_KH - 1, 0, ndy)
                idx = jnp.where(vl & vr, dyl, jnp.where(vl, 16 + dyl, jnp.where(vr, 32 + dyr, 16 + ndy)))
                o_ref[0, 0, i * GRID_W:(i + 1) * GRID_W, jp * 128:(jp + 1) * 128] = m_ref[0, idx]

    return pl.pallas_call(
        body, name="na_bias_table", grid=(NA_HEADS, 3),
        in_specs=[pl.BlockSpec((1, npair, GRID_W, 128), lambda h, cs: (h, 0, 0, 0))],
        out_specs=pl.BlockSpec((1, 1, NA_QT, NA_KT), lambda h, cs: (h, cs, 0, 0)),
        out_shape=jax.ShapeDtypeStruct((NA_HEADS, 3, NA_QT, NA_KT), F32),
        compiler_params=_cparams(("parallel", "parallel")),
    )(pairs)


def _na_specs(t):
    nb = t // NA_QT
    nkb = t // NA_KCH
    npieces = NA_KT // NA_KCH

    def kb0(b):
        return jnp.clip(b * (NA_QT // NA_KCH) - 1, 0, nkb - npieces)

    def case(b):
        return jnp.where(b == 0, 0, jnp.where(b == nb - 1, 2, 1))

    q_spec = pl.BlockSpec((NA_QT, 128), lambda p, b: (b, p))
    k_specs = [pl.BlockSpec((NA_KCH, 128), lambda p, b, i=i: (kb0(b) + i, 4 + p)) for i in range(npieces)]
    v_specs = [pl.BlockSpec((NA_KCH, 128), lambda p, b, i=i: (kb0(b) + i, 8 + p)) for i in range(npieces)]
    tbl_spec = pl.BlockSpec((2, 1, NA_QT, NA_KT), lambda p, b: (p, case(b), 0, 0))
    io_spec = pl.BlockSpec((NA_QT, 128), lambda p, b: (b, p))
    return nb, npieces, kb0, case, q_spec, k_specs, v_specs, tbl_spec, io_spec


NA_HALF = NA_QT // 2
NA_COMBOS_ALL = tuple((i, 0, NA_QT) for i in range(NA_KT // NA_KCH))
NA_COMBOS_INNER = ((0, 0, NA_HALF),) + tuple((i, 0, NA_QT) for i in range(1, NA_KT // NA_KCH - 1)) \
    + ((NA_KT // NA_KCH - 1, NA_HALF, NA_QT),)


def _na_place(val, r0, r1):
    if (r0, r1) == (0, NA_QT):
        return val
    z = jnp.zeros((NA_HALF, val.shape[1]), val.dtype)
    return jnp.concatenate([val, z] if r0 == 0 else [z, val], axis=0)


def _na_fwd(proj, tbl):
    t = proj.shape[0]
    nb, npieces, _, _, q_spec, k_specs, v_specs, tbl_spec, io_spec = _na_specs(t)
    lse_spec = pl.BlockSpec((1, NA_QT, 2), lambda p, b: (p, b, 0))

    def body(*refs):
        q_ref = refs[0]
        k_refs = refs[1:1 + npieces]
        v_refs = refs[1 + npieces:1 + 2 * npieces]
        tbl_ref, o_ref, lse_ref = refs[1 + 2 * npieces:]
        b = pl.program_id(1)

        def compute(combos):
            lane = lax.broadcasted_iota(jnp.int32, (1, 128), 1)
            qv = q_ref[...].astype(F32) * (NA_HEAD_DIM ** -0.5)
            ks = [r[...].astype(BF16) for r in k_refs]
            vs = [r[...].astype(BF16) for r in v_refs]
            hs = range(2)
            msk = [(lane // NA_HEAD_DIM) == hh for hh in hs]
            qh = [jnp.where(msk[hh], qv, 0.0).astype(BF16) for hh in hs]
            s = [[_nt(qh[hh][r0:r1], ks[i]) + tbl_ref[hh, 0, r0:r1, i * NA_KCH:(i + 1) * NA_KCH]
                  for i, r0, r1 in combos] for hh in hs]
            for h0 in (0, NA_HALF):
                rows = slice(h0, h0 + NA_HALF)
                cover = [(c, i, h0 - r0) for c, (i, r0, r1) in enumerate(combos) if r0 <= h0 < r1]
                part = [[s[hh][c][off:off + NA_HALF] for c, _, off in cover] for hh in hs]
                m = [functools.reduce(jnp.maximum, [jnp.max(v, axis=1, keepdims=True) for v in part[hh]]) for hh in hs]
                p = [[jnp.exp(v - m[hh]) for v in part[hh]] for hh in hs]
                l = [functools.reduce(jnp.add, [jnp.sum(v, axis=1, keepdims=True) for v in p[hh]]) for hh in hs]
                o = [functools.reduce(jnp.add, [_nn(p[hh][k].astype(BF16), vs[i]) for k, (_, i, _) in enumerate(cover)])
                     for hh in hs]
                for hh in hs:
                    lse_ref[0, rows, hh:hh + 1] = m[hh] + jnp.log(l[hh])
                o_ref[rows, :] = jnp.where(msk[0], o[0] / l[0], o[1] / l[1])

        inner = (b > 0) & (b < nb - 1)
        pl.when(inner)(lambda: compute(NA_COMBOS_INNER))
        pl.when(jnp.logical_not(inner))(lambda: compute(NA_COMBOS_ALL))

    return pl.pallas_call(
        body, name="na_fwd", grid=(4, nb),
        in_specs=[q_spec] + k_specs + v_specs + [tbl_spec],
        out_specs=[io_spec, lse_spec],
        out_shape=[jax.ShapeDtypeStruct((t, NA_W), F32), jax.ShapeDtypeStruct((4, t, 2), F32)],
        compiler_params=_cparams(("parallel", "arbitrary")),
    )(*([proj] * (1 + 2 * npieces)), tbl)


def _na_bwd(proj, tbl, d_o, o_na, lse):
    t = proj.shape[0]
    nb, npieces, kb0, case, q_spec, k_specs, v_specs, tbl_spec, io_spec = _na_specs(t)

    def body(*refs):
        q_ref = refs[0]
        k_refs = refs[1:1 + npieces]
        v_refs = refs[1 + npieces:1 + 2 * npieces]
        (tbl_ref, do_ref, o_ref, lse_ref, dq_ref, dk_hbm, dv_hbm, rpb_ref,
         dk_acc, dv_acc, dk_out, dv_out, s_scr, dp_scr, dsb_scr, pnb_scr, sem) = refs[1 + 2 * npieces:]
        p_id = pl.program_id(0)
        b = pl.program_id(1)

        @pl.when(b == 0)
        def _():
            dk_acc[...] = jnp.zeros_like(dk_acc)
            dv_acc[...] = jnp.zeros_like(dv_acc)

        @pl.when((b == 0) | (b == 1) | (b == nb - 1))
        def _():
            rpb_ref[...] = jnp.zeros_like(rpb_ref)

        def compute(combos):
            lane = lax.broadcasted_iota(jnp.int32, (1, 128), 1)
            scale = NA_HEAD_DIM ** -0.5
            qv = q_ref[...].astype(F32) * scale
            ks = [r[...].astype(BF16) for r in k_refs]
            vs = [r[...].astype(BF16) for r in v_refs]
            dov = do_ref[...]
            ov = o_ref[...]
            tok0 = kb0(b) * NA_KCH
            hs = range(2)
            msk = [(lane // NA_HEAD_DIM) == hh for hh in hs]
            qh = [jnp.where(msk[hh], qv, 0.0).astype(BF16) for hh in hs]
            doh = [jnp.where(msk[hh], dov, 0.0) for hh in hs]
            dohb = [doh[hh].astype(BF16) for hh in hs]
            dd = [jnp.sum(doh[hh] * ov, axis=1, keepdims=True) for hh in hs]
            for hh in hs:
                for c, (i, q0, q1) in enumerate(combos):
                    slot = (hh * len(combos) + c) % 2
                    cols = slice(i * NA_KCH, (i + 1) * NA_KCH)
                    s_scr[slot, 0:q1 - q0] = _nt(qh[hh][q0:q1], ks[i])
                    dp_scr[slot, 0:q1 - q0] = _nt(dohb[hh][q0:q1], vs[i])
                    for r0 in range(q0, q1, NA_RC):
                        rows = slice(r0, r0 + NA_RC)
                        loc = slice(r0 - q0, r0 - q0 + NA_RC)
                        p = jnp.exp(s_scr[slot, loc, :] + tbl_ref[hh, 0, rows, cols] - lse_ref[0, rows, hh:hh + 1])
                        d = p * (dp_scr[slot, loc, :] - dd[hh][rows])
                        pnb_scr[hh, rows, cols] = p.astype(BF16)
                        dsb_scr[hh, rows, cols] = d.astype(BF16)
                done = {(i, q0) for i, q0, _ in combos} | {(i, NA_HALF) for i, q0, q1 in combos if q1 - q0 == NA_QT}
                for i in range(npieces):
                    for q0 in (0, NA_HALF):
                        if (i, q0) not in done:
                            dsb_scr[hh, q0:q0 + NA_HALF, i * NA_KCH:(i + 1) * NA_KCH] = jnp.zeros(
                                (NA_HALF, NA_KCH), BF16)
            dqh = [functools.reduce(jnp.add, [_na_place(_nn(dsb_scr[hh, q0:q1, i * NA_KCH:(i + 1) * NA_KCH], ks[i]),
                                                        q0, q1) for i, q0, q1 in combos]) for hh in hs]
            dq_ref[...] = (jnp.where(msk[0], dqh[0], dqh[1]) * scale).astype(BF16)
            for i in range(npieces):
                rows = pl.ds(pl.multiple_of(tok0 + i * NA_KCH, NA_KCH), NA_KCH)
                cols = slice(i * NA_KCH, (i + 1) * NA_KCH)
                q0, q1 = [(a, e) for j, a, e in combos if j == i][0]
                dk_acc[rows, :] += (_tn(dsb_scr[0, q0:q1, cols], qh[0][q0:q1])
                                    + _tn(dsb_scr[1, q0:q1, cols], qh[1][q0:q1]))
                dv_acc[rows, :] += (_tn(pnb_scr[0, q0:q1, cols], dohb[0][q0:q1])
                                    + _tn(pnb_scr[1, q0:q1, cols], dohb[1][q0:q1]))
            for hh in hs:
                acc = dsb_scr[hh, 0:GRID_W, :].astype(F32)
                for i in range(1, NA_QROWS):
                    acc = acc + pltpu.roll(dsb_scr[hh, i * GRID_W:(i + 1) * GRID_W, :].astype(F32),
                                           NA_KT - i * GRID_W, 1)
                rpb_ref[0, 0, hh] += acc

        inner = (b > 0) & (b < nb - 1)
        pl.when(inner)(lambda: compute(NA_COMBOS_INNER))
        pl.when(jnp.logical_not(inner))(lambda: compute(NA_COMBOS_ALL))

        @pl.when(b == nb - 1)
        def _():
            def copies(pair):
                cols = pl.ds(pl.multiple_of(pair * 128, 128), 128)
                return (pltpu.make_async_copy(dk_out, dk_hbm.at[:, cols], sem.at[0]),
                        pltpu.make_async_copy(dv_out, dv_hbm.at[:, cols], sem.at[1]))

            @pl.when(p_id > 0)
            def _():
                for cp in copies(p_id - 1):
                    cp.wait()
            dk_out[...] = dk_acc[...].astype(BF16)
            dv_out[...] = dv_acc[...].astype(BF16)
            for cp in copies(p_id):
                cp.start()

            @pl.when(p_id == NA_HEADS // 2 - 1)
            def _():
                for cp in copies(p_id):
                    cp.wait()

    o512 = jax.ShapeDtypeStruct((t, NA_W), BF16)
    return pl.pallas_call(
        body, name="na_bwd", grid=(4, nb),
        in_specs=[q_spec] + k_specs + v_specs + [tbl_spec, io_spec, io_spec,
                                                 pl.BlockSpec((1, NA_QT, 2), lambda p, b: (p, b, 0))],
        out_specs=[io_spec, pl.BlockSpec(memory_space=pl.ANY), pl.BlockSpec(memory_space=pl.ANY),
                   pl.BlockSpec((1, 1, 2, GRID_W, NA_KT), lambda p, b: (p, case(b), 0, 0, 0))],
        out_shape=[o512, o512, o512, jax.ShapeDtypeStruct((4, 3, 2, GRID_W, NA_KT), F32)],
        scratch_shapes=[pltpu.VMEM((t, 128), F32), pltpu.VMEM((t, 128), F32),
                        pltpu.VMEM((t, 128), BF16), pltpu.VMEM((t, 128), BF16),
                        pltpu.VMEM((2, NA_QT, NA_KCH), F32), pltpu.VMEM((2, NA_QT, NA_KCH), F32),
                        pltpu.VMEM((2, NA_QT, NA_KT), BF16), pltpu.VMEM((2, NA_QT, NA_KT), BF16),
                        pltpu.SemaphoreType.DMA((2,))],
        compiler_params=_cparams(("arbitrary", "arbitrary")),
    )(*([proj] * (1 + 2 * npieces)), tbl, d_o, o_na, lse)


def _rpb_reduce(rpbacc, rows):
    nacc = 4 * 3 * 2

    def shift_body(a_ref, o_ref):
        acc = a_ref[0, 0:1, :]
        for cq in range(1, GRID_W):
            acc = acc + pltpu.roll(a_ref[0, cq:cq + 1, :], NA_KT - cq, 1)
        o_ref[0] = jnp.broadcast_to(acc, (8, NA_KT))

    vec = pl.pallas_call(
        shift_body, name="rpb_shift", grid=(nacc,),
        in_specs=[pl.BlockSpec((1, GRID_W, NA_KT), lambda a: (a, 0, 0))],
        out_specs=pl.BlockSpec((1, 8, NA_KT), lambda a: (a, 0, 0)),
        out_shape=jax.ShapeDtypeStruct((nacc, 8, NA_KT), F32),
        compiler_params=_cparams(("parallel",)),
    )(rpbacc.reshape(nacc, GRID_W, NA_KT))
    a = vec[:, 0].reshape(4, 3, 2, NA_KT).transpose(0, 2, 1, 3).reshape(NA_HEADS, 3, NA_KT)
    if rows // NA_QROWS < 3:
        a = a.at[:, 1].set(0.0)
    dd = np.arange(NA_KROWS)[:, None]
    dxo = np.arange(-(NA_KW - 1), NA_KW)[None, :]
    idx = ((dd * GRID_W + dxo) % NA_KT).reshape(-1)
    g = a[..., idx].reshape(NA_HEADS, 3 * NA_KROWS, 2 * NA_KW - 1)
    g = jnp.pad(g, ((0, 0), (0, 0), (0, 128 - (2 * NA_KW - 1))))
    nmat = np.zeros((16, 3 * NA_KROWS), np.float32)
    for cs, delta in enumerate((0, -(NA_KH // 2), -(NA_KROWS - NA_QROWS))):
        for d in range(NA_KROWS):
            jmi = d - NA_KROWS if (cs == 0 and d > NA_KH - 1) else d
            dy = jmi + delta + NA_KH - 1
            if 0 <= dy <= 2 * NA_KH - 2:
                nmat[dy, cs * NA_KROWS + d] = 1.0

    def body(n_ref, g_ref, o_ref):
        o_ref[0] = jnp.dot(n_ref[...], g_ref[0], precision=HI, preferred_element_type=F32)

    out = pl.pallas_call(
        body, name="rpb_reduce", grid=(NA_HEADS,),
        in_specs=[pl.BlockSpec((16, nmat.shape[1]), lambda h: (0, 0)),
                  pl.BlockSpec((1, nmat.shape[1], 128), lambda h: (h, 0, 0))],
        out_specs=pl.BlockSpec((1, 16, 128), lambda h: (h, 0, 0)),
        out_shape=jax.ShapeDtypeStruct((NA_HEADS, 16, 128), F32),
        compiler_params=_cparams(("parallel",)),
    )(jnp.asarray(nmat), g)
    return out[:, :2 * NA_KH - 1, :2 * NA_KW - 1]


def _halo_specs(tm, t, col, width=1024):
    nth = t // CONV_HALO
    per = tm // CONV_HALO
    return [pl.BlockSpec((tm, width), lambda i: (i, col)),
            pl.BlockSpec((CONV_HALO, width), lambda i: (jnp.maximum(i * per - 1, 0), col)),
            pl.BlockSpec((CONV_HALO, width), lambda i: (jnp.minimum((i + 1) * per, nth - 1), col))]


def _fill_ext(ext, cur_ref, prev_ref, next_ref, tm, nt):
    i = pl.program_id(0)
    hl = CONV_HALO
    ext[0:hl, :] = jnp.where(i == 0, 0.0, prev_ref[...].astype(F32))
    ext[hl:hl + tm, :] = cur_ref[...].astype(F32)
    ext[hl + tm:2 * hl + tm, :] = jnp.where(i == nt - 1, 0.0, next_ref[...].astype(F32))


CONV_HALO = 16
CONV_RC = 16
CONV_CB = 512


def _conv_chunks(tm):
    return [(slice(cb, cb + CONV_CB), slice(rb, rb + CONV_RC))
            for cb in range(0, 1024, CONV_CB) for rb in range(0, tm, CONV_RC)]


def _conv_fwd(proj, conv_w8, conv_b, tm):
    t = proj.shape[0]
    nt = t // tm

    def body(u_ref, up_ref, un_ref, w_ref, b_ref, pre_ref, act_ref, ext):
        _fill_ext(ext, u_ref, up_ref, un_ref, tm, nt)
        for cs, rs in _conv_chunks(tm):
            pre = b_ref[:, cs] + w_ref[0:1, cs] * ext[pl.ds(rs.start + CONV_HALO - 2, CONV_RC), cs]
            for j in range(1, CONV_W):
                pre = pre + w_ref[j:j + 1, cs] * ext[pl.ds(rs.start + CONV_HALO - 2 + j, CONV_RC), cs]
            pre_ref[rs, cs] = pre
            act_ref[rs, cs] = _silu(pre)

    full = pl.BlockSpec((tm, 1024), lambda i: (i, 0))
    o = jax.ShapeDtypeStruct((t, 1024), F32)
    return pl.pallas_call(
        body, name="conv_fwd", grid=(nt,),
        in_specs=_halo_specs(tm, t, 2) + [pl.BlockSpec((8, 1024), lambda i: (0, 0)), _row(1024)],
        out_specs=[full, full], out_shape=[o, o],
        scratch_shapes=[pltpu.VMEM((tm + 2 * CONV_HALO, 1024), F32)],
        compiler_params=_cparams(("parallel",)),
    )(proj, proj, proj, conv_w8, conv_b)


def _conv_bwd(dq, dk, pre, proj, conv_w8, tm):
    t = pre.shape[0]
    nt = t // tm

    def body(dq_ref, dqp_ref, dqn_ref, dk_ref, dkp_ref, dkn_ref, pre_ref, prep_ref, pren_ref,
             u_ref, up_ref, un_ref, w_ref, du_ref, gw_ref, gb_ref, extd, extu):
        i = pl.program_id(0)
        hl = CONV_HALO

        @pl.when(i == 0)
        def _():
            gw_ref[...] = jnp.zeros_like(gw_ref)
            gb_ref[...] = jnp.zeros_like(gb_ref)
        for rows, dqr, dkr, prr, edge in ((slice(0, hl), dqp_ref, dkp_ref, prep_ref, i == 0),
                                          (slice(hl, hl + tm), dq_ref, dk_ref, pre_ref, None),
                                          (slice(hl + tm, 2 * hl + tm), dqn_ref, dkn_ref, pren_ref, i == nt - 1)):
            ds = _dsilu(prr[...])
            dl = dqr[...] * ds[:, 0:ML_W]
            dr = dkr[...] * ds[:, ML_W:]
            if edge is not None:
                dl = jnp.where(edge, 0.0, dl)
                dr = jnp.where(edge, 0.0, dr)
            extd[rows, 0:ML_W] = dl
            extd[rows, ML_W:] = dr
        _fill_ext(extu, u_ref, up_ref, un_ref, tm, nt)
        gb_ref[...] += jnp.sum(extd[hl:hl + tm, :], axis=0, keepdims=True)
        gacc = None
        for cs, rs in _conv_chunks(tm):
            if rs.start == 0:
                gacc = [jnp.zeros((8, CONV_CB), F32) for _ in range(CONV_W)]
            du = w_ref[0:1, cs] * extd[pl.ds(rs.start + hl + 2, CONV_RC), cs]
            for j in range(1, CONV_W):
                du = du + w_ref[j:j + 1, cs] * extd[pl.ds(rs.start + hl + 2 - j, CONV_RC), cs]
            du_ref[rs, cs] = du.astype(BF16)
            dcur = extd[pl.ds(rs.start + hl, CONV_RC), cs]
            for j in range(CONV_W):
                prod = dcur * extu[pl.ds(rs.start + hl - 2 + j, CONV_RC), cs]
                gacc[j] = gacc[j] + functools.reduce(
                    jnp.add, [prod[k:k + 8] for k in range(0, CONV_RC, 8)])
            if rs.stop == tm:
                for j in range(CONV_W):
                    gw_ref[j:j + 1, cs] += jnp.sum(gacc[j], axis=0, keepdims=True)

    full = pl.BlockSpec((tm, 1024), lambda i: (i, 0))
    return pl.pallas_call(
        body, name="conv_bwd", grid=(nt,),
        in_specs=_halo_specs(tm, t, 0, ML_W) + _halo_specs(tm, t, 0, ML_W) + _halo_specs(tm, t, 0)
        + _halo_specs(tm, t, 2) + [pl.BlockSpec((8, 1024), lambda i: (0, 0))],
        out_specs=[full, pl.BlockSpec((8, 1024), lambda i: (0, 0)), _row(1024)],
        out_shape=[jax.ShapeDtypeStruct((t, 1024), BF16), jax.ShapeDtypeStruct((8, 1024), F32),
                   jax.ShapeDtypeStruct((1, 1024), F32)],
        scratch_shapes=[pltpu.VMEM((tm + 2 * CONV_HALO, 1024), F32), pltpu.VMEM((tm + 2 * CONV_HALO, 1024), F32)],
        compiler_params=_cparams(("arbitrary",)),
    )(dq, dq, dq, dk, dk, dk, pre, pre, pre, proj, proj, proj, conv_w8)


def _ml_consts(rev):
    iu = lax.broadcasted_iota(jnp.int32, (ML_CHUNK, ML_CHUNK), 0)
    js = lax.broadcasted_iota(jnp.int32, (ML_CHUNK, ML_CHUNK), 1)
    eye = iu == js
    le = iu <= js
    ge = iu >= js
    csum, csum_t, sees = (ge, le, ge) if rev else (le, ge, le)
    return eye, csum.astype(F32), csum_t.astype(F32), sees


def _col(row, eye):
    return jnp.sum(jnp.where(eye, row, 0.0), axis=1, keepdims=True)


def _rowof(col, eye):
    return jnp.sum(jnp.where(eye, col, 0.0), axis=0, keepdims=True)


def _row8(row):
    top = lax.broadcasted_iota(jnp.int32, (8, row.shape[1]), 0) == 0
    return jnp.where(top, row, jnp.zeros_like(row))


def _outer_rows(a_row, b_row_bf16):
    hi = a_row.astype(BF16)
    lo = (a_row - hi.astype(F32)).astype(BF16)
    r_a = lax.broadcasted_iota(jnp.int32, (8, a_row.shape[1]), 0)
    r_b = lax.broadcasted_iota(jnp.int32, (8, b_row_bf16.shape[1]), 0)
    lhs = jnp.where(r_a == 0, hi, jnp.where(r_a == 1, lo, jnp.zeros_like(hi)))
    rhs = jnp.where(r_b < 2, b_row_bf16, jnp.zeros_like(b_row_bf16))
    return _tn(lhs, rhs)


def _ml_gates(gi, gf, m0, csum, rev):
    lf = jax.nn.log_sigmoid(gf)
    b_rows = jnp.dot(lf, csum, precision=HI, preferred_element_type=F32)
    bl = jnp.sum(lf, axis=1, keepdims=True)
    a_rows = bl - b_rows + gi
    mloc = jnp.max(a_rows, axis=1, keepdims=True)
    order = list(range(ML_NB))[::-1] if rev else list(range(ML_NB))
    mp, mn, decay = {}, {}, {}
    m = m0
    for n in order:
        mp[n] = m
        m = jnp.maximum(bl[n:n + 1] + m, mloc[n:n + 1])
        mn[n] = m
    for n in order:
        decay[n] = jnp.exp(bl[n:n + 1] + mp[n] - mn[n])
    return b_rows, a_rows, gi - b_rows, mp, mn, decay, order


def _ml_load(q_ref, k_ref, v_ref, n):
    sl = slice(n * ML_CHUNK, (n + 1) * ML_CHUNK)
    qb = q_ref[sl, :].astype(BF16)
    kb = (k_ref[sl, :] * (ML_HEAD_DIM ** -0.5)).astype(BF16)
    vn = v_ref[sl, :].astype(F32)
    return sl, qb, kb, vn


def _ml_state_scan(q_ref, k_ref, v_ref, a_rows, mn, decay, order, c0, n0):
    ns = range(ML_NB)
    ld = [_ml_load(q_ref, k_ref, v_ref, n) for n in ns]
    vt = [ld[n][3].T for n in ns]
    w_row = [jnp.exp(a_rows[n:n + 1] - mn[n]) for n in ns]
    u = [_nn((vt[n] * w_row[n]).astype(BF16), ld[n][2]) for n in ns]
    nu = [_nn(_row8(w_row[n]).astype(BF16), ld[n][2])[0:1] for n in ns]
    cp, npv = {}, {}
    c, nv = c0, n0
    for n in order:
        cp[n], npv[n] = c, nv
        c = decay[n] * c + u[n]
        nv = decay[n] * nv + nu[n]
    return ld, vt, cp, npv, w_row, c, nv


def _ml_intra_all(ld, vt, b_rows, imb_rows, mp, cp, npv, sees, eye):
    ns = range(ML_NB)
    qk = [_nt(ld[n][2], ld[n][1]) for n in ns]
    cq = [_nt(cp[n].astype(BF16), ld[n][1]) for n in ns]
    qn = [_nt(_row8(npv[n]).astype(BF16), ld[n][1])[0:1] for n in ns]
    imb_col = [_col(imb_rows[n:n + 1], eye) for n in ns]
    dlog = [jnp.where(sees, b_rows[n:n + 1] + imb_col[n], NEG) for n in ns]
    m_inter = [b_rows[n:n + 1] + mp[n] for n in ns]
    m_t = [jnp.maximum(m_inter[n], jnp.max(dlog[n], axis=0, keepdims=True)) for n in ns]
    pm = [jnp.exp(dlog[n] - m_t[n]) for n in ns]
    inter = [jnp.exp(m_inter[n] - m_t[n]) for n in ns]
    floor = [jnp.exp(-m_t[n]) for n in ns]
    s = [qk[n] * pm[n] for n in ns]
    sv = [_nn(vt[n].astype(BF16), s[n].astype(BF16)) for n in ns]
    den = [jnp.sum(s[n], axis=0, keepdims=True) + inter[n] * qn[n] for n in ns]
    num = [sv[n] + inter[n] * cq[n] for n in ns]
    dn = [jnp.maximum(jnp.abs(den[n]), floor[n]) for n in ns]
    return [dict(pm=pm[n], s=s[n], inter=inter[n], cq=cq[n], qn=qn[n], num=num[n], den=den[n],
                 floor=floor[n], dn=dn[n]) for n in ns]


def _ml_specs(t, rev):
    nblk = t // ML_TB
    blk = (lambda g: nblk - 1 - g) if rev else (lambda g: g)
    hps = ML_HPS
    tile = lambda c0: pl.BlockSpec((ML_TB, 128 * hps), lambda hg, g, c0=c0: (blk(g), c0 // hps + hg))
    gate = pl.BlockSpec((hps, ML_NB, ML_CHUNK), lambda hg, g: (hg, blk(g), 0))
    cchk = pl.BlockSpec((hps, 1, 128, 128), lambda hg, g: (hg, blk(g), 0, 0))
    nmchk = pl.BlockSpec((hps, 1, 8, 128), lambda hg, g: (hg, blk(g), 0, 0))
    return nblk, blk, tile, gate, cchk, nmchk


def _ml_head_views(refs, hh):
    cols = slice(hh * ML_HEAD_DIM, (hh + 1) * ML_HEAD_DIM)
    return [r.at[:, cols] if len(r.shape) == 2 else r.at[hh] for r in refs]


def _ml_fwd(qk_act, proj, gi, gf, rev, name):
    t = qk_act.shape[0]
    nblk, _, tile, gate, cchk, nmchk = _ml_specs(t, rev)

    def body(*refs):
        for hh in range(ML_HPS):
            one_head(*_ml_head_views(refs, hh))

    def one_head(q_ref, k_ref, v_ref, gi_ref, gf_ref, h_ref, cchk_ref, nmchk_ref, c_ref, nm_ref):
        @pl.when(pl.program_id(1) == 0)
        def _():
            c_ref[...] = jnp.zeros_like(c_ref)
            nm_ref[...] = jnp.zeros_like(nm_ref)
        cchk_ref[0] = c_ref[...]
        nmchk_ref[0] = nm_ref[...]
        eye, csum, _, sees = _ml_consts(rev)
        b_rows, a_rows, imb_rows, mp, mn, decay, order = _ml_gates(
            gi_ref[...], gf_ref[...], nm_ref[1:2, 0:1], csum, rev)
        ld, vt, cp, npv, _, c, nv = _ml_state_scan(q_ref, k_ref, v_ref, a_rows, mn, decay, order,
                                                   c_ref[...], nm_ref[0:1, :])
        c_ref[...] = c
        nm_ref[0:1, :] = nv
        nm_ref[1:2, :] = jnp.broadcast_to(mn[order[-1]], (1, 128))
        rs = _ml_intra_all(ld, vt, b_rows, imb_rows, mp, cp, npv, sees, eye)
        ht = [rs[n]['num'] / rs[n]['dn'] for n in range(ML_NB)]
        for n in range(ML_NB):
            h_ref[n * ML_CHUNK:(n + 1) * ML_CHUNK, :] = ht[n].T

    return pl.pallas_call(
        body, name=name, grid=(ML_HEADS // ML_HPS, nblk),
        in_specs=[tile(0), tile(4), tile(24), gate, gate],
        out_specs=[tile(0), cchk, nmchk],
        out_shape=[jax.ShapeDtypeStruct((t, ML_W), F32),
                   jax.ShapeDtypeStruct((ML_HEADS, nblk, 128, 128), F32),
                   jax.ShapeDtypeStruct((ML_HEADS, nblk, 8, 128), F32)],
        scratch_shapes=[pltpu.VMEM((ML_HPS, 128, 128), F32), pltpu.VMEM((ML_HPS, 8, 128), F32)],
        compiler_params=_cparams(("parallel", "arbitrary")),
    )(qk_act, qk_act, proj, gi, gf)


def _ml_bwd(qk_act, proj, gi, gf, dh, cchk_a, nmchk_a, prev, rev, name):
    t = qk_act.shape[0]
    nblk, _, tile, gate, cchk, nmchk = _ml_specs(t, not rev)

    def body(*refs):
        for hh in range(ML_HPS):
            one_head(*_ml_head_views(refs, hh))

    def one_head(q_ref, k_ref, v_ref, gi_ref, gf_ref, dh_ref, cchk_ref, nmchk_ref, *rest):
        prev_refs = rest[:len(prev)]
        dq_ref, dk_ref, dv_ref, dgi_ref, dgf_ref, dc_ref, dn_ref, db_scr, dbl_scr, di_scr = rest[len(prev):]

        def plus_prev(val, which, rows):
            return val + prev_refs[which][rows, :] if prev else val

        @pl.when(pl.program_id(1) == 0)
        def _():
            dc_ref[...] = jnp.zeros_like(dc_ref)
            dn_ref[...] = jnp.zeros_like(dn_ref)
        eye, csum, csum_t, sees = _ml_consts(rev)
        gfv = gf_ref[...]
        b_rows, a_rows, imb_rows, mp, mn, decay, order = _ml_gates(
            gi_ref[...], gfv, nmchk_ref[0, 1:2, 0:1], csum, rev)
        ld, vt, cp, npv, w_row, _, _ = _ml_state_scan(q_ref, k_ref, v_ref, a_rows, mn, decay, order,
                                                      cchk_ref[0], nmchk_ref[0, 0:1, :])
        ns = range(ML_NB)
        rs = _ml_intra_all(ld, vt, b_rows, imb_rows, mp, cp, npv, sees, eye)
        sls = [ld[n][0] for n in ns]
        qbs = [ld[n][1] for n in ns]
        kbs = [ld[n][2] for n in ns]
        vbs = [ld[n][3].astype(BF16) for n in ns]
        rdn = [1.0 / rs[n]['dn'] for n in ns]
        dnum = [dh_ref[sls[n], :].T * rdn[n] for n in ns]
        hsum = [jnp.sum(dnum[n] * rs[n]['num'], axis=0, keepdims=True) for n in ns]
        dden = [jnp.where(jnp.abs(rs[n]['den']) > rs[n]['floor'],
                          -hsum[n] * rdn[n] * jnp.sign(rs[n]['den']), 0.0) for n in ns]
        dnb = [dnum[n].astype(BF16) for n in ns]
        dsf = [_nn(vbs[n], dnb[n]) + dden[n] for n in ns]
        dv0 = [_nt(rs[n]['s'].astype(BF16), dnb[n]) for n in ns]
        gb = [(dsf[n] * rs[n]['pm']).astype(BF16) for n in ns]
        cpb = [cp[n].astype(BF16) for n in ns]
        idd = [rs[n]['inter'] * dden[n] for n in ns]
        idn = [(rs[n]['inter'] * dnum[n]).astype(BF16) for n in ns]
        dqa = [_tn(gb[n], kbs[n]) for n in ns]
        dqc = [_tn(idn[n], cpb[n]) for n in ns]
        dqn = [_outer_rows(idd[n], npv[n].astype(BF16)) for n in ns]
        dk0 = [_nn(gb[n], qbs[n]) for n in ns]
        xs = [_nn(idn[n], qbs[n]) for n in ns]
        for n in ns:
            dq_ref[sls[n], :] = plus_prev(dqa[n] + dqc[n] + dqn[n], 0, sls[n])
        rr = [dsf[n] * rs[n]['s'] for n in ns]
        dinter = [jnp.sum(dnum[n] * rs[n]['cq'], axis=0, keepdims=True) + dden[n] * rs[n]['qn'] for n in ns]
        dbt = [jnp.sum(rr[n], axis=0, keepdims=True) + dinter[n] * rs[n]['inter'] for n in ns]
        dimb = [jnp.sum(rr[n], axis=1, keepdims=True) for n in ns]
        xns = [_nn(_row8(idd[n]).astype(BF16), qbs[n])[0:1] for n in ns]
        dcn, dnn = {}, {}
        dc, dn = dc_ref[...], dn_ref[0:1, :]
        for n in order[::-1]:
            dcn[n], dnn[n] = dc, dn
            dc = decay[n] * dc + xs[n]
            dn = decay[n] * dn + xns[n]
        dc_ref[...] = dc
        dn_ref[0:1, :] = dn
        kscale = ML_HEAD_DIM ** -0.5
        dcb = [dcn[n].astype(BF16) for n in ns]
        z = [_nn(vbs[n], dcb[n]) for n in ns]
        kd = [_nt(kbs[n], dcb[n]) for n in ns]
        ddecay = [jnp.sum(jnp.sum(dcn[n] * cp[n], axis=1, keepdims=True), axis=0, keepdims=True)
                  + jnp.sum(dnn[n] * npv[n], axis=1, keepdims=True) for n in ns]
        zd = [z[n] + dnn[n] for n in ns]
        dw = [jnp.sum(zd[n] * kbs[n].astype(F32), axis=1, keepdims=True) for n in ns]
        wcol = [_col(w_row[n], eye) for n in ns]
        for n in ns:
            dv_ref[sls[n], :] = plus_prev(dv0[n] + wcol[n] * kd[n], 2, sls[n]).astype(dv_ref.dtype)
            dk_ref[sls[n], :] = plus_prev((dk0[n] + wcol[n] * zd[n]) * kscale, 1, sls[n])
        da = [dw[n] * wcol[n] for n in ns]
        dbl = [jnp.sum(da[n], axis=0, keepdims=True) + ddecay[n] * decay[n] for n in ns]
        key_row = [_rowof(dimb[n] + da[n], eye) for n in ns]
        for n in ns:
            db_scr[n:n + 1, :] = dbt[n] - key_row[n]
            di_scr[n:n + 1, :] = key_row[n]
            dbl_scr[n:n + 1, :] = jnp.broadcast_to(dbl[n], (1, ML_CHUNK))
        dlf = jnp.dot(db_scr[...], csum_t, precision=HI, preferred_element_type=F32) + dbl_scr[...]
        dgf_ref[...] = dlf * jax.nn.sigmoid(-gfv)
        dgi_ref[...] = di_scr[...]

    nc = t // ML_CHUNK
    o512 = jax.ShapeDtypeStruct((t, ML_W), F32)
    og = jax.ShapeDtypeStruct((ML_HEADS, nc, ML_CHUNK), F32)
    return pl.pallas_call(
        body, name=name, grid=(ML_HEADS // ML_HPS, nblk),
        in_specs=[tile(0), tile(4), tile(24), gate, gate, tile(0), cchk, nmchk] + [tile(0)] * len(prev),
        out_specs=[tile(0), tile(0), tile(0), gate, gate],
        out_shape=[o512, o512, jax.ShapeDtypeStruct((t, ML_W), BF16 if prev else F32), og, og],
        scratch_shapes=[pltpu.VMEM((ML_HPS, 128, 128), F32), pltpu.VMEM((ML_HPS, 8, 128), F32)]
        + [pltpu.VMEM((ML_HPS, ML_NB, ML_CHUNK), F32)] * 3,
        compiler_params=_cparams(("parallel", "arbitrary")),
    )(qk_act, qk_act, proj, gi, gf, dh, cchk_a, nmchk_a, *prev)


def _gate_rows(gates16, t):
    g = gates16.reshape(t // ML_CHUNK, ML_CHUNK, 4, ML_HEADS).transpose(2, 3, 0, 1)
    return g[0], g[1], g[2], g[3]


def _gate_cols(dgi_f, dgf_f, dgi_b, dgf_b, t):
    g = jnp.stack([dgi_f, dgf_f, dgi_b, dgf_b]).transpose(2, 3, 0, 1).reshape(t, 4 * ML_HEADS)
    return jnp.pad(g, ((0, 0), (0, 128 - 4 * ML_HEADS)))


def _local_step(x, target, shift, scale, gate, norm_w, w_in_t, b_in_p, conv_w8, conv_b, rpb,
                ml_norm_w, w_out_b, final_norm_w):
    t = x.shape[0]
    rows = t // GRID_W
    tm = 512
    proj, gates = _in_proj(x, norm_w, scale, shift, w_in_t, b_in_p)
    tbl = _na_bias_table(rpb, rows)
    o_na, lse_na = _na_fwd(proj, tbl)
    pre, qk_act = _conv_fwd(proj, conv_w8, conv_b, 2 * tm)
    gi_f, gf_f, gi_b, gf_b = _gate_rows(gates[:, :4 * ML_HEADS], t)
    h_f, cchk_f, nmchk_f = _ml_fwd(qk_act, proj, gi_f, gf_f, False, "ml_fwd_f")
    h_b, cchk_b, nmchk_b = _ml_fwd(qk_act, proj, gi_b, gf_b, True, "ml_fwd_b")
    (loss, dres, d_ona, d_naz, dhs, d_o, d_z, dgate, g_fnw, g_mlnw, g_w_out) = _tail(
        o_na, proj, h_f, h_b, x, target, gate, ml_norm_w, final_norm_w, w_out_b)
    dq_na, dk_na, dv_na, rpbacc = _na_bwd(proj, tbl, d_ona, o_na, lse_na)
    g_rpb = _rpb_reduce(rpbacc, rows)
    dq_f, dk_f, dv_f, dgi_f, dgf_f = _ml_bwd(qk_act, proj, gi_f, gf_f, dhs, cchk_f, nmchk_f, (),
                                             False, "ml_bwd_f")
    dq_ml, dk_ml, dv_ml, dgi_b, dgf_b = _ml_bwd(qk_act, proj, gi_b, gf_b, dhs, cchk_b, nmchk_b, (dq_f, dk_f, dv_f),
                                                True, "ml_bwd_b")
    du, g_conv_w, g_conv_b = _conv_bwd(dq_ml, dk_ml, pre, proj, conv_w8, tm)
    dgates = _gate_cols(dgi_f, dgf_f, dgi_b, dgf_b, t)
    grad_x, g_w_in, g_b_in, dscale, dshift, g_nw = _in_bwd(
        [dq_na, dk_na, dv_na, d_naz, du, dv_ml, d_o, d_z, dgates], x, dres, w_in_t, norm_w, scale, shift)
    dmod = jnp.concatenate([dshift, dscale, dgate], axis=1)
    return (loss, grad_x, dmod, g_nw, g_w_in, g_b_in, g_conv_w, g_conv_b, g_rpb, g_mlnw, g_w_out, g_fnw)


MESH = pl.DeviceIdType.MESH
N_DEV = 8
ANY = pl.BlockSpec(memory_space=pl.ANY)
WHOLE_VMEM = pl.BlockSpec(memory_space=pltpu.VMEM)


def _allgather8(blocks, name):
    na = len(blocks)

    def body(*refs):
        x_refs = refs[:na]
        out_refs = refs[na:2 * na]
        send_sems, recv_sems, local_sems = refs[2 * na:]
        x, y, c = lax.axis_index("x"), lax.axis_index("y"), lax.axis_index("c")
        me, sibling = (x, y, c), (x, y, 1 - c)
        chips = [(1 - x, y), (x, 1 - y), (1 - x, 1 - y)]

        def rows(a, px, py, pc):
            return out_refs[a].at[4 * px + 2 * py + pc]

        def copy(a, k, block, to, src=None):
            return pltpu.make_async_remote_copy(
                src_ref=rows(a, *block) if src is None else src, dst_ref=rows(a, *block),
                send_sem=send_sems.at[a, k], recv_sem=recv_sems.at[a, k],
                device_id=to, device_id_type=MESH)

        mine, first, passed = [], [], []
        for a in range(na):
            cp = pltpu.make_async_copy(x_refs[a], rows(a, *me), local_sems.at[a])
            cp.start()
            mine.append(cp)
            first.append(copy(a, 0, me, sibling, src=x_refs[a]))
            first += [copy(a, 1 + j, me, (*chip, c), src=x_refs[a]) for j, chip in enumerate(chips)]
        for cp in first:
            cp.start()
        for a in range(na):
            for j, chip in enumerate(chips):
                copy(a, 1 + j, (*chip, c), me).wait_recv()
                fwd = copy(a, 4 + j, (*chip, c), sibling)
                fwd.start()
                passed.append(fwd)
        for a in range(na):
            copy(a, 0, sibling, me).wait_recv()
            for j, chip in enumerate(chips):
                copy(a, 4 + j, (*chip, 1 - c), me).wait_recv()
        for cp in first + passed:
            cp.wait_send()
        for cp in mine:
            cp.wait()

    return pl.pallas_call(
        body, name=name,
        out_shape=[jax.ShapeDtypeStruct((N_DEV,) + b.shape, b.dtype) for b in blocks],
        in_specs=[WHOLE_VMEM] * na, out_specs=[WHOLE_VMEM] * na,
        scratch_shapes=[pltpu.SemaphoreType.DMA((na, 7)), pltpu.SemaphoreType.DMA((na, 7)),
                        pltpu.SemaphoreType.DMA((na,))],
        compiler_params=pltpu.CompilerParams(vmem_limit_bytes=VMEM_LIMIT),
    )(*blocks)


def _pair_exchange(arrs, name):
    na = len(arrs)

    def body(*refs):
        in_refs = refs[:na]
        out_refs = refs[na:2 * na]
        send_sems, recv_sems = refs[2 * na:]
        sibling = (lax.axis_index("x"), lax.axis_index("y"), 1 - lax.axis_index("c"))
        copies = [pltpu.make_async_remote_copy(
            src_ref=in_refs[a], dst_ref=out_refs[a], send_sem=send_sems.at[a], recv_sem=recv_sems.at[a],
            device_id=sibling, device_id_type=MESH) for a in range(na)]
        for cp in copies:
            cp.start()
        for cp in copies:
            cp.wait()

    return pl.pallas_call(
        body, name=name,
        out_shape=[jax.ShapeDtypeStruct(a.shape, a.dtype) for a in arrs],
        in_specs=[ANY] * na, out_specs=[ANY] * na,
        scratch_shapes=[pltpu.SemaphoreType.DMA((na,)), pltpu.SemaphoreType.DMA((na,))],
    )(*arrs)


def _chip_exchange(arrs, name):
    na = len(arrs)

    def body(*refs):
        in_refs = refs[:na]
        out_refs = refs[na:2 * na]
        send_sems, recv_sems, local_sems = refs[2 * na:]
        x, y, c = lax.axis_index("x"), lax.axis_index("y"), lax.axis_index("c")
        my_chip = 2 * x + y
        chips = [(1 - x, y), (x, 1 - y), (1 - x, 1 - y)]
        local, remote = [], []
        for a in range(na):
            cp = pltpu.make_async_copy(in_refs[a].at[my_chip], out_refs[a].at[my_chip], local_sems.at[a])
            cp.start()
            local.append(cp)
            for j, (px, py) in enumerate(chips):
                cp = pltpu.make_async_remote_copy(
                    src_ref=in_refs[a].at[2 * px + py], dst_ref=out_refs[a].at[my_chip],
                    send_sem=send_sems.at[a, j], recv_sem=recv_sems.at[a, j],
                    device_id=(px, py, c), device_id_type=MESH)
                cp.start()
                remote.append(cp)
        for cp in remote:
            cp.wait()
        for cp in local:
            cp.wait()

    return pl.pallas_call(
        body, name=name,
        out_shape=[jax.ShapeDtypeStruct(a.shape, a.dtype) for a in arrs],
        in_specs=[ANY] * na, out_specs=[ANY] * na,
        scratch_shapes=[pltpu.SemaphoreType.DMA((na, 3)), pltpu.SemaphoreType.DMA((na, 3)),
                        pltpu.SemaphoreType.DMA((na,))],
    )(*arrs)


def _rows_tile(r):
    for cand in (512, 256, 128, 64, 32, 16, 8):
        if r % cand == 0:
            return cand
    return r


def _add2(a, b, name, out_dtype):
    s, r, n = a.shape
    tr = _rows_tile(r)

    def body(a_ref, b_ref, o_ref):
        o_ref[...] = (a_ref[...] + b_ref[...]).astype(out_dtype)

    spec = pl.BlockSpec((1, tr, n), lambda i, j: (i, j, 0))
    return pl.pallas_call(
        body, name=name, grid=(s, r // tr), in_specs=[spec, spec], out_specs=spec,
        out_shape=jax.ShapeDtypeStruct(a.shape, out_dtype),
        compiler_params=_cparams(("parallel", "parallel")),
    )(a, b)


def _sum_slabs(a, name):
    s, r, n = a.shape
    tr = _rows_tile(r)

    def body(a_ref, o_ref):
        acc = a_ref[0].astype(F32)
        for k in range(1, s):
            acc = acc + a_ref[k].astype(F32)
        o_ref[...] = acc

    return pl.pallas_call(
        body, name=name, grid=(r // tr,),
        in_specs=[pl.BlockSpec((s, tr, n), lambda i: (0, i, 0))],
        out_specs=pl.BlockSpec((tr, n), lambda i: (i, 0)),
        out_shape=jax.ShapeDtypeStruct((r, n), F32),
        compiler_params=_cparams(("parallel",)),
    )(a)


ADAMW_WHOLE = 64 * 1024


def _adamw(w, g, m, v, name):
    r, n = w.shape
    if r * n <= ADAMW_WHOLE:
        blk, grid, imap = (r, n), (1,), (lambda i: (0, 0))
    elif r % 8 == 0:
        blk, grid, imap = (_rows_tile(r), n), (r // _rows_tile(r),), (lambda i: (i, 0))
    else:
        blk, grid, imap = (r, 128), (n // 128,), (lambda i: (0, i))
    c1 = 1.0 / (1.0 - ADAM_B1 ** ADAM_STEP)
    c2 = 1.0 / (1.0 - ADAM_B2 ** ADAM_STEP)

    def body(w_ref, g_ref, m_ref, v_ref, d_ref, nm_ref, nv_ref):
        gv = g_ref[...]
        nm = ADAM_B1 * m_ref[...] + (1.0 - ADAM_B1) * gv
        nv = ADAM_B2 * v_ref[...] + (1.0 - ADAM_B2) * (gv * gv)
        nm_ref[...] = nm
        nv_ref[...] = nv
        d_ref[...] = -ADAM_LR * ((nm * c1) / (jnp.sqrt(nv * c2) + ADAM_EPS) + ADAM_WD * w_ref[...])

    spec = pl.BlockSpec(blk, imap)
    o = jax.ShapeDtypeStruct((r, n), F32)
    return pl.pallas_call(
        body, name=name, grid=grid, in_specs=[spec] * 4, out_specs=[spec] * 3, out_shape=[o, o, o],
        compiler_params=_cparams(("parallel",)),
    )(w, g, m, v)


def _mod_fwd(c_all, w_ada_s, b_ada_s):
    def body(c_ref, w_ref, b_ref, o_ref):
        o_ref[...] = jnp.dot(_silu(c_ref[...]), w_ref[...], precision=HI, preferred_element_type=F32) + b_ref[...]

    return pl.pallas_call(
        body, name="mod_fwd", out_shape=jax.ShapeDtypeStruct((c_all.shape[0], w_ada_s.shape[1]), F32),
        in_specs=[WHOLE_VMEM] * 3, out_specs=WHOLE_VMEM,
        compiler_params=pltpu.CompilerParams(vmem_limit_bytes=VMEM_LIMIT),
    )(c_all, w_ada_s, b_ada_s)


def _wada_grad(c_all, dmod_s):
    def body(c_ref, d_ref, o_ref):
        o_ref[...] = lax.dot_general(_silu(c_ref[...]), d_ref[...], (((0,), (0,)), ((), ())),
                                     precision=HI, preferred_element_type=F32)

    return pl.pallas_call(
        body, name="w_ada_grad", out_shape=jax.ShapeDtypeStruct((c_all.shape[1], dmod_s.shape[1]), F32),
        in_specs=[WHOLE_VMEM] * 2, out_specs=WHOLE_VMEM,
        compiler_params=pltpu.CompilerParams(vmem_limit_bytes=VMEM_LIMIT),
    )(c_all, dmod_s)


SMALL_ROWS = 24


def _pad_rows(v, nrows):
    v = v.reshape(-1)
    return jnp.pad(v, (0, nrows * 1024 - v.shape[0])).reshape(nrows, 1024)


def _pack_small(b_ada, norm_w, b_in, conv_w_full, conv_b, rpb, ml_norm_w, final_norm_w, last):
    parts = [_pad_rows(b_ada, 3), _pad_rows(norm_w, 1), _pad_rows(b_in, 5), _pad_rows(conv_w_full, 5),
             _pad_rows(conv_b, 1), _pad_rows(rpb, 4), _pad_rows(ml_norm_w, 1), _pad_rows(final_norm_w, 1),
             _pad_rows(last, 3)]
    return jnp.concatenate(parts, axis=0)


def _unpack_small(p):
    return dict(b_ada=p[0:3].reshape(1, 3072), norm_w=p[3:4], b_in=p[4:9].reshape(-1)[:IN_W].reshape(1, IN_W),
                conv_w=p[9:14], conv_b=p[14:15],
                rpb=p[15:19].reshape(-1)[:NA_HEADS * 15 * 31].reshape(1, NA_HEADS, 15, 31),
                ml_norm_w=p[19:20, :ML_W], final_norm_w=p[20], last=p[21])


def kernel(x, c, w_ada, b_ada, norm_w, w_in, b_in, conv_w, conv_b, rpb, ml_norm_w, w_out, final_norm_w, loss_target, m_w_ada, m_b_ada, m_norm_w, m_w_in, m_b_in, m_conv_w, m_conv_b, m_rpb, m_ml_norm_w, m_w_out, m_final_norm_w, v_w_ada, v_b_ada, v_norm_w, v_w_in, v_b_in, v_conv_w, v_conv_b, v_rpb, v_ml_norm_w, v_w_out, v_final_norm_w):
    xi, yi, ci = lax.axis_index("x"), lax.axis_index("y"), lax.axis_index("c")
    chip = 2 * xi + yi
    dev = 2 * chip + ci
    t = x.shape[1]
    ada_n = w_ada.shape[2]
    in_n = w_in.shape[2]
    out_r = w_out.shape[1]

    c_blk = jnp.pad(c, ((0, 7), (0, 0)))
    w_in_t, m_w_in_t, v_w_in_t = w_in[0].T, m_w_in[0].T, v_w_in[0].T
    in_h = in_n // 2
    w_in_half = lax.dynamic_slice_in_dim(w_in_t, ci * in_h, in_h, axis=0).astype(BF16)
    w_out_half = lax.dynamic_slice_in_dim(w_out[0], ci * (out_r // 2), out_r // 2, axis=0).astype(BF16)
    conv_blk = jnp.pad(conv_w[0], ((0, 3), (0, 0)))
    c_g, conv_g, w_in_g, w_out_g = _allgather8([c_blk, conv_blk, w_in_half, w_out_half], "gather_c_weights")
    c_all = c_g[:, 0]
    w_out_g = w_out_g.reshape(D_MODEL, D_MODEL)
    b_ada_s = lax.dynamic_slice_in_dim(b_ada, chip * ada_n, ada_n, axis=1)
    mod_s = _mod_fwd(c_all, w_ada[0], b_ada_s)
    (mod_g,) = _allgather8([mod_s], "gather_mod")
    mod_mine = lax.dynamic_index_in_dim(mod_g, dev, axis=1, keepdims=False)
    mod = mod_mine[0::2].reshape(1, 3 * D_MODEL)
    shift, scale, gate = mod[:, :D_MODEL], mod[:, D_MODEL:2 * D_MODEL], mod[:, 2 * D_MODEL:]

    w_in_tp = jnp.pad(w_in_g.reshape(IN_W, D_MODEL), ((0, IN_PAD - IN_W), (0, 0)))
    b_in_p = jnp.pad(b_in, ((0, 0), (0, IN_PAD - IN_W)))
    conv_w8 = conv_g.reshape(4, 2, 8, conv_w.shape[2])[:, 0].transpose(1, 0, 2).reshape(8, D_MODEL)

    (loss, grad_x, dmod, g_nw, g_w_in, g_b_in, g_conv_w, g_conv_b, g_rpb, g_mlnw, g_w_out, g_fnw) = _local_step(
        x[0], loss_target[0], shift, scale, gate, norm_w, w_in_tp, b_in_p, conv_w8, conv_b, rpb[0],
        ml_norm_w, w_out_g, final_norm_w.reshape(1, D_MODEL))

    g_in_t = g_w_in

    def halves(a, per_chip, h):
        return jnp.stack([lax.dynamic_slice_in_dim(a, k * per_chip + h * (per_chip // 2), per_chip // 2, axis=0)
                          for k in range(4)])

    ri, ro = _pair_exchange([halves(g_in_t, in_n, 1 - ci), halves(g_w_out, out_r, 1 - ci)], "rs_pair")
    pi = _add2(halves(g_in_t, in_n, ci), ri, "rs_pair_add_in", BF16)
    po = _add2(halves(g_w_out, out_r, ci), ro, "rs_pair_add_out", BF16)
    qi, qo = _chip_exchange([pi, po], "rs_chips")
    si = _sum_slabs(qi, "rs_sum_in")
    so = _sum_slabs(qo, "rs_sum_out")
    ti, to = _pair_exchange([si, so], "rs_share")
    g_w_in_s = jnp.where(ci == 0, jnp.concatenate([si, ti], axis=0), jnp.concatenate([ti, si], axis=0))
    g_w_out_s = jnp.where(ci == 0, jnp.concatenate([so, to], axis=0), jnp.concatenate([to, so], axis=0))

    small = _pack_small(dmod, g_nw, g_b_in[:, :IN_W], g_conv_w[:CONV_W], g_conv_b, g_rpb, g_mlnw, g_fnw,
                        jnp.pad(loss, ((0, 0), (0, 1024 - 128))))
    (small_g,) = _allgather8([small], "gather_small")
    small_sum = _sum_slabs(small_g, "small_sum")
    gs = _unpack_small(small_sum)
    dmod_all = small_g[:, 0:3].reshape(N_DEV, 3 * D_MODEL)
    g_w_ada_s = _wada_grad(c_all, lax.dynamic_slice_in_dim(dmod_all, chip * ada_n, ada_n, axis=1))
    g_conv_w_s = lax.dynamic_slice_in_dim(gs['conv_w'], chip * conv_w.shape[2], conv_w.shape[2], axis=1)
    loss_total = gs['last'][0]

    small_names = ('b_ada', 'norm_w', 'b_in', 'conv_b', 'rpb', 'ml_norm_w', 'final_norm_w')
    small_w = (b_ada, norm_w, b_in, conv_b, rpb, ml_norm_w, final_norm_w)
    small_m = (m_b_ada, m_norm_w, m_b_in, m_conv_b, m_rpb, m_ml_norm_w, m_final_norm_w)
    small_v = (v_b_ada, v_norm_w, v_b_in, v_conv_b, v_rpb, v_ml_norm_w, v_final_norm_w)
    ds_, nms, nvs = {}, {}, {}
    for nm_, w_, m_, v_ in zip(small_names, small_w, small_m, small_v):
        two_d = (NA_HEADS, w_.size // NA_HEADS) if nm_ == 'rpb' else (1, w_.size)
        outs = _adamw(w_.reshape(two_d), gs[nm_].reshape(two_d), m_.reshape(two_d), v_.reshape(two_d),
                      "adamw_" + nm_)
        ds_[nm_], nms[nm_], nvs[nm_] = [o.reshape(w_.shape) for o in outs]
    d_ada, nm_ada, nv_ada = _adamw(w_ada[0], g_w_ada_s, m_w_ada[0], v_w_ada[0], "adamw_w_ada")
    d_in, nm_in, nv_in = _adamw(w_in_t, g_w_in_s, m_w_in_t, v_w_in_t, "adamw_w_in")
    d_out, nm_out, nv_out = _adamw(w_out[0], g_w_out_s, m_w_out[0], v_w_out[0], "adamw_w_out")
    d_cw, nm_cw, nv_cw = _adamw(conv_w[0], g_conv_w_s, m_conv_w[0], v_conv_w[0], "adamw_conv_w")

    def group(big_ada, big_in, big_out, cw, sm):
        return (big_ada[None], sm['b_ada'], sm['norm_w'], big_in.T[None], sm['b_in'], cw[None], sm['conv_b'],
                sm['rpb'], sm['ml_norm_w'], big_out[None], sm['final_norm_w'])

    return ((loss_total, grad_x[None])
            + group(g_w_ada_s, g_w_in_s, g_w_out_s, g_conv_w_s, gs)
            + group(d_ada, d_in, d_out, d_cw, ds_)
            + group(nm_ada, nm_in, nm_out, nm_cw, nms)
            + group(nv_ada, nv_in, nv_out, nv_cw, nvs))
```

```python
import functools

import numpy as np
import jax
import jax.numpy as jnp
from jax import lax
from jax.experimental import pallas as pl
from jax.experimental.pallas import tpu as pltpu

F32 = jnp.float32
BF16 = jnp.bfloat16
HI = lax.Precision.HIGHEST

D_MODEL = 1024
GRID_W = 64
NA_W = 512
NA_HEAD_DIM = 64
NA_HEADS = 8
NA_KH = 8
NA_KW = 16
ML_W = 512
ML_HEADS = 4
ML_HEAD_DIM = 128
ML_CHUNK = 128
CONV_W = 5
EPS = 1e-6
IN_W = 4 * NA_W + 5 * ML_W + 4 * ML_HEADS
IN_MAIN = 4 * NA_W + 5 * ML_W
IN_PAD = IN_MAIN + 128
NEG = -1e30

ADAM_LR = 0.001
ADAM_B1 = 0.9
ADAM_B2 = 0.999
ADAM_EPS = 1e-08
ADAM_WD = 0.01
ADAM_STEP = 10

NA_QROWS = 8
NA_KROWS = 16
NA_QT = NA_QROWS * GRID_W
NA_KT = NA_KROWS * GRID_W
NA_KCH = 256
NA_RC = 32
ML_NB = 32
ML_TB = ML_NB * ML_CHUNK
ML_HPS = 1
ML_GROUP = 8
ML_GROUP_BWD = ML_NB

VMEM_LIMIT = 56 * 1024 * 1024
IN_BWD_VMEM_LIMIT = 60 * 1024 * 1024


def _cparams(sem, vmem=VMEM_LIMIT):
    return pltpu.CompilerParams(dimension_semantics=sem, vmem_limit_bytes=vmem)


def _silu(x):
    return x * jax.nn.sigmoid(x)


def _dsilu(x):
    s = jax.nn.sigmoid(x)
    return s * (1.0 + x * (1.0 - s))


def _dot(a, b, dims):
    return lax.dot_general(a, b, (dims, ((), ())), preferred_element_type=F32)


def _nn(a, b):
    return _dot(a, b, ((1,), (0,)))


def _nt(a, b):
    return _dot(a, b, ((1,), (1,)))


def _tn(a, b):
    return _dot(a, b, ((0,), (0,)))


def _row(n):
    return pl.BlockSpec((1, n), lambda i: (0, 0))


def _modulated_norm(xv, nw, sc, sh):
    r = lax.rsqrt(jnp.mean(xv * xv, axis=-1, keepdims=True) + EPS)
    xn = xv * r
    return xn * nw * (1.0 + sc) + sh, xn, r


IN_TN = 768


def _in_proj(x, norm_w, scale, shift, w_in_t, b_in_p):
    t, d = x.shape
    tm = 2048
    gcol = IN_MAIN // 128

    def body(x_ref, nw_ref, sc_ref, sh_ref, w_ref, b_ref, wg_ref, bg_ref, proj_ref, g_ref, h_scr):
        @pl.when(pl.program_id(1) == 0)
        def _():
            h, _, _ = _modulated_norm(x_ref[...], nw_ref[...], sc_ref[...], sh_ref[...])
            h_scr[...] = h.astype(BF16)
            g_ref[...] = _nt(h_scr[...], wg_ref[...]) + bg_ref[...]
        proj_ref[...] = (_nt(h_scr[...], w_ref[...]) + b_ref[...]).astype(BF16)

    row = lambda n: pl.BlockSpec((1, n), lambda i, j: (0, 0))
    return pl.pallas_call(
        body, name="in_proj", grid=(t // tm, IN_MAIN // IN_TN),
        in_specs=[pl.BlockSpec((tm, d), lambda i, j: (i, 0)), row(d), row(d), row(d),
                  pl.BlockSpec((IN_TN, d), lambda i, j: (j, 0)), pl.BlockSpec((1, IN_TN), lambda i, j: (0, j)),
                  pl.BlockSpec((128, d), lambda i, j: (gcol, 0)), pl.BlockSpec((1, 128), lambda i, j: (0, gcol))],
        out_specs=[pl.BlockSpec((tm, IN_TN), lambda i, j: (i, j)), pl.BlockSpec((tm, 128), lambda i, j: (i, 0))],
        out_shape=[jax.ShapeDtypeStruct((t, IN_MAIN), BF16), jax.ShapeDtypeStruct((t, 128), F32)],
        scratch_shapes=[pltpu.VMEM((tm, d), BF16)],
        compiler_params=_cparams(("parallel", "arbitrary")),
    )(x, norm_w, scale, shift, w_in_t, b_in_p, w_in_t, b_in_p)


def _ml_norm_parts(hs, o, z, nw):
    outs = []
    for hh in range(ML_HEADS):
        sl = slice(hh * ML_HEAD_DIM, (hh + 1) * ML_HEAD_DIM)
        so = jax.nn.sigmoid(o[:, sl])
        hm = hs[:, sl] * so
        mu = jnp.mean(hm, axis=-1, keepdims=True)
        cen = hm - mu
        var = jnp.mean(cen * cen, axis=-1, keepdims=True)
        rs = lax.rsqrt(var + EPS)
        outs.append((sl, cen * rs, rs, so))
    return outs


def _tail(o_na, proj, h_f, h_b, x, target, gate, ml_norm_w, fnw, w_out_b):
    t, d = x.shape
    tm = 512

    def body(ona_ref, naz_ref, hf_ref, hb_ref, o_ref, z_ref, x_ref, tg_ref, g_ref, nw_ref, fw_ref, w_ref,
             loss_ref, dres_ref, dona_ref, dnaz_ref, dhs_ref, do_ref, dz_ref, dgate_ref, gfw_ref, gnw_ref,
             gwo_ref, mix_scr):
        @pl.when(pl.program_id(0) == 0)
        def _():
            for r in (loss_ref, dgate_ref, gfw_ref, gnw_ref, gwo_ref):
                r[...] = jnp.zeros_like(r)
        naz = naz_ref[...].astype(F32)
        ona = ona_ref[...]
        sg_naz = jax.nn.sigmoid(naz)
        sna = naz * sg_naz
        mix_scr[:, 0:NA_W] = (ona * sna).astype(BF16)
        hs = hf_ref[...] + hb_ref[...]
        z = z_ref[...].astype(F32)
        ov = o_ref[...].astype(F32)
        parts = _ml_norm_parts(hs, ov, z, nw_ref[...])
        sgz = [jax.nn.sigmoid(z[:, sl]) for sl, _, _, _ in parts]
        for (sl, xn, _, _), sg in zip(parts, sgz):
            mix_scr[:, NA_W + sl.start:NA_W + sl.stop] = (xn * nw_ref[:, sl] * (z[:, sl] * sg)).astype(BF16)
        mixb = mix_scr[...]
        wv = w_ref[...]
        yv = _nn(mixb, wv)
        gate_v = g_ref[...]
        hres = x_ref[...] + gate_v * yv
        r = lax.rsqrt(jnp.mean(hres * hres, axis=-1, keepdims=True) + EPS)
        xnf = hres * r
        err = xnf * fw_ref[...] - tg_ref[...]
        loss_ref[...] += 0.5 * jnp.sum(jnp.sum(err * err, axis=-1, keepdims=True) * (1.0 / d), axis=0, keepdims=True)
        dout = err * (1.0 / d)
        gfw_ref[...] += jnp.sum(dout * xnf, axis=0, keepdims=True)
        dxn = dout * fw_ref[...]
        dres = r * (dxn - xnf * jnp.mean(dxn * xnf, axis=-1, keepdims=True))
        dres_ref[...] = dres
        dgate_ref[...] += jnp.sum(dres * yv, axis=0, keepdims=True)
        dyb = (dres * gate_v).astype(BF16)
        gwo_ref[...] += _tn(mixb, dyb)
        dmix = _nt(dyb, wv)
        dna = dmix[:, 0:NA_W]
        dona_ref[...] = dna * sna
        dnaz_ref[...] = (dna * ona * (sg_naz * (1.0 + naz * (1.0 - sg_naz)))).astype(BF16)
        for (sl, xn, rs, so), sg in zip(parts, sgz):
            dyv = dmix[:, NA_W + sl.start:NA_W + sl.stop]
            zz = z[:, sl]
            sz = zz * sg
            w = nw_ref[:, sl]
            dz_ref[:, sl] = (dyv * xn * w * (sg * (1.0 + zz * (1.0 - sg)))).astype(BF16)
            gnw_ref[:, sl] += jnp.sum(dyv * xn * sz, axis=0, keepdims=True)
            dxm = dyv * w * sz
            dhm = rs * (dxm - jnp.mean(dxm, axis=-1, keepdims=True)
                        - xn * jnp.mean(dxm * xn, axis=-1, keepdims=True))
            dhs_ref[:, sl] = dhm * so
            do_ref[:, sl] = (dhm * hs[:, sl] * so * (1.0 - so)).astype(BF16)

    blk = lambda c: pl.BlockSpec((tm, 512), lambda i, c=c: (i, c))
    full = pl.BlockSpec((tm, d), lambda i: (i, 0))
    o512 = jax.ShapeDtypeStruct((t, 512), F32)
    b512 = jax.ShapeDtypeStruct((t, 512), BF16)
    whole = pl.BlockSpec((d, d), lambda i: (0, 0))
    return pl.pallas_call(
        body, name="tail", grid=(t // tm,),
        in_specs=[blk(0), blk(3), blk(0), blk(0), blk(7), blk(8), full, full, _row(d), _row(ML_W), _row(d), whole],
        out_specs=[pl.BlockSpec((1, 128), lambda i: (0, 0)), full] + [blk(0)] * 5
        + [_row(d), _row(d), _row(ML_W), whole],
        out_shape=[jax.ShapeDtypeStruct((1, 128), F32), jax.ShapeDtypeStruct((t, d), F32),
                   o512, b512, o512, b512, b512]
        + [jax.ShapeDtypeStruct((1, d), F32), jax.ShapeDtypeStruct((1, d), F32),
           jax.ShapeDtypeStruct((1, ML_W), F32), jax.ShapeDtypeStruct((d, d), F32)],
        scratch_shapes=[pltpu.VMEM((tm, d), BF16)],
        compiler_params=_cparams(("arbitrary",)),
    )(o_na, proj, h_f, h_b, proj, proj, x, target, gate, ml_norm_w, fnw, w_out_b)


def _in_bwd(pieces, x, dres, w_in_t, norm_w, scale, shift):
    t, d = x.shape
    tm = 512
    nt = t // tm
    widths = [p.shape[1] for p in pieces]
    offs = [sum(widths[:k]) for k in range(len(widths))]
    assert sum(widths) == IN_PAD
    npc = len(pieces)

    def body(*refs):
        p_refs = refs[:npc]
        (x_ref, dres_ref, w_hbm, nw_ref, sc_ref, sh_ref,
         gx_ref, gw_hbm, gb_ref, dsc_ref, dsh_ref, gnw_ref, w_vmem, acc, stage, sem) = refs[npc:]
        i = pl.program_id(0)

        @pl.when(i == 0)
        def _():
            cp = pltpu.make_async_copy(w_hbm, w_vmem, sem.at[0])
            cp.start()
            acc[...] = jnp.zeros_like(acc)
            gb_ref[...] = jnp.zeros_like(gb_ref)
            dsc_ref[...] = jnp.zeros_like(dsc_ref)
            dsh_ref[...] = jnp.zeros_like(dsh_ref)
            gnw_ref[...] = jnp.zeros_like(gnw_ref)
            cp.wait()

        nw = nw_ref[...]
        s1 = 1.0 + sc_ref[...]
        h, xn, r = _modulated_norm(x_ref[...], nw, sc_ref[...], sh_ref[...])
        hb = h.astype(BF16)
        dhv = jnp.zeros((tm, d), F32)
        for p_ref, c0, w in zip(p_refs, offs, widths):
            pt = p_ref[...]
            pb = pt.astype(BF16)
            dhv = dhv + _nn(pb, w_vmem[c0:c0 + w, :])
            acc[:, c0:c0 + w] += _tn(hb, pb)
            gb_ref[:, c0:c0 + w] += jnp.sum(pt.astype(F32), axis=0, keepdims=True)
        dsh_ref[...] += jnp.sum(dhv, axis=0, keepdims=True)
        dsc_ref[...] += jnp.sum(dhv * xn * nw, axis=0, keepdims=True)
        gnw_ref[...] += jnp.sum(dhv * xn * s1, axis=0, keepdims=True)
        dxn = dhv * nw * s1
        gx_ref[...] = dres_ref[...] + r * (dxn - xn * jnp.mean(dxn * xn, axis=-1, keepdims=True))

        @pl.when(i == nt - 1)
        def _():
            copies = []
            for blk in range(IN_PAD // 128):
                slot = blk % 2
                if blk >= 2:
                    copies[blk - 2].wait()
                stage[slot] = acc[:, blk * 128:(blk + 1) * 128].T
                cp = pltpu.make_async_copy(stage.at[slot], gw_hbm.at[pl.ds(blk * 128, 128), :], sem.at[1 + slot])
                cp.start()
                copies.append(cp)
            copies[-2].wait()
            copies[-1].wait()

    full = pl.BlockSpec((tm, d), lambda i: (i, 0))
    return pl.pallas_call(
        body, name="in_bwd", grid=(nt,),
        in_specs=[pl.BlockSpec((tm, w), lambda i: (i, 0)) for w in widths]
        + [full, full, pl.BlockSpec(memory_space=pl.ANY), _row(d), _row(d), _row(d)],
        out_specs=[full, pl.BlockSpec(memory_space=pl.ANY), _row(IN_PAD), _row(d), _row(d), _row(d)],
        out_shape=[jax.ShapeDtypeStruct((t, d), F32), jax.ShapeDtypeStruct((IN_PAD, d), F32),
                   jax.ShapeDtypeStruct((1, IN_PAD), F32)] + [jax.ShapeDtypeStruct((1, d), F32)] * 3,
        scratch_shapes=[pltpu.VMEM((IN_PAD, d), BF16), pltpu.VMEM((d, IN_PAD), F32),
                        pltpu.VMEM((2, 128, d), F32), pltpu.SemaphoreType.DMA((3,))],
        compiler_params=_cparams(("arbitrary",), IN_BWD_VMEM_LIMIT),
    )(*pieces, x, dres, w_in_t, norm_w, scale, shift)


def _na_static(rows):
    cases = [(0, 0), (NA_QROWS, NA_QROWS - 4), (rows - NA_QROWS, rows - NA_KROWS)]
    dy = np.zeros((3, NA_QROWS, NA_KROWS), np.int32)
    rv = np.zeros((3, NA_QROWS, NA_KROWS), bool)
    for cs, (r0, kr0) in enumerate(cases):
        for i in range(NA_QROWS):
            for j in range(NA_KROWS):
                r, kr = r0 + i, kr0 + j
                rs = min(max(r - NA_KH // 2, 0), rows - NA_KH)
                rv[cs, i, j] = rs <= kr <= rs + NA_KH - 1
                dy[cs, i, j] = min(max(kr - r + NA_KH - 1, 0), 2 * NA_KH - 2)
    cq = np.arange(GRID_W)[:, None]
    ck = np.arange(GRID_W)[None, :]
    cs0 = np.clip(cq - NA_KW // 2, 0, GRID_W - NA_KW)
    cv = (ck >= cs0) & (ck < cs0 + NA_KW)
    dx = np.clip(ck - cq, -(NA_KW - 1), NA_KW - 1) + NA_KW - 1
    return dy, rv, dx.astype(np.int32), cv


def _na_bias_table(rpb, rows):
    _, _, dx, cv = _na_static(rows)
    ndy = 2 * NA_KH - 1
    onehot = (dx.reshape(1, -1) == np.arange(2 * NA_KW - 1)[:, None]).astype(np.float32)
    rpx = jnp.dot(rpb.reshape(NA_HEADS * ndy, 2 * NA_KW - 1), jnp.asarray(onehot), precision=HI)
    rpx = jnp.where(cv[None, None], rpx.reshape(NA_HEADS, ndy, GRID_W, GRID_W), NEG)
    neg = jnp.full((NA_HEADS, 1, GRID_W, GRID_W), NEG, F32)
    rpx = jnp.concatenate([rpx, neg], axis=1)
    nxt = jnp.concatenate([rpx[:, 1:], neg], axis=1)
    negs = jnp.broadcast_to(neg, rpx.shape)
    pairs = jnp.concatenate([jnp.concatenate([rpx, nxt], axis=3), jnp.concatenate([rpx, negs], axis=3),
                             jnp.concatenate([negs, rpx], axis=3)], axis=1)
    npair = pairs.shape[1]

    def body(m_ref, o_ref):
        cs = pl.program_id(1)
        r0 = jnp.where(cs == 0, 0, jnp.where(cs == 1, NA_QROWS, rows - NA_QROWS))
        kr0 = jnp.where(cs == 0, 0, jnp.where(cs == 1, NA_QROWS - NA_KH // 2, rows - NA_KROWS))
        for i in range(NA_QROWS):
            r = r0 + i
            rs = jnp.clip(r - NA_KH // 2, 0, rows - NA_KH)
            for jp in range(NA_KROWS // 2):
                kl = kr0 + 2 * jp
                vl = (kl >= rs) & (kl <= rs + NA_KH - 1)
                vr = (kl + 1 >= rs) & (kl + 1 <= rs + NA_KH - 1)
                dyl = jnp.clip(kl - r + NA_KH - 1, 0, ndy)
                dyr = jnp.clip(kl + 1 - r + NA_KH - 1, 0, ndy)
                idx = jnp.where(vl & vr, dyl, jnp.where(vl, 16 + dyl, jnp.where(vr, 32 + dyr, 16 + ndy)))
                o_ref[0, 0, i * GRID_W:(i + 1) * GRID_W, jp * 128:(jp + 1) * 128] = m_ref[0, idx]

    return pl.pallas_call(
        body, name="na_bias_table", grid=(NA_HEADS, 3),
        in_specs=[pl.BlockSpec((1, npair, GRID_W, 128), lambda h, cs: (h, 0, 0, 0))],
        out_specs=pl.BlockSpec((1, 1, NA_QT, NA_KT), lambda h, cs: (h, cs, 0, 0)),
        out_shape=jax.ShapeDtypeStruct((NA_HEADS, 3, NA_QT, NA_KT), F32),
        compiler_params=_cparams(("parallel", "parallel")),
    )(pairs)


def _na_specs(t):
    nb = t // NA_QT
    nkb = t // NA_KCH
    npieces = NA_KT // NA_KCH

    def kb0(b):
        return jnp.clip(b * (NA_QT // NA_KCH) - 1, 0, nkb - npieces)

    def case(b):
        return jnp.where(b == 0, 0, jnp.where(b == nb - 1, 2, 1))

    q_spec = pl.BlockSpec((NA_QT, 128), lambda p, b: (b, p))
    k_specs = [pl.BlockSpec((NA_KCH, 128), lambda p, b, i=i: (kb0(b) + i, 4 + p)) for i in range(npieces)]
    v_specs = [pl.BlockSpec((NA_KCH, 128), lambda p, b, i=i: (kb0(b) + i, 8 + p)) for i in range(npieces)]
    tbl_spec = pl.BlockSpec((2, 1, NA_QT, NA_KT), lambda p, b: (p, case(b), 0, 0))
    io_spec = pl.BlockSpec((NA_QT, 128), lambda p, b: (b, p))
    return nb, npieces, kb0, case, q_spec, k_specs, v_specs, tbl_spec, io_spec


NA_HALF = NA_QT // 2
NA_COMBOS_ALL = tuple((i, 0, NA_QT) for i in range(NA_KT // NA_KCH))
NA_COMBOS_INNER = ((0, 0, NA_HALF),) + tuple((i, 0, NA_QT) for i in range(1, NA_KT // NA_KCH - 1)) \
    + ((NA_KT // NA_KCH - 1, NA_HALF, NA_QT),)


def _na_place(val, r0, r1):
    if (r0, r1) == (0, NA_QT):
        return val
    z = jnp.zeros((NA_HALF, val.shape[1]), val.dtype)
    return jnp.concatenate([val, z] if r0 == 0 else [z, val], axis=0)


def _na_fwd(proj, tbl):
    t = proj.shape[0]
    nb, npieces, _, _, q_spec, k_specs, v_specs, tbl_spec, io_spec = _na_specs(t)
    lse_spec = pl.BlockSpec((1, NA_QT, 2), lambda p, b: (p, b, 0))

    def body(*refs):
        q_ref = refs[0]
        k_refs = refs[1:1 + npieces]
        v_refs = refs[1 + npieces:1 + 2 * npieces]
        tbl_ref, o_ref, lse_ref = refs[1 + 2 * npieces:]
        b = pl.program_id(1)

        def compute(combos):
            lane = lax.broadcasted_iota(jnp.int32, (1, 128), 1)
            qv = q_ref[...].astype(F32) * (NA_HEAD_DIM ** -0.5)
            ks = [r[...].astype(BF16) for r in k_refs]
            vs = [r[...].astype(BF16) for r in v_refs]
            hs = range(2)
            msk = [(lane // NA_HEAD_DIM) == hh for hh in hs]
            qh = [jnp.where(msk[hh], qv, 0.0).astype(BF16) for hh in hs]
            s = [[_nt(qh[hh][r0:r1], ks[i]) + tbl_ref[hh, 0, r0:r1, i * NA_KCH:(i + 1) * NA_KCH]
                  for i, r0, r1 in combos] for hh in hs]
            for h0 in (0, NA_HALF):
                rows = slice(h0, h0 + NA_HALF)
                cover = [(c, i, h0 - r0) for c, (i, r0, r1) in enumerate(combos) if r0 <= h0 < r1]
                part = [[s[hh][c][off:off + NA_HALF] for c, _, off in cover] for hh in hs]
                m = [functools.reduce(jnp.maximum, [jnp.max(v, axis=1, keepdims=True) for v in part[hh]]) for hh in hs]
                p = [[jnp.exp(v - m[hh]) for v in part[hh]] for hh in hs]
                l = [functools.reduce(jnp.add, [jnp.sum(v, axis=1, keepdims=True) for v in p[hh]]) for hh in hs]
                o = [functools.reduce(jnp.add, [_nn(p[hh][k].astype(BF16), vs[i]) for k, (_, i, _) in enumerate(cover)])
                     for hh in hs]
                for hh in hs:
                    lse_ref[0, rows, hh:hh + 1] = m[hh] + jnp.log(l[hh])
                o_ref[rows, :] = jnp.where(msk[0], o[0] / l[0], o[1] / l[1])

        inner = (b > 0) & (b < nb - 1)
        pl.when(inner)(lambda: compute(NA_COMBOS_INNER))
        pl.when(jnp.logical_not(inner))(lambda: compute(NA_COMBOS_ALL))

    return pl.pallas_call(
        body, name="na_fwd", grid=(4, nb),
        in_specs=[q_spec] + k_specs + v_specs + [tbl_spec],
        out_specs=[io_spec, lse_spec],
        out_shape=[jax.ShapeDtypeStruct((t, NA_W), F32), jax.ShapeDtypeStruct((4, t, 2), F32)],
        compiler_params=_cparams(("parallel", "arbitrary")),
    )(*([proj] * (1 + 2 * npieces)), tbl)


def _na_bwd(proj, tbl, d_o, o_na, lse):
    t = proj.shape[0]
    nb, npieces, kb0, case, q_spec, k_specs, v_specs, tbl_spec, io_spec = _na_specs(t)

    def body(*refs):
        q_ref = refs[0]
        k_refs = refs[1:1 + npieces]
        v_refs = refs[1 + npieces:1 + 2 * npieces]
        (tbl_ref, do_ref, o_ref, lse_ref, dq_ref, dk_hbm, dv_hbm, rpb_ref,
         dk_acc, dv_acc, dk_out, dv_out, s_scr, dp_scr, dsb_scr, pnb_scr, sem) = refs[1 + 2 * npieces:]
        p_id = pl.program_id(0)
        b = pl.program_id(1)

        @pl.when(b == 0)
        def _():
            dk_acc[...] = jnp.zeros_like(dk_acc)
            dv_acc[...] = jnp.zeros_like(dv_acc)

        @pl.when((b == 0) | (b == 1) | (b == nb - 1))
        def _():
            rpb_ref[...] = jnp.zeros_like(rpb_ref)

        def compute(combos):
            lane = lax.broadcasted_iota(jnp.int32, (1, 128), 1)
            scale = NA_HEAD_DIM ** -0.5
            qv = q_ref[...].astype(F32) * scale
            ks = [r[...].astype(BF16) for r in k_refs]
            vs = [r[...].astype(BF16) for r in v_refs]
            dov = do_ref[...]
            ov = o_ref[...]
            tok0 = kb0(b) * NA_KCH
            hs = range(2)
            msk = [(lane // NA_HEAD_DIM) == hh for hh in hs]
            qh = [jnp.where(msk[hh], qv, 0.0).astype(BF16) for hh in hs]
            doh = [jnp.where(msk[hh], dov, 0.0) for hh in hs]
            dohb = [doh[hh].astype(BF16) for hh in hs]
            dd = [jnp.sum(doh[hh] * ov, axis=1, keepdims=True) for hh in hs]
            for hh in hs:
                for c, (i, q0, q1) in enumerate(combos):
                    slot = (hh * len(combos) + c) % 2
                    cols = slice(i * NA_KCH, (i + 1) * NA_KCH)
                    s_scr[slot, 0:q1 - q0] = _nt(qh[hh][q0:q1], ks[i])
                    dp_scr[slot, 0:q1 - q0] = _nt(dohb[hh][q0:q1], vs[i])
                    for r0 in range(q0, q1, NA_RC):
                        rows = slice(r0, r0 + NA_RC)
                        loc = slice(r0 - q0, r0 - q0 + NA_RC)
                        p = jnp.exp(s_scr[slot, loc, :] + tbl_ref[hh, 0, rows, cols] - lse_ref[0, rows, hh:hh + 1])
                        d = p * (dp_scr[slot, loc, :] - dd[hh][rows])
                        pnb_scr[hh, rows, cols] = p.astype(BF16)
                        dsb_scr[hh, rows, cols] = d.astype(BF16)
                done = {(i, q0) for i, q0, _ in combos} | {(i, NA_HALF) for i, q0, q1 in combos if q1 - q0 == NA_QT}
                for i in range(npieces):
                    for q0 in (0, NA_HALF):
                        if (i, q0) not in done:
                            dsb_scr[hh, q0:q0 + NA_HALF, i * NA_KCH:(i + 1) * NA_KCH] = jnp.zeros(
                                (NA_HALF, NA_KCH), BF16)
            dqh = [functools.reduce(jnp.add, [_na_place(_nn(dsb_scr[hh, q0:q1, i * NA_KCH:(i + 1) * NA_KCH], ks[i]),
                                                        q0, q1) for i, q0, q1 in combos]) for hh in hs]
            dq_ref[...] = (jnp.where(msk[0], dqh[0], dqh[1]) * scale).astype(BF16)
            for i in range(npieces):
                rows = pl.ds(pl.multiple_of(tok0 + i * NA_KCH, NA_KCH), NA_KCH)
                cols = slice(i * NA_KCH, (i + 1) * NA_KCH)
                q0, q1 = [(a, e) for j, a, e in combos if j == i][0]
                dk_acc[rows, :] += (_tn(dsb_scr[0, q0:q1, cols], qh[0][q0:q1])
                                    + _tn(dsb_scr[1, q0:q1, cols], qh[1][q0:q1]))
                dv_acc[rows, :] += (_tn(pnb_scr[0, q0:q1, cols], dohb[0][q0:q1])
                                    + _tn(pnb_scr[1, q0:q1, cols], dohb[1][q0:q1]))
            for hh in hs:
                acc = dsb_scr[hh, 0:GRID_W, :].astype(F32)
                for i in range(1, NA_QROWS):
                    acc = acc + pltpu.roll(dsb_scr[hh, i * GRID_W:(i + 1) * GRID_W, :].astype(F32),
                                           NA_KT - i * GRID_W, 1)
                rpb_ref[0, 0, hh] += acc

        inner = (b > 0) & (b < nb - 1)
        pl.when(inner)(lambda: compute(NA_COMBOS_INNER))
        pl.when(jnp.logical_not(inner))(lambda: compute(NA_COMBOS_ALL))

        @pl.when(b == nb - 1)
        def _():
            def copies(pair):
                cols = pl.ds(pl.multiple_of(pair * 128, 128), 128)
                return (pltpu.make_async_copy(dk_out, dk_hbm.at[:, cols], sem.at[0]),
                        pltpu.make_async_copy(dv_out, dv_hbm.at[:, cols], sem.at[1]))

            @pl.when(p_id > 0)
            def _():
                for cp in copies(p_id - 1):
                    cp.wait()
            dk_out[...] = dk_acc[...].astype(BF16)
            dv_out[...] = dv_acc[...].astype(BF16)
            for cp in copies(p_id):
                cp.start()

            @pl.when(p_id == NA_HEADS // 2 - 1)
            def _():
                for cp in copies(p_id):
                    cp.wait()

    o512 = jax.ShapeDtypeStruct((t, NA_W), BF16)
    return pl.pallas_call(
        body, name="na_bwd", grid=(4, nb),
        in_specs=[q_spec] + k_specs + v_specs + [tbl_spec, io_spec, io_spec,
                                                 pl.BlockSpec((1, NA_QT, 2), lambda p, b: (p, b, 0))],
        out_specs=[io_spec, pl.BlockSpec(memory_space=pl.ANY), pl.BlockSpec(memory_space=pl.ANY),
                   pl.BlockSpec((1, 1, 2, GRID_W, NA_KT), lambda p, b: (p, case(b), 0, 0, 0))],
        out_shape=[o512, o512, o512, jax.ShapeDtypeStruct((4, 3, 2, GRID_W, NA_KT), F32)],
        scratch_shapes=[pltpu.VMEM((t, 128), F32), pltpu.VMEM((t, 128), F32),
                        pltpu.VMEM((t, 128), BF16), pltpu.VMEM((t, 128), BF16),
                        pltpu.VMEM((2, NA_QT, NA_KCH), F32), pltpu.VMEM((2, NA_QT, NA_KCH), F32),
                        pltpu.VMEM((2, NA_QT, NA_KT), BF16), pltpu.VMEM((2, NA_QT, NA_KT), BF16),
                        pltpu.SemaphoreType.DMA((2,))],
        compiler_params=_cparams(("arbitrary", "arbitrary")),
    )(*([proj] * (1 + 2 * npieces)), tbl, d_o, o_na, lse)


def _rpb_reduce(rpbacc, rows):
    nacc = 4 * 3 * 2

    def shift_body(a_ref, o_ref):
        acc = a_ref[0, 0:1, :]
        for cq in range(1, GRID_W):
            acc = acc + pltpu.roll(a_ref[0, cq:cq + 1, :], NA_KT - cq, 1)
        o_ref[0] = jnp.broadcast_to(acc, (8, NA_KT))

    vec = pl.pallas_call(
        shift_body, name="rpb_shift", grid=(nacc,),
        in_specs=[pl.BlockSpec((1, GRID_W, NA_KT), lambda a: (a, 0, 0))],
        out_specs=pl.BlockSpec((1, 8, NA_KT), lambda a: (a, 0, 0)),
        out_shape=jax.ShapeDtypeStruct((nacc, 8, NA_KT), F32),
        compiler_params=_cparams(("parallel",)),
    )(rpbacc.reshape(nacc, GRID_W, NA_KT))
    a = vec[:, 0].reshape(4, 3, 2, NA_KT).transpose(0, 2, 1, 3).reshape(NA_HEADS, 3, NA_KT)
    if rows // NA_QROWS < 3:
        a = a.at[:, 1].set(0.0)
    dd = np.arange(NA_KROWS)[:, None]
    dxo = np.arange(-(NA_KW - 1), NA_KW)[None, :]
    idx = ((dd * GRID_W + dxo) % NA_KT).reshape(-1)
    g = a[..., idx].reshape(NA_HEADS, 3 * NA_KROWS, 2 * NA_KW - 1)
    g = jnp.pad(g, ((0, 0), (0, 0), (0, 128 - (2 * NA_KW - 1))))
    nmat = np.zeros((16, 3 * NA_KROWS), np.float32)
    for cs, delta in enumerate((0, -(NA_KH // 2), -(NA_KROWS - NA_QROWS))):
        for d in range(NA_KROWS):
            jmi = d - NA_KROWS if (cs == 0 and d > NA_KH - 1) else d
            dy = jmi + delta + NA_KH - 1
            if 0 <= dy <= 2 * NA_KH - 2:
                nmat[dy, cs * NA_KROWS + d] = 1.0

    def body(n_ref, g_ref, o_ref):
        o_ref[0] = jnp.dot(n_ref[...], g_ref[0], precision=HI, preferred_element_type=F32)

    out = pl.pallas_call(
        body, name="rpb_reduce", grid=(NA_HEADS,),
        in_specs=[pl.BlockSpec((16, nmat.shape[1]), lambda h: (0, 0)),
                  pl.BlockSpec((1, nmat.shape[1], 128), lambda h: (h, 0, 0))],
        out_specs=pl.BlockSpec((1, 16, 128), lambda h: (h, 0, 0)),
        out_shape=jax.ShapeDtypeStruct((NA_HEADS, 16, 128), F32),
        compiler_params=_cparams(("parallel",)),
    )(jnp.asarray(nmat), g)
    return out[:, :2 * NA_KH - 1, :2 * NA_KW - 1]


def _halo_specs(tm, t, col, width=1024):
    nth = t // CONV_HALO
    per = tm // CONV_HALO
    return [pl.BlockSpec((tm, width), lambda i: (i, col)),
            pl.BlockSpec((CONV_HALO, width), lambda i: (jnp.maximum(i * per - 1, 0), col)),
            pl.BlockSpec((CONV_HALO, width), lambda i: (jnp.minimum((i + 1) * per, nth - 1), col))]


def _fill_ext(ext, cur_ref, prev_ref, next_ref, tm, nt):
    i = pl.program_id(0)
    hl = CONV_HALO
    ext[0:hl, :] = jnp.where(i == 0, 0.0, prev_ref[...].astype(F32))
    ext[hl:hl + tm, :] = cur_ref[...].astype(F32)
    ext[hl + tm:2 * hl + tm, :] = jnp.where(i == nt - 1, 0.0, next_ref[...].astype(F32))


CONV_HALO = 16
CONV_RC = 16
CONV_CB = 512


def _conv_chunks(tm):
    return [(slice(cb, cb + CONV_CB), slice(rb, rb + CONV_RC))
            for cb in range(0, 1024, CONV_CB) for rb in range(0, tm, CONV_RC)]


def _conv_fwd(proj, conv_w8, conv_b, tm):
    t = proj.shape[0]
    nt = t // tm

    def body(u_ref, up_ref, un_ref, w_ref, b_ref, pre_ref, act_ref, ext):
        _fill_ext(ext, u_ref, up_ref, un_ref, tm, nt)
        for cs, rs in _conv_chunks(tm):
            pre = b_ref[:, cs] + w_ref[0:1, cs] * ext[pl.ds(rs.start + CONV_HALO - 2, CONV_RC), cs]
            for j in range(1, CONV_W):
                pre = pre + w_ref[j:j + 1, cs] * ext[pl.ds(rs.start + CONV_HALO - 2 + j, CONV_RC), cs]
            pre_ref[rs, cs] = pre
            act_ref[rs, cs] = _silu(pre)

    full = pl.BlockSpec((tm, 1024), lambda i: (i, 0))
    o = jax.ShapeDtypeStruct((t, 1024), F32)
    return pl.pallas_call(
        body, name="conv_fwd", grid=(nt,),
        in_specs=_halo_specs(tm, t, 2) + [pl.BlockSpec((8, 1024), lambda i: (0, 0)), _row(1024)],
        out_specs=[full, full], out_shape=[o, o],
        scratch_shapes=[pltpu.VMEM((tm + 2 * CONV_HALO, 1024), F32)],
        compiler_params=_cparams(("parallel",)),
    )(proj, proj, proj, conv_w8, conv_b)


def _conv_bwd(dq, dk, pre, proj, conv_w8, tm):
    t = pre.shape[0]
    nt = t // tm

    def body(dq_ref, dqp_ref, dqn_ref, dk_ref, dkp_ref, dkn_ref, pre_ref, prep_ref, pren_ref,
             u_ref, up_ref, un_ref, w_ref, du_ref, gw_ref, gb_ref, extd, extu):
        i = pl.program_id(0)
        hl = CONV_HALO

        @pl.when(i == 0)
        def _():
            gw_ref[...] = jnp.zeros_like(gw_ref)
            gb_ref[...] = jnp.zeros_like(gb_ref)
        for rows, dqr, dkr, prr, edge in ((slice(0, hl), dqp_ref, dkp_ref, prep_ref, i == 0),
                                          (slice(hl, hl + tm), dq_ref, dk_ref, pre_ref, None),
                                          (slice(hl + tm, 2 * hl + tm), dqn_ref, dkn_ref, pren_ref, i == nt - 1)):
            ds = _dsilu(prr[...])
            dl = dqr[...] * ds[:, 0:ML_W]
            dr = dkr[...] * ds[:, ML_W:]
            if edge is not None:
                dl = jnp.where(edge, 0.0, dl)
                dr = jnp.where(edge, 0.0, dr)
            extd[rows, 0:ML_W] = dl
            extd[rows, ML_W:] = dr
        _fill_ext(extu, u_ref, up_ref, un_ref, tm, nt)
        gb_ref[...] += jnp.sum(extd[hl:hl + tm, :], axis=0, keepdims=True)
        gacc = None
        for cs, rs in _conv_chunks(tm):
            if rs.start == 0:
                gacc = [jnp.zeros((8, CONV_CB), F32) for _ in range(CONV_W)]
            du = w_ref[0:1, cs] * extd[pl.ds(rs.start + hl + 2, CONV_RC), cs]
            for j in range(1, CONV_W):
                du = du + w_ref[j:j + 1, cs] * extd[pl.ds(rs.start + hl + 2 - j, CONV_RC), cs]
            du_ref[rs, cs] = du.astype(BF16)
            dcur = extd[pl.ds(rs.start + hl, CONV_RC), cs]
            for j in range(CONV_W):
                prod = dcur * extu[pl.ds(rs.start + hl - 2 + j, CONV_RC), cs]
                gacc[j] = gacc[j] + functools.reduce(
                    jnp.add, [prod[k:k + 8] for k in range(0, CONV_RC, 8)])
            if rs.stop == tm:
                for j in range(CONV_W):
                    gw_ref[j:j + 1, cs] += jnp.sum(gacc[j], axis=0, keepdims=True)

    full = pl.BlockSpec((tm, 1024), lambda i: (i, 0))
    return pl.pallas_call(
        body, name="conv_bwd", grid=(nt,),
        in_specs=_halo_specs(tm, t, 0, ML_W) + _halo_specs(tm, t, 0, ML_W) + _halo_specs(tm, t, 0)
        + _halo_specs(tm, t, 2) + [pl.BlockSpec((8, 1024), lambda i: (0, 0))],
        out_specs=[full, pl.BlockSpec((8, 1024), lambda i: (0, 0)), _row(1024)],
        out_shape=[jax.ShapeDtypeStruct((t, 1024), BF16), jax.ShapeDtypeStruct((8, 1024), F32),
                   jax.ShapeDtypeStruct((1, 1024), F32)],
        scratch_shapes=[pltpu.VMEM((tm + 2 * CONV_HALO, 1024), F32), pltpu.VMEM((tm + 2 * CONV_HALO, 1024), F32)],
        compiler_params=_cparams(("arbitrary",)),
    )(dq, dq, dq, dk, dk, dk, pre, pre, pre, proj, proj, proj, conv_w8)


def _ml_consts(rev):
    iu = lax.broadcasted_iota(jnp.int32, (ML_CHUNK, ML_CHUNK), 0)
    js = lax.broadcasted_iota(jnp.int32, (ML_CHUNK, ML_CHUNK), 1)
    eye = iu == js
    le = iu <= js
    ge = iu >= js
    csum, csum_t, sees = (ge, le, ge) if rev else (le, ge, le)
    return eye, csum.astype(F32), csum_t.astype(F32), sees


def _col(row, eye):
    return jnp.sum(jnp.where(eye, row, 0.0), axis=1, keepdims=True)


def _rowof(col, eye):
    return jnp.sum(jnp.where(eye, col, 0.0), axis=0, keepdims=True)


def _row8(row):
    top = lax.broadcasted_iota(jnp.int32, (8, row.shape[1]), 0) == 0
    return jnp.where(top, row, jnp.zeros_like(row))


def _outer_rows(a_row, b_row_bf16):
    hi = a_row.astype(BF16)
    lo = (a_row - hi.astype(F32)).astype(BF16)
    r_a = lax.broadcasted_iota(jnp.int32, (8, a_row.shape[1]), 0)
    r_b = lax.broadcasted_iota(jnp.int32, (8, b_row_bf16.shape[1]), 0)
    lhs = jnp.where(r_a == 0, hi, jnp.where(r_a == 1, lo, jnp.zeros_like(hi)))
    rhs = jnp.where(r_b < 2, b_row_bf16, jnp.zeros_like(b_row_bf16))
    return _tn(lhs, rhs)


def _ml_gates(gi, gf, m0, csum, rev):
    lf = jax.nn.log_sigmoid(gf)
    b_rows = jnp.dot(lf, csum, precision=HI, preferred_element_type=F32)
    bl = jnp.sum(lf, axis=1, keepdims=True)
    a_rows = bl - b_rows + gi
    mloc = jnp.max(a_rows, axis=1, keepdims=True)
    order = list(range(ML_NB))[::-1] if rev else list(range(ML_NB))
    mp, mn, decay = {}, {}, {}
    m = m0
    for n in order:
        mp[n] = m
        m = jnp.maximum(bl[n:n + 1] + m, mloc[n:n + 1])
        mn[n] = m
    for n in order:
        decay[n] = jnp.exp(bl[n:n + 1] + mp[n] - mn[n])
    return b_rows, a_rows, gi - b_rows, mp, mn, decay, order


def _ml_load(q_ref, k_ref, v_ref, n):
    sl = slice(n * ML_CHUNK, (n + 1) * ML_CHUNK)
    qb = q_ref[sl, :].astype(BF16)
    kb = (k_ref[sl, :] * (ML_HEAD_DIM ** -0.5)).astype(BF16)
    vn = v_ref[sl, :].astype(F32)
    return sl, qb, kb, vn


def _ml_state_scan(q_ref, k_ref, v_ref, a_rows, mn, decay, order, c0, n0):
    ns = range(ML_NB)
    ld = [_ml_load(q_ref, k_ref, v_ref, n) for n in ns]
    vt = [ld[n][3].T for n in ns]
    w_row = [jnp.exp(a_rows[n:n + 1] - mn[n]) for n in ns]
    u = [_nn((vt[n] * w_row[n]).astype(BF16), ld[n][2]) for n in ns]
    nu = [_nn(_row8(w_row[n]).astype(BF16), ld[n][2])[0:1] for n in ns]
    cp, npv = {}, {}
    c, nv = c0, n0
    for n in order:
        cp[n], npv[n] = c, nv
        c = decay[n] * c + u[n]
        nv = decay[n] * nv + nu[n]
    return ld, vt, cp, npv, w_row, c, nv


def _ml_intra_all(ld, vt, b_rows, imb_rows, mp, cp, npv, sees, eye, ns):
    qk = {n: _nt(ld[n][2], ld[n][1]) for n in ns}
    cq = {n: _nt(cp[n].astype(BF16), ld[n][1]) for n in ns}
    qn = {n: _nt(_row8(npv[n]).astype(BF16), ld[n][1])[0:1] for n in ns}
    imb_col = {n: _col(imb_rows[n:n + 1], eye) for n in ns}
    dlog = {n: jnp.where(sees, b_rows[n:n + 1] + imb_col[n], NEG) for n in ns}
    m_inter = {n: b_rows[n:n + 1] + mp[n] for n in ns}
    m_t = {n: jnp.maximum(m_inter[n], jnp.max(dlog[n], axis=0, keepdims=True)) for n in ns}
    pm = {n: jnp.exp(dlog[n] - m_t[n]) for n in ns}
    inter = {n: jnp.exp(m_inter[n] - m_t[n]) for n in ns}
    floor = {n: jnp.exp(-m_t[n]) for n in ns}
    s = {n: qk[n] * pm[n] for n in ns}
    sv = {n: _nn(vt[n].astype(BF16), s[n].astype(BF16)) for n in ns}
    den = {n: jnp.sum(s[n], axis=0, keepdims=True) + inter[n] * qn[n] for n in ns}
    num = {n: sv[n] + inter[n] * cq[n] for n in ns}
    dn = {n: jnp.maximum(jnp.abs(den[n]), floor[n]) for n in ns}
    return {n: dict(pm=pm[n], s=s[n], inter=inter[n], cq=cq[n], qn=qn[n], num=num[n], den=den[n],
                    floor=floor[n], dn=dn[n]) for n in ns}


def _ml_specs(t, rev):
    nblk = t // ML_TB
    blk = (lambda g: nblk - 1 - g) if rev else (lambda g: g)
    hps = ML_HPS
    tile = lambda c0: pl.BlockSpec((ML_TB, 128 * hps), lambda hg, g, c0=c0: (blk(g), c0 // hps + hg))
    gate = pl.BlockSpec((hps, ML_NB, ML_CHUNK), lambda hg, g: (hg, blk(g), 0))
    cchk = pl.BlockSpec((hps, 1, 128, 128), lambda hg, g: (hg, blk(g), 0, 0))
    nmchk = pl.BlockSpec((hps, 1, 8, 128), lambda hg, g: (hg, blk(g), 0, 0))
    return nblk, blk, tile, gate, cchk, nmchk


def _ml_head_views(refs, hh):
    cols = slice(hh * ML_HEAD_DIM, (hh + 1) * ML_HEAD_DIM)
    return [r.at[:, cols] if len(r.shape) == 2 else r.at[hh] for r in refs]


def _ml_fwd(qk_act, proj, gi, gf, rev, name):
    t = qk_act.shape[0]
    nblk, _, tile, gate, cchk, nmchk = _ml_specs(t, rev)

    def body(*refs):
        for hh in range(ML_HPS):
            one_head(*_ml_head_views(refs, hh))

    def one_head(q_ref, k_ref, v_ref, gi_ref, gf_ref, h_ref, cchk_ref, nmchk_ref, c_ref, nm_ref):
        @pl.when(pl.program_id(1) == 0)
        def _():
            c_ref[...] = jnp.zeros_like(c_ref)
            nm_ref[...] = jnp.zeros_like(nm_ref)
        cchk_ref[0] = c_ref[...]
        nmchk_ref[0] = nm_ref[...]
        eye, csum, _, sees = _ml_consts(rev)
        b_rows, a_rows, imb_rows, mp, mn, decay, order = _ml_gates(
            gi_ref[...], gf_ref[...], nm_ref[1:2, 0:1], csum, rev)
        ld, vt, cp, npv, _, c, nv = _ml_state_scan(q_ref, k_ref, v_ref, a_rows, mn, decay, order,
                                                   c_ref[...], nm_ref[0:1, :])
        c_ref[...] = c
        nm_ref[0:1, :] = nv
        nm_ref[1:2, :] = jnp.broadcast_to(mn[order[-1]], (1, 128))
        for g0 in range(0, ML_NB, ML_GROUP):
            ns = range(g0, g0 + ML_GROUP)
            rs = _ml_intra_all(ld, vt, b_rows, imb_rows, mp, cp, npv, sees, eye, ns)
            ht = {n: rs[n]['num'] / rs[n]['dn'] for n in ns}
            for n in ns:
                h_ref[n * ML_CHUNK:(n + 1) * ML_CHUNK, :] = ht[n].T

    return pl.pallas_call(
        body, name=name, grid=(ML_HEADS // ML_HPS, nblk),
        in_specs=[tile(0), tile(4), tile(24), gate, gate],
        out_specs=[tile(0), cchk, nmchk],
        out_shape=[jax.ShapeDtypeStruct((t, ML_W), F32),
                   jax.ShapeDtypeStruct((ML_HEADS, nblk, 128, 128), F32),
                   jax.ShapeDtypeStruct((ML_HEADS, nblk, 8, 128), F32)],
        scratch_shapes=[pltpu.VMEM((ML_HPS, 128, 128), F32), pltpu.VMEM((ML_HPS, 8, 128), F32)],
        compiler_params=_cparams(("parallel", "arbitrary")),
    )(qk_act, qk_act, proj, gi, gf)


def _ml_bwd(qk_act, proj, gi, gf, dh, cchk_a, nmchk_a, prev, rev, name):
    t = qk_act.shape[0]
    nblk, _, tile, gate, cchk, nmchk = _ml_specs(t, not rev)

    def body(*refs):
        for hh in range(ML_HPS):
            one_head(*_ml_head_views(refs, hh))

    def one_head(q_ref, k_ref, v_ref, gi_ref, gf_ref, dh_ref, cchk_ref, nmchk_ref, *rest):
        prev_refs = rest[:len(prev)]
        dq_ref, dk_ref, dv_ref, dgi_ref, dgf_ref, dc_ref, dn_ref, db_scr, dbl_scr, di_scr = rest[len(prev):]

        def plus_prev(val, which, rows):
            return val + prev_refs[which][rows, :] if prev else val

        @pl.when(pl.program_id(1) == 0)
        def _():
            dc_ref[...] = jnp.zeros_like(dc_ref)
            dn_ref[...] = jnp.zeros_like(dn_ref)
        eye, csum, csum_t, sees = _ml_consts(rev)
        gfv = gf_ref[...]
        b_rows, a_rows, imb_rows, mp, mn, decay, order = _ml_gates(
            gi_ref[...], gfv, nmchk_ref[0, 1:2, 0:1], csum, rev)
        ld, vt, cp, npv, w_row, _, _ = _ml_state_scan(q_ref, k_ref, v_ref, a_rows, mn, decay, order,
                                                      cchk_ref[0], nmchk_ref[0, 0:1, :])
        sls = [ld[n][0] for n in range(ML_NB)]
        qbs = [ld[n][1] for n in range(ML_NB)]
        kbs = [ld[n][2] for n in range(ML_NB)]
        vbs = [ld[n][3].astype(BF16) for n in range(ML_NB)]
        dv0, dk0, xs, xns, dbt, dimb = {}, {}, {}, {}, {}, {}
        for g0 in range(0, ML_NB, ML_GROUP_BWD):
            ns = range(g0, g0 + ML_GROUP_BWD)
            rs = _ml_intra_all(ld, vt, b_rows, imb_rows, mp, cp, npv, sees, eye, ns)
            rdn = {n: 1.0 / rs[n]['dn'] for n in ns}
            dnum = {n: dh_ref[sls[n], :].T * rdn[n] for n in ns}
            hsum = {n: jnp.sum(dnum[n] * rs[n]['num'], axis=0, keepdims=True) for n in ns}
            dden = {n: jnp.where(jnp.abs(rs[n]['den']) > rs[n]['floor'],
                                 -hsum[n] * rdn[n] * jnp.sign(rs[n]['den']), 0.0) for n in ns}
            dnb = {n: dnum[n].astype(BF16) for n in ns}
            dsf = {n: _nn(vbs[n], dnb[n]) + dden[n] for n in ns}
            dv0.update({n: _nt(rs[n]['s'].astype(BF16), dnb[n]) for n in ns})
            gb = {n: (dsf[n] * rs[n]['pm']).astype(BF16) for n in ns}
            cpb = {n: cp[n].astype(BF16) for n in ns}
            idd = {n: rs[n]['inter'] * dden[n] for n in ns}
            idn = {n: (rs[n]['inter'] * dnum[n]).astype(BF16) for n in ns}
            dqa = {n: _tn(gb[n], kbs[n]) for n in ns}
            dqc = {n: _tn(idn[n], cpb[n]) for n in ns}
            dqn = {n: _outer_rows(idd[n], npv[n].astype(BF16)) for n in ns}
            dk0.update({n: _nn(gb[n], qbs[n]) for n in ns})
            xs.update({n: _nn(idn[n], qbs[n]) for n in ns})
            for n in ns:
                dq_ref[sls[n], :] = plus_prev(dqa[n] + dqc[n] + dqn[n], 0, sls[n])
            rr = {n: dsf[n] * rs[n]['s'] for n in ns}
            dinter = {n: jnp.sum(dnum[n] * rs[n]['cq'], axis=0, keepdims=True) + dden[n] * rs[n]['qn'] for n in ns}
            dbt.update({n: jnp.sum(rr[n], axis=0, keepdims=True) + dinter[n] * rs[n]['inter'] for n in ns})
            dimb.update({n: jnp.sum(rr[n], axis=1, keepdims=True) for n in ns})
            xns.update({n: _nn(_row8(idd[n]).astype(BF16), qbs[n])[0:1] for n in ns})
        ns = range(ML_NB)
        dcn, dnn = {}, {}
        dc, dn = dc_ref[...], dn_ref[0:1, :]
        for n in order[::-1]:
            dcn[n], dnn[n] = dc, dn
            dc = decay[n] * dc + xs[n]
            dn = decay[n] * dn + xns[n]
        dc_ref[...] = dc
        dn_ref[0:1, :] = dn
        kscale = ML_HEAD_DIM ** -0.5
        dcb = [dcn[n].astype(BF16) for n in ns]
        z = [_nn(vbs[n], dcb[n]) for n in ns]
        kd = [_nt(kbs[n], dcb[n]) for n in ns]
        ddecay = [jnp.sum(jnp.sum(dcn[n] * cp[n], axis=1, keepdims=True), axis=0, keepdims=True)
                  + jnp.sum(dnn[n] * npv[n], axis=1, keepdims=True) for n in ns]
        zd = [z[n] + dnn[n] for n in ns]
        dw = [jnp.sum(zd[n] * kbs[n].astype(F32), axis=1, keepdims=True) for n in ns]
        wcol = [_col(w_row[n], eye) for n in ns]
        for n in ns:
            dv_ref[sls[n], :] = plus_prev(dv0[n] + wcol[n] * kd[n], 2, sls[n]).astype(dv_ref.dtype)
            dk_ref[sls[n], :] = plus_prev((dk0[n] + wcol[n] * zd[n]) * kscale, 1, sls[n])
        da = [dw[n] * wcol[n] for n in ns]
        dbl = [jnp.sum(da[n], axis=0, keepdims=True) + ddecay[n] * decay[n] for n in ns]
        key_row = [_rowof(dimb[n] + da[n], eye) for n in ns]
        for n in ns:
            db_scr[n:n + 1, :] = dbt[n] - key_row[n]
            di_scr[n:n + 1, :] = key_row[n]
            dbl_scr[n:n + 1, :] = jnp.broadcast_to(dbl[n], (1, ML_CHUNK))
        dlf = jnp.dot(db_scr[...], csum_t, precision=HI, preferred_element_type=F32) + dbl_scr[...]
        dgf_ref[...] = dlf * jax.nn.sigmoid(-gfv)
        dgi_ref[...] = di_scr[...]

    nc = t // ML_CHUNK
    o512 = jax.ShapeDtypeStruct((t, ML_W), F32)
    og = jax.ShapeDtypeStruct((ML_HEADS, nc, ML_CHUNK), F32)
    return pl.pallas_call(
        body, name=name, grid=(ML_HEADS // ML_HPS, nblk),
        in_specs=[tile(0), tile(4), tile(24), gate, gate, tile(0), cchk, nmchk] + [tile(0)] * len(prev),
        out_specs=[tile(0), tile(0), tile(0), gate, gate],
        out_shape=[o512, o512, jax.ShapeDtypeStruct((t, ML_W), BF16 if prev else F32), og, og],
        scratch_shapes=[pltpu.VMEM((ML_HPS, 128, 128), F32), pltpu.VMEM((ML_HPS, 8, 128), F32)]
        + [pltpu.VMEM((ML_HPS, ML_NB, ML_CHUNK), F32)] * 3,
        compiler_params=_cparams(("parallel", "arbitrary")),
    )(qk_act, qk_act, proj, gi, gf, dh, cchk_a, nmchk_a, *prev)


def _gate_rows(gates16, t):
    g = gates16.reshape(t // ML_CHUNK, ML_CHUNK, 4, ML_HEADS).transpose(2, 3, 0, 1)
    return g[0], g[1], g[2], g[3]


def _gate_cols(dgi_f, dgf_f, dgi_b, dgf_b, t):
    g = jnp.stack([dgi_f, dgf_f, dgi_b, dgf_b]).transpose(2, 3, 0, 1).reshape(t, 4 * ML_HEADS)
    return jnp.pad(g, ((0, 0), (0, 128 - 4 * ML_HEADS)))


def _local_step(x, target, shift, scale, gate, norm_w, w_in_t, b_in_p, conv_w8, conv_b, rpb,
                ml_norm_w, w_out_b, final_norm_w):
    t = x.shape[0]
    rows = t // GRID_W
    tm = 512
    proj, gates = _in_proj(x, norm_w, scale, shift, w_in_t, b_in_p)
    tbl = _na_bias_table(rpb, rows)
    o_na, lse_na = _na_fwd(proj, tbl)
    pre, qk_act = _conv_fwd(proj, conv_w8, conv_b, 2 * tm)
    gi_f, gf_f, gi_b, gf_b = _gate_rows(gates[:, :4 * ML_HEADS], t)
    h_f, cchk_f, nmchk_f = _ml_fwd(qk_act, proj, gi_f, gf_f, False, "ml_fwd_f")
    h_b, cchk_b, nmchk_b = _ml_fwd(qk_act, proj, gi_b, gf_b, True, "ml_fwd_b")
    (loss, dres, d_ona, d_naz, dhs, d_o, d_z, dgate, g_fnw, g_mlnw, g_w_out) = _tail(
        o_na, proj, h_f, h_b, x, target, gate, ml_norm_w, final_norm_w, w_out_b)
    dq_na, dk_na, dv_na, rpbacc = _na_bwd(proj, tbl, d_ona, o_na, lse_na)
    g_rpb = _rpb_reduce(rpbacc, rows)
    dq_f, dk_f, dv_f, dgi_f, dgf_f = _ml_bwd(qk_act, proj, gi_f, gf_f, dhs, cchk_f, nmchk_f, (),
                                             False, "ml_bwd_f")
    dq_ml, dk_ml, dv_ml, dgi_b, dgf_b = _ml_bwd(qk_act, proj, gi_b, gf_b, dhs, cchk_b, nmchk_b, (dq_f, dk_f, dv_f),
                                                True, "ml_bwd_b")
    du, g_conv_w, g_conv_b = _conv_bwd(dq_ml, dk_ml, pre, proj, conv_w8, tm)
    dgates = _gate_cols(dgi_f, dgf_f, dgi_b, dgf_b, t)
    grad_x, g_w_in, g_b_in, dscale, dshift, g_nw = _in_bwd(
        [dq_na, dk_na, dv_na, d_naz, du, dv_ml, d_o, d_z, dgates], x, dres, w_in_t, norm_w, scale, shift)
    dmod = jnp.concatenate([dshift, dscale, dgate], axis=1)
    return (loss, grad_x, dmod, g_nw, g_w_in, g_b_in, g_conv_w, g_conv_b, g_rpb, g_mlnw, g_w_out, g_fnw)


MESH = pl.DeviceIdType.MESH
N_DEV = 8
ANY = pl.BlockSpec(memory_space=pl.ANY)
WHOLE_VMEM = pl.BlockSpec(memory_space=pltpu.VMEM)


def _allgather8(blocks, name):
    na = len(blocks)

    def body(*refs):
        x_refs = refs[:na]
        out_refs = refs[na:2 * na]
        send_sems, recv_sems, local_sems = refs[2 * na:]
        x, y, c = lax.axis_index("x"), lax.axis_index("y"), lax.axis_index("c")
        me, sibling = (x, y, c), (x, y, 1 - c)
        chips = [(1 - x, y), (x, 1 - y), (1 - x, 1 - y)]

        def rows(a, px, py, pc):
            return out_refs[a].at[4 * px + 2 * py + pc]

        def copy(a, k, block, to, src=None):
            return pltpu.make_async_remote_copy(
                src_ref=rows(a, *block) if src is None else src, dst_ref=rows(a, *block),
                send_sem=send_sems.at[a, k], recv_sem=recv_sems.at[a, k],
                device_id=to, device_id_type=MESH)

        mine, first, passed = [], [], []
        for a in range(na):
            cp = pltpu.make_async_copy(x_refs[a], rows(a, *me), local_sems.at[a])
            cp.start()
            mine.append(cp)
            first.append(copy(a, 0, me, sibling, src=x_refs[a]))
            first += [copy(a, 1 + j, me, (*chip, c), src=x_refs[a]) for j, chip in enumerate(chips)]
        for cp in first:
            cp.start()
        for a in range(na):
            for j, chip in enumerate(chips):
                copy(a, 1 + j, (*chip, c), me).wait_recv()
                fwd = copy(a, 4 + j, (*chip, c), sibling)
                fwd.start()
                passed.append(fwd)
        for a in range(na):
            copy(a, 0, sibling, me).wait_recv()
            for j, chip in enumerate(chips):
                copy(a, 4 + j, (*chip, 1 - c), me).wait_recv()
        for cp in first + passed:
            cp.wait_send()
        for cp in mine:
            cp.wait()

    return pl.pallas_call(
        body, name=name,
        out_shape=[jax.ShapeDtypeStruct((N_DEV,) + b.shape, b.dtype) for b in blocks],
        in_specs=[WHOLE_VMEM] * na, out_specs=[WHOLE_VMEM] * na,
        scratch_shapes=[pltpu.SemaphoreType.DMA((na, 7)), pltpu.SemaphoreType.DMA((na, 7)),
                        pltpu.SemaphoreType.DMA((na,))],
        compiler_params=pltpu.CompilerParams(vmem_limit_bytes=VMEM_LIMIT),
    )(*blocks)


def _pair_exchange(arrs, name):
    na = len(arrs)

    def body(*refs):
        in_refs = refs[:na]
        out_refs = refs[na:2 * na]
        send_sems, recv_sems = refs[2 * na:]
        sibling = (lax.axis_index("x"), lax.axis_index("y"), 1 - lax.axis_index("c"))
        copies = [pltpu.make_async_remote_copy(
            src_ref=in_refs[a], dst_ref=out_refs[a], send_sem=send_sems.at[a], recv_sem=recv_sems.at[a],
            device_id=sibling, device_id_type=MESH) for a in range(na)]
        for cp in copies:
            cp.start()
        for cp in copies:
            cp.wait()

    return pl.pallas_call(
        body, name=name,
        out_shape=[jax.ShapeDtypeStruct(a.shape, a.dtype) for a in arrs],
        in_specs=[ANY] * na, out_specs=[ANY] * na,
        scratch_shapes=[pltpu.SemaphoreType.DMA((na,)), pltpu.SemaphoreType.DMA((na,))],
    )(*arrs)


def _chip_exchange(arrs, name):
    na = len(arrs)

    def body(*refs):
        in_refs = refs[:na]
        out_refs = refs[na:2 * na]
        send_sems, recv_sems, local_sems = refs[2 * na:]
        x, y, c = lax.axis_index("x"), lax.axis_index("y"), lax.axis_index("c")
        my_chip = 2 * x + y
        chips = [(1 - x, y), (x, 1 - y), (1 - x, 1 - y)]
        local, remote = [], []
        for a in range(na):
            cp = pltpu.make_async_copy(in_refs[a].at[my_chip], out_refs[a].at[my_chip], local_sems.at[a])
            cp.start()
            local.append(cp)
            for j, (px, py) in enumerate(chips):
                cp = pltpu.make_async_remote_copy(
                    src_ref=in_refs[a].at[2 * px + py], dst_ref=out_refs[a].at[my_chip],
                    send_sem=send_sems.at[a, j], recv_sem=recv_sems.at[a, j],
                    device_id=(px, py, c), device_id_type=MESH)
                cp.start()
                remote.append(cp)
        for cp in remote:
            cp.wait()
        for cp in local:
            cp.wait()

    return pl.pallas_call(
        body, name=name,
        out_shape=[jax.ShapeDtypeStruct(a.shape, a.dtype) for a in arrs],
        in_specs=[ANY] * na, out_specs=[ANY] * na,
        scratch_shapes=[pltpu.SemaphoreType.DMA((na, 3)), pltpu.SemaphoreType.DMA((na, 3)),
                        pltpu.SemaphoreType.DMA((na,))],
    )(*arrs)


def _rows_tile(r):
    for cand in (512, 256, 128, 64, 32, 16, 8):
        if r % cand == 0:
            return cand
    return r


def _add2(a, b, name, out_dtype):
    s, r, n = a.shape
    tr = _rows_tile(r)

    def body(a_ref, b_ref, o_ref):
        o_ref[...] = (a_ref[...] + b_ref[...]).astype(out_dtype)

    spec = pl.BlockSpec((1, tr, n), lambda i, j: (i, j, 0))
    return pl.pallas_call(
        body, name=name, grid=(s, r // tr), in_specs=[spec, spec], out_specs=spec,
        out_shape=jax.ShapeDtypeStruct(a.shape, out_dtype),
        compiler_params=_cparams(("parallel", "parallel")),
    )(a, b)


def _sum_slabs(a, name):
    s, r, n = a.shape
    tr = _rows_tile(r)

    def body(a_ref, o_ref):
        acc = a_ref[0].astype(F32)
        for k in range(1, s):
            acc = acc + a_ref[k].astype(F32)
        o_ref[...] = acc

    return pl.pallas_call(
        body, name=name, grid=(r // tr,),
        in_specs=[pl.BlockSpec((s, tr, n), lambda i: (0, i, 0))],
        out_specs=pl.BlockSpec((tr, n), lambda i: (i, 0)),
        out_shape=jax.ShapeDtypeStruct((r, n), F32),
        compiler_params=_cparams(("parallel",)),
    )(a)


ADAMW_WHOLE = 64 * 1024


def _adamw(w, g, m, v, name):
    r, n = w.shape
    if r * n <= ADAMW_WHOLE:
        blk, grid, imap = (r, n), (1,), (lambda i: (0, 0))
    elif r % 8 == 0:
        blk, grid, imap = (_rows_tile(r), n), (r // _rows_tile(r),), (lambda i: (i, 0))
    else:
        blk, grid, imap = (r, 128), (n // 128,), (lambda i: (0, i))
    c1 = 1.0 / (1.0 - ADAM_B1 ** ADAM_STEP)
    c2 = 1.0 / (1.0 - ADAM_B2 ** ADAM_STEP)

    def body(w_ref, g_ref, m_ref, v_ref, d_ref, nm_ref, nv_ref):
        gv = g_ref[...]
        nm = ADAM_B1 * m_ref[...] + (1.0 - ADAM_B1) * gv
        nv = ADAM_B2 * v_ref[...] + (1.0 - ADAM_B2) * (gv * gv)
        nm_ref[...] = nm
        nv_ref[...] = nv
        d_ref[...] = -ADAM_LR * ((nm * c1) / (jnp.sqrt(nv * c2) + ADAM_EPS) + ADAM_WD * w_ref[...])

    spec = pl.BlockSpec(blk, imap)
    o = jax.ShapeDtypeStruct((r, n), F32)
    return pl.pallas_call(
        body, name=name, grid=grid, in_specs=[spec] * 4, out_specs=[spec] * 3, out_shape=[o, o, o],
        compiler_params=_cparams(("parallel",)),
    )(w, g, m, v)


def _mod_fwd(c_all, w_ada_s, b_ada_s):
    def body(c_ref, w_ref, b_ref, o_ref):
        o_ref[...] = jnp.dot(_silu(c_ref[...]), w_ref[...], precision=HI, preferred_element_type=F32) + b_ref[...]

    return pl.pallas_call(
        body, name="mod_fwd", out_shape=jax.ShapeDtypeStruct((c_all.shape[0], w_ada_s.shape[1]), F32),
        in_specs=[WHOLE_VMEM] * 3, out_specs=WHOLE_VMEM,
        compiler_params=pltpu.CompilerParams(vmem_limit_bytes=VMEM_LIMIT),
    )(c_all, w_ada_s, b_ada_s)


def _wada_grad(c_all, dmod_s):
    def body(c_ref, d_ref, o_ref):
        o_ref[...] = lax.dot_general(_silu(c_ref[...]), d_ref[...], (((0,), (0,)), ((), ())),
                                     precision=HI, preferred_element_type=F32)

    return pl.pallas_call(
        body, name="w_ada_grad", out_shape=jax.ShapeDtypeStruct((c_all.shape[1], dmod_s.shape[1]), F32),
        in_specs=[WHOLE_VMEM] * 2, out_specs=WHOLE_VMEM,
        compiler_params=pltpu.CompilerParams(vmem_limit_bytes=VMEM_LIMIT),
    )(c_all, dmod_s)


SMALL_ROWS = 24


def _pad_rows(v, nrows):
    v = v.reshape(-1)
    return jnp.pad(v, (0, nrows * 1024 - v.shape[0])).reshape(nrows, 1024)


def _pack_small(b_ada, norm_w, b_in, conv_w_full, conv_b, rpb, ml_norm_w, final_norm_w, last):
    parts = [_pad_rows(b_ada, 3), _pad_rows(norm_w, 1), _pad_rows(b_in, 5), _pad_rows(conv_w_full, 5),
             _pad_rows(conv_b, 1), _pad_rows(rpb, 4), _pad_rows(ml_norm_w, 1), _pad_rows(final_norm_w, 1),
             _pad_rows(last, 3)]
    return jnp.concatenate(parts, axis=0)


def _unpack_small(p):
    return dict(b_ada=p[0:3].reshape(1, 3072), norm_w=p[3:4], b_in=p[4:9].reshape(-1)[:IN_W].reshape(1, IN_W),
                conv_w=p[9:14], conv_b=p[14:15],
                rpb=p[15:19].reshape(-1)[:NA_HEADS * 15 * 31].reshape(1, NA_HEADS, 15, 31),
                ml_norm_w=p[19:20, :ML_W], final_norm_w=p[20], last=p[21])


def kernel(x, c, w_ada, b_ada, norm_w, w_in, b_in, conv_w, conv_b, rpb, ml_norm_w, w_out, final_norm_w, loss_target, m_w_ada, m_b_ada, m_norm_w, m_w_in, m_b_in, m_conv_w, m_conv_b, m_rpb, m_ml_norm_w, m_w_out, m_final_norm_w, v_w_ada, v_b_ada, v_norm_w, v_w_in, v_b_in, v_conv_w, v_conv_b, v_rpb, v_ml_norm_w, v_w_out, v_final_norm_w):
    xi, yi, ci = lax.axis_index("x"), lax.axis_index("y"), lax.axis_index("c")
    chip = 2 * xi + yi
    dev = 2 * chip + ci
    t = x.shape[1]
    ada_n = w_ada.shape[2]
    in_n = w_in.shape[2]
    out_r = w_out.shape[1]

    c_blk = jnp.pad(c, ((0, 7), (0, 0)))
    w_in_t, m_w_in_t, v_w_in_t = w_in[0].T, m_w_in[0].T, v_w_in[0].T
    in_h = in_n // 2
    w_in_half = lax.dynamic_slice_in_dim(w_in_t, ci * in_h, in_h, axis=0).astype(BF16)
    w_out_half = lax.dynamic_slice_in_dim(w_out[0], ci * (out_r // 2), out_r // 2, axis=0).astype(BF16)
    conv_blk = jnp.pad(conv_w[0], ((0, 3), (0, 0)))
    c_g, conv_g, w_in_g, w_out_g = _allgather8([c_blk, conv_blk, w_in_half, w_out_half], "gather_c_weights")
    c_all = c_g[:, 0]
    w_out_g = w_out_g.reshape(D_MODEL, D_MODEL)
    b_ada_s = lax.dynamic_slice_in_dim(b_ada, chip * ada_n, ada_n, axis=1)
    mod_s = _mod_fwd(c_all, w_ada[0], b_ada_s)
    (mod_g,) = _allgather8([mod_s], "gather_mod")
    mod_mine = lax.dynamic_index_in_dim(mod_g, dev, axis=1, keepdims=False)
    mod = mod_mine[0::2].reshape(1, 3 * D_MODEL)
    shift, scale, gate = mod[:, :D_MODEL], mod[:, D_MODEL:2 * D_MODEL], mod[:, 2 * D_MODEL:]

    w_in_tp = jnp.pad(w_in_g.reshape(IN_W, D_MODEL), ((0, IN_PAD - IN_W), (0, 0)))
    b_in_p = jnp.pad(b_in, ((0, 0), (0, IN_PAD - IN_W)))
    conv_w8 = conv_g.reshape(4, 2, 8, conv_w.shape[2])[:, 0].transpose(1, 0, 2).reshape(8, D_MODEL)

    (loss, grad_x, dmod, g_nw, g_w_in, g_b_in, g_conv_w, g_conv_b, g_rpb, g_mlnw, g_w_out, g_fnw) = _local_step(
        x[0], loss_target[0], shift, scale, gate, norm_w, w_in_tp, b_in_p, conv_w8, conv_b, rpb[0],
        ml_norm_w, w_out_g, final_norm_w.reshape(1, D_MODEL))

    g_in_t = g_w_in

    def halves(a, per_chip, h):
        return jnp.stack([lax.dynamic_slice_in_dim(a, k * per_chip + h * (per_chip // 2), per_chip // 2, axis=0)
                          for k in range(4)])

    ri, ro = _pair_exchange([halves(g_in_t, in_n, 1 - ci), halves(g_w_out, out_r, 1 - ci)], "rs_pair")
    pi = _add2(halves(g_in_t, in_n, ci), ri, "rs_pair_add_in", BF16)
    po = _add2(halves(g_w_out, out_r, ci), ro, "rs_pair_add_out", BF16)
    qi, qo = _chip_exchange([pi, po], "rs_chips")
    si = _sum_slabs(qi, "rs_sum_in")
    so = _sum_slabs(qo, "rs_sum_out")
    ti, to = _pair_exchange([si, so], "rs_share")
    g_w_in_s = jnp.where(ci == 0, jnp.concatenate([si, ti], axis=0), jnp.concatenate([ti, si], axis=0))
    g_w_out_s = jnp.where(ci == 0, jnp.concatenate([so, to], axis=0), jnp.concatenate([to, so], axis=0))

    small = _pack_small(dmod, g_nw, g_b_in[:, :IN_W], g_conv_w[:CONV_W], g_conv_b, g_rpb, g_mlnw, g_fnw,
                        jnp.pad(loss, ((0, 0), (0, 1024 - 128))))
    (small_g,) = _allgather8([small], "gather_small")
    small_sum = _sum_slabs(small_g, "small_sum")
    gs = _unpack_small(small_sum)
    dmod_all = small_g[:, 0:3].reshape(N_DEV, 3 * D_MODEL)
    g_w_ada_s = _wada_grad(c_all, lax.dynamic_slice_in_dim(dmod_all, chip * ada_n, ada_n, axis=1))
    g_conv_w_s = lax.dynamic_slice_in_dim(gs['conv_w'], chip * conv_w.shape[2], conv_w.shape[2], axis=1)
    loss_total = gs['last'][0]

    small_names = ('b_ada', 'norm_w', 'b_in', 'conv_b', 'rpb', 'ml_norm_w', 'final_norm_w')
    small_w = (b_ada, norm_w, b_in, conv_b, rpb, ml_norm_w, final_norm_w)
    small_m = (m_b_ada, m_norm_w, m_b_in, m_conv_b, m_rpb, m_ml_norm_w, m_final_norm_w)
    small_v = (v_b_ada, v_norm_w, v_b_in, v_conv_b, v_rpb, v_ml_norm_w, v_final_norm_w)
    ds_, nms, nvs = {}, {}, {}
    for nm_, w_, m_, v_ in zip(small_names, small_w, small_m, small_v):
        two_d = (NA_HEADS, w_.size // NA_HEADS) if nm_ == 'rpb' else (1, w_.size)
        outs = _adamw(w_.reshape(two_d), gs[nm_].reshape(two_d), m_.reshape(two_d), v_.reshape(two_d),
                      "adamw_" + nm_)
        ds_[nm_], nms[nm_], nvs[nm_] = [o.reshape(w_.shape) for o in outs]
    d_ada, nm_ada, nv_ada = _adamw(w_ada[0], g_w_ada_s, m_w_ada[0], v_w_ada[0], "adamw_w_ada")
    d_in, nm_in, nv_in = _adamw(w_in_t, g_w_in_s, m_w_in_t, v_w_in_t, "adamw_w_in")
    d_out, nm_out, nv_out = _adamw(w_out[0], g_w_out_s, m_w_out[0], v_w_out[0], "adamw_w_out")
    d_cw, nm_cw, nv_cw = _adamw(conv_w[0], g_conv_w_s, m_conv_w[0], v_conv_w[0], "adamw_conv_w")

    def group(big_ada, big_in, big_out, cw, sm):
        return (big_ada[None], sm['b_ada'], sm['norm_w'], big_in.T[None], sm['b_in'], cw[None], sm['conv_b'],
                sm['rpb'], sm['ml_norm_w'], big_out[None], sm['final_norm_w'])

    return ((loss_total, grad_x[None])
            + group(g_w_ada_s, g_w_in_s, g_w_out_s, g_conv_w_s, gs)
            + group(d_ada, d_in, d_out, d_cw, ds_)
            + group(nm_ada, nm_in, nm_out, nm_cw, nms)
            + group(nv_ada, nv_in, nv_out, nv_cw, nvs))
```

```python
import functools

import numpy as np
import jax
import jax.numpy as jnp
from jax import lax
from jax.experimental import pallas as pl
from jax.experimental.pallas import tpu as pltpu

F32 = jnp.float32
BF16 = jnp.bfloat16
HI = lax.Precision.HIGHEST

D_MODEL = 1024
GRID_W = 64
NA_W = 512
NA_HEAD_DIM = 64
NA_HEADS = 8
NA_KH = 8
NA_KW = 16
ML_W = 512
ML_HEADS = 4
ML_HEAD_DIM = 128
ML_CHUNK = 128
CONV_W = 5
EPS = 1e-6
IN_W = 4 * NA_W + 5 * ML_W + 4 * ML_HEADS
IN_MAIN = 4 * NA_W + 5 * ML_W
IN_PAD = IN_MAIN + 128
NEG = -1e30

ADAM_LR = 0.001
ADAM_B1 = 0.9
ADAM_B2 = 0.999
ADAM_EPS = 1e-08
ADAM_WD = 0.01
ADAM_STEP = 10

NA_QROWS = 8
NA_KROWS = 16
NA_QT = NA_QROWS * GRID_W
NA_KT = NA_KROWS * GRID_W
NA_KCH = 256
NA_RC = 32
ML_NB = 32
ML_TB = ML_NB * ML_CHUNK
ML_HPS = 1

VMEM_LIMIT = 56 * 1024 * 1024
IN_BWD_VMEM_LIMIT = 60 * 1024 * 1024


def _cparams(sem, vmem=VMEM_LIMIT):
    return pltpu.CompilerParams(dimension_semantics=sem, vmem_limit_bytes=vmem)


def _silu(x):
    return x * jax.nn.sigmoid(x)


def _dsilu(x):
    s = jax.nn.sigmoid(x)
    return s * (1.0 + x * (1.0 - s))


def _dot(a, b, dims):
    return lax.dot_general(a, b, (dims, ((), ())), preferred_element_type=F32)


def _nn(a, b):
    return _dot(a, b, ((1,), (0,)))


def _nt(a, b):
    return _dot(a, b, ((1,), (1,)))


def _tn(a, b):
    return _dot(a, b, ((0,), (0,)))


def _row(n):
    return pl.BlockSpec((1, n), lambda i: (0, 0))


def _modulated_norm(xv, nw, sc, sh):
    r = lax.rsqrt(jnp.mean(xv * xv, axis=-1, keepdims=True) + EPS)
    xn = xv * r
    return xn * nw * (1.0 + sc) + sh, xn, r


IN_TN = 768


def _in_proj(x, norm_w, scale, shift, w_in_t, b_in_p):
    t, d = x.shape
    tm = 2048
    gcol = IN_MAIN // 128

    def body(x_ref, nw_ref, sc_ref, sh_ref, w_ref, b_ref, wg_ref, bg_ref, proj_ref, g_ref, h_scr):
        @pl.when(pl.program_id(1) == 0)
        def _():
            h, _, _ = _modulated_norm(x_ref[...], nw_ref[...], sc_ref[...], sh_ref[...])
            h_scr[...] = h.astype(BF16)
            g_ref[...] = _nt(h_scr[...], wg_ref[...]) + bg_ref[...]
        proj_ref[...] = (_nt(h_scr[...], w_ref[...]) + b_ref[...]).astype(BF16)

    row = lambda n: pl.BlockSpec((1, n), lambda i, j: (0, 0))
    return pl.pallas_call(
        body, name="in_proj", grid=(t // tm, IN_MAIN // IN_TN),
        in_specs=[pl.BlockSpec((tm, d), lambda i, j: (i, 0)), row(d), row(d), row(d),
                  pl.BlockSpec((IN_TN, d), lambda i, j: (j, 0)), pl.BlockSpec((1, IN_TN), lambda i, j: (0, j)),
                  pl.BlockSpec((128, d), lambda i, j: (gcol, 0)), pl.BlockSpec((1, 128), lambda i, j: (0, gcol))],
        out_specs=[pl.BlockSpec((tm, IN_TN), lambda i, j: (i, j)), pl.BlockSpec((tm, 128), lambda i, j: (i, 0))],
        out_shape=[jax.ShapeDtypeStruct((t, IN_MAIN), BF16), jax.ShapeDtypeStruct((t, 128), F32)],
        scratch_shapes=[pltpu.VMEM((tm, d), BF16)],
        compiler_params=_cparams(("parallel", "arbitrary")),
    )(x, norm_w, scale, shift, w_in_t, b_in_p, w_in_t, b_in_p)


def _ml_norm_parts(hs, o, z, nw):
    outs = []
    for hh in range(ML_HEADS):
        sl = slice(hh * ML_HEAD_DIM, (hh + 1) * ML_HEAD_DIM)
        so = jax.nn.sigmoid(o[:, sl])
        hm = hs[:, sl] * so
        mu = jnp.mean(hm, axis=-1, keepdims=True)
        cen = hm - mu
        var = jnp.mean(cen * cen, axis=-1, keepdims=True)
        rs = lax.rsqrt(var + EPS)
        outs.append((sl, cen * rs, rs, so))
    return outs


def _tail(o_na, proj, h_f, h_b, x, target, gate, ml_norm_w, fnw, w_out_b):
    t, d = x.shape
    tm = 512

    def body(ona_ref, naz_ref, hf_ref, hb_ref, o_ref, z_ref, x_ref, tg_ref, g_ref, nw_ref, fw_ref, w_ref,
             loss_ref, dres_ref, dona_ref, dnaz_ref, dhs_ref, do_ref, dz_ref, dgate_ref, gfw_ref, gnw_ref,
             gwo_ref, mix_scr):
        @pl.when(pl.program_id(0) == 0)
        def _():
            for r in (loss_ref, dgate_ref, gfw_ref, gnw_ref, gwo_ref):
                r[...] = jnp.zeros_like(r)
        naz = naz_ref[...].astype(F32)
        ona = ona_ref[...]
        sg_naz = jax.nn.sigmoid(naz)
        sna = naz * sg_naz
        mix_scr[:, 0:NA_W] = (ona * sna).astype(BF16)
        hs = hf_ref[...] + hb_ref[...]
        z = z_ref[...].astype(F32)
        ov = o_ref[...].astype(F32)
        parts = _ml_norm_parts(hs, ov, z, nw_ref[...])
        sgz = [jax.nn.sigmoid(z[:, sl]) for sl, _, _, _ in parts]
        for (sl, xn, _, _), sg in zip(parts, sgz):
            mix_scr[:, NA_W + sl.start:NA_W + sl.stop] = (xn * nw_ref[:, sl] * (z[:, sl] * sg)).astype(BF16)
        mixb = mix_scr[...]
        wv = w_ref[...]
        yv = _nn(mixb, wv)
        gate_v = g_ref[...]
        hres = x_ref[...] + gate_v * yv
        r = lax.rsqrt(jnp.mean(hres * hres, axis=-1, keepdims=True) + EPS)
        xnf = hres * r
        err = xnf * fw_ref[...] - tg_ref[...]
        loss_ref[...] += 0.5 * jnp.sum(jnp.sum(err * err, axis=-1, keepdims=True) * (1.0 / d), axis=0, keepdims=True)
        dout = err * (1.0 / d)
        gfw_ref[...] += jnp.sum(dout * xnf, axis=0, keepdims=True)
        dxn = dout * fw_ref[...]
        dres = r * (dxn - xnf * jnp.mean(dxn * xnf, axis=-1, keepdims=True))
        dres_ref[...] = dres
        dgate_ref[...] += jnp.sum(dres * yv, axis=0, keepdims=True)
        dyb = (dres * gate_v).astype(BF16)
        gwo_ref[...] += _tn(mixb, dyb)
        dmix = _nt(dyb, wv)
        dna = dmix[:, 0:NA_W]
        dona_ref[...] = dna * sna
        dnaz_ref[...] = (dna * ona * (sg_naz * (1.0 + naz * (1.0 - sg_naz)))).astype(BF16)
        for (sl, xn, rs, so), sg in zip(parts, sgz):
            dyv = dmix[:, NA_W + sl.start:NA_W + sl.stop]
            zz = z[:, sl]
            sz = zz * sg
            w = nw_ref[:, sl]
            dz_ref[:, sl] = (dyv * xn * w * (sg * (1.0 + zz * (1.0 - sg)))).astype(BF16)
            gnw_ref[:, sl] += jnp.sum(dyv * xn * sz, axis=0, keepdims=True)
            dxm = dyv * w * sz
            dhm = rs * (dxm - jnp.mean(dxm, axis=-1, keepdims=True)
                        - xn * jnp.mean(dxm * xn, axis=-1, keepdims=True))
            dhs_ref[:, sl] = dhm * so
            do_ref[:, sl] = (dhm * hs[:, sl] * so * (1.0 - so)).astype(BF16)

    blk = lambda c: pl.BlockSpec((tm, 512), lambda i, c=c: (i, c))
    full = pl.BlockSpec((tm, d), lambda i: (i, 0))
    o512 = jax.ShapeDtypeStruct((t, 512), F32)
    b512 = jax.ShapeDtypeStruct((t, 512), BF16)
    whole = pl.BlockSpec((d, d), lambda i: (0, 0))
    return pl.pallas_call(
        body, name="tail", grid=(t // tm,),
        in_specs=[blk(0), blk(3), blk(0), blk(0), blk(7), blk(8), full, full, _row(d), _row(ML_W), _row(d), whole],
        out_specs=[pl.BlockSpec((1, 128), lambda i: (0, 0)), full] + [blk(0)] * 5
        + [_row(d), _row(d), _row(ML_W), whole],
        out_shape=[jax.ShapeDtypeStruct((1, 128), F32), jax.ShapeDtypeStruct((t, d), F32),
                   o512, b512, o512, b512, b512]
        + [jax.ShapeDtypeStruct((1, d), F32), jax.ShapeDtypeStruct((1, d), F32),
           jax.ShapeDtypeStruct((1, ML_W), F32), jax.ShapeDtypeStruct((d, d), F32)],
        scratch_shapes=[pltpu.VMEM((tm, d), BF16)],
        compiler_params=_cparams(("arbitrary",)),
    )(o_na, proj, h_f, h_b, proj, proj, x, target, gate, ml_norm_w, fnw, w_out_b)


def _in_bwd(pieces, x, dres, w_in_t, norm_w, scale, shift):
    t, d = x.shape
    tm = 512
    nt = t // tm
    widths = [p.shape[1] for p in pieces]
    offs = [sum(widths[:k]) for k in range(len(widths))]
    assert sum(widths) == IN_PAD
    npc = len(pieces)

    def body(*refs):
        p_refs = refs[:npc]
        (x_ref, dres_ref, w_hbm, nw_ref, sc_ref, sh_ref,
         gx_ref, gw_hbm, gb_ref, dsc_ref, dsh_ref, gnw_ref, w_vmem, acc, stage, sem) = refs[npc:]
        i = pl.program_id(0)

        @pl.when(i == 0)
        def _():
            cp = pltpu.make_async_copy(w_hbm, w_vmem, sem.at[0])
            cp.start()
            acc[...] = jnp.zeros_like(acc)
            gb_ref[...] = jnp.zeros_like(gb_ref)
            dsc_ref[...] = jnp.zeros_like(dsc_ref)
            dsh_ref[...] = jnp.zeros_like(dsh_ref)
            gnw_ref[...] = jnp.zeros_like(gnw_ref)
            cp.wait()

        nw = nw_ref[...]
        s1 = 1.0 + sc_ref[...]
        h, xn, r = _modulated_norm(x_ref[...], nw, sc_ref[...], sh_ref[...])
        hb = h.astype(BF16)
        dhv = jnp.zeros((tm, d), F32)
        for p_ref, c0, w in zip(p_refs, offs, widths):
            pt = p_ref[...]
            pb = pt.astype(BF16)
            dhv = dhv + _nn(pb, w_vmem[c0:c0 + w, :])
            acc[:, c0:c0 + w] += _tn(hb, pb)
            gb_ref[:, c0:c0 + w] += jnp.sum(pt.astype(F32), axis=0, keepdims=True)
        dsh_ref[...] += jnp.sum(dhv, axis=0, keepdims=True)
        dsc_ref[...] += jnp.sum(dhv * xn * nw, axis=0, keepdims=True)
        gnw_ref[...] += jnp.sum(dhv * xn * s1, axis=0, keepdims=True)
        dxn = dhv * nw * s1
        gx_ref[...] = dres_ref[...] + r * (dxn - xn * jnp.mean(dxn * xn, axis=-1, keepdims=True))

        @pl.when(i == nt - 1)
        def _():
            copies = []
            for blk in range(IN_PAD // 128):
                slot = blk % 2
                if blk >= 2:
                    copies[blk - 2].wait()
                stage[slot] = acc[:, blk * 128:(blk + 1) * 128].T
                cp = pltpu.make_async_copy(stage.at[slot], gw_hbm.at[pl.ds(blk * 128, 128), :], sem.at[1 + slot])
                cp.start()
                copies.append(cp)
            copies[-2].wait()
            copies[-1].wait()

    full = pl.BlockSpec((tm, d), lambda i: (i, 0))
    return pl.pallas_call(
        body, name="in_bwd", grid=(nt,),
        in_specs=[pl.BlockSpec((tm, w), lambda i: (i, 0)) for w in widths]
        + [full, full, pl.BlockSpec(memory_space=pl.ANY), _row(d), _row(d), _row(d)],
        out_specs=[full, pl.BlockSpec(memory_space=pl.ANY), _row(IN_PAD), _row(d), _row(d), _row(d)],
        out_shape=[jax.ShapeDtypeStruct((t, d), F32), jax.ShapeDtypeStruct((IN_PAD, d), F32),
                   jax.ShapeDtypeStruct((1, IN_PAD), F32)] + [jax.ShapeDtypeStruct((1, d), F32)] * 3,
        scratch_shapes=[pltpu.VMEM((IN_PAD, d), BF16), pltpu.VMEM((d, IN_PAD), F32),
                        pltpu.VMEM((2, 128, d), F32), pltpu.SemaphoreType.DMA((3,))],
        compiler_params=_cparams(("arbitrary",), IN_BWD_VMEM_LIMIT),
    )(*pieces, x, dres, w_in_t, norm_w, scale, shift)


def _na_static(rows):
    cases = [(0, 0), (NA_QROWS, NA_QROWS - 4), (rows - NA_QROWS, rows - NA_KROWS)]
    dy = np.zeros((3, NA_QROWS, NA_KROWS), np.int32)
    rv = np.zeros((3, NA_QROWS, NA_KROWS), bool)
    for cs, (r0, kr0) in enumerate(cases):
        for i in range(NA_QROWS):
            for j in range(NA_KROWS):
                r, kr = r0 + i, kr0 + j
                rs = min(max(r - NA_KH // 2, 0), rows - NA_KH)
                rv[cs, i, j] = rs <= kr <= rs + NA_KH - 1
                dy[cs, i, j] = min(max(kr - r + NA_KH - 1, 0), 2 * NA_KH - 2)
    cq = np.arange(GRID_W)[:, None]
    ck = np.arange(GRID_W)[None, :]
    cs0 = np.clip(cq - NA_KW // 2, 0, GRID_W - NA_KW)
    cv = (ck >= cs0) & (ck < cs0 + NA_KW)
    dx = np.clip(ck - cq, -(NA_KW - 1), NA_KW - 1) + NA_KW - 1
    return dy, rv, dx.astype(np.int32), cv


def _na_bias_table(rpb, rows):
    _, _, dx, cv = _na_static(rows)
    ndy = 2 * NA_KH - 1
    onehot = (dx.reshape(1, -1) == np.arange(2 * NA_KW - 1)[:, None]).astype(np.float32)
    rpx = jnp.dot(rpb.reshape(NA_HEADS * ndy, 2 * NA_KW - 1), jnp.asarray(onehot), precision=HI)
    rpx = jnp.where(cv[None, None], rpx.reshape(NA_HEADS, ndy, GRID_W, GRID_W), NEG)
    neg = jnp.full((NA_HEADS, 1, GRID_W, GRID_W), NEG, F32)
    rpx = jnp.concatenate([rpx, neg], axis=1)
    nxt = jnp.concatenate([rpx[:, 1:], neg], axis=1)
    negs = jnp.broadcast_to(neg, rpx.shape)
    pairs = jnp.concatenate([jnp.concatenate([rpx, nxt], axis=3), jnp.concatenate([rpx, negs], axis=3),
                             jnp.concatenate([negs, rpx], axis=3)], axis=1)
    npair = pairs.shape[1]

    def body(m_ref, o_ref):
        cs = pl.program_id(1)
        r0 = jnp.where(cs == 0, 0, jnp.where(cs == 1, NA_QROWS, rows - NA_QROWS))
        kr0 = jnp.where(cs == 0, 0, jnp.where(cs == 1, NA_QROWS - NA_KH // 2, rows - NA_KROWS))
        for i in range(NA_QROWS):
            r = r0 + i
            rs = jnp.clip(r - NA_KH // 2, 0, rows - NA_KH)
            for jp in range(NA_KROWS // 2):
                kl = kr0 + 2 * jp
                vl = (kl >= rs) & (kl <= rs + NA_KH - 1)
                vr = (kl + 1 >= rs) & (kl + 1 <= rs + NA_KH - 1)
                dyl = jnp.clip(kl - r + NA_KH - 1, 0, ndy)
                dyr = jnp.clip(kl + 1 - r + NA_KH - 1, 0, ndy)
                idx = jnp.where(vl & vr, dyl, jnp.where(vl, 16 + dyl, jnp.where(vr, 32 + dyr, 16 + ndy)))
                o_ref[0, 0, i * GRID_W:(i + 1) * GRID_W, jp * 128:(jp + 1) * 128] = m_ref[0, idx]

    return pl.pallas_call(
        body, name="na_bias_table", grid=(NA_HEADS, 3),
        in_specs=[pl.BlockSpec((1, npair, GRID_W, 128), lambda h, cs: (h, 0, 0, 0))],
        out_specs=pl.BlockSpec((1, 1, NA_QT, NA_KT), lambda h, cs: (h, cs, 0, 0)),
        out_shape=jax.ShapeDtypeStruct((NA_HEADS, 3, NA_QT, NA_KT), F32),
        compiler_params=_cparams(("parallel", "parallel")),
    )(pairs)


def _na_specs(t):
    nb = t // NA_QT
    nkb = t // NA_KCH
    npieces = NA_KT // NA_KCH

    def kb0(b):
        return jnp.clip(b * (NA_QT // NA_KCH) - 1, 0, nkb - npieces)

    def case(b):
        return jnp.where(b == 0, 0, jnp.where(b == nb - 1, 2, 1))

    q_spec = pl.BlockSpec((NA_QT, 128), lambda p, b: (b, p))
    k_specs = [pl.BlockSpec((NA_KCH, 128), lambda p, b, i=i: (kb0(b) + i, 4 + p)) for i in range(npieces)]
    v_specs = [pl.BlockSpec((NA_KCH, 128), lambda p, b, i=i: (kb0(b) + i, 8 + p)) for i in range(npieces)]
    tbl_spec = pl.BlockSpec((2, 1, NA_QT, NA_KT), lambda p, b: (p, case(b), 0, 0))
    io_spec = pl.BlockSpec((NA_QT, 128), lambda p, b: (b, p))
    return nb, npieces, kb0, case, q_spec, k_specs, v_specs, tbl_spec, io_spec


NA_HALF = NA_QT // 2
NA_COMBOS_ALL = tuple((i, 0, NA_QT) for i in range(NA_KT // NA_KCH))
NA_COMBOS_INNER = ((0, 0, NA_HALF),) + tuple((i, 0, NA_QT) for i in range(1, NA_KT // NA_KCH - 1)) \
    + ((NA_KT // NA_KCH - 1, NA_HALF, NA_QT),)


def _na_place(val, r0, r1):
    if (r0, r1) == (0, NA_QT):
        return val
    z = jnp.zeros((NA_HALF, val.shape[1]), val.dtype)
    return jnp.concatenate([val, z] if r0 == 0 else [z, val], axis=0)


def _na_fwd(proj, tbl):
    t = proj.shape[0]
    nb, npieces, _, _, q_spec, k_specs, v_specs, tbl_spec, io_spec = _na_specs(t)
    lse_spec = pl.BlockSpec((1, NA_QT, 2), lambda p, b: (p, b, 0))

    def body(*refs):
        q_ref = refs[0]
        k_refs = refs[1:1 + npieces]
        v_refs = refs[1 + npieces:1 + 2 * npieces]
        tbl_ref, o_ref, lse_ref = refs[1 + 2 * npieces:]
        b = pl.program_id(1)

        def compute(combos):
            lane = lax.broadcasted_iota(jnp.int32, (1, 128), 1)
            qv = q_ref[...].astype(F32) * (NA_HEAD_DIM ** -0.5)
            ks = [r[...].astype(BF16) for r in k_refs]
            vs = [r[...].astype(BF16) for r in v_refs]
            hs = range(2)
            msk = [(lane // NA_HEAD_DIM) == hh for hh in hs]
            qh = [jnp.where(msk[hh], qv, 0.0).astype(BF16) for hh in hs]
            s = [[_nt(qh[hh][r0:r1], ks[i]) + tbl_ref[hh, 0, r0:r1, i * NA_KCH:(i + 1) * NA_KCH]
                  for i, r0, r1 in combos] for hh in hs]
            for h0 in (0, NA_HALF):
                rows = slice(h0, h0 + NA_HALF)
                cover = [(c, i, h0 - r0) for c, (i, r0, r1) in enumerate(combos) if r0 <= h0 < r1]
                part = [[s[hh][c][off:off + NA_HALF] for c, _, off in cover] for hh in hs]
                m = [functools.reduce(jnp.maximum, [jnp.max(v, axis=1, keepdims=True) for v in part[hh]]) for hh in hs]
                p = [[jnp.exp(v - m[hh]) for v in part[hh]] for hh in hs]
                l = [functools.reduce(jnp.add, [jnp.sum(v, axis=1, keepdims=True) for v in p[hh]]) for hh in hs]
                o = [functools.reduce(jnp.add, [_nn(p[hh][k].astype(BF16), vs[i]) for k, (_, i, _) in enumerate(cover)])
                     for hh in hs]
                for hh in hs:
                    lse_ref[0, rows, hh:hh + 1] = m[hh] + jnp.log(l[hh])
                o_ref[rows, :] = jnp.where(msk[0], o[0] / l[0], o[1] / l[1])

        inner = (b > 0) & (b < nb - 1)
        pl.when(inner)(lambda: compute(NA_COMBOS_INNER))
        pl.when(jnp.logical_not(inner))(lambda: compute(NA_COMBOS_ALL))

    return pl.pallas_call(
        body, name="na_fwd", grid=(4, nb),
        in_specs=[q_spec] + k_specs + v_specs + [tbl_spec],
        out_specs=[io_spec, lse_spec],
        out_shape=[jax.ShapeDtypeStruct((t, NA_W), F32), jax.ShapeDtypeStruct((4, t, 2), F32)],
        compiler_params=_cparams(("parallel", "arbitrary")),
    )(*([proj] * (1 + 2 * npieces)), tbl)


def _na_bwd(proj, tbl, d_o, o_na, lse):
    t = proj.shape[0]
    nb, npieces, kb0, case, q_spec, k_specs, v_specs, tbl_spec, io_spec = _na_specs(t)

    def body(*refs):
        q_ref = refs[0]
        k_refs = refs[1:1 + npieces]
        v_refs = refs[1 + npieces:1 + 2 * npieces]
        (tbl_ref, do_ref, o_ref, lse_ref, dq_ref, dk_hbm, dv_hbm, rpb_ref,
         dk_acc, dv_acc, dk_out, dv_out, s_scr, dp_scr, dsb_scr, pnb_scr, sem) = refs[1 + 2 * npieces:]
        p_id = pl.program_id(0)
        b = pl.program_id(1)

        @pl.when(b == 0)
        def _():
            dk_acc[...] = jnp.zeros_like(dk_acc)
            dv_acc[...] = jnp.zeros_like(dv_acc)

        @pl.when((b == 0) | (b == 1) | (b == nb - 1))
        def _():
            rpb_ref[...] = jnp.zeros_like(rpb_ref)

        def compute(combos):
            lane = lax.broadcasted_iota(jnp.int32, (1, 128), 1)
            scale = NA_HEAD_DIM ** -0.5
            qv = q_ref[...].astype(F32) * scale
            ks = [r[...].astype(BF16) for r in k_refs]
            vs = [r[...].astype(BF16) for r in v_refs]
            dov = do_ref[...]
            ov = o_ref[...]
            tok0 = kb0(b) * NA_KCH
            hs = range(2)
            msk = [(lane // NA_HEAD_DIM) == hh for hh in hs]
            qh = [jnp.where(msk[hh], qv, 0.0).astype(BF16) for hh in hs]
            doh = [jnp.where(msk[hh], dov, 0.0) for hh in hs]
            dohb = [doh[hh].astype(BF16) for hh in hs]
            dd = [jnp.sum(doh[hh] * ov, axis=1, keepdims=True) for hh in hs]
            for hh in hs:
                for c, (i, q0, q1) in enumerate(combos):
                    slot = (hh * len(combos) + c) % 2
                    cols = slice(i * NA_KCH, (i + 1) * NA_KCH)
                    s_scr[slot, 0:q1 - q0] = _nt(qh[hh][q0:q1], ks[i])
                    dp_scr[slot, 0:q1 - q0] = _nt(dohb[hh][q0:q1], vs[i])
                    for r0 in range(q0, q1, NA_RC):
                        rows = slice(r0, r0 + NA_RC)
                        loc = slice(r0 - q0, r0 - q0 + NA_RC)
                        p = jnp.exp(s_scr[slot, loc, :] + tbl_ref[hh, 0, rows, cols] - lse_ref[0, rows, hh:hh + 1])
                        d = p * (dp_scr[slot, loc, :] - dd[hh][rows])
                        pnb_scr[hh, rows, cols] = p.astype(BF16)
                        dsb_scr[hh, rows, cols] = d.astype(BF16)
                done = {(i, q0) for i, q0, _ in combos} | {(i, NA_HALF) for i, q0, q1 in combos if q1 - q0 == NA_QT}
                for i in range(npieces):
                    for q0 in (0, NA_HALF):
                        if (i, q0) not in done:
                            dsb_scr[hh, q0:q0 + NA_HALF, i * NA_KCH:(i + 1) * NA_KCH] = jnp.zeros(
                                (NA_HALF, NA_KCH), BF16)
            dqh = [functools.reduce(jnp.add, [_na_place(_nn(dsb_scr[hh, q0:q1, i * NA_KCH:(i + 1) * NA_KCH], ks[i]),
                                                        q0, q1) for i, q0, q1 in combos]) for hh in hs]
            dq_ref[...] = (jnp.where(msk[0], dqh[0], dqh[1]) * scale).astype(BF16)
            for i in range(npieces):
                rows = pl.ds(pl.multiple_of(tok0 + i * NA_KCH, NA_KCH), NA_KCH)
                cols = slice(i * NA_KCH, (i + 1) * NA_KCH)
                q0, q1 = [(a, e) for j, a, e in combos if j == i][0]
                dk_acc[rows, :] += (_tn(dsb_scr[0, q0:q1, cols], qh[0][q0:q1])
                                    + _tn(dsb_scr[1, q0:q1, cols], qh[1][q0:q1]))
                dv_acc[rows, :] += (_tn(pnb_scr[0, q0:q1, cols], dohb[0][q0:q1])
                                    + _tn(pnb_scr[1, q0:q1, cols], dohb[1][q0:q1]))
            for hh in hs:
                acc = dsb_scr[hh, 0:GRID_W, :].astype(F32)
                for i in range(1, NA_QROWS):
                    acc = acc + pltpu.roll(dsb_scr[hh, i * GRID_W:(i + 1) * GRID_W, :].astype(F32),
                                           NA_KT - i * GRID_W, 1)
                rpb_ref[0, 0, hh] += acc

        inner = (b > 0) & (b < nb - 1)
        pl.when(inner)(lambda: compute(NA_COMBOS_INNER))
        pl.when(jnp.logical_not(inner))(lambda: compute(NA_COMBOS_ALL))

        @pl.when(b == nb - 1)
        def _():
            def copies(pair):
                cols = pl.ds(pl.multiple_of(pair * 128, 128), 128)
                return (pltpu.make_async_copy(dk_out, dk_hbm.at[:, cols], sem.at[0]),
                        pltpu.make_async_copy(dv_out, dv_hbm.at[:, cols], sem.at[1]))

            @pl.when(p_id > 0)
            def _():
                for cp in copies(p_id - 1):
                    cp.wait()
            dk_out[...] = dk_acc[...].astype(BF16)
            dv_out[...] = dv_acc[...].astype(BF16)
            for cp in copies(p_id):
                cp.start()

            @pl.when(p_id == NA_HEADS // 2 - 1)
            def _():
                for cp in copies(p_id):
                    cp.wait()

    o512 = jax.ShapeDtypeStruct((t, NA_W), BF16)
    return pl.pallas_call(
        body, name="na_bwd", grid=(4, nb),
        in_specs=[q_spec] + k_specs + v_specs + [tbl_spec, io_spec, io_spec,
                                                 pl.BlockSpec((1, NA_QT, 2), lambda p, b: (p, b, 0))],
        out_specs=[io_spec, pl.BlockSpec(memory_space=pl.ANY), pl.BlockSpec(memory_space=pl.ANY),
                   pl.BlockSpec((1, 1, 2, GRID_W, NA_KT), lambda p, b: (p, case(b), 0, 0, 0))],
        out_shape=[o512, o512, o512, jax.ShapeDtypeStruct((4, 3, 2, GRID_W, NA_KT), F32)],
        scratch_shapes=[pltpu.VMEM((t, 128), F32), pltpu.VMEM((t, 128), F32),
                        pltpu.VMEM((t, 128), BF16), pltpu.VMEM((t, 128), BF16),
                        pltpu.VMEM((2, NA_QT, NA_KCH), F32), pltpu.VMEM((2, NA_QT, NA_KCH), F32),
                        pltpu.VMEM((2, NA_QT, NA_KT), BF16), pltpu.VMEM((2, NA_QT, NA_KT), BF16),
                        pltpu.SemaphoreType.DMA((2,))],
        compiler_params=_cparams(("arbitrary", "arbitrary")),
    )(*([proj] * (1 + 2 * npieces)), tbl, d_o, o_na, lse)


def _rpb_reduce(rpbacc, rows):
    nacc = 4 * 3 * 2

    def shift_body(a_ref, o_ref):
        acc = a_ref[0, 0:1, :]
        for cq in range(1, GRID_W):
            acc = acc + pltpu.roll(a_ref[0, cq:cq + 1, :], NA_KT - cq, 1)
        o_ref[0] = jnp.broadcast_to(acc, (8, NA_KT))

    vec = pl.pallas_call(
        shift_body, name="rpb_shift", grid=(nacc,),
        in_specs=[pl.BlockSpec((1, GRID_W, NA_KT), lambda a: (a, 0, 0))],
        out_specs=pl.BlockSpec((1, 8, NA_KT), lambda a: (a, 0, 0)),
        out_shape=jax.ShapeDtypeStruct((nacc, 8, NA_KT), F32),
        compiler_params=_cparams(("parallel",)),
    )(rpbacc.reshape(nacc, GRID_W, NA_KT))
    a = vec[:, 0].reshape(4, 3, 2, NA_KT).transpose(0, 2, 1, 3).reshape(NA_HEADS, 3, NA_KT)
    if rows // NA_QROWS < 3:
        a = a.at[:, 1].set(0.0)
    dd = np.arange(NA_KROWS)[:, None]
    dxo = np.arange(-(NA_KW - 1), NA_KW)[None, :]
    idx = ((dd * GRID_W + dxo) % NA_KT).reshape(-1)
    g = a[..., idx].reshape(NA_HEADS, 3 * NA_KROWS, 2 * NA_KW - 1)
    g = jnp.pad(g, ((0, 0), (0, 0), (0, 128 - (2 * NA_KW - 1))))
    nmat = np.zeros((16, 3 * NA_KROWS), np.float32)
    for cs, delta in enumerate((0, -(NA_KH // 2), -(NA_KROWS - NA_QROWS))):
        for d in range(NA_KROWS):
            jmi = d - NA_KROWS if (cs == 0 and d > NA_KH - 1) else d
            dy = jmi + delta + NA_KH - 1
            if 0 <= dy <= 2 * NA_KH - 2:
                nmat[dy, cs * NA_KROWS + d] = 1.0

    def body(n_ref, g_ref, o_ref):
        o_ref[0] = jnp.dot(n_ref[...], g_ref[0], precision=HI, preferred_element_type=F32)

    out = pl.pallas_call(
        body, name="rpb_reduce", grid=(NA_HEADS,),
        in_specs=[pl.BlockSpec((16, nmat.shape[1]), lambda h: (0, 0)),
                  pl.BlockSpec((1, nmat.shape[1], 128), lambda h: (h, 0, 0))],
        out_specs=pl.BlockSpec((1, 16, 128), lambda h: (h, 0, 0)),
        out_shape=jax.ShapeDtypeStruct((NA_HEADS, 16, 128), F32),
        compiler_params=_cparams(("parallel",)),
    )(jnp.asarray(nmat), g)
    return out[:, :2 * NA_KH - 1, :2 * NA_KW - 1]


def _halo_specs(tm, t, col, width=1024):
    nth = t // CONV_HALO
    per = tm // CONV_HALO
    return [pl.BlockSpec((tm, width), lambda i: (i, col)),
            pl.BlockSpec((CONV_HALO, width), lambda i: (jnp.maximum(i * per - 1, 0), col)),
            pl.BlockSpec((CONV_HALO, width), lambda i: (jnp.minimum((i + 1) * per, nth - 1), col))]


def _fill_ext(ext, cur_ref, prev_ref, next_ref, tm, nt):
    i = pl.program_id(0)
    hl = CONV_HALO
    ext[0:hl, :] = jnp.where(i == 0, 0.0, prev_ref[...].astype(F32))
    ext[hl:hl + tm, :] = cur_ref[...].astype(F32)
    ext[hl + tm:2 * hl + tm, :] = jnp.where(i == nt - 1, 0.0, next_ref[...].astype(F32))


CONV_HALO = 16
CONV_RC = 16
CONV_CB = 512


def _conv_chunks(tm):
    return [(slice(cb, cb + CONV_CB), slice(rb, rb + CONV_RC))
            for cb in range(0, 1024, CONV_CB) for rb in range(0, tm, CONV_RC)]


def _conv_masks():
    rows = lax.broadcasted_iota(jnp.int32, (8, CONV_CB), 0)
    return {off: (rows >= off if off > 0 else rows < 8 + off)
            for off in range(-(CONV_W // 2), CONV_W // 2 + 1) if off != 0}


def _conv_windows(ext, row0, cs, masks):
    ng = CONV_RC // 8
    grp = [ext[pl.ds(row0 + 8 * (g - 1), 8), cs] for g in range(ng + 2)]
    wins = []
    for j in range(CONV_W):
        off = j - CONV_W // 2
        if off == 0:
            parts = grp[1:ng + 1]
        elif off > 0:
            parts = [pltpu.roll(jnp.where(masks[off], grp[g + 1], grp[g + 2]), 8 - off, axis=0) for g in range(ng)]
        else:
            parts = [pltpu.roll(jnp.where(masks[off], grp[g + 1], grp[g]), -off, axis=0) for g in range(ng)]
        wins.append(jnp.concatenate(parts, axis=0))
    return wins


def _conv_fwd(proj, conv_w8, conv_b, tm):
    t = proj.shape[0]
    nt = t // tm

    def body(u_ref, up_ref, un_ref, w_ref, b_ref, pre_ref, act_ref, ext):
        _fill_ext(ext, u_ref, up_ref, un_ref, tm, nt)
        masks = _conv_masks()
        for cs, rs in _conv_chunks(tm):
            wins = _conv_windows(ext, rs.start + CONV_HALO, cs, masks)
            pre = b_ref[:, cs] + w_ref[0:1, cs] * wins[0]
            for j in range(1, CONV_W):
                pre = pre + w_ref[j:j + 1, cs] * wins[j]
            pre_ref[rs, cs] = pre
            act_ref[rs, cs] = _silu(pre)

    full = pl.BlockSpec((tm, 1024), lambda i: (i, 0))
    o = jax.ShapeDtypeStruct((t, 1024), F32)
    return pl.pallas_call(
        body, name="conv_fwd", grid=(nt,),
        in_specs=_halo_specs(tm, t, 2) + [pl.BlockSpec((8, 1024), lambda i: (0, 0)), _row(1024)],
        out_specs=[full, full], out_shape=[o, o],
        scratch_shapes=[pltpu.VMEM((tm + 2 * CONV_HALO, 1024), F32)],
        compiler_params=_cparams(("parallel",)),
    )(proj, proj, proj, conv_w8, conv_b)


def _conv_bwd(dq, dk, pre, proj, conv_w8, tm):
    t = pre.shape[0]
    nt = t // tm

    def body(dq_ref, dqp_ref, dqn_ref, dk_ref, dkp_ref, dkn_ref, pre_ref, prep_ref, pren_ref,
             u_ref, up_ref, un_ref, w_ref, du_ref, gw_ref, gb_ref, extd, extu):
        i = pl.program_id(0)
        hl = CONV_HALO

        @pl.when(i == 0)
        def _():
            gw_ref[...] = jnp.zeros_like(gw_ref)
            gb_ref[...] = jnp.zeros_like(gb_ref)
        for rows, dqr, dkr, prr, edge in ((slice(0, hl), dqp_ref, dkp_ref, prep_ref, i == 0),
                                          (slice(hl, hl + tm), dq_ref, dk_ref, pre_ref, None),
                                          (slice(hl + tm, 2 * hl + tm), dqn_ref, dkn_ref, pren_ref, i == nt - 1)):
            ds = _dsilu(prr[...])
            dl = dqr[...] * ds[:, 0:ML_W]
            dr = dkr[...] * ds[:, ML_W:]
            if edge is not None:
                dl = jnp.where(edge, 0.0, dl)
                dr = jnp.where(edge, 0.0, dr)
            extd[rows, 0:ML_W] = dl
            extd[rows, ML_W:] = dr
        _fill_ext(extu, u_ref, up_ref, un_ref, tm, nt)
        gb_ref[...] += jnp.sum(extd[hl:hl + tm, :], axis=0, keepdims=True)
        gacc = None
        masks = _conv_masks()
        for cs, rs in _conv_chunks(tm):
            if rs.start == 0:
                gacc = [jnp.zeros((8, CONV_CB), F32) for _ in range(CONV_W)]
            wd = _conv_windows(extd, rs.start + hl, cs, masks)
            wu = _conv_windows(extu, rs.start + hl, cs, masks)
            du = w_ref[0:1, cs] * wd[CONV_W - 1]
            for j in range(1, CONV_W):
                du = du + w_ref[j:j + 1, cs] * wd[CONV_W - 1 - j]
            du_ref[rs, cs] = du.astype(BF16)
            dcur = wd[CONV_W // 2]
            for j in range(CONV_W):
                prod = dcur * wu[j]
                gacc[j] = gacc[j] + functools.reduce(
                    jnp.add, [prod[k:k + 8] for k in range(0, CONV_RC, 8)])
            if rs.stop == tm:
                for j in range(CONV_W):
                    gw_ref[j:j + 1, cs] += jnp.sum(gacc[j], axis=0, keepdims=True)

    full = pl.BlockSpec((tm, 1024), lambda i: (i, 0))
    return pl.pallas_call(
        body, name="conv_bwd", grid=(nt,),
        in_specs=_halo_specs(tm, t, 0, ML_W) + _halo_specs(tm, t, 0, ML_W) + _halo_specs(tm, t, 0)
        + _halo_specs(tm, t, 2) + [pl.BlockSpec((8, 1024), lambda i: (0, 0))],
        out_specs=[full, pl.BlockSpec((8, 1024), lambda i: (0, 0)), _row(1024)],
        out_shape=[jax.ShapeDtypeStruct((t, 1024), BF16), jax.ShapeDtypeStruct((8, 1024), F32),
                   jax.ShapeDtypeStruct((1, 1024), F32)],
        scratch_shapes=[pltpu.VMEM((tm + 2 * CONV_HALO, 1024), F32), pltpu.VMEM((tm + 2 * CONV_HALO, 1024), F32)],
        compiler_params=_cparams(("arbitrary",)),
    )(dq, dq, dq, dk, dk, dk, pre, pre, pre, proj, proj, proj, conv_w8)


def _ml_consts(rev):
    iu = lax.broadcasted_iota(jnp.int32, (ML_CHUNK, ML_CHUNK), 0)
    js = lax.broadcasted_iota(jnp.int32, (ML_CHUNK, ML_CHUNK), 1)
    eye = iu == js
    le = iu <= js
    ge = iu >= js
    csum, csum_t, sees = (ge, le, ge) if rev else (le, ge, le)
    return eye, csum.astype(F32), csum_t.astype(F32), sees


def _col(row, eye):
    return jnp.sum(jnp.where(eye, row, 0.0), axis=1, keepdims=True)


def _rowof(col, eye):
    return jnp.sum(jnp.where(eye, col, 0.0), axis=0, keepdims=True)


def _row8(row):
    top = lax.broadcasted_iota(jnp.int32, (8, row.shape[1]), 0) == 0
    return jnp.where(top, row, jnp.zeros_like(row))


def _outer_rows(a_row, b_row_bf16):
    hi = a_row.astype(BF16)
    lo = (a_row - hi.astype(F32)).astype(BF16)
    r_a = lax.broadcasted_iota(jnp.int32, (8, a_row.shape[1]), 0)
    r_b = lax.broadcasted_iota(jnp.int32, (8, b_row_bf16.shape[1]), 0)
    lhs = jnp.where(r_a == 0, hi, jnp.where(r_a == 1, lo, jnp.zeros_like(hi)))
    rhs = jnp.where(r_b < 2, b_row_bf16, jnp.zeros_like(b_row_bf16))
    return _tn(lhs, rhs)


def _ml_gates(gi, gf, m0, csum, rev):
    lf = jax.nn.log_sigmoid(gf)
    b_rows = jnp.dot(lf, csum, precision=HI, preferred_element_type=F32)
    bl = jnp.sum(lf, axis=1, keepdims=True)
    a_rows = bl - b_rows + gi
    mloc = jnp.max(a_rows, axis=1, keepdims=True)
    order = list(range(ML_NB))[::-1] if rev else list(range(ML_NB))
    mp, mn, decay = {}, {}, {}
    m = m0
    for n in order:
        mp[n] = m
        m = jnp.maximum(bl[n:n + 1] + m, mloc[n:n + 1])
        mn[n] = m
    for n in order:
        decay[n] = jnp.exp(bl[n:n + 1] + mp[n] - mn[n])
    return b_rows, a_rows, gi - b_rows, mp, mn, decay, order


def _ml_load(q_ref, k_ref, v_ref, n):
    sl = slice(n * ML_CHUNK, (n + 1) * ML_CHUNK)
    qb = q_ref[sl, :].astype(BF16)
    kb = (k_ref[sl, :] * (ML_HEAD_DIM ** -0.5)).astype(BF16)
    vn = v_ref[sl, :].astype(F32)
    return sl, qb, kb, vn


def _ml_state_scan(q_ref, k_ref, v_ref, a_rows, mn, decay, order, c0, n0):
    ns = range(ML_NB)
    ld = [_ml_load(q_ref, k_ref, v_ref, n) for n in ns]
    vt = [ld[n][3].T for n in ns]
    w_row = [jnp.exp(a_rows[n:n + 1] - mn[n]) for n in ns]
    u = [_nn((vt[n] * w_row[n]).astype(BF16), ld[n][2]) for n in ns]
    nu = [_nn(_row8(w_row[n]).astype(BF16), ld[n][2])[0:1] for n in ns]
    cp, npv = {}, {}
    c, nv = c0, n0
    for n in order:
        cp[n], npv[n] = c, nv
        c = decay[n] * c + u[n]
        nv = decay[n] * nv + nu[n]
    return ld, vt, cp, npv, w_row, c, nv


def _ml_intra_all(ld, vt, b_rows, imb_rows, mp, cp, npv, sees, eye):
    ns = range(ML_NB)
    qk = [_nt(ld[n][2], ld[n][1]) for n in ns]
    cq = [_nt(cp[n].astype(BF16), ld[n][1]) for n in ns]
    qn = [_nt(_row8(npv[n]).astype(BF16), ld[n][1])[0:1] for n in ns]
    imb_col = [_col(imb_rows[n:n + 1], eye) for n in ns]
    dlog = [jnp.where(sees, b_rows[n:n + 1] + imb_col[n], NEG) for n in ns]
    m_inter = [b_rows[n:n + 1] + mp[n] for n in ns]
    m_t = [jnp.maximum(m_inter[n], jnp.max(dlog[n], axis=0, keepdims=True)) for n in ns]
    pm = [jnp.exp(dlog[n] - m_t[n]) for n in ns]
    inter = [jnp.exp(m_inter[n] - m_t[n]) for n in ns]
    floor = [jnp.exp(-m_t[n]) for n in ns]
    s = [qk[n] * pm[n] for n in ns]
    sv = [_nn(vt[n].astype(BF16), s[n].astype(BF16)) for n in ns]
    den = [jnp.sum(s[n], axis=0, keepdims=True) + inter[n] * qn[n] for n in ns]
    num = [sv[n] + inter[n] * cq[n] for n in ns]
    dn = [jnp.maximum(jnp.abs(den[n]), floor[n]) for n in ns]
    return [dict(pm=pm[n], s=s[n], inter=inter[n], cq=cq[n], qn=qn[n], num=num[n], den=den[n],
                 floor=floor[n], dn=dn[n]) for n in ns]


def _ml_specs(t, rev):
    nblk = t // ML_TB
    blk = (lambda g: nblk - 1 - g) if rev else (lambda g: g)
    hps = ML_HPS
    tile = lambda c0: pl.BlockSpec((ML_TB, 128 * hps), lambda hg, g, c0=c0: (blk(g), c0 // hps + hg))
    gate = pl.BlockSpec((hps, ML_NB, ML_CHUNK), lambda hg, g: (hg, blk(g), 0))
    cchk = pl.BlockSpec((hps, 1, 128, 128), lambda hg, g: (hg, blk(g), 0, 0))
    nmchk = pl.BlockSpec((hps, 1, 8, 128), lambda hg, g: (hg, blk(g), 0, 0))
    return nblk, blk, tile, gate, cchk, nmchk


def _ml_head_views(refs, hh):
    cols = slice(hh * ML_HEAD_DIM, (hh + 1) * ML_HEAD_DIM)
    return [r.at[:, cols] if len(r.shape) == 2 else r.at[hh] for r in refs]


def _ml_fwd(qk_act, proj, gi, gf, rev, name):
    t = qk_act.shape[0]
    nblk, _, tile, gate, cchk, nmchk = _ml_specs(t, rev)

    def body(*refs):
        for hh in range(ML_HPS):
            one_head(*_ml_head_views(refs, hh))

    def one_head(q_ref, k_ref, v_ref, gi_ref, gf_ref, h_ref, cchk_ref, nmchk_ref, c_ref, nm_ref):
        @pl.when(pl.program_id(1) == 0)
        def _():
            c_ref[...] = jnp.zeros_like(c_ref)
            nm_ref[...] = jnp.zeros_like(nm_ref)
        cchk_ref[0] = c_ref[...]
        nmchk_ref[0] = nm_ref[...]
        eye, csum, _, sees = _ml_consts(rev)
        b_rows, a_rows, imb_rows, mp, mn, decay, order = _ml_gates(
            gi_ref[...], gf_ref[...], nm_ref[1:2, 0:1], csum, rev)
        ld, vt, cp, npv, _, c, nv = _ml_state_scan(q_ref, k_ref, v_ref, a_rows, mn, decay, order,
                                                   c_ref[...], nm_ref[0:1, :])
        c_ref[...] = c
        nm_ref[0:1, :] = nv
        nm_ref[1:2, :] = jnp.broadcast_to(mn[order[-1]], (1, 128))
        rs = _ml_intra_all(ld, vt, b_rows, imb_rows, mp, cp, npv, sees, eye)
        ht = [rs[n]['num'] / rs[n]['dn'] for n in range(ML_NB)]
        for n in range(ML_NB):
            h_ref[n * ML_CHUNK:(n + 1) * ML_CHUNK, :] = ht[n].T

    return pl.pallas_call(
        body, name=name, grid=(ML_HEADS // ML_HPS, nblk),
        in_specs=[tile(0), tile(4), tile(24), gate, gate],
        out_specs=[tile(0), cchk, nmchk],
        out_shape=[jax.ShapeDtypeStruct((t, ML_W), F32),
                   jax.ShapeDtypeStruct((ML_HEADS, nblk, 128, 128), F32),
                   jax.ShapeDtypeStruct((ML_HEADS, nblk, 8, 128), F32)],
        scratch_shapes=[pltpu.VMEM((ML_HPS, 128, 128), F32), pltpu.VMEM((ML_HPS, 8, 128), F32)],
        compiler_params=_cparams(("parallel", "arbitrary")),
    )(qk_act, qk_act, proj, gi, gf)


def _ml_bwd(qk_act, proj, gi, gf, dh, cchk_a, nmchk_a, prev, rev, name):
    t = qk_act.shape[0]
    nblk, _, tile, gate, cchk, nmchk = _ml_specs(t, not rev)

    def body(*refs):
        for hh in range(ML_HPS):
            one_head(*_ml_head_views(refs, hh))

    def one_head(q_ref, k_ref, v_ref, gi_ref, gf_ref, dh_ref, cchk_ref, nmchk_ref, *rest):
        prev_refs = rest[:len(prev)]
        dq_ref, dk_ref, dv_ref, dgi_ref, dgf_ref, dc_ref, dn_ref, db_scr, dbl_scr, di_scr = rest[len(prev):]

        def plus_prev(val, which, rows):
            return val + prev_refs[which][rows, :] if prev else val

        @pl.when(pl.program_id(1) == 0)
        def _():
            dc_ref[...] = jnp.zeros_like(dc_ref)
            dn_ref[...] = jnp.zeros_like(dn_ref)
        eye, csum, csum_t, sees = _ml_consts(rev)
        gfv = gf_ref[...]
        b_rows, a_rows, imb_rows, mp, mn, decay, order = _ml_gates(
            gi_ref[...], gfv, nmchk_ref[0, 1:2, 0:1], csum, rev)
        ld, vt, cp, npv, w_row, _, _ = _ml_state_scan(q_ref, k_ref, v_ref, a_rows, mn, decay, order,
                                                      cchk_ref[0], nmchk_ref[0, 0:1, :])
        ns = range(ML_NB)
        rs = _ml_intra_all(ld, vt, b_rows, imb_rows, mp, cp, npv, sees, eye)
        sls = [ld[n][0] for n in ns]
        qbs = [ld[n][1] for n in ns]
        kbs = [ld[n][2] for n in ns]
        vbs = [ld[n][3].astype(BF16) for n in ns]
        rdn = [1.0 / rs[n]['dn'] for n in ns]
        dnum = [dh_ref[sls[n], :].T * rdn[n] for n in ns]
        hsum = [jnp.sum(dnum[n] * rs[n]['num'], axis=0, keepdims=True) for n in ns]
        dden = [jnp.where(jnp.abs(rs[n]['den']) > rs[n]['floor'],
                          -hsum[n] * rdn[n] * jnp.sign(rs[n]['den']), 0.0) for n in ns]
        dnb = [dnum[n].astype(BF16) for n in ns]
        dsf = [_nn(vbs[n], dnb[n]) + dden[n] for n in ns]
        dv0 = [_nt(rs[n]['s'].astype(BF16), dnb[n]) for n in ns]
        gb = [(dsf[n] * rs[n]['pm']).astype(BF16) for n in ns]
        cpb = [cp[n].astype(BF16) for n in ns]
        idd = [rs[n]['inter'] * dden[n] for n in ns]
        idn = [(rs[n]['inter'] * dnum[n]).astype(BF16) for n in ns]
        dqa = [_tn(gb[n], kbs[n]) for n in ns]
        dqc = [_tn(idn[n], cpb[n]) for n in ns]
        dqn = [_outer_rows(idd[n], npv[n].astype(BF16)) for n in ns]
        dk0 = [_nn(gb[n], qbs[n]) for n in ns]
        xs = [_nn(idn[n], qbs[n]) for n in ns]
        for n in ns:
            dq_ref[sls[n], :] = plus_prev(dqa[n] + dqc[n] + dqn[n], 0, sls[n])
        rr = [dsf[n] * rs[n]['s'] for n in ns]
        dinter = [jnp.sum(dnum[n] * rs[n]['cq'], axis=0, keepdims=True) + dden[n] * rs[n]['qn'] for n in ns]
        dbt = [jnp.sum(rr[n], axis=0, keepdims=True) + dinter[n] * rs[n]['inter'] for n in ns]
        dimb = [jnp.sum(rr[n], axis=1, keepdims=True) for n in ns]
        xns = [_nn(_row8(idd[n]).astype(BF16), qbs[n])[0:1] for n in ns]
        dcn, dnn = {}, {}
        dc, dn = dc_ref[...], dn_ref[0:1, :]
        for n in order[::-1]:
            dcn[n], dnn[n] = dc, dn
            dc = decay[n] * dc + xs[n]
            dn = decay[n] * dn + xns[n]
        dc_ref[...] = dc
        dn_ref[0:1, :] = dn
        kscale = ML_HEAD_DIM ** -0.5
        dcb = [dcn[n].astype(BF16) for n in ns]
        z = [_nn(vbs[n], dcb[n]) for n in ns]
        kd = [_nt(kbs[n], dcb[n]) for n in ns]
        ddecay = [jnp.sum(jnp.sum(dcn[n] * cp[n], axis=1, keepdims=True), axis=0, keepdims=True)
                  + jnp.sum(dnn[n] * npv[n], axis=1, keepdims=True) for n in ns]
        zd = [z[n] + dnn[n] for n in ns]
        dw = [jnp.sum(zd[n] * kbs[n].astype(F32), axis=1, keepdims=True) for n in ns]
        wcol = [_col(w_row[n], eye) for n in ns]
        for n in ns:
            dv_ref[sls[n], :] = plus_prev(dv0[n] + wcol[n] * kd[n], 2, sls[n]).astype(dv_ref.dtype)
            dk_ref[sls[n], :] = plus_prev((dk0[n] + wcol[n] * zd[n]) * kscale, 1, sls[n])
        da = [dw[n] * wcol[n] for n in ns]
        dbl = [jnp.sum(da[n], axis=0, keepdims=True) + ddecay[n] * decay[n] for n in ns]
        key_row = [_rowof(dimb[n] + da[n], eye) for n in ns]
        for n in ns:
            db_scr[n:n + 1, :] = dbt[n] - key_row[n]
            di_scr[n:n + 1, :] = key_row[n]
            dbl_scr[n:n + 1, :] = jnp.broadcast_to(dbl[n], (1, ML_CHUNK))
        dlf = jnp.dot(db_scr[...], csum_t, precision=HI, preferred_element_type=F32) + dbl_scr[...]
        dgf_ref[...] = dlf * jax.nn.sigmoid(-gfv)
        dgi_ref[...] = di_scr[...]

    nc = t // ML_CHUNK
    o512 = jax.ShapeDtypeStruct((t, ML_W), F32)
    og = jax.ShapeDtypeStruct((ML_HEADS, nc, ML_CHUNK), F32)
    return pl.pallas_call(
        body, name=name, grid=(ML_HEADS // ML_HPS, nblk),
        in_specs=[tile(0), tile(4), tile(24), gate, gate, tile(0), cchk, nmchk] + [tile(0)] * len(prev),
        out_specs=[tile(0), tile(0), tile(0), gate, gate],
        out_shape=[o512, o512, jax.ShapeDtypeStruct((t, ML_W), BF16 if prev else F32), og, og],
        scratch_shapes=[pltpu.VMEM((ML_HPS, 128, 128), F32), pltpu.VMEM((ML_HPS, 8, 128), F32)]
        + [pltpu.VMEM((ML_HPS, ML_NB, ML_CHUNK), F32)] * 3,
        compiler_params=_cparams(("parallel", "arbitrary")),
    )(qk_act, qk_act, proj, gi, gf, dh, cchk_a, nmchk_a, *prev)


def _gate_rows(gates16, t):
    g = gates16.reshape(t // ML_CHUNK, ML_CHUNK, 4, ML_HEADS).transpose(2, 3, 0, 1)
    return g[0], g[1], g[2], g[3]


def _gate_cols(dgi_f, dgf_f, dgi_b, dgf_b, t):
    g = jnp.stack([dgi_f, dgf_f, dgi_b, dgf_b]).transpose(2, 3, 0, 1).reshape(t, 4 * ML_HEADS)
    return jnp.pad(g, ((0, 0), (0, 128 - 4 * ML_HEADS)))


def _local_step(x, target, shift, scale, gate, norm_w, w_in_t, b_in_p, conv_w8, conv_b, rpb,
                ml_norm_w, w_out_b, final_norm_w):
    t = x.shape[0]
    rows = t // GRID_W
    tm = 512
    proj, gates = _in_proj(x, norm_w, scale, shift, w_in_t, b_in_p)
    tbl = _na_bias_table(rpb, rows)
    o_na, lse_na = _na_fwd(proj, tbl)
    pre, qk_act = _conv_fwd(proj, conv_w8, conv_b, 2 * tm)
    gi_f, gf_f, gi_b, gf_b = _gate_rows(gates[:, :4 * ML_HEADS], t)
    h_f, cchk_f, nmchk_f = _ml_fwd(qk_act, proj, gi_f, gf_f, False, "ml_fwd_f")
    h_b, cchk_b, nmchk_b = _ml_fwd(qk_act, proj, gi_b, gf_b, True, "ml_fwd_b")
    (loss, dres, d_ona, d_naz, dhs, d_o, d_z, dgate, g_fnw, g_mlnw, g_w_out) = _tail(
        o_na, proj, h_f, h_b, x, target, gate, ml_norm_w, final_norm_w, w_out_b)
    dq_na, dk_na, dv_na, rpbacc = _na_bwd(proj, tbl, d_ona, o_na, lse_na)
    g_rpb = _rpb_reduce(rpbacc, rows)
    dq_f, dk_f, dv_f, dgi_f, dgf_f = _ml_bwd(qk_act, proj, gi_f, gf_f, dhs, cchk_f, nmchk_f, (),
                                             False, "ml_bwd_f")
    dq_ml, dk_ml, dv_ml, dgi_b, dgf_b = _ml_bwd(qk_act, proj, gi_b, gf_b, dhs, cchk_b, nmchk_b, (dq_f, dk_f, dv_f),
                                                True, "ml_bwd_b")
    du, g_conv_w, g_conv_b = _conv_bwd(dq_ml, dk_ml, pre, proj, conv_w8, tm)
    dgates = _gate_cols(dgi_f, dgf_f, dgi_b, dgf_b, t)
    grad_x, g_w_in, g_b_in, dscale, dshift, g_nw = _in_bwd(
        [dq_na, dk_na, dv_na, d_naz, du, dv_ml, d_o, d_z, dgates], x, dres, w_in_t, norm_w, scale, shift)
    dmod = jnp.concatenate([dshift, dscale, dgate], axis=1)
    return (loss, grad_x, dmod, g_nw, g_w_in, g_b_in, g_conv_w, g_conv_b, g_rpb, g_mlnw, g_w_out, g_fnw)


MESH = pl.DeviceIdType.MESH
N_DEV = 8
ANY = pl.BlockSpec(memory_space=pl.ANY)
WHOLE_VMEM = pl.BlockSpec(memory_space=pltpu.VMEM)


def _allgather8(blocks, name):
    na = len(blocks)

    def body(*refs):
        x_refs = refs[:na]
        out_refs = refs[na:2 * na]
        send_sems, recv_sems, local_sems = refs[2 * na:]
        x, y, c = lax.axis_index("x"), lax.axis_index("y"), lax.axis_index("c")
        me, sibling = (x, y, c), (x, y, 1 - c)
        chips = [(1 - x, y), (x, 1 - y), (1 - x, 1 - y)]

        def rows(a, px, py, pc):
            return out_refs[a].at[4 * px + 2 * py + pc]

        def copy(a, k, block, to, src=None):
            return pltpu.make_async_remote_copy(
                src_ref=rows(a, *block) if src is None else src, dst_ref=rows(a, *block),
                send_sem=send_sems.at[a, k], recv_sem=recv_sems.at[a, k],
                device_id=to, device_id_type=MESH)

        mine, first, passed = [], [], []
        for a in range(na):
            cp = pltpu.make_async_copy(x_refs[a], rows(a, *me), local_sems.at[a])
            cp.start()
            mine.append(cp)
            first.append(copy(a, 0, me, sibling, src=x_refs[a]))
            first += [copy(a, 1 + j, me, (*chip, c), src=x_refs[a]) for j, chip in enumerate(chips)]
        for cp in first:
            cp.start()
        for a in range(na):
            for j, chip in enumerate(chips):
                copy(a, 1 + j, (*chip, c), me).wait_recv()
                fwd = copy(a, 4 + j, (*chip, c), sibling)
                fwd.start()
                passed.append(fwd)
        for a in range(na):
            copy(a, 0, sibling, me).wait_recv()
            for j, chip in enumerate(chips):
                copy(a, 4 + j, (*chip, 1 - c), me).wait_recv()
        for cp in first + passed:
            cp.wait_send()
        for cp in mine:
            cp.wait()

    return pl.pallas_call(
        body, name=name,
        out_shape=[jax.ShapeDtypeStruct((N_DEV,) + b.shape, b.dtype) for b in blocks],
        in_specs=[WHOLE_VMEM] * na, out_specs=[WHOLE_VMEM] * na,
        scratch_shapes=[pltpu.SemaphoreType.DMA((na, 7)), pltpu.SemaphoreType.DMA((na, 7)),
                        pltpu.SemaphoreType.DMA((na,))],
        compiler_params=pltpu.CompilerParams(vmem_limit_bytes=VMEM_LIMIT),
    )(*blocks)


def _pair_exchange(arrs, name):
    na = len(arrs)

    def body(*refs):
        in_refs = refs[:na]
        out_refs = refs[na:2 * na]
        send_sems, recv_sems = refs[2 * na:]
        sibling = (lax.axis_index("x"), lax.axis_index("y"), 1 - lax.axis_index("c"))
        copies = [pltpu.make_async_remote_copy(
            src_ref=in_refs[a], dst_ref=out_refs[a], send_sem=send_sems.at[a], recv_sem=recv_sems.at[a],
            device_id=sibling, device_id_type=MESH) for a in range(na)]
        for cp in copies:
            cp.start()
        for cp in copies:
            cp.wait()

    return pl.pallas_call(
        body, name=name,
        out_shape=[jax.ShapeDtypeStruct(a.shape, a.dtype) for a in arrs],
        in_specs=[ANY] * na, out_specs=[ANY] * na,
        scratch_shapes=[pltpu.SemaphoreType.DMA((na,)), pltpu.SemaphoreType.DMA((na,))],
    )(*arrs)


def _chip_exchange(arrs, name):
    na = len(arrs)

    def body(*refs):
        in_refs = refs[:na]
        out_refs = refs[na:2 * na]
        send_sems, recv_sems, local_sems = refs[2 * na:]
        x, y, c = lax.axis_index("x"), lax.axis_index("y"), lax.axis_index("c")
        my_chip = 2 * x + y
        chips = [(1 - x, y), (x, 1 - y), (1 - x, 1 - y)]
        local, remote = [], []
        for a in range(na):
            cp = pltpu.make_async_copy(in_refs[a].at[my_chip], out_refs[a].at[my_chip], local_sems.at[a])
            cp.start()
            local.append(cp)
            for j, (px, py) in enumerate(chips):
                cp = pltpu.make_async_remote_copy(
                    src_ref=in_refs[a].at[2 * px + py], dst_ref=out_refs[a].at[my_chip],
                    send_sem=send_sems.at[a, j], recv_sem=recv_sems.at[a, j],
                    device_id=(px, py, c), device_id_type=MESH)
                cp.start()
                remote.append(cp)
        for cp in remote:
            cp.wait()
        for cp in local:
            cp.wait()

    return pl.pallas_call(
        body, name=name,
        out_shape=[jax.ShapeDtypeStruct(a.shape, a.dtype) for a in arrs],
        in_specs=[ANY] * na, out_specs=[ANY] * na,
        scratch_shapes=[pltpu.SemaphoreType.DMA((na, 3)), pltpu.SemaphoreType.DMA((na, 3)),
                        pltpu.SemaphoreType.DMA((na,))],
    )(*arrs)


def _rows_tile(r):
    for cand in (512, 256, 128, 64, 32, 16, 8):
        if r % cand == 0:
            return cand
    return r


def _add2(a, b, name, out_dtype):
    s, r, n = a.shape
    tr = _rows_tile(r)

    def body(a_ref, b_ref, o_ref):
        o_ref[...] = (a_ref[...] + b_ref[...]).astype(out_dtype)

    spec = pl.BlockSpec((1, tr, n), lambda i, j: (i, j, 0))
    return pl.pallas_call(
        body, name=name, grid=(s, r // tr), in_specs=[spec, spec], out_specs=spec,
        out_shape=jax.ShapeDtypeStruct(a.shape, out_dtype),
        compiler_params=_cparams(("parallel", "parallel")),
    )(a, b)


def _sum_slabs(a, name):
    s, r, n = a.shape
    tr = _rows_tile(r)

    def body(a_ref, o_ref):
        acc = a_ref[0].astype(F32)
        for k in range(1, s):
            acc = acc + a_ref[k].astype(F32)
        o_ref[...] = acc

    return pl.pallas_call(
        body, name=name, grid=(r // tr,),
        in_specs=[pl.BlockSpec((s, tr, n), lambda i: (0, i, 0))],
        out_specs=pl.BlockSpec((tr, n), lambda i: (i, 0)),
        out_shape=jax.ShapeDtypeStruct((r, n), F32),
        compiler_params=_cparams(("parallel",)),
    )(a)


ADAMW_WHOLE = 64 * 1024


def _adamw(w, g, m, v, name):
    r, n = w.shape
    if r * n <= ADAMW_WHOLE:
        blk, grid, imap = (r, n), (1,), (lambda i: (0, 0))
    elif r % 8 == 0:
        blk, grid, imap = (_rows_tile(r), n), (r // _rows_tile(r),), (lambda i: (i, 0))
    else:
        blk, grid, imap = (r, 128), (n // 128,), (lambda i: (0, i))
    c1 = 1.0 / (1.0 - ADAM_B1 ** ADAM_STEP)
    c2 = 1.0 / (1.0 - ADAM_B2 ** ADAM_STEP)

    def body(w_ref, g_ref, m_ref, v_ref, d_ref, nm_ref, nv_ref):
        gv = g_ref[...]
        nm = ADAM_B1 * m_ref[...] + (1.0 - ADAM_B1) * gv
        nv = ADAM_B2 * v_ref[...] + (1.0 - ADAM_B2) * (gv * gv)
        nm_ref[...] = nm
        nv_ref[...] = nv
        d_ref[...] = -ADAM_LR * ((nm * c1) / (jnp.sqrt(nv * c2) + ADAM_EPS) + ADAM_WD * w_ref[...])

    spec = pl.BlockSpec(blk, imap)
    o = jax.ShapeDtypeStruct((r, n), F32)
    return pl.pallas_call(
        body, name=name, grid=grid, in_specs=[spec] * 4, out_specs=[spec] * 3, out_shape=[o, o, o],
        compiler_params=_cparams(("parallel",)),
    )(w, g, m, v)


def _mod_fwd(c_all, w_ada_s, b_ada_s):
    def body(c_ref, w_ref, b_ref, o_ref):
        o_ref[...] = jnp.dot(_silu(c_ref[...]), w_ref[...], precision=HI, preferred_element_type=F32) + b_ref[...]

    return pl.pallas_call(
        body, name="mod_fwd", out_shape=jax.ShapeDtypeStruct((c_all.shape[0], w_ada_s.shape[1]), F32),
        in_specs=[WHOLE_VMEM] * 3, out_specs=WHOLE_VMEM,
        compiler_params=pltpu.CompilerParams(vmem_limit_bytes=VMEM_LIMIT),
    )(c_all, w_ada_s, b_ada_s)


def _wada_grad(c_all, dmod_s):
    def body(c_ref, d_ref, o_ref):
        o_ref[...] = lax.dot_general(_silu(c_ref[...]), d_ref[...], (((0,), (0,)), ((), ())),
                                     precision=HI, preferred_element_type=F32)

    return pl.pallas_call(
        body, name="w_ada_grad", out_shape=jax.ShapeDtypeStruct((c_all.shape[1], dmod_s.shape[1]), F32),
        in_specs=[WHOLE_VMEM] * 2, out_specs=WHOLE_VMEM,
        compiler_params=pltpu.CompilerParams(vmem_limit_bytes=VMEM_LIMIT),
    )(c_all, dmod_s)


SMALL_ROWS = 24


def _pad_rows(v, nrows):
    v = v.reshape(-1)
    return jnp.pad(v, (0, nrows * 1024 - v.shape[0])).reshape(nrows, 1024)


def _pack_small(b_ada, norm_w, b_in, conv_w_full, conv_b, rpb, ml_norm_w, final_norm_w, last):
    parts = [_pad_rows(b_ada, 3), _pad_rows(norm_w, 1), _pad_rows(b_in, 5), _pad_rows(conv_w_full, 5),
             _pad_rows(conv_b, 1), _pad_rows(rpb, 4), _pad_rows(ml_norm_w, 1), _pad_rows(final_norm_w, 1),
             _pad_rows(last, 3)]
    return jnp.concatenate(parts, axis=0)


def _unpack_small(p):
    return dict(b_ada=p[0:3].reshape(1, 3072), norm_w=p[3:4], b_in=p[4:9].reshape(-1)[:IN_W].reshape(1, IN_W),
                conv_w=p[9:14], conv_b=p[14:15],
                rpb=p[15:19].reshape(-1)[:NA_HEADS * 15 * 31].reshape(1, NA_HEADS, 15, 31),
                ml_norm_w=p[19:20, :ML_W], final_norm_w=p[20], last=p[21])


def kernel(x, c, w_ada, b_ada, norm_w, w_in, b_in, conv_w, conv_b, rpb, ml_norm_w, w_out, final_norm_w, loss_target, m_w_ada, m_b_ada, m_norm_w, m_w_in, m_b_in, m_conv_w, m_conv_b, m_rpb, m_ml_norm_w, m_w_out, m_final_norm_w, v_w_ada, v_b_ada, v_norm_w, v_w_in, v_b_in, v_conv_w, v_conv_b, v_rpb, v_ml_norm_w, v_w_out, v_final_norm_w):
    xi, yi, ci = lax.axis_index("x"), lax.axis_index("y"), lax.axis_index("c")
    chip = 2 * xi + yi
    dev = 2 * chip + ci
    t = x.shape[1]
    ada_n = w_ada.shape[2]
    in_n = w_in.shape[2]
    out_r = w_out.shape[1]

    c_blk = jnp.pad(c, ((0, 7), (0, 0)))
    w_in_t, m_w_in_t, v_w_in_t = w_in[0].T, m_w_in[0].T, v_w_in[0].T
    in_h = in_n // 2
    w_in_half = lax.dynamic_slice_in_dim(w_in_t, ci * in_h, in_h, axis=0).astype(BF16)
    w_out_half = lax.dynamic_slice_in_dim(w_out[0], ci * (out_r // 2), out_r // 2, axis=0).astype(BF16)
    conv_blk = jnp.pad(conv_w[0], ((0, 3), (0, 0)))
    c_g, conv_g, w_in_g, w_out_g = _allgather8([c_blk, conv_blk, w_in_half, w_out_half], "gather_c_weights")
    c_all = c_g[:, 0]
    w_out_g = w_out_g.reshape(D_MODEL, D_MODEL)
    b_ada_s = lax.dynamic_slice_in_dim(b_ada, chip * ada_n, ada_n, axis=1)
    mod_s = _mod_fwd(c_all, w_ada[0], b_ada_s)
    (mod_g,) = _allgather8([mod_s], "gather_mod")
    mod_mine = lax.dynamic_index_in_dim(mod_g, dev, axis=1, keepdims=False)
    mod = mod_mine[0::2].reshape(1, 3 * D_MODEL)
    shift, scale, gate = mod[:, :D_MODEL], mod[:, D_MODEL:2 * D_MODEL], mod[:, 2 * D_MODEL:]

    w_in_tp = jnp.pad(w_in_g.reshape(IN_W, D_MODEL), ((0, IN_PAD - IN_W), (0, 0)))
    b_in_p = jnp.pad(b_in, ((0, 0), (0, IN_PAD - IN_W)))
    conv_w8 = conv_g.reshape(4, 2, 8, conv_w.shape[2])[:, 0].transpose(1, 0, 2).reshape(8, D_MODEL)

    (loss, grad_x, dmod, g_nw, g_w_in, g_b_in, g_conv_w, g_conv_b, g_rpb, g_mlnw, g_w_out, g_fnw) = _local_step(
        x[0], loss_target[0], shift, scale, gate, norm_w, w_in_tp, b_in_p, conv_w8, conv_b, rpb[0],
        ml_norm_w, w_out_g, final_norm_w.reshape(1, D_MODEL))

    g_in_t = g_w_in

    def halves(a, per_chip, h):
        return jnp.stack([lax.dynamic_slice_in_dim(a, k * per_chip + h * (per_chip // 2), per_chip // 2, axis=0)
                          for k in range(4)])

    ri, ro = _pair_exchange([halves(g_in_t, in_n, 1 - ci), halves(g_w_out, out_r, 1 - ci)], "rs_pair")
    pi = _add2(halves(g_in_t, in_n, ci), ri, "rs_pair_add_in", BF16)
    po = _add2(halves(g_w_out, out_r, ci), ro, "rs_pair_add_out", BF16)
    qi, qo = _chip_exchange([pi, po], "rs_chips")
    si = _sum_slabs(qi, "rs_sum_in")
    so = _sum_slabs(qo, "rs_sum_out")
    ti, to = _pair_exchange([si, so], "rs_share")
    g_w_in_s = jnp.where(ci == 0, jnp.concatenate([si, ti], axis=0), jnp.concatenate([ti, si], axis=0))
    g_w_out_s = jnp.where(ci == 0, jnp.concatenate([so, to], axis=0), jnp.concatenate([to, so], axis=0))

    small = _pack_small(dmod, g_nw, g_b_in[:, :IN_W], g_conv_w[:CONV_W], g_conv_b, g_rpb, g_mlnw, g_fnw,
                        jnp.pad(loss, ((0, 0), (0, 1024 - 128))))
    (small_g,) = _allgather8([small], "gather_small")
    small_sum = _sum_slabs(small_g, "small_sum")
    gs = _unpack_small(small_sum)
    dmod_all = small_g[:, 0:3].reshape(N_DEV, 3 * D_MODEL)
    g_w_ada_s = _wada_grad(c_all, lax.dynamic_slice_in_dim(dmod_all, chip * ada_n, ada_n, axis=1))
    g_conv_w_s = lax.dynamic_slice_in_dim(gs['conv_w'], chip * conv_w.shape[2], conv_w.shape[2], axis=1)
    loss_total = gs['last'][0]

    small_names = ('b_ada', 'norm_w', 'b_in', 'conv_b', 'rpb', 'ml_norm_w', 'final_norm_w')
    small_w = (b_ada, norm_w, b_in, conv_b, rpb, ml_norm_w, final_norm_w)
    small_m = (m_b_ada, m_norm_w, m_b_in, m_conv_b, m_rpb, m_ml_norm_w, m_final_norm_w)
    small_v = (v_b_ada, v_norm_w, v_b_in, v_conv_b, v_rpb, v_ml_norm_w, v_final_norm_w)
    ds_, nms, nvs = {}, {}, {}
    for nm_, w_, m_, v_ in zip(small_names, small_w, small_m, small_v):
        two_d = (NA_HEADS, w_.size // NA_HEADS) if nm_ == 'rpb' else (1, w_.size)
        outs = _adamw(w_.reshape(two_d), gs[nm_].reshape(two_d), m_.reshape(two_d), v_.reshape(two_d),
                      "adamw_" + nm_)
        ds_[nm_], nms[nm_], nvs[nm_] = [o.reshape(w_.shape) for o in outs]
    d_ada, nm_ada, nv_ada = _adamw(w_ada[0], g_w_ada_s, m_w_ada[0], v_w_ada[0], "adamw_w_ada")
    d_in, nm_in, nv_in = _adamw(w_in_t, g_w_in_s, m_w_in_t, v_w_in_t, "adamw_w_in")
    d_out, nm_out, nv_out = _adamw(w_out[0], g_w_out_s, m_w_out[0], v_w_out[0], "adamw_w_out")
    d_cw, nm_cw, nv_cw = _adamw(conv_w[0], g_conv_w_s, m_conv_w[0], v_conv_w[0], "adamw_conv_w")

    def group(big_ada, big_in, big_out, cw, sm):
        return (big_ada[None], sm['b_ada'], sm['norm_w'], big_in.T[None], sm['b_in'], cw[None], sm['conv_b'],
                sm['rpb'], sm['ml_norm_w'], big_out[None], sm['final_norm_w'])

    return ((loss_total, grad_x[None])
            + group(g_w_ada_s, g_w_in_s, g_w_out_s, g_conv_w_s, gs)
            + group(d_ada, d_in, d_out, d_cw, ds_)
            + group(nm_ada, nm_in, nm_out, nm_cw, nms)
            + group(nv_ada, nv_in, nv_out, nv_cw, nvs))
```

```python
import functools

import numpy as np
import jax
import jax.numpy as jnp
from jax import lax
from jax.experimental import pallas as pl
from jax.experimental.pallas import tpu as pltpu

F32 = jnp.float32
BF16 = jnp.bfloat16
HI = lax.Precision.HIGHEST

D_MODEL = 1024
GRID_W = 64
NA_W = 512
NA_HEAD_DIM = 64
NA_HEADS = 8
NA_KH = 8
NA_KW = 16
ML_W = 512
ML_HEADS = 4
ML_HEAD_DIM = 128
ML_CHUNK = 128
CONV_W = 5
EPS = 1e-6
IN_W = 4 * NA_W + 5 * ML_W + 4 * ML_HEADS
IN_MAIN = 4 * NA_W + 5 * ML_W
IN_PAD = IN_MAIN + 128
NEG = -1e30

ADAM_LR = 0.001
ADAM_B1 = 0.9
ADAM_B2 = 0.999
ADAM_EPS = 1e-08
ADAM_WD = 0.01
ADAM_STEP = 10

NA_QROWS = 8
NA_KROWS = 16
NA_QT = NA_QROWS * GRID_W
NA_KT = NA_KROWS * GRID_W
NA_KCH = 256
NA_RC = 32
ML_NB = 32
ML_TB = ML_NB * ML_CHUNK
ML_HPS = 1

VMEM_LIMIT = 56 * 1024 * 1024
IN_BWD_VMEM_LIMIT = 60 * 1024 * 1024


def _cparams(sem, vmem=VMEM_LIMIT):
    return pltpu.CompilerParams(dimension_semantics=sem, vmem_limit_bytes=vmem)


def _silu(x):
    return x * jax.nn.sigmoid(x)


def _dsilu(x):
    s = jax.nn.sigmoid(x)
    return s * (1.0 + x * (1.0 - s))


def _dot(a, b, dims):
    return lax.dot_general(a, b, (dims, ((), ())), preferred_element_type=F32)


def _nn(a, b):
    return _dot(a, b, ((1,), (0,)))


def _nt(a, b):
    return _dot(a, b, ((1,), (1,)))


def _tn(a, b):
    return _dot(a, b, ((0,), (0,)))


def _row(n):
    return pl.BlockSpec((1, n), lambda i: (0, 0))


def _modulated_norm(xv, nw, sc, sh):
    r = lax.rsqrt(jnp.mean(xv * xv, axis=-1, keepdims=True) + EPS)
    xn = xv * r
    return xn * nw * (1.0 + sc) + sh, xn, r


IN_TN = 768


def _in_proj(x, norm_w, scale, shift, w_in_t, b_in_p):
    t, d = x.shape
    tm = 2048
    gcol = IN_MAIN // 128

    def body(x_ref, nw_ref, sc_ref, sh_ref, w_ref, b_ref, wg_ref, bg_ref, proj_ref, g_ref, h_scr):
        @pl.when(pl.program_id(1) == 0)
        def _():
            h, _, _ = _modulated_norm(x_ref[...], nw_ref[...], sc_ref[...], sh_ref[...])
            h_scr[...] = h.astype(BF16)
            g_ref[...] = _nt(h_scr[...], wg_ref[...]) + bg_ref[...]
        proj_ref[...] = (_nt(h_scr[...], w_ref[...]) + b_ref[...]).astype(BF16)

    row = lambda n: pl.BlockSpec((1, n), lambda i, j: (0, 0))
    return pl.pallas_call(
        body, name="in_proj", grid=(t // tm, IN_MAIN // IN_TN),
        in_specs=[pl.BlockSpec((tm, d), lambda i, j: (i, 0)), row(d), row(d), row(d),
                  pl.BlockSpec((IN_TN, d), lambda i, j: (j, 0)), pl.BlockSpec((1, IN_TN), lambda i, j: (0, j)),
                  pl.BlockSpec((128, d), lambda i, j: (gcol, 0)), pl.BlockSpec((1, 128), lambda i, j: (0, gcol))],
        out_specs=[pl.BlockSpec((tm, IN_TN), lambda i, j: (i, j)), pl.BlockSpec((tm, 128), lambda i, j: (i, 0))],
        out_shape=[jax.ShapeDtypeStruct((t, IN_MAIN), BF16), jax.ShapeDtypeStruct((t, 128), F32)],
        scratch_shapes=[pltpu.VMEM((tm, d), BF16)],
        compiler_params=_cparams(("parallel", "arbitrary")),
    )(x, norm_w, scale, shift, w_in_t, b_in_p, w_in_t, b_in_p)


def _ml_norm_parts(hs, o, z, nw):
    outs = []
    for hh in range(ML_HEADS):
        sl = slice(hh * ML_HEAD_DIM, (hh + 1) * ML_HEAD_DIM)
        so = jax.nn.sigmoid(o[:, sl])
        hm = hs[:, sl] * so
        mu = jnp.mean(hm, axis=-1, keepdims=True)
        cen = hm - mu
        var = jnp.mean(cen * cen, axis=-1, keepdims=True)
        rs = lax.rsqrt(var + EPS)
        outs.append((sl, cen * rs, rs, so))
    return outs


def _tail(o_na, proj, h_f, h_b, x, target, gate, ml_norm_w, fnw, w_out_b):
    t, d = x.shape
    tm = 512

    def body(ona_ref, naz_ref, hf_ref, hb_ref, o_ref, z_ref, x_ref, tg_ref, g_ref, nw_ref, fw_ref, w_ref,
             loss_ref, dres_ref, dona_ref, dnaz_ref, dhs_ref, do_ref, dz_ref, dgate_ref, gfw_ref, gnw_ref,
             gwo_ref, mix_scr):
        @pl.when(pl.program_id(0) == 0)
        def _():
            for r in (loss_ref, dgate_ref, gfw_ref, gnw_ref, gwo_ref):
                r[...] = jnp.zeros_like(r)
        naz = naz_ref[...].astype(F32)
        ona = ona_ref[...]
        sg_naz = jax.nn.sigmoid(naz)
        sna = naz * sg_naz
        mix_scr[:, 0:NA_W] = (ona * sna).astype(BF16)
        hs = hf_ref[...] + hb_ref[...]
        z = z_ref[...].astype(F32)
        ov = o_ref[...].astype(F32)
        parts = _ml_norm_parts(hs, ov, z, nw_ref[...])
        sgz = [jax.nn.sigmoid(z[:, sl]) for sl, _, _, _ in parts]
        for (sl, xn, _, _), sg in zip(parts, sgz):
            mix_scr[:, NA_W + sl.start:NA_W + sl.stop] = (xn * nw_ref[:, sl] * (z[:, sl] * sg)).astype(BF16)
        mixb = mix_scr[...]
        wv = w_ref[...]
        yv = _nn(mixb, wv)
        gate_v = g_ref[...]
        hres = x_ref[...] + gate_v * yv
        r = lax.rsqrt(jnp.mean(hres * hres, axis=-1, keepdims=True) + EPS)
        xnf = hres * r
        err = xnf * fw_ref[...] - tg_ref[...]
        loss_ref[...] += 0.5 * jnp.sum(jnp.sum(err * err, axis=-1, keepdims=True) * (1.0 / d), axis=0, keepdims=True)
        dout = err * (1.0 / d)
        gfw_ref[...] += jnp.sum(dout * xnf, axis=0, keepdims=True)
        dxn = dout * fw_ref[...]
        dres = r * (dxn - xnf * jnp.mean(dxn * xnf, axis=-1, keepdims=True))
        dres_ref[...] = dres
        dgate_ref[...] += jnp.sum(dres * yv, axis=0, keepdims=True)
        dyb = (dres * gate_v).astype(BF16)
        gwo_ref[...] += _tn(mixb, dyb)
        dmix = _nt(dyb, wv)
        dna = dmix[:, 0:NA_W]
        dona_ref[...] = dna * sna
        dnaz_ref[...] = (dna * ona * (sg_naz * (1.0 + naz * (1.0 - sg_naz)))).astype(BF16)
        for (sl, xn, rs, so), sg in zip(parts, sgz):
            dyv = dmix[:, NA_W + sl.start:NA_W + sl.stop]
            zz = z[:, sl]
            sz = zz * sg
            w = nw_ref[:, sl]
            dz_ref[:, sl] = (dyv * xn * w * (sg * (1.0 + zz * (1.0 - sg)))).astype(BF16)
            gnw_ref[:, sl] += jnp.sum(dyv * xn * sz, axis=0, keepdims=True)
            dxm = dyv * w * sz
            dhm = rs * (dxm - jnp.mean(dxm, axis=-1, keepdims=True)
                        - xn * jnp.mean(dxm * xn, axis=-1, keepdims=True))
            dhs_ref[:, sl] = dhm * so
            do_ref[:, sl] = (dhm * hs[:, sl] * so * (1.0 - so)).astype(BF16)

    blk = lambda c: pl.BlockSpec((tm, 512), lambda i, c=c: (i, c))
    full = pl.BlockSpec((tm, d), lambda i: (i, 0))
    o512 = jax.ShapeDtypeStruct((t, 512), F32)
    b512 = jax.ShapeDtypeStruct((t, 512), BF16)
    whole = pl.BlockSpec((d, d), lambda i: (0, 0))
    return pl.pallas_call(
        body, name="tail", grid=(t // tm,),
        in_specs=[blk(0), blk(3), blk(0), blk(0), blk(7), blk(8), full, full, _row(d), _row(ML_W), _row(d), whole],
        out_specs=[pl.BlockSpec((1, 128), lambda i: (0, 0)), full] + [blk(0)] * 5
        + [_row(d), _row(d), _row(ML_W), whole],
        out_shape=[jax.ShapeDtypeStruct((1, 128), F32), jax.ShapeDtypeStruct((t, d), F32),
                   o512, b512, o512, b512, b512]
        + [jax.ShapeDtypeStruct((1, d), F32), jax.ShapeDtypeStruct((1, d), F32),
           jax.ShapeDtypeStruct((1, ML_W), F32), jax.ShapeDtypeStruct((d, d), F32)],
        scratch_shapes=[pltpu.VMEM((tm, d), BF16)],
        compiler_params=_cparams(("arbitrary",)),
    )(o_na, proj, h_f, h_b, proj, proj, x, target, gate, ml_norm_w, fnw, w_out_b)


def _in_bwd(pieces, x, dres, w_in_t, norm_w, scale, shift):
    t, d = x.shape
    tm = 512
    nt = t // tm
    widths = [p.shape[1] for p in pieces]
    offs = [sum(widths[:k]) for k in range(len(widths))]
    assert sum(widths) == IN_PAD
    npc = len(pieces)

    def body(*refs):
        p_refs = refs[:npc]
        (x_ref, dres_ref, w_hbm, nw_ref, sc_ref, sh_ref,
         gx_ref, gw_hbm, gb_ref, dsc_ref, dsh_ref, gnw_ref, w_vmem, acc, stage, sem) = refs[npc:]
        i = pl.program_id(0)

        @pl.when(i == 0)
        def _():
            cp = pltpu.make_async_copy(w_hbm, w_vmem, sem.at[0])
            cp.start()
            acc[...] = jnp.zeros_like(acc)
            gb_ref[...] = jnp.zeros_like(gb_ref)
            dsc_ref[...] = jnp.zeros_like(dsc_ref)
            dsh_ref[...] = jnp.zeros_like(dsh_ref)
            gnw_ref[...] = jnp.zeros_like(gnw_ref)
            cp.wait()

        nw = nw_ref[...]
        s1 = 1.0 + sc_ref[...]
        h, xn, r = _modulated_norm(x_ref[...], nw, sc_ref[...], sh_ref[...])
        hb = h.astype(BF16)
        dhv = jnp.zeros((tm, d), F32)
        for p_ref, c0, w in zip(p_refs, offs, widths):
            pt = p_ref[...]
            pb = pt.astype(BF16)
            dhv = dhv + _nn(pb, w_vmem[c0:c0 + w, :])
            acc[:, c0:c0 + w] += _tn(hb, pb)
            gb_ref[:, c0:c0 + w] += jnp.sum(pt.astype(F32), axis=0, keepdims=True)
        dsh_ref[...] += jnp.sum(dhv, axis=0, keepdims=True)
        dsc_ref[...] += jnp.sum(dhv * xn * nw, axis=0, keepdims=True)
        gnw_ref[...] += jnp.sum(dhv * xn * s1, axis=0, keepdims=True)
        dxn = dhv * nw * s1
        gx_ref[...] = dres_ref[...] + r * (dxn - xn * jnp.mean(dxn * xn, axis=-1, keepdims=True))

        @pl.when(i == nt - 1)
        def _():
            copies = []
            for blk in range(IN_PAD // 128):
                slot = blk % 2
                if blk >= 2:
                    copies[blk - 2].wait()
                stage[slot] = acc[:, blk * 128:(blk + 1) * 128].T
                cp = pltpu.make_async_copy(stage.at[slot], gw_hbm.at[pl.ds(blk * 128, 128), :], sem.at[1 + slot])
                cp.start()
                copies.append(cp)
            copies[-2].wait()
            copies[-1].wait()

    full = pl.BlockSpec((tm, d), lambda i: (i, 0))
    return pl.pallas_call(
        body, name="in_bwd", grid=(nt,),
        in_specs=[pl.BlockSpec((tm, w), lambda i: (i, 0)) for w in widths]
        + [full, full, pl.BlockSpec(memory_space=pl.ANY), _row(d), _row(d), _row(d)],
        out_specs=[full, pl.BlockSpec(memory_space=pl.ANY), _row(IN_PAD), _row(d), _row(d), _row(d)],
        out_shape=[jax.ShapeDtypeStruct((t, d), F32), jax.ShapeDtypeStruct((IN_PAD, d), F32),
                   jax.ShapeDtypeStruct((1, IN_PAD), F32)] + [jax.ShapeDtypeStruct((1, d), F32)] * 3,
        scratch_shapes=[pltpu.VMEM((IN_PAD, d), BF16), pltpu.VMEM((d, IN_PAD), F32),
                        pltpu.VMEM((2, 128, d), F32), pltpu.SemaphoreType.DMA((3,))],
        compiler_params=_cparams(("arbitrary",), IN_BWD_VMEM_LIMIT),
    )(*pieces, x, dres, w_in_t, norm_w, scale, shift)


def _na_static(rows):
    cases = [(0, 0), (NA_QROWS, NA_QROWS - 4), (rows - NA_QROWS, rows - NA_KROWS)]
    dy = np.zeros((3, NA_QROWS, NA_KROWS), np.int32)
    rv = np.zeros((3, NA_QROWS, NA_KROWS), bool)
    for cs, (r0, kr0) in enumerate(cases):
        for i in range(NA_QROWS):
            for j in range(NA_KROWS):
                r, kr = r0 + i, kr0 + j
                rs = min(max(r - NA_KH // 2, 0), rows - NA_KH)
                rv[cs, i, j] = rs <= kr <= rs + NA_KH - 1
                dy[cs, i, j] = min(max(kr - r + NA_KH - 1, 0), 2 * NA_KH - 2)
    cq = np.arange(GRID_W)[:, None]
    ck = np.arange(GRID_W)[None, :]
    cs0 = np.clip(cq - NA_KW // 2, 0, GRID_W - NA_KW)
    cv = (ck >= cs0) & (ck < cs0 + NA_KW)
    dx = np.clip(ck - cq, -(NA_KW - 1), NA_KW - 1) + NA_KW - 1
    return dy, rv, dx.astype(np.int32), cv


def _na_bias_table(rpb, rows):
    _, _, dx, cv = _na_static(rows)
    ndy = 2 * NA_KH - 1
    onehot = (dx.reshape(1, -1) == np.arange(2 * NA_KW - 1)[:, None]).astype(np.float32)
    rpx = jnp.dot(rpb.reshape(NA_HEADS * ndy, 2 * NA_KW - 1), jnp.asarray(onehot), precision=HI)
    rpx = jnp.where(cv[None, None], rpx.reshape(NA_HEADS, ndy, GRID_W, GRID_W), NEG)
    neg = jnp.full((NA_HEADS, 1, GRID_W, GRID_W), NEG, F32)
    rpx = jnp.concatenate([rpx, neg], axis=1)
    nxt = jnp.concatenate([rpx[:, 1:], neg], axis=1)
    negs = jnp.broadcast_to(neg, rpx.shape)
    pairs = jnp.concatenate([jnp.concatenate([rpx, nxt], axis=3), jnp.concatenate([rpx, negs], axis=3),
                             jnp.concatenate([negs, rpx], axis=3)], axis=1)
    npair = pairs.shape[1]

    def body(m_ref, o_ref):
        cs = pl.program_id(1)
        r0 = jnp.where(cs == 0, 0, jnp.where(cs == 1, NA_QROWS, rows - NA_QROWS))
        kr0 = jnp.where(cs == 0, 0, jnp.where(cs == 1, NA_QROWS - NA_KH // 2, rows - NA_KROWS))
        for i in range(NA_QROWS):
            r = r0 + i
            rs = jnp.clip(r - NA_KH // 2, 0, rows - NA_KH)
            for jp in range(NA_KROWS // 2):
                kl = kr0 + 2 * jp
                vl = (kl >= rs) & (kl <= rs + NA_KH - 1)
                vr = (kl + 1 >= rs) & (kl + 1 <= rs + NA_KH - 1)
                dyl = jnp.clip(kl - r + NA_KH - 1, 0, ndy)
                dyr = jnp.clip(kl + 1 - r + NA_KH - 1, 0, ndy)
                idx = jnp.where(vl & vr, dyl, jnp.where(vl, 16 + dyl, jnp.where(vr, 32 + dyr, 16 + ndy)))
                o_ref[0, 0, i * GRID_W:(i + 1) * GRID_W, jp * 128:(jp + 1) * 128] = m_ref[0, idx]

    return pl.pallas_call(
        body, name="na_bias_table", grid=(NA_HEADS, 3),
        in_specs=[pl.BlockSpec((1, npair, GRID_W, 128), lambda h, cs: (h, 0, 0, 0))],
        out_specs=pl.BlockSpec((1, 1, NA_QT, NA_KT), lambda h, cs: (h, cs, 0, 0)),
        out_shape=jax.ShapeDtypeStruct((NA_HEADS, 3, NA_QT, NA_KT), F32),
        compiler_params=_cparams(("parallel", "parallel")),
    )(pairs)


def _na_specs(t):
    nb = t // NA_QT
    nkb = t // NA_KCH
    npieces = NA_KT // NA_KCH

    def kb0(b):
        return jnp.clip(b * (NA_QT // NA_KCH) - 1, 0, nkb - npieces)

    def case(b):
        return jnp.where(b == 0, 0, jnp.where(b == nb - 1, 2, 1))

    q_spec = pl.BlockSpec((NA_QT, 128), lambda p, b: (b, p))
    k_specs = [pl.BlockSpec((NA_KCH, 128), lambda p, b, i=i: (kb0(b) + i, 4 + p)) for i in range(npieces)]
    v_specs = [pl.BlockSpec((NA_KCH, 128), lambda p, b, i=i: (kb0(b) + i, 8 + p)) for i in range(npieces)]
    tbl_spec = pl.BlockSpec((2, 1, NA_QT, NA_KT), lambda p, b: (p, case(b), 0, 0))
    io_spec = pl.BlockSpec((NA_QT, 128), lambda p, b: (b, p))
    return nb, npieces, kb0, case, q_spec, k_specs, v_specs, tbl_spec, io_spec


NA_HALF = NA_QT // 2
NA_COMBOS_ALL = tuple((i, 0, NA_QT) for i in range(NA_KT // NA_KCH))
NA_COMBOS_INNER = ((0, 0, NA_HALF),) + tuple((i, 0, NA_QT) for i in range(1, NA_KT // NA_KCH - 1)) \
    + ((NA_KT // NA_KCH - 1, NA_HALF, NA_QT),)


def _na_place(val, r0, r1):
    if (r0, r1) == (0, NA_QT):
        return val
    z = jnp.zeros((NA_HALF, val.shape[1]), val.dtype)
    return jnp.concatenate([val, z] if r0 == 0 else [z, val], axis=0)


def _na_fwd(proj, tbl):
    t = proj.shape[0]
    nb, npieces, _, _, q_spec, k_specs, v_specs, tbl_spec, io_spec = _na_specs(t)
    lse_spec = pl.BlockSpec((1, NA_QT, 2), lambda p, b: (p, b, 0))

    def body(*refs):
        q_ref = refs[0]
        k_refs = refs[1:1 + npieces]
        v_refs = refs[1 + npieces:1 + 2 * npieces]
        tbl_ref, o_ref, lse_ref = refs[1 + 2 * npieces:]
        b = pl.program_id(1)

        def compute(combos):
            lane = lax.broadcasted_iota(jnp.int32, (1, 128), 1)
            qv = q_ref[...].astype(F32) * (NA_HEAD_DIM ** -0.5)
            ks = [r[...].astype(BF16) for r in k_refs]
            vs = [r[...].astype(BF16) for r in v_refs]
            hs = range(2)
            msk = [(lane // NA_HEAD_DIM) == hh for hh in hs]
            qh = [jnp.where(msk[hh], qv, 0.0).astype(BF16) for hh in hs]
            s = [[_nt(qh[hh][r0:r1], ks[i]) + tbl_ref[hh, 0, r0:r1, i * NA_KCH:(i + 1) * NA_KCH]
                  for i, r0, r1 in combos] for hh in hs]
            for h0 in (0, NA_HALF):
                rows = slice(h0, h0 + NA_HALF)
                cover = [(c, i, h0 - r0) for c, (i, r0, r1) in enumerate(combos) if r0 <= h0 < r1]
                part = [[s[hh][c][off:off + NA_HALF] for c, _, off in cover] for hh in hs]
                m = [functools.reduce(jnp.maximum, [jnp.max(v, axis=1, keepdims=True) for v in part[hh]]) for hh in hs]
                p = [[jnp.exp(v - m[hh]) for v in part[hh]] for hh in hs]
                l = [functools.reduce(jnp.add, [jnp.sum(v, axis=1, keepdims=True) for v in p[hh]]) for hh in hs]
                o = [functools.reduce(jnp.add, [_nn(p[hh][k].astype(BF16), vs[i]) for k, (_, i, _) in enumerate(cover)])
                     for hh in hs]
                for hh in hs:
                    lse_ref[0, rows, hh:hh + 1] = m[hh] + jnp.log(l[hh])
                o_ref[rows, :] = jnp.where(msk[0], o[0] / l[0], o[1] / l[1])

        inner = (b > 0) & (b < nb - 1)
        pl.when(inner)(lambda: compute(NA_COMBOS_INNER))
        pl.when(jnp.logical_not(inner))(lambda: compute(NA_COMBOS_ALL))

    return pl.pallas_call(
        body, name="na_fwd", grid=(4, nb),
        in_specs=[q_spec] + k_specs + v_specs + [tbl_spec],
        out_specs=[io_spec, lse_spec],
        out_shape=[jax.ShapeDtypeStruct((t, NA_W), F32), jax.ShapeDtypeStruct((4, t, 2), F32)],
        compiler_params=_cparams(("parallel", "arbitrary")),
    )(*([proj] * (1 + 2 * npieces)), tbl)


def _na_bwd(proj, tbl, d_o, o_na, lse):
    t = proj.shape[0]
    nb, npieces, kb0, case, q_spec, k_specs, v_specs, tbl_spec, io_spec = _na_specs(t)

    def body(*refs):
        q_ref = refs[0]
        k_refs = refs[1:1 + npieces]
        v_refs = refs[1 + npieces:1 + 2 * npieces]
        (tbl_ref, do_ref, o_ref, lse_ref, dq_ref, dk_hbm, dv_hbm, rpb_ref,
         dk_acc, dv_acc, dk_out, dv_out, s_scr, dp_scr, dsb_scr, pnb_scr, sem) = refs[1 + 2 * npieces:]
        p_id = pl.program_id(0)
        b = pl.program_id(1)

        @pl.when(b == 0)
        def _():
            dk_acc[...] = jnp.zeros_like(dk_acc)
            dv_acc[...] = jnp.zeros_like(dv_acc)

        @pl.when((b == 0) | (b == 1) | (b == nb - 1))
        def _():
            rpb_ref[...] = jnp.zeros_like(rpb_ref)

        def compute(combos):
            lane = lax.broadcasted_iota(jnp.int32, (1, 128), 1)
            scale = NA_HEAD_DIM ** -0.5
            qv = q_ref[...].astype(F32) * scale
            ks = [r[...].astype(BF16) for r in k_refs]
            vs = [r[...].astype(BF16) for r in v_refs]
            dov = do_ref[...]
            ov = o_ref[...]
            tok0 = kb0(b) * NA_KCH
            hs = range(2)
            msk = [(lane // NA_HEAD_DIM) == hh for hh in hs]
            qh = [jnp.where(msk[hh], qv, 0.0).astype(BF16) for hh in hs]
            doh = [jnp.where(msk[hh], dov, 0.0) for hh in hs]
            dohb = [doh[hh].astype(BF16) for hh in hs]
            dd = [jnp.sum(doh[hh] * ov, axis=1, keepdims=True) for hh in hs]
            for hh in hs:
                for c, (i, q0, q1) in enumerate(combos):
                    slot = (hh * len(combos) + c) % 2
                    cols = slice(i * NA_KCH, (i + 1) * NA_KCH)
                    s_scr[slot, 0:q1 - q0] = _nt(qh[hh][q0:q1], ks[i])
                    dp_scr[slot, 0:q1 - q0] = _nt(dohb[hh][q0:q1], vs[i])
                    for r0 in range(q0, q1, NA_RC):
                        rows = slice(r0, r0 + NA_RC)
                        loc = slice(r0 - q0, r0 - q0 + NA_RC)
                        p = jnp.exp(s_scr[slot, loc, :] + tbl_ref[hh, 0, rows, cols] - lse_ref[0, rows, hh:hh + 1])
                        d = p * (dp_scr[slot, loc, :] - dd[hh][rows])
                        pnb_scr[hh, rows, cols] = p.astype(BF16)
                        dsb_scr[hh, rows, cols] = d.astype(BF16)
                done = {(i, q0) for i, q0, _ in combos} | {(i, NA_HALF) for i, q0, q1 in combos if q1 - q0 == NA_QT}
                for i in range(npieces):
                    for q0 in (0, NA_HALF):
                        if (i, q0) not in done:
                            dsb_scr[hh, q0:q0 + NA_HALF, i * NA_KCH:(i + 1) * NA_KCH] = jnp.zeros(
                                (NA_HALF, NA_KCH), BF16)
            dqh = [functools.reduce(jnp.add, [_na_place(_nn(dsb_scr[hh, q0:q1, i * NA_KCH:(i + 1) * NA_KCH], ks[i]),
                                                        q0, q1) for i, q0, q1 in combos]) for hh in hs]
            dq_ref[...] = (jnp.where(msk[0], dqh[0], dqh[1]) * scale).astype(BF16)
            for i in range(npieces):
                rows = pl.ds(pl.multiple_of(tok0 + i * NA_KCH, NA_KCH), NA_KCH)
                cols = slice(i * NA_KCH, (i + 1) * NA_KCH)
                q0, q1 = [(a, e) for j, a, e in combos if j == i][0]
                dk_acc[rows, :] += (_tn(dsb_scr[0, q0:q1, cols], qh[0][q0:q1])
                                    + _tn(dsb_scr[1, q0:q1, cols], qh[1][q0:q1]))
                dv_acc[rows, :] += (_tn(pnb_scr[0, q0:q1, cols], dohb[0][q0:q1])
                                    + _tn(pnb_scr[1, q0:q1, cols], dohb[1][q0:q1]))
            for hh in hs:
                acc = dsb_scr[hh, 0:GRID_W, :].astype(F32)
                for i in range(1, NA_QROWS):
                    acc = acc + pltpu.roll(dsb_scr[hh, i * GRID_W:(i + 1) * GRID_W, :].astype(F32),
                                           NA_KT - i * GRID_W, 1)
                rpb_ref[0, 0, hh] += acc

        inner = (b > 0) & (b < nb - 1)
        pl.when(inner)(lambda: compute(NA_COMBOS_INNER))
        pl.when(jnp.logical_not(inner))(lambda: compute(NA_COMBOS_ALL))

        @pl.when(b == nb - 1)
        def _():
            def copies(pair):
                cols = pl.ds(pl.multiple_of(pair * 128, 128), 128)
                return (pltpu.make_async_copy(dk_out, dk_hbm.at[:, cols], sem.at[0]),
                        pltpu.make_async_copy(dv_out, dv_hbm.at[:, cols], sem.at[1]))

            @pl.when(p_id > 0)
            def _():
                for cp in copies(p_id - 1):
                    cp.wait()
            dk_out[...] = dk_acc[...].astype(BF16)
            dv_out[...] = dv_acc[...].astype(BF16)
            for cp in copies(p_id):
                cp.start()

            @pl.when(p_id == NA_HEADS // 2 - 1)
            def _():
                for cp in copies(p_id):
                    cp.wait()

    o512 = jax.ShapeDtypeStruct((t, NA_W), BF16)
    return pl.pallas_call(
        body, name="na_bwd", grid=(4, nb),
        in_specs=[q_spec] + k_specs + v_specs + [tbl_spec, io_spec, io_spec,
                                                 pl.BlockSpec((1, NA_QT, 2), lambda p, b: (p, b, 0))],
        out_specs=[io_spec, pl.BlockSpec(memory_space=pl.ANY), pl.BlockSpec(memory_space=pl.ANY),
                   pl.BlockSpec((1, 1, 2, GRID_W, NA_KT), lambda p, b: (p, case(b), 0, 0, 0))],
        out_shape=[o512, o512, o512, jax.ShapeDtypeStruct((4, 3, 2, GRID_W, NA_KT), F32)],
        scratch_shapes=[pltpu.VMEM((t, 128), F32), pltpu.VMEM((t, 128), F32),
                        pltpu.VMEM((t, 128), BF16), pltpu.VMEM((t, 128), BF16),
                        pltpu.VMEM((2, NA_QT, NA_KCH), F32), pltpu.VMEM((2, NA_QT, NA_KCH), F32),
                        pltpu.VMEM((2, NA_QT, NA_KT), BF16), pltpu.VMEM((2, NA_QT, NA_KT), BF16),
                        pltpu.SemaphoreType.DMA((2,))],
        compiler_params=_cparams(("arbitrary", "arbitrary")),
    )(*([proj] * (1 + 2 * npieces)), tbl, d_o, o_na, lse)


def _rpb_reduce(rpbacc, rows):
    nacc = 4 * 3 * 2

    def shift_body(a_ref, o_ref):
        acc = a_ref[0, 0:1, :]
        for cq in range(1, GRID_W):
            acc = acc + pltpu.roll(a_ref[0, cq:cq + 1, :], NA_KT - cq, 1)
        o_ref[0] = jnp.broadcast_to(acc, (8, NA_KT))

    vec = pl.pallas_call(
        shift_body, name="rpb_shift", grid=(nacc,),
        in_specs=[pl.BlockSpec((1, GRID_W, NA_KT), lambda a: (a, 0, 0))],
        out_specs=pl.BlockSpec((1, 8, NA_KT), lambda a: (a, 0, 0)),
        out_shape=jax.ShapeDtypeStruct((nacc, 8, NA_KT), F32),
        compiler_params=_cparams(("parallel",)),
    )(rpbacc.reshape(nacc, GRID_W, NA_KT))
    a = vec[:, 0].reshape(4, 3, 2, NA_KT).transpose(0, 2, 1, 3).reshape(NA_HEADS, 3, NA_KT)
    if rows // NA_QROWS < 3:
        a = a.at[:, 1].set(0.0)
    dd = np.arange(NA_KROWS)[:, None]
    dxo = np.arange(-(NA_KW - 1), NA_KW)[None, :]
    idx = ((dd * GRID_W + dxo) % NA_KT).reshape(-1)
    g = a[..., idx].reshape(NA_HEADS, 3 * NA_KROWS, 2 * NA_KW - 1)
    g = jnp.pad(g, ((0, 0), (0, 0), (0, 128 - (2 * NA_KW - 1))))
    nmat = np.zeros((16, 3 * NA_KROWS), np.float32)
    for cs, delta in enumerate((0, -(NA_KH // 2), -(NA_KROWS - NA_QROWS))):
        for d in range(NA_KROWS):
            jmi = d - NA_KROWS if (cs == 0 and d > NA_KH - 1) else d
            dy = jmi + delta + NA_KH - 1
            if 0 <= dy <= 2 * NA_KH - 2:
                nmat[dy, cs * NA_KROWS + d] = 1.0

    def body(n_ref, g_ref, o_ref):
        o_ref[0] = jnp.dot(n_ref[...], g_ref[0], precision=HI, preferred_element_type=F32)

    out = pl.pallas_call(
        body, name="rpb_reduce", grid=(NA_HEADS,),
        in_specs=[pl.BlockSpec((16, nmat.shape[1]), lambda h: (0, 0)),
                  pl.BlockSpec((1, nmat.shape[1], 128), lambda h: (h, 0, 0))],
        out_specs=pl.BlockSpec((1, 16, 128), lambda h: (h, 0, 0)),
        out_shape=jax.ShapeDtypeStruct((NA_HEADS, 16, 128), F32),
        compiler_params=_cparams(("parallel",)),
    )(jnp.asarray(nmat), g)
    return out[:, :2 * NA_KH - 1, :2 * NA_KW - 1]


def _halo_specs(tm, t, col, width=1024):
    nth = t // CONV_HALO
    per = tm // CONV_HALO
    return [pl.BlockSpec((tm, width), lambda i: (i, col)),
            pl.BlockSpec((CONV_HALO, width), lambda i: (jnp.maximum(i * per - 1, 0), col)),
            pl.BlockSpec((CONV_HALO, width), lambda i: (jnp.minimum((i + 1) * per, nth - 1), col))]


def _fill_ext(ext, cur_ref, prev_ref, next_ref, tm, nt):
    i = pl.program_id(0)
    hl = CONV_HALO
    ext[0:hl, :] = jnp.where(i == 0, 0.0, prev_ref[...].astype(F32))
    ext[hl:hl + tm, :] = cur_ref[...].astype(F32)
    ext[hl + tm:2 * hl + tm, :] = jnp.where(i == nt - 1, 0.0, next_ref[...].astype(F32))


CONV_HALO = 16
CONV_RC = 16
CONV_CB = 512


def _conv_chunks(tm):
    return [(slice(cb, cb + CONV_CB), slice(rb, rb + CONV_RC))
            for cb in range(0, 1024, CONV_CB) for rb in range(0, tm, CONV_RC)]


def _conv_masks():
    rows = lax.broadcasted_iota(jnp.int32, (8, CONV_CB), 0)
    return {off: (rows >= off if off > 0 else rows < 8 + off)
            for off in range(-(CONV_W // 2), CONV_W // 2 + 1) if off != 0}


def _conv_windows(ext, row0, cs, masks):
    ng = CONV_RC // 8
    grp = [ext[pl.ds(row0 + 8 * (g - 1), 8), cs] for g in range(ng + 2)]
    wins = []
    for j in range(CONV_W):
        off = j - CONV_W // 2
        if off == 0:
            parts = grp[1:ng + 1]
        elif off > 0:
            parts = [pltpu.roll(jnp.where(masks[off], grp[g + 1], grp[g + 2]), 8 - off, axis=0) for g in range(ng)]
        else:
            parts = [pltpu.roll(jnp.where(masks[off], grp[g + 1], grp[g]), -off, axis=0) for g in range(ng)]
        wins.append(jnp.concatenate(parts, axis=0))
    return wins


def _conv_fwd(proj, conv_w8, conv_b, tm):
    t = proj.shape[0]
    nt = t // tm

    def body(u_ref, up_ref, un_ref, w_ref, b_ref, pre_ref, act_ref, ext):
        _fill_ext(ext, u_ref, up_ref, un_ref, tm, nt)
        masks = _conv_masks()
        for cs, rs in _conv_chunks(tm):
            wins = _conv_windows(ext, rs.start + CONV_HALO, cs, masks)
            pre = b_ref[:, cs] + w_ref[0:1, cs] * wins[0]
            for j in range(1, CONV_W):
                pre = pre + w_ref[j:j + 1, cs] * wins[j]
            pre_ref[rs, cs] = pre
            act_ref[rs, cs] = _silu(pre)

    full = pl.BlockSpec((tm, 1024), lambda i: (i, 0))
    o = jax.ShapeDtypeStruct((t, 1024), F32)
    return pl.pallas_call(
        body, name="conv_fwd", grid=(nt,),
        in_specs=_halo_specs(tm, t, 2) + [pl.BlockSpec((8, 1024), lambda i: (0, 0)), _row(1024)],
        out_specs=[full, full], out_shape=[o, o],
        scratch_shapes=[pltpu.VMEM((tm + 2 * CONV_HALO, 1024), F32)],
        compiler_params=_cparams(("parallel",)),
    )(proj, proj, proj, conv_w8, conv_b)


def _conv_bwd(dq, dk, pre, proj, conv_w8, tm):
    t = pre.shape[0]
    nt = t // tm

    def body(dq_ref, dqp_ref, dqn_ref, dk_ref, dkp_ref, dkn_ref, pre_ref, prep_ref, pren_ref,
             u_ref, up_ref, un_ref, w_ref, du_ref, gw_ref, gb_ref, extd, extu):
        i = pl.program_id(0)
        hl = CONV_HALO

        @pl.when(i == 0)
        def _():
            gw_ref[...] = jnp.zeros_like(gw_ref)
            gb_ref[...] = jnp.zeros_like(gb_ref)
        for rows, dqr, dkr, prr, edge in ((slice(0, hl), dqp_ref, dkp_ref, prep_ref, i == 0),
                                          (slice(hl, hl + tm), dq_ref, dk_ref, pre_ref, None),
                                          (slice(hl + tm, 2 * hl + tm), dqn_ref, dkn_ref, pren_ref, i == nt - 1)):
            ds = _dsilu(prr[...])
            dl = dqr[...] * ds[:, 0:ML_W]
            dr = dkr[...] * ds[:, ML_W:]
            if edge is not None:
                dl = jnp.where(edge, 0.0, dl)
                dr = jnp.where(edge, 0.0, dr)
            extd[rows, 0:ML_W] = dl
            extd[rows, ML_W:] = dr
        _fill_ext(extu, u_ref, up_ref, un_ref, tm, nt)
        gb_ref[...] += jnp.sum(extd[hl:hl + tm, :], axis=0, keepdims=True)
        gacc = None
        masks = _conv_masks()
        for cs, rs in _conv_chunks(tm):
            if rs.start == 0:
                gacc = [jnp.zeros((8, CONV_CB), F32) for _ in range(CONV_W)]
            wd = _conv_windows(extd, rs.start + hl, cs, masks)
            du = w_ref[0:1, cs] * wd[CONV_W - 1]
            for j in range(1, CONV_W):
                du = du + w_ref[j:j + 1, cs] * wd[CONV_W - 1 - j]
            du_ref[rs, cs] = du.astype(BF16)
            ucur = extu[pl.ds(rs.start + hl, CONV_RC), cs]
            for j in range(CONV_W):
                prod = ucur * wd[CONV_W - 1 - j]
                gacc[j] = gacc[j] + functools.reduce(
                    jnp.add, [prod[k:k + 8] for k in range(0, CONV_RC, 8)])
            if rs.stop == tm:
                for j in range(CONV_W):
                    gw_ref[j:j + 1, cs] += jnp.sum(gacc[j], axis=0, keepdims=True)

    full = pl.BlockSpec((tm, 1024), lambda i: (i, 0))
    return pl.pallas_call(
        body, name="conv_bwd", grid=(nt,),
        in_specs=_halo_specs(tm, t, 0, ML_W) + _halo_specs(tm, t, 0, ML_W) + _halo_specs(tm, t, 0)
        + _halo_specs(tm, t, 2) + [pl.BlockSpec((8, 1024), lambda i: (0, 0))],
        out_specs=[full, pl.BlockSpec((8, 1024), lambda i: (0, 0)), _row(1024)],
        out_shape=[jax.ShapeDtypeStruct((t, 1024), BF16), jax.ShapeDtypeStruct((8, 1024), F32),
                   jax.ShapeDtypeStruct((1, 1024), F32)],
        scratch_shapes=[pltpu.VMEM((tm + 2 * CONV_HALO, 1024), F32), pltpu.VMEM((tm + 2 * CONV_HALO, 1024), F32)],
        compiler_params=_cparams(("arbitrary",)),
    )(dq, dq, dq, dk, dk, dk, pre, pre, pre, proj, proj, proj, conv_w8)


def _ml_consts(rev):
    iu = lax.broadcasted_iota(jnp.int32, (ML_CHUNK, ML_CHUNK), 0)
    js = lax.broadcasted_iota(jnp.int32, (ML_CHUNK, ML_CHUNK), 1)
    eye = iu == js
    le = iu <= js
    ge = iu >= js
    csum, csum_t, sees = (ge, le, ge) if rev else (le, ge, le)
    return eye, csum.astype(F32), csum_t.astype(F32), sees


def _col(row, eye):
    return jnp.sum(jnp.where(eye, row, 0.0), axis=1, keepdims=True)


def _rowof(col, eye):
    return jnp.sum(jnp.where(eye, col, 0.0), axis=0, keepdims=True)


def _row8(row):
    top = lax.broadcasted_iota(jnp.int32, (8, row.shape[1]), 0) == 0
    return jnp.where(top, row, jnp.zeros_like(row))


def _outer_rows(a_row, b_row_bf16):
    hi = a_row.astype(BF16)
    lo = (a_row - hi.astype(F32)).astype(BF16)
    r_a = lax.broadcasted_iota(jnp.int32, (8, a_row.shape[1]), 0)
    r_b = lax.broadcasted_iota(jnp.int32, (8, b_row_bf16.shape[1]), 0)
    lhs = jnp.where(r_a == 0, hi, jnp.where(r_a == 1, lo, jnp.zeros_like(hi)))
    rhs = jnp.where(r_b < 2, b_row_bf16, jnp.zeros_like(b_row_bf16))
    return _tn(lhs, rhs)


def _ml_gates(gi, gf, m0, csum, rev):
    lf = jax.nn.log_sigmoid(gf)
    b_rows = jnp.dot(lf, csum, precision=HI, preferred_element_type=F32)
    bl = jnp.sum(lf, axis=1, keepdims=True)
    a_rows = bl - b_rows + gi
    mloc = jnp.max(a_rows, axis=1, keepdims=True)
    order = list(range(ML_NB))[::-1] if rev else list(range(ML_NB))
    mp, mn, decay = {}, {}, {}
    m = m0
    for n in order:
        mp[n] = m
        m = jnp.maximum(bl[n:n + 1] + m, mloc[n:n + 1])
        mn[n] = m
    for n in order:
        decay[n] = jnp.exp(bl[n:n + 1] + mp[n] - mn[n])
    return b_rows, a_rows, gi - b_rows, mp, mn, decay, order


def _ml_load(q_ref, k_ref, v_ref, n):
    sl = slice(n * ML_CHUNK, (n + 1) * ML_CHUNK)
    qb = q_ref[sl, :].astype(BF16)
    kb = (k_ref[sl, :] * (ML_HEAD_DIM ** -0.5)).astype(BF16)
    vn = v_ref[sl, :].astype(F32)
    return sl, qb, kb, vn


def _ml_state_scan(q_ref, k_ref, v_ref, a_rows, mn, decay, order, c0, n0):
    ns = range(ML_NB)
    ld = [_ml_load(q_ref, k_ref, v_ref, n) for n in ns]
    vt = [ld[n][3].T for n in ns]
    w_row = [jnp.exp(a_rows[n:n + 1] - mn[n]) for n in ns]
    u = [_nn((vt[n] * w_row[n]).astype(BF16), ld[n][2]) for n in ns]
    nu = [_nn(_row8(w_row[n]).astype(BF16), ld[n][2])[0:1] for n in ns]
    cp, npv = {}, {}
    c, nv = c0, n0
    for n in order:
        cp[n], npv[n] = c, nv
        c = decay[n] * c + u[n]
        nv = decay[n] * nv + nu[n]
    return ld, vt, cp, npv, w_row, c, nv


def _ml_intra_all(ld, vt, b_rows, imb_rows, mp, cp, npv, sees, eye):
    ns = range(ML_NB)
    qk = [_nt(ld[n][2], ld[n][1]) for n in ns]
    cq = [_nt(cp[n].astype(BF16), ld[n][1]) for n in ns]
    qn = [_nt(_row8(npv[n]).astype(BF16), ld[n][1])[0:1] for n in ns]
    imb_col = [_col(imb_rows[n:n + 1], eye) for n in ns]
    dlog = [jnp.where(sees, b_rows[n:n + 1] + imb_col[n], NEG) for n in ns]
    m_inter = [b_rows[n:n + 1] + mp[n] for n in ns]
    m_t = [jnp.maximum(m_inter[n], jnp.max(dlog[n], axis=0, keepdims=True)) for n in ns]
    pm = [jnp.exp(dlog[n] - m_t[n]) for n in ns]
    inter = [jnp.exp(m_inter[n] - m_t[n]) for n in ns]
    floor = [jnp.exp(-m_t[n]) for n in ns]
    s = [qk[n] * pm[n] for n in ns]
    sv = [_nn(vt[n].astype(BF16), s[n].astype(BF16)) for n in ns]
    den = [jnp.sum(s[n], axis=0, keepdims=True) + inter[n] * qn[n] for n in ns]
    num = [sv[n] + inter[n] * cq[n] for n in ns]
    dn = [jnp.maximum(jnp.abs(den[n]), floor[n]) for n in ns]
    return [dict(pm=pm[n], s=s[n], inter=inter[n], cq=cq[n], qn=qn[n], num=num[n], den=den[n],
                 floor=floor[n], dn=dn[n]) for n in ns]


def _ml_specs(t, rev):
    nblk = t // ML_TB
    blk = (lambda g: nblk - 1 - g) if rev else (lambda g: g)
    hps = ML_HPS
    tile = lambda c0: pl.BlockSpec((ML_TB, 128 * hps), lambda hg, g, c0=c0: (blk(g), c0 // hps + hg))
    gate = pl.BlockSpec((hps, ML_NB, ML_CHUNK), lambda hg, g: (hg, blk(g), 0))
    cchk = pl.BlockSpec((hps, 1, 128, 128), lambda hg, g: (hg, blk(g), 0, 0))
    nmchk = pl.BlockSpec((hps, 1, 8, 128), lambda hg, g: (hg, blk(g), 0, 0))
    return nblk, blk, tile, gate, cchk, nmchk


def _ml_head_views(refs, hh):
    cols = slice(hh * ML_HEAD_DIM, (hh + 1) * ML_HEAD_DIM)
    return [r.at[:, cols] if len(r.shape) == 2 else r.at[hh] for r in refs]


def _ml_fwd(qk_act, proj, gi, gf, rev, name):
    t = qk_act.shape[0]
    nblk, _, tile, gate, cchk, nmchk = _ml_specs(t, rev)

    def body(*refs):
        for hh in range(ML_HPS):
            one_head(*_ml_head_views(refs, hh))

    def one_head(q_ref, k_ref, v_ref, gi_ref, gf_ref, h_ref, cchk_ref, nmchk_ref, c_ref, nm_ref):
        @pl.when(pl.program_id(1) == 0)
        def _():
            c_ref[...] = jnp.zeros_like(c_ref)
            nm_ref[...] = jnp.zeros_like(nm_ref)
        cchk_ref[0] = c_ref[...]
        nmchk_ref[0] = nm_ref[...]
        eye, csum, _, sees = _ml_consts(rev)
        b_rows, a_rows, imb_rows, mp, mn, decay, order = _ml_gates(
            gi_ref[...], gf_ref[...], nm_ref[1:2, 0:1], csum, rev)
        ld, vt, cp, npv, _, c, nv = _ml_state_scan(q_ref, k_ref, v_ref, a_rows, mn, decay, order,
                                                   c_ref[...], nm_ref[0:1, :])
        c_ref[...] = c
        nm_ref[0:1, :] = nv
        nm_ref[1:2, :] = jnp.broadcast_to(mn[order[-1]], (1, 128))
        rs = _ml_intra_all(ld, vt, b_rows, imb_rows, mp, cp, npv, sees, eye)
        ht = [rs[n]['num'] / rs[n]['dn'] for n in range(ML_NB)]
        for n in range(ML_NB):
            h_ref[n * ML_CHUNK:(n + 1) * ML_CHUNK, :] = ht[n].T

    return pl.pallas_call(
        body, name=name, grid=(ML_HEADS // ML_HPS, nblk),
        in_specs=[tile(0), tile(4), tile(24), gate, gate],
        out_specs=[tile(0), cchk, nmchk],
        out_shape=[jax.ShapeDtypeStruct((t, ML_W), F32),
                   jax.ShapeDtypeStruct((ML_HEADS, nblk, 128, 128), F32),
                   jax.ShapeDtypeStruct((ML_HEADS, nblk, 8, 128), F32)],
        scratch_shapes=[pltpu.VMEM((ML_HPS, 128, 128), F32), pltpu.VMEM((ML_HPS, 8, 128), F32)],
        compiler_params=_cparams(("parallel", "arbitrary")),
    )(qk_act, qk_act, proj, gi, gf)


def _ml_bwd(qk_act, proj, gi, gf, dh, cchk_a, nmchk_a, prev, rev, name):
    t = qk_act.shape[0]
    nblk, _, tile, gate, cchk, nmchk = _ml_specs(t, not rev)

    def body(*refs):
        for hh in range(ML_HPS):
            one_head(*_ml_head_views(refs, hh))

    def one_head(q_ref, k_ref, v_ref, gi_ref, gf_ref, dh_ref, cchk_ref, nmchk_ref, *rest):
        prev_refs = rest[:len(prev)]
        dq_ref, dk_ref, dv_ref, dgi_ref, dgf_ref, dc_ref, dn_ref, db_scr, dbl_scr, di_scr = rest[len(prev):]

        def plus_prev(val, which, rows):
            return val + prev_refs[which][rows, :] if prev else val

        @pl.when(pl.program_id(1) == 0)
        def _():
            dc_ref[...] = jnp.zeros_like(dc_ref)
            dn_ref[...] = jnp.zeros_like(dn_ref)
        eye, csum, csum_t, sees = _ml_consts(rev)
        gfv = gf_ref[...]
        b_rows, a_rows, imb_rows, mp, mn, decay, order = _ml_gates(
            gi_ref[...], gfv, nmchk_ref[0, 1:2, 0:1], csum, rev)
        ld, vt, cp, npv, w_row, _, _ = _ml_state_scan(q_ref, k_ref, v_ref, a_rows, mn, decay, order,
                                                      cchk_ref[0], nmchk_ref[0, 0:1, :])
        ns = range(ML_NB)
        rs = _ml_intra_all(ld, vt, b_rows, imb_rows, mp, cp, npv, sees, eye)
        sls = [ld[n][0] for n in ns]
        qbs = [ld[n][1] for n in ns]
        kbs = [ld[n][2] for n in ns]
        vbs = [ld[n][3].astype(BF16) for n in ns]
        rdn = [1.0 / rs[n]['dn'] for n in ns]
        dnum = [dh_ref[sls[n], :].T * rdn[n] for n in ns]
        hsum = [jnp.sum(dnum[n] * rs[n]['num'], axis=0, keepdims=True) for n in ns]
        dden = [jnp.where(jnp.abs(rs[n]['den']) > rs[n]['floor'],
                          -hsum[n] * rdn[n] * jnp.sign(rs[n]['den']), 0.0) for n in ns]
        dnb = [dnum[n].astype(BF16) for n in ns]
        dsf = [_nn(vbs[n], dnb[n]) + dden[n] for n in ns]
        dv0 = [_nt(rs[n]['s'].astype(BF16), dnb[n]) for n in ns]
        gb = [(dsf[n] * rs[n]['pm']).astype(BF16) for n in ns]
        cpb = [cp[n].astype(BF16) for n in ns]
        idd = [rs[n]['inter'] * dden[n] for n in ns]
        idn = [(rs[n]['inter'] * dnum[n]).astype(BF16) for n in ns]
        dqa = [_tn(gb[n], kbs[n]) for n in ns]
        dqc = [_tn(idn[n], cpb[n]) for n in ns]
        dqn = [_outer_rows(idd[n], npv[n].astype(BF16)) for n in ns]
        dk0 = [_nn(gb[n], qbs[n]) for n in ns]
        xs = [_nn(idn[n], qbs[n]) for n in ns]
        for n in ns:
            dq_ref[sls[n], :] = plus_prev(dqa[n] + dqc[n] + dqn[n], 0, sls[n])
        rr = [dsf[n] * rs[n]['s'] for n in ns]
        dinter = [jnp.sum(dnum[n] * rs[n]['cq'], axis=0, keepdims=True) + dden[n] * rs[n]['qn'] for n in ns]
        dbt = [jnp.sum(rr[n], axis=0, keepdims=True) + dinter[n] * rs[n]['inter'] for n in ns]
        dimb = [jnp.sum(rr[n], axis=1, keepdims=True) for n in ns]
        xns = [_nn(_row8(idd[n]).astype(BF16), qbs[n])[0:1] for n in ns]
        dcn, dnn = {}, {}
        dc, dn = dc_ref[...], dn_ref[0:1, :]
        for n in order[::-1]:
            dcn[n], dnn[n] = dc, dn
            dc = decay[n] * dc + xs[n]
            dn = decay[n] * dn + xns[n]
        dc_ref[...] = dc
        dn_ref[0:1, :] = dn
        kscale = ML_HEAD_DIM ** -0.5
        dcb = [dcn[n].astype(BF16) for n in ns]
        z = [_nn(vbs[n], dcb[n]) for n in ns]
        kd = [_nt(kbs[n], dcb[n]) for n in ns]
        ddecay = [jnp.sum(jnp.sum(dcn[n] * cp[n], axis=1, keepdims=True), axis=0, keepdims=True)
                  + jnp.sum(dnn[n] * npv[n], axis=1, keepdims=True) for n in ns]
        zd = [z[n] + dnn[n] for n in ns]
        dw = [jnp.sum(zd[n] * kbs[n].astype(F32), axis=1, keepdims=True) for n in ns]
        wcol = [_col(w_row[n], eye) for n in ns]
        for n in ns:
            dv_ref[sls[n], :] = plus_prev(dv0[n] + wcol[n] * kd[n], 2, sls[n]).astype(dv_ref.dtype)
            dk_ref[sls[n], :] = plus_prev((dk0[n] + wcol[n] * zd[n]) * kscale, 1, sls[n])
        da = [dw[n] * wcol[n] for n in ns]
        dbl = [jnp.sum(da[n], axis=0, keepdims=True) + ddecay[n] * decay[n] for n in ns]
        key_row = [_rowof(dimb[n] + da[n], eye) for n in ns]
        for n in ns:
            db_scr[n:n + 1, :] = dbt[n] - key_row[n]
            di_scr[n:n + 1, :] = key_row[n]
            dbl_scr[n:n + 1, :] = jnp.broadcast_to(dbl[n], (1, ML_CHUNK))
        dlf = jnp.dot(db_scr[...], csum_t, precision=HI, preferred_element_type=F32) + dbl_scr[...]
        dgf_ref[...] = dlf * jax.nn.sigmoid(-gfv)
        dgi_ref[...] = di_scr[...]

    nc = t // ML_CHUNK
    o512 = jax.ShapeDtypeStruct((t, ML_W), F32)
    og = jax.ShapeDtypeStruct((ML_HEADS, nc, ML_CHUNK), F32)
    return pl.pallas_call(
        body, name=name, grid=(ML_HEADS // ML_HPS, nblk),
        in_specs=[tile(0), tile(4), tile(24), gate, gate, tile(0), cchk, nmchk] + [tile(0)] * len(prev),
        out_specs=[tile(0), tile(0), tile(0), gate, gate],
        out_shape=[o512, o512, jax.ShapeDtypeStruct((t, ML_W), BF16 if prev else F32), og, og],
        scratch_shapes=[pltpu.VMEM((ML_HPS, 128, 128), F32), pltpu.VMEM((ML_HPS, 8, 128), F32)]
        + [pltpu.VMEM((ML_HPS, ML_NB, ML_CHUNK), F32)] * 3,
        compiler_params=_cparams(("parallel", "arbitrary")),
    )(qk_act, qk_act, proj, gi, gf, dh, cchk_a, nmchk_a, *prev)


def _gate_rows(gates16, t):
    g = gates16.reshape(t // ML_CHUNK, ML_CHUNK, 4, ML_HEADS).transpose(2, 3, 0, 1)
    return g[0], g[1], g[2], g[3]


def _gate_cols(dgi_f, dgf_f, dgi_b, dgf_b, t):
    g = jnp.stack([dgi_f, dgf_f, dgi_b, dgf_b]).transpose(2, 3, 0, 1).reshape(t, 4 * ML_HEADS)
    return jnp.pad(g, ((0, 0), (0, 128 - 4 * ML_HEADS)))


def _local_step(x, target, shift, scale, gate, norm_w, w_in_t, b_in_p, conv_w8, conv_b, rpb,
                ml_norm_w, w_out_b, final_norm_w):
    t = x.shape[0]
    rows = t // GRID_W
    tm = 512
    proj, gates = _in_proj(x, norm_w, scale, shift, w_in_t, b_in_p)
    tbl = _na_bias_table(rpb, rows)
    o_na, lse_na = _na_fwd(proj, tbl)
    pre, qk_act = _conv_fwd(proj, conv_w8, conv_b, 2 * tm)
    gi_f, gf_f, gi_b, gf_b = _gate_rows(gates[:, :4 * ML_HEADS], t)
    h_f, cchk_f, nmchk_f = _ml_fwd(qk_act, proj, gi_f, gf_f, False, "ml_fwd_f")
    h_b, cchk_b, nmchk_b = _ml_fwd(qk_act, proj, gi_b, gf_b, True, "ml_fwd_b")
    (loss, dres, d_ona, d_naz, dhs, d_o, d_z, dgate, g_fnw, g_mlnw, g_w_out) = _tail(
        o_na, proj, h_f, h_b, x, target, gate, ml_norm_w, final_norm_w, w_out_b)
    dq_na, dk_na, dv_na, rpbacc = _na_bwd(proj, tbl, d_ona, o_na, lse_na)
    g_rpb = _rpb_reduce(rpbacc, rows)
    dq_f, dk_f, dv_f, dgi_f, dgf_f = _ml_bwd(qk_act, proj, gi_f, gf_f, dhs, cchk_f, nmchk_f, (),
                                             False, "ml_bwd_f")
    dq_ml, dk_ml, dv_ml, dgi_b, dgf_b = _ml_bwd(qk_act, proj, gi_b, gf_b, dhs, cchk_b, nmchk_b, (dq_f, dk_f, dv_f),
                                                True, "ml_bwd_b")
    du, g_conv_w, g_conv_b = _conv_bwd(dq_ml, dk_ml, pre, proj, conv_w8, tm)
    dgates = _gate_cols(dgi_f, dgf_f, dgi_b, dgf_b, t)
    grad_x, g_w_in, g_b_in, dscale, dshift, g_nw = _in_bwd(
        [dq_na, dk_na, dv_na, d_naz, du, dv_ml, d_o, d_z, dgates], x, dres, w_in_t, norm_w, scale, shift)
    dmod = jnp.concatenate([dshift, dscale, dgate], axis=1)
    return (loss, grad_x, dmod, g_nw, g_w_in, g_b_in, g_conv_w, g_conv_b, g_rpb, g_mlnw, g_w_out, g_fnw)


MESH = pl.DeviceIdType.MESH
N_DEV = 8
ANY = pl.BlockSpec(memory_space=pl.ANY)
WHOLE_VMEM = pl.BlockSpec(memory_space=pltpu.VMEM)


def _allgather8(blocks, name):
    na = len(blocks)

    def body(*refs):
        x_refs = refs[:na]
        out_refs = refs[na:2 * na]
        send_sems, recv_sems, local_sems = refs[2 * na:]
        x, y, c = lax.axis_index("x"), lax.axis_index("y"), lax.axis_index("c")
        me, sibling = (x, y, c), (x, y, 1 - c)
        chips = [(1 - x, y), (x, 1 - y), (1 - x, 1 - y)]

        def rows(a, px, py, pc):
            return out_refs[a].at[4 * px + 2 * py + pc]

        def copy(a, k, block, to, src=None):
            return pltpu.make_async_remote_copy(
                src_ref=rows(a, *block) if src is None else src, dst_ref=rows(a, *block),
                send_sem=send_sems.at[a, k], recv_sem=recv_sems.at[a, k],
                device_id=to, device_id_type=MESH)

        mine, first, passed = [], [], []
        for a in range(na):
            cp = pltpu.make_async_copy(x_refs[a], rows(a, *me), local_sems.at[a])
            cp.start()
            mine.append(cp)
            first.append(copy(a, 0, me, sibling, src=x_refs[a]))
            first += [copy(a, 1 + j, me, (*chip, c), src=x_refs[a]) for j, chip in enumerate(chips)]
        for cp in first:
            cp.start()
        for a in range(na):
            for j, chip in enumerate(chips):
                copy(a, 1 + j, (*chip, c), me).wait_recv()
                fwd = copy(a, 4 + j, (*chip, c), sibling)
                fwd.start()
                passed.append(fwd)
        for a in range(na):
            copy(a, 0, sibling, me).wait_recv()
            for j, chip in enumerate(chips):
                copy(a, 4 + j, (*chip, 1 - c), me).wait_recv()
        for cp in first + passed:
            cp.wait_send()
        for cp in mine:
            cp.wait()

    return pl.pallas_call(
        body, name=name,
        out_shape=[jax.ShapeDtypeStruct((N_DEV,) + b.shape, b.dtype) for b in blocks],
        in_specs=[WHOLE_VMEM] * na, out_specs=[WHOLE_VMEM] * na,
        scratch_shapes=[pltpu.SemaphoreType.DMA((na, 7)), pltpu.SemaphoreType.DMA((na, 7)),
                        pltpu.SemaphoreType.DMA((na,))],
        compiler_params=pltpu.CompilerParams(vmem_limit_bytes=VMEM_LIMIT),
    )(*blocks)


def _pair_exchange(arrs, name):
    na = len(arrs)

    def body(*refs):
        in_refs = refs[:na]
        out_refs = refs[na:2 * na]
        send_sems, recv_sems = refs[2 * na:]
        sibling = (lax.axis_index("x"), lax.axis_index("y"), 1 - lax.axis_index("c"))
        copies = [pltpu.make_async_remote_copy(
            src_ref=in_refs[a], dst_ref=out_refs[a], send_sem=send_sems.at[a], recv_sem=recv_sems.at[a],
            device_id=sibling, device_id_type=MESH) for a in range(na)]
        for cp in copies:
            cp.start()
        for cp in copies:
            cp.wait()

    return pl.pallas_call(
        body, name=name,
        out_shape=[jax.ShapeDtypeStruct(a.shape, a.dtype) for a in arrs],
        in_specs=[ANY] * na, out_specs=[ANY] * na,
        scratch_shapes=[pltpu.SemaphoreType.DMA((na,)), pltpu.SemaphoreType.DMA((na,))],
    )(*arrs)


def _chip_exchange(arrs, name):
    na = len(arrs)

    def body(*refs):
        in_refs = refs[:na]
        out_refs = refs[na:2 * na]
        send_sems, recv_sems, local_sems = refs[2 * na:]
        x, y, c = lax.axis_index("x"), lax.axis_index("y"), lax.axis_index("c")
        my_chip = 2 * x + y
        chips = [(1 - x, y), (x, 1 - y), (1 - x, 1 - y)]
        local, remote = [], []
        for a in range(na):
            cp = pltpu.make_async_copy(in_refs[a].at[my_chip], out_refs[a].at[my_chip], local_sems.at[a])
            cp.start()
            local.append(cp)
            for j, (px, py) in enumerate(chips):
                cp = pltpu.make_async_remote_copy(
                    src_ref=in_refs[a].at[2 * px + py], dst_ref=out_refs[a].at[my_chip],
                    send_sem=send_sems.at[a, j], recv_sem=recv_sems.at[a, j],
                    device_id=(px, py, c), device_id_type=MESH)
                cp.start()
                remote.append(cp)
        for cp in remote:
            cp.wait()
        for cp in local:
            cp.wait()

    return pl.pallas_call(
        body, name=name,
        out_shape=[jax.ShapeDtypeStruct(a.shape, a.dtype) for a in arrs],
        in_specs=[ANY] * na, out_specs=[ANY] * na,
        scratch_shapes=[pltpu.SemaphoreType.DMA((na, 3)), pltpu.SemaphoreType.DMA((na, 3)),
                        pltpu.SemaphoreType.DMA((na,))],
    )(*arrs)


def _rows_tile(r):
    for cand in (512, 256, 128, 64, 32, 16, 8):
        if r % cand == 0:
            return cand
    return r


def _add2(a, b, name, out_dtype):
    s, r, n = a.shape
    tr = _rows_tile(r)

    def body(a_ref, b_ref, o_ref):
        o_ref[...] = (a_ref[...] + b_ref[...]).astype(out_dtype)

    spec = pl.BlockSpec((1, tr, n), lambda i, j: (i, j, 0))
    return pl.pallas_call(
        body, name=name, grid=(s, r // tr), in_specs=[spec, spec], out_specs=spec,
        out_shape=jax.ShapeDtypeStruct(a.shape, out_dtype),
        compiler_params=_cparams(("parallel", "parallel")),
    )(a, b)


def _sum_slabs(a, name):
    s, r, n = a.shape
    tr = _rows_tile(r)

    def body(a_ref, o_ref):
        acc = a_ref[0].astype(F32)
        for k in range(1, s):
            acc = acc + a_ref[k].astype(F32)
        o_ref[...] = acc

    return pl.pallas_call(
        body, name=name, grid=(r // tr,),
        in_specs=[pl.BlockSpec((s, tr, n), lambda i: (0, i, 0))],
        out_specs=pl.BlockSpec((tr, n), lambda i: (i, 0)),
        out_shape=jax.ShapeDtypeStruct((r, n), F32),
        compiler_params=_cparams(("parallel",)),
    )(a)


ADAMW_WHOLE = 64 * 1024


def _adamw(w, g, m, v, name):
    r, n = w.shape
    if r * n <= ADAMW_WHOLE:
        blk, grid, imap = (r, n), (1,), (lambda i: (0, 0))
    elif r % 8 == 0:
        blk, grid, imap = (_rows_tile(r), n), (r // _rows_tile(r),), (lambda i: (i, 0))
    else:
        blk, grid, imap = (r, 128), (n // 128,), (lambda i: (0, i))
    c1 = 1.0 / (1.0 - ADAM_B1 ** ADAM_STEP)
    c2 = 1.0 / (1.0 - ADAM_B2 ** ADAM_STEP)

    def body(w_ref, g_ref, m_ref, v_ref, d_ref, nm_ref, nv_ref):
        gv = g_ref[...]
        nm = ADAM_B1 * m_ref[...] + (1.0 - ADAM_B1) * gv
        nv = ADAM_B2 * v_ref[...] + (1.0 - ADAM_B2) * (gv * gv)
        nm_ref[...] = nm
        nv_ref[...] = nv
        d_ref[...] = -ADAM_LR * ((nm * c1) / (jnp.sqrt(nv * c2) + ADAM_EPS) + ADAM_WD * w_ref[...])

    spec = pl.BlockSpec(blk, imap)
    o = jax.ShapeDtypeStruct((r, n), F32)
    return pl.pallas_call(
        body, name=name, grid=grid, in_specs=[spec] * 4, out_specs=[spec] * 3, out_shape=[o, o, o],
        compiler_params=_cparams(("parallel",)),
    )(w, g, m, v)


def _mod_fwd(c_all, w_ada_s, b_ada_s):
    def body(c_ref, w_ref, b_ref, o_ref):
        o_ref[...] = jnp.dot(_silu(c_ref[...]), w_ref[...], precision=HI, preferred_element_type=F32) + b_ref[...]

    return pl.pallas_call(
        body, name="mod_fwd", out_shape=jax.ShapeDtypeStruct((c_all.shape[0], w_ada_s.shape[1]), F32),
        in_specs=[WHOLE_VMEM] * 3, out_specs=WHOLE_VMEM,
        compiler_params=pltpu.CompilerParams(vmem_limit_bytes=VMEM_LIMIT),
    )(c_all, w_ada_s, b_ada_s)


def _wada_grad(c_all, dmod_s):
    def body(c_ref, d_ref, o_ref):
        o_ref[...] = lax.dot_general(_silu(c_ref[...]), d_ref[...], (((0,), (0,)), ((), ())),
                                     precision=HI, preferred_element_type=F32)

    return pl.pallas_call(
        body, name="w_ada_grad", out_shape=jax.ShapeDtypeStruct((c_all.shape[1], dmod_s.shape[1]), F32),
        in_specs=[WHOLE_VMEM] * 2, out_specs=WHOLE_VMEM,
        compiler_params=pltpu.CompilerParams(vmem_limit_bytes=VMEM_LIMIT),
    )(c_all, dmod_s)


SMALL_ROWS = 24


def _pad_rows(v, nrows):
    v = v.reshape(-1)
    return jnp.pad(v, (0, nrows * 1024 - v.shape[0])).reshape(nrows, 1024)


def _pack_small(b_ada, norm_w, b_in, conv_w_full, conv_b, rpb, ml_norm_w, final_norm_w, last):
    parts = [_pad_rows(b_ada, 3), _pad_rows(norm_w, 1), _pad_rows(b_in, 5), _pad_rows(conv_w_full, 5),
             _pad_rows(conv_b, 1), _pad_rows(rpb, 4), _pad_rows(ml_norm_w, 1), _pad_rows(final_norm_w, 1),
             _pad_rows(last, 3)]
    return jnp.concatenate(parts, axis=0)


def _unpack_small(p):
    return dict(b_ada=p[0:3].reshape(1, 3072), norm_w=p[3:4], b_in=p[4:9].reshape(-1)[:IN_W].reshape(1, IN_W),
                conv_w=p[9:14], conv_b=p[14:15],
                rpb=p[15:19].reshape(-1)[:NA_HEADS * 15 * 31].reshape(1, NA_HEADS, 15, 31),
                ml_norm_w=p[19:20, :ML_W], final_norm_w=p[20], last=p[21])


def kernel(x, c, w_ada, b_ada, norm_w, w_in, b_in, conv_w, conv_b, rpb, ml_norm_w, w_out, final_norm_w, loss_target, m_w_ada, m_b_ada, m_norm_w, m_w_in, m_b_in, m_conv_w, m_conv_b, m_rpb, m_ml_norm_w, m_w_out, m_final_norm_w, v_w_ada, v_b_ada, v_norm_w, v_w_in, v_b_in, v_conv_w, v_conv_b, v_rpb, v_ml_norm_w, v_w_out, v_final_norm_w):
    xi, yi, ci = lax.axis_index("x"), lax.axis_index("y"), lax.axis_index("c")
    chip = 2 * xi + yi
    dev = 2 * chip + ci
    t = x.shape[1]
    ada_n = w_ada.shape[2]
    in_n = w_in.shape[2]
    out_r = w_out.shape[1]

    c_blk = jnp.pad(c, ((0, 7), (0, 0)))
    w_in_t, m_w_in_t, v_w_in_t = w_in[0].T, m_w_in[0].T, v_w_in[0].T
    in_h = in_n // 2
    w_in_half = lax.dynamic_slice_in_dim(w_in_t, ci * in_h, in_h, axis=0).astype(BF16)
    w_out_half = lax.dynamic_slice_in_dim(w_out[0], ci * (out_r // 2), out_r // 2, axis=0).astype(BF16)
    conv_blk = jnp.pad(conv_w[0], ((0, 3), (0, 0)))
    c_g, conv_g, w_in_g, w_out_g = _allgather8([c_blk, conv_blk, w_in_half, w_out_half], "gather_c_weights")
    c_all = c_g[:, 0]
    w_out_g = w_out_g.reshape(D_MODEL, D_MODEL)
    b_ada_s = lax.dynamic_slice_in_dim(b_ada, chip * ada_n, ada_n, axis=1)
    mod_s = _mod_fwd(c_all, w_ada[0], b_ada_s)
    (mod_g,) = _allgather8([mod_s], "gather_mod")
    mod_mine = lax.dynamic_index_in_dim(mod_g, dev, axis=1, keepdims=False)
    mod = mod_mine[0::2].reshape(1, 3 * D_MODEL)
    shift, scale, gate = mod[:, :D_MODEL], mod[:, D_MODEL:2 * D_MODEL], mod[:, 2 * D_MODEL:]

    w_in_tp = jnp.pad(w_in_g.reshape(IN_W, D_MODEL), ((0, IN_PAD - IN_W), (0, 0)))
    b_in_p = jnp.pad(b_in, ((0, 0), (0, IN_PAD - IN_W)))
    conv_w8 = conv_g.reshape(4, 2, 8, conv_w.shape[2])[:, 0].transpose(1, 0, 2).reshape(8, D_MODEL)

    (loss, grad_x, dmod, g_nw, g_w_in, g_b_in, g_conv_w, g_conv_b, g_rpb, g_mlnw, g_w_out, g_fnw) = _local_step(
        x[0], loss_target[0], shift, scale, gate, norm_w, w_in_tp, b_in_p, conv_w8, conv_b, rpb[0],
        ml_norm_w, w_out_g, final_norm_w.reshape(1, D_MODEL))

    g_in_t = g_w_in

    def halves(a, per_chip, h):
        return jnp.stack([lax.dynamic_slice_in_dim(a, k * per_chip + h * (per_chip // 2), per_chip // 2, axis=0)
                          for k in range(4)])

    ri, ro = _pair_exchange([halves(g_in_t, in_n, 1 - ci), halves(g_w_out, out_r, 1 - ci)], "rs_pair")
    pi = _add2(halves(g_in_t, in_n, ci), ri, "rs_pair_add_in", BF16)
    po = _add2(halves(g_w_out, out_r, ci), ro, "rs_pair_add_out", BF16)
    qi, qo = _chip_exchange([pi, po], "rs_chips")
    si = _sum_slabs(qi, "rs_sum_in")
    so = _sum_slabs(qo, "rs_sum_out")
    ti, to = _pair_exchange([si, so], "rs_share")
    g_w_in_s = jnp.where(ci == 0, jnp.concatenate([si, ti], axis=0), jnp.concatenate([ti, si], axis=0))
    g_w_out_s = jnp.where(ci == 0, jnp.concatenate([so, to], axis=0), jnp.concatenate([to, so], axis=0))

    small = _pack_small(dmod, g_nw, g_b_in[:, :IN_W], g_conv_w[:CONV_W], g_conv_b, g_rpb, g_mlnw, g_fnw,
                        jnp.pad(loss, ((0, 0), (0, 1024 - 128))))
    (small_g,) = _allgather8([small], "gather_small")
    small_sum = _sum_slabs(small_g, "small_sum")
    gs = _unpack_small(small_sum)
    dmod_all = small_g[:, 0:3].reshape(N_DEV, 3 * D_MODEL)
    g_w_ada_s = _wada_grad(c_all, lax.dynamic_slice_in_dim(dmod_all, chip * ada_n, ada_n, axis=1))
    g_conv_w_s = lax.dynamic_slice_in_dim(gs['conv_w'], chip * conv_w.shape[2], conv_w.shape[2], axis=1)
    loss_total = gs['last'][0]

    small_names = ('b_ada', 'norm_w', 'b_in', 'conv_b', 'rpb', 'ml_norm_w', 'final_norm_w')
    small_w = (b_ada, norm_w, b_in, conv_b, rpb, ml_norm_w, final_norm_w)
    small_m = (m_b_ada, m_norm_w, m_b_in, m_conv_b, m_rpb, m_ml_norm_w, m_final_norm_w)
    small_v = (v_b_ada, v_norm_w, v_b_in, v_conv_b, v_rpb, v_ml_norm_w, v_final_norm_w)
    ds_, nms, nvs = {}, {}, {}
    for nm_, w_, m_, v_ in zip(small_names, small_w, small_m, small_v):
        two_d = (NA_HEADS, w_.size // NA_HEADS) if nm_ == 'rpb' else (1, w_.size)
        outs = _adamw(w_.reshape(two_d), gs[nm_].reshape(two_d), m_.reshape(two_d), v_.reshape(two_d),
                      "adamw_" + nm_)
        ds_[nm_], nms[nm_], nvs[nm_] = [o.reshape(w_.shape) for o in outs]
    d_ada, nm_ada, nv_ada = _adamw(w_ada[0], g_w_ada_s, m_w_ada[0], v_w_ada[0], "adamw_w_ada")
    d_in, nm_in, nv_in = _adamw(w_in_t, g_w_in_s, m_w_in_t, v_w_in_t, "adamw_w_in")
    d_out, nm_out, nv_out = _adamw(w_out[0], g_w_out_s, m_w_out[0], v_w_out[0], "adamw_w_out")
    d_cw, nm_cw, nv_cw = _adamw(conv_w[0], g_conv_w_s, m_conv_w[0], v_conv_w[0], "adamw_conv_w")

    def group(big_ada, big_in, big_out, cw, sm):
        return (big_ada[None], sm['b_ada'], sm['norm_w'], big_in.T[None], sm['b_in'], cw[None], sm['conv_b'],
                sm['rpb'], sm['ml_norm_w'], big_out[None], sm['final_norm_w'])

    return ((loss_total, grad_x[None])
            + group(g_w_ada_s, g_w_in_s, g_w_out_s, g_conv_w_s, gs)
            + group(d_ada, d_in, d_out, d_cw, ds_)
            + group(nm_ada, nm_in, nm_out, nm_cw, nms)
            + group(nv_ada, nv_in, nv_out, nv_cw, nvs))
```

```python
import functools

import numpy as np
import jax
import jax.numpy as jnp
from jax import lax
from jax.experimental import pallas as pl
from jax.experimental.pallas import tpu as pltpu

F32 = jnp.float32
BF16 = jnp.bfloat16
HI = lax.Precision.HIGHEST

D_MODEL = 1024
GRID_W = 64
NA_W = 512
NA_HEAD_DIM = 64
NA_HEADS = 8
NA_KH = 8
NA_KW = 16
ML_W = 512
ML_HEADS = 4
ML_HEAD_DIM = 128
ML_CHUNK = 128
CONV_W = 5
EPS = 1e-6
IN_W = 4 * NA_W + 5 * ML_W + 4 * ML_HEADS
IN_MAIN = 4 * NA_W + 5 * ML_W
IN_PAD = IN_MAIN + 128
NEG = -1e30

ADAM_LR = 0.001
ADAM_B1 = 0.9
ADAM_B2 = 0.999
ADAM_EPS = 1e-08
ADAM_WD = 0.01
ADAM_STEP = 10

NA_QROWS = 8
NA_KROWS = 16
NA_QT = NA_QROWS * GRID_W
NA_KT = NA_KROWS * GRID_W
NA_KCH = 256
NA_RC = 32
ML_NB = 32
ML_TB = ML_NB * ML_CHUNK
ML_HPS = 1

VMEM_LIMIT = 56 * 1024 * 1024
IN_BWD_VMEM_LIMIT = 60 * 1024 * 1024


def _cparams(sem, vmem=VMEM_LIMIT):
    return pltpu.CompilerParams(dimension_semantics=sem, vmem_limit_bytes=vmem)


def _silu(x):
    return x * jax.nn.sigmoid(x)


def _dsilu(x):
    s = jax.nn.sigmoid(x)
    return s * (1.0 + x * (1.0 - s))


def _dot(a, b, dims):
    return lax.dot_general(a, b, (dims, ((), ())), preferred_element_type=F32)


def _nn(a, b):
    return _dot(a, b, ((1,), (0,)))


def _nt(a, b):
    return _dot(a, b, ((1,), (1,)))


def _tn(a, b):
    return _dot(a, b, ((0,), (0,)))


def _row(n):
    return pl.BlockSpec((1, n), lambda i: (0, 0))


def _modulated_norm(xv, nw, sc, sh):
    r = lax.rsqrt(jnp.mean(xv * xv, axis=-1, keepdims=True) + EPS)
    xn = xv * r
    return xn * nw * (1.0 + sc) + sh, xn, r


IN_TN = 768


def _in_proj(x, norm_w, scale, shift, w_in_t, b_in_p):
    t, d = x.shape
    tm = 2048
    gcol = IN_MAIN // 128

    def body(x_ref, nw_ref, sc_ref, sh_ref, w_ref, b_ref, wg_ref, bg_ref, proj_ref, g_ref, h_scr):
        @pl.when(pl.program_id(1) == 0)
        def _():
            h, _, _ = _modulated_norm(x_ref[...], nw_ref[...], sc_ref[...], sh_ref[...])
            h_scr[...] = h.astype(BF16)
            g_ref[...] = _nt(h_scr[...], wg_ref[...]) + bg_ref[...]
        proj_ref[...] = (_nt(h_scr[...], w_ref[...]) + b_ref[...]).astype(BF16)

    row = lambda n: pl.BlockSpec((1, n), lambda i, j: (0, 0))
    return pl.pallas_call(
        body, name="in_proj", grid=(t // tm, IN_MAIN // IN_TN),
        in_specs=[pl.BlockSpec((tm, d), lambda i, j: (i, 0)), row(d), row(d), row(d),
                  pl.BlockSpec((IN_TN, d), lambda i, j: (j, 0)), pl.BlockSpec((1, IN_TN), lambda i, j: (0, j)),
                  pl.BlockSpec((128, d), lambda i, j: (gcol, 0)), pl.BlockSpec((1, 128), lambda i, j: (0, gcol))],
        out_specs=[pl.BlockSpec((tm, IN_TN), lambda i, j: (i, j)), pl.BlockSpec((tm, 128), lambda i, j: (i, 0))],
        out_shape=[jax.ShapeDtypeStruct((t, IN_MAIN), BF16), jax.ShapeDtypeStruct((t, 128), F32)],
        scratch_shapes=[pltpu.VMEM((tm, d), BF16)],
        compiler_params=_cparams(("parallel", "arbitrary")),
    )(x, norm_w, scale, shift, w_in_t, b_in_p, w_in_t, b_in_p)


def _ml_norm_parts(hs, o, z, nw):
    outs = []
    for hh in range(ML_HEADS):
        sl = slice(hh * ML_HEAD_DIM, (hh + 1) * ML_HEAD_DIM)
        so = jax.nn.sigmoid(o[:, sl])
        hm = hs[:, sl] * so
        mu = jnp.mean(hm, axis=-1, keepdims=True)
        cen = hm - mu
        var = jnp.mean(cen * cen, axis=-1, keepdims=True)
        rs = lax.rsqrt(var + EPS)
        outs.append((sl, cen * rs, rs, so))
    return outs


def _tail(o_na, proj, h_f, h_b, x, target, gate, ml_norm_w, fnw, w_out_b):
    t, d = x.shape
    tm = 512

    def body(ona_ref, naz_ref, hf_ref, hb_ref, o_ref, z_ref, x_ref, tg_ref, g_ref, nw_ref, fw_ref, w_ref,
             loss_ref, dres_ref, dona_ref, dnaz_ref, dhs_ref, do_ref, dz_ref, dgate_ref, gfw_ref, gnw_ref,
             gwo_ref, mix_scr):
        @pl.when(pl.program_id(0) == 0)
        def _():
            for r in (loss_ref, dgate_ref, gfw_ref, gnw_ref, gwo_ref):
                r[...] = jnp.zeros_like(r)
        naz = naz_ref[...].astype(F32)
        ona = ona_ref[...]
        sg_naz = jax.nn.sigmoid(naz)
        sna = naz * sg_naz
        mix_scr[:, 0:NA_W] = (ona * sna).astype(BF16)
        hs = hf_ref[...] + hb_ref[...]
        z = z_ref[...].astype(F32)
        ov = o_ref[...].astype(F32)
        parts = _ml_norm_parts(hs, ov, z, nw_ref[...])
        sgz = [jax.nn.sigmoid(z[:, sl]) for sl, _, _, _ in parts]
        for (sl, xn, _, _), sg in zip(parts, sgz):
            mix_scr[:, NA_W + sl.start:NA_W + sl.stop] = (xn * nw_ref[:, sl] * (z[:, sl] * sg)).astype(BF16)
        mixb = mix_scr[...]
        wv = w_ref[...]
        yv = _nn(mixb, wv)
        gate_v = g_ref[...]
        hres = x_ref[...] + gate_v * yv
        r = lax.rsqrt(jnp.mean(hres * hres, axis=-1, keepdims=True) + EPS)
        xnf = hres * r
        err = xnf * fw_ref[...] - tg_ref[...]
        loss_ref[...] += 0.5 * jnp.sum(jnp.sum(err * err, axis=-1, keepdims=True) * (1.0 / d), axis=0, keepdims=True)
        dout = err * (1.0 / d)
        gfw_ref[...] += jnp.sum(dout * xnf, axis=0, keepdims=True)
        dxn = dout * fw_ref[...]
        dres = r * (dxn - xnf * jnp.mean(dxn * xnf, axis=-1, keepdims=True))
        dres_ref[...] = dres
        dgate_ref[...] += jnp.sum(dres * yv, axis=0, keepdims=True)
        dyb = (dres * gate_v).astype(BF16)
        gwo_ref[...] += _tn(mixb, dyb)
        dmix = _nt(dyb, wv)
        dna = dmix[:, 0:NA_W]
        dona_ref[...] = dna * sna
        dnaz_ref[...] = (dna * ona * (sg_naz * (1.0 + naz * (1.0 - sg_naz)))).astype(BF16)
        for (sl, xn, rs, so), sg in zip(parts, sgz):
            dyv = dmix[:, NA_W + sl.start:NA_W + sl.stop]
            zz = z[:, sl]
            sz = zz * sg
            w = nw_ref[:, sl]
            dz_ref[:, sl] = (dyv * xn * w * (sg * (1.0 + zz * (1.0 - sg)))).astype(BF16)
            gnw_ref[:, sl] += jnp.sum(dyv * xn * sz, axis=0, keepdims=True)
            dxm = dyv * w * sz
            dhm = rs * (dxm - jnp.mean(dxm, axis=-1, keepdims=True)
                        - xn * jnp.mean(dxm * xn, axis=-1, keepdims=True))
            dhs_ref[:, sl] = dhm * so
            do_ref[:, sl] = (dhm * hs[:, sl] * so * (1.0 - so)).astype(BF16)

    blk = lambda c: pl.BlockSpec((tm, 512), lambda i, c=c: (i, c))
    full = pl.BlockSpec((tm, d), lambda i: (i, 0))
    o512 = jax.ShapeDtypeStruct((t, 512), F32)
    b512 = jax.ShapeDtypeStruct((t, 512), BF16)
    whole = pl.BlockSpec((d, d), lambda i: (0, 0))
    return pl.pallas_call(
        body, name="tail", grid=(t // tm,),
        in_specs=[blk(0), blk(3), blk(0), blk(0), blk(7), blk(8), full, full, _row(d), _row(ML_W), _row(d), whole],
        out_specs=[pl.BlockSpec((1, 128), lambda i: (0, 0)), full] + [blk(0)] * 5
        + [_row(d), _row(d), _row(ML_W), whole],
        out_shape=[jax.ShapeDtypeStruct((1, 128), F32), jax.ShapeDtypeStruct((t, d), F32),
                   o512, b512, o512, b512, b512]
        + [jax.ShapeDtypeStruct((1, d), F32), jax.ShapeDtypeStruct((1, d), F32),
           jax.ShapeDtypeStruct((1, ML_W), F32), jax.ShapeDtypeStruct((d, d), F32)],
        scratch_shapes=[pltpu.VMEM((tm, d), BF16)],
        compiler_params=_cparams(("arbitrary",)),
    )(o_na, proj, h_f, h_b, proj, proj, x, target, gate, ml_norm_w, fnw, w_out_b)


def _in_bwd(pieces, x, dres, w_in_t, norm_w, scale, shift):
    t, d = x.shape
    tm = 512
    nt = t // tm
    widths = [p.shape[1] for p in pieces]
    offs = [sum(widths[:k]) for k in range(len(widths))]
    assert sum(widths) == IN_PAD
    npc = len(pieces)

    def body(*refs):
        p_refs = refs[:npc]
        (x_ref, dres_ref, w_hbm, nw_ref, sc_ref, sh_ref,
         gx_ref, gw_hbm, gb_ref, dsc_ref, dsh_ref, gnw_ref, w_vmem, acc, stage, sem) = refs[npc:]
        i = pl.program_id(0)

        @pl.when(i == 0)
        def _():
            cp = pltpu.make_async_copy(w_hbm, w_vmem, sem.at[0])
            cp.start()
            acc[...] = jnp.zeros_like(acc)
            gb_ref[...] = jnp.zeros_like(gb_ref)
            dsc_ref[...] = jnp.zeros_like(dsc_ref)
            dsh_ref[...] = jnp.zeros_like(dsh_ref)
            gnw_ref[...] = jnp.zeros_like(gnw_ref)
            cp.wait()

        nw = nw_ref[...]
        s1 = 1.0 + sc_ref[...]
        h, xn, r = _modulated_norm(x_ref[...], nw, sc_ref[...], sh_ref[...])
        hb = h.astype(BF16)
        dhv = jnp.zeros((tm, d), F32)
        for p_ref, c0, w in zip(p_refs, offs, widths):
            pt = p_ref[...]
            pb = pt.astype(BF16)
            dhv = dhv + _nn(pb, w_vmem[c0:c0 + w, :])
            acc[:, c0:c0 + w] += _tn(hb, pb)
            gb_ref[:, c0:c0 + w] += jnp.sum(pt.astype(F32), axis=0, keepdims=True)
        dsh_ref[...] += jnp.sum(dhv, axis=0, keepdims=True)
        dsc_ref[...] += jnp.sum(dhv * xn * nw, axis=0, keepdims=True)
        gnw_ref[...] += jnp.sum(dhv * xn * s1, axis=0, keepdims=True)
        dxn = dhv * nw * s1
        gx_ref[...] = dres_ref[...] + r * (dxn - xn * jnp.mean(dxn * xn, axis=-1, keepdims=True))

        @pl.when(i == nt - 1)
        def _():
            copies = []
            for blk in range(IN_PAD // 128):
                slot = blk % 2
                if blk >= 2:
                    copies[blk - 2].wait()
                stage[slot] = acc[:, blk * 128:(blk + 1) * 128].T
                cp = pltpu.make_async_copy(stage.at[slot], gw_hbm.at[pl.ds(blk * 128, 128), :], sem.at[1 + slot])
                cp.start()
                copies.append(cp)
            copies[-2].wait()
            copies[-1].wait()

    full = pl.BlockSpec((tm, d), lambda i: (i, 0))
    return pl.pallas_call(
        body, name="in_bwd", grid=(nt,),
        in_specs=[pl.BlockSpec((tm, w), lambda i: (i, 0)) for w in widths]
        + [full, full, pl.BlockSpec(memory_space=pl.ANY), _row(d), _row(d), _row(d)],
        out_specs=[full, pl.BlockSpec(memory_space=pl.ANY), _row(IN_PAD), _row(d), _row(d), _row(d)],
        out_shape=[jax.ShapeDtypeStruct((t, d), F32), jax.ShapeDtypeStruct((IN_PAD, d), F32),
                   jax.ShapeDtypeStruct((1, IN_PAD), F32)] + [jax.ShapeDtypeStruct((1, d), F32)] * 3,
        scratch_shapes=[pltpu.VMEM((IN_PAD, d), BF16), pltpu.VMEM((d, IN_PAD), F32),
                        pltpu.VMEM((2, 128, d), F32), pltpu.SemaphoreType.DMA((3,))],
        compiler_params=_cparams(("arbitrary",), IN_BWD_VMEM_LIMIT),
    )(*pieces, x, dres, w_in_t, norm_w, scale, shift)


def _na_static(rows):
    cases = [(0, 0), (NA_QROWS, NA_QROWS - 4), (rows - NA_QROWS, rows - NA_KROWS)]
    dy = np.zeros((3, NA_QROWS, NA_KROWS), np.int32)
    rv = np.zeros((3, NA_QROWS, NA_KROWS), bool)
    for cs, (r0, kr0) in enumerate(cases):
        for i in range(NA_QROWS):
            for j in range(NA_KROWS):
                r, kr = r0 + i, kr0 + j
                rs = min(max(r - NA_KH // 2, 0), rows - NA_KH)
                rv[cs, i, j] = rs <= kr <= rs + NA_KH - 1
                dy[cs, i, j] = min(max(kr - r + NA_KH - 1, 0), 2 * NA_KH - 2)
    cq = np.arange(GRID_W)[:, None]
    ck = np.arange(GRID_W)[None, :]
    cs0 = np.clip(cq - NA_KW // 2, 0, GRID_W - NA_KW)
    cv = (ck >= cs0) & (ck < cs0 + NA_KW)
    dx = np.clip(ck - cq, -(NA_KW - 1), NA_KW - 1) + NA_KW - 1
    return dy, rv, dx.astype(np.int32), cv


def _na_bias_table(rpb, rows):
    _, _, dx, cv = _na_static(rows)
    ndy = 2 * NA_KH - 1
    onehot = (dx.reshape(1, -1) == np.arange(2 * NA_KW - 1)[:, None]).astype(np.float32)
    rpx = jnp.dot(rpb.reshape(NA_HEADS * ndy, 2 * NA_KW - 1), jnp.asarray(onehot), precision=HI)
    rpx = jnp.where(cv[None, None], rpx.reshape(NA_HEADS, ndy, GRID_W, GRID_W), NEG)
    neg = jnp.full((NA_HEADS, 1, GRID_W, GRID_W), NEG, F32)
    rpx = jnp.concatenate([rpx, neg], axis=1)
    nxt = jnp.concatenate([rpx[:, 1:], neg], axis=1)
    negs = jnp.broadcast_to(neg, rpx.shape)
    pairs = jnp.concatenate([jnp.concatenate([rpx, nxt], axis=3), jnp.concatenate([rpx, negs], axis=3),
                             jnp.concatenate([negs, rpx], axis=3)], axis=1)
    npair = pairs.shape[1]

    def body(m_ref, o_ref):
        cs = pl.program_id(1)
        r0 = jnp.where(cs == 0, 0, jnp.where(cs == 1, NA_QROWS, rows - NA_QROWS))
        kr0 = jnp.where(cs == 0, 0, jnp.where(cs == 1, NA_QROWS - NA_KH // 2, rows - NA_KROWS))
        for i in range(NA_QROWS):
            r = r0 + i
            rs = jnp.clip(r - NA_KH // 2, 0, rows - NA_KH)
            for jp in range(NA_KROWS // 2):
                kl = kr0 + 2 * jp
                vl = (kl >= rs) & (kl <= rs + NA_KH - 1)
                vr = (kl + 1 >= rs) & (kl + 1 <= rs + NA_KH - 1)
                dyl = jnp.clip(kl - r + NA_KH - 1, 0, ndy)
                dyr = jnp.clip(kl + 1 - r + NA_KH - 1, 0, ndy)
                idx = jnp.where(vl & vr, dyl, jnp.where(vl, 16 + dyl, jnp.where(vr, 32 + dyr, 16 + ndy)))
                o_ref[0, 0, i * GRID_W:(i + 1) * GRID_W, jp * 128:(jp + 1) * 128] = m_ref[0, idx]

    return pl.pallas_call(
        body, name="na_bias_table", grid=(NA_HEADS, 3),
        in_specs=[pl.BlockSpec((1, npair, GRID_W, 128), lambda h, cs: (h, 0, 0, 0))],
        out_specs=pl.BlockSpec((1, 1, NA_QT, NA_KT), lambda h, cs: (h, cs, 0, 0)),
        out_shape=jax.ShapeDtypeStruct((NA_HEADS, 3, NA_QT, NA_KT), F32),
        compiler_params=_cparams(("parallel", "parallel")),
    )(pairs)


def _na_specs(t):
    nb = t // NA_QT
    nkb = t // NA_KCH
    npieces = NA_KT // NA_KCH

    def kb0(b):
        return jnp.clip(b * (NA_QT // NA_KCH) - 1, 0, nkb - npieces)

    def case(b):
        return jnp.where(b == 0, 0, jnp.where(b == nb - 1, 2, 1))

    q_spec = pl.BlockSpec((NA_QT, 128), lambda p, b: (b, p))
    k_specs = [pl.BlockSpec((NA_KCH, 128), lambda p, b, i=i: (kb0(b) + i, 4 + p)) for i in range(npieces)]
    v_specs = [pl.BlockSpec((NA_KCH, 128), lambda p, b, i=i: (kb0(b) + i, 8 + p)) for i in range(npieces)]
    tbl_spec = pl.BlockSpec((2, 1, NA_QT, NA_KT), lambda p, b: (p, case(b), 0, 0))
    io_spec = pl.BlockSpec((NA_QT, 128), lambda p, b: (b, p))
    return nb, npieces, kb0, case, q_spec, k_specs, v_specs, tbl_spec, io_spec


NA_HALF = NA_QT // 2
NA_COMBOS_ALL = tuple((i, 0, NA_QT) for i in range(NA_KT // NA_KCH))
NA_COMBOS_INNER = ((0, 0, NA_HALF),) + tuple((i, 0, NA_QT) for i in range(1, NA_KT // NA_KCH - 1)) \
    + ((NA_KT // NA_KCH - 1, NA_HALF, NA_QT),)


def _na_place(val, r0, r1):
    if (r0, r1) == (0, NA_QT):
        return val
    z = jnp.zeros((NA_HALF, val.shape[1]), val.dtype)
    return jnp.concatenate([val, z] if r0 == 0 else [z, val], axis=0)


def _na_fwd(proj, tbl):
    t = proj.shape[0]
    nb, npieces, _, _, q_spec, k_specs, v_specs, tbl_spec, io_spec = _na_specs(t)
    lse_spec = pl.BlockSpec((1, NA_QT, 2), lambda p, b: (p, b, 0))

    def body(*refs):
        q_ref = refs[0]
        k_refs = refs[1:1 + npieces]
        v_refs = refs[1 + npieces:1 + 2 * npieces]
        tbl_ref, o_ref, lse_ref = refs[1 + 2 * npieces:]
        b = pl.program_id(1)

        def compute(combos):
            lane = lax.broadcasted_iota(jnp.int32, (1, 128), 1)
            qv = q_ref[...].astype(F32) * (NA_HEAD_DIM ** -0.5)
            ks = [r[...].astype(BF16) for r in k_refs]
            vs = [r[...].astype(BF16) for r in v_refs]
            hs = range(2)
            msk = [(lane // NA_HEAD_DIM) == hh for hh in hs]
            qh = [jnp.where(msk[hh], qv, 0.0).astype(BF16) for hh in hs]
            s = [[_nt(qh[hh][r0:r1], ks[i]) + tbl_ref[hh, 0, r0:r1, i * NA_KCH:(i + 1) * NA_KCH]
                  for i, r0, r1 in combos] for hh in hs]
            for h0 in (0, NA_HALF):
                rows = slice(h0, h0 + NA_HALF)
                cover = [(c, i, h0 - r0) for c, (i, r0, r1) in enumerate(combos) if r0 <= h0 < r1]
                part = [[s[hh][c][off:off + NA_HALF] for c, _, off in cover] for hh in hs]
                m = [functools.reduce(jnp.maximum, [jnp.max(v, axis=1, keepdims=True) for v in part[hh]]) for hh in hs]
                p = [[jnp.exp(v - m[hh]) for v in part[hh]] for hh in hs]
                l = [functools.reduce(jnp.add, [jnp.sum(v, axis=1, keepdims=True) for v in p[hh]]) for hh in hs]
                o = [functools.reduce(jnp.add, [_nn(p[hh][k].astype(BF16), vs[i]) for k, (_, i, _) in enumerate(cover)])
                     for hh in hs]
                for hh in hs:
                    lse_ref[0, rows, hh:hh + 1] = m[hh] + jnp.log(l[hh])
                o_ref[rows, :] = jnp.where(msk[0], o[0] / l[0], o[1] / l[1])

        inner = (b > 0) & (b < nb - 1)
        pl.when(inner)(lambda: compute(NA_COMBOS_INNER))
        pl.when(jnp.logical_not(inner))(lambda: compute(NA_COMBOS_ALL))

    return pl.pallas_call(
        body, name="na_fwd", grid=(4, nb),
        in_specs=[q_spec] + k_specs + v_specs + [tbl_spec],
        out_specs=[io_spec, lse_spec],
        out_shape=[jax.ShapeDtypeStruct((t, NA_W), F32), jax.ShapeDtypeStruct((4, t, 2), F32)],
        compiler_params=_cparams(("parallel", "arbitrary")),
    )(*([proj] * (1 + 2 * npieces)), tbl)


def _na_bwd(proj, tbl, d_o, o_na, lse):
    t = proj.shape[0]
    nb, npieces, kb0, case, q_spec, k_specs, v_specs, tbl_spec, io_spec = _na_specs(t)

    def body(*refs):
        q_ref = refs[0]
        k_refs = refs[1:1 + npieces]
        v_refs = refs[1 + npieces:1 + 2 * npieces]
        (tbl_ref, do_ref, o_ref, lse_ref, dq_ref, dk_hbm, dv_hbm, rpb_ref,
         dk_acc, dv_acc, dk_out, dv_out, s_scr, dp_scr, dsb_scr, pnb_scr, sem) = refs[1 + 2 * npieces:]
        p_id = pl.program_id(0)
        b = pl.program_id(1)

        @pl.when(b == 0)
        def _():
            dk_acc[...] = jnp.zeros_like(dk_acc)
            dv_acc[...] = jnp.zeros_like(dv_acc)

        @pl.when((b == 0) | (b == 1) | (b == nb - 1))
        def _():
            rpb_ref[...] = jnp.zeros_like(rpb_ref)

        def compute(combos):
            lane = lax.broadcasted_iota(jnp.int32, (1, 128), 1)
            scale = NA_HEAD_DIM ** -0.5
            qv = q_ref[...].astype(F32) * scale
            ks = [r[...].astype(BF16) for r in k_refs]
            vs = [r[...].astype(BF16) for r in v_refs]
            dov = do_ref[...]
            ov = o_ref[...]
            tok0 = kb0(b) * NA_KCH
            hs = range(2)
            msk = [(lane // NA_HEAD_DIM) == hh for hh in hs]
            qh = [jnp.where(msk[hh], qv, 0.0).astype(BF16) for hh in hs]
            doh = [jnp.where(msk[hh], dov, 0.0) for hh in hs]
            dohb = [doh[hh].astype(BF16) for hh in hs]
            dd = [jnp.sum(doh[hh] * ov, axis=1, keepdims=True) for hh in hs]
            for hh in hs:
                for c, (i, q0, q1) in enumerate(combos):
                    slot = (hh * len(combos) + c) % 2
                    cols = slice(i * NA_KCH, (i + 1) * NA_KCH)
                    s_scr[slot, 0:q1 - q0] = _nt(qh[hh][q0:q1], ks[i])
                    dp_scr[slot, 0:q1 - q0] = _nt(dohb[hh][q0:q1], vs[i])
                    for r0 in range(q0, q1, NA_RC):
                        rows = slice(r0, r0 + NA_RC)
                        loc = slice(r0 - q0, r0 - q0 + NA_RC)
                        p = jnp.exp(s_scr[slot, loc, :] + tbl_ref[hh, 0, rows, cols] - lse_ref[0, rows, hh:hh + 1])
                        d = p * (dp_scr[slot, loc, :] - dd[hh][rows])
                        pnb_scr[hh, rows, cols] = p.astype(BF16)
                        dsb_scr[hh, rows, cols] = d.astype(BF16)
                done = {(i, q0) for i, q0, _ in combos} | {(i, NA_HALF) for i, q0, q1 in combos if q1 - q0 == NA_QT}
                for i in range(npieces):
                    for q0 in (0, NA_HALF):
                        if (i, q0) not in done:
                            dsb_scr[hh, q0:q0 + NA_HALF, i * NA_KCH:(i + 1) * NA_KCH] = jnp.zeros(
                                (NA_HALF, NA_KCH), BF16)
            dqh = [functools.reduce(jnp.add, [_na_place(_nn(dsb_scr[hh, q0:q1, i * NA_KCH:(i + 1) * NA_KCH], ks[i]),
                                                        q0, q1) for i, q0, q1 in combos]) for hh in hs]
            dq_ref[...] = (jnp.where(msk[0], dqh[0], dqh[1]) * scale).astype(BF16)
            for i in range(npieces):
                rows = pl.ds(pl.multiple_of(tok0 + i * NA_KCH, NA_KCH), NA_KCH)
                cols = slice(i * NA_KCH, (i + 1) * NA_KCH)
                q0, q1 = [(a, e) for j, a, e in combos if j == i][0]
                dk_acc[rows, :] += (_tn(dsb_scr[0, q0:q1, cols], qh[0][q0:q1])
                                    + _tn(dsb_scr[1, q0:q1, cols], qh[1][q0:q1]))
                dv_acc[rows, :] += (_tn(pnb_scr[0, q0:q1, cols], dohb[0][q0:q1])
                                    + _tn(pnb_scr[1, q0:q1, cols], dohb[1][q0:q1]))
            for hh in hs:
                acc = dsb_scr[hh, 0:GRID_W, :].astype(F32)
                for i in range(1, NA_QROWS):
                    acc = acc + pltpu.roll(dsb_scr[hh, i * GRID_W:(i + 1) * GRID_W, :].astype(F32),
                                           NA_KT - i * GRID_W, 1)
                rpb_ref[0, 0, hh] += acc

        inner = (b > 0) & (b < nb - 1)
        pl.when(inner)(lambda: compute(NA_COMBOS_INNER))
        pl.when(jnp.logical_not(inner))(lambda: compute(NA_COMBOS_ALL))

        @pl.when(b == nb - 1)
        def _():
            def copies(pair):
                cols = pl.ds(pl.multiple_of(pair * 128, 128), 128)
                return (pltpu.make_async_copy(dk_out, dk_hbm.at[:, cols], sem.at[0]),
                        pltpu.make_async_copy(dv_out, dv_hbm.at[:, cols], sem.at[1]))

            @pl.when(p_id > 0)
            def _():
                for cp in copies(p_id - 1):
                    cp.wait()
            dk_out[...] = dk_acc[...].astype(BF16)
            dv_out[...] = dv_acc[...].astype(BF16)
            for cp in copies(p_id):
                cp.start()

            @pl.when(p_id == NA_HEADS // 2 - 1)
            def _():
                for cp in copies(p_id):
                    cp.wait()

    o512 = jax.ShapeDtypeStruct((t, NA_W), BF16)
    return pl.pallas_call(
        body, name="na_bwd", grid=(4, nb),
        in_specs=[q_spec] + k_specs + v_specs + [tbl_spec, io_spec, io_spec,
                                                 pl.BlockSpec((1, NA_QT, 2), lambda p, b: (p, b, 0))],
        out_specs=[io_spec, pl.BlockSpec(memory_space=pl.ANY), pl.BlockSpec(memory_space=pl.ANY),
                   pl.BlockSpec((1, 1, 2, GRID_W, NA_KT), lambda p, b: (p, case(b), 0, 0, 0))],
        out_shape=[o512, o512, o512, jax.ShapeDtypeStruct((4, 3, 2, GRID_W, NA_KT), F32)],
        scratch_shapes=[pltpu.VMEM((t, 128), F32), pltpu.VMEM((t, 128), F32),
                        pltpu.VMEM((t, 128), BF16), pltpu.VMEM((t, 128), BF16),
                        pltpu.VMEM((2, NA_QT, NA_KCH), F32), pltpu.VMEM((2, NA_QT, NA_KCH), F32),
                        pltpu.VMEM((2, NA_QT, NA_KT), BF16), pltpu.VMEM((2, NA_QT, NA_KT), BF16),
                        pltpu.SemaphoreType.DMA((2,))],
        compiler_params=_cparams(("arbitrary", "arbitrary")),
    )(*([proj] * (1 + 2 * npieces)), tbl, d_o, o_na, lse)


def _rpb_reduce(rpbacc, rows):
    nacc = 4 * 3 * 2

    def shift_body(a_ref, o_ref):
        acc = a_ref[0, 0:1, :]
        for cq in range(1, GRID_W):
            acc = acc + pltpu.roll(a_ref[0, cq:cq + 1, :], NA_KT - cq, 1)
        o_ref[0] = jnp.broadcast_to(acc, (8, NA_KT))

    vec = pl.pallas_call(
        shift_body, name="rpb_shift", grid=(nacc,),
        in_specs=[pl.BlockSpec((1, GRID_W, NA_KT), lambda a: (a, 0, 0))],
        out_specs=pl.BlockSpec((1, 8, NA_KT), lambda a: (a, 0, 0)),
        out_shape=jax.ShapeDtypeStruct((nacc, 8, NA_KT), F32),
        compiler_params=_cparams(("parallel",)),
    )(rpbacc.reshape(nacc, GRID_W, NA_KT))
    a = vec[:, 0].reshape(4, 3, 2, NA_KT).transpose(0, 2, 1, 3).reshape(NA_HEADS, 3, NA_KT)
    if rows // NA_QROWS < 3:
        a = a.at[:, 1].set(0.0)
    dd = np.arange(NA_KROWS)[:, None]
    dxo = np.arange(-(NA_KW - 1), NA_KW)[None, :]
    idx = ((dd * GRID_W + dxo) % NA_KT).reshape(-1)
    g = a[..., idx].reshape(NA_HEADS, 3 * NA_KROWS, 2 * NA_KW - 1)
    g = jnp.pad(g, ((0, 0), (0, 0), (0, 128 - (2 * NA_KW - 1))))
    nmat = np.zeros((16, 3 * NA_KROWS), np.float32)
    for cs, delta in enumerate((0, -(NA_KH // 2), -(NA_KROWS - NA_QROWS))):
        for d in range(NA_KROWS):
            jmi = d - NA_KROWS if (cs == 0 and d > NA_KH - 1) else d
            dy = jmi + delta + NA_KH - 1
            if 0 <= dy <= 2 * NA_KH - 2:
                nmat[dy, cs * NA_KROWS + d] = 1.0

    def body(n_ref, g_ref, o_ref):
        o_ref[0] = jnp.dot(n_ref[...], g_ref[0], precision=HI, preferred_element_type=F32)

    out = pl.pallas_call(
        body, name="rpb_reduce", grid=(NA_HEADS,),
        in_specs=[pl.BlockSpec((16, nmat.shape[1]), lambda h: (0, 0)),
                  pl.BlockSpec((1, nmat.shape[1], 128), lambda h: (h, 0, 0))],
        out_specs=pl.BlockSpec((1, 16, 128), lambda h: (h, 0, 0)),
        out_shape=jax.ShapeDtypeStruct((NA_HEADS, 16, 128), F32),
        compiler_params=_cparams(("parallel",)),
    )(jnp.asarray(nmat), g)
    return out[:, :2 * NA_KH - 1, :2 * NA_KW - 1]


def _halo_specs(tm, t, col, width=1024):
    nth = t // CONV_HALO
    per = tm // CONV_HALO
    return [pl.BlockSpec((tm, width), lambda i: (i, col)),
            pl.BlockSpec((CONV_HALO, width), lambda i: (jnp.maximum(i * per - 1, 0), col)),
            pl.BlockSpec((CONV_HALO, width), lambda i: (jnp.minimum((i + 1) * per, nth - 1), col))]


def _fill_ext(ext, cur_ref, prev_ref, next_ref, tm, nt):
    i = pl.program_id(0)
    hl = CONV_HALO
    ext[0:hl, :] = jnp.where(i == 0, 0.0, prev_ref[...].astype(F32))
    ext[hl:hl + tm, :] = cur_ref[...].astype(F32)
    ext[hl + tm:2 * hl + tm, :] = jnp.where(i == nt - 1, 0.0, next_ref[...].astype(F32))


CONV_HALO = 16
CONV_RC = 16
CONV_CB = 512


def _conv_chunks(tm):
    return [(slice(cb, cb + CONV_CB), slice(rb, rb + CONV_RC))
            for cb in range(0, 1024, CONV_CB) for rb in range(0, tm, CONV_RC)]


def _conv_masks():
    rows = lax.broadcasted_iota(jnp.int32, (8, CONV_CB), 0)
    return {off: (rows >= off if off > 0 else rows < 8 + off)
            for off in range(-(CONV_W // 2), CONV_W // 2 + 1) if off != 0}


def _conv_windows(ext, row0, cs, masks):
    ng = CONV_RC // 8
    grp = [ext[pl.ds(row0 + 8 * (g - 1), 8), cs] for g in range(ng + 2)]
    wins = []
    for j in range(CONV_W):
        off = j - CONV_W // 2
        if off == 0:
            parts = grp[1:ng + 1]
        elif off > 0:
            parts = [pltpu.roll(jnp.where(masks[off], grp[g + 1], grp[g + 2]), 8 - off, axis=0) for g in range(ng)]
        else:
            parts = [pltpu.roll(jnp.where(masks[off], grp[g + 1], grp[g]), -off, axis=0) for g in range(ng)]
        wins.append(jnp.concatenate(parts, axis=0))
    return wins


def _conv_fwd(proj, conv_w8, conv_b, tm):
    t = proj.shape[0]
    nt = t // tm

    def body(u_ref, up_ref, un_ref, w_ref, b_ref, pre_ref, act_ref, ext):
        _fill_ext(ext, u_ref, up_ref, un_ref, tm, nt)
        masks = _conv_masks()
        for cs, rs in _conv_chunks(tm):
            wins = _conv_windows(ext, rs.start + CONV_HALO, cs, masks)
            pre = b_ref[:, cs] + w_ref[0:1, cs] * wins[0]
            for j in range(1, CONV_W):
                pre = pre + w_ref[j:j + 1, cs] * wins[j]
            pre_ref[rs, cs] = pre
            act_ref[rs, cs] = _silu(pre)

    full = pl.BlockSpec((tm, 1024), lambda i: (i, 0))
    o = jax.ShapeDtypeStruct((t, 1024), F32)
    return pl.pallas_call(
        body, name="conv_fwd", grid=(nt,),
        in_specs=_halo_specs(tm, t, 2) + [pl.BlockSpec((8, 1024), lambda i: (0, 0)), _row(1024)],
        out_specs=[full, full], out_shape=[o, o],
        scratch_shapes=[pltpu.VMEM((tm + 2 * CONV_HALO, 1024), F32)],
        compiler_params=_cparams(("parallel",)),
    )(proj, proj, proj, conv_w8, conv_b)


def _conv_bwd(dq, dk, pre, proj, conv_w8, tm):
    t = pre.shape[0]
    nt = t // tm

    def body(dq_ref, dqp_ref, dqn_ref, dk_ref, dkp_ref, dkn_ref, pre_ref, prep_ref, pren_ref,
             u_ref, w_ref, du_ref, gw_ref, gb_ref, extd):
        i = pl.program_id(0)
        hl = CONV_HALO

        @pl.when(i == 0)
        def _():
            gw_ref[...] = jnp.zeros_like(gw_ref)
            gb_ref[...] = jnp.zeros_like(gb_ref)
        for rows, dqr, dkr, prr, edge in ((slice(0, hl), dqp_ref, dkp_ref, prep_ref, i == 0),
                                          (slice(hl, hl + tm), dq_ref, dk_ref, pre_ref, None),
                                          (slice(hl + tm, 2 * hl + tm), dqn_ref, dkn_ref, pren_ref, i == nt - 1)):
            ds = _dsilu(prr[...])
            dl = dqr[...] * ds[:, 0:ML_W]
            dr = dkr[...] * ds[:, ML_W:]
            if edge is not None:
                dl = jnp.where(edge, 0.0, dl)
                dr = jnp.where(edge, 0.0, dr)
            extd[rows, 0:ML_W] = dl
            extd[rows, ML_W:] = dr
        gb_ref[...] += jnp.sum(extd[hl:hl + tm, :], axis=0, keepdims=True)
        gacc = None
        masks = _conv_masks()
        for cs, rs in _conv_chunks(tm):
            if rs.start == 0:
                gacc = [jnp.zeros((8, CONV_CB), F32) for _ in range(CONV_W)]
            wd = _conv_windows(extd, rs.start + hl, cs, masks)
            du = w_ref[0:1, cs] * wd[CONV_W - 1]
            for j in range(1, CONV_W):
                du = du + w_ref[j:j + 1, cs] * wd[CONV_W - 1 - j]
            du_ref[rs, cs] = du.astype(BF16)
            ucur = u_ref[rs, cs].astype(F32)
            for j in range(CONV_W):
                prod = ucur * wd[CONV_W - 1 - j]
                gacc[j] = gacc[j] + functools.reduce(
                    jnp.add, [prod[k:k + 8] for k in range(0, CONV_RC, 8)])
            if rs.stop == tm:
                for j in range(CONV_W):
                    gw_ref[j:j + 1, cs] += jnp.sum(gacc[j], axis=0, keepdims=True)

    full = pl.BlockSpec((tm, 1024), lambda i: (i, 0))
    return pl.pallas_call(
        body, name="conv_bwd", grid=(nt,),
        in_specs=_halo_specs(tm, t, 0, ML_W) + _halo_specs(tm, t, 0, ML_W) + _halo_specs(tm, t, 0)
        + [pl.BlockSpec((tm, 1024), lambda i: (i, 2)), pl.BlockSpec((8, 1024), lambda i: (0, 0))],
        out_specs=[full, pl.BlockSpec((8, 1024), lambda i: (0, 0)), _row(1024)],
        out_shape=[jax.ShapeDtypeStruct((t, 1024), BF16), jax.ShapeDtypeStruct((8, 1024), F32),
                   jax.ShapeDtypeStruct((1, 1024), F32)],
        scratch_shapes=[pltpu.VMEM((tm + 2 * CONV_HALO, 1024), F32)],
        compiler_params=_cparams(("arbitrary",)),
    )(dq, dq, dq, dk, dk, dk, pre, pre, pre, proj, conv_w8)


def _ml_consts(rev):
    iu = lax.broadcasted_iota(jnp.int32, (ML_CHUNK, ML_CHUNK), 0)
    js = lax.broadcasted_iota(jnp.int32, (ML_CHUNK, ML_CHUNK), 1)
    eye = iu == js
    le = iu <= js
    ge = iu >= js
    csum, csum_t, sees = (ge, le, ge) if rev else (le, ge, le)
    return eye, csum.astype(F32), csum_t.astype(F32), sees


def _col(row, eye):
    return jnp.sum(jnp.where(eye, row, 0.0), axis=1, keepdims=True)


def _rowof(col, eye):
    return jnp.sum(jnp.where(eye, col, 0.0), axis=0, keepdims=True)


def _row8(row):
    top = lax.broadcasted_iota(jnp.int32, (8, row.shape[1]), 0) == 0
    return jnp.where(top, row, jnp.zeros_like(row))


def _outer_rows(a_row, b_row_bf16):
    hi = a_row.astype(BF16)
    lo = (a_row - hi.astype(F32)).astype(BF16)
    r_a = lax.broadcasted_iota(jnp.int32, (8, a_row.shape[1]), 0)
    r_b = lax.broadcasted_iota(jnp.int32, (8, b_row_bf16.shape[1]), 0)
    lhs = jnp.where(r_a == 0, hi, jnp.where(r_a == 1, lo, jnp.zeros_like(hi)))
    rhs = jnp.where(r_b < 2, b_row_bf16, jnp.zeros_like(b_row_bf16))
    return _tn(lhs, rhs)


def _ml_gates(gi, gf, m0, csum, rev):
    lf = jax.nn.log_sigmoid(gf)
    b_rows = jnp.dot(lf, csum, precision=HI, preferred_element_type=F32)
    bl = jnp.sum(lf, axis=1, keepdims=True)
    a_rows = bl - b_rows + gi
    mloc = jnp.max(a_rows, axis=1, keepdims=True)
    order = list(range(ML_NB))[::-1] if rev else list(range(ML_NB))
    mp, mn, decay = {}, {}, {}
    m = m0
    for n in order:
        mp[n] = m
        m = jnp.maximum(bl[n:n + 1] + m, mloc[n:n + 1])
        mn[n] = m
    for n in order:
        decay[n] = jnp.exp(bl[n:n + 1] + mp[n] - mn[n])
    return b_rows, a_rows, gi - b_rows, mp, mn, decay, order


def _ml_load(q_ref, k_ref, v_ref, n):
    sl = slice(n * ML_CHUNK, (n + 1) * ML_CHUNK)
    qb = q_ref[sl, :].astype(BF16)
    kb = (k_ref[sl, :] * (ML_HEAD_DIM ** -0.5)).astype(BF16)
    vn = v_ref[sl, :].astype(F32)
    return sl, qb, kb, vn


def _ml_state_scan(q_ref, k_ref, v_ref, a_rows, mn, decay, order, c0, n0):
    ns = range(ML_NB)
    ld = [_ml_load(q_ref, k_ref, v_ref, n) for n in ns]
    vt = [ld[n][3].T for n in ns]
    w_row = [jnp.exp(a_rows[n:n + 1] - mn[n]) for n in ns]
    u = [_nn((vt[n] * w_row[n]).astype(BF16), ld[n][2]) for n in ns]
    nu = [_nn(_row8(w_row[n]).astype(BF16), ld[n][2])[0:1] for n in ns]
    cp, npv = {}, {}
    c, nv = c0, n0
    for n in order:
        cp[n], npv[n] = c, nv
        c = decay[n] * c + u[n]
        nv = decay[n] * nv + nu[n]
    return ld, vt, cp, npv, w_row, c, nv


def _ml_intra_all(ld, vt, b_rows, imb_rows, mp, cp, npv, sees, eye):
    ns = range(ML_NB)
    qk = [_nt(ld[n][2], ld[n][1]) for n in ns]
    cq = [_nt(cp[n].astype(BF16), ld[n][1]) for n in ns]
    qn = [_nt(_row8(npv[n]).astype(BF16), ld[n][1])[0:1] for n in ns]
    imb_col = [_col(imb_rows[n:n + 1], eye) for n in ns]
    dlog = [jnp.where(sees, b_rows[n:n + 1] + imb_col[n], NEG) for n in ns]
    m_inter = [b_rows[n:n + 1] + mp[n] for n in ns]
    m_t = [jnp.maximum(m_inter[n], jnp.max(dlog[n], axis=0, keepdims=True)) for n in ns]
    pm = [jnp.exp(dlog[n] - m_t[n]) for n in ns]
    inter = [jnp.exp(m_inter[n] - m_t[n]) for n in ns]
    floor = [jnp.exp(-m_t[n]) for n in ns]
    s = [qk[n] * pm[n] for n in ns]
    sv = [_nn(vt[n].astype(BF16), s[n].astype(BF16)) for n in ns]
    den = [jnp.sum(s[n], axis=0, keepdims=True) + inter[n] * qn[n] for n in ns]
    num = [sv[n] + inter[n] * cq[n] for n in ns]
    dn = [jnp.maximum(jnp.abs(den[n]), floor[n]) for n in ns]
    return [dict(pm=pm[n], s=s[n], inter=inter[n], cq=cq[n], qn=qn[n], num=num[n], den=den[n],
                 floor=floor[n], dn=dn[n]) for n in ns]


def _ml_specs(t, rev):
    nblk = t // ML_TB
    blk = (lambda g: nblk - 1 - g) if rev else (lambda g: g)
    hps = ML_HPS
    tile = lambda c0: pl.BlockSpec((ML_TB, 128 * hps), lambda hg, g, c0=c0: (blk(g), c0 // hps + hg))
    gate = pl.BlockSpec((hps, ML_NB, ML_CHUNK), lambda hg, g: (hg, blk(g), 0))
    cchk = pl.BlockSpec((hps, 1, 128, 128), lambda hg, g: (hg, blk(g), 0, 0))
    nmchk = pl.BlockSpec((hps, 1, 8, 128), lambda hg, g: (hg, blk(g), 0, 0))
    return nblk, blk, tile, gate, cchk, nmchk


def _ml_head_views(refs, hh):
    cols = slice(hh * ML_HEAD_DIM, (hh + 1) * ML_HEAD_DIM)
    return [r.at[:, cols] if len(r.shape) == 2 else r.at[hh] for r in refs]


def _ml_fwd(qk_act, proj, gi, gf, rev, name):
    t = qk_act.shape[0]
    nblk, _, tile, gate, cchk, nmchk = _ml_specs(t, rev)

    def body(*refs):
        for hh in range(ML_HPS):
            one_head(*_ml_head_views(refs, hh))

    def one_head(q_ref, k_ref, v_ref, gi_ref, gf_ref, h_ref, cchk_ref, nmchk_ref, c_ref, nm_ref):
        @pl.when(pl.program_id(1) == 0)
        def _():
            c_ref[...] = jnp.zeros_like(c_ref)
            nm_ref[...] = jnp.zeros_like(nm_ref)
        cchk_ref[0] = c_ref[...]
        nmchk_ref[0] = nm_ref[...]
        eye, csum, _, sees = _ml_consts(rev)
        b_rows, a_rows, imb_rows, mp, mn, decay, order = _ml_gates(
            gi_ref[...], gf_ref[...], nm_ref[1:2, 0:1], csum, rev)
        ld, vt, cp, npv, _, c, nv = _ml_state_scan(q_ref, k_ref, v_ref, a_rows, mn, decay, order,
                                                   c_ref[...], nm_ref[0:1, :])
        c_ref[...] = c
        nm_ref[0:1, :] = nv
        nm_ref[1:2, :] = jnp.broadcast_to(mn[order[-1]], (1, 128))
        rs = _ml_intra_all(ld, vt, b_rows, imb_rows, mp, cp, npv, sees, eye)
        ht = [rs[n]['num'] / rs[n]['dn'] for n in range(ML_NB)]
        for n in range(ML_NB):
            h_ref[n * ML_CHUNK:(n + 1) * ML_CHUNK, :] = ht[n].T

    return pl.pallas_call(
        body, name=name, grid=(ML_HEADS // ML_HPS, nblk),
        in_specs=[tile(0), tile(4), tile(24), gate, gate],
        out_specs=[tile(0), cchk, nmchk],
        out_shape=[jax.ShapeDtypeStruct((t, ML_W), F32),
                   jax.ShapeDtypeStruct((ML_HEADS, nblk, 128, 128), F32),
                   jax.ShapeDtypeStruct((ML_HEADS, nblk, 8, 128), F32)],
        scratch_shapes=[pltpu.VMEM((ML_HPS, 128, 128), F32), pltpu.VMEM((ML_HPS, 8, 128), F32)],
        compiler_params=_cparams(("parallel", "arbitrary")),
    )(qk_act, qk_act, proj, gi, gf)


def _ml_bwd(qk_act, proj, gi, gf, dh, cchk_a, nmchk_a, prev, rev, name):
    t = qk_act.shape[0]
    nblk, _, tile, gate, cchk, nmchk = _ml_specs(t, not rev)

    def body(*refs):
        for hh in range(ML_HPS):
            one_head(*_ml_head_views(refs, hh))

    def one_head(q_ref, k_ref, v_ref, gi_ref, gf_ref, dh_ref, cchk_ref, nmchk_ref, *rest):
        prev_refs = rest[:len(prev)]
        dq_ref, dk_ref, dv_ref, dgi_ref, dgf_ref, dc_ref, dn_ref, db_scr, dbl_scr, di_scr = rest[len(prev):]

        def plus_prev(val, which, rows):
            return val + prev_refs[which][rows, :] if prev else val

        @pl.when(pl.program_id(1) == 0)
        def _():
            dc_ref[...] = jnp.zeros_like(dc_ref)
            dn_ref[...] = jnp.zeros_like(dn_ref)
        eye, csum, csum_t, sees = _ml_consts(rev)
        gfv = gf_ref[...]
        b_rows, a_rows, imb_rows, mp, mn, decay, order = _ml_gates(
            gi_ref[...], gfv, nmchk_ref[0, 1:2, 0:1], csum, rev)
        ld, vt, cp, npv, w_row, _, _ = _ml_state_scan(q_ref, k_ref, v_ref, a_rows, mn, decay, order,
                                                      cchk_ref[0], nmchk_ref[0, 0:1, :])
        ns = range(ML_NB)
        rs = _ml_intra_all(ld, vt, b_rows, imb_rows, mp, cp, npv, sees, eye)
        sls = [ld[n][0] for n in ns]
        qbs = [ld[n][1] for n in ns]
        kbs = [ld[n][2] for n in ns]
        vbs = [ld[n][3].astype(BF16) for n in ns]
        rdn = [1.0 / rs[n]['dn'] for n in ns]
        dnum = [dh_ref[sls[n], :].T * rdn[n] for n in ns]
        hsum = [jnp.sum(dnum[n] * rs[n]['num'], axis=0, keepdims=True) for n in ns]
        dden = [jnp.where(jnp.abs(rs[n]['den']) > rs[n]['floor'],
                          -hsum[n] * rdn[n] * jnp.sign(rs[n]['den']), 0.0) for n in ns]
        dnb = [dnum[n].astype(BF16) for n in ns]
        dsf = [_nn(vbs[n], dnb[n]) + dden[n] for n in ns]
        dv0 = [_nt(rs[n]['s'].astype(BF16), dnb[n]) for n in ns]
        gb = [(dsf[n] * rs[n]['pm']).astype(BF16) for n in ns]
        cpb = [cp[n].astype(BF16) for n in ns]
        idd = [rs[n]['inter'] * dden[n] for n in ns]
        idn = [(rs[n]['inter'] * dnum[n]).astype(BF16) for n in ns]
        dqa = [_tn(gb[n], kbs[n]) for n in ns]
        dqc = [_tn(idn[n], cpb[n]) for n in ns]
        dqn = [_outer_rows(idd[n], npv[n].astype(BF16)) for n in ns]
        dk0 = [_nn(gb[n], qbs[n]) for n in ns]
        xs = [_nn(idn[n], qbs[n]) for n in ns]
        for n in ns:
            dq_ref[sls[n], :] = plus_prev(dqa[n] + dqc[n] + dqn[n], 0, sls[n])
        rr = [dsf[n] * rs[n]['s'] for n in ns]
        dinter = [jnp.sum(dnum[n] * rs[n]['cq'], axis=0, keepdims=True) + dden[n] * rs[n]['qn'] for n in ns]
        dbt = [jnp.sum(rr[n], axis=0, keepdims=True) + dinter[n] * rs[n]['inter'] for n in ns]
        dimb = [jnp.sum(rr[n], axis=1, keepdims=True) for n in ns]
        xns = [_nn(_row8(idd[n]).astype(BF16), qbs[n])[0:1] for n in ns]
        dcn, dnn = {}, {}
        dc, dn = dc_ref[...], dn_ref[0:1, :]
        for n in order[::-1]:
            dcn[n], dnn[n] = dc, dn
            dc = decay[n] * dc + xs[n]
            dn = decay[n] * dn + xns[n]
        dc_ref[...] = dc
        dn_ref[0:1, :] = dn
        kscale = ML_HEAD_DIM ** -0.5
        dcb = [dcn[n].astype(BF16) for n in ns]
        z = [_nn(vbs[n], dcb[n]) for n in ns]
        kd = [_nt(kbs[n], dcb[n]) for n in ns]
        ddecay = [jnp.sum(jnp.sum(dcn[n] * cp[n], axis=1, keepdims=True), axis=0, keepdims=True)
                  + jnp.sum(dnn[n] * npv[n], axis=1, keepdims=True) for n in ns]
        zd = [z[n] + dnn[n] for n in ns]
        dw = [jnp.sum(zd[n] * kbs[n].astype(F32), axis=1, keepdims=True) for n in ns]
        wcol = [_col(w_row[n], eye) for n in ns]
        for n in ns:
            dv_ref[sls[n], :] = plus_prev(dv0[n] + wcol[n] * kd[n], 2, sls[n]).astype(dv_ref.dtype)
            dk_ref[sls[n], :] = plus_prev((dk0[n] + wcol[n] * zd[n]) * kscale, 1, sls[n])
        da = [dw[n] * wcol[n] for n in ns]
        dbl = [jnp.sum(da[n], axis=0, keepdims=True) + ddecay[n] * decay[n] for n in ns]
        key_row = [_rowof(dimb[n] + da[n], eye) for n in ns]
        for n in ns:
            db_scr[n:n + 1, :] = dbt[n] - key_row[n]
            di_scr[n:n + 1, :] = key_row[n]
            dbl_scr[n:n + 1, :] = jnp.broadcast_to(dbl[n], (1, ML_CHUNK))
        dlf = jnp.dot(db_scr[...], csum_t, precision=HI, preferred_element_type=F32) + dbl_scr[...]
        dgf_ref[...] = dlf * jax.nn.sigmoid(-gfv)
        dgi_ref[...] = di_scr[...]

    nc = t // ML_CHUNK
    o512 = jax.ShapeDtypeStruct((t, ML_W), F32)
    og = jax.ShapeDtypeStruct((ML_HEADS, nc, ML_CHUNK), F32)
    return pl.pallas_call(
        body, name=name, grid=(ML_HEADS // ML_HPS, nblk),
        in_specs=[tile(0), tile(4), tile(24), gate, gate, tile(0), cchk, nmchk] + [tile(0)] * len(prev),
        out_specs=[tile(0), tile(0), tile(0), gate, gate],
        out_shape=[o512, o512, jax.ShapeDtypeStruct((t, ML_W), BF16 if prev else F32), og, og],
        scratch_shapes=[pltpu.VMEM((ML_HPS, 128, 128), F32), pltpu.VMEM((ML_HPS, 8, 128), F32)]
        + [pltpu.VMEM((ML_HPS, ML_NB, ML_CHUNK), F32)] * 3,
        compiler_params=_cparams(("parallel", "arbitrary")),
    )(qk_act, qk_act, proj, gi, gf, dh, cchk_a, nmchk_a, *prev)


def _gate_rows(gates16, t):
    g = gates16.reshape(t // ML_CHUNK, ML_CHUNK, 4, ML_HEADS).transpose(2, 3, 0, 1)
    return g[0], g[1], g[2], g[3]


def _gate_cols(dgi_f, dgf_f, dgi_b, dgf_b, t):
    g = jnp.stack([dgi_f, dgf_f, dgi_b, dgf_b]).transpose(2, 3, 0, 1).reshape(t, 4 * ML_HEADS)
    return jnp.pad(g, ((0, 0), (0, 128 - 4 * ML_HEADS)))


def _local_step(x, target, shift, scale, gate, norm_w, w_in_t, b_in_p, conv_w8, conv_b, rpb,
                ml_norm_w, w_out_b, final_norm_w):
    t = x.shape[0]
    rows = t // GRID_W
    tm = 512
    proj, gates = _in_proj(x, norm_w, scale, shift, w_in_t, b_in_p)
    tbl = _na_bias_table(rpb, rows)
    o_na, lse_na = _na_fwd(proj, tbl)
    pre, qk_act = _conv_fwd(proj, conv_w8, conv_b, 2 * tm)
    gi_f, gf_f, gi_b, gf_b = _gate_rows(gates[:, :4 * ML_HEADS], t)
    h_f, cchk_f, nmchk_f = _ml_fwd(qk_act, proj, gi_f, gf_f, False, "ml_fwd_f")
    h_b, cchk_b, nmchk_b = _ml_fwd(qk_act, proj, gi_b, gf_b, True, "ml_fwd_b")
    (loss, dres, d_ona, d_naz, dhs, d_o, d_z, dgate, g_fnw, g_mlnw, g_w_out) = _tail(
        o_na, proj, h_f, h_b, x, target, gate, ml_norm_w, final_norm_w, w_out_b)
    dq_na, dk_na, dv_na, rpbacc = _na_bwd(proj, tbl, d_ona, o_na, lse_na)
    g_rpb = _rpb_reduce(rpbacc, rows)
    dq_f, dk_f, dv_f, dgi_f, dgf_f = _ml_bwd(qk_act, proj, gi_f, gf_f, dhs, cchk_f, nmchk_f, (),
                                             False, "ml_bwd_f")
    dq_ml, dk_ml, dv_ml, dgi_b, dgf_b = _ml_bwd(qk_act, proj, gi_b, gf_b, dhs, cchk_b, nmchk_b, (dq_f, dk_f, dv_f),
                                                True, "ml_bwd_b")
    du, g_conv_w, g_conv_b = _conv_bwd(dq_ml, dk_ml, pre, proj, conv_w8, tm)
    dgates = _gate_cols(dgi_f, dgf_f, dgi_b, dgf_b, t)
    grad_x, g_w_in, g_b_in, dscale, dshift, g_nw = _in_bwd(
        [dq_na, dk_na, dv_na, d_naz, du, dv_ml, d_o, d_z, dgates], x, dres, w_in_t, norm_w, scale, shift)
    dmod = jnp.concatenate([dshift, dscale, dgate], axis=1)
    return (loss, grad_x, dmod, g_nw, g_w_in, g_b_in, g_conv_w, g_conv_b, g_rpb, g_mlnw, g_w_out, g_fnw)


MESH = pl.DeviceIdType.MESH
N_DEV = 8
ANY = pl.BlockSpec(memory_space=pl.ANY)
WHOLE_VMEM = pl.BlockSpec(memory_space=pltpu.VMEM)


def _allgather8(blocks, name):
    na = len(blocks)

    def body(*refs):
        x_refs = refs[:na]
        out_refs = refs[na:2 * na]
        send_sems, recv_sems, local_sems = refs[2 * na:]
        x, y, c = lax.axis_index("x"), lax.axis_index("y"), lax.axis_index("c")
        me, sibling = (x, y, c), (x, y, 1 - c)
        chips = [(1 - x, y), (x, 1 - y), (1 - x, 1 - y)]

        def rows(a, px, py, pc):
            return out_refs[a].at[4 * px + 2 * py + pc]

        def copy(a, k, block, to, src=None):
            return pltpu.make_async_remote_copy(
                src_ref=rows(a, *block) if src is None else src, dst_ref=rows(a, *block),
                send_sem=send_sems.at[a, k], recv_sem=recv_sems.at[a, k],
                device_id=to, device_id_type=MESH)

        mine, first, passed = [], [], []
        for a in range(na):
            cp = pltpu.make_async_copy(x_refs[a], rows(a, *me), local_sems.at[a])
            cp.start()
            mine.append(cp)
            first.append(copy(a, 0, me, sibling, src=x_refs[a]))
            first += [copy(a, 1 + j, me, (*chip, c), src=x_refs[a]) for j, chip in enumerate(chips)]
        for cp in first:
            cp.start()
        for a in range(na):
            for j, chip in enumerate(chips):
                copy(a, 1 + j, (*chip, c), me).wait_recv()
                fwd = copy(a, 4 + j, (*chip, c), sibling)
                fwd.start()
                passed.append(fwd)
        for a in range(na):
            copy(a, 0, sibling, me).wait_recv()
            for j, chip in enumerate(chips):
                copy(a, 4 + j, (*chip, 1 - c), me).wait_recv()
        for cp in first + passed:
            cp.wait_send()
        for cp in mine:
            cp.wait()

    return pl.pallas_call(
        body, name=name,
        out_shape=[jax.ShapeDtypeStruct((N_DEV,) + b.shape, b.dtype) for b in blocks],
        in_specs=[WHOLE_VMEM] * na, out_specs=[WHOLE_VMEM] * na,
        scratch_shapes=[pltpu.SemaphoreType.DMA((na, 7)), pltpu.SemaphoreType.DMA((na, 7)),
                        pltpu.SemaphoreType.DMA((na,))],
        compiler_params=pltpu.CompilerParams(vmem_limit_bytes=VMEM_LIMIT),
    )(*blocks)


def _pair_exchange(arrs, name):
    na = len(arrs)

    def body(*refs):
        in_refs = refs[:na]
        out_refs = refs[na:2 * na]
        send_sems, recv_sems = refs[2 * na:]
        sibling = (lax.axis_index("x"), lax.axis_index("y"), 1 - lax.axis_index("c"))
        copies = [pltpu.make_async_remote_copy(
            src_ref=in_refs[a], dst_ref=out_refs[a], send_sem=send_sems.at[a], recv_sem=recv_sems.at[a],
            device_id=sibling, device_id_type=MESH) for a in range(na)]
        for cp in copies:
            cp.start()
        for cp in copies:
            cp.wait()

    return pl.pallas_call(
        body, name=name,
        out_shape=[jax.ShapeDtypeStruct(a.shape, a.dtype) for a in arrs],
        in_specs=[ANY] * na, out_specs=[ANY] * na,
        scratch_shapes=[pltpu.SemaphoreType.DMA((na,)), pltpu.SemaphoreType.DMA((na,))],
    )(*arrs)


def _chip_exchange(arrs, name):
    na = len(arrs)

    def body(*refs):
        in_refs = refs[:na]
        out_refs = refs[na:2 * na]
        send_sems, recv_sems, local_sems = refs[2 * na:]
        x, y, c = lax.axis_index("x"), lax.axis_index("y"), lax.axis_index("c")
        my_chip = 2 * x + y
        chips = [(1 - x, y), (x, 1 - y), (1 - x, 1 - y)]
        local, remote = [], []
        for a in range(na):
            cp = pltpu.make_async_copy(in_refs[a].at[my_chip], out_refs[a].at[my_chip], local_sems.at[a])
            cp.start()
            local.append(cp)
            for j, (px, py) in enumerate(chips):
                cp = pltpu.make_async_remote_copy(
                    src_ref=in_refs[a].at[2 * px + py], dst_ref=out_refs[a].at[my_chip],
                    send_sem=send_sems.at[a, j], recv_sem=recv_sems.at[a, j],
                    device_id=(px, py, c), device_id_type=MESH)
                cp.start()
                remote.append(cp)
        for cp in remote:
            cp.wait()
        for cp in local:
            cp.wait()

    return pl.pallas_call(
        body, name=name,
        out_shape=[jax.ShapeDtypeStruct(a.shape, a.dtype) for a in arrs],
        in_specs=[ANY] * na, out_specs=[ANY] * na,
        scratch_shapes=[pltpu.SemaphoreType.DMA((na, 3)), pltpu.SemaphoreType.DMA((na, 3)),
                        pltpu.SemaphoreType.DMA((na,))],
    )(*arrs)


def _rows_tile(r):
    for cand in (512, 256, 128, 64, 32, 16, 8):
        if r % cand == 0:
            return cand
    return r


def _add2(a, b, name, out_dtype):
    s, r, n = a.shape
    tr = _rows_tile(r)

    def body(a_ref, b_ref, o_ref):
        o_ref[...] = (a_ref[...] + b_ref[...]).astype(out_dtype)

    spec = pl.BlockSpec((1, tr, n), lambda i, j: (i, j, 0))
    return pl.pallas_call(
        body, name=name, grid=(s, r // tr), in_specs=[spec, spec], out_specs=spec,
        out_shape=jax.ShapeDtypeStruct(a.shape, out_dtype),
        compiler_params=_cparams(("parallel", "parallel")),
    )(a, b)


def _sum_slabs(a, name):
    s, r, n = a.shape
    tr = _rows_tile(r)

    def body(a_ref, o_ref):
        acc = a_ref[0].astype(F32)
        for k in range(1, s):
            acc = acc + a_ref[k].astype(F32)
        o_ref[...] = acc

    return pl.pallas_call(
        body, name=name, grid=(r // tr,),
        in_specs=[pl.BlockSpec((s, tr, n), lambda i: (0, i, 0))],
        out_specs=pl.BlockSpec((tr, n), lambda i: (i, 0)),
        out_shape=jax.ShapeDtypeStruct((r, n), F32),
        compiler_params=_cparams(("parallel",)),
    )(a)


ADAMW_WHOLE = 64 * 1024


def _adamw(w, g, m, v, name):
    r, n = w.shape
    if r * n <= ADAMW_WHOLE:
        blk, grid, imap = (r, n), (1,), (lambda i: (0, 0))
    elif r % 8 == 0:
        blk, grid, imap = (_rows_tile(r), n), (r // _rows_tile(r),), (lambda i: (i, 0))
    else:
        blk, grid, imap = (r, 128), (n // 128,), (lambda i: (0, i))
    c1 = 1.0 / (1.0 - ADAM_B1 ** ADAM_STEP)
    c2 = 1.0 / (1.0 - ADAM_B2 ** ADAM_STEP)

    def body(w_ref, g_ref, m_ref, v_ref, d_ref, nm_ref, nv_ref):
        gv = g_ref[...]
        nm = ADAM_B1 * m_ref[...] + (1.0 - ADAM_B1) * gv
        nv = ADAM_B2 * v_ref[...] + (1.0 - ADAM_B2) * (gv * gv)
        nm_ref[...] = nm
        nv_ref[...] = nv
        d_ref[...] = -ADAM_LR * ((nm * c1) / (jnp.sqrt(nv * c2) + ADAM_EPS) + ADAM_WD * w_ref[...])

    spec = pl.BlockSpec(blk, imap)
    o = jax.ShapeDtypeStruct((r, n), F32)
    return pl.pallas_call(
        body, name=name, grid=grid, in_specs=[spec] * 4, out_specs=[spec] * 3, out_shape=[o, o, o],
        compiler_params=_cparams(("parallel",)),
    )(w, g, m, v)


def _mod_fwd(c_all, w_ada_s, b_ada_s):
    def body(c_ref, w_ref, b_ref, o_ref):
        o_ref[...] = jnp.dot(_silu(c_ref[...]), w_ref[...], precision=HI, preferred_element_type=F32) + b_ref[...]

    return pl.pallas_call(
        body, name="mod_fwd", out_shape=jax.ShapeDtypeStruct((c_all.shape[0], w_ada_s.shape[1]), F32),
        in_specs=[WHOLE_VMEM] * 3, out_specs=WHOLE_VMEM,
        compiler_params=pltpu.CompilerParams(vmem_limit_bytes=VMEM_LIMIT),
    )(c_all, w_ada_s, b_ada_s)


def _wada_grad(c_all, dmod_s):
    def body(c_ref, d_ref, o_ref):
        o_ref[...] = lax.dot_general(_silu(c_ref[...]), d_ref[...], (((0,), (0,)), ((), ())),
                                     precision=HI, preferred_element_type=F32)

    return pl.pallas_call(
        body, name="w_ada_grad", out_shape=jax.ShapeDtypeStruct((c_all.shape[1], dmod_s.shape[1]), F32),
        in_specs=[WHOLE_VMEM] * 2, out_specs=WHOLE_VMEM,
        compiler_params=pltpu.CompilerParams(vmem_limit_bytes=VMEM_LIMIT),
    )(c_all, dmod_s)


SMALL_ROWS = 24


def _pad_rows(v, nrows):
    v = v.reshape(-1)
    return jnp.pad(v, (0, nrows * 1024 - v.shape[0])).reshape(nrows, 1024)


def _pack_small(b_ada, norm_w, b_in, conv_w_full, conv_b, rpb, ml_norm_w, final_norm_w, last):
    parts = [_pad_rows(b_ada, 3), _pad_rows(norm_w, 1), _pad_rows(b_in, 5), _pad_rows(conv_w_full, 5),
             _pad_rows(conv_b, 1), _pad_rows(rpb, 4), _pad_rows(ml_norm_w, 1), _pad_rows(final_norm_w, 1),
             _pad_rows(last, 3)]
    return jnp.concatenate(parts, axis=0)


def _unpack_small(p):
    return dict(b_ada=p[0:3].reshape(1, 3072), norm_w=p[3:4], b_in=p[4:9].reshape(-1)[:IN_W].reshape(1, IN_W),
                conv_w=p[9:14], conv_b=p[14:15],
                rpb=p[15:19].reshape(-1)[:NA_HEADS * 15 * 31].reshape(1, NA_HEADS, 15, 31),
                ml_norm_w=p[19:20, :ML_W], final_norm_w=p[20], last=p[21])


def kernel(x, c, w_ada, b_ada, norm_w, w_in, b_in, conv_w, conv_b, rpb, ml_norm_w, w_out, final_norm_w, loss_target, m_w_ada, m_b_ada, m_norm_w, m_w_in, m_b_in, m_conv_w, m_conv_b, m_rpb, m_ml_norm_w, m_w_out, m_final_norm_w, v_w_ada, v_b_ada, v_norm_w, v_w_in, v_b_in, v_conv_w, v_conv_b, v_rpb, v_ml_norm_w, v_w_out, v_final_norm_w):
    xi, yi, ci = lax.axis_index("x"), lax.axis_index("y"), lax.axis_index("c")
    chip = 2 * xi + yi
    dev = 2 * chip + ci
    t = x.shape[1]
    ada_n = w_ada.shape[2]
    in_n = w_in.shape[2]
    out_r = w_out.shape[1]

    c_blk = jnp.pad(c, ((0, 7), (0, 0)))
    w_in_t, m_w_in_t, v_w_in_t = w_in[0].T, m_w_in[0].T, v_w_in[0].T
    in_h = in_n // 2
    w_in_half = lax.dynamic_slice_in_dim(w_in_t, ci * in_h, in_h, axis=0).astype(BF16)
    w_out_half = lax.dynamic_slice_in_dim(w_out[0], ci * (out_r // 2), out_r // 2, axis=0).astype(BF16)
    conv_blk = jnp.pad(conv_w[0], ((0, 3), (0, 0)))
    c_g, conv_g, w_in_g, w_out_g = _allgather8([c_blk, conv_blk, w_in_half, w_out_half], "gather_c_weights")
    c_all = c_g[:, 0]
    w_out_g = w_out_g.reshape(D_MODEL, D_MODEL)
    b_ada_s = lax.dynamic_slice_in_dim(b_ada, chip * ada_n, ada_n, axis=1)
    mod_s = _mod_fwd(c_all, w_ada[0], b_ada_s)
    (mod_g,) = _allgather8([mod_s], "gather_mod")
    mod_mine = lax.dynamic_index_in_dim(mod_g, dev, axis=1, keepdims=False)
    mod = mod_mine[0::2].reshape(1, 3 * D_MODEL)
    shift, scale, gate = mod[:, :D_MODEL], mod[:, D_MODEL:2 * D_MODEL], mod[:, 2 * D_MODEL:]

    w_in_tp = jnp.pad(w_in_g.reshape(IN_W, D_MODEL), ((0, IN_PAD - IN_W), (0, 0)))
    b_in_p = jnp.pad(b_in, ((0, 0), (0, IN_PAD - IN_W)))
    conv_w8 = conv_g.reshape(4, 2, 8, conv_w.shape[2])[:, 0].transpose(1, 0, 2).reshape(8, D_MODEL)

    (loss, grad_x, dmod, g_nw, g_w_in, g_b_in, g_conv_w, g_conv_b, g_rpb, g_mlnw, g_w_out, g_fnw) = _local_step(
        x[0], loss_target[0], shift, scale, gate, norm_w, w_in_tp, b_in_p, conv_w8, conv_b, rpb[0],
        ml_norm_w, w_out_g, final_norm_w.reshape(1, D_MODEL))

    g_in_t = g_w_in

    def halves(a, per_chip, h):
        return jnp.stack([lax.dynamic_slice_in_dim(a, k * per_chip + h * (per_chip // 2), per_chip // 2, axis=0)
                          for k in range(4)])

    ri, ro = _pair_exchange([halves(g_in_t, in_n, 1 - ci), halves(g_w_out, out_r, 1 - ci)], "rs_pair")
    pi = _add2(halves(g_in_t, in_n, ci), ri, "rs_pair_add_in", BF16)
    po = _add2(halves(g_w_out, out_r, ci), ro, "rs_pair_add_out", BF16)
    qi, qo = _chip_exchange([pi, po], "rs_chips")
    si = _sum_slabs(qi, "rs_sum_in")
    so = _sum_slabs(qo, "rs_sum_out")
    ti, to = _pair_exchange([si, so], "rs_share")
    g_w_in_s = jnp.where(ci == 0, jnp.concatenate([si, ti], axis=0), jnp.concatenate([ti, si], axis=0))
    g_w_out_s = jnp.where(ci == 0, jnp.concatenate([so, to], axis=0), jnp.concatenate([to, so], axis=0))

    small = _pack_small(dmod, g_nw, g_b_in[:, :IN_W], g_conv_w[:CONV_W], g_conv_b, g_rpb, g_mlnw, g_fnw,
                        jnp.pad(loss, ((0, 0), (0, 1024 - 128))))
    (small_g,) = _allgather8([small], "gather_small")
    small_sum = _sum_slabs(small_g, "small_sum")
    gs = _unpack_small(small_sum)
    dmod_all = small_g[:, 0:3].reshape(N_DEV, 3 * D_MODEL)
    g_w_ada_s = _wada_grad(c_all, lax.dynamic_slice_in_dim(dmod_all, chip * ada_n, ada_n, axis=1))
    g_conv_w_s = lax.dynamic_slice_in_dim(gs['conv_w'], chip * conv_w.shape[2], conv_w.shape[2], axis=1)
    loss_total = gs['last'][0]

    small_names = ('b_ada', 'norm_w', 'b_in', 'conv_b', 'rpb', 'ml_norm_w', 'final_norm_w')
    small_w = (b_ada, norm_w, b_in, conv_b, rpb, ml_norm_w, final_norm_w)
    small_m = (m_b_ada, m_norm_w, m_b_in, m_conv_b, m_rpb, m_ml_norm_w, m_final_norm_w)
    small_v = (v_b_ada, v_norm_w, v_b_in, v_conv_b, v_rpb, v_ml_norm_w, v_final_norm_w)
    ds_, nms, nvs = {}, {}, {}
    for nm_, w_, m_, v_ in zip(small_names, small_w, small_m, small_v):
        two_d = (NA_HEADS, w_.size // NA_HEADS) if nm_ == 'rpb' else (1, w_.size)
        outs = _adamw(w_.reshape(two_d), gs[nm_].reshape(two_d), m_.reshape(two_d), v_.reshape(two_d),
                      "adamw_" + nm_)
        ds_[nm_], nms[nm_], nvs[nm_] = [o.reshape(w_.shape) for o in outs]
    d_ada, nm_ada, nv_ada = _adamw(w_ada[0], g_w_ada_s, m_w_ada[0], v_w_ada[0], "adamw_w_ada")
    d_in, nm_in, nv_in = _adamw(w_in_t, g_w_in_s, m_w_in_t, v_w_in_t, "adamw_w_in")
    d_out, nm_out, nv_out = _adamw(w_out[0], g_w_out_s, m_w_out[0], v_w_out[0], "adamw_w_out")
    d_cw, nm_cw, nv_cw = _adamw(conv_w[0], g_conv_w_s, m_conv_w[0], v_conv_w[0], "adamw_conv_w")

    def group(big_ada, big_in, big_out, cw, sm):
        return (big_ada[None], sm['b_ada'], sm['norm_w'], big_in.T[None], sm['b_in'], cw[None], sm['conv_b'],
                sm['rpb'], sm['ml_norm_w'], big_out[None], sm['final_norm_w'])

    return ((loss_total, grad_x[None])
            + group(g_w_ada_s, g_w_in_s, g_w_out_s, g_conv_w_s, gs)
            + group(d_ada, d_in, d_out, d_cw, ds_)
            + group(nm_ada, nm_in, nm_out, nm_cw, nms)
            + group(nv_ada, nv_in, nv_out, nv_cw, nvs))
```
